```python
import jax
import jax.numpy as jnp
from jax import lax
import numpy as np


D_MODEL = 1024
BATCH = 8
SEQ = 16384
DEPTH = 2

CHUNK = 64
D_MIX = D_MODEL
EPS = 1e-6
NEG_INF = -1e30

SGU_WIDTH = D_MIX // 4
SGU_HEADS = 4
SGU_HEAD_DIM = SGU_WIDTH // SGU_HEADS
SGU_BLOCK = 128

POOL_WIDTH = D_MIX // 4
POOL_WINDOWS = (2, 4, 8, 16)
POOL_GROUPS = len(POOL_WINDOWS)
POOL_GROUP_DIM = POOL_WIDTH // POOL_GROUPS

MLA_WIDTH = D_MIX // 2
MLA_HEADS = 4
V_HEAD_DIM = MLA_WIDTH // MLA_HEADS
QK_NOPE_DIM = 128
QK_ROPE_DIM = 64
QK_HEAD_DIM = QK_NOPE_DIM + QK_ROPE_DIM
Q_LORA_RANK = 384
KV_LORA_RANK = 256
ROPE_BASE = 10000.0
Q_BLOCK = 128

IN_SPLITS = (SGU_WIDTH, SGU_WIDTH, SGU_WIDTH, POOL_WIDTH, POOL_WIDTH, Q_LORA_RANK, KV_LORA_RANK, QK_ROPE_DIM, MLA_WIDTH)
D_IN = sum(IN_SPLITS)

kernel_name = "hybrid_sgu_pool_mla_block"


def rms_norm(x, g):
    xf = x.astype(jnp.float32)
    y = xf * lax.rsqrt(jnp.mean(xf * xf, axis=-1, keepdims=True) + EPS)
    return (y * g.astype(jnp.float32)).astype(x.dtype)


def layer_norm(x, g, b):
    xf = x.astype(jnp.float32)
    mu = jnp.mean(xf, axis=-1, keepdims=True)
    var = jnp.mean(jnp.square(xf - mu), axis=-1, keepdims=True)
    y = (xf - mu) * lax.rsqrt(var + EPS)
    return (y * g.astype(jnp.float32) + b.astype(jnp.float32)).astype(x.dtype)


def rope_tables(positions):
    inv_freq = ROPE_BASE ** (-jnp.arange(0, QK_ROPE_DIM, 2, dtype=jnp.float32) / QK_ROPE_DIM)
    ang = positions.astype(jnp.float32)[..., None] * inv_freq
    return jnp.cos(ang)[:, :, None, :], jnp.sin(ang)[:, :, None, :]


def apply_rope(x, cos, sin):
    xf = x.astype(jnp.float32)
    x1, x2 = jnp.split(xf, 2, axis=-1)
    return jnp.concatenate([x1 * cos - x2 * sin, x2 * cos + x1 * sin], axis=-1).astype(x.dtype)


def sgu_mixer(u, v, w_s, b_s, ln_g, ln_b):
    bsz, seq, _ = v.shape
    v = layer_norm(v, ln_g, ln_b)
    vb = v.reshape(bsz, seq // SGU_BLOCK, SGU_BLOCK, SGU_HEADS, SGU_HEAD_DIM)
    pos_chunk = jnp.arange(SGU_BLOCK) // CHUNK
    mask = (pos_chunk[None, :] <= pos_chunk[:, None]).astype(w_s.dtype)
    mixed = jnp.einsum("hij,bnjhd->bnihd", w_s * mask, vb) + b_s.T[None, None, :, :, None]
    return u * mixed.reshape(bsz, seq, SGU_WIDTH)


def pool_mixer(p, w_g, scale):
    bsz, seq, _ = p.shape
    pf = p.astype(jnp.float32)
    cs = jnp.concatenate([jnp.zeros((bsz, 1, POOL_WIDTH), jnp.float32), jnp.cumsum(pf, axis=1)], axis=1)
    t = jnp.arange(seq)
    outs = []
    for g, w in enumerate(POOL_WINDOWS):
        lo, hi = g * POOL_GROUP_DIM, (g + 1) * POOL_GROUP_DIM
        csg = cs[:, :, lo:hi]
        upper = csg[:, 1:]
        lower = jnp.concatenate([jnp.zeros((bsz, w - 1, POOL_GROUP_DIM), jnp.float32), csg[:, :seq + 1 - w]], axis=1)
        count = jnp.minimum(t + 1, w).astype(jnp.float32)[None, :, None]
        outs.append((upper - lower) / count - pf[:, :, lo:hi])
    pooled = jnp.stack(outs, axis=2).astype(p.dtype)
    mixed = jnp.einsum("bsgc,gcd->bsgd", pooled, w_g).reshape(bsz, seq, POOL_WIDTH)
    return mixed * scale


def mla_mixer(c_q, c_kv, k_rope, q_norm_g, w_uq, kv_norm_g, w_ukv, cos, sin):
    bsz, seq, _ = c_q.shape
    q = (rms_norm(c_q, q_norm_g) @ w_uq).reshape(bsz, seq, MLA_HEADS, QK_HEAD_DIM)
    q = jnp.concatenate([q[..., :QK_NOPE_DIM], apply_rope(q[..., QK_NOPE_DIM:], cos, sin)], axis=-1)
    kv = (rms_norm(c_kv, kv_norm_g) @ w_ukv).reshape(bsz, seq, MLA_HEADS, QK_NOPE_DIM + V_HEAD_DIM)
    k_nope, v = kv[..., :QK_NOPE_DIM], kv[..., QK_NOPE_DIM:]
    k_pe = apply_rope(k_rope[:, :, None, :], cos, sin)
    k = jnp.concatenate([k_nope, jnp.broadcast_to(k_pe, (bsz, seq, MLA_HEADS, QK_ROPE_DIM))], axis=-1)
    scale = QK_HEAD_DIM ** -0.5
    n_blk = seq // Q_BLOCK
    q_blocks = q.reshape(bsz, n_blk, Q_BLOCK, MLA_HEADS, QK_HEAD_DIM).transpose(1, 0, 2, 3, 4)
    key_chunk = jnp.arange(seq) // CHUNK

    def attend(args):
        q_blk, blk = args
        s = jnp.einsum("bqhd,bkhd->bhqk", q_blk, k).astype(jnp.float32) * scale
        q_chunk = (blk * Q_BLOCK + jnp.arange(Q_BLOCK)) // CHUNK
        allowed = key_chunk[None, :] <= q_chunk[:, None]
        s = jnp.where(allowed, s, NEG_INF)
        p = jax.nn.softmax(s, axis=-1).astype(v.dtype)
        return jnp.einsum("bhqk,bkhd->bqhd", p, v)

    o = lax.map(attend, (q_blocks, jnp.arange(n_blk)))
    return o.transpose(1, 0, 2, 3, 4).reshape(bsz, seq, MLA_WIDTH)


def hybrid_layer(x, pre_g, post_g, w_in, sgu_w, sgu_b, sgu_ln_g, sgu_ln_b, pool_w, pool_scale, q_norm_g, w_uq, kv_norm_g, w_ukv, w_out, cos, sin):
    h = rms_norm(x, pre_g)
    z = h @ w_in
    offs = [int(o) for o in np.cumsum(IN_SPLITS)[:-1]]
    sgu_u, sgu_v, sgu_gate, pool_in, pool_gate, c_q, c_kv, k_rope, mla_gate = jnp.split(z, offs, axis=-1)
    ya = sgu_mixer(sgu_u, sgu_v, sgu_w, sgu_b, sgu_ln_g, sgu_ln_b) * jax.nn.silu(sgu_gate)
    yb = pool_mixer(pool_in, pool_w, pool_scale) * jax.nn.silu(pool_gate)
    yc = mla_mixer(c_q, c_kv, k_rope, q_norm_g, w_uq, kv_norm_g, w_ukv, cos, sin) * jax.nn.silu(mla_gate)
    y = jnp.concatenate([ya, yb, yc], axis=-1) @ w_out
    return x + rms_norm(y, post_g)


def _fwd_setup_inputs(seed: int = 0) -> dict:
    key = jax.random.key(seed)
    ks = jax.random.split(key, 16)

    def nrm(k, shape, s):
        return jax.random.normal(k, shape, jnp.float32) * s

    x = jax.random.normal(ks[0], (BATCH, SEQ, D_MODEL), jnp.float32)
    positions = jnp.broadcast_to(jnp.arange(SEQ, dtype=jnp.int32)[None, :], (BATCH, SEQ))
    pre_norm_g = 1.0 + nrm(ks[1], (DEPTH, D_MODEL), 0.02)
    post_norm_g = 1.0 + nrm(ks[2], (DEPTH, D_MODEL), 0.02)
    w_in = nrm(ks[3], (DEPTH, D_MODEL, D_IN), D_MODEL ** -0.5)
    sgu_w = nrm(ks[4], (DEPTH, SGU_HEADS, SGU_BLOCK, SGU_BLOCK), SGU_BLOCK ** -0.5)
    sgu_b = 1.0 + nrm(ks[5], (DEPTH, SGU_HEADS, SGU_BLOCK), 0.02)
    sgu_ln_g = 1.0 + nrm(ks[6], (DEPTH, SGU_WIDTH), 0.02)
    sgu_ln_b = nrm(ks[7], (DEPTH, SGU_WIDTH), 0.02)
    pool_w = nrm(ks[8], (DEPTH, POOL_GROUPS, POOL_GROUP_DIM, POOL_GROUP_DIM), POOL_GROUP_DIM ** -0.5)
    pool_scale = 1.0 + nrm(ks[9], (DEPTH, POOL_WIDTH), 0.02)
    q_norm_g = 1.0 + nrm(ks[10], (DEPTH, Q_LORA_RANK), 0.02)
    w_uq = nrm(ks[11], (DEPTH, Q_LORA_RANK, MLA_HEADS * QK_HEAD_DIM), Q_LORA_RANK ** -0.5)
    kv_norm_g = 1.0 + nrm(ks[12], (DEPTH, KV_LORA_RANK), 0.02)
    w_ukv = nrm(ks[13], (DEPTH, KV_LORA_RANK, MLA_HEADS * (QK_NOPE_DIM + V_HEAD_DIM)), KV_LORA_RANK ** -0.5)
    w_out = nrm(ks[14], (DEPTH, D_MIX, D_MODEL), D_MIX ** -0.5)
    return {"x": x, "positions": positions, "pre_norm_g": pre_norm_g, "post_norm_g": post_norm_g, "w_in": w_in, "sgu_w": sgu_w, "sgu_b": sgu_b, "sgu_ln_g": sgu_ln_g, "sgu_ln_b": sgu_ln_b, "pool_w": pool_w, "pool_scale": pool_scale, "q_norm_g": q_norm_g, "w_uq": w_uq, "kv_norm_g": kv_norm_g, "w_ukv": w_ukv, "w_out": w_out}


def _fwd_reference(x, positions, pre_norm_g, post_norm_g, w_in, sgu_w, sgu_b, sgu_ln_g, sgu_ln_b, pool_w, pool_scale, q_norm_g, w_uq, kv_norm_g, w_ukv, w_out):
    cos, sin = rope_tables(positions)
    for l in range(DEPTH):
        x = hybrid_layer(x, pre_norm_g[l], post_norm_g[l], w_in[l], sgu_w[l], sgu_b[l], sgu_ln_g[l], sgu_ln_b[l], pool_w[l], pool_scale[l], q_norm_g[l], w_uq[l], kv_norm_g[l], w_ukv[l], w_out[l], cos, sin)
    return x


import jax as _jax
import jax.numpy as _jnp

TWIN_FORMAT = 'train_step'
FWD_PARAMS = ['x', 'positions', 'pre_norm_g', 'post_norm_g', 'w_in', 'sgu_w', 'sgu_b', 'sgu_ln_g', 'sgu_ln_b', 'pool_w', 'pool_scale', 'q_norm_g', 'w_uq', 'kv_norm_g', 'w_ukv', 'w_out']
TWIN_WEIGHTS = ['pre_norm_g', 'post_norm_g', 'w_in', 'sgu_w', 'sgu_b', 'sgu_ln_g', 'sgu_ln_b', 'pool_w', 'pool_scale', 'q_norm_g', 'w_uq', 'kv_norm_g', 'w_ukv', 'w_out']
TWIN_DIFF_INPUT = 'x'
TWIN_INPUTS = ['x', 'positions', 'pre_norm_g', 'post_norm_g', 'w_in', 'sgu_w', 'sgu_b', 'sgu_ln_g', 'sgu_ln_b', 'pool_w', 'pool_scale', 'q_norm_g', 'w_uq', 'kv_norm_g', 'w_ukv', 'w_out', 'loss_target', 'm_pre_norm_g', 'm_post_norm_g', 'm_w_in', 'm_sgu_w', 'm_sgu_b', 'm_sgu_ln_g', 'm_sgu_ln_b', 'm_pool_w', 'm_pool_scale', 'm_q_norm_g', 'm_w_uq', 'm_kv_norm_g', 'm_w_ukv', 'm_w_out', 'v_pre_norm_g', 'v_post_norm_g', 'v_w_in', 'v_sgu_w', 'v_sgu_b', 'v_sgu_ln_g', 'v_sgu_ln_b', 'v_pool_w', 'v_pool_scale', 'v_q_norm_g', 'v_w_uq', 'v_kv_norm_g', 'v_w_ukv', 'v_w_out']
TWIN_OUTPUTS = ['loss', 'grad_x', 'grad_pre_norm_g', 'grad_post_norm_g', 'grad_w_in', 'grad_sgu_w', 'grad_sgu_b', 'grad_sgu_ln_g', 'grad_sgu_ln_b', 'grad_pool_w', 'grad_pool_scale', 'grad_q_norm_g', 'grad_w_uq', 'grad_kv_norm_g', 'grad_w_ukv', 'grad_w_out', 'delta_pre_norm_g', 'delta_post_norm_g', 'delta_w_in', 'delta_sgu_w', 'delta_sgu_b', 'delta_sgu_ln_g', 'delta_sgu_ln_b', 'delta_pool_w', 'delta_pool_scale', 'delta_q_norm_g', 'delta_w_uq', 'delta_kv_norm_g', 'delta_w_ukv', 'delta_w_out', 'new_m_pre_norm_g', 'new_m_post_norm_g', 'new_m_w_in', 'new_m_sgu_w', 'new_m_sgu_b', 'new_m_sgu_ln_g', 'new_m_sgu_ln_b', 'new_m_pool_w', 'new_m_pool_scale', 'new_m_q_norm_g', 'new_m_w_uq', 'new_m_kv_norm_g', 'new_m_w_ukv', 'new_m_w_out', 'new_v_pre_norm_g', 'new_v_post_norm_g', 'new_v_w_in', 'new_v_sgu_w', 'new_v_sgu_b', 'new_v_sgu_ln_g', 'new_v_sgu_ln_b', 'new_v_pool_w', 'new_v_pool_scale', 'new_v_q_norm_g', 'new_v_w_uq', 'new_v_kv_norm_g', 'new_v_w_ukv', 'new_v_w_out']
TWIN_LEAF_KINDS = {'loss': 'loss', 'grad_x': 'grad_x', 'grad_pre_norm_g': 'grad_w', 'grad_post_norm_g': 'grad_w', 'grad_w_in': 'grad_w', 'grad_sgu_w': 'grad_w', 'grad_sgu_b': 'grad_w', 'grad_sgu_ln_g': 'grad_w', 'grad_sgu_ln_b': 'grad_w', 'grad_pool_w': 'grad_w', 'grad_pool_scale': 'grad_w', 'grad_q_norm_g': 'grad_w', 'grad_w_uq': 'grad_w', 'grad_kv_norm_g': 'grad_w', 'grad_w_ukv': 'grad_w', 'grad_w_out': 'grad_w', 'delta_pre_norm_g': 'delta_w', 'delta_post_norm_g': 'delta_w', 'delta_w_in': 'delta_w', 'delta_sgu_w': 'delta_w', 'delta_sgu_b': 'delta_w', 'delta_sgu_ln_g': 'delta_w', 'delta_sgu_ln_b': 'delta_w', 'delta_pool_w': 'delta_w', 'delta_pool_scale': 'delta_w', 'delta_q_norm_g': 'delta_w', 'delta_w_uq': 'delta_w', 'delta_kv_norm_g': 'delta_w', 'delta_w_ukv': 'delta_w', 'delta_w_out': 'delta_w', 'new_m_pre_norm_g': 'new_m', 'new_m_post_norm_g': 'new_m', 'new_m_w_in': 'new_m', 'new_m_sgu_w': 'new_m', 'new_m_sgu_b': 'new_m', 'new_m_sgu_ln_g': 'new_m', 'new_m_sgu_ln_b': 'new_m', 'new_m_pool_w': 'new_m', 'new_m_pool_scale': 'new_m', 'new_m_q_norm_g': 'new_m', 'new_m_w_uq': 'new_m', 'new_m_kv_norm_g': 'new_m', 'new_m_w_ukv': 'new_m', 'new_m_w_out': 'new_m', 'new_v_pre_norm_g': 'new_v', 'new_v_post_norm_g': 'new_v', 'new_v_w_in': 'new_v', 'new_v_sgu_w': 'new_v', 'new_v_sgu_b': 'new_v', 'new_v_sgu_ln_g': 'new_v', 'new_v_sgu_ln_b': 'new_v', 'new_v_pool_w': 'new_v', 'new_v_pool_scale': 'new_v', 'new_v_q_norm_g': 'new_v', 'new_v_w_uq': 'new_v', 'new_v_kv_norm_g': 'new_v', 'new_v_w_ukv': 'new_v', 'new_v_w_out': 'new_v'}


def _forward(args):
    return _fwd_reference(*[args[k] for k in FWD_PARAMS])


def _output_shape():
    def fwd():
        inp = _fwd_setup_inputs(0)
        return _fwd_reference(*[inp[k] for k in FWD_PARAMS])
    out = _jax.eval_shape(fwd)
    return out.shape, out.dtype

N_MICROBATCH = 1
ADAM_LR = 0.001
ADAM_B1 = 0.9
ADAM_B2 = 0.999
ADAM_EPS = 1e-08
ADAM_WD = 0.01
ADAM_STEP = 10
PER_EXAMPLE_BATCH_AXIS = {'x': 0, 'positions': 0, 'loss_target': 0}
SHARED_INPUTS = []
_WEIGHT_DTYPES = {'pre_norm_g': _jnp.float32, 'post_norm_g': _jnp.float32, 'w_in': _jnp.float32, 'sgu_w': _jnp.float32, 'sgu_b': _jnp.float32, 'sgu_ln_g': _jnp.float32, 'sgu_ln_b': _jnp.float32, 'pool_w': _jnp.float32, 'pool_scale': _jnp.float32, 'q_norm_g': _jnp.float32, 'w_uq': _jnp.float32, 'kv_norm_g': _jnp.float32, 'w_ukv': _jnp.float32, 'w_out': _jnp.float32}
MOMENT_SCALE = {'pre_norm_g': 1.288253e+00, 'post_norm_g': 1.279831e+02, 'w_in': 8.861526e-01, 'sgu_w': 5.901093e-01, 'sgu_b': 6.746786e-01, 'sgu_ln_g': 8.793829e-01, 'sgu_ln_b': 9.928828e-01, 'pool_w': 1.287142e+00, 'pool_scale': 1.496825e+00, 'q_norm_g': 1.790010e-01, 'w_uq': 1.270359e-01, 'kv_norm_g': 3.258554e-01, 'w_ukv': 1.483481e-01, 'w_out': 1.202502e+00}


def _to_microbatches(a, axis):
    t = _jnp.moveaxis(a, axis, 0)
    t = t.reshape((N_MICROBATCH, t.shape[0] // N_MICROBATCH) + t.shape[1:])
    return _jnp.moveaxis(t, 1, axis + 1)


def setup_inputs(seed: int = 0) -> dict:
    inp = _fwd_setup_inputs(seed)
    key = _jax.random.fold_in(_jax.random.key(seed), 7919)
    shape, _ = _output_shape()
    out = dict(inp)
    out["loss_target"] = _jax.random.normal(_jax.random.fold_in(key, 0), shape, _jnp.float32)
    for i, name in enumerate(TWIN_WEIGHTS):
        w = inp[name].astype(_jnp.float32)
        if MOMENT_SCALE is None:
            s = _jnp.sqrt(_jnp.mean(_jnp.square(w)) + 1e-30)
        else:
            s = MOMENT_SCALE[name]
        km, kv = _jax.random.split(_jax.random.fold_in(key, i + 1))
        out[name] = w
        out["m_" + name] = s * _jax.random.normal(km, w.shape, _jnp.float32)
        out["v_" + name] = (s * s) * _jax.random.uniform(kv, w.shape, _jnp.float32, 0.5, 1.5)
    if N_MICROBATCH > 1:
        for name, axis in PER_EXAMPLE_BATCH_AXIS.items():
            out[name] = _to_microbatches(out[name], axis)
    return {'x': out['x'], 'positions': out['positions'], 'pre_norm_g': out['pre_norm_g'], 'post_norm_g': out['post_norm_g'], 'w_in': out['w_in'], 'sgu_w': out['sgu_w'], 'sgu_b': out['sgu_b'], 'sgu_ln_g': out['sgu_ln_g'], 'sgu_ln_b': out['sgu_ln_b'], 'pool_w': out['pool_w'], 'pool_scale': out['pool_scale'], 'q_norm_g': out['q_norm_g'], 'w_uq': out['w_uq'], 'kv_norm_g': out['kv_norm_g'], 'w_ukv': out['w_ukv'], 'w_out': out['w_out'], 'loss_target': out['loss_target'], 'm_pre_norm_g': out['m_pre_norm_g'], 'm_post_norm_g': out['m_post_norm_g'], 'm_w_in': out['m_w_in'], 'm_sgu_w': out['m_sgu_w'], 'm_sgu_b': out['m_sgu_b'], 'm_sgu_ln_g': out['m_sgu_ln_g'], 'm_sgu_ln_b': out['m_sgu_ln_b'], 'm_pool_w': out['m_pool_w'], 'm_pool_scale': out['m_pool_scale'], 'm_q_norm_g': out['m_q_norm_g'], 'm_w_uq': out['m_w_uq'], 'm_kv_norm_g': out['m_kv_norm_g'], 'm_w_ukv': out['m_w_ukv'], 'm_w_out': out['m_w_out'], 'v_pre_norm_g': out['v_pre_norm_g'], 'v_post_norm_g': out['v_post_norm_g'], 'v_w_in': out['v_w_in'], 'v_sgu_w': out['v_sgu_w'], 'v_sgu_b': out['v_sgu_b'], 'v_sgu_ln_g': out['v_sgu_ln_g'], 'v_sgu_ln_b': out['v_sgu_ln_b'], 'v_pool_w': out['v_pool_w'], 'v_pool_scale': out['v_pool_scale'], 'v_q_norm_g': out['v_q_norm_g'], 'v_w_uq': out['v_w_uq'], 'v_kv_norm_g': out['v_kv_norm_g'], 'v_w_ukv': out['v_w_ukv'], 'v_w_out': out['v_w_out']}


def _loss(weights, diff, rest, loss_target):
    with _jax.named_scope("forward"):
        args = {**rest, TWIN_DIFF_INPUT: diff, **{k: w.astype(_WEIGHT_DTYPES[k]) for k, w in weights.items()}}
        y = _forward(args)
    with _jax.named_scope("loss_head"):
        err = _jnp.square(y.astype(_jnp.float32) - loss_target)
        return 0.5 * _jnp.sum(_jnp.mean(err, axis=-1)) if err.ndim else 0.5 * err


def _adamw(w, g, m, v):
    m = ADAM_B1 * m + (1.0 - ADAM_B1) * g
    v = ADAM_B2 * v + (1.0 - ADAM_B2) * _jnp.square(g)
    m_hat = m / (1.0 - ADAM_B1 ** ADAM_STEP)
    v_hat = v / (1.0 - ADAM_B2 ** ADAM_STEP)
    delta = -ADAM_LR * (m_hat / (_jnp.sqrt(v_hat) + ADAM_EPS) + ADAM_WD * w)
    return delta, m, v


def reference(x, positions, pre_norm_g, post_norm_g, w_in, sgu_w, sgu_b, sgu_ln_g, sgu_ln_b, pool_w, pool_scale, q_norm_g, w_uq, kv_norm_g, w_ukv, w_out, loss_target, m_pre_norm_g, m_post_norm_g, m_w_in, m_sgu_w, m_sgu_b, m_sgu_ln_g, m_sgu_ln_b, m_pool_w, m_pool_scale, m_q_norm_g, m_w_uq, m_kv_norm_g, m_w_ukv, m_w_out, v_pre_norm_g, v_post_norm_g, v_w_in, v_sgu_w, v_sgu_b, v_sgu_ln_g, v_sgu_ln_b, v_pool_w, v_pool_scale, v_q_norm_g, v_w_uq, v_kv_norm_g, v_w_ukv, v_w_out):
    given = dict(x=x, positions=positions, pre_norm_g=pre_norm_g, post_norm_g=post_norm_g, w_in=w_in, sgu_w=sgu_w, sgu_b=sgu_b, sgu_ln_g=sgu_ln_g, sgu_ln_b=sgu_ln_b, pool_w=pool_w, pool_scale=pool_scale, q_norm_g=q_norm_g, w_uq=w_uq, kv_norm_g=kv_norm_g, w_ukv=w_ukv, w_out=w_out, loss_target=loss_target, m_pre_norm_g=m_pre_norm_g, m_post_norm_g=m_post_norm_g, m_w_in=m_w_in, m_sgu_w=m_sgu_w, m_sgu_b=m_sgu_b, m_sgu_ln_g=m_sgu_ln_g, m_sgu_ln_b=m_sgu_ln_b, m_pool_w=m_pool_w, m_pool_scale=m_pool_scale, m_q_norm_g=m_q_norm_g, m_w_uq=m_w_uq, m_kv_norm_g=m_kv_norm_g, m_w_ukv=m_w_ukv, m_w_out=m_w_out, v_pre_norm_g=v_pre_norm_g, v_post_norm_g=v_post_norm_g, v_w_in=v_w_in, v_sgu_w=v_sgu_w, v_sgu_b=v_sgu_b, v_sgu_ln_g=v_sgu_ln_g, v_sgu_ln_b=v_sgu_ln_b, v_pool_w=v_pool_w, v_pool_scale=v_pool_scale, v_q_norm_g=v_q_norm_g, v_w_uq=v_w_uq, v_kv_norm_g=v_kv_norm_g, v_w_ukv=v_w_ukv, v_w_out=v_w_out)
    weights = {n: given[n] for n in TWIN_WEIGHTS}
    shared = {n: given[n] for n in SHARED_INPUTS}
    per_example = {n: given[n] for n in ['x', 'positions']}
    grad_fn = _jax.value_and_grad(_loss, argnums=(0, 1))

    def one_microbatch(ex, loss_target):
        ex = dict(ex)
        diff = ex.pop(TWIN_DIFF_INPUT)
        return grad_fn(weights, diff, {**shared, **ex}, loss_target)

    if N_MICROBATCH == 1:
        loss, (grad_w, grad_x) = one_microbatch(per_example, given["loss_target"])
    else:
        def body(carry, xs):
            loss_sum, grad_sum = carry
            l_k, (gw_k, gx_k) = one_microbatch(xs[0], xs[1])
            with _jax.named_scope("update"):
                return (loss_sum + l_k, _jax.tree.map(_jnp.add, grad_sum, gw_k)), gx_k

        init = (_jnp.zeros((), _jnp.float32), _jax.tree.map(_jnp.zeros_like, weights))
        (loss, grad_w), grad_x = _jax.lax.scan(body, init, (per_example, given["loss_target"]))
    with _jax.named_scope("update"):
        delta_w, new_m, new_v = {}, {}, {}
        for n in TWIN_WEIGHTS:
            delta_w[n], new_m[n], new_v[n] = _adamw(weights[n], grad_w[n], given["m_" + n], given["v_" + n])
    return (loss, grad_x, *[grad_w[n] for n in TWIN_WEIGHTS], *[delta_w[n] for n in TWIN_WEIGHTS],
            *[new_m[n] for n in TWIN_WEIGHTS], *[new_v[n] for n in TWIN_WEIGHTS])
```

```python
import jax
import jax.numpy as jnp
from jax import lax
from jax.experimental import pallas as pl
from jax.experimental.pallas import tpu as pltpu

F32 = jnp.float32
MXU_DTYPE = jnp.bfloat16
EPS = 1e-6
NEG_INF = -1e30
DEPTH = 2
N_HEADS = 4
QK_PAD = 256
V_DIM = 128
SCALE = 192 ** -0.5
ROPE_BASE = 10000.0
ADAM_LR, ADAM_B1, ADAM_B2, ADAM_EPS, ADAM_WD, ADAM_STEP = 0.001, 0.9, 0.999, 1e-08, 0.01, 10
VMEM_LIMIT_BYTES = 56 * 1024 * 1024
MESH = pl.DeviceIdType.MESH
ANY = pl.BlockSpec(memory_space=pl.ANY)

Z_MIX, Z_C, Z_KR, Z_GATE = 1280, 640, 128, 512
Z_W = Z_MIX + Z_C + Z_KR + Z_GATE


def _cp(n_axes=1):
    return pltpu.CompilerParams(dimension_semantics=("arbitrary",) * n_axes, vmem_limit_bytes=VMEM_LIMIT_BYTES)


def _dot(a, b):
    return lax.dot_general(a, b, (((1,), (0,)), ((), ())), preferred_element_type=F32)


def _dot_nt(a, b):
    return lax.dot_general(a, b, (((1,), (1,)), ((), ())), preferred_element_type=F32)


def _dot_tn(a, b):
    return lax.dot_general(a, b, (((0,), (0,)), ((), ())), preferred_element_type=F32)


def _mx(a):
    return a.astype(MXU_DTYPE)


def _silu_and_grad(g):
    sg = jax.nn.sigmoid(g)
    return g * sg, sg * (1.0 + g * (1.0 - sg))


def _rms(x, g):
    r = lax.rsqrt(jnp.mean(x * x, axis=-1, keepdims=True) + EPS)
    return x * r * g, r


def _rms_bwd(x, r, g, dy):
    xhat = x * r
    dyg = dy * g
    dx = r * (dyg - xhat * jnp.mean(dyg * xhat, axis=-1, keepdims=True))
    return dx, dy * xhat


def _acc(ref, val, first):
    @pl.when(first)
    def _():
        ref[...] = val

    @pl.when(jnp.logical_not(first))
    def _():
        ref[...] += val


def _colsum(a):
    return jnp.sum(a, axis=0, keepdims=True)


def _in_proj_fwd(x, g, w, name, tm=512):
    t, d = x.shape
    n = w.shape[1]
    tm = min(tm, t)

    def body(x_ref, g_ref, w_ref, z_ref, h_ref):
        h, _ = _rms(x_ref[...], g_ref[...])
        h = _mx(h)
        h_ref[...] = h
        z_ref[...] = _dot(h, w_ref[...])

    return pl.pallas_call(
        body, name=name, grid=(t // tm,),
        in_specs=[pl.BlockSpec((tm, d), lambda i: (i, 0)), pl.BlockSpec((1, d), lambda i: (0, 0)),
                  pl.BlockSpec((d, n), lambda i: (0, 0))],
        out_specs=[pl.BlockSpec((tm, n), lambda i: (i, 0)), pl.BlockSpec((tm, d), lambda i: (i, 0))],
        out_shape=[jax.ShapeDtypeStruct((t, n), F32), jax.ShapeDtypeStruct((t, d), MXU_DTYPE)],
        compiler_params=_cp())(x, g, w)


def _in_proj_bwd(dz_mix, dz_c, dz_kr, dz_gate, h, x, d_res, w, g, name, tm=256):
    t, d = x.shape
    n = w.shape[1]
    tm = min(tm, t)

    def body(dm_ref, dc_ref, dk_ref, dg_ref, h_ref, x_ref, dres_ref, w_ref, g_ref, dx_ref, dw_ref, dgn_ref):
        first = pl.program_id(0) == 0
        dz = jnp.concatenate([dm_ref[...], dc_ref[...], dk_ref[...], dg_ref[...]], axis=1)

        @pl.when(first)
        def _():
            dw_ref[...] = jnp.zeros(dw_ref.shape, F32)

        hb = h_ref[...]
        for c0 in range(0, n, 512):
            dw_ref[:, c0:c0 + 512] += _dot_tn(hb, dz[:, c0:c0 + 512])
        dh = _dot_nt(dz, w_ref[...])
        xf = x_ref[...]
        r = lax.rsqrt(jnp.mean(xf * xf, axis=-1, keepdims=True) + EPS)
        dx, dgt = _rms_bwd(xf, r, g_ref[...], dh)
        dx_ref[...] = dx + dres_ref[...]
        _acc(dgn_ref, _colsum(dgt), first)

    row = lambda wd: pl.BlockSpec((tm, wd), lambda i: (i, 0))
    fixed = lambda a, b: pl.BlockSpec((a, b), lambda i: (0, 0))
    return pl.pallas_call(
        body, name=name, grid=(t // tm,),
        in_specs=[row(Z_MIX), row(Z_C), row(Z_KR), row(Z_GATE), row(d), row(d), row(d), fixed(d, n), fixed(1, d)],
        out_specs=[row(d), fixed(d, n), fixed(1, d)],
        out_shape=[jax.ShapeDtypeStruct((t, d), F32), jax.ShapeDtypeStruct((d, n), F32),
                   jax.ShapeDtypeStruct((1, d), F32)],
        compiler_params=_cp())(dz_mix, dz_c, dz_kr, dz_gate, h, x, d_res, w, g)


def _lane_group(shape):
    return lax.broadcasted_iota(jnp.int32, shape, 1) // 64


def _select_group(vals):
    grp = _lane_group(vals[0].shape)
    out = vals[3]
    for gi in (2, 1, 0):
        out = jnp.where(grp == gi, vals[gi], out)
    return out


def _sgu_mask(transposed):
    r = (lax.broadcasted_iota(jnp.int32, (512, 128), 0) % 128) // 64
    c = lax.broadcasted_iota(jnp.int32, (512, 128), 1) // 64
    return (r <= c) if transposed else (c <= r)


def _sgu_apply(wstack, vb, nblk):
    outs = []
    for n in range(nblk):
        r = _dot(wstack, vb[n * 128:(n + 1) * 128, :])
        outs.append(_select_group([r[hh * 128:(hh + 1) * 128, :] for hh in range(4)]))
    return jnp.concatenate(outs, axis=0)


def _layer_norm(v, g, b):
    mu = jnp.mean(v, axis=-1, keepdims=True)
    vc = v - mu
    rstd = lax.rsqrt(jnp.mean(vc * vc, axis=-1, keepdims=True) + EPS)
    vhat = vc * rstd
    return vhat * g + b, vhat, rstd


def _pool_counts(t0, n):
    t = t0 + lax.broadcasted_iota(jnp.int32, (n, 256), 0)
    w = _select_group([jnp.full((n, 256), wv, jnp.int32) for wv in (2, 4, 8, 16)])
    return jnp.minimum(t + 1, w).astype(F32)


def _pooled(p, halo, t0):
    tm = p.shape[0]
    ext = jnp.concatenate([halo, p], axis=0)
    s2 = ext + pltpu.roll(ext, 1, 0)
    s4 = s2 + pltpu.roll(s2, 2, 0)
    s8 = s4 + pltpu.roll(s4, 4, 0)
    s16 = s8 + pltpu.roll(s8, 8, 0)
    sel = _select_group([s2, s4, s8, s16])[16:, :]
    return sel / _pool_counts(t0, tm) - p


def _pooled_bwd(dpool, dpool_halo, t0):
    tm = dpool.shape[0]
    n = tm + 16
    ext = jnp.concatenate([dpool, dpool_halo], axis=0) / _pool_counts(t0, n)
    f2 = ext + pltpu.roll(ext, n - 1, 0)
    f4 = f2 + pltpu.roll(f2, n - 2, 0)
    f8 = f4 + pltpu.roll(f4, n - 4, 0)
    f16 = f8 + pltpu.roll(f8, n - 8, 0)
    return _select_group([f2, f4, f8, f16])[:tm, :] - dpool


def _mix_specs(t, tm):
    nt16 = t // 16
    zrow = pl.BlockSpec((tm, Z_MIX), lambda i: (i, 0))
    prev_halo = pl.BlockSpec((16, 256), lambda i: (jnp.maximum(i * (tm // 16) - 1, 0), 3))
    fixed = lambda a, b: pl.BlockSpec((a, b), lambda i: (0, 0))
    params = [fixed(512, 128), fixed(128, 256), fixed(1, 256), fixed(1, 256), fixed(256, 256), fixed(1, 256)]
    return nt16, zrow, prev_halo, fixed, params


def _mix_fwd(z, sgu_w, sgu_bias, ln_g, ln_b, pool_wbd, pool_scale, name, tm=256):
    t = z.shape[0]
    tm = min(tm, t)
    _, zrow, prev_halo, _, params = _mix_specs(t, tm)

    def body(z_ref, halo_ref, w_ref, bias_ref, lng_ref, lnb_ref, pw_ref, ps_ref, y_ref):
        i = pl.program_id(0)
        u, v, gate = z_ref[:, 0:256], z_ref[:, 256:512], z_ref[:, 512:768]
        p, pgate = z_ref[:, 768:1024], z_ref[:, 1024:1280]
        vn, _, _ = _layer_norm(v, lng_ref[...], lnb_ref[...])
        wm = _mx(jnp.where(_sgu_mask(False), w_ref[...], 0.0))
        mixed = _sgu_apply(wm, _mx(vn), tm // 128) + jnp.tile(bias_ref[...], (tm // 128, 1))
        ya = u * mixed * _silu_and_grad(gate)[0]
        halo = jnp.where(i > 0, halo_ref[...], 0.0)
        pooled = _pooled(p, halo, i * tm)
        yb = _dot(_mx(pooled), pw_ref[...]) * ps_ref[...] * _silu_and_grad(pgate)[0]
        y_ref[...] = _mx(jnp.concatenate([ya, yb], axis=1))

    return pl.pallas_call(
        body, name=name, grid=(t // tm,),
        in_specs=[zrow, prev_halo] + params,
        out_specs=pl.BlockSpec((tm, 512), lambda i: (i, 0)),
        out_shape=jax.ShapeDtypeStruct((t, 512), MXU_DTYPE),
        compiler_params=_cp())(z, z, sgu_w, sgu_bias, ln_g, ln_b, pool_wbd, pool_scale)


def _mix_bwd(z, dycat, sgu_w, sgu_wt, sgu_bias, ln_g, ln_b, pool_wbd, pool_scale, name, tm=256):
    t = z.shape[0]
    tm = min(tm, t)
    nt16, zrow, prev_halo, fixed, params = _mix_specs(t, tm)
    nblk = tm // 128
    last = t // tm - 1

    def body(z_ref, halo_ref, zn_ref, dy_ref, dyn_ref, w_ref, wt_ref, bias_ref, lng_ref, lnb_ref, pw_ref, ps_ref,
             dz_ref, dw_ref, db_ref, dlng_ref, dlnb_ref, dpw_ref, dps_ref):
        i = pl.program_id(0)
        first = i == 0
        u, v, gate = z_ref[:, 0:256], z_ref[:, 256:512], z_ref[:, 512:768]
        p, pgate = z_ref[:, 768:1024], z_ref[:, 1024:1280]
        dya, dyb = dy_ref[:, 0:256], dy_ref[:, 256:512]
        vn, vhat, rstd = _layer_norm(v, lng_ref[...], lnb_ref[...])
        vnb = _mx(vn)
        wm = _mx(jnp.where(_sgu_mask(False), w_ref[...], 0.0))
        wmt = _mx(jnp.where(_sgu_mask(True), wt_ref[...], 0.0))
        mixed = _sgu_apply(wm, vnb, nblk) + jnp.tile(bias_ref[...], (nblk, 1))
        silu, dsilu = _silu_and_grad(gate)
        t1 = u * mixed
        d_gate = dya * t1 * dsilu
        d_t1 = dya * silu
        d_u = d_t1 * mixed
        d_mixed = d_t1 * u
        dmb = _mx(d_mixed)
        d_vn = _sgu_apply(wmt, dmb, nblk)
        grp = _lane_group((128, 256))
        lane = lax.broadcasted_iota(jnp.int32, (128, 128), 1)
        dws = [jnp.zeros((128, 128), F32) for _ in range(4)]
        dbias = jnp.zeros((128, 128), F32)
        for n in range(nblk):
            dm_n, dmb_n, vnb_n = d_mixed[n * 128:(n + 1) * 128], dmb[n * 128:(n + 1) * 128], vnb[n * 128:(n + 1) * 128]
            for hh in range(4):
                dws[hh] = dws[hh] + _dot_nt(jnp.where(grp == hh, dmb_n, jnp.zeros_like(dmb_n)), vnb_n)
                rs = jnp.sum(jnp.where(grp == hh, dm_n, 0.0), axis=-1, keepdims=True)
                dbias = dbias + jnp.where(lane == hh, rs, 0.0)
        _acc(dw_ref, jnp.concatenate(dws, axis=0), first)
        _acc(db_ref, dbias, first)
        _acc(dlng_ref, _colsum(d_vn * vhat), first)
        _acc(dlnb_ref, _colsum(d_vn), first)
        dvh = d_vn * lng_ref[...]
        d_v = rstd * (dvh - jnp.mean(dvh, axis=-1, keepdims=True) - vhat * jnp.mean(dvh * vhat, axis=-1, keepdims=True))

        @pl.when(i == last)
        def _():
            dw_ref[...] = jnp.where(_sgu_mask(False), dw_ref[...], 0.0)

        halo = jnp.where(i > 0, halo_ref[...], 0.0)
        pooled = _pooled(p, halo, i * tm)
        pooled_b = _mx(pooled)
        mixedp = _dot(pooled_b, pw_ref[...])
        psilu, pdsilu = _silu_and_grad(pgate)
        d_pgate = dyb * (mixedp * ps_ref[...]) * pdsilu
        d_ms = dyb * psilu
        _acc(dps_ref, _colsum(d_ms * mixedp), first)
        dmpb = _mx(d_ms * ps_ref[...])
        _acc(dpw_ref, _dot_tn(pooled_b, dmpb), first)
        d_pooled = _dot_nt(dmpb, pw_ref[...])
        dmp_halo = _mx(dyn_ref[...] * _silu_and_grad(zn_ref[...])[0] * ps_ref[...])
        d_pooled_halo = jnp.where(i < last, _dot_nt(dmp_halo, pw_ref[...]), 0.0)
        d_p = _pooled_bwd(d_pooled, d_pooled_halo, i * tm)
        dz_ref[...] = _mx(jnp.concatenate([d_u, d_v, d_gate, d_p, d_pgate], axis=1))

    nxt = lambda i: jnp.minimum((i + 1) * (tm // 16), nt16 - 1)
    return pl.pallas_call(
        body, name=name, grid=(t // tm,),
        in_specs=[zrow, prev_halo, pl.BlockSpec((16, 256), lambda i: (nxt(i), 4)),
                  pl.BlockSpec((tm, 512), lambda i: (i, 0)), pl.BlockSpec((16, 256), lambda i: (nxt(i), 1)),
                  params[0], fixed(512, 128)] + params[1:],
        out_specs=[pl.BlockSpec((tm, Z_MIX), lambda i: (i, 0)), fixed(512, 128), fixed(128, 128), fixed(1, 256),
                   fixed(1, 256), fixed(256, 256), fixed(1, 256)],
        out_shape=[jax.ShapeDtypeStruct((t, Z_MIX), MXU_DTYPE), jax.ShapeDtypeStruct((512, 128), F32),
                   jax.ShapeDtypeStruct((128, 128), F32), jax.ShapeDtypeStruct((1, 256), F32),
                   jax.ShapeDtypeStruct((1, 256), F32), jax.ShapeDtypeStruct((256, 256), F32),
                   jax.ShapeDtypeStruct((1, 256), F32)],
        compiler_params=_cp())(z, z, z, dycat, dycat, sgu_w, sgu_wt, sgu_bias, ln_g, ln_b, pool_wbd, pool_scale)


def _rot_half(x, transpose):
    w = x.shape[1]
    lane = lax.broadcasted_iota(jnp.int32, x.shape, 1) % min(w, 256)
    base = 128 if w >= 256 else 0
    lo = jnp.logical_and(lane >= base, lane < base + 32)
    hi = jnp.logical_and(lane >= base + 32, lane < base + 64)
    up = pltpu.roll(x, w - 32, 1)
    down = pltpu.roll(x, 32, 1)
    if transpose:
        return jnp.where(lo, up, jnp.where(hi, -down, 0.0))
    return jnp.where(lo, -up, jnp.where(hi, down, 0.0))


def _rope(x, c, s):
    return x * c + _rot_half(x, False) * s


def _rope_bwd(dy, c, s):
    return dy * c + _rot_half(dy * s, True)


def _qkv_fwd(z, rc, rs, w_uq, w_ukv, gq, gkv, name, tm=256):
    t = z.shape[0]
    tm = min(tm, t)

    def body(zc_ref, zk_ref, rc_ref, rs_ref, wq_ref, wkv_ref, gq_ref, gkv_ref, q_ref, k_ref, v_ref):
        cq, ckv = zc_ref[:, 0:384], zc_ref[:, 384:640]
        c, s = rc_ref[...], rs_ref[...]
        qn, _ = _rms(cq, gq_ref[...])
        q = _rope(_dot(_mx(qn), wq_ref[...]), jnp.tile(c, (1, N_HEADS)), jnp.tile(s, (1, N_HEADS)))
        kvn, _ = _rms(ckv, gkv_ref[...])
        kv = _dot(_mx(kvn), wkv_ref[...])
        kpe = _rope(zk_ref[...], c[:, 128:256], s[:, 128:256])
        for hh in range(N_HEADS):
            q_ref[hh] = _mx(q[:, hh * QK_PAD:(hh + 1) * QK_PAD])
            k_ref[hh] = _mx(jnp.concatenate([kv[:, hh * 128:(hh + 1) * 128], kpe], axis=1))
            v_ref[hh] = _mx(kv[:, 512 + hh * 128:512 + (hh + 1) * 128])

    fixed = lambda a, b: pl.BlockSpec((a, b), lambda i: (0, 0))
    heads = lambda wd: pl.BlockSpec((N_HEADS, tm, wd), lambda i: (0, i, 0))
    return pl.pallas_call(
        body, name=name, grid=(t // tm,),
        in_specs=[pl.BlockSpec((tm, Z_C), lambda i: (i, Z_MIX // Z_C)),
                  pl.BlockSpec((tm, Z_KR), lambda i: (i, (Z_MIX + Z_C) // Z_KR)),
                  pl.BlockSpec((tm, 256), lambda i: (i, 0)), pl.BlockSpec((tm, 256), lambda i: (i, 0)),
                  fixed(384, 1024), fixed(256, 1024), fixed(1, 384), fixed(1, 256)],
        out_specs=[heads(QK_PAD), heads(QK_PAD), heads(V_DIM)],
        out_shape=[jax.ShapeDtypeStruct((N_HEADS, t, QK_PAD), MXU_DTYPE),
                   jax.ShapeDtypeStruct((N_HEADS, t, QK_PAD), MXU_DTYPE),
                   jax.ShapeDtypeStruct((N_HEADS, t, V_DIM), MXU_DTYPE)],
        compiler_params=_cp())(z, z, rc, rs, w_uq, w_ukv, gq, gkv)


def _qkv_bwd(dq, dk, dv, z, rc, rs, w_uq, w_ukv, gq, gkv, name, tm=256):
    t = z.shape[0]
    tm = min(tm, t)

    def body(dq_ref, dk_ref, dv_ref, zc_ref, rc_ref, rs_ref, wq_ref, wkv_ref, gq_ref, gkv_ref,
             dzc_ref, dzk_ref, dwq_ref, dwkv_ref, dgq_ref, dgkv_ref):
        first = pl.program_id(0) == 0
        cq, ckv = zc_ref[:, 0:384], zc_ref[:, 384:640]
        c, s = rc_ref[...], rs_ref[...]
        dq_all = jnp.concatenate([dq_ref[hh] for hh in range(N_HEADS)], axis=1)
        dqp = _mx(_rope_bwd(dq_all, jnp.tile(c, (1, N_HEADS)), jnp.tile(s, (1, N_HEADS))))
        qn, rq = _rms(cq, gq_ref[...])
        _acc(dwq_ref, _dot_tn(_mx(qn), dqp), first)
        d_cq, dgq_t = _rms_bwd(cq, rq, gq_ref[...], _dot_nt(dqp, wq_ref[...]))
        _acc(dgq_ref, _colsum(dgq_t), first)
        dkpe = dk_ref[0][:, 128:256]
        for hh in range(1, N_HEADS):
            dkpe = dkpe + dk_ref[hh][:, 128:256]
        dzk_ref[...] = _mx(_rope_bwd(dkpe, c[:, 128:256], s[:, 128:256]))
        dkv = _mx(jnp.concatenate([dk_ref[hh][:, 0:128] for hh in range(N_HEADS)]
                                  + [dv_ref[hh] for hh in range(N_HEADS)], axis=1))
        kvn, rkv = _rms(ckv, gkv_ref[...])
        _acc(dwkv_ref, _dot_tn(_mx(kvn), dkv), first)
        d_ckv, dgkv_t = _rms_bwd(ckv, rkv, gkv_ref[...], _dot_nt(dkv, wkv_ref[...]))
        _acc(dgkv_ref, _colsum(dgkv_t), first)
        dzc_ref[...] = _mx(jnp.concatenate([d_cq, d_ckv], axis=1))

    fixed = lambda a, b: pl.BlockSpec((a, b), lambda i: (0, 0))
    heads = lambda wd: pl.BlockSpec((N_HEADS, tm, wd), lambda i: (0, i, 0))
    return pl.pallas_call(
        body, name=name, grid=(t // tm,),
        in_specs=[heads(QK_PAD), heads(QK_PAD), heads(V_DIM), pl.BlockSpec((tm, Z_C), lambda i: (i, Z_MIX // Z_C)),
                  pl.BlockSpec((tm, 256), lambda i: (i, 0)), pl.BlockSpec((tm, 256), lambda i: (i, 0)),
                  fixed(384, 1024), fixed(256, 1024), fixed(1, 384), fixed(1, 256)],
        out_specs=[pl.BlockSpec((tm, Z_C), lambda i: (i, 0)), pl.BlockSpec((tm, Z_KR), lambda i: (i, 0)),
                   fixed(384, 1024), fixed(256, 1024), fixed(1, 384), fixed(1, 256)],
        out_shape=[jax.ShapeDtypeStruct((t, Z_C), MXU_DTYPE), jax.ShapeDtypeStruct((t, Z_KR), MXU_DTYPE),
                   jax.ShapeDtypeStruct((384, 1024), F32), jax.ShapeDtypeStruct((256, 1024), F32),
                   jax.ShapeDtypeStruct((1, 384), F32), jax.ShapeDtypeStruct((1, 256), F32)],
        compiler_params=_cp())(dq, dk, dv, z, rc, rs, w_uq, w_ukv, gq, gkv)


def _chunk_mask(tq):
    r = lax.broadcasted_iota(jnp.int32, (tq, tq), 0) // 64
    c = lax.broadcasted_iota(jnp.int32, (tq, tq), 1) // 64
    return c <= r


def _gate_block(tq):
    return pl.BlockSpec((tq, 128), lambda h, i: (i, (Z_MIX + Z_C + Z_KR) // 128 + h))


def _attn_fwd(qh, kh, vh, z, name, tq=512):
    t = qh.shape[1]
    tq = min(tq, t)

    def body(q_ref, g_ref, k_hbm, v_hbm, o_ref, yc_ref, lse_ref, k_v, v_v, m_s, l_s, acc_s, sem):
        h, i = pl.program_id(0), pl.program_id(1)

        @pl.when(i == 0)
        def _():
            ck = pltpu.make_async_copy(k_hbm.at[h], k_v, sem.at[0])
            cv = pltpu.make_async_copy(v_hbm.at[h], v_v, sem.at[1])
            ck.start()
            cv.start()
            ck.wait()
            cv.wait()

        q = q_ref[...]
        m_s[...] = jnp.full(m_s.shape, NEG_INF, F32)
        l_s[...] = jnp.zeros(l_s.shape, F32)
        acc_s[...] = jnp.zeros(acc_s.shape, F32)

        def block(j, masked):
            ks = pl.ds(pl.multiple_of(j * tq, tq), tq)
            s = _dot_nt(q, k_v[ks, :]) * SCALE
            if masked:
                s = jnp.where(_chunk_mask(tq), s, NEG_INF)
            m_old = m_s[:, 0:1]
            m_new = jnp.maximum(m_old, jnp.max(s, axis=-1, keepdims=True))
            p = jnp.exp(s - m_new)
            alpha = jnp.exp(m_old - m_new)
            l_s[...] = jnp.broadcast_to(alpha * l_s[:, 0:1] + jnp.sum(p, axis=-1, keepdims=True), l_s.shape)
            m_s[...] = jnp.broadcast_to(m_new, m_s.shape)
            acc_s[...] = alpha * acc_s[...] + _dot(_mx(p), v_v[ks, :])

        def loop_body(j, carry):
            block(j, False)
            return carry

        lax.fori_loop(0, i, loop_body, 0)
        block(i, True)
        o = acc_s[...] / l_s[:, 0:1]
        o_ref[...] = o
        yc_ref[...] = _mx(o * _silu_and_grad(g_ref[...])[0])
        lse_ref[...] = m_s[...] + jnp.log(l_s[...])

    return pl.pallas_call(
        body, name=name, grid=(N_HEADS, t // tq),
        in_specs=[pl.BlockSpec((None, tq, QK_PAD), lambda h, i: (h, i, 0)), _gate_block(tq), ANY, ANY],
        out_specs=[pl.BlockSpec((tq, 128), lambda h, i: (i, h)), pl.BlockSpec((tq, 128), lambda h, i: (i, h)),
                   pl.BlockSpec((None, tq, 128), lambda h, i: (h, i, 0))],
        out_shape=[jax.ShapeDtypeStruct((t, N_HEADS * V_DIM), F32), jax.ShapeDtypeStruct((t, N_HEADS * V_DIM), MXU_DTYPE),
                   jax.ShapeDtypeStruct((N_HEADS, t, 128), F32)],
        scratch_shapes=[pltpu.VMEM((t, QK_PAD), MXU_DTYPE), pltpu.VMEM((t, V_DIM), MXU_DTYPE),
                        pltpu.VMEM((tq, 128), F32), pltpu.VMEM((tq, 128), F32), pltpu.VMEM((tq, V_DIM), F32),
                        pltpu.SemaphoreType.DMA((2,))],
        compiler_params=_cp(2))(qh, z, kh, vh)


def _attn_bwd(qh, kh, vh, o, lse, dycat, z, name, tq=512):
    t = qh.shape[1]
    tq = min(tq, t)
    nq = t // tq

    def body(q_ref, o_ref, lse_ref, dy_ref, g_ref, k_hbm, v_hbm, dq_ref, dgate_ref, dk_hbm, dv_hbm,
             k_v, v_v, dk_acc, dv_acc, dq_acc, sem):
        h, i = pl.program_id(0), pl.program_id(1)

        @pl.when(i == 0)
        def _():
            ck = pltpu.make_async_copy(k_hbm.at[h], k_v, sem.at[0])
            cv = pltpu.make_async_copy(v_hbm.at[h], v_v, sem.at[1])
            ck.start()
            cv.start()
            dk_acc[...] = jnp.zeros(dk_acc.shape, F32)
            dv_acc[...] = jnp.zeros(dv_acc.shape, F32)
            ck.wait()
            cv.wait()

        gate, dy, of = g_ref[...], dy_ref[...], o_ref[...]
        silu, dsilu = _silu_and_grad(gate)
        do = dy * silu
        delta = jnp.sum(do * of, axis=-1, keepdims=True)
        dgate_ref[...] = _mx(dy * of * dsilu)
        dob = _mx(do)
        q = q_ref[...]
        lse_col = lse_ref[:, 0:1]
        dq_acc[...] = jnp.zeros(dq_acc.shape, F32)

        def block(j, masked):
            ks = pl.ds(pl.multiple_of(j * tq, tq), tq)
            k, v = k_v[ks, :], v_v[ks, :]
            p = jnp.exp(_dot_nt(q, k) * SCALE - lse_col)
            if masked:
                p = jnp.where(_chunk_mask(tq), p, 0.0)
            ds = p * (_dot_nt(dob, v) - delta) * SCALE
            pb, dsb = _mx(p), _mx(ds)
            dq_acc[...] += _dot(dsb, k)
            dk_acc[ks, :] += _dot_tn(dsb, q)
            dv_acc[ks, :] += _dot_tn(pb, dob)

        def loop_body(j, carry):
            block(j, False)
            return carry

        lax.fori_loop(0, i, loop_body, 0)
        block(i, True)
        dq_ref[...] = dq_acc[...]

        @pl.when(i == nq - 1)
        def _():
            ck = pltpu.make_async_copy(dk_acc, dk_hbm.at[h], sem.at[0])
            cv = pltpu.make_async_copy(dv_acc, dv_hbm.at[h], sem.at[1])
            ck.start()
            cv.start()
            ck.wait()
            cv.wait()

    return pl.pallas_call(
        body, name=name, grid=(N_HEADS, nq),
        in_specs=[pl.BlockSpec((None, tq, QK_PAD), lambda h, i: (h, i, 0)),
                  pl.BlockSpec((tq, 128), lambda h, i: (i, h)),
                  pl.BlockSpec((None, tq, 128), lambda h, i: (h, i, 0)),
                  pl.BlockSpec((tq, 128), lambda h, i: (i, N_HEADS + h)), _gate_block(tq), ANY, ANY],
        out_specs=[pl.BlockSpec((None, tq, QK_PAD), lambda h, i: (h, i, 0)),
                   pl.BlockSpec((tq, 128), lambda h, i: (i, h)), ANY, ANY],
        out_shape=[jax.ShapeDtypeStruct((N_HEADS, t, QK_PAD), F32), jax.ShapeDtypeStruct((t, Z_GATE), MXU_DTYPE),
                   jax.ShapeDtypeStruct((N_HEADS, t, QK_PAD), F32), jax.ShapeDtypeStruct((N_HEADS, t, V_DIM), F32)],
        scratch_shapes=[pltpu.VMEM((t, QK_PAD), MXU_DTYPE), pltpu.VMEM((t, V_DIM), MXU_DTYPE),
                        pltpu.VMEM((t, QK_PAD), F32), pltpu.VMEM((t, V_DIM), F32), pltpu.VMEM((tq, QK_PAD), F32),
                        pltpu.SemaphoreType.DMA((2,))],
        compiler_params=_cp(2))(qh, o, lse, dycat, z, kh, vh)


def _out_proj_fwd(yab, yc, w, x, g, target, name, tm=512):
    t, d = x.shape
    tm = min(tm, t)
    is_last = target is not None

    def body(*refs):
        if is_last:
            yab_ref, yc_ref, w_ref, x_ref, g_ref, t_ref, y_ref, dout_ref, loss_ref = refs
        else:
            yab_ref, yc_ref, w_ref, x_ref, g_ref, y_ref, out_ref = refs
        y = _dot(jnp.concatenate([yab_ref[...], yc_ref[...]], axis=1), w_ref[...])
        y_ref[...] = y
        out = x_ref[...] + _rms(y, g_ref[...])[0]
        if is_last:
            diff = out - t_ref[...]
            dout_ref[...] = diff * (1.0 / d)
            part = jnp.sum(jnp.sum(diff * diff, axis=-1, keepdims=True), axis=0, keepdims=True) * (0.5 / d)
            _acc(loss_ref, jnp.broadcast_to(part, (1, 128)), pl.program_id(0) == 0)
        else:
            out_ref[...] = out

    row = lambda wd: pl.BlockSpec((tm, wd), lambda i: (i, 0))
    fixed = lambda a, b: pl.BlockSpec((a, b), lambda i: (0, 0))
    in_specs = [row(512), row(512), fixed(d, d), row(d), fixed(1, d)]
    args = [yab, yc, w, x, g]
    out_specs = [row(d), row(d)]
    out_shape = [jax.ShapeDtypeStruct((t, d), F32), jax.ShapeDtypeStruct((t, d), F32)]
    if is_last:
        in_specs.append(row(d))
        args.append(target)
        out_specs.append(fixed(1, 128))
        out_shape.append(jax.ShapeDtypeStruct((1, 128), F32))
    return pl.pallas_call(body, name=name, grid=(t // tm,), in_specs=in_specs, out_specs=out_specs,
                          out_shape=out_shape, compiler_params=_cp())(*args)


def _out_proj_bwd(dout, y, yab, yc, w, g, name, tm=512):
    t, d = y.shape
    tm = min(tm, t)

    def body(dout_ref, y_ref, yab_ref, yc_ref, w_ref, g_ref, dycat_ref, dw_ref, dg_ref):
        first = pl.program_id(0) == 0
        y = y_ref[...]
        r = lax.rsqrt(jnp.mean(y * y, axis=-1, keepdims=True) + EPS)
        dy, dgt = _rms_bwd(y, r, g_ref[...], dout_ref[...])
        _acc(dg_ref, _colsum(dgt), first)
        dyb = _mx(dy)
        _acc(dw_ref, _dot_tn(jnp.concatenate([yab_ref[...], yc_ref[...]], axis=1), dyb), first)
        dycat_ref[...] = _dot_nt(dyb, w_ref[...])

    row = lambda wd: pl.BlockSpec((tm, wd), lambda i: (i, 0))
    fixed = lambda a, b: pl.BlockSpec((a, b), lambda i: (0, 0))
    return pl.pallas_call(
        body, name=name, grid=(t // tm,),
        in_specs=[row(d), row(d), row(512), row(512), fixed(d, d), fixed(1, d)],
        out_specs=[row(d), fixed(d, d), fixed(1, d)],
        out_shape=[jax.ShapeDtypeStruct((t, d), F32), jax.ShapeDtypeStruct((d, d), F32),
                   jax.ShapeDtypeStruct((1, d), F32)],
        compiler_params=_cp())(dout, y, yab, yc, w, g)


def _mesh_pos():
    return lax.axis_index("x"), lax.axis_index("y"), lax.axis_index("c")


def _remote(src, dst, send_sem, recv_sem, to):
    return pltpu.make_async_remote_copy(src_ref=src, dst_ref=dst, send_sem=send_sem, recv_sem=recv_sem,
                                        device_id=to, device_id_type=MESH)


def _gather_weights(shards):
    n = len(shards)

    def body(*refs):
        ins, outs = refs[:n], refs[n:2 * n]
        send_sems, recv_sems, local_sems = refs[2 * n:]
        x, y, c = _mesh_pos()
        me, sibling = (x, y, c), (x, y, 1 - c)
        chips = [(1 - x, y), (x, 1 - y), (1 - x, 1 - y)]
        slot = lambda cx, cy, layer: 2 * (2 * cx + cy) + layer
        local = []
        for a in range(n):
            for layer in range(2):
                local.append(pltpu.make_async_copy(ins[a].at[layer], outs[a].at[slot(x, y, layer)],
                                                   local_sems.at[a, layer]))
                local[-1].start()
        first = []
        for a in range(n):
            for j, (cx, cy) in enumerate(chips):
                first.append(_remote(ins[a].at[c], outs[a].at[slot(x, y, c)], send_sems.at[a, j], recv_sems.at[a, j],
                                     (cx, cy, c)))
                first[-1].start()
        passed = []
        for a in range(n):
            for j, (cx, cy) in enumerate(chips):
                blk = outs[a].at[slot(cx, cy, c)]
                _remote(blk, blk, send_sems.at[a, j], recv_sems.at[a, j], me).wait_recv()
                passed.append(_remote(blk, blk, send_sems.at[a, 3 + j], recv_sems.at[a, 3 + j], sibling))
                passed[-1].start()
        for a in range(n):
            for j, (cx, cy) in enumerate(chips):
                blk = outs[a].at[slot(cx, cy, 1 - c)]
                _remote(blk, blk, send_sems.at[a, 3 + j], recv_sems.at[a, 3 + j], me).wait_recv()
        for cp in first + passed:
            cp.wait_send()
        for cp in local:
            cp.wait()

    return pl.pallas_call(
        body, name="gather_weights", in_specs=[ANY] * n, out_specs=[ANY] * n,
        out_shape=[jax.ShapeDtypeStruct((8,) + s.shape[1:], s.dtype) for s in shards],
        scratch_shapes=[pltpu.SemaphoreType.DMA((n, 6)), pltpu.SemaphoreType.DMA((n, 6)),
                        pltpu.SemaphoreType.DMA((n, 2))])(*shards)


def _all_to_all(parts, common):
    n = len(parts)

    def body(*refs):
        ins, outs = refs[:n + 1], refs[n + 1:2 * n + 2]
        send_sems, recv_sems, local_sems = refs[2 * n + 2:]
        x, y, c = _mesh_pos()
        me = 4 * x + 2 * y + c
        src = lambda a, j: ins[a].at[j] if a < n else ins[a]
        local = []
        for a in range(n + 1):
            local.append(pltpu.make_async_copy(src(a, me), outs[a].at[me], local_sems.at[a]))
            local[-1].start()
        sent = []
        for rel in range(1, 8):
            px = 1 - x if rel & 4 else x
            py = 1 - y if rel & 2 else y
            pc = 1 - c if rel & 1 else c
            peer = 4 * px + 2 * py + pc
            for a in range(n + 1):
                sent.append(_remote(src(a, peer), outs[a].at[me], send_sems.at[a, rel], recv_sems.at[a, rel],
                                    (px, py, pc)))
                sent[-1].start()
        for rel in range(1, 8):
            px = 1 - x if rel & 4 else x
            py = 1 - y if rel & 2 else y
            pc = 1 - c if rel & 1 else c
            peer = 4 * px + 2 * py + pc
            for a in range(n + 1):
                _remote(src(a, peer), outs[a].at[peer], send_sems.at[a, rel], recv_sems.at[a, rel],
                        (x, y, c)).wait_recv()
        for cp in sent:
            cp.wait_send()
        for cp in local:
            cp.wait()

    return pl.pallas_call(
        body, name="grad_all_to_all", in_specs=[ANY] * (n + 1), out_specs=[ANY] * (n + 1),
        out_shape=[jax.ShapeDtypeStruct(p.shape, p.dtype) for p in parts]
        + [jax.ShapeDtypeStruct((8,) + common.shape, common.dtype)],
        scratch_shapes=[pltpu.SemaphoreType.DMA((n + 1, 8)), pltpu.SemaphoreType.DMA((n + 1, 8)),
                        pltpu.SemaphoreType.DMA((n + 1,))])(*parts, common)


def _sibling_exchange(sums):
    n = len(sums)

    def body(*refs):
        ins, outs = refs[:n], refs[n:2 * n]
        send_sems, recv_sems, local_sems = refs[2 * n:]
        x, y, c = _mesh_pos()
        copies = []
        for a in range(n):
            copies.append(pltpu.make_async_copy(ins[a], outs[a].at[c], local_sems.at[a]))
            copies[-1].start()
        sent = []
        for a in range(n):
            sent.append(_remote(ins[a], outs[a].at[c], send_sems.at[a], recv_sems.at[a], (x, y, 1 - c)))
            sent[-1].start()
        for a in range(n):
            _remote(ins[a], outs[a].at[1 - c], send_sems.at[a], recv_sems.at[a], (x, y, c)).wait_recv()
        for cp in sent:
            cp.wait_send()
        for cp in copies:
            cp.wait()

    return pl.pallas_call(
        body, name="sibling_exchange", in_specs=[ANY] * n, out_specs=[ANY] * n,
        out_shape=[jax.ShapeDtypeStruct((2,) + s.shape, s.dtype) for s in sums],
        scratch_shapes=[pltpu.SemaphoreType.DMA((n,)), pltpu.SemaphoreType.DMA((n,)),
                        pltpu.SemaphoreType.DMA((n,))])(*sums)


def _sum8(parts, name, tr):
    _, r, c = parts.shape
    tr = min(tr, r)

    def body(p_ref, o_ref):
        acc = p_ref[0]
        for j in range(1, 8):
            acc = acc + p_ref[j]
        o_ref[...] = acc

    return pl.pallas_call(
        body, name=name, grid=(r // tr,),
        in_specs=[pl.BlockSpec((8, tr, c), lambda i: (0, i, 0))], out_specs=pl.BlockSpec((tr, c), lambda i: (i, 0)),
        out_shape=jax.ShapeDtypeStruct((r, c), parts.dtype), compiler_params=_cp())(parts)


def _adamw(w, g, m, v, name, tr):
    r, c = w.shape
    tr = min(tr, r)

    def body(w_ref, g_ref, m_ref, v_ref, d_ref, nm_ref, nv_ref):
        gg = g_ref[...]
        nm = ADAM_B1 * m_ref[...] + (1.0 - ADAM_B1) * gg
        nv = ADAM_B2 * v_ref[...] + (1.0 - ADAM_B2) * jnp.square(gg)
        m_hat = nm / (1.0 - ADAM_B1 ** ADAM_STEP)
        v_hat = nv / (1.0 - ADAM_B2 ** ADAM_STEP)
        d_ref[...] = -ADAM_LR * (m_hat / (jnp.sqrt(v_hat) + ADAM_EPS) + ADAM_WD * w_ref[...])
        nm_ref[...] = nm
        nv_ref[...] = nv

    blk = pl.BlockSpec((tr, c), lambda i: (i, 0))
    return pl.pallas_call(
        body, name=name, grid=(r // tr,), in_specs=[blk] * 4, out_specs=[blk] * 3,
        out_shape=[jax.ShapeDtypeStruct((r, c), F32)] * 3, compiler_params=_cp())(w, g, m, v)


def _rope_tables(positions):
    inv_freq = ROPE_BASE ** (-jnp.arange(0, 64, 2, dtype=F32) / 64)
    ang = positions.astype(F32)[:, None] * inv_freq
    cos, sin = jnp.cos(ang), jnp.sin(ang)
    t = positions.shape[0]
    rc = jnp.concatenate([jnp.ones((t, 128), F32), cos, cos, jnp.ones((t, 64), F32)], axis=1)
    rs = jnp.concatenate([jnp.zeros((t, 128), F32), sin, sin, jnp.zeros((t, 64), F32)], axis=1)
    return rc, rs


def _layer_params(l, w_in, w_uq, w_ukv, w_out, small):
    p = {}
    p["w_in"] = jnp.concatenate([w_in[l][:, :1984], jnp.zeros((1024, 64), w_in.dtype), w_in[l][:, 1984:]], axis=1)
    p["w_uq"] = jnp.pad(w_uq[l].reshape(384, 4, 192), ((0, 0), (0, 0), (0, 64))).reshape(384, 1024)
    p["w_ukv"] = w_ukv[l].reshape(256, 4, 2, 128).transpose(0, 2, 1, 3).reshape(256, 1024)
    p["w_out"] = w_out[l]
    p["pre_g"] = small["pre_norm_g"][l][None]
    p["post_g"] = small["post_norm_g"][l][None]
    p["sgu_w"] = small["sgu_w"][l].reshape(512, 128)
    p["sgu_wt"] = small["sgu_w"][l].transpose(0, 2, 1).reshape(512, 128)
    p["sgu_bias"] = jnp.repeat(small["sgu_b"][l].T, 64, axis=1)
    p["ln_g"] = small["sgu_ln_g"][l][None]
    p["ln_b"] = small["sgu_ln_b"][l][None]
    p["pool_wbd"] = _mx(jax.scipy.linalg.block_diag(*[small["pool_w"][l][gi] for gi in range(4)]))
    p["pool_scale"] = small["pool_scale"][l][None]
    p["gq"] = small["q_norm_g"][l][None]
    p["gkv"] = small["kv_norm_g"][l][None]
    return p


def _layer_fwd(l, x, p, rc, rs, target):
    z, h = _in_proj_fwd(x, p["pre_g"], p["w_in"], f"in_proj_fwd_{l}")
    yab = _mix_fwd(z, p["sgu_w"], p["sgu_bias"], p["ln_g"], p["ln_b"], p["pool_wbd"], p["pool_scale"], f"mix_fwd_{l}")
    qh, kh, vh = _qkv_fwd(z, rc, rs, p["w_uq"], p["w_ukv"], p["gq"], p["gkv"], f"qkv_fwd_{l}")
    o, yc, lse = _attn_fwd(qh, kh, vh, z, f"attn_fwd_{l}")
    outs = _out_proj_fwd(yab, yc, p["w_out"], x, p["post_g"], target, f"out_proj_fwd_{l}")
    saved = dict(x=x, z=z, h=h, yab=yab, qh=qh, kh=kh, vh=vh, o=o, yc=yc, lse=lse, y=outs[0])
    return saved, outs[1:]


def _layer_bwd(l, dout, sv, p, rc, rs):
    dycat, dw_out, dpost = _out_proj_bwd(dout, sv["y"], sv["yab"], sv["yc"], p["w_out"], p["post_g"], f"out_proj_bwd_{l}")
    dq, dgate, dk, dv = _attn_bwd(sv["qh"], sv["kh"], sv["vh"], sv["o"], sv["lse"], dycat, sv["z"], f"attn_bwd_{l}")
    dzc, dzk, dwq, dwkv, dgq, dgkv = _qkv_bwd(dq, dk, dv, sv["z"], rc, rs, p["w_uq"], p["w_ukv"], p["gq"], p["gkv"],
                                              f"qkv_bwd_{l}")
    dzm, dsw, dsb, dlng, dlnb, dpw, dps = _mix_bwd(sv["z"], dycat, p["sgu_w"], p["sgu_wt"], p["sgu_bias"], p["ln_g"],
                                                   p["ln_b"], p["pool_wbd"], p["pool_scale"], f"mix_bwd_{l}")
    dx, dw_in, dpre = _in_proj_bwd(dzm, dzc, dzk, dgate, sv["h"], sv["x"], dout, p["w_in"], p["pre_g"], f"in_proj_bwd_{l}")
    grads = {
        "pre_norm_g": dpre[0], "post_norm_g": dpost[0],
        "w_in": jnp.concatenate([dw_in[:, :1984], dw_in[:, 2048:]], axis=1),
        "sgu_w": dsw.reshape(4, 128, 128), "sgu_b": dsb[:, :4].T, "sgu_ln_g": dlng[0], "sgu_ln_b": dlnb[0],
        "pool_w": jnp.stack([dpw[64 * gi:64 * gi + 64, 64 * gi:64 * gi + 64] for gi in range(4)]),
        "pool_scale": dps[0], "q_norm_g": dgq[0],
        "w_uq": dwq.reshape(384, 4, 256)[:, :, :192].reshape(384, 768), "kv_norm_g": dgkv[0],
        "w_ukv": dwkv.reshape(256, 2, 4, 128).transpose(0, 2, 1, 3).reshape(256, 1024), "w_out": dw_out,
    }
    return dx, grads


SMALL_NAMES = ["pre_norm_g", "post_norm_g", "sgu_w", "sgu_b", "sgu_ln_g", "sgu_ln_b", "pool_w", "pool_scale",
               "q_norm_g", "kv_norm_g"]
BIG_NAMES = ["w_in", "w_uq", "w_ukv", "w_out"]
WEIGHT_NAMES = ["pre_norm_g", "post_norm_g", "w_in", "sgu_w", "sgu_b", "sgu_ln_g", "sgu_ln_b", "pool_w", "pool_scale",
                "q_norm_g", "w_uq", "kv_norm_g", "w_ukv", "w_out"]


def _local_step(x, positions, target, w_in, w_uq, w_ukv, w_out, small):
    rc, rs = _rope_tables(positions)
    params = [_layer_params(l, w_in, w_uq, w_ukv, w_out, small) for l in range(DEPTH)]
    saved = []
    for l in range(DEPTH):
        sv, outs = _layer_fwd(l, x, params[l], rc, rs, target if l == DEPTH - 1 else None)
        saved.append(sv)
        if l < DEPTH - 1:
            x = outs[0]
    dout, loss = outs
    grads = [None] * DEPTH
    for l in reversed(range(DEPTH)):
        dout, grads[l] = _layer_bwd(l, dout, saved[l], params[l], rc, rs)
    return loss[0, 0], dout, {k: jnp.stack([grads[l][k] for l in range(DEPTH)]) for k in WEIGHT_NAMES}


def _pack_small(tree):
    flat = jnp.concatenate([tree[k].reshape(-1) for k in SMALL_NAMES])
    rows = -(-flat.shape[0] // 1024) * 8
    return jnp.pad(flat, (0, rows * 128 - flat.shape[0])).reshape(rows, 128)


def _unpack_small(packed, like):
    flat = packed.reshape(-1)
    out, off = {}, 0
    for k in SMALL_NAMES:
        size = like[k].size
        out[k] = flat[off:off + size].reshape(like[k].shape)
        off += size
    return out


def kernel(x, positions, pre_norm_g, post_norm_g, w_in, sgu_w, sgu_b, sgu_ln_g, sgu_ln_b, pool_w, pool_scale, q_norm_g, w_uq, kv_norm_g, w_ukv, w_out, loss_target, m_pre_norm_g, m_post_norm_g, m_w_in, m_sgu_w, m_sgu_b, m_sgu_ln_g, m_sgu_ln_b, m_pool_w, m_pool_scale, m_q_norm_g, m_w_uq, m_kv_norm_g, m_w_ukv, m_w_out, v_pre_norm_g, v_post_norm_g, v_w_in, v_sgu_w, v_sgu_b, v_sgu_ln_g, v_sgu_ln_b, v_pool_w, v_pool_scale, v_q_norm_g, v_w_uq, v_kv_norm_g, v_w_ukv, v_w_out):
    w = dict(pre_norm_g=pre_norm_g, post_norm_g=post_norm_g, w_in=w_in, sgu_w=sgu_w, sgu_b=sgu_b, sgu_ln_g=sgu_ln_g,
             sgu_ln_b=sgu_ln_b, pool_w=pool_w, pool_scale=pool_scale, q_norm_g=q_norm_g, w_uq=w_uq, kv_norm_g=kv_norm_g,
             w_ukv=w_ukv, w_out=w_out)
    m = dict(pre_norm_g=m_pre_norm_g, post_norm_g=m_post_norm_g, w_in=m_w_in, sgu_w=m_sgu_w, sgu_b=m_sgu_b,
             sgu_ln_g=m_sgu_ln_g, sgu_ln_b=m_sgu_ln_b, pool_w=m_pool_w, pool_scale=m_pool_scale, q_norm_g=m_q_norm_g,
             w_uq=m_w_uq, kv_norm_g=m_kv_norm_g, w_ukv=m_w_ukv, w_out=m_w_out)
    v = dict(pre_norm_g=v_pre_norm_g, post_norm_g=v_post_norm_g, w_in=v_w_in, sgu_w=v_sgu_w, sgu_b=v_sgu_b,
             sgu_ln_g=v_sgu_ln_g, sgu_ln_b=v_sgu_ln_b, pool_w=v_pool_w, pool_scale=v_pool_scale, q_norm_g=v_q_norm_g,
             w_uq=v_w_uq, kv_norm_g=v_kv_norm_g, w_ukv=v_w_ukv, w_out=v_w_out)

    g_in, g_uq, g_ukv, g_out = _gather_weights([_mx(w[k]) for k in BIG_NAMES])
    cols = lambda g: g.reshape((4, 2) + g.shape[1:]).transpose(1, 2, 0, 3).reshape(2, g.shape[1], 4 * g.shape[2])
    full_out = g_out.reshape(4, 2, 256, 1024).transpose(1, 0, 2, 3).reshape(2, 1024, 1024)
    loss, dx, grads = _local_step(x[0], positions[0], loss_target[0], cols(g_in), cols(g_uq), cols(g_ukv), full_out, w)

    split_cols = lambda g: g.reshape(2, g.shape[1], 4, g.shape[2] // 4).transpose(2, 0, 1, 3).reshape(8, g.shape[1], g.shape[2] // 4)
    parts = [split_cols(grads["w_in"]), split_cols(grads["w_uq"]), split_cols(grads["w_ukv"]),
             grads["w_out"].reshape(2, 4, 256, 1024).transpose(1, 0, 2, 3).reshape(8, 256, 1024)]
    received = _all_to_all(parts, _pack_small(grads))
    sums = [_sum8(received[a], f"sum_{BIG_NAMES[a]}", 128) for a in range(4)]
    small_sum = _unpack_small(_sum8(received[4], "sum_small", received[4].shape[1]), w)
    both = _sibling_exchange(sums)
    total = dict(small_sum)
    for a, k in enumerate(BIG_NAMES):
        total[k] = both[a]

    packed = _adamw(_pack_small(w), _pack_small(total), _pack_small(m), _pack_small(v), "adamw_small", 2048)
    small_out = [_unpack_small(pk, w) for pk in packed]
    delta, new_m, new_v = {}, {}, {}
    for k in SMALL_NAMES:
        delta[k], new_m[k], new_v[k] = (so[k] for so in small_out)
    for k in BIG_NAMES:
        shape = w[k].shape
        flat = lambda a: a.reshape(shape[0] * shape[1], shape[2])
        res = _adamw(flat(w[k]), flat(total[k]), flat(m[k]), flat(v[k]), f"adamw_{k}", 256)
        delta[k], new_m[k], new_v[k] = (r.reshape(shape) for r in res)

    loss = lax.psum(loss, ("x", "y", "c"))
    return (loss, dx[None], *[total[k] for k in WEIGHT_NAMES], *[delta[k] for k in WEIGHT_NAMES],
            *[new_m[k] for k in WEIGHT_NAMES], *[new_v[k] for k in WEIGHT_NAMES])
```

```python
import jax
import jax.numpy as jnp
from jax import lax
from jax.experimental import pallas as pl
from jax.experimental.pallas import tpu as pltpu

F32 = jnp.float32
MXU_DTYPE = jnp.bfloat16
EPS = 1e-6
NEG_INF = -1e30
DEPTH = 2
N_HEADS = 4
QK_PAD = 256
V_DIM = 128
SCALE = 192 ** -0.5
LOG2E = 1.4426950408889634
ROPE_BASE = 10000.0
ADAM_LR, ADAM_B1, ADAM_B2, ADAM_EPS, ADAM_WD, ADAM_STEP = 0.001, 0.9, 0.999, 1e-08, 0.01, 10
VMEM_LIMIT_BYTES = 56 * 1024 * 1024
MESH = pl.DeviceIdType.MESH
ANY = pl.BlockSpec(memory_space=pl.ANY)

Z_MIX, Z_C, Z_KR, Z_GATE = 1280, 640, 128, 512
Z_W = Z_MIX + Z_C + Z_KR + Z_GATE


def _cp(n_axes=1):
    return pltpu.CompilerParams(dimension_semantics=("arbitrary",) * n_axes, vmem_limit_bytes=VMEM_LIMIT_BYTES)


def _dot(a, b):
    return lax.dot_general(a, b, (((1,), (0,)), ((), ())), preferred_element_type=F32)


def _dot_nt(a, b):
    return lax.dot_general(a, b, (((1,), (1,)), ((), ())), preferred_element_type=F32)


def _dot_tn(a, b):
    return lax.dot_general(a, b, (((0,), (0,)), ((), ())), preferred_element_type=F32)


def _mx(a):
    return a.astype(MXU_DTYPE)


def _silu_and_grad(g):
    sg = jax.nn.sigmoid(g)
    return g * sg, sg * (1.0 + g * (1.0 - sg))


def _rms(x, g):
    r = lax.rsqrt(jnp.mean(x * x, axis=-1, keepdims=True) + EPS)
    return x * r * g, r


def _rms_bwd(x, r, g, dy):
    xhat = x * r
    dyg = dy * g
    dx = r * (dyg - xhat * jnp.mean(dyg * xhat, axis=-1, keepdims=True))
    return dx, dy * xhat


def _acc(ref, val, first):
    @pl.when(first)
    def _():
        ref[...] = val

    @pl.when(jnp.logical_not(first))
    def _():
        ref[...] += val


def _colsum(a):
    return jnp.sum(a, axis=0, keepdims=True)


def _in_proj_fwd(x, g, w, name, tm=512):
    t, d = x.shape
    n = w.shape[1]
    tm = min(tm, t)

    def body(x_ref, g_ref, w_ref, z_ref, h_ref):
        h, _ = _rms(x_ref[...], g_ref[...])
        h = _mx(h)
        h_ref[...] = h
        z_ref[...] = _dot(h, w_ref[...])

    return pl.pallas_call(
        body, name=name, grid=(t // tm,),
        in_specs=[pl.BlockSpec((tm, d), lambda i: (i, 0)), pl.BlockSpec((1, d), lambda i: (0, 0)),
                  pl.BlockSpec((d, n), lambda i: (0, 0))],
        out_specs=[pl.BlockSpec((tm, n), lambda i: (i, 0)), pl.BlockSpec((tm, d), lambda i: (i, 0))],
        out_shape=[jax.ShapeDtypeStruct((t, n), F32), jax.ShapeDtypeStruct((t, d), MXU_DTYPE)],
        compiler_params=_cp())(x, g, w)


def _in_proj_bwd(dz_mix, dz_c, dz_kr, dz_gate, h, x, d_res, w, g, name, tm=256):
    t, d = x.shape
    n = w.shape[1]
    tm = min(tm, t)

    def body(dm_ref, dc_ref, dk_ref, dg_ref, h_ref, x_ref, dres_ref, w_ref, g_ref, dx_ref, dw_ref, dgn_ref):
        first = pl.program_id(0) == 0
        dz = jnp.concatenate([dm_ref[...], dc_ref[...], dk_ref[...], dg_ref[...]], axis=1)

        @pl.when(first)
        def _():
            dw_ref[...] = jnp.zeros(dw_ref.shape, F32)

        hb = h_ref[...]
        for c0 in range(0, n, 512):
            dw_ref[:, c0:c0 + 512] += _dot_tn(hb, dz[:, c0:c0 + 512])
        dh = _dot_nt(dz, w_ref[...])
        xf = x_ref[...]
        r = lax.rsqrt(jnp.mean(xf * xf, axis=-1, keepdims=True) + EPS)
        dx, dgt = _rms_bwd(xf, r, g_ref[...], dh)
        dx_ref[...] = dx + dres_ref[...]
        _acc(dgn_ref, _colsum(dgt), first)

    row = lambda wd: pl.BlockSpec((tm, wd), lambda i: (i, 0))
    fixed = lambda a, b: pl.BlockSpec((a, b), lambda i: (0, 0))
    return pl.pallas_call(
        body, name=name, grid=(t // tm,),
        in_specs=[row(Z_MIX), row(Z_C), row(Z_KR), row(Z_GATE), row(d), row(d), row(d), fixed(d, n), fixed(1, d)],
        out_specs=[row(d), fixed(d, n), fixed(1, d)],
        out_shape=[jax.ShapeDtypeStruct((t, d), F32), jax.ShapeDtypeStruct((d, n), F32),
                   jax.ShapeDtypeStruct((1, d), F32)],
        compiler_params=_cp())(dz_mix, dz_c, dz_kr, dz_gate, h, x, d_res, w, g)


def _lane_group(shape):
    return lax.broadcasted_iota(jnp.int32, shape, 1) // 64


def _select_group(vals):
    grp = _lane_group(vals[0].shape)
    out = vals[3]
    for gi in (2, 1, 0):
        out = jnp.where(grp == gi, vals[gi], out)
    return out


def _sgu_mask(transposed):
    r = (lax.broadcasted_iota(jnp.int32, (512, 128), 0) % 128) // 64
    c = lax.broadcasted_iota(jnp.int32, (512, 128), 1) // 64
    return (r <= c) if transposed else (c <= r)


def _sgu_apply(wstack, vb, nblk):
    outs = []
    for n in range(nblk):
        r = _dot(wstack, vb[n * 128:(n + 1) * 128, :])
        outs.append(_select_group([r[hh * 128:(hh + 1) * 128, :] for hh in range(4)]))
    return jnp.concatenate(outs, axis=0)


def _layer_norm(v, g, b):
    mu = jnp.mean(v, axis=-1, keepdims=True)
    vc = v - mu
    rstd = lax.rsqrt(jnp.mean(vc * vc, axis=-1, keepdims=True) + EPS)
    vhat = vc * rstd
    return vhat * g + b, vhat, rstd


def _pool_counts(t0, n):
    t = t0 + lax.broadcasted_iota(jnp.int32, (n, 256), 0)
    w = _select_group([jnp.full((n, 256), wv, jnp.int32) for wv in (2, 4, 8, 16)])
    return jnp.minimum(t + 1, w).astype(F32)


def _pooled(p, halo, t0):
    tm = p.shape[0]
    ext = jnp.concatenate([halo, p], axis=0)
    s2 = ext + pltpu.roll(ext, 1, 0)
    s4 = s2 + pltpu.roll(s2, 2, 0)
    s8 = s4 + pltpu.roll(s4, 4, 0)
    s16 = s8 + pltpu.roll(s8, 8, 0)
    sel = _select_group([s2, s4, s8, s16])[16:, :]
    return sel / _pool_counts(t0, tm) - p


def _pooled_bwd(dpool, dpool_halo, t0):
    tm = dpool.shape[0]
    n = tm + 16
    ext = jnp.concatenate([dpool, dpool_halo], axis=0) / _pool_counts(t0, n)
    f2 = ext + pltpu.roll(ext, n - 1, 0)
    f4 = f2 + pltpu.roll(f2, n - 2, 0)
    f8 = f4 + pltpu.roll(f4, n - 4, 0)
    f16 = f8 + pltpu.roll(f8, n - 8, 0)
    return _select_group([f2, f4, f8, f16])[:tm, :] - dpool


def _mix_specs(t, tm):
    nt16 = t // 16
    zrow = pl.BlockSpec((tm, Z_MIX), lambda i: (i, 0))
    prev_halo = pl.BlockSpec((16, 256), lambda i: (jnp.maximum(i * (tm // 16) - 1, 0), 3))
    fixed = lambda a, b: pl.BlockSpec((a, b), lambda i: (0, 0))
    params = [fixed(512, 128), fixed(128, 256), fixed(1, 256), fixed(1, 256), fixed(256, 256), fixed(1, 256)]
    return nt16, zrow, prev_halo, fixed, params


def _mix_fwd(z, sgu_w, sgu_bias, ln_g, ln_b, pool_wbd, pool_scale, name, tm=256):
    t = z.shape[0]
    tm = min(tm, t)
    _, zrow, prev_halo, _, params = _mix_specs(t, tm)

    def body(z_ref, halo_ref, w_ref, bias_ref, lng_ref, lnb_ref, pw_ref, ps_ref, y_ref):
        i = pl.program_id(0)
        u, v, gate = z_ref[:, 0:256], z_ref[:, 256:512], z_ref[:, 512:768]
        p, pgate = z_ref[:, 768:1024], z_ref[:, 1024:1280]
        vn, _, _ = _layer_norm(v, lng_ref[...], lnb_ref[...])
        wm = _mx(jnp.where(_sgu_mask(False), w_ref[...], 0.0))
        mixed = _sgu_apply(wm, _mx(vn), tm // 128) + jnp.tile(bias_ref[...], (tm // 128, 1))
        ya = u * mixed * _silu_and_grad(gate)[0]
        halo = jnp.where(i > 0, halo_ref[...], 0.0)
        pooled = _pooled(p, halo, i * tm)
        yb = _dot(_mx(pooled), pw_ref[...]) * ps_ref[...] * _silu_and_grad(pgate)[0]
        y_ref[...] = _mx(jnp.concatenate([ya, yb], axis=1))

    return pl.pallas_call(
        body, name=name, grid=(t // tm,),
        in_specs=[zrow, prev_halo] + params,
        out_specs=pl.BlockSpec((tm, 512), lambda i: (i, 0)),
        out_shape=jax.ShapeDtypeStruct((t, 512), MXU_DTYPE),
        compiler_params=_cp())(z, z, sgu_w, sgu_bias, ln_g, ln_b, pool_wbd, pool_scale)


def _mix_bwd(z, dycat, sgu_w, sgu_wt, sgu_bias, ln_g, ln_b, pool_wbd, pool_scale, name, tm=256):
    t = z.shape[0]
    tm = min(tm, t)
    nt16, zrow, prev_halo, fixed, params = _mix_specs(t, tm)
    nblk = tm // 128
    last = t // tm - 1

    def body(z_ref, halo_ref, zn_ref, dy_ref, dyn_ref, w_ref, wt_ref, bias_ref, lng_ref, lnb_ref, pw_ref, ps_ref,
             dz_ref, dw_ref, db_ref, dlng_ref, dlnb_ref, dpw_ref, dps_ref):
        i = pl.program_id(0)
        first = i == 0
        u, v, gate = z_ref[:, 0:256], z_ref[:, 256:512], z_ref[:, 512:768]
        p, pgate = z_ref[:, 768:1024], z_ref[:, 1024:1280]
        dya, dyb = dy_ref[:, 0:256], dy_ref[:, 256:512]
        vn, vhat, rstd = _layer_norm(v, lng_ref[...], lnb_ref[...])
        vnb = _mx(vn)
        wm = _mx(jnp.where(_sgu_mask(False), w_ref[...], 0.0))
        wmt = _mx(jnp.where(_sgu_mask(True), wt_ref[...], 0.0))
        mixed = _sgu_apply(wm, vnb, nblk) + jnp.tile(bias_ref[...], (nblk, 1))
        silu, dsilu = _silu_and_grad(gate)
        t1 = u * mixed
        d_gate = dya * t1 * dsilu
        d_t1 = dya * silu
        d_u = d_t1 * mixed
        d_mixed = d_t1 * u
        dmb = _mx(d_mixed)
        d_vn = _sgu_apply(wmt, dmb, nblk)
        grp = _lane_group((128, 256))
        lane = lax.broadcasted_iota(jnp.int32, (128, 128), 1)
        dws = [jnp.zeros((128, 128), F32) for _ in range(4)]
        dbias = jnp.zeros((128, 128), F32)
        for n in range(nblk):
            dm_n, dmb_n, vnb_n = d_mixed[n * 128:(n + 1) * 128], dmb[n * 128:(n + 1) * 128], vnb[n * 128:(n + 1) * 128]
            for hh in range(4):
                dws[hh] = dws[hh] + _dot_nt(jnp.where(grp == hh, dmb_n, jnp.zeros_like(dmb_n)), vnb_n)
                rs = jnp.sum(jnp.where(grp == hh, dm_n, 0.0), axis=-1, keepdims=True)
                dbias = dbias + jnp.where(lane == hh, rs, 0.0)
        _acc(dw_ref, jnp.concatenate(dws, axis=0), first)
        _acc(db_ref, dbias, first)
        _acc(dlng_ref, _colsum(d_vn * vhat), first)
        _acc(dlnb_ref, _colsum(d_vn), first)
        dvh = d_vn * lng_ref[...]
        d_v = rstd * (dvh - jnp.mean(dvh, axis=-1, keepdims=True) - vhat * jnp.mean(dvh * vhat, axis=-1, keepdims=True))

        @pl.when(i == last)
        def _():
            dw_ref[...] = jnp.where(_sgu_mask(False), dw_ref[...], 0.0)

        halo = jnp.where(i > 0, halo_ref[...], 0.0)
        pooled = _pooled(p, halo, i * tm)
        pooled_b = _mx(pooled)
        mixedp = _dot(pooled_b, pw_ref[...])
        psilu, pdsilu = _silu_and_grad(pgate)
        d_pgate = dyb * (mixedp * ps_ref[...]) * pdsilu
        d_ms = dyb * psilu
        _acc(dps_ref, _colsum(d_ms * mixedp), first)
        dmpb = _mx(d_ms * ps_ref[...])
        _acc(dpw_ref, _dot_tn(pooled_b, dmpb), first)
        d_pooled = _dot_nt(dmpb, pw_ref[...])
        dmp_halo = _mx(dyn_ref[...] * _silu_and_grad(zn_ref[...])[0] * ps_ref[...])
        d_pooled_halo = jnp.where(i < last, _dot_nt(dmp_halo, pw_ref[...]), 0.0)
        d_p = _pooled_bwd(d_pooled, d_pooled_halo, i * tm)
        dz_ref[...] = _mx(jnp.concatenate([d_u, d_v, d_gate, d_p, d_pgate], axis=1))

    nxt = lambda i: jnp.minimum((i + 1) * (tm // 16), nt16 - 1)
    return pl.pallas_call(
        body, name=name, grid=(t // tm,),
        in_specs=[zrow, prev_halo, pl.BlockSpec((16, 256), lambda i: (nxt(i), 4)),
                  pl.BlockSpec((tm, 512), lambda i: (i, 0)), pl.BlockSpec((16, 256), lambda i: (nxt(i), 1)),
                  params[0], fixed(512, 128)] + params[1:],
        out_specs=[pl.BlockSpec((tm, Z_MIX), lambda i: (i, 0)), fixed(512, 128), fixed(128, 128), fixed(1, 256),
                   fixed(1, 256), fixed(256, 256), fixed(1, 256)],
        out_shape=[jax.ShapeDtypeStruct((t, Z_MIX), MXU_DTYPE), jax.ShapeDtypeStruct((512, 128), F32),
                   jax.ShapeDtypeStruct((128, 128), F32), jax.ShapeDtypeStruct((1, 256), F32),
                   jax.ShapeDtypeStruct((1, 256), F32), jax.ShapeDtypeStruct((256, 256), F32),
                   jax.ShapeDtypeStruct((1, 256), F32)],
        compiler_params=_cp())(z, z, z, dycat, dycat, sgu_w, sgu_wt, sgu_bias, ln_g, ln_b, pool_wbd, pool_scale)


def _rot_half(x, transpose):
    w = x.shape[1]
    lane = lax.broadcasted_iota(jnp.int32, x.shape, 1) % min(w, 256)
    base = 128 if w >= 256 else 0
    lo = jnp.logical_and(lane >= base, lane < base + 32)
    hi = jnp.logical_and(lane >= base + 32, lane < base + 64)
    up = pltpu.roll(x, w - 32, 1)
    down = pltpu.roll(x, 32, 1)
    if transpose:
        return jnp.where(lo, up, jnp.where(hi, -down, 0.0))
    return jnp.where(lo, -up, jnp.where(hi, down, 0.0))


def _rope(x, c, s):
    return x * c + _rot_half(x, False) * s


def _rope_bwd(dy, c, s):
    return dy * c + _rot_half(dy * s, True)


def _qkv_fwd(z, rc, rs, w_uq, w_ukv, gq, gkv, name, tm=256):
    t = z.shape[0]
    tm = min(tm, t)

    def body(zc_ref, zk_ref, rc_ref, rs_ref, wq_ref, wkv_ref, gq_ref, gkv_ref, q_ref, k_ref, v_ref):
        cq, ckv = zc_ref[:, 0:384], zc_ref[:, 384:640]
        c, s = rc_ref[...], rs_ref[...]
        qn, _ = _rms(cq, gq_ref[...])
        q = _rope(_dot(_mx(qn), wq_ref[...]), jnp.tile(c, (1, N_HEADS)), jnp.tile(s, (1, N_HEADS)))
        kvn, _ = _rms(ckv, gkv_ref[...])
        kv = _dot(_mx(kvn), wkv_ref[...])
        kpe = _rope(zk_ref[...], c[:, 128:256], s[:, 128:256])
        for hh in range(N_HEADS):
            q_ref[hh] = _mx(q[:, hh * QK_PAD:(hh + 1) * QK_PAD])
            k_ref[hh] = _mx(jnp.concatenate([kv[:, hh * 128:(hh + 1) * 128], kpe], axis=1))
            v_ref[hh] = _mx(kv[:, 512 + hh * 128:512 + (hh + 1) * 128])

    fixed = lambda a, b: pl.BlockSpec((a, b), lambda i: (0, 0))
    heads = lambda wd: pl.BlockSpec((N_HEADS, tm, wd), lambda i: (0, i, 0))
    return pl.pallas_call(
        body, name=name, grid=(t // tm,),
        in_specs=[pl.BlockSpec((tm, Z_C), lambda i: (i, Z_MIX // Z_C)),
                  pl.BlockSpec((tm, Z_KR), lambda i: (i, (Z_MIX + Z_C) // Z_KR)),
                  pl.BlockSpec((tm, 256), lambda i: (i, 0)), pl.BlockSpec((tm, 256), lambda i: (i, 0)),
                  fixed(384, 1024), fixed(256, 1024), fixed(1, 384), fixed(1, 256)],
        out_specs=[heads(QK_PAD), heads(QK_PAD), heads(V_DIM)],
        out_shape=[jax.ShapeDtypeStruct((N_HEADS, t, QK_PAD), MXU_DTYPE),
                   jax.ShapeDtypeStruct((N_HEADS, t, QK_PAD), MXU_DTYPE),
                   jax.ShapeDtypeStruct((N_HEADS, t, V_DIM), MXU_DTYPE)],
        compiler_params=_cp())(z, z, rc, rs, w_uq, w_ukv, gq, gkv)


def _qkv_bwd(dq, dk, dv, z, rc, rs, w_uq, w_ukv, gq, gkv, name, tm=256):
    t = z.shape[0]
    tm = min(tm, t)

    def body(dq_ref, dk_ref, dv_ref, zc_ref, rc_ref, rs_ref, wq_ref, wkv_ref, gq_ref, gkv_ref,
             dzc_ref, dzk_ref, dwq_ref, dwkv_ref, dgq_ref, dgkv_ref):
        first = pl.program_id(0) == 0
        cq, ckv = zc_ref[:, 0:384], zc_ref[:, 384:640]
        c, s = rc_ref[...], rs_ref[...]
        dq_all = jnp.concatenate([dq_ref[hh] for hh in range(N_HEADS)], axis=1)
        dqp = _mx(_rope_bwd(dq_all, jnp.tile(c, (1, N_HEADS)), jnp.tile(s, (1, N_HEADS))))
        qn, rq = _rms(cq, gq_ref[...])
        _acc(dwq_ref, _dot_tn(_mx(qn), dqp), first)
        d_cq, dgq_t = _rms_bwd(cq, rq, gq_ref[...], _dot_nt(dqp, wq_ref[...]))
        _acc(dgq_ref, _colsum(dgq_t), first)
        dkpe = dk_ref[0][:, 128:256]
        for hh in range(1, N_HEADS):
            dkpe = dkpe + dk_ref[hh][:, 128:256]
        dzk_ref[...] = _mx(_rope_bwd(dkpe, c[:, 128:256], s[:, 128:256]))
        dkv = _mx(jnp.concatenate([dk_ref[hh][:, 0:128] for hh in range(N_HEADS)]
                                  + [dv_ref[hh] for hh in range(N_HEADS)], axis=1))
        kvn, rkv = _rms(ckv, gkv_ref[...])
        _acc(dwkv_ref, _dot_tn(_mx(kvn), dkv), first)
        d_ckv, dgkv_t = _rms_bwd(ckv, rkv, gkv_ref[...], _dot_nt(dkv, wkv_ref[...]))
        _acc(dgkv_ref, _colsum(dgkv_t), first)
        dzc_ref[...] = _mx(jnp.concatenate([d_cq, d_ckv], axis=1))

    fixed = lambda a, b: pl.BlockSpec((a, b), lambda i: (0, 0))
    heads = lambda wd: pl.BlockSpec((N_HEADS, tm, wd), lambda i: (0, i, 0))
    return pl.pallas_call(
        body, name=name, grid=(t // tm,),
        in_specs=[heads(QK_PAD), heads(QK_PAD), heads(V_DIM), pl.BlockSpec((tm, Z_C), lambda i: (i, Z_MIX // Z_C)),
                  pl.BlockSpec((tm, 256), lambda i: (i, 0)), pl.BlockSpec((tm, 256), lambda i: (i, 0)),
                  fixed(384, 1024), fixed(256, 1024), fixed(1, 384), fixed(1, 256)],
        out_specs=[pl.BlockSpec((tm, Z_C), lambda i: (i, 0)), pl.BlockSpec((tm, Z_KR), lambda i: (i, 0)),
                   fixed(384, 1024), fixed(256, 1024), fixed(1, 384), fixed(1, 256)],
        out_shape=[jax.ShapeDtypeStruct((t, Z_C), MXU_DTYPE), jax.ShapeDtypeStruct((t, Z_KR), MXU_DTYPE),
                   jax.ShapeDtypeStruct((384, 1024), F32), jax.ShapeDtypeStruct((256, 1024), F32),
                   jax.ShapeDtypeStruct((1, 384), F32), jax.ShapeDtypeStruct((1, 256), F32)],
        compiler_params=_cp())(dq, dk, dv, z, rc, rs, w_uq, w_ukv, gq, gkv)


def _chunk_mask(tq):
    r = lax.broadcasted_iota(jnp.int32, (tq, tq), 0) // 64
    c = lax.broadcasted_iota(jnp.int32, (tq, tq), 1) // 64
    return c <= r


def _gate_block(tq):
    return pl.BlockSpec((tq, 128), lambda h, i: (i, (Z_MIX + Z_C + Z_KR) // 128 + h))


def _attn_fwd(qh, kh, vh, z, name, tq=512):
    t = qh.shape[1]
    tq = min(tq, t)

    def body(q_ref, g_ref, k_hbm, v_hbm, o_ref, yc_ref, lse_ref, k_v, v_v, m_s, acc_s, s_a, s_b, sem):
        h, i = pl.program_id(0), pl.program_id(1)

        @pl.when(i == 0)
        def _():
            ck = pltpu.make_async_copy(k_hbm.at[h], k_v, sem.at[0])
            cv = pltpu.make_async_copy(v_hbm.at[h], v_v.at[:, 0:V_DIM], sem.at[1])
            ck.start()
            cv.start()
            v_v[:, V_DIM:2 * V_DIM] = jnp.ones((t, V_DIM), MXU_DTYPE)
            ck.wait()
            cv.wait()

        q = q_ref[...]
        m_s[...] = jnp.full(m_s.shape, NEG_INF, F32)
        acc_s[...] = jnp.zeros(acc_s.shape, F32)

        def keys(j):
            return pl.ds(pl.multiple_of(j * tq, tq), tq)

        def scores(s_ref, j):
            s_ref[...] = _dot_nt(q, k_v[keys(j), :]) * (SCALE * LOG2E)

        def softmax_pv(s_ref, j, masked):
            s = s_ref[...]
            if masked:
                s = jnp.where(_chunk_mask(tq), s, NEG_INF)
            m_old = m_s[...]
            m_new = jnp.maximum(m_old, jnp.max(s, axis=-1, keepdims=True))
            p = jnp.exp2(s - jnp.tile(m_new, (1, tq // 128)))
            alpha = jnp.exp2(m_old - m_new)
            m_s[...] = m_new
            acc_s[...] = jnp.tile(alpha, (1, 2)) * acc_s[...] + _dot(_mx(p), v_v[keys(j), :])

        scores(s_a, 0)

        def pair(pp, carry):
            scores(s_b, 2 * pp + 1)
            softmax_pv(s_a, 2 * pp, False)
            scores(s_a, 2 * pp + 2)
            softmax_pv(s_b, 2 * pp + 1, False)
            return carry

        lax.fori_loop(0, i // 2, pair, 0)

        @pl.when(i % 2 == 1)
        def _():
            scores(s_b, i)
            softmax_pv(s_a, i - 1, False)
            softmax_pv(s_b, i, True)

        @pl.when(i % 2 == 0)
        def _():
            softmax_pv(s_a, i, True)

        l = acc_s[:, V_DIM:2 * V_DIM]
        o = acc_s[:, 0:V_DIM] / l
        o_ref[...] = o
        yc_ref[...] = _mx(o * _silu_and_grad(g_ref[...])[0])
        lse_ref[...] = m_s[...] + jnp.log2(l)

    return pl.pallas_call(
        body, name=name, grid=(N_HEADS, t // tq),
        in_specs=[pl.BlockSpec((None, tq, QK_PAD), lambda h, i: (h, i, 0)), _gate_block(tq), ANY, ANY],
        out_specs=[pl.BlockSpec((tq, 128), lambda h, i: (i, h)), pl.BlockSpec((tq, 128), lambda h, i: (i, h)),
                   pl.BlockSpec((None, tq, 128), lambda h, i: (h, i, 0))],
        out_shape=[jax.ShapeDtypeStruct((t, N_HEADS * V_DIM), F32), jax.ShapeDtypeStruct((t, N_HEADS * V_DIM), MXU_DTYPE),
                   jax.ShapeDtypeStruct((N_HEADS, t, 128), F32)],
        scratch_shapes=[pltpu.VMEM((t, QK_PAD), MXU_DTYPE), pltpu.VMEM((t, 2 * V_DIM), MXU_DTYPE),
                        pltpu.VMEM((tq, 128), F32), pltpu.VMEM((tq, 2 * V_DIM), F32),
                        pltpu.VMEM((tq, tq), F32), pltpu.VMEM((tq, tq), F32), pltpu.SemaphoreType.DMA((2,))],
        compiler_params=_cp(2))(qh, z, kh, vh)


def _attn_bwd(qh, kh, vh, o, lse, dycat, z, name, tq=512):
    t = qh.shape[1]
    tq = min(tq, t)
    nq = t // tq

    def body(q_ref, o_ref, lse_ref, dy_ref, g_ref, k_hbm, v_hbm, dq_ref, dgate_ref, dk_hbm, dv_hbm,
             k_v, v_v, dk_acc, dv_acc, dq_acc, sem):
        h, i = pl.program_id(0), pl.program_id(1)

        @pl.when(i == 0)
        def _():
            ck = pltpu.make_async_copy(k_hbm.at[h], k_v, sem.at[0])
            cv = pltpu.make_async_copy(v_hbm.at[h], v_v, sem.at[1])
            ck.start()
            cv.start()
            dk_acc[...] = jnp.zeros(dk_acc.shape, F32)
            dv_acc[...] = jnp.zeros(dv_acc.shape, F32)
            ck.wait()
            cv.wait()

        gate, dy, of = g_ref[...], dy_ref[...], o_ref[...]
        silu, dsilu = _silu_and_grad(gate)
        do = dy * silu
        delta = jnp.sum(do * of, axis=-1, keepdims=True)
        dgate_ref[...] = _mx(dy * of * dsilu)
        dob = _mx(do)
        q = q_ref[...]
        lse_col = lse_ref[:, 0:1]
        dq_acc[...] = jnp.zeros(dq_acc.shape, F32)

        def block(j, masked):
            ks = pl.ds(pl.multiple_of(j * tq, tq), tq)
            k, v = k_v[ks, :], v_v[ks, :]
            p = jnp.exp2(_dot_nt(q, k) * (SCALE * LOG2E) - lse_col)
            if masked:
                p = jnp.where(_chunk_mask(tq), p, 0.0)
            ds = p * (_dot_nt(dob, v) - delta) * SCALE
            pb, dsb = _mx(p), _mx(ds)
            dq_acc[...] += _dot(dsb, k)
            dk_acc[ks, :] += _dot_tn(dsb, q)
            dv_acc[ks, :] += _dot_tn(pb, dob)

        def loop_body(j, carry):
            block(j, False)
            return carry

        lax.fori_loop(0, i, loop_body, 0)
        block(i, True)
        dq_ref[...] = dq_acc[...]

        @pl.when(i == nq - 1)
        def _():
            ck = pltpu.make_async_copy(dk_acc, dk_hbm.at[h], sem.at[0])
            cv = pltpu.make_async_copy(dv_acc, dv_hbm.at[h], sem.at[1])
            ck.start()
            cv.start()
            ck.wait()
            cv.wait()

    return pl.pallas_call(
        body, name=name, grid=(N_HEADS, nq),
        in_specs=[pl.BlockSpec((None, tq, QK_PAD), lambda h, i: (h, i, 0)),
                  pl.BlockSpec((tq, 128), lambda h, i: (i, h)),
                  pl.BlockSpec((None, tq, 128), lambda h, i: (h, i, 0)),
                  pl.BlockSpec((tq, 128), lambda h, i: (i, N_HEADS + h)), _gate_block(tq), ANY, ANY],
        out_specs=[pl.BlockSpec((None, tq, QK_PAD), lambda h, i: (h, i, 0)),
                   pl.BlockSpec((tq, 128), lambda h, i: (i, h)), ANY, ANY],
        out_shape=[jax.ShapeDtypeStruct((N_HEADS, t, QK_PAD), F32), jax.ShapeDtypeStruct((t, Z_GATE), MXU_DTYPE),
                   jax.ShapeDtypeStruct((N_HEADS, t, QK_PAD), F32), jax.ShapeDtypeStruct((N_HEADS, t, V_DIM), F32)],
        scratch_shapes=[pltpu.VMEM((t, QK_PAD), MXU_DTYPE), pltpu.VMEM((t, V_DIM), MXU_DTYPE),
                        pltpu.VMEM((t, QK_PAD), F32), pltpu.VMEM((t, V_DIM), F32), pltpu.VMEM((tq, QK_PAD), F32),
                        pltpu.SemaphoreType.DMA((2,))],
        compiler_params=_cp(2))(qh, o, lse, dycat, z, kh, vh)


def _out_proj_fwd(yab, yc, w, x, g, target, name, tm=512):
    t, d = x.shape
    tm = min(tm, t)
    is_last = target is not None

    def body(*refs):
        if is_last:
            yab_ref, yc_ref, w_ref, x_ref, g_ref, t_ref, y_ref, dout_ref, loss_ref = refs
        else:
            yab_ref, yc_ref, w_ref, x_ref, g_ref, y_ref, out_ref = refs
        y = _dot(jnp.concatenate([yab_ref[...], yc_ref[...]], axis=1), w_ref[...])
        y_ref[...] = y
        out = x_ref[...] + _rms(y, g_ref[...])[0]
        if is_last:
            diff = out - t_ref[...]
            dout_ref[...] = diff * (1.0 / d)
            part = jnp.sum(jnp.sum(diff * diff, axis=-1, keepdims=True), axis=0, keepdims=True) * (0.5 / d)
            _acc(loss_ref, jnp.broadcast_to(part, (1, 128)), pl.program_id(0) == 0)
        else:
            out_ref[...] = out

    row = lambda wd: pl.BlockSpec((tm, wd), lambda i: (i, 0))
    fixed = lambda a, b: pl.BlockSpec((a, b), lambda i: (0, 0))
    in_specs = [row(512), row(512), fixed(d, d), row(d), fixed(1, d)]
    args = [yab, yc, w, x, g]
    out_specs = [row(d), row(d)]
    out_shape = [jax.ShapeDtypeStruct((t, d), F32), jax.ShapeDtypeStruct((t, d), F32)]
    if is_last:
        in_specs.append(row(d))
        args.append(target)
        out_specs.append(fixed(1, 128))
        out_shape.append(jax.ShapeDtypeStruct((1, 128), F32))
    return pl.pallas_call(body, name=name, grid=(t // tm,), in_specs=in_specs, out_specs=out_specs,
                          out_shape=out_shape, compiler_params=_cp())(*args)


def _out_proj_bwd(dout, y, yab, yc, w, g, name, tm=512):
    t, d = y.shape
    tm = min(tm, t)

    def body(dout_ref, y_ref, yab_ref, yc_ref, w_ref, g_ref, dycat_ref, dw_ref, dg_ref):
        first = pl.program_id(0) == 0
        y = y_ref[...]
        r = lax.rsqrt(jnp.mean(y * y, axis=-1, keepdims=True) + EPS)
        dy, dgt = _rms_bwd(y, r, g_ref[...], dout_ref[...])
        _acc(dg_ref, _colsum(dgt), first)
        dyb = _mx(dy)
        _acc(dw_ref, _dot_tn(jnp.concatenate([yab_ref[...], yc_ref[...]], axis=1), dyb), first)
        dycat_ref[...] = _dot_nt(dyb, w_ref[...])

    row = lambda wd: pl.BlockSpec((tm, wd), lambda i: (i, 0))
    fixed = lambda a, b: pl.BlockSpec((a, b), lambda i: (0, 0))
    return pl.pallas_call(
        body, name=name, grid=(t // tm,),
        in_specs=[row(d), row(d), row(512), row(512), fixed(d, d), fixed(1, d)],
        out_specs=[row(d), fixed(d, d), fixed(1, d)],
        out_shape=[jax.ShapeDtypeStruct((t, d), F32), jax.ShapeDtypeStruct((d, d), F32),
                   jax.ShapeDtypeStruct((1, d), F32)],
        compiler_params=_cp())(dout, y, yab, yc, w, g)


def _mesh_pos():
    return lax.axis_index("x"), lax.axis_index("y"), lax.axis_index("c")


def _remote(src, dst, send_sem, recv_sem, to):
    return pltpu.make_async_remote_copy(src_ref=src, dst_ref=dst, send_sem=send_sem, recv_sem=recv_sem,
                                        device_id=to, device_id_type=MESH)


def _gather_weights(shards):
    n = len(shards)

    def body(*refs):
        ins, outs = refs[:n], refs[n:2 * n]
        send_sems, recv_sems, local_sems = refs[2 * n:]
        x, y, c = _mesh_pos()
        me, sibling = (x, y, c), (x, y, 1 - c)
        chips = [(1 - x, y), (x, 1 - y), (1 - x, 1 - y)]
        slot = lambda cx, cy, layer: 2 * (2 * cx + cy) + layer
        local = []
        for a in range(n):
            for layer in range(2):
                local.append(pltpu.make_async_copy(ins[a].at[layer], outs[a].at[slot(x, y, layer)],
                                                   local_sems.at[a, layer]))
                local[-1].start()
        first = []
        for a in range(n):
            for j, (cx, cy) in enumerate(chips):
                first.append(_remote(ins[a].at[c], outs[a].at[slot(x, y, c)], send_sems.at[a, j], recv_sems.at[a, j],
                                     (cx, cy, c)))
                first[-1].start()
        passed = []
        for a in range(n):
            for j, (cx, cy) in enumerate(chips):
                blk = outs[a].at[slot(cx, cy, c)]
                _remote(blk, blk, send_sems.at[a, j], recv_sems.at[a, j], me).wait_recv()
                passed.append(_remote(blk, blk, send_sems.at[a, 3 + j], recv_sems.at[a, 3 + j], sibling))
                passed[-1].start()
        for a in range(n):
            for j, (cx, cy) in enumerate(chips):
                blk = outs[a].at[slot(cx, cy, 1 - c)]
                _remote(blk, blk, send_sems.at[a, 3 + j], recv_sems.at[a, 3 + j], me).wait_recv()
        for cp in first + passed:
            cp.wait_send()
        for cp in local:
            cp.wait()

    return pl.pallas_call(
        body, name="gather_weights", in_specs=[ANY] * n, out_specs=[ANY] * n,
        out_shape=[jax.ShapeDtypeStruct((8,) + s.shape[1:], s.dtype) for s in shards],
        scratch_shapes=[pltpu.SemaphoreType.DMA((n, 6)), pltpu.SemaphoreType.DMA((n, 6)),
                        pltpu.SemaphoreType.DMA((n, 2))])(*shards)


def _all_to_all(parts, common):
    n = len(parts)

    def body(*refs):
        ins, outs = refs[:n + 1], refs[n + 1:2 * n + 2]
        send_sems, recv_sems, local_sems = refs[2 * n + 2:]
        x, y, c = _mesh_pos()
        me = 4 * x + 2 * y + c
        src = lambda a, j: ins[a].at[j] if a < n else ins[a]
        local = []
        for a in range(n + 1):
            local.append(pltpu.make_async_copy(src(a, me), outs[a].at[me], local_sems.at[a]))
            local[-1].start()
        sent = []
        for rel in range(1, 8):
            px = 1 - x if rel & 4 else x
            py = 1 - y if rel & 2 else y
            pc = 1 - c if rel & 1 else c
            peer = 4 * px + 2 * py + pc
            for a in range(n + 1):
                sent.append(_remote(src(a, peer), outs[a].at[me], send_sems.at[a, rel], recv_sems.at[a, rel],
                                    (px, py, pc)))
                sent[-1].start()
        for rel in range(1, 8):
            px = 1 - x if rel & 4 else x
            py = 1 - y if rel & 2 else y
            pc = 1 - c if rel & 1 else c
            peer = 4 * px + 2 * py + pc
            for a in range(n + 1):
                _remote(src(a, peer), outs[a].at[peer], send_sems.at[a, rel], recv_sems.at[a, rel],
                        (x, y, c)).wait_recv()
        for cp in sent:
            cp.wait_send()
        for cp in local:
            cp.wait()

    return pl.pallas_call(
        body, name="grad_all_to_all", in_specs=[ANY] * (n + 1), out_specs=[ANY] * (n + 1),
        out_shape=[jax.ShapeDtypeStruct(p.shape, p.dtype) for p in parts]
        + [jax.ShapeDtypeStruct((8,) + common.shape, common.dtype)],
        scratch_shapes=[pltpu.SemaphoreType.DMA((n + 1, 8)), pltpu.SemaphoreType.DMA((n + 1, 8)),
                        pltpu.SemaphoreType.DMA((n + 1,))])(*parts, common)


def _sibling_exchange(sums):
    n = len(sums)

    def body(*refs):
        ins, outs = refs[:n], refs[n:2 * n]
        send_sems, recv_sems, local_sems = refs[2 * n:]
        x, y, c = _mesh_pos()
        copies = []
        for a in range(n):
            copies.append(pltpu.make_async_copy(ins[a], outs[a].at[c], local_sems.at[a]))
            copies[-1].start()
        sent = []
        for a in range(n):
            sent.append(_remote(ins[a], outs[a].at[c], send_sems.at[a], recv_sems.at[a], (x, y, 1 - c)))
            sent[-1].start()
        for a in range(n):
            _remote(ins[a], outs[a].at[1 - c], send_sems.at[a], recv_sems.at[a], (x, y, c)).wait_recv()
        for cp in sent:
            cp.wait_send()
        for cp in copies:
            cp.wait()

    return pl.pallas_call(
        body, name="sibling_exchange", in_specs=[ANY] * n, out_specs=[ANY] * n,
        out_shape=[jax.ShapeDtypeStruct((2,) + s.shape, s.dtype) for s in sums],
        scratch_shapes=[pltpu.SemaphoreType.DMA((n,)), pltpu.SemaphoreType.DMA((n,)),
                        pltpu.SemaphoreType.DMA((n,))])(*sums)


def _sum8(parts, name, tr):
    _, r, c = parts.shape
    tr = min(tr, r)

    def body(p_ref, o_ref):
        acc = p_ref[0]
        for j in range(1, 8):
            acc = acc + p_ref[j]
        o_ref[...] = acc

    return pl.pallas_call(
        body, name=name, grid=(r // tr,),
        in_specs=[pl.BlockSpec((8, tr, c), lambda i: (0, i, 0))], out_specs=pl.BlockSpec((tr, c), lambda i: (i, 0)),
        out_shape=jax.ShapeDtypeStruct((r, c), parts.dtype), compiler_params=_cp())(parts)


def _adamw(w, g, m, v, name, tr):
    r, c = w.shape
    tr = min(tr, r)

    def body(w_ref, g_ref, m_ref, v_ref, d_ref, nm_ref, nv_ref):
        gg = g_ref[...]
        nm = ADAM_B1 * m_ref[...] + (1.0 - ADAM_B1) * gg
        nv = ADAM_B2 * v_ref[...] + (1.0 - ADAM_B2) * jnp.square(gg)
        m_hat = nm / (1.0 - ADAM_B1 ** ADAM_STEP)
        v_hat = nv / (1.0 - ADAM_B2 ** ADAM_STEP)
        d_ref[...] = -ADAM_LR * (m_hat / (jnp.sqrt(v_hat) + ADAM_EPS) + ADAM_WD * w_ref[...])
        nm_ref[...] = nm
        nv_ref[...] = nv

    blk = pl.BlockSpec((tr, c), lambda i: (i, 0))
    return pl.pallas_call(
        body, name=name, grid=(r // tr,), in_specs=[blk] * 4, out_specs=[blk] * 3,
        out_shape=[jax.ShapeDtypeStruct((r, c), F32)] * 3, compiler_params=_cp())(w, g, m, v)


def _rope_tables(positions):
    inv_freq = ROPE_BASE ** (-jnp.arange(0, 64, 2, dtype=F32) / 64)
    ang = positions.astype(F32)[:, None] * inv_freq
    cos, sin = jnp.cos(ang), jnp.sin(ang)
    t = positions.shape[0]
    rc = jnp.concatenate([jnp.ones((t, 128), F32), cos, cos, jnp.ones((t, 64), F32)], axis=1)
    rs = jnp.concatenate([jnp.zeros((t, 128), F32), sin, sin, jnp.zeros((t, 64), F32)], axis=1)
    return rc, rs


def _layer_params(l, w_in, w_uq, w_ukv, w_out, small):
    p = {}
    p["w_in"] = jnp.concatenate([w_in[l][:, :1984], jnp.zeros((1024, 64), w_in.dtype), w_in[l][:, 1984:]], axis=1)
    p["w_uq"] = jnp.pad(w_uq[l].reshape(384, 4, 192), ((0, 0), (0, 0), (0, 64))).reshape(384, 1024)
    p["w_ukv"] = w_ukv[l].reshape(256, 4, 2, 128).transpose(0, 2, 1, 3).reshape(256, 1024)
    p["w_out"] = w_out[l]
    p["pre_g"] = small["pre_norm_g"][l][None]
    p["post_g"] = small["post_norm_g"][l][None]
    p["sgu_w"] = small["sgu_w"][l].reshape(512, 128)
    p["sgu_wt"] = small["sgu_w"][l].transpose(0, 2, 1).reshape(512, 128)
    p["sgu_bias"] = jnp.repeat(small["sgu_b"][l].T, 64, axis=1)
    p["ln_g"] = small["sgu_ln_g"][l][None]
    p["ln_b"] = small["sgu_ln_b"][l][None]
    p["pool_wbd"] = _mx(jax.scipy.linalg.block_diag(*[small["pool_w"][l][gi] for gi in range(4)]))
    p["pool_scale"] = small["pool_scale"][l][None]
    p["gq"] = small["q_norm_g"][l][None]
    p["gkv"] = small["kv_norm_g"][l][None]
    return p


def _layer_fwd(l, x, p, rc, rs, target):
    z, h = _in_proj_fwd(x, p["pre_g"], p["w_in"], f"in_proj_fwd_{l}")
    yab = _mix_fwd(z, p["sgu_w"], p["sgu_bias"], p["ln_g"], p["ln_b"], p["pool_wbd"], p["pool_scale"], f"mix_fwd_{l}")
    qh, kh, vh = _qkv_fwd(z, rc, rs, p["w_uq"], p["w_ukv"], p["gq"], p["gkv"], f"qkv_fwd_{l}")
    o, yc, lse = _attn_fwd(qh, kh, vh, z, f"attn_fwd_{l}")
    outs = _out_proj_fwd(yab, yc, p["w_out"], x, p["post_g"], target, f"out_proj_fwd_{l}")
    saved = dict(x=x, z=z, h=h, yab=yab, qh=qh, kh=kh, vh=vh, o=o, yc=yc, lse=lse, y=outs[0])
    return saved, outs[1:]


def _layer_bwd(l, dout, sv, p, rc, rs):
    dycat, dw_out, dpost = _out_proj_bwd(dout, sv["y"], sv["yab"], sv["yc"], p["w_out"], p["post_g"], f"out_proj_bwd_{l}")
    dq, dgate, dk, dv = _attn_bwd(sv["qh"], sv["kh"], sv["vh"], sv["o"], sv["lse"], dycat, sv["z"], f"attn_bwd_{l}")
    dzc, dzk, dwq, dwkv, dgq, dgkv = _qkv_bwd(dq, dk, dv, sv["z"], rc, rs, p["w_uq"], p["w_ukv"], p["gq"], p["gkv"],
                                              f"qkv_bwd_{l}")
    dzm, dsw, dsb, dlng, dlnb, dpw, dps = _mix_bwd(sv["z"], dycat, p["sgu_w"], p["sgu_wt"], p["sgu_bias"], p["ln_g"],
                                                   p["ln_b"], p["pool_wbd"], p["pool_scale"], f"mix_bwd_{l}")
    dx, dw_in, dpre = _in_proj_bwd(dzm, dzc, dzk, dgate, sv["h"], sv["x"], dout, p["w_in"], p["pre_g"], f"in_proj_bwd_{l}")
    grads = {
        "pre_norm_g": dpre[0], "post_norm_g": dpost[0],
        "w_in": jnp.concatenate([dw_in[:, :1984], dw_in[:, 2048:]], axis=1),
        "sgu_w": dsw.reshape(4, 128, 128), "sgu_b": dsb[:, :4].T, "sgu_ln_g": dlng[0], "sgu_ln_b": dlnb[0],
        "pool_w": jnp.stack([dpw[64 * gi:64 * gi + 64, 64 * gi:64 * gi + 64] for gi in range(4)]),
        "pool_scale": dps[0], "q_norm_g": dgq[0],
        "w_uq": dwq.reshape(384, 4, 256)[:, :, :192].reshape(384, 768), "kv_norm_g": dgkv[0],
        "w_ukv": dwkv.reshape(256, 2, 4, 128).transpose(0, 2, 1, 3).reshape(256, 1024), "w_out": dw_out,
    }
    return dx, grads


SMALL_NAMES = ["pre_norm_g", "post_norm_g", "sgu_w", "sgu_b", "sgu_ln_g", "sgu_ln_b", "pool_w", "pool_scale",
               "q_norm_g", "kv_norm_g"]
BIG_NAMES = ["w_in", "w_uq", "w_ukv", "w_out"]
WEIGHT_NAMES = ["pre_norm_g", "post_norm_g", "w_in", "sgu_w", "sgu_b", "sgu_ln_g", "sgu_ln_b", "pool_w", "pool_scale",
                "q_norm_g", "w_uq", "kv_norm_g", "w_ukv", "w_out"]


def _local_step(x, positions, target, w_in, w_uq, w_ukv, w_out, small):
    rc, rs = _rope_tables(positions)
    params = [_layer_params(l, w_in, w_uq, w_ukv, w_out, small) for l in range(DEPTH)]
    saved = []
    for l in range(DEPTH):
        sv, outs = _layer_fwd(l, x, params[l], rc, rs, target if l == DEPTH - 1 else None)
        saved.append(sv)
        if l < DEPTH - 1:
            x = outs[0]
    dout, loss = outs
    grads = [None] * DEPTH
    for l in reversed(range(DEPTH)):
        dout, grads[l] = _layer_bwd(l, dout, saved[l], params[l], rc, rs)
    return loss[0, 0], dout, {k: jnp.stack([grads[l][k] for l in range(DEPTH)]) for k in WEIGHT_NAMES}


def _pack_small(tree):
    flat = jnp.concatenate([tree[k].reshape(-1) for k in SMALL_NAMES])
    rows = -(-flat.shape[0] // 1024) * 8
    return jnp.pad(flat, (0, rows * 128 - flat.shape[0])).reshape(rows, 128)


def _unpack_small(packed, like):
    flat = packed.reshape(-1)
    out, off = {}, 0
    for k in SMALL_NAMES:
        size = like[k].size
        out[k] = flat[off:off + size].reshape(like[k].shape)
        off += size
    return out


def kernel(x, positions, pre_norm_g, post_norm_g, w_in, sgu_w, sgu_b, sgu_ln_g, sgu_ln_b, pool_w, pool_scale, q_norm_g, w_uq, kv_norm_g, w_ukv, w_out, loss_target, m_pre_norm_g, m_post_norm_g, m_w_in, m_sgu_w, m_sgu_b, m_sgu_ln_g, m_sgu_ln_b, m_pool_w, m_pool_scale, m_q_norm_g, m_w_uq, m_kv_norm_g, m_w_ukv, m_w_out, v_pre_norm_g, v_post_norm_g, v_w_in, v_sgu_w, v_sgu_b, v_sgu_ln_g, v_sgu_ln_b, v_pool_w, v_pool_scale, v_q_norm_g, v_w_uq, v_kv_norm_g, v_w_ukv, v_w_out):
    w = dict(pre_norm_g=pre_norm_g, post_norm_g=post_norm_g, w_in=w_in, sgu_w=sgu_w, sgu_b=sgu_b, sgu_ln_g=sgu_ln_g,
             sgu_ln_b=sgu_ln_b, pool_w=pool_w, pool_scale=pool_scale, q_norm_g=q_norm_g, w_uq=w_uq, kv_norm_g=kv_norm_g,
             w_ukv=w_ukv, w_out=w_out)
    m = dict(pre_norm_g=m_pre_norm_g, post_norm_g=m_post_norm_g, w_in=m_w_in, sgu_w=m_sgu_w, sgu_b=m_sgu_b,
             sgu_ln_g=m_sgu_ln_g, sgu_ln_b=m_sgu_ln_b, pool_w=m_pool_w, pool_scale=m_pool_scale, q_norm_g=m_q_norm_g,
             w_uq=m_w_uq, kv_norm_g=m_kv_norm_g, w_ukv=m_w_ukv, w_out=m_w_out)
    v = dict(pre_norm_g=v_pre_norm_g, post_norm_g=v_post_norm_g, w_in=v_w_in, sgu_w=v_sgu_w, sgu_b=v_sgu_b,
             sgu_ln_g=v_sgu_ln_g, sgu_ln_b=v_sgu_ln_b, pool_w=v_pool_w, pool_scale=v_pool_scale, q_norm_g=v_q_norm_g,
             w_uq=v_w_uq, kv_norm_g=v_kv_norm_g, w_ukv=v_w_ukv, w_out=v_w_out)

    g_in, g_uq, g_ukv, g_out = _gather_weights([_mx(w[k]) for k in BIG_NAMES])
    cols = lambda g: g.reshape((4, 2) + g.shape[1:]).transpose(1, 2, 0, 3).reshape(2, g.shape[1], 4 * g.shape[2])
    full_out = g_out.reshape(4, 2, 256, 1024).transpose(1, 0, 2, 3).reshape(2, 1024, 1024)
    loss, dx, grads = _local_step(x[0], positions[0], loss_target[0], cols(g_in), cols(g_uq), cols(g_ukv), full_out, w)

    split_cols = lambda g: g.reshape(2, g.shape[1], 4, g.shape[2] // 4).transpose(2, 0, 1, 3).reshape(8, g.shape[1], g.shape[2] // 4)
    parts = [split_cols(grads["w_in"]), split_cols(grads["w_uq"]), split_cols(grads["w_ukv"]),
             grads["w_out"].reshape(2, 4, 256, 1024).transpose(1, 0, 2, 3).reshape(8, 256, 1024)]
    received = _all_to_all(parts, _pack_small(grads))
    sums = [_sum8(received[a], f"sum_{BIG_NAMES[a]}", 128) for a in range(4)]
    small_sum = _unpack_small(_sum8(received[4], "sum_small", received[4].shape[1]), w)
    both = _sibling_exchange(sums)
    total = dict(small_sum)
    for a, k in enumerate(BIG_NAMES):
        total[k] = both[a]

    packed = _adamw(_pack_small(w), _pack_small(total), _pack_small(m), _pack_small(v), "adamw_small", 2048)
    small_out = [_unpack_small(pk, w) for pk in packed]
    delta, new_m, new_v = {}, {}, {}
    for k in SMALL_NAMES:
        delta[k], new_m[k], new_v[k] = (so[k] for so in small_out)
    for k in BIG_NAMES:
        shape = w[k].shape
        flat = lambda a: a.reshape(shape[0] * shape[1], shape[2])
        res = _adamw(flat(w[k]), flat(total[k]), flat(m[k]), flat(v[k]), f"adamw_{k}", 256)
        delta[k], new_m[k], new_v[k] = (r.reshape(shape) for r in res)

    loss = lax.psum(loss, ("x", "y", "c"))
    return (loss, dx[None], *[total[k] for k in WEIGHT_NAMES], *[delta[k] for k in WEIGHT_NAMES],
            *[new_m[k] for k in WEIGHT_NAMES], *[new_v[k] for k in WEIGHT_NAMES])
```

```python
import jax
import jax.numpy as jnp
from jax import lax
from jax.experimental import pallas as pl
from jax.experimental.pallas import tpu as pltpu

F32 = jnp.float32
MXU_DTYPE = jnp.bfloat16
EPS = 1e-6
NEG_INF = -1e30
DEPTH = 2
N_HEADS = 4
QK_PAD = 256
V_DIM = 128
SCALE = 192 ** -0.5
LOG2E = 1.4426950408889634
ROPE_BASE = 10000.0
ADAM_LR, ADAM_B1, ADAM_B2, ADAM_EPS, ADAM_WD, ADAM_STEP = 0.001, 0.9, 0.999, 1e-08, 0.01, 10
VMEM_LIMIT_BYTES = 56 * 1024 * 1024
MESH = pl.DeviceIdType.MESH
ANY = pl.BlockSpec(memory_space=pl.ANY)

Z_MIX, Z_C, Z_KR, Z_GATE = 1280, 640, 128, 512
Z_W = Z_MIX + Z_C + Z_KR + Z_GATE


def _cp(n_axes=1):
    return pltpu.CompilerParams(dimension_semantics=("arbitrary",) * n_axes, vmem_limit_bytes=VMEM_LIMIT_BYTES)


def _dot(a, b):
    return lax.dot_general(a, b, (((1,), (0,)), ((), ())), preferred_element_type=F32)


def _dot_nt(a, b):
    return lax.dot_general(a, b, (((1,), (1,)), ((), ())), preferred_element_type=F32)


def _dot_tn(a, b):
    return lax.dot_general(a, b, (((0,), (0,)), ((), ())), preferred_element_type=F32)


def _mx(a):
    return a.astype(MXU_DTYPE)


def _silu_and_grad(g):
    sg = jax.nn.sigmoid(g)
    return g * sg, sg * (1.0 + g * (1.0 - sg))


def _rms(x, g):
    r = lax.rsqrt(jnp.mean(x * x, axis=-1, keepdims=True) + EPS)
    return x * r * g, r


def _rms_bwd(x, r, g, dy):
    xhat = x * r
    dyg = dy * g
    dx = r * (dyg - xhat * jnp.mean(dyg * xhat, axis=-1, keepdims=True))
    return dx, dy * xhat


def _acc(ref, val, first):
    @pl.when(first)
    def _():
        ref[...] = val

    @pl.when(jnp.logical_not(first))
    def _():
        ref[...] += val


def _colsum(a):
    return jnp.sum(a, axis=0, keepdims=True)


def _in_proj_fwd(x, g, w, name, tm=512):
    t, d = x.shape
    n = w.shape[1]
    tm = min(tm, t)

    def body(x_ref, g_ref, w_ref, z_ref, h_ref):
        h, _ = _rms(x_ref[...], g_ref[...])
        h = _mx(h)
        h_ref[...] = h
        z_ref[...] = _dot(h, w_ref[...])

    return pl.pallas_call(
        body, name=name, grid=(t // tm,),
        in_specs=[pl.BlockSpec((tm, d), lambda i: (i, 0)), pl.BlockSpec((1, d), lambda i: (0, 0)),
                  pl.BlockSpec((d, n), lambda i: (0, 0))],
        out_specs=[pl.BlockSpec((tm, n), lambda i: (i, 0)), pl.BlockSpec((tm, d), lambda i: (i, 0))],
        out_shape=[jax.ShapeDtypeStruct((t, n), F32), jax.ShapeDtypeStruct((t, d), MXU_DTYPE)],
        compiler_params=_cp())(x, g, w)


def _in_proj_bwd(dz_mix, dz_c, dz_kr, dz_gate, h, x, d_res, w, g, name, tm=256):
    t, d = x.shape
    n = w.shape[1]
    tm = min(tm, t)

    def body(dm_ref, dc_ref, dk_ref, dg_ref, h_ref, x_ref, dres_ref, w_ref, g_ref, dx_ref, dw_ref, dgn_ref):
        first = pl.program_id(0) == 0
        dz = jnp.concatenate([dm_ref[...], dc_ref[...], dk_ref[...], dg_ref[...]], axis=1)

        @pl.when(first)
        def _():
            dw_ref[...] = jnp.zeros(dw_ref.shape, F32)

        hb = h_ref[...]
        for c0 in range(0, n, 512):
            dw_ref[:, c0:c0 + 512] += _dot_tn(hb, dz[:, c0:c0 + 512])
        dh = _dot_nt(dz, w_ref[...])
        xf = x_ref[...]
        r = lax.rsqrt(jnp.mean(xf * xf, axis=-1, keepdims=True) + EPS)
        dx, dgt = _rms_bwd(xf, r, g_ref[...], dh)
        dx_ref[...] = dx + dres_ref[...]
        _acc(dgn_ref, _colsum(dgt), first)

    row = lambda wd: pl.BlockSpec((tm, wd), lambda i: (i, 0))
    fixed = lambda a, b: pl.BlockSpec((a, b), lambda i: (0, 0))
    return pl.pallas_call(
        body, name=name, grid=(t // tm,),
        in_specs=[row(Z_MIX), row(Z_C), row(Z_KR), row(Z_GATE), row(d), row(d), row(d), fixed(d, n), fixed(1, d)],
        out_specs=[row(d), fixed(d, n), fixed(1, d)],
        out_shape=[jax.ShapeDtypeStruct((t, d), F32), jax.ShapeDtypeStruct((d, n), F32),
                   jax.ShapeDtypeStruct((1, d), F32)],
        compiler_params=_cp())(dz_mix, dz_c, dz_kr, dz_gate, h, x, d_res, w, g)


def _lane_group(shape):
    return lax.broadcasted_iota(jnp.int32, shape, 1) // 64


def _select_group(vals):
    grp = _lane_group(vals[0].shape)
    out = vals[3]
    for gi in (2, 1, 0):
        out = jnp.where(grp == gi, vals[gi], out)
    return out


def _sgu_mask(transposed):
    r = (lax.broadcasted_iota(jnp.int32, (512, 128), 0) % 128) // 64
    c = lax.broadcasted_iota(jnp.int32, (512, 128), 1) // 64
    return (r <= c) if transposed else (c <= r)


def _sgu_apply(wstack, vb, nblk):
    outs = []
    for n in range(nblk):
        r = _dot(wstack, vb[n * 128:(n + 1) * 128, :])
        outs.append(_select_group([r[hh * 128:(hh + 1) * 128, :] for hh in range(4)]))
    return jnp.concatenate(outs, axis=0)


def _layer_norm(v, g, b):
    mu = jnp.mean(v, axis=-1, keepdims=True)
    vc = v - mu
    rstd = lax.rsqrt(jnp.mean(vc * vc, axis=-1, keepdims=True) + EPS)
    vhat = vc * rstd
    return vhat * g + b, vhat, rstd


def _pool_counts(t0, n):
    t = t0 + lax.broadcasted_iota(jnp.int32, (n, 256), 0)
    w = _select_group([jnp.full((n, 256), wv, jnp.int32) for wv in (2, 4, 8, 16)])
    return jnp.minimum(t + 1, w).astype(F32)


def _pooled(p, halo, t0):
    tm = p.shape[0]
    ext = jnp.concatenate([halo, p], axis=0)
    s2 = ext + pltpu.roll(ext, 1, 0)
    s4 = s2 + pltpu.roll(s2, 2, 0)
    s8 = s4 + pltpu.roll(s4, 4, 0)
    s16 = s8 + pltpu.roll(s8, 8, 0)
    sel = _select_group([s2, s4, s8, s16])[16:, :]
    return sel / _pool_counts(t0, tm) - p


def _pooled_bwd(dpool, dpool_halo, t0):
    tm = dpool.shape[0]
    n = tm + 16
    ext = jnp.concatenate([dpool, dpool_halo], axis=0) / _pool_counts(t0, n)
    f2 = ext + pltpu.roll(ext, n - 1, 0)
    f4 = f2 + pltpu.roll(f2, n - 2, 0)
    f8 = f4 + pltpu.roll(f4, n - 4, 0)
    f16 = f8 + pltpu.roll(f8, n - 8, 0)
    return _select_group([f2, f4, f8, f16])[:tm, :] - dpool


def _mix_specs(t, tm):
    nt16 = t // 16
    zrow = pl.BlockSpec((tm, Z_MIX), lambda i: (i, 0))
    prev_halo = pl.BlockSpec((16, 256), lambda i: (jnp.maximum(i * (tm // 16) - 1, 0), 3))
    fixed = lambda a, b: pl.BlockSpec((a, b), lambda i: (0, 0))
    params = [fixed(512, 128), fixed(128, 256), fixed(1, 256), fixed(1, 256), fixed(256, 256), fixed(1, 256)]
    return nt16, zrow, prev_halo, fixed, params


def _mix_fwd(z, sgu_w, sgu_bias, ln_g, ln_b, pool_wbd, pool_scale, name, tm=256):
    t = z.shape[0]
    tm = min(tm, t)
    _, zrow, prev_halo, _, params = _mix_specs(t, tm)

    def body(z_ref, halo_ref, w_ref, bias_ref, lng_ref, lnb_ref, pw_ref, ps_ref, y_ref):
        i = pl.program_id(0)
        u, v, gate = z_ref[:, 0:256], z_ref[:, 256:512], z_ref[:, 512:768]
        p, pgate = z_ref[:, 768:1024], z_ref[:, 1024:1280]
        vn, _, _ = _layer_norm(v, lng_ref[...], lnb_ref[...])
        wm = _mx(jnp.where(_sgu_mask(False), w_ref[...], 0.0))
        mixed = _sgu_apply(wm, _mx(vn), tm // 128) + jnp.tile(bias_ref[...], (tm // 128, 1))
        ya = u * mixed * _silu_and_grad(gate)[0]
        halo = jnp.where(i > 0, halo_ref[...], 0.0)
        pooled = _pooled(p, halo, i * tm)
        yb = _dot(_mx(pooled), pw_ref[...]) * ps_ref[...] * _silu_and_grad(pgate)[0]
        y_ref[...] = _mx(jnp.concatenate([ya, yb], axis=1))

    return pl.pallas_call(
        body, name=name, grid=(t // tm,),
        in_specs=[zrow, prev_halo] + params,
        out_specs=pl.BlockSpec((tm, 512), lambda i: (i, 0)),
        out_shape=jax.ShapeDtypeStruct((t, 512), MXU_DTYPE),
        compiler_params=_cp())(z, z, sgu_w, sgu_bias, ln_g, ln_b, pool_wbd, pool_scale)


def _mix_bwd(z, dycat, sgu_w, sgu_wt, sgu_bias, ln_g, ln_b, pool_wbd, pool_scale, name, tm=256):
    t = z.shape[0]
    tm = min(tm, t)
    nt16, zrow, prev_halo, fixed, params = _mix_specs(t, tm)
    nblk = tm // 128
    last = t // tm - 1

    def body(z_ref, halo_ref, zn_ref, dy_ref, dyn_ref, w_ref, wt_ref, bias_ref, lng_ref, lnb_ref, pw_ref, ps_ref,
             dz_ref, dw_ref, db_ref, dlng_ref, dlnb_ref, dpw_ref, dps_ref):
        i = pl.program_id(0)
        first = i == 0
        u, v, gate = z_ref[:, 0:256], z_ref[:, 256:512], z_ref[:, 512:768]
        p, pgate = z_ref[:, 768:1024], z_ref[:, 1024:1280]
        dya, dyb = dy_ref[:, 0:256], dy_ref[:, 256:512]
        vn, vhat, rstd = _layer_norm(v, lng_ref[...], lnb_ref[...])
        vnb = _mx(vn)
        wm = _mx(jnp.where(_sgu_mask(False), w_ref[...], 0.0))
        wmt = _mx(jnp.where(_sgu_mask(True), wt_ref[...], 0.0))
        mixed = _sgu_apply(wm, vnb, nblk) + jnp.tile(bias_ref[...], (nblk, 1))
        silu, dsilu = _silu_and_grad(gate)
        t1 = u * mixed
        d_gate = dya * t1 * dsilu
        d_t1 = dya * silu
        d_u = d_t1 * mixed
        d_mixed = d_t1 * u
        dmb = _mx(d_mixed)
        d_vn = _sgu_apply(wmt, dmb, nblk)
        grp = _lane_group((128, 256))
        lane = lax.broadcasted_iota(jnp.int32, (128, 128), 1)
        dws = [jnp.zeros((128, 128), F32) for _ in range(4)]
        dbias = jnp.zeros((128, 128), F32)
        for n in range(nblk):
            dm_n, dmb_n, vnb_n = d_mixed[n * 128:(n + 1) * 128], dmb[n * 128:(n + 1) * 128], vnb[n * 128:(n + 1) * 128]
            for hh in range(4):
                dws[hh] = dws[hh] + _dot_nt(jnp.where(grp == hh, dmb_n, jnp.zeros_like(dmb_n)), vnb_n)
                rs = jnp.sum(jnp.where(grp == hh, dm_n, 0.0), axis=-1, keepdims=True)
                dbias = dbias + jnp.where(lane == hh, rs, 0.0)
        _acc(dw_ref, jnp.concatenate(dws, axis=0), first)
        _acc(db_ref, dbias, first)
        _acc(dlng_ref, _colsum(d_vn * vhat), first)
        _acc(dlnb_ref, _colsum(d_vn), first)
        dvh = d_vn * lng_ref[...]
        d_v = rstd * (dvh - jnp.mean(dvh, axis=-1, keepdims=True) - vhat * jnp.mean(dvh * vhat, axis=-1, keepdims=True))

        @pl.when(i == last)
        def _():
            dw_ref[...] = jnp.where(_sgu_mask(False), dw_ref[...], 0.0)

        halo = jnp.where(i > 0, halo_ref[...], 0.0)
        pooled = _pooled(p, halo, i * tm)
        pooled_b = _mx(pooled)
        mixedp = _dot(pooled_b, pw_ref[...])
        psilu, pdsilu = _silu_and_grad(pgate)
        d_pgate = dyb * (mixedp * ps_ref[...]) * pdsilu
        d_ms = dyb * psilu
        _acc(dps_ref, _colsum(d_ms * mixedp), first)
        dmpb = _mx(d_ms * ps_ref[...])
        _acc(dpw_ref, _dot_tn(pooled_b, dmpb), first)
        d_pooled = _dot_nt(dmpb, pw_ref[...])
        dmp_halo = _mx(dyn_ref[...] * _silu_and_grad(zn_ref[...])[0] * ps_ref[...])
        d_pooled_halo = jnp.where(i < last, _dot_nt(dmp_halo, pw_ref[...]), 0.0)
        d_p = _pooled_bwd(d_pooled, d_pooled_halo, i * tm)
        dz_ref[...] = _mx(jnp.concatenate([d_u, d_v, d_gate, d_p, d_pgate], axis=1))

    nxt = lambda i: jnp.minimum((i + 1) * (tm // 16), nt16 - 1)
    return pl.pallas_call(
        body, name=name, grid=(t // tm,),
        in_specs=[zrow, prev_halo, pl.BlockSpec((16, 256), lambda i: (nxt(i), 4)),
                  pl.BlockSpec((tm, 512), lambda i: (i, 0)), pl.BlockSpec((16, 256), lambda i: (nxt(i), 1)),
                  params[0], fixed(512, 128)] + params[1:],
        out_specs=[pl.BlockSpec((tm, Z_MIX), lambda i: (i, 0)), fixed(512, 128), fixed(128, 128), fixed(1, 256),
                   fixed(1, 256), fixed(256, 256), fixed(1, 256)],
        out_shape=[jax.ShapeDtypeStruct((t, Z_MIX), MXU_DTYPE), jax.ShapeDtypeStruct((512, 128), F32),
                   jax.ShapeDtypeStruct((128, 128), F32), jax.ShapeDtypeStruct((1, 256), F32),
                   jax.ShapeDtypeStruct((1, 256), F32), jax.ShapeDtypeStruct((256, 256), F32),
                   jax.ShapeDtypeStruct((1, 256), F32)],
        compiler_params=_cp())(z, z, z, dycat, dycat, sgu_w, sgu_wt, sgu_bias, ln_g, ln_b, pool_wbd, pool_scale)


def _rot_half(x, transpose):
    w = x.shape[1]
    lane = lax.broadcasted_iota(jnp.int32, x.shape, 1) % min(w, 256)
    base = 128 if w >= 256 else 0
    lo = jnp.logical_and(lane >= base, lane < base + 32)
    hi = jnp.logical_and(lane >= base + 32, lane < base + 64)
    up = pltpu.roll(x, w - 32, 1)
    down = pltpu.roll(x, 32, 1)
    if transpose:
        return jnp.where(lo, up, jnp.where(hi, -down, 0.0))
    return jnp.where(lo, -up, jnp.where(hi, down, 0.0))


def _rope(x, c, s):
    return x * c + _rot_half(x, False) * s


def _rope_bwd(dy, c, s):
    return dy * c + _rot_half(dy * s, True)


def _qkv_fwd(z, rc, rs, w_uq, w_ukv, gq, gkv, name, tm=256):
    t = z.shape[0]
    tm = min(tm, t)

    def body(zc_ref, zk_ref, rc_ref, rs_ref, wq_ref, wkv_ref, gq_ref, gkv_ref, q_ref, k_ref, v_ref):
        cq, ckv = zc_ref[:, 0:384], zc_ref[:, 384:640]
        c, s = rc_ref[...], rs_ref[...]
        qn, _ = _rms(cq, gq_ref[...])
        q = _rope(_dot(_mx(qn), wq_ref[...]), jnp.tile(c, (1, N_HEADS)), jnp.tile(s, (1, N_HEADS)))
        kvn, _ = _rms(ckv, gkv_ref[...])
        kv = _dot(_mx(kvn), wkv_ref[...])
        kpe = _rope(zk_ref[...], c[:, 128:256], s[:, 128:256])
        for hh in range(N_HEADS):
            q_ref[hh] = _mx(q[:, hh * QK_PAD:(hh + 1) * QK_PAD])
            k_ref[hh] = _mx(jnp.concatenate([kv[:, hh * 128:(hh + 1) * 128], kpe], axis=1))
            v_ref[hh] = _mx(kv[:, 512 + hh * 128:512 + (hh + 1) * 128])

    fixed = lambda a, b: pl.BlockSpec((a, b), lambda i: (0, 0))
    heads = lambda wd: pl.BlockSpec((N_HEADS, tm, wd), lambda i: (0, i, 0))
    return pl.pallas_call(
        body, name=name, grid=(t // tm,),
        in_specs=[pl.BlockSpec((tm, Z_C), lambda i: (i, Z_MIX // Z_C)),
                  pl.BlockSpec((tm, Z_KR), lambda i: (i, (Z_MIX + Z_C) // Z_KR)),
                  pl.BlockSpec((tm, 256), lambda i: (i, 0)), pl.BlockSpec((tm, 256), lambda i: (i, 0)),
                  fixed(384, 1024), fixed(256, 1024), fixed(1, 384), fixed(1, 256)],
        out_specs=[heads(QK_PAD), heads(QK_PAD), heads(V_DIM)],
        out_shape=[jax.ShapeDtypeStruct((N_HEADS, t, QK_PAD), MXU_DTYPE),
                   jax.ShapeDtypeStruct((N_HEADS, t, QK_PAD), MXU_DTYPE),
                   jax.ShapeDtypeStruct((N_HEADS, t, V_DIM), MXU_DTYPE)],
        compiler_params=_cp())(z, z, rc, rs, w_uq, w_ukv, gq, gkv)


def _qkv_bwd(dq, dk, dv, z, rc, rs, w_uq, w_ukv, gq, gkv, name, tm=256):
    t = z.shape[0]
    tm = min(tm, t)

    def body(dq_ref, dk_ref, dv_ref, zc_ref, rc_ref, rs_ref, wq_ref, wkv_ref, gq_ref, gkv_ref,
             dzc_ref, dzk_ref, dwq_ref, dwkv_ref, dgq_ref, dgkv_ref):
        first = pl.program_id(0) == 0
        cq, ckv = zc_ref[:, 0:384], zc_ref[:, 384:640]
        c, s = rc_ref[...], rs_ref[...]
        dq_all = jnp.concatenate([dq_ref[hh] for hh in range(N_HEADS)], axis=1)
        dqp = _mx(_rope_bwd(dq_all, jnp.tile(c, (1, N_HEADS)), jnp.tile(s, (1, N_HEADS))))
        qn, rq = _rms(cq, gq_ref[...])
        _acc(dwq_ref, _dot_tn(_mx(qn), dqp), first)
        d_cq, dgq_t = _rms_bwd(cq, rq, gq_ref[...], _dot_nt(dqp, wq_ref[...]))
        _acc(dgq_ref, _colsum(dgq_t), first)
        dkpe = dk_ref[0][:, 128:256]
        for hh in range(1, N_HEADS):
            dkpe = dkpe + dk_ref[hh][:, 128:256]
        dzk_ref[...] = _mx(_rope_bwd(dkpe, c[:, 128:256], s[:, 128:256]))
        dkv = _mx(jnp.concatenate([dk_ref[hh][:, 0:128] for hh in range(N_HEADS)]
                                  + [dv_ref[hh] for hh in range(N_HEADS)], axis=1))
        kvn, rkv = _rms(ckv, gkv_ref[...])
        _acc(dwkv_ref, _dot_tn(_mx(kvn), dkv), first)
        d_ckv, dgkv_t = _rms_bwd(ckv, rkv, gkv_ref[...], _dot_nt(dkv, wkv_ref[...]))
        _acc(dgkv_ref, _colsum(dgkv_t), first)
        dzc_ref[...] = _mx(jnp.concatenate([d_cq, d_ckv], axis=1))

    fixed = lambda a, b: pl.BlockSpec((a, b), lambda i: (0, 0))
    heads = lambda wd: pl.BlockSpec((N_HEADS, tm, wd), lambda i: (0, i, 0))
    return pl.pallas_call(
        body, name=name, grid=(t // tm,),
        in_specs=[heads(QK_PAD), heads(QK_PAD), heads(V_DIM), pl.BlockSpec((tm, Z_C), lambda i: (i, Z_MIX // Z_C)),
                  pl.BlockSpec((tm, 256), lambda i: (i, 0)), pl.BlockSpec((tm, 256), lambda i: (i, 0)),
                  fixed(384, 1024), fixed(256, 1024), fixed(1, 384), fixed(1, 256)],
        out_specs=[pl.BlockSpec((tm, Z_C), lambda i: (i, 0)), pl.BlockSpec((tm, Z_KR), lambda i: (i, 0)),
                   fixed(384, 1024), fixed(256, 1024), fixed(1, 384), fixed(1, 256)],
        out_shape=[jax.ShapeDtypeStruct((t, Z_C), MXU_DTYPE), jax.ShapeDtypeStruct((t, Z_KR), MXU_DTYPE),
                   jax.ShapeDtypeStruct((384, 1024), F32), jax.ShapeDtypeStruct((256, 1024), F32),
                   jax.ShapeDtypeStruct((1, 384), F32), jax.ShapeDtypeStruct((1, 256), F32)],
        compiler_params=_cp())(dq, dk, dv, z, rc, rs, w_uq, w_ukv, gq, gkv)


def _chunk_mask(tq):
    r = lax.broadcasted_iota(jnp.int32, (tq, tq), 0) // 64
    c = lax.broadcasted_iota(jnp.int32, (tq, tq), 1) // 64
    return c <= r


def _gate_block(tq):
    return pl.BlockSpec((tq, 128), lambda h, i: (i, (Z_MIX + Z_C + Z_KR) // 128 + h))


def _attn_fwd(qh, kh, vh, z, name, tq=512):
    t = qh.shape[1]
    tq = min(tq, t)

    def body(q_ref, g_ref, k_hbm, v_hbm, o_ref, yc_ref, lse_ref, k_v, v_v, m_s, acc_s, s_a, s_b, sem):
        h, i = pl.program_id(0), pl.program_id(1)

        @pl.when(i == 0)
        def _():
            ck = pltpu.make_async_copy(k_hbm.at[h], k_v, sem.at[0])
            cv = pltpu.make_async_copy(v_hbm.at[h], v_v.at[:, 0:V_DIM], sem.at[1])
            ck.start()
            cv.start()
            v_v[:, V_DIM:2 * V_DIM] = jnp.ones((t, V_DIM), MXU_DTYPE)
            ck.wait()
            cv.wait()

        q = q_ref[...]
        m_s[...] = jnp.full(m_s.shape, NEG_INF, F32)
        acc_s[...] = jnp.zeros(acc_s.shape, F32)

        def keys(j):
            return pl.ds(pl.multiple_of(j * tq, tq), tq)

        def scores(s_ref, j):
            s_ref[...] = _dot_nt(q, k_v[keys(j), :]) * (SCALE * LOG2E)

        def softmax_pv(s_ref, j, masked):
            s = s_ref[...]
            if masked:
                s = jnp.where(_chunk_mask(tq), s, NEG_INF)
            m_old = m_s[...]
            m_new = jnp.maximum(m_old, jnp.max(s, axis=-1, keepdims=True))
            p = jnp.exp2(s - jnp.tile(m_new, (1, tq // 128)))
            alpha = jnp.exp2(m_old - m_new)
            m_s[...] = m_new
            acc_s[...] = jnp.tile(alpha, (1, 2)) * acc_s[...] + _dot(_mx(p), v_v[keys(j), :])

        scores(s_a, 0)

        def pair(pp, carry):
            scores(s_b, 2 * pp + 1)
            softmax_pv(s_a, 2 * pp, False)
            scores(s_a, 2 * pp + 2)
            softmax_pv(s_b, 2 * pp + 1, False)
            return carry

        lax.fori_loop(0, i // 2, pair, 0)

        @pl.when(i % 2 == 1)
        def _():
            scores(s_b, i)
            softmax_pv(s_a, i - 1, False)
            softmax_pv(s_b, i, True)

        @pl.when(i % 2 == 0)
        def _():
            softmax_pv(s_a, i, True)

        l = acc_s[:, V_DIM:2 * V_DIM]
        o = acc_s[:, 0:V_DIM] / l
        o_ref[...] = o
        yc_ref[...] = _mx(o * _silu_and_grad(g_ref[...])[0])
        lse_ref[...] = m_s[...] + jnp.log2(l)

    return pl.pallas_call(
        body, name=name, grid=(N_HEADS, t // tq),
        in_specs=[pl.BlockSpec((None, tq, QK_PAD), lambda h, i: (h, i, 0)), _gate_block(tq), ANY, ANY],
        out_specs=[pl.BlockSpec((tq, 128), lambda h, i: (i, h)), pl.BlockSpec((tq, 128), lambda h, i: (i, h)),
                   pl.BlockSpec((None, tq, 128), lambda h, i: (h, i, 0))],
        out_shape=[jax.ShapeDtypeStruct((t, N_HEADS * V_DIM), F32), jax.ShapeDtypeStruct((t, N_HEADS * V_DIM), MXU_DTYPE),
                   jax.ShapeDtypeStruct((N_HEADS, t, 128), F32)],
        scratch_shapes=[pltpu.VMEM((t, QK_PAD), MXU_DTYPE), pltpu.VMEM((t, 2 * V_DIM), MXU_DTYPE),
                        pltpu.VMEM((tq, 128), F32), pltpu.VMEM((tq, 2 * V_DIM), F32),
                        pltpu.VMEM((tq, tq), F32), pltpu.VMEM((tq, tq), F32), pltpu.SemaphoreType.DMA((2,))],
        compiler_params=_cp(2))(qh, z, kh, vh)


def _attn_bwd(qh, kh, vh, o, lse, dycat, z, name, tq=512):
    t = qh.shape[1]
    tq = min(tq, t)
    nq = t // tq

    def body(q_ref, o_ref, lse_ref, dy_ref, g_ref, k_hbm, v_hbm, dq_ref, dgate_ref, dk_hbm, dv_hbm,
             k_v, v_v, dk_acc, dv_acc, dq_acc, delta_s, s_a, dp_a, s_b, dp_b, sem):
        h, i = pl.program_id(0), pl.program_id(1)

        @pl.when(i == 0)
        def _():
            ck = pltpu.make_async_copy(k_hbm.at[h], k_v, sem.at[0])
            cv = pltpu.make_async_copy(v_hbm.at[h], v_v, sem.at[1])
            ck.start()
            cv.start()
            dk_acc[...] = jnp.zeros(dk_acc.shape, F32)
            dv_acc[...] = jnp.zeros(dv_acc.shape, F32)
            ck.wait()
            cv.wait()

        gate, dy, of = g_ref[...], dy_ref[...], o_ref[...]
        silu, dsilu = _silu_and_grad(gate)
        do = dy * silu
        delta = jnp.sum(do * of, axis=-1, keepdims=True)
        dgate_ref[...] = _mx(dy * of * dsilu)
        dob = _mx(do)
        q = q_ref[...]
        delta_s[...] = jnp.broadcast_to(delta, delta_s.shape)
        dq_acc[...] = jnp.zeros(dq_acc.shape, F32)

        def keys(j):
            return pl.ds(pl.multiple_of(j * tq, tq), tq)

        def scores(s_ref, dp_ref, j):
            s_ref[...] = _dot_nt(q, k_v[keys(j), :]) * (SCALE * LOG2E)
            dp_ref[...] = _dot_nt(dob, v_v[keys(j), :])

        def grads(s_ref, dp_ref, j, masked):
            ks = keys(j)
            p = jnp.exp2(s_ref[...] - jnp.tile(lse_ref[...], (1, tq // 128)))
            if masked:
                p = jnp.where(_chunk_mask(tq), p, 0.0)
            ds = p * (dp_ref[...] - jnp.tile(delta_s[...], (1, tq // 128))) * SCALE
            pb, dsb = _mx(p), _mx(ds)
            dq_acc[...] += _dot(dsb, k_v[ks, :])
            dk_acc[ks, :] += _dot_tn(dsb, q)
            dv_acc[ks, :] += _dot_tn(pb, dob)

        scores(s_a, dp_a, 0)

        def pair(pp, carry):
            scores(s_b, dp_b, 2 * pp + 1)
            grads(s_a, dp_a, 2 * pp, False)
            scores(s_a, dp_a, 2 * pp + 2)
            grads(s_b, dp_b, 2 * pp + 1, False)
            return carry

        lax.fori_loop(0, i // 2, pair, 0)

        @pl.when(i % 2 == 1)
        def _():
            scores(s_b, dp_b, i)
            grads(s_a, dp_a, i - 1, False)
            grads(s_b, dp_b, i, True)

        @pl.when(i % 2 == 0)
        def _():
            grads(s_a, dp_a, i, True)

        dq_ref[...] = dq_acc[...]

        @pl.when(i == nq - 1)
        def _():
            ck = pltpu.make_async_copy(dk_acc, dk_hbm.at[h], sem.at[0])
            cv = pltpu.make_async_copy(dv_acc, dv_hbm.at[h], sem.at[1])
            ck.start()
            cv.start()
            ck.wait()
            cv.wait()

    return pl.pallas_call(
        body, name=name, grid=(N_HEADS, nq),
        in_specs=[pl.BlockSpec((None, tq, QK_PAD), lambda h, i: (h, i, 0)),
                  pl.BlockSpec((tq, 128), lambda h, i: (i, h)),
                  pl.BlockSpec((None, tq, 128), lambda h, i: (h, i, 0)),
                  pl.BlockSpec((tq, 128), lambda h, i: (i, N_HEADS + h)), _gate_block(tq), ANY, ANY],
        out_specs=[pl.BlockSpec((None, tq, QK_PAD), lambda h, i: (h, i, 0)),
                   pl.BlockSpec((tq, 128), lambda h, i: (i, h)), ANY, ANY],
        out_shape=[jax.ShapeDtypeStruct((N_HEADS, t, QK_PAD), F32), jax.ShapeDtypeStruct((t, Z_GATE), MXU_DTYPE),
                   jax.ShapeDtypeStruct((N_HEADS, t, QK_PAD), F32), jax.ShapeDtypeStruct((N_HEADS, t, V_DIM), F32)],
        scratch_shapes=[pltpu.VMEM((t, QK_PAD), MXU_DTYPE), pltpu.VMEM((t, V_DIM), MXU_DTYPE),
                        pltpu.VMEM((t, QK_PAD), F32), pltpu.VMEM((t, V_DIM), F32), pltpu.VMEM((tq, QK_PAD), F32),
                        pltpu.VMEM((tq, 128), F32)] + [pltpu.VMEM((tq, tq), F32)] * 4
        + [pltpu.SemaphoreType.DMA((2,))],
        compiler_params=_cp(2))(qh, o, lse, dycat, z, kh, vh)


def _out_proj_fwd(yab, yc, w, x, g, target, name, tm=512):
    t, d = x.shape
    tm = min(tm, t)
    is_last = target is not None

    def body(*refs):
        if is_last:
            yab_ref, yc_ref, w_ref, x_ref, g_ref, t_ref, y_ref, dout_ref, loss_ref = refs
        else:
            yab_ref, yc_ref, w_ref, x_ref, g_ref, y_ref, out_ref = refs
        y = _dot(jnp.concatenate([yab_ref[...], yc_ref[...]], axis=1), w_ref[...])
        y_ref[...] = y
        out = x_ref[...] + _rms(y, g_ref[...])[0]
        if is_last:
            diff = out - t_ref[...]
            dout_ref[...] = diff * (1.0 / d)
            part = jnp.sum(jnp.sum(diff * diff, axis=-1, keepdims=True), axis=0, keepdims=True) * (0.5 / d)
            _acc(loss_ref, jnp.broadcast_to(part, (1, 128)), pl.program_id(0) == 0)
        else:
            out_ref[...] = out

    row = lambda wd: pl.BlockSpec((tm, wd), lambda i: (i, 0))
    fixed = lambda a, b: pl.BlockSpec((a, b), lambda i: (0, 0))
    in_specs = [row(512), row(512), fixed(d, d), row(d), fixed(1, d)]
    args = [yab, yc, w, x, g]
    out_specs = [row(d), row(d)]
    out_shape = [jax.ShapeDtypeStruct((t, d), F32), jax.ShapeDtypeStruct((t, d), F32)]
    if is_last:
        in_specs.append(row(d))
        args.append(target)
        out_specs.append(fixed(1, 128))
        out_shape.append(jax.ShapeDtypeStruct((1, 128), F32))
    return pl.pallas_call(body, name=name, grid=(t // tm,), in_specs=in_specs, out_specs=out_specs,
                          out_shape=out_shape, compiler_params=_cp())(*args)


def _out_proj_bwd(dout, y, yab, yc, w, g, name, tm=512):
    t, d = y.shape
    tm = min(tm, t)

    def body(dout_ref, y_ref, yab_ref, yc_ref, w_ref, g_ref, dycat_ref, dw_ref, dg_ref):
        first = pl.program_id(0) == 0
        y = y_ref[...]
        r = lax.rsqrt(jnp.mean(y * y, axis=-1, keepdims=True) + EPS)
        dy, dgt = _rms_bwd(y, r, g_ref[...], dout_ref[...])
        _acc(dg_ref, _colsum(dgt), first)
        dyb = _mx(dy)
        _acc(dw_ref, _dot_tn(jnp.concatenate([yab_ref[...], yc_ref[...]], axis=1), dyb), first)
        dycat_ref[...] = _dot_nt(dyb, w_ref[...])

    row = lambda wd: pl.BlockSpec((tm, wd), lambda i: (i, 0))
    fixed = lambda a, b: pl.BlockSpec((a, b), lambda i: (0, 0))
    return pl.pallas_call(
        body, name=name, grid=(t // tm,),
        in_specs=[row(d), row(d), row(512), row(512), fixed(d, d), fixed(1, d)],
        out_specs=[row(d), fixed(d, d), fixed(1, d)],
        out_shape=[jax.ShapeDtypeStruct((t, d), F32), jax.ShapeDtypeStruct((d, d), F32),
                   jax.ShapeDtypeStruct((1, d), F32)],
        compiler_params=_cp())(dout, y, yab, yc, w, g)


def _mesh_pos():
    return lax.axis_index("x"), lax.axis_index("y"), lax.axis_index("c")


def _remote(src, dst, send_sem, recv_sem, to):
    return pltpu.make_async_remote_copy(src_ref=src, dst_ref=dst, send_sem=send_sem, recv_sem=recv_sem,
                                        device_id=to, device_id_type=MESH)


def _gather_weights(shards):
    n = len(shards)

    def body(*refs):
        ins, outs = refs[:n], refs[n:2 * n]
        send_sems, recv_sems, local_sems = refs[2 * n:]
        x, y, c = _mesh_pos()
        me, sibling = (x, y, c), (x, y, 1 - c)
        chips = [(1 - x, y), (x, 1 - y), (1 - x, 1 - y)]
        slot = lambda cx, cy, layer: 2 * (2 * cx + cy) + layer
        local = []
        for a in range(n):
            for layer in range(2):
                local.append(pltpu.make_async_copy(ins[a].at[layer], outs[a].at[slot(x, y, layer)],
                                                   local_sems.at[a, layer]))
                local[-1].start()
        first = []
        for a in range(n):
            for j, (cx, cy) in enumerate(chips):
                first.append(_remote(ins[a].at[c], outs[a].at[slot(x, y, c)], send_sems.at[a, j], recv_sems.at[a, j],
                                     (cx, cy, c)))
                first[-1].start()
        passed = []
        for a in range(n):
            for j, (cx, cy) in enumerate(chips):
                blk = outs[a].at[slot(cx, cy, c)]
                _remote(blk, blk, send_sems.at[a, j], recv_sems.at[a, j], me).wait_recv()
                passed.append(_remote(blk, blk, send_sems.at[a, 3 + j], recv_sems.at[a, 3 + j], sibling))
                passed[-1].start()
        for a in range(n):
            for j, (cx, cy) in enumerate(chips):
                blk = outs[a].at[slot(cx, cy, 1 - c)]
                _remote(blk, blk, send_sems.at[a, 3 + j], recv_sems.at[a, 3 + j], me).wait_recv()
        for cp in first + passed:
            cp.wait_send()
        for cp in local:
            cp.wait()

    return pl.pallas_call(
        body, name="gather_weights", in_specs=[ANY] * n, out_specs=[ANY] * n,
        out_shape=[jax.ShapeDtypeStruct((8,) + s.shape[1:], s.dtype) for s in shards],
        scratch_shapes=[pltpu.SemaphoreType.DMA((n, 6)), pltpu.SemaphoreType.DMA((n, 6)),
                        pltpu.SemaphoreType.DMA((n, 2))])(*shards)


def _all_to_all(parts, common):
    n = len(parts)

    def body(*refs):
        ins, outs = refs[:n + 1], refs[n + 1:2 * n + 2]
        send_sems, recv_sems, local_sems = refs[2 * n + 2:]
        x, y, c = _mesh_pos()
        me = 4 * x + 2 * y + c
        src = lambda a, j: ins[a].at[j] if a < n else ins[a]
        local = []
        for a in range(n + 1):
            local.append(pltpu.make_async_copy(src(a, me), outs[a].at[me], local_sems.at[a]))
            local[-1].start()
        sent = []
        for rel in range(1, 8):
            px = 1 - x if rel & 4 else x
            py = 1 - y if rel & 2 else y
            pc = 1 - c if rel & 1 else c
            peer = 4 * px + 2 * py + pc
            for a in range(n + 1):
                sent.append(_remote(src(a, peer), outs[a].at[me], send_sems.at[a, rel], recv_sems.at[a, rel],
                                    (px, py, pc)))
                sent[-1].start()
        for rel in range(1, 8):
            px = 1 - x if rel & 4 else x
            py = 1 - y if rel & 2 else y
            pc = 1 - c if rel & 1 else c
            peer = 4 * px + 2 * py + pc
            for a in range(n + 1):
                _remote(src(a, peer), outs[a].at[peer], send_sems.at[a, rel], recv_sems.at[a, rel],
                        (x, y, c)).wait_recv()
        for cp in sent:
            cp.wait_send()
        for cp in local:
            cp.wait()

    return pl.pallas_call(
        body, name="grad_all_to_all", in_specs=[ANY] * (n + 1), out_specs=[ANY] * (n + 1),
        out_shape=[jax.ShapeDtypeStruct(p.shape, p.dtype) for p in parts]
        + [jax.ShapeDtypeStruct((8,) + common.shape, common.dtype)],
        scratch_shapes=[pltpu.SemaphoreType.DMA((n + 1, 8)), pltpu.SemaphoreType.DMA((n + 1, 8)),
                        pltpu.SemaphoreType.DMA((n + 1,))])(*parts, common)


def _sibling_exchange(sums):
    n = len(sums)

    def body(*refs):
        ins, outs = refs[:n], refs[n:2 * n]
        send_sems, recv_sems, local_sems = refs[2 * n:]
        x, y, c = _mesh_pos()
        copies = []
        for a in range(n):
            copies.append(pltpu.make_async_copy(ins[a], outs[a].at[c], local_sems.at[a]))
            copies[-1].start()
        sent = []
        for a in range(n):
            sent.append(_remote(ins[a], outs[a].at[c], send_sems.at[a], recv_sems.at[a], (x, y, 1 - c)))
            sent[-1].start()
        for a in range(n):
            _remote(ins[a], outs[a].at[1 - c], send_sems.at[a], recv_sems.at[a], (x, y, c)).wait_recv()
        for cp in sent:
            cp.wait_send()
        for cp in copies:
            cp.wait()

    return pl.pallas_call(
        body, name="sibling_exchange", in_specs=[ANY] * n, out_specs=[ANY] * n,
        out_shape=[jax.ShapeDtypeStruct((2,) + s.shape, s.dtype) for s in sums],
        scratch_shapes=[pltpu.SemaphoreType.DMA((n,)), pltpu.SemaphoreType.DMA((n,)),
                        pltpu.SemaphoreType.DMA((n,))])(*sums)


def _sum8(parts, name, tr):
    _, r, c = parts.shape
    tr = min(tr, r)

    def body(p_ref, o_ref):
        acc = p_ref[0]
        for j in range(1, 8):
            acc = acc + p_ref[j]
        o_ref[...] = acc

    return pl.pallas_call(
        body, name=name, grid=(r // tr,),
        in_specs=[pl.BlockSpec((8, tr, c), lambda i: (0, i, 0))], out_specs=pl.BlockSpec((tr, c), lambda i: (i, 0)),
        out_shape=jax.ShapeDtypeStruct((r, c), parts.dtype), compiler_params=_cp())(parts)


def _adamw(w, g, m, v, name, tr):
    r, c = w.shape
    tr = min(tr, r)

    def body(w_ref, g_ref, m_ref, v_ref, d_ref, nm_ref, nv_ref):
        gg = g_ref[...]
        nm = ADAM_B1 * m_ref[...] + (1.0 - ADAM_B1) * gg
        nv = ADAM_B2 * v_ref[...] + (1.0 - ADAM_B2) * jnp.square(gg)
        m_hat = nm / (1.0 - ADAM_B1 ** ADAM_STEP)
        v_hat = nv / (1.0 - ADAM_B2 ** ADAM_STEP)
        d_ref[...] = -ADAM_LR * (m_hat / (jnp.sqrt(v_hat) + ADAM_EPS) + ADAM_WD * w_ref[...])
        nm_ref[...] = nm
        nv_ref[...] = nv

    blk = pl.BlockSpec((tr, c), lambda i: (i, 0))
    return pl.pallas_call(
        body, name=name, grid=(r // tr,), in_specs=[blk] * 4, out_specs=[blk] * 3,
        out_shape=[jax.ShapeDtypeStruct((r, c), F32)] * 3, compiler_params=_cp())(w, g, m, v)


def _rope_tables(positions):
    inv_freq = ROPE_BASE ** (-jnp.arange(0, 64, 2, dtype=F32) / 64)
    ang = positions.astype(F32)[:, None] * inv_freq
    cos, sin = jnp.cos(ang), jnp.sin(ang)
    t = positions.shape[0]
    rc = jnp.concatenate([jnp.ones((t, 128), F32), cos, cos, jnp.ones((t, 64), F32)], axis=1)
    rs = jnp.concatenate([jnp.zeros((t, 128), F32), sin, sin, jnp.zeros((t, 64), F32)], axis=1)
    return rc, rs


def _layer_params(l, w_in, w_uq, w_ukv, w_out, small):
    p = {}
    p["w_in"] = jnp.concatenate([w_in[l][:, :1984], jnp.zeros((1024, 64), w_in.dtype), w_in[l][:, 1984:]], axis=1)
    p["w_uq"] = jnp.pad(w_uq[l].reshape(384, 4, 192), ((0, 0), (0, 0), (0, 64))).reshape(384, 1024)
    p["w_ukv"] = w_ukv[l].reshape(256, 4, 2, 128).transpose(0, 2, 1, 3).reshape(256, 1024)
    p["w_out"] = w_out[l]
    p["pre_g"] = small["pre_norm_g"][l][None]
    p["post_g"] = small["post_norm_g"][l][None]
    p["sgu_w"] = small["sgu_w"][l].reshape(512, 128)
    p["sgu_wt"] = small["sgu_w"][l].transpose(0, 2, 1).reshape(512, 128)
    p["sgu_bias"] = jnp.repeat(small["sgu_b"][l].T, 64, axis=1)
    p["ln_g"] = small["sgu_ln_g"][l][None]
    p["ln_b"] = small["sgu_ln_b"][l][None]
    p["pool_wbd"] = _mx(jax.scipy.linalg.block_diag(*[small["pool_w"][l][gi] for gi in range(4)]))
    p["pool_scale"] = small["pool_scale"][l][None]
    p["gq"] = small["q_norm_g"][l][None]
    p["gkv"] = small["kv_norm_g"][l][None]
    return p


def _layer_fwd(l, x, p, rc, rs, target):
    z, h = _in_proj_fwd(x, p["pre_g"], p["w_in"], f"in_proj_fwd_{l}")
    yab = _mix_fwd(z, p["sgu_w"], p["sgu_bias"], p["ln_g"], p["ln_b"], p["pool_wbd"], p["pool_scale"], f"mix_fwd_{l}")
    qh, kh, vh = _qkv_fwd(z, rc, rs, p["w_uq"], p["w_ukv"], p["gq"], p["gkv"], f"qkv_fwd_{l}")
    o, yc, lse = _attn_fwd(qh, kh, vh, z, f"attn_fwd_{l}")
    outs = _out_proj_fwd(yab, yc, p["w_out"], x, p["post_g"], target, f"out_proj_fwd_{l}")
    saved = dict(x=x, z=z, h=h, yab=yab, qh=qh, kh=kh, vh=vh, o=o, yc=yc, lse=lse, y=outs[0])
    return saved, outs[1:]


def _layer_bwd(l, dout, sv, p, rc, rs):
    dycat, dw_out, dpost = _out_proj_bwd(dout, sv["y"], sv["yab"], sv["yc"], p["w_out"], p["post_g"], f"out_proj_bwd_{l}")
    dq, dgate, dk, dv = _attn_bwd(sv["qh"], sv["kh"], sv["vh"], sv["o"], sv["lse"], dycat, sv["z"], f"attn_bwd_{l}")
    dzc, dzk, dwq, dwkv, dgq, dgkv = _qkv_bwd(dq, dk, dv, sv["z"], rc, rs, p["w_uq"], p["w_ukv"], p["gq"], p["gkv"],
                                              f"qkv_bwd_{l}")
    dzm, dsw, dsb, dlng, dlnb, dpw, dps = _mix_bwd(sv["z"], dycat, p["sgu_w"], p["sgu_wt"], p["sgu_bias"], p["ln_g"],
                                                   p["ln_b"], p["pool_wbd"], p["pool_scale"], f"mix_bwd_{l}")
    dx, dw_in, dpre = _in_proj_bwd(dzm, dzc, dzk, dgate, sv["h"], sv["x"], dout, p["w_in"], p["pre_g"], f"in_proj_bwd_{l}")
    grads = {
        "pre_norm_g": dpre[0], "post_norm_g": dpost[0],
        "w_in": jnp.concatenate([dw_in[:, :1984], dw_in[:, 2048:]], axis=1),
        "sgu_w": dsw.reshape(4, 128, 128), "sgu_b": dsb[:, :4].T, "sgu_ln_g": dlng[0], "sgu_ln_b": dlnb[0],
        "pool_w": jnp.stack([dpw[64 * gi:64 * gi + 64, 64 * gi:64 * gi + 64] for gi in range(4)]),
        "pool_scale": dps[0], "q_norm_g": dgq[0],
        "w_uq": dwq.reshape(384, 4, 256)[:, :, :192].reshape(384, 768), "kv_norm_g": dgkv[0],
        "w_ukv": dwkv.reshape(256, 2, 4, 128).transpose(0, 2, 1, 3).reshape(256, 1024), "w_out": dw_out,
    }
    return dx, grads


SMALL_NAMES = ["pre_norm_g", "post_norm_g", "sgu_w", "sgu_b", "sgu_ln_g", "sgu_ln_b", "pool_w", "pool_scale",
               "q_norm_g", "kv_norm_g"]
BIG_NAMES = ["w_in", "w_uq", "w_ukv", "w_out"]
WEIGHT_NAMES = ["pre_norm_g", "post_norm_g", "w_in", "sgu_w", "sgu_b", "sgu_ln_g", "sgu_ln_b", "pool_w", "pool_scale",
                "q_norm_g", "w_uq", "kv_norm_g", "w_ukv", "w_out"]


def _local_step(x, positions, target, w_in, w_uq, w_ukv, w_out, small):
    rc, rs = _rope_tables(positions)
    params = [_layer_params(l, w_in, w_uq, w_ukv, w_out, small) for l in range(DEPTH)]
    saved = []
    for l in range(DEPTH):
        sv, outs = _layer_fwd(l, x, params[l], rc, rs, target if l == DEPTH - 1 else None)
        saved.append(sv)
        if l < DEPTH - 1:
            x = outs[0]
    dout, loss = outs
    grads = [None] * DEPTH
    for l in reversed(range(DEPTH)):
        dout, grads[l] = _layer_bwd(l, dout, saved[l], params[l], rc, rs)
    return loss[0, 0], dout, {k: jnp.stack([grads[l][k] for l in range(DEPTH)]) for k in WEIGHT_NAMES}


def _pack_small(tree):
    flat = jnp.concatenate([tree[k].reshape(-1) for k in SMALL_NAMES])
    rows = -(-flat.shape[0] // 1024) * 8
    return jnp.pad(flat, (0, rows * 128 - flat.shape[0])).reshape(rows, 128)


def _unpack_small(packed, like):
    flat = packed.reshape(-1)
    out, off = {}, 0
    for k in SMALL_NAMES:
        size = like[k].size
        out[k] = flat[off:off + size].reshape(like[k].shape)
        off += size
    return out


def kernel(x, positions, pre_norm_g, post_norm_g, w_in, sgu_w, sgu_b, sgu_ln_g, sgu_ln_b, pool_w, pool_scale, q_norm_g, w_uq, kv_norm_g, w_ukv, w_out, loss_target, m_pre_norm_g, m_post_norm_g, m_w_in, m_sgu_w, m_sgu_b, m_sgu_ln_g, m_sgu_ln_b, m_pool_w, m_pool_scale, m_q_norm_g, m_w_uq, m_kv_norm_g, m_w_ukv, m_w_out, v_pre_norm_g, v_post_norm_g, v_w_in, v_sgu_w, v_sgu_b, v_sgu_ln_g, v_sgu_ln_b, v_pool_w, v_pool_scale, v_q_norm_g, v_w_uq, v_kv_norm_g, v_w_ukv, v_w_out):
    w = dict(pre_norm_g=pre_norm_g, post_norm_g=post_norm_g, w_in=w_in, sgu_w=sgu_w, sgu_b=sgu_b, sgu_ln_g=sgu_ln_g,
             sgu_ln_b=sgu_ln_b, pool_w=pool_w, pool_scale=pool_scale, q_norm_g=q_norm_g, w_uq=w_uq, kv_norm_g=kv_norm_g,
             w_ukv=w_ukv, w_out=w_out)
    m = dict(pre_norm_g=m_pre_norm_g, post_norm_g=m_post_norm_g, w_in=m_w_in, sgu_w=m_sgu_w, sgu_b=m_sgu_b,
             sgu_ln_g=m_sgu_ln_g, sgu_ln_b=m_sgu_ln_b, pool_w=m_pool_w, pool_scale=m_pool_scale, q_norm_g=m_q_norm_g,
             w_uq=m_w_uq, kv_norm_g=m_kv_norm_g, w_ukv=m_w_ukv, w_out=m_w_out)
    v = dict(pre_norm_g=v_pre_norm_g, post_norm_g=v_post_norm_g, w_in=v_w_in, sgu_w=v_sgu_w, sgu_b=v_sgu_b,
             sgu_ln_g=v_sgu_ln_g, sgu_ln_b=v_sgu_ln_b, pool_w=v_pool_w, pool_scale=v_pool_scale, q_norm_g=v_q_norm_g,
             w_uq=v_w_uq, kv_norm_g=v_kv_norm_g, w_ukv=v_w_ukv, w_out=v_w_out)

    g_in, g_uq, g_ukv, g_out = _gather_weights([_mx(w[k]) for k in BIG_NAMES])
    cols = lambda g: g.reshape((4, 2) + g.shape[1:]).transpose(1, 2, 0, 3).reshape(2, g.shape[1], 4 * g.shape[2])
    full_out = g_out.reshape(4, 2, 256, 1024).transpose(1, 0, 2, 3).reshape(2, 1024, 1024)
    loss, dx, grads = _local_step(x[0], positions[0], loss_target[0], cols(g_in), cols(g_uq), cols(g_ukv), full_out, w)

    split_cols = lambda g: g.reshape(2, g.shape[1], 4, g.shape[2] // 4).transpose(2, 0, 1, 3).reshape(8, g.shape[1], g.shape[2] // 4)
    parts = [split_cols(grads["w_in"]), split_cols(grads["w_uq"]), split_cols(grads["w_ukv"]),
             grads["w_out"].reshape(2, 4, 256, 1024).transpose(1, 0, 2, 3).reshape(8, 256, 1024)]
    received = _all_to_all(parts, _pack_small(grads))
    sums = [_sum8(received[a], f"sum_{BIG_NAMES[a]}", 128) for a in range(4)]
    small_sum = _unpack_small(_sum8(received[4], "sum_small", received[4].shape[1]), w)
    both = _sibling_exchange(sums)
    total = dict(small_sum)
    for a, k in enumerate(BIG_NAMES):
        total[k] = both[a]

    packed = _adamw(_pack_small(w), _pack_small(total), _pack_small(m), _pack_small(v), "adamw_small", 2048)
    small_out = [_unpack_small(pk, w) for pk in packed]
    delta, new_m, new_v = {}, {}, {}
    for k in SMALL_NAMES:
        delta[k], new_m[k], new_v[k] = (so[k] for so in small_out)
    for k in BIG_NAMES:
        shape = w[k].shape
        flat = lambda a: a.reshape(shape[0] * shape[1], shape[2])
        res = _adamw(flat(w[k]), flat(total[k]), flat(m[k]), flat(v[k]), f"adamw_{k}", 256)
        delta[k], new_m[k], new_v[k] = (r.reshape(shape) for r in res)

    loss = lax.psum(loss, ("x", "y", "c"))
    return (loss, dx[None], *[total[k] for k in WEIGHT_NAMES], *[delta[k] for k in WEIGHT_NAMES],
            *[new_m[k] for k in WEIGHT_NAMES], *[new_v[k] for k in WEIGHT_NAMES])
```

```python
import jax
import jax.numpy as jnp
from jax import lax
from jax.experimental import pallas as pl
from jax.experimental.pallas import tpu as pltpu

F32 = jnp.float32
MXU_DTYPE = jnp.bfloat16
EPS = 1e-6
NEG_INF = -1e30
DEPTH = 2
N_HEADS = 4
QK_PAD = 256
V_DIM = 128
SCALE = 192 ** -0.5
LOG2E = 1.4426950408889634
ROPE_BASE = 10000.0
ADAM_LR, ADAM_B1, ADAM_B2, ADAM_EPS, ADAM_WD, ADAM_STEP = 0.001, 0.9, 0.999, 1e-08, 0.01, 10
VMEM_LIMIT_BYTES = 56 * 1024 * 1024
MESH = pl.DeviceIdType.MESH
ANY = pl.BlockSpec(memory_space=pl.ANY)

Z_MIX, Z_C, Z_KR, Z_GATE = 1280, 640, 128, 512
Z_W = Z_MIX + Z_C + Z_KR + Z_GATE


def _cp(n_axes=1):
    return pltpu.CompilerParams(dimension_semantics=("arbitrary",) * n_axes, vmem_limit_bytes=VMEM_LIMIT_BYTES)


def _dot(a, b):
    return lax.dot_general(a, b, (((1,), (0,)), ((), ())), preferred_element_type=F32)


def _dot_nt(a, b):
    return lax.dot_general(a, b, (((1,), (1,)), ((), ())), preferred_element_type=F32)


def _dot_tn(a, b):
    return lax.dot_general(a, b, (((0,), (0,)), ((), ())), preferred_element_type=F32)


def _mx(a):
    return a.astype(MXU_DTYPE)


def _silu_and_grad(g):
    sg = jax.nn.sigmoid(g)
    return g * sg, sg * (1.0 + g * (1.0 - sg))


def _rms(x, g):
    r = lax.rsqrt(jnp.mean(x * x, axis=-1, keepdims=True) + EPS)
    return x * r * g, r


def _rms_bwd(x, r, g, dy):
    xhat = x * r
    dyg = dy * g
    dx = r * (dyg - xhat * jnp.mean(dyg * xhat, axis=-1, keepdims=True))
    return dx, dy * xhat


def _acc(ref, val, first):
    @pl.when(first)
    def _():
        ref[...] = val

    @pl.when(jnp.logical_not(first))
    def _():
        ref[...] += val


def _colsum(a):
    return jnp.sum(a, axis=0, keepdims=True)


def _in_proj_fwd(x, g, w, name, tm=512):
    t, d = x.shape
    n = w.shape[1]
    tm = min(tm, t)

    def body(x_ref, g_ref, w_ref, z_ref, h_ref):
        h, _ = _rms(x_ref[...], g_ref[...])
        h = _mx(h)
        h_ref[...] = h
        z_ref[...] = _dot(h, w_ref[...])

    return pl.pallas_call(
        body, name=name, grid=(t // tm,),
        in_specs=[pl.BlockSpec((tm, d), lambda i: (i, 0)), pl.BlockSpec((1, d), lambda i: (0, 0)),
                  pl.BlockSpec((d, n), lambda i: (0, 0))],
        out_specs=[pl.BlockSpec((tm, n), lambda i: (i, 0)), pl.BlockSpec((tm, d), lambda i: (i, 0))],
        out_shape=[jax.ShapeDtypeStruct((t, n), F32), jax.ShapeDtypeStruct((t, d), MXU_DTYPE)],
        compiler_params=_cp())(x, g, w)


def _in_proj_bwd(dz_mix, dz_c, dz_kr, dz_gate, h, x, d_res, w, g, name, tm=256):
    t, d = x.shape
    n = w.shape[1]
    tm = min(tm, t)

    def body(dm_ref, dc_ref, dk_ref, dg_ref, h_ref, x_ref, dres_ref, w_ref, g_ref, dx_ref, dw_ref, dgn_ref):
        first = pl.program_id(0) == 0
        dz = jnp.concatenate([dm_ref[...], dc_ref[...], dk_ref[...], dg_ref[...]], axis=1)

        @pl.when(first)
        def _():
            dw_ref[...] = jnp.zeros(dw_ref.shape, F32)

        hb = h_ref[...]
        for c0 in range(0, n, 512):
            dw_ref[:, c0:c0 + 512] += _dot_tn(hb, dz[:, c0:c0 + 512])
        dh = _dot_nt(dz, w_ref[...])
        xf = x_ref[...]
        r = lax.rsqrt(jnp.mean(xf * xf, axis=-1, keepdims=True) + EPS)
        dx, dgt = _rms_bwd(xf, r, g_ref[...], dh)
        dx_ref[...] = dx + dres_ref[...]
        _acc(dgn_ref, _colsum(dgt), first)

    row = lambda wd: pl.BlockSpec((tm, wd), lambda i: (i, 0))
    fixed = lambda a, b: pl.BlockSpec((a, b), lambda i: (0, 0))
    return pl.pallas_call(
        body, name=name, grid=(t // tm,),
        in_specs=[row(Z_MIX), row(Z_C), row(Z_KR), row(Z_GATE), row(d), row(d), row(d), fixed(d, n), fixed(1, d)],
        out_specs=[row(d), fixed(d, n), fixed(1, d)],
        out_shape=[jax.ShapeDtypeStruct((t, d), F32), jax.ShapeDtypeStruct((d, n), F32),
                   jax.ShapeDtypeStruct((1, d), F32)],
        compiler_params=_cp())(dz_mix, dz_c, dz_kr, dz_gate, h, x, d_res, w, g)


def _lane_group(shape):
    return lax.broadcasted_iota(jnp.int32, shape, 1) // 64


def _select_group(vals):
    grp = _lane_group(vals[0].shape)
    out = vals[3]
    for gi in (2, 1, 0):
        out = jnp.where(grp == gi, vals[gi], out)
    return out


def _sgu_mask(transposed):
    r = (lax.broadcasted_iota(jnp.int32, (512, 128), 0) % 128) // 64
    c = lax.broadcasted_iota(jnp.int32, (512, 128), 1) // 64
    return (r <= c) if transposed else (c <= r)


def _sgu_apply(wstack, vb, nblk):
    outs = []
    for n in range(nblk):
        r = _dot(wstack, vb[n * 128:(n + 1) * 128, :])
        outs.append(_select_group([r[hh * 128:(hh + 1) * 128, :] for hh in range(4)]))
    return jnp.concatenate(outs, axis=0)


def _layer_norm(v, g, b):
    mu = jnp.mean(v, axis=-1, keepdims=True)
    vc = v - mu
    rstd = lax.rsqrt(jnp.mean(vc * vc, axis=-1, keepdims=True) + EPS)
    vhat = vc * rstd
    return vhat * g + b, vhat, rstd


def _pool_counts(t0, n):
    t = t0 + lax.broadcasted_iota(jnp.int32, (n, 256), 0)
    w = _select_group([jnp.full((n, 256), wv, jnp.int32) for wv in (2, 4, 8, 16)])
    return jnp.minimum(t + 1, w).astype(F32)


def _pooled(p, halo, t0):
    tm = p.shape[0]
    ext = jnp.concatenate([halo, p], axis=0)
    s2 = ext + pltpu.roll(ext, 1, 0)
    s4 = s2 + pltpu.roll(s2, 2, 0)
    s8 = s4 + pltpu.roll(s4, 4, 0)
    s16 = s8 + pltpu.roll(s8, 8, 0)
    sel = _select_group([s2, s4, s8, s16])[16:, :]
    return sel / _pool_counts(t0, tm) - p


def _pooled_bwd(dpool, dpool_halo, t0):
    tm = dpool.shape[0]
    n = tm + 16
    ext = jnp.concatenate([dpool, dpool_halo], axis=0) / _pool_counts(t0, n)
    f2 = ext + pltpu.roll(ext, n - 1, 0)
    f4 = f2 + pltpu.roll(f2, n - 2, 0)
    f8 = f4 + pltpu.roll(f4, n - 4, 0)
    f16 = f8 + pltpu.roll(f8, n - 8, 0)
    return _select_group([f2, f4, f8, f16])[:tm, :] - dpool


def _mix_specs(t, tm):
    nt16 = t // 16
    zrow = pl.BlockSpec((tm, Z_MIX), lambda i: (i, 0))
    prev_halo = pl.BlockSpec((16, 256), lambda i: (jnp.maximum(i * (tm // 16) - 1, 0), 3))
    fixed = lambda a, b: pl.BlockSpec((a, b), lambda i: (0, 0))
    params = [fixed(512, 128), fixed(128, 256), fixed(1, 256), fixed(1, 256), fixed(256, 256), fixed(1, 256)]
    return nt16, zrow, prev_halo, fixed, params


def _mix_fwd(z, sgu_w, sgu_bias, ln_g, ln_b, pool_wbd, pool_scale, name, tm=256):
    t = z.shape[0]
    tm = min(tm, t)
    _, zrow, prev_halo, _, params = _mix_specs(t, tm)

    def body(z_ref, halo_ref, w_ref, bias_ref, lng_ref, lnb_ref, pw_ref, ps_ref, y_ref):
        i = pl.program_id(0)
        u, v, gate = z_ref[:, 0:256], z_ref[:, 256:512], z_ref[:, 512:768]
        p, pgate = z_ref[:, 768:1024], z_ref[:, 1024:1280]
        vn, _, _ = _layer_norm(v, lng_ref[...], lnb_ref[...])
        wm = _mx(jnp.where(_sgu_mask(False), w_ref[...], 0.0))
        mixed = _sgu_apply(wm, _mx(vn), tm // 128) + jnp.tile(bias_ref[...], (tm // 128, 1))
        ya = u * mixed * _silu_and_grad(gate)[0]
        halo = jnp.where(i > 0, halo_ref[...], 0.0)
        pooled = _pooled(p, halo, i * tm)
        yb = _dot(_mx(pooled), pw_ref[...]) * ps_ref[...] * _silu_and_grad(pgate)[0]
        y_ref[...] = _mx(jnp.concatenate([ya, yb], axis=1))

    return pl.pallas_call(
        body, name=name, grid=(t // tm,),
        in_specs=[zrow, prev_halo] + params,
        out_specs=pl.BlockSpec((tm, 512), lambda i: (i, 0)),
        out_shape=jax.ShapeDtypeStruct((t, 512), MXU_DTYPE),
        compiler_params=_cp())(z, z, sgu_w, sgu_bias, ln_g, ln_b, pool_wbd, pool_scale)


def _mix_bwd(z, dycat, sgu_w, sgu_wt, sgu_bias, ln_g, ln_b, pool_wbd, pool_scale, name, tm=256):
    t = z.shape[0]
    tm = min(tm, t)
    nt16, zrow, prev_halo, fixed, params = _mix_specs(t, tm)
    nblk = tm // 128
    last = t // tm - 1

    def body(z_ref, halo_ref, zn_ref, dy_ref, dyn_ref, w_ref, wt_ref, bias_ref, lng_ref, lnb_ref, pw_ref, ps_ref,
             dz_ref, dw_ref, db_ref, dlng_ref, dlnb_ref, dpw_ref, dps_ref):
        i = pl.program_id(0)
        first = i == 0
        u, v, gate = z_ref[:, 0:256], z_ref[:, 256:512], z_ref[:, 512:768]
        p, pgate = z_ref[:, 768:1024], z_ref[:, 1024:1280]
        dya, dyb = dy_ref[:, 0:256], dy_ref[:, 256:512]
        vn, vhat, rstd = _layer_norm(v, lng_ref[...], lnb_ref[...])
        vnb = _mx(vn)
        wm = _mx(jnp.where(_sgu_mask(False), w_ref[...], 0.0))
        wmt = _mx(jnp.where(_sgu_mask(True), wt_ref[...], 0.0))
        mixed = _sgu_apply(wm, vnb, nblk) + jnp.tile(bias_ref[...], (nblk, 1))
        silu, dsilu = _silu_and_grad(gate)
        t1 = u * mixed
        d_gate = dya * t1 * dsilu
        d_t1 = dya * silu
        d_u = d_t1 * mixed
        d_mixed = d_t1 * u
        dmb = _mx(d_mixed)
        d_vn = _sgu_apply(wmt, dmb, nblk)
        grp = _lane_group((128, 256))
        lane = lax.broadcasted_iota(jnp.int32, (128, 128), 1)
        dws = [jnp.zeros((128, 128), F32) for _ in range(4)]
        dbias = jnp.zeros((128, 128), F32)
        for n in range(nblk):
            dm_n, dmb_n, vnb_n = d_mixed[n * 128:(n + 1) * 128], dmb[n * 128:(n + 1) * 128], vnb[n * 128:(n + 1) * 128]
            for hh in range(4):
                dws[hh] = dws[hh] + _dot_nt(jnp.where(grp == hh, dmb_n, jnp.zeros_like(dmb_n)), vnb_n)
                rs = jnp.sum(jnp.where(grp == hh, dm_n, 0.0), axis=-1, keepdims=True)
                dbias = dbias + jnp.where(lane == hh, rs, 0.0)
        _acc(dw_ref, jnp.concatenate(dws, axis=0), first)
        _acc(db_ref, dbias, first)
        _acc(dlng_ref, _colsum(d_vn * vhat), first)
        _acc(dlnb_ref, _colsum(d_vn), first)
        dvh = d_vn * lng_ref[...]
        d_v = rstd * (dvh - jnp.mean(dvh, axis=-1, keepdims=True) - vhat * jnp.mean(dvh * vhat, axis=-1, keepdims=True))

        @pl.when(i == last)
        def _():
            dw_ref[...] = jnp.where(_sgu_mask(False), dw_ref[...], 0.0)

        halo = jnp.where(i > 0, halo_ref[...], 0.0)
        pooled = _pooled(p, halo, i * tm)
        pooled_b = _mx(pooled)
        mixedp = _dot(pooled_b, pw_ref[...])
        psilu, pdsilu = _silu_and_grad(pgate)
        d_pgate = dyb * (mixedp * ps_ref[...]) * pdsilu
        d_ms = dyb * psilu
        _acc(dps_ref, _colsum(d_ms * mixedp), first)
        dmpb = _mx(d_ms * ps_ref[...])
        _acc(dpw_ref, _dot_tn(pooled_b, dmpb), first)
        d_pooled = _dot_nt(dmpb, pw_ref[...])
        dmp_halo = _mx(dyn_ref[...] * _silu_and_grad(zn_ref[...])[0] * ps_ref[...])
        d_pooled_halo = jnp.where(i < last, _dot_nt(dmp_halo, pw_ref[...]), 0.0)
        d_p = _pooled_bwd(d_pooled, d_pooled_halo, i * tm)
        dz_ref[...] = _mx(jnp.concatenate([d_u, d_v, d_gate, d_p, d_pgate], axis=1))

    nxt = lambda i: jnp.minimum((i + 1) * (tm // 16), nt16 - 1)
    return pl.pallas_call(
        body, name=name, grid=(t // tm,),
        in_specs=[zrow, prev_halo, pl.BlockSpec((16, 256), lambda i: (nxt(i), 4)),
                  pl.BlockSpec((tm, 512), lambda i: (i, 0)), pl.BlockSpec((16, 256), lambda i: (nxt(i), 1)),
                  params[0], fixed(512, 128)] + params[1:],
        out_specs=[pl.BlockSpec((tm, Z_MIX), lambda i: (i, 0)), fixed(512, 128), fixed(128, 128), fixed(1, 256),
                   fixed(1, 256), fixed(256, 256), fixed(1, 256)],
        out_shape=[jax.ShapeDtypeStruct((t, Z_MIX), MXU_DTYPE), jax.ShapeDtypeStruct((512, 128), F32),
                   jax.ShapeDtypeStruct((128, 128), F32), jax.ShapeDtypeStruct((1, 256), F32),
                   jax.ShapeDtypeStruct((1, 256), F32), jax.ShapeDtypeStruct((256, 256), F32),
                   jax.ShapeDtypeStruct((1, 256), F32)],
        compiler_params=_cp())(z, z, z, dycat, dycat, sgu_w, sgu_wt, sgu_bias, ln_g, ln_b, pool_wbd, pool_scale)


def _rot_half(x, transpose):
    w = x.shape[1]
    lane = lax.broadcasted_iota(jnp.int32, x.shape, 1) % min(w, 256)
    base = 128 if w >= 256 else 0
    lo = jnp.logical_and(lane >= base, lane < base + 32)
    hi = jnp.logical_and(lane >= base + 32, lane < base + 64)
    up = pltpu.roll(x, w - 32, 1)
    down = pltpu.roll(x, 32, 1)
    if transpose:
        return jnp.where(lo, up, jnp.where(hi, -down, 0.0))
    return jnp.where(lo, -up, jnp.where(hi, down, 0.0))


def _rope(x, c, s):
    return x * c + _rot_half(x, False) * s


def _rope_bwd(dy, c, s):
    return dy * c + _rot_half(dy * s, True)


def _qkv_fwd(z, rc, rs, w_uq, w_ukv, gq, gkv, name, tm=256):
    t = z.shape[0]
    tm = min(tm, t)

    def body(zc_ref, zk_ref, rc_ref, rs_ref, wq_ref, wkv_ref, gq_ref, gkv_ref, q_ref, k_ref, v_ref):
        cq, ckv = zc_ref[:, 0:384], zc_ref[:, 384:640]
        c, s = rc_ref[...], rs_ref[...]
        qn, _ = _rms(cq, gq_ref[...])
        q = _rope(_dot(_mx(qn), wq_ref[...]), jnp.tile(c, (1, N_HEADS)), jnp.tile(s, (1, N_HEADS)))
        kvn, _ = _rms(ckv, gkv_ref[...])
        kv = _dot(_mx(kvn), wkv_ref[...])
        kpe = _rope(zk_ref[...], c[:, 128:256], s[:, 128:256])
        for hh in range(N_HEADS):
            q_ref[hh] = _mx(q[:, hh * QK_PAD:(hh + 1) * QK_PAD])
            k_ref[hh] = _mx(jnp.concatenate([kv[:, hh * 128:(hh + 1) * 128], kpe], axis=1))
            v_ref[hh] = _mx(kv[:, 512 + hh * 128:512 + (hh + 1) * 128])

    fixed = lambda a, b: pl.BlockSpec((a, b), lambda i: (0, 0))
    heads = lambda wd: pl.BlockSpec((N_HEADS, tm, wd), lambda i: (0, i, 0))
    return pl.pallas_call(
        body, name=name, grid=(t // tm,),
        in_specs=[pl.BlockSpec((tm, Z_C), lambda i: (i, Z_MIX // Z_C)),
                  pl.BlockSpec((tm, Z_KR), lambda i: (i, (Z_MIX + Z_C) // Z_KR)),
                  pl.BlockSpec((tm, 256), lambda i: (i, 0)), pl.BlockSpec((tm, 256), lambda i: (i, 0)),
                  fixed(384, 1024), fixed(256, 1024), fixed(1, 384), fixed(1, 256)],
        out_specs=[heads(QK_PAD), heads(QK_PAD), heads(V_DIM)],
        out_shape=[jax.ShapeDtypeStruct((N_HEADS, t, QK_PAD), MXU_DTYPE),
                   jax.ShapeDtypeStruct((N_HEADS, t, QK_PAD), MXU_DTYPE),
                   jax.ShapeDtypeStruct((N_HEADS, t, V_DIM), MXU_DTYPE)],
        compiler_params=_cp())(z, z, rc, rs, w_uq, w_ukv, gq, gkv)


def _qkv_bwd(dq, dk, dv, z, rc, rs, w_uq, w_ukv, gq, gkv, name, tm=256):
    t = z.shape[0]
    tm = min(tm, t)

    def body(dq_ref, dk_ref, dv_ref, zc_ref, rc_ref, rs_ref, wq_ref, wkv_ref, gq_ref, gkv_ref,
             dzc_ref, dzk_ref, dwq_ref, dwkv_ref, dgq_ref, dgkv_ref):
        first = pl.program_id(0) == 0
        cq, ckv = zc_ref[:, 0:384], zc_ref[:, 384:640]
        c, s = rc_ref[...], rs_ref[...]
        dq_all = jnp.concatenate([dq_ref[hh] for hh in range(N_HEADS)], axis=1)
        dqp = _mx(_rope_bwd(dq_all, jnp.tile(c, (1, N_HEADS)), jnp.tile(s, (1, N_HEADS))))
        qn, rq = _rms(cq, gq_ref[...])
        _acc(dwq_ref, _dot_tn(_mx(qn), dqp), first)
        d_cq, dgq_t = _rms_bwd(cq, rq, gq_ref[...], _dot_nt(dqp, wq_ref[...]))
        _acc(dgq_ref, _colsum(dgq_t), first)
        dkpe = dk_ref[0][:, 128:256]
        for hh in range(1, N_HEADS):
            dkpe = dkpe + dk_ref[hh][:, 128:256]
        dzk_ref[...] = _mx(_rope_bwd(dkpe, c[:, 128:256], s[:, 128:256]))
        dkv = _mx(jnp.concatenate([dk_ref[hh][:, 0:128] for hh in range(N_HEADS)]
                                  + [dv_ref[hh] for hh in range(N_HEADS)], axis=1))
        kvn, rkv = _rms(ckv, gkv_ref[...])
        _acc(dwkv_ref, _dot_tn(_mx(kvn), dkv), first)
        d_ckv, dgkv_t = _rms_bwd(ckv, rkv, gkv_ref[...], _dot_nt(dkv, wkv_ref[...]))
        _acc(dgkv_ref, _colsum(dgkv_t), first)
        dzc_ref[...] = _mx(jnp.concatenate([d_cq, d_ckv], axis=1))

    fixed = lambda a, b: pl.BlockSpec((a, b), lambda i: (0, 0))
    heads = lambda wd: pl.BlockSpec((N_HEADS, tm, wd), lambda i: (0, i, 0))
    return pl.pallas_call(
        body, name=name, grid=(t // tm,),
        in_specs=[heads(QK_PAD), heads(QK_PAD), heads(V_DIM), pl.BlockSpec((tm, Z_C), lambda i: (i, Z_MIX // Z_C)),
                  pl.BlockSpec((tm, 256), lambda i: (i, 0)), pl.BlockSpec((tm, 256), lambda i: (i, 0)),
                  fixed(384, 1024), fixed(256, 1024), fixed(1, 384), fixed(1, 256)],
        out_specs=[pl.BlockSpec((tm, Z_C), lambda i: (i, 0)), pl.BlockSpec((tm, Z_KR), lambda i: (i, 0)),
                   fixed(384, 1024), fixed(256, 1024), fixed(1, 384), fixed(1, 256)],
        out_shape=[jax.ShapeDtypeStruct((t, Z_C), MXU_DTYPE), jax.ShapeDtypeStruct((t, Z_KR), MXU_DTYPE),
                   jax.ShapeDtypeStruct((384, 1024), F32), jax.ShapeDtypeStruct((256, 1024), F32),
                   jax.ShapeDtypeStruct((1, 384), F32), jax.ShapeDtypeStruct((1, 256), F32)],
        compiler_params=_cp())(dq, dk, dv, z, rc, rs, w_uq, w_ukv, gq, gkv)


def _chunk_mask(tq):
    r = lax.broadcasted_iota(jnp.int32, (tq, tq), 0) // 64
    c = lax.broadcasted_iota(jnp.int32, (tq, tq), 1) // 64
    return c <= r


def _gate_block(tq):
    return pl.BlockSpec((tq, 128), lambda h, i: (i, (Z_MIX + Z_C + Z_KR) // 128 + h))


def _attn_fwd(qh, kh, vh, z, name, tq=512):
    t = qh.shape[1]
    tq = min(tq, t)

    def body(q_ref, g_ref, k_hbm, v_hbm, o_ref, yc_ref, lse_ref, k_v, v_v, m_s, acc_s, s_a, s_b, sem):
        h, i = pl.program_id(0), pl.program_id(1)

        @pl.when(i == 0)
        def _():
            ck = pltpu.make_async_copy(k_hbm.at[h], k_v, sem.at[0])
            cv = pltpu.make_async_copy(v_hbm.at[h], v_v.at[:, 0:V_DIM], sem.at[1])
            ck.start()
            cv.start()
            v_v[:, V_DIM:2 * V_DIM] = jnp.ones((t, V_DIM), MXU_DTYPE)
            ck.wait()
            cv.wait()

        q = q_ref[...]
        m_s[...] = jnp.full(m_s.shape, NEG_INF, F32)
        acc_s[...] = jnp.zeros(acc_s.shape, F32)

        def keys(j):
            return pl.ds(pl.multiple_of(j * tq, tq), tq)

        def scores(s_ref, j):
            s_ref[...] = _dot_nt(q, k_v[keys(j), :]) * (SCALE * LOG2E)

        def softmax_pv(s_ref, j, masked):
            s = s_ref[...]
            if masked:
                s = jnp.where(_chunk_mask(tq), s, NEG_INF)
            m_old = m_s[...]
            m_new = jnp.maximum(m_old, jnp.max(s, axis=-1, keepdims=True))
            p = jnp.exp2(s - jnp.tile(m_new, (1, tq // 128)))
            alpha = jnp.exp2(m_old - m_new)
            m_s[...] = m_new
            acc_s[...] = jnp.tile(alpha, (1, 2)) * acc_s[...] + _dot(_mx(p), v_v[keys(j), :])

        scores(s_a, 0)

        def pair(pp, carry):
            scores(s_b, 2 * pp + 1)
            softmax_pv(s_a, 2 * pp, False)
            scores(s_a, 2 * pp + 2)
            softmax_pv(s_b, 2 * pp + 1, False)
            return carry

        lax.fori_loop(0, i // 2, pair, 0)

        @pl.when(i % 2 == 1)
        def _():
            scores(s_b, i)
            softmax_pv(s_a, i - 1, False)
            softmax_pv(s_b, i, True)

        @pl.when(i % 2 == 0)
        def _():
            softmax_pv(s_a, i, True)

        l = acc_s[:, V_DIM:2 * V_DIM]
        o = acc_s[:, 0:V_DIM] / l
        o_ref[...] = o
        yc_ref[...] = _mx(o * _silu_and_grad(g_ref[...])[0])
        lse_ref[...] = m_s[...] + jnp.log2(l)

    return pl.pallas_call(
        body, name=name, grid=(N_HEADS, t // tq),
        in_specs=[pl.BlockSpec((None, tq, QK_PAD), lambda h, i: (h, i, 0)), _gate_block(tq), ANY, ANY],
        out_specs=[pl.BlockSpec((tq, 128), lambda h, i: (i, h)), pl.BlockSpec((tq, 128), lambda h, i: (i, h)),
                   pl.BlockSpec((None, tq, 128), lambda h, i: (h, i, 0))],
        out_shape=[jax.ShapeDtypeStruct((t, N_HEADS * V_DIM), F32), jax.ShapeDtypeStruct((t, N_HEADS * V_DIM), MXU_DTYPE),
                   jax.ShapeDtypeStruct((N_HEADS, t, 128), F32)],
        scratch_shapes=[pltpu.VMEM((t, QK_PAD), MXU_DTYPE), pltpu.VMEM((t, 2 * V_DIM), MXU_DTYPE),
                        pltpu.VMEM((tq, 128), F32), pltpu.VMEM((tq, 2 * V_DIM), F32),
                        pltpu.VMEM((tq, tq), F32), pltpu.VMEM((tq, tq), F32), pltpu.SemaphoreType.DMA((2,))],
        compiler_params=_cp(2))(qh, z, kh, vh)


def _attn_bwd(qh, kh, vh, o, lse, dycat, z, name, tq=512):
    t = qh.shape[1]
    tq = min(tq, t)
    nq = t // tq

    def body(q_ref, o_ref, lse_ref, dy_ref, g_ref, k_hbm, v_hbm, dq_ref, dgate_ref, dk_hbm, dv_hbm,
             k_v, v_v, dk_acc, dv_acc, dq_acc, delta_s, s_a, dp_a, s_b, dp_b, sem):
        h, i = pl.program_id(0), pl.program_id(1)

        @pl.when(i == 0)
        def _():
            ck = pltpu.make_async_copy(k_hbm.at[h], k_v, sem.at[0])
            cv = pltpu.make_async_copy(v_hbm.at[h], v_v, sem.at[1])
            ck.start()
            cv.start()
            dk_acc[...] = jnp.zeros(dk_acc.shape, F32)
            dv_acc[...] = jnp.zeros(dv_acc.shape, F32)
            ck.wait()
            cv.wait()

        gate, dy, of = g_ref[...], dy_ref[...], o_ref[...]
        silu, dsilu = _silu_and_grad(gate)
        do = dy * silu
        delta = jnp.sum(do * of, axis=-1, keepdims=True)
        dgate_ref[...] = _mx(dy * of * dsilu)
        dob = _mx(do)
        q = q_ref[...]
        delta_s[...] = jnp.broadcast_to(delta, delta_s.shape)
        dq_acc[...] = jnp.zeros(dq_acc.shape, F32)

        def keys(j):
            return pl.ds(pl.multiple_of(j * tq, tq), tq)

        def scores(s_ref, dp_ref, j):
            s_ref[...] = _dot_nt(q, k_v[keys(j), :]) * (SCALE * LOG2E)
            dp_ref[...] = _dot_nt(dob, v_v[keys(j), :])

        def grads(s_ref, dp_ref, j, masked):
            ks = keys(j)
            p = jnp.exp2(s_ref[...] - jnp.tile(lse_ref[...], (1, tq // 128)))
            if masked:
                p = jnp.where(_chunk_mask(tq), p, 0.0)
            ds = p * (dp_ref[...] - jnp.tile(delta_s[...], (1, tq // 128))) * SCALE
            pb, dsb = _mx(p), _mx(ds)
            dq_acc[...] += _dot(dsb, k_v[ks, :])
            dk_acc[ks, :] += _dot_tn(dsb, q)
            dv_acc[ks, :] += _dot_tn(pb, dob)

        scores(s_a, dp_a, 0)

        def pair(pp, carry):
            scores(s_b, dp_b, 2 * pp + 1)
            grads(s_a, dp_a, 2 * pp, False)
            scores(s_a, dp_a, 2 * pp + 2)
            grads(s_b, dp_b, 2 * pp + 1, False)
            return carry

        lax.fori_loop(0, i // 2, pair, 0)

        @pl.when(i % 2 == 1)
        def _():
            scores(s_b, dp_b, i)
            grads(s_a, dp_a, i - 1, False)
            grads(s_b, dp_b, i, True)

        @pl.when(i % 2 == 0)
        def _():
            grads(s_a, dp_a, i, True)

        dq_ref[...] = dq_acc[...]

        @pl.when(i == nq - 1)
        def _():
            ck = pltpu.make_async_copy(dk_acc, dk_hbm.at[h], sem.at[0])
            cv = pltpu.make_async_copy(dv_acc, dv_hbm.at[h], sem.at[1])
            ck.start()
            cv.start()
            ck.wait()
            cv.wait()

    return pl.pallas_call(
        body, name=name, grid=(N_HEADS, nq),
        in_specs=[pl.BlockSpec((None, tq, QK_PAD), lambda h, i: (h, i, 0)),
                  pl.BlockSpec((tq, 128), lambda h, i: (i, h)),
                  pl.BlockSpec((None, tq, 128), lambda h, i: (h, i, 0)),
                  pl.BlockSpec((tq, 128), lambda h, i: (i, N_HEADS + h)), _gate_block(tq), ANY, ANY],
        out_specs=[pl.BlockSpec((None, tq, QK_PAD), lambda h, i: (h, i, 0)),
                   pl.BlockSpec((tq, 128), lambda h, i: (i, h)), ANY, ANY],
        out_shape=[jax.ShapeDtypeStruct((N_HEADS, t, QK_PAD), F32), jax.ShapeDtypeStruct((t, Z_GATE), MXU_DTYPE),
                   jax.ShapeDtypeStruct((N_HEADS, t, QK_PAD), F32), jax.ShapeDtypeStruct((N_HEADS, t, V_DIM), F32)],
        scratch_shapes=[pltpu.VMEM((t, QK_PAD), MXU_DTYPE), pltpu.VMEM((t, V_DIM), MXU_DTYPE),
                        pltpu.VMEM((t, QK_PAD), F32), pltpu.VMEM((t, V_DIM), F32), pltpu.VMEM((tq, QK_PAD), F32),
                        pltpu.VMEM((tq, 128), F32)] + [pltpu.VMEM((tq, tq), F32)] * 4
        + [pltpu.SemaphoreType.DMA((2,))],
        compiler_params=_cp(2))(qh, o, lse, dycat, z, kh, vh)


def _out_proj_fwd(yab, yc, w, x, g, target, name, tm=512):
    t, d = x.shape
    tm = min(tm, t)
    is_last = target is not None

    def body(*refs):
        if is_last:
            yab_ref, yc_ref, w_ref, x_ref, g_ref, t_ref, y_ref, dout_ref, loss_ref = refs
        else:
            yab_ref, yc_ref, w_ref, x_ref, g_ref, y_ref, out_ref = refs
        y = _dot(jnp.concatenate([yab_ref[...], yc_ref[...]], axis=1), w_ref[...])
        y_ref[...] = y
        out = x_ref[...] + _rms(y, g_ref[...])[0]
        if is_last:
            diff = out - t_ref[...]
            dout_ref[...] = diff * (1.0 / d)
            part = jnp.sum(jnp.sum(diff * diff, axis=-1, keepdims=True), axis=0, keepdims=True) * (0.5 / d)
            _acc(loss_ref, jnp.broadcast_to(part, (1, 128)), pl.program_id(0) == 0)
        else:
            out_ref[...] = out

    row = lambda wd: pl.BlockSpec((tm, wd), lambda i: (i, 0))
    fixed = lambda a, b: pl.BlockSpec((a, b), lambda i: (0, 0))
    in_specs = [row(512), row(512), fixed(d, d), row(d), fixed(1, d)]
    args = [yab, yc, w, x, g]
    out_specs = [row(d), row(d)]
    out_shape = [jax.ShapeDtypeStruct((t, d), F32), jax.ShapeDtypeStruct((t, d), F32)]
    if is_last:
        in_specs.append(row(d))
        args.append(target)
        out_specs.append(fixed(1, 128))
        out_shape.append(jax.ShapeDtypeStruct((1, 128), F32))
    return pl.pallas_call(body, name=name, grid=(t // tm,), in_specs=in_specs, out_specs=out_specs,
                          out_shape=out_shape, compiler_params=_cp())(*args)


def _out_proj_bwd(dout, y, yab, yc, w, g, name, tm=512):
    t, d = y.shape
    tm = min(tm, t)

    def body(dout_ref, y_ref, yab_ref, yc_ref, w_ref, g_ref, dycat_ref, dw_ref, dg_ref):
        first = pl.program_id(0) == 0
        y = y_ref[...]
        r = lax.rsqrt(jnp.mean(y * y, axis=-1, keepdims=True) + EPS)
        dy, dgt = _rms_bwd(y, r, g_ref[...], dout_ref[...])
        _acc(dg_ref, _colsum(dgt), first)
        dyb = _mx(dy)
        _acc(dw_ref, _dot_tn(jnp.concatenate([yab_ref[...], yc_ref[...]], axis=1), dyb), first)
        dycat_ref[...] = _dot_nt(dyb, w_ref[...])

    row = lambda wd: pl.BlockSpec((tm, wd), lambda i: (i, 0))
    fixed = lambda a, b: pl.BlockSpec((a, b), lambda i: (0, 0))
    return pl.pallas_call(
        body, name=name, grid=(t // tm,),
        in_specs=[row(d), row(d), row(512), row(512), fixed(d, d), fixed(1, d)],
        out_specs=[row(d), fixed(d, d), fixed(1, d)],
        out_shape=[jax.ShapeDtypeStruct((t, d), F32), jax.ShapeDtypeStruct((d, d), F32),
                   jax.ShapeDtypeStruct((1, d), F32)],
        compiler_params=_cp())(dout, y, yab, yc, w, g)


def _mesh_pos():
    return lax.axis_index("x"), lax.axis_index("y"), lax.axis_index("c")


def _remote(src, dst, send_sem, recv_sem, to):
    return pltpu.make_async_remote_copy(src_ref=src, dst_ref=dst, send_sem=send_sem, recv_sem=recv_sem,
                                        device_id=to, device_id_type=MESH)


CHUNK_ROWS = 256


def _pieces(rows):
    return [(s, min(CHUNK_ROWS, rows - s)) for s in range(0, rows, CHUNK_ROWS)]


def _piece_table(shapes):
    return [(a, s, sz) for a, shp in enumerate(shapes) for s, sz in _pieces(shp[-2])]


def _gather_weights(shards):
    n = len(shards)
    table = _piece_table([s.shape for s in shards])
    npc = len(table)

    def body(*refs):
        ins, outs = refs[:n], refs[n:2 * n]
        send_sems, recv_sems, fwd_send, fwd_recv, local_sems = refs[2 * n:]
        x, y, c = _mesh_pos()
        me, sibling = (x, y, c), (x, y, 1 - c)
        chips = [(1 - x, y), (x, 1 - y), (1 - x, 1 - y)]
        slot = lambda cx, cy, layer: 2 * (2 * cx + cy) + layer
        local = []
        for a in range(n):
            for layer in range(2):
                local.append(pltpu.make_async_copy(ins[a].at[layer], outs[a].at[slot(x, y, layer)],
                                                   local_sems.at[a, layer]))
                local[-1].start()
        first = []
        for a in range(n):
            for j, (cx, cy) in enumerate(chips):
                first.append(_remote(ins[a].at[c], outs[a].at[slot(x, y, c)], send_sems.at[a, j], recv_sems.at[a, j],
                                     (cx, cy, c)))
                first[-1].start()
        passed = []
        for j, (cx, cy) in enumerate(chips):
            for a in range(n):
                blk = outs[a].at[slot(cx, cy, c)]
                _remote(blk, blk, send_sems.at[a, j], recv_sems.at[a, j], me).wait_recv()
            for q, (a, s, sz) in enumerate(table):
                rows = outs[a].at[slot(cx, cy, c), pl.ds(s, sz)]
                passed.append(_remote(rows, rows, fwd_send.at[j, q], fwd_recv.at[j, q], sibling))
                passed[-1].start()
        for j, (cx, cy) in enumerate(chips):
            for q, (a, s, sz) in enumerate(table):
                rows = outs[a].at[slot(cx, cy, 1 - c), pl.ds(s, sz)]
                _remote(rows, rows, fwd_send.at[j, q], fwd_recv.at[j, q], me).wait_recv()
        for cp in first + passed:
            cp.wait_send()
        for cp in local:
            cp.wait()

    return pl.pallas_call(
        body, name="gather_weights", in_specs=[ANY] * n, out_specs=[ANY] * n,
        out_shape=[jax.ShapeDtypeStruct((8,) + s.shape[1:], s.dtype) for s in shards],
        scratch_shapes=[pltpu.SemaphoreType.DMA((n, 3)), pltpu.SemaphoreType.DMA((n, 3)),
                        pltpu.SemaphoreType.DMA((3, npc)), pltpu.SemaphoreType.DMA((3, npc)),
                        pltpu.SemaphoreType.DMA((n, 2))])(*shards)


def _pair_exchange(parts, common):
    n = len(parts)
    table = _piece_table([p.shape for p in parts] + [common.shape])
    npc = len(table)

    def body(*refs):
        ins, outs = refs[:n + 1], refs[n + 1:2 * n + 2]
        send_sems, recv_sems = refs[2 * n + 2:]
        x, y, c = _mesh_pos()
        sent = []
        for k in range(4):
            for q, (a, s, sz) in enumerate(table):
                if a == n and k > 0:
                    continue
                src = ins[a].at[2 * k + 1 - c, pl.ds(s, sz)] if a < n else ins[a].at[pl.ds(s, sz)]
                dst = outs[a].at[k, pl.ds(s, sz)] if a < n else outs[a].at[pl.ds(s, sz)]
                sent.append(_remote(src, dst, send_sems.at[k, q], recv_sems.at[k, q], (x, y, 1 - c)))
                sent[-1].start()
        for k in range(4):
            for q, (a, s, sz) in enumerate(table):
                if a == n and k > 0:
                    continue
                dst = outs[a].at[k, pl.ds(s, sz)] if a < n else outs[a].at[pl.ds(s, sz)]
                _remote(dst, dst, send_sems.at[k, q], recv_sems.at[k, q], (x, y, c)).wait_recv()
        for cp in sent:
            cp.wait_send()

    return pl.pallas_call(
        body, name="grad_pair_exchange", in_specs=[ANY] * (n + 1), out_specs=[ANY] * (n + 1),
        out_shape=[jax.ShapeDtypeStruct((4,) + p.shape[1:], p.dtype) for p in parts]
        + [jax.ShapeDtypeStruct(common.shape, common.dtype)],
        scratch_shapes=[pltpu.SemaphoreType.DMA((4, npc)), pltpu.SemaphoreType.DMA((4, npc))])(*parts, common)


def _chip_exchange(parts, common):
    n = len(parts)
    table = _piece_table([p.shape for p in parts] + [common.shape])
    npc = len(table)

    def body(*refs):
        ins, outs = refs[:n + 1], refs[n + 1:2 * n + 2]
        send_sems, recv_sems, local_sems = refs[2 * n + 2:]
        x, y, c = _mesh_pos()
        mine = 2 * x + y
        chips = [(1 - x, y), (x, 1 - y), (1 - x, 1 - y)]
        src = lambda a, k: ins[a].at[k] if a < n else ins[a]
        local = []
        for a in range(n + 1):
            local.append(pltpu.make_async_copy(src(a, mine), outs[a].at[mine], local_sems.at[a]))
            local[-1].start()
        sent = []
        for j, (cx, cy) in enumerate(chips):
            for q, (a, s, sz) in enumerate(table):
                sent.append(_remote(src(a, 2 * cx + cy).at[pl.ds(s, sz)], outs[a].at[mine, pl.ds(s, sz)],
                                    send_sems.at[j, q], recv_sems.at[j, q], (cx, cy, c)))
                sent[-1].start()
        for j, (cx, cy) in enumerate(chips):
            for q, (a, s, sz) in enumerate(table):
                dst = outs[a].at[2 * cx + cy, pl.ds(s, sz)]
                _remote(dst, dst, send_sems.at[j, q], recv_sems.at[j, q], (x, y, c)).wait_recv()
        for cp in sent:
            cp.wait_send()
        for cp in local:
            cp.wait()

    return pl.pallas_call(
        body, name="grad_chip_exchange", in_specs=[ANY] * (n + 1), out_specs=[ANY] * (n + 1),
        out_shape=[jax.ShapeDtypeStruct(p.shape, p.dtype) for p in parts]
        + [jax.ShapeDtypeStruct((4,) + common.shape, common.dtype)],
        scratch_shapes=[pltpu.SemaphoreType.DMA((3, npc)), pltpu.SemaphoreType.DMA((3, npc)),
                        pltpu.SemaphoreType.DMA((n + 1,))])(*parts, common)


def _sibling_exchange(sums):
    n = len(sums)
    table = _piece_table([s.shape for s in sums])
    npc = len(table)

    def body(*refs):
        ins, outs = refs[:n], refs[n:2 * n]
        send_sems, recv_sems, local_sems = refs[2 * n:]
        x, y, c = _mesh_pos()
        copies = []
        for a in range(n):
            copies.append(pltpu.make_async_copy(ins[a], outs[a].at[c], local_sems.at[a]))
            copies[-1].start()
        sent = []
        for q, (a, s, sz) in enumerate(table):
            sent.append(_remote(ins[a].at[pl.ds(s, sz)], outs[a].at[c, pl.ds(s, sz)], send_sems.at[q], recv_sems.at[q],
                                (x, y, 1 - c)))
            sent[-1].start()
        for q, (a, s, sz) in enumerate(table):
            dst = outs[a].at[1 - c, pl.ds(s, sz)]
            _remote(dst, dst, send_sems.at[q], recv_sems.at[q], (x, y, c)).wait_recv()
        for cp in sent:
            cp.wait_send()
        for cp in copies:
            cp.wait()

    return pl.pallas_call(
        body, name="sibling_exchange", in_specs=[ANY] * n, out_specs=[ANY] * n,
        out_shape=[jax.ShapeDtypeStruct((2,) + s.shape, s.dtype) for s in sums],
        scratch_shapes=[pltpu.SemaphoreType.DMA((npc,)), pltpu.SemaphoreType.DMA((npc,)),
                        pltpu.SemaphoreType.DMA((n,))])(*sums)


def _sum_pair(core, parts, got, name, tr):
    _, r, c = got.shape
    tr = min(tr, r)

    def body(core_ref, p_ref, g_ref, o_ref):
        o_ref[...] = p_ref[...] + g_ref[...]

    return pl.pallas_call(
        body, name=name,
        grid_spec=pltpu.PrefetchScalarGridSpec(
            num_scalar_prefetch=1, grid=(4, r // tr),
            in_specs=[pl.BlockSpec((None, tr, c), lambda k, i, core_ref: (2 * k + core_ref[0], i, 0)),
                      pl.BlockSpec((None, tr, c), lambda k, i, core_ref: (k, i, 0))],
            out_specs=pl.BlockSpec((None, tr, c), lambda k, i, core_ref: (k, i, 0))),
        out_shape=jax.ShapeDtypeStruct(got.shape, got.dtype), compiler_params=_cp(2))(core, parts, got)


def _sum_leading(parts, name, tr):
    nlead, r, c = parts.shape
    tr = min(tr, r)

    def body(p_ref, o_ref):
        acc = p_ref[0]
        for j in range(1, nlead):
            acc = acc + p_ref[j]
        o_ref[...] = acc

    return pl.pallas_call(
        body, name=name, grid=(r // tr,),
        in_specs=[pl.BlockSpec((nlead, tr, c), lambda i: (0, i, 0))], out_specs=pl.BlockSpec((tr, c), lambda i: (i, 0)),
        out_shape=jax.ShapeDtypeStruct((r, c), parts.dtype), compiler_params=_cp())(parts)


def _adamw(w, g, m, v, name, tr):
    r, c = w.shape
    tr = min(tr, r)

    def body(w_ref, g_ref, m_ref, v_ref, d_ref, nm_ref, nv_ref):
        gg = g_ref[...]
        nm = ADAM_B1 * m_ref[...] + (1.0 - ADAM_B1) * gg
        nv = ADAM_B2 * v_ref[...] + (1.0 - ADAM_B2) * jnp.square(gg)
        m_hat = nm / (1.0 - ADAM_B1 ** ADAM_STEP)
        v_hat = nv / (1.0 - ADAM_B2 ** ADAM_STEP)
        d_ref[...] = -ADAM_LR * (m_hat / (jnp.sqrt(v_hat) + ADAM_EPS) + ADAM_WD * w_ref[...])
        nm_ref[...] = nm
        nv_ref[...] = nv

    blk = pl.BlockSpec((tr, c), lambda i: (i, 0))
    return pl.pallas_call(
        body, name=name, grid=(r // tr,), in_specs=[blk] * 4, out_specs=[blk] * 3,
        out_shape=[jax.ShapeDtypeStruct((r, c), F32)] * 3, compiler_params=_cp())(w, g, m, v)


def _rope_tables(positions):
    inv_freq = ROPE_BASE ** (-jnp.arange(0, 64, 2, dtype=F32) / 64)
    ang = positions.astype(F32)[:, None] * inv_freq
    cos, sin = jnp.cos(ang), jnp.sin(ang)
    t = positions.shape[0]
    rc = jnp.concatenate([jnp.ones((t, 128), F32), cos, cos, jnp.ones((t, 64), F32)], axis=1)
    rs = jnp.concatenate([jnp.zeros((t, 128), F32), sin, sin, jnp.zeros((t, 64), F32)], axis=1)
    return rc, rs


def _layer_params(l, w_in, w_uq, w_ukv, w_out, small):
    p = {}
    p["w_in"] = jnp.concatenate([w_in[l][:, :1984], jnp.zeros((1024, 64), w_in.dtype), w_in[l][:, 1984:]], axis=1)
    p["w_uq"] = jnp.pad(w_uq[l].reshape(384, 4, 192), ((0, 0), (0, 0), (0, 64))).reshape(384, 1024)
    p["w_ukv"] = w_ukv[l].reshape(256, 4, 2, 128).transpose(0, 2, 1, 3).reshape(256, 1024)
    p["w_out"] = w_out[l]
    p["pre_g"] = small["pre_norm_g"][l][None]
    p["post_g"] = small["post_norm_g"][l][None]
    p["sgu_w"] = small["sgu_w"][l].reshape(512, 128)
    p["sgu_wt"] = small["sgu_w"][l].transpose(0, 2, 1).reshape(512, 128)
    p["sgu_bias"] = jnp.repeat(small["sgu_b"][l].T, 64, axis=1)
    p["ln_g"] = small["sgu_ln_g"][l][None]
    p["ln_b"] = small["sgu_ln_b"][l][None]
    p["pool_wbd"] = _mx(jax.scipy.linalg.block_diag(*[small["pool_w"][l][gi] for gi in range(4)]))
    p["pool_scale"] = small["pool_scale"][l][None]
    p["gq"] = small["q_norm_g"][l][None]
    p["gkv"] = small["kv_norm_g"][l][None]
    return p


def _layer_fwd(l, x, p, rc, rs, target):
    z, h = _in_proj_fwd(x, p["pre_g"], p["w_in"], f"in_proj_fwd_{l}")
    yab = _mix_fwd(z, p["sgu_w"], p["sgu_bias"], p["ln_g"], p["ln_b"], p["pool_wbd"], p["pool_scale"], f"mix_fwd_{l}")
    qh, kh, vh = _qkv_fwd(z, rc, rs, p["w_uq"], p["w_ukv"], p["gq"], p["gkv"], f"qkv_fwd_{l}")
    o, yc, lse = _attn_fwd(qh, kh, vh, z, f"attn_fwd_{l}")
    outs = _out_proj_fwd(yab, yc, p["w_out"], x, p["post_g"], target, f"out_proj_fwd_{l}")
    saved = dict(x=x, z=z, h=h, yab=yab, qh=qh, kh=kh, vh=vh, o=o, yc=yc, lse=lse, y=outs[0])
    return saved, outs[1:]


def _layer_bwd(l, dout, sv, p, rc, rs):
    dycat, dw_out, dpost = _out_proj_bwd(dout, sv["y"], sv["yab"], sv["yc"], p["w_out"], p["post_g"], f"out_proj_bwd_{l}")
    dq, dgate, dk, dv = _attn_bwd(sv["qh"], sv["kh"], sv["vh"], sv["o"], sv["lse"], dycat, sv["z"], f"attn_bwd_{l}")
    dzc, dzk, dwq, dwkv, dgq, dgkv = _qkv_bwd(dq, dk, dv, sv["z"], rc, rs, p["w_uq"], p["w_ukv"], p["gq"], p["gkv"],
                                              f"qkv_bwd_{l}")
    dzm, dsw, dsb, dlng, dlnb, dpw, dps = _mix_bwd(sv["z"], dycat, p["sgu_w"], p["sgu_wt"], p["sgu_bias"], p["ln_g"],
                                                   p["ln_b"], p["pool_wbd"], p["pool_scale"], f"mix_bwd_{l}")
    dx, dw_in, dpre = _in_proj_bwd(dzm, dzc, dzk, dgate, sv["h"], sv["x"], dout, p["w_in"], p["pre_g"], f"in_proj_bwd_{l}")
    grads = {
        "pre_norm_g": dpre[0], "post_norm_g": dpost[0],
        "w_in": jnp.concatenate([dw_in[:, :1984], dw_in[:, 2048:]], axis=1),
        "sgu_w": dsw.reshape(4, 128, 128), "sgu_b": dsb[:, :4].T, "sgu_ln_g": dlng[0], "sgu_ln_b": dlnb[0],
        "pool_w": jnp.stack([dpw[64 * gi:64 * gi + 64, 64 * gi:64 * gi + 64] for gi in range(4)]),
        "pool_scale": dps[0], "q_norm_g": dgq[0],
        "w_uq": dwq.reshape(384, 4, 256)[:, :, :192].reshape(384, 768), "kv_norm_g": dgkv[0],
        "w_ukv": dwkv.reshape(256, 2, 4, 128).transpose(0, 2, 1, 3).reshape(256, 1024), "w_out": dw_out,
    }
    return dx, grads


SMALL_NAMES = ["pre_norm_g", "post_norm_g", "sgu_w", "sgu_b", "sgu_ln_g", "sgu_ln_b", "pool_w", "pool_scale",
               "q_norm_g", "kv_norm_g"]
BIG_NAMES = ["w_in", "w_uq", "w_ukv", "w_out"]
WEIGHT_NAMES = ["pre_norm_g", "post_norm_g", "w_in", "sgu_w", "sgu_b", "sgu_ln_g", "sgu_ln_b", "pool_w", "pool_scale",
                "q_norm_g", "w_uq", "kv_norm_g", "w_ukv", "w_out"]


def _local_step(x, positions, target, w_in, w_uq, w_ukv, w_out, small):
    rc, rs = _rope_tables(positions)
    params = [_layer_params(l, w_in, w_uq, w_ukv, w_out, small) for l in range(DEPTH)]
    saved = []
    for l in range(DEPTH):
        sv, outs = _layer_fwd(l, x, params[l], rc, rs, target if l == DEPTH - 1 else None)
        saved.append(sv)
        if l < DEPTH - 1:
            x = outs[0]
    dout, loss = outs
    grads = [None] * DEPTH
    for l in reversed(range(DEPTH)):
        dout, grads[l] = _layer_bwd(l, dout, saved[l], params[l], rc, rs)
    return loss[0, 0], dout, {k: jnp.stack([grads[l][k] for l in range(DEPTH)]) for k in WEIGHT_NAMES}


def _pack_small(tree):
    flat = jnp.concatenate([tree[k].reshape(-1) for k in SMALL_NAMES])
    rows = -(-flat.shape[0] // 1024) * 8
    return jnp.pad(flat, (0, rows * 128 - flat.shape[0])).reshape(rows, 128)


def _unpack_small(packed, like):
    flat = packed.reshape(-1)
    out, off = {}, 0
    for k in SMALL_NAMES:
        size = like[k].size
        out[k] = flat[off:off + size].reshape(like[k].shape)
        off += size
    return out


def kernel(x, positions, pre_norm_g, post_norm_g, w_in, sgu_w, sgu_b, sgu_ln_g, sgu_ln_b, pool_w, pool_scale, q_norm_g, w_uq, kv_norm_g, w_ukv, w_out, loss_target, m_pre_norm_g, m_post_norm_g, m_w_in, m_sgu_w, m_sgu_b, m_sgu_ln_g, m_sgu_ln_b, m_pool_w, m_pool_scale, m_q_norm_g, m_w_uq, m_kv_norm_g, m_w_ukv, m_w_out, v_pre_norm_g, v_post_norm_g, v_w_in, v_sgu_w, v_sgu_b, v_sgu_ln_g, v_sgu_ln_b, v_pool_w, v_pool_scale, v_q_norm_g, v_w_uq, v_kv_norm_g, v_w_ukv, v_w_out):
    w = dict(pre_norm_g=pre_norm_g, post_norm_g=post_norm_g, w_in=w_in, sgu_w=sgu_w, sgu_b=sgu_b, sgu_ln_g=sgu_ln_g,
             sgu_ln_b=sgu_ln_b, pool_w=pool_w, pool_scale=pool_scale, q_norm_g=q_norm_g, w_uq=w_uq, kv_norm_g=kv_norm_g,
             w_ukv=w_ukv, w_out=w_out)
    m = dict(pre_norm_g=m_pre_norm_g, post_norm_g=m_post_norm_g, w_in=m_w_in, sgu_w=m_sgu_w, sgu_b=m_sgu_b,
             sgu_ln_g=m_sgu_ln_g, sgu_ln_b=m_sgu_ln_b, pool_w=m_pool_w, pool_scale=m_pool_scale, q_norm_g=m_q_norm_g,
             w_uq=m_w_uq, kv_norm_g=m_kv_norm_g, w_ukv=m_w_ukv, w_out=m_w_out)
    v = dict(pre_norm_g=v_pre_norm_g, post_norm_g=v_post_norm_g, w_in=v_w_in, sgu_w=v_sgu_w, sgu_b=v_sgu_b,
             sgu_ln_g=v_sgu_ln_g, sgu_ln_b=v_sgu_ln_b, pool_w=v_pool_w, pool_scale=v_pool_scale, q_norm_g=v_q_norm_g,
             w_uq=v_w_uq, kv_norm_g=v_kv_norm_g, w_ukv=v_w_ukv, w_out=v_w_out)

    g_in, g_uq, g_ukv, g_out = _gather_weights([_mx(w[k]) for k in BIG_NAMES])
    cols = lambda g: g.reshape((4, 2) + g.shape[1:]).transpose(1, 2, 0, 3).reshape(2, g.shape[1], 4 * g.shape[2])
    full_out = g_out.reshape(4, 2, 256, 1024).transpose(1, 0, 2, 3).reshape(2, 1024, 1024)
    loss, dx, grads = _local_step(x[0], positions[0], loss_target[0], cols(g_in), cols(g_uq), cols(g_ukv), full_out, w)

    split_cols = lambda g: g.reshape(2, g.shape[1], 4, g.shape[2] // 4).transpose(2, 0, 1, 3).reshape(8, g.shape[1], g.shape[2] // 4)
    parts = [split_cols(grads["w_in"]), split_cols(grads["w_uq"]), split_cols(grads["w_ukv"]),
             grads["w_out"].reshape(2, 4, 256, 1024).transpose(1, 0, 2, 3).reshape(8, 256, 1024)]
    common = _pack_small(grads)
    got = _pair_exchange(parts, common)
    core = lax.axis_index("c").astype(jnp.int32).reshape(1)
    chip_parts = [_sum_pair(core, parts[a], got[a], f"pair_sum_{BIG_NAMES[a]}", 128) for a in range(4)]
    chip_common = _sum_leading(jnp.stack([common, got[4]]), "pair_sum_small", common.shape[0])
    received = _chip_exchange(chip_parts, chip_common)
    sums = [_sum_leading(received[a], f"sum_{BIG_NAMES[a]}", 128) for a in range(4)]
    small_sum = _unpack_small(_sum_leading(received[4], "sum_small", received[4].shape[1]), w)
    both = _sibling_exchange(sums)
    total = dict(small_sum)
    for a, k in enumerate(BIG_NAMES):
        total[k] = both[a]

    packed = _adamw(_pack_small(w), _pack_small(total), _pack_small(m), _pack_small(v), "adamw_small", 2048)
    small_out = [_unpack_small(pk, w) for pk in packed]
    delta, new_m, new_v = {}, {}, {}
    for k in SMALL_NAMES:
        delta[k], new_m[k], new_v[k] = (so[k] for so in small_out)
    for k in BIG_NAMES:
        shape = w[k].shape
        flat = lambda a: a.reshape(shape[0] * shape[1], shape[2])
        res = _adamw(flat(w[k]), flat(total[k]), flat(m[k]), flat(v[k]), f"adamw_{k}", 256)
        delta[k], new_m[k], new_v[k] = (r.reshape(shape) for r in res)

    loss = lax.psum(loss, ("x", "y", "c"))
    return (loss, dx[None], *[total[k] for k in WEIGHT_NAMES], *[delta[k] for k in WEIGHT_NAMES],
            *[new_m[k] for k in WEIGHT_NAMES], *[new_v[k] for k in WEIGHT_NAMES])
```

```python
import jax
import jax.numpy as jnp
from jax import lax
from jax.experimental import pallas as pl
from jax.experimental.pallas import tpu as pltpu

F32 = jnp.float32
MXU_DTYPE = jnp.bfloat16
WIRE_DTYPE = jnp.bfloat16
EPS = 1e-6
NEG_INF = -1e30
DEPTH = 2
N_HEADS = 4
QK_PAD = 256
V_DIM = 128
SCALE = 192 ** -0.5
LOG2E = 1.4426950408889634
ROPE_BASE = 10000.0
ADAM_LR, ADAM_B1, ADAM_B2, ADAM_EPS, ADAM_WD, ADAM_STEP = 0.001, 0.9, 0.999, 1e-08, 0.01, 10
VMEM_LIMIT_BYTES = 56 * 1024 * 1024
MESH = pl.DeviceIdType.MESH
ANY = pl.BlockSpec(memory_space=pl.ANY)

Z_MIX, Z_C, Z_KR, Z_GATE = 1280, 640, 128, 512
Z_W = Z_MIX + Z_C + Z_KR + Z_GATE


def _cp(n_axes=1):
    return pltpu.CompilerParams(dimension_semantics=("arbitrary",) * n_axes, vmem_limit_bytes=VMEM_LIMIT_BYTES)


def _dot(a, b):
    return lax.dot_general(a, b, (((1,), (0,)), ((), ())), preferred_element_type=F32)


def _dot_nt(a, b):
    return lax.dot_general(a, b, (((1,), (1,)), ((), ())), preferred_element_type=F32)


def _dot_tn(a, b):
    return lax.dot_general(a, b, (((0,), (0,)), ((), ())), preferred_element_type=F32)


def _mx(a):
    return a.astype(MXU_DTYPE)


def _silu_and_grad(g):
    sg = jax.nn.sigmoid(g)
    return g * sg, sg * (1.0 + g * (1.0 - sg))


def _rms(x, g):
    r = lax.rsqrt(jnp.mean(x * x, axis=-1, keepdims=True) + EPS)
    return x * r * g, r


def _rms_bwd(x, r, g, dy):
    xhat = x * r
    dyg = dy * g
    dx = r * (dyg - xhat * jnp.mean(dyg * xhat, axis=-1, keepdims=True))
    return dx, dy * xhat


def _acc(ref, val, first):
    @pl.when(first)
    def _():
        ref[...] = val

    @pl.when(jnp.logical_not(first))
    def _():
        ref[...] += val


def _colsum(a):
    return jnp.sum(a, axis=0, keepdims=True)


def _in_proj_fwd(x, g, w, name, tm=512):
    t, d = x.shape
    n = w.shape[1]
    tm = min(tm, t)

    def body(x_ref, g_ref, w_ref, z_ref, h_ref):
        h, _ = _rms(x_ref[...], g_ref[...])
        h = _mx(h)
        h_ref[...] = h
        z_ref[...] = _dot(h, w_ref[...])

    return pl.pallas_call(
        body, name=name, grid=(t // tm,),
        in_specs=[pl.BlockSpec((tm, d), lambda i: (i, 0)), pl.BlockSpec((1, d), lambda i: (0, 0)),
                  pl.BlockSpec((d, n), lambda i: (0, 0))],
        out_specs=[pl.BlockSpec((tm, n), lambda i: (i, 0)), pl.BlockSpec((tm, d), lambda i: (i, 0))],
        out_shape=[jax.ShapeDtypeStruct((t, n), F32), jax.ShapeDtypeStruct((t, d), MXU_DTYPE)],
        compiler_params=_cp())(x, g, w)


def _in_proj_bwd(dz_mix, dz_c, dz_kr, dz_gate, h, x, d_res, w, g, name, tm=256):
    t, d = x.shape
    n = w.shape[1]
    tm = min(tm, t)

    def body(dm_ref, dc_ref, dk_ref, dg_ref, h_ref, x_ref, dres_ref, w_ref, g_ref, dx_ref, dw_ref, dgn_ref):
        first = pl.program_id(0) == 0
        dz = jnp.concatenate([dm_ref[...], dc_ref[...], dk_ref[...], dg_ref[...]], axis=1)

        @pl.when(first)
        def _():
            dw_ref[...] = jnp.zeros(dw_ref.shape, F32)

        hb = h_ref[...]
        for c0 in range(0, n, 512):
            dw_ref[:, c0:c0 + 512] += _dot_tn(hb, dz[:, c0:c0 + 512])
        dh = _dot_nt(dz, w_ref[...])
        xf = x_ref[...]
        r = lax.rsqrt(jnp.mean(xf * xf, axis=-1, keepdims=True) + EPS)
        dx, dgt = _rms_bwd(xf, r, g_ref[...], dh)
        dx_ref[...] = dx + dres_ref[...]
        _acc(dgn_ref, _colsum(dgt), first)

    row = lambda wd: pl.BlockSpec((tm, wd), lambda i: (i, 0))
    fixed = lambda a, b: pl.BlockSpec((a, b), lambda i: (0, 0))
    return pl.pallas_call(
        body, name=name, grid=(t // tm,),
        in_specs=[row(Z_MIX), row(Z_C), row(Z_KR), row(Z_GATE), row(d), row(d), row(d), fixed(d, n), fixed(1, d)],
        out_specs=[row(d), fixed(d, n), fixed(1, d)],
        out_shape=[jax.ShapeDtypeStruct((t, d), F32), jax.ShapeDtypeStruct((d, n), F32),
                   jax.ShapeDtypeStruct((1, d), F32)],
        compiler_params=_cp())(dz_mix, dz_c, dz_kr, dz_gate, h, x, d_res, w, g)


def _lane_group(shape):
    return lax.broadcasted_iota(jnp.int32, shape, 1) // 64


def _select_group(vals):
    grp = _lane_group(vals[0].shape)
    out = vals[3]
    for gi in (2, 1, 0):
        out = jnp.where(grp == gi, vals[gi], out)
    return out


def _sgu_mask(transposed):
    r = (lax.broadcasted_iota(jnp.int32, (512, 128), 0) % 128) // 64
    c = lax.broadcasted_iota(jnp.int32, (512, 128), 1) // 64
    return (r <= c) if transposed else (c <= r)


def _sgu_apply(wstack, vb, nblk):
    outs = []
    for n in range(nblk):
        r = _dot(wstack, vb[n * 128:(n + 1) * 128, :])
        outs.append(_select_group([r[hh * 128:(hh + 1) * 128, :] for hh in range(4)]))
    return jnp.concatenate(outs, axis=0)


def _layer_norm(v, g, b):
    mu = jnp.mean(v, axis=-1, keepdims=True)
    vc = v - mu
    rstd = lax.rsqrt(jnp.mean(vc * vc, axis=-1, keepdims=True) + EPS)
    vhat = vc * rstd
    return vhat * g + b, vhat, rstd


def _pool_counts(t0, n):
    t = t0 + lax.broadcasted_iota(jnp.int32, (n, 256), 0)
    w = _select_group([jnp.full((n, 256), wv, jnp.int32) for wv in (2, 4, 8, 16)])
    return jnp.minimum(t + 1, w).astype(F32)


def _pooled(p, halo, t0):
    tm = p.shape[0]
    ext = jnp.concatenate([halo, p], axis=0)
    s2 = ext + pltpu.roll(ext, 1, 0)
    s4 = s2 + pltpu.roll(s2, 2, 0)
    s8 = s4 + pltpu.roll(s4, 4, 0)
    s16 = s8 + pltpu.roll(s8, 8, 0)
    sel = _select_group([s2, s4, s8, s16])[16:, :]
    return sel / _pool_counts(t0, tm) - p


def _pooled_bwd(dpool, dpool_halo, t0):
    tm = dpool.shape[0]
    n = tm + 16
    ext = jnp.concatenate([dpool, dpool_halo], axis=0) / _pool_counts(t0, n)
    f2 = ext + pltpu.roll(ext, n - 1, 0)
    f4 = f2 + pltpu.roll(f2, n - 2, 0)
    f8 = f4 + pltpu.roll(f4, n - 4, 0)
    f16 = f8 + pltpu.roll(f8, n - 8, 0)
    return _select_group([f2, f4, f8, f16])[:tm, :] - dpool


def _mix_specs(t, tm):
    nt16 = t // 16
    zrow = pl.BlockSpec((tm, Z_MIX), lambda i: (i, 0))
    prev_halo = pl.BlockSpec((16, 256), lambda i: (jnp.maximum(i * (tm // 16) - 1, 0), 3))
    fixed = lambda a, b: pl.BlockSpec((a, b), lambda i: (0, 0))
    params = [fixed(512, 128), fixed(128, 256), fixed(1, 256), fixed(1, 256), fixed(256, 256), fixed(1, 256)]
    return nt16, zrow, prev_halo, fixed, params


def _mix_fwd(z, sgu_w, sgu_bias, ln_g, ln_b, pool_wbd, pool_scale, name, tm=256):
    t = z.shape[0]
    tm = min(tm, t)
    _, zrow, prev_halo, _, params = _mix_specs(t, tm)

    def body(z_ref, halo_ref, w_ref, bias_ref, lng_ref, lnb_ref, pw_ref, ps_ref, y_ref):
        i = pl.program_id(0)
        u, v, gate = z_ref[:, 0:256], z_ref[:, 256:512], z_ref[:, 512:768]
        p, pgate = z_ref[:, 768:1024], z_ref[:, 1024:1280]
        vn, _, _ = _layer_norm(v, lng_ref[...], lnb_ref[...])
        wm = _mx(jnp.where(_sgu_mask(False), w_ref[...], 0.0))
        mixed = _sgu_apply(wm, _mx(vn), tm // 128) + jnp.tile(bias_ref[...], (tm // 128, 1))
        ya = u * mixed * _silu_and_grad(gate)[0]
        halo = jnp.where(i > 0, halo_ref[...], 0.0)
        pooled = _pooled(p, halo, i * tm)
        yb = _dot(_mx(pooled), pw_ref[...]) * ps_ref[...] * _silu_and_grad(pgate)[0]
        y_ref[...] = _mx(jnp.concatenate([ya, yb], axis=1))

    return pl.pallas_call(
        body, name=name, grid=(t // tm,),
        in_specs=[zrow, prev_halo] + params,
        out_specs=pl.BlockSpec((tm, 512), lambda i: (i, 0)),
        out_shape=jax.ShapeDtypeStruct((t, 512), MXU_DTYPE),
        compiler_params=_cp())(z, z, sgu_w, sgu_bias, ln_g, ln_b, pool_wbd, pool_scale)


def _mix_bwd(z, dycat, sgu_w, sgu_wt, sgu_bias, ln_g, ln_b, pool_wbd, pool_scale, name, tm=256):
    t = z.shape[0]
    tm = min(tm, t)
    nt16, zrow, prev_halo, fixed, params = _mix_specs(t, tm)
    nblk = tm // 128
    last = t // tm - 1

    def body(z_ref, halo_ref, zn_ref, dy_ref, dyn_ref, w_ref, wt_ref, bias_ref, lng_ref, lnb_ref, pw_ref, ps_ref,
             dz_ref, dw_ref, db_ref, dlng_ref, dlnb_ref, dpw_ref, dps_ref):
        i = pl.program_id(0)
        first = i == 0
        u, v, gate = z_ref[:, 0:256], z_ref[:, 256:512], z_ref[:, 512:768]
        p, pgate = z_ref[:, 768:1024], z_ref[:, 1024:1280]
        dya, dyb = dy_ref[:, 0:256], dy_ref[:, 256:512]
        vn, vhat, rstd = _layer_norm(v, lng_ref[...], lnb_ref[...])
        vnb = _mx(vn)
        wm = _mx(jnp.where(_sgu_mask(False), w_ref[...], 0.0))
        wmt = _mx(jnp.where(_sgu_mask(True), wt_ref[...], 0.0))
        mixed = _sgu_apply(wm, vnb, nblk) + jnp.tile(bias_ref[...], (nblk, 1))
        silu, dsilu = _silu_and_grad(gate)
        t1 = u * mixed
        d_gate = dya * t1 * dsilu
        d_t1 = dya * silu
        d_u = d_t1 * mixed
        d_mixed = d_t1 * u
        dmb = _mx(d_mixed)
        d_vn = _sgu_apply(wmt, dmb, nblk)
        grp = _lane_group((128, 256))
        lane = lax.broadcasted_iota(jnp.int32, (128, 128), 1)
        dws = [jnp.zeros((128, 128), F32) for _ in range(4)]
        dbias = jnp.zeros((128, 128), F32)
        for n in range(nblk):
            dm_n, dmb_n, vnb_n = d_mixed[n * 128:(n + 1) * 128], dmb[n * 128:(n + 1) * 128], vnb[n * 128:(n + 1) * 128]
            for hh in range(4):
                dws[hh] = dws[hh] + _dot_nt(jnp.where(grp == hh, dmb_n, jnp.zeros_like(dmb_n)), vnb_n)
                rs = jnp.sum(jnp.where(grp == hh, dm_n, 0.0), axis=-1, keepdims=True)
                dbias = dbias + jnp.where(lane == hh, rs, 0.0)
        _acc(dw_ref, jnp.concatenate(dws, axis=0), first)
        _acc(db_ref, dbias, first)
        _acc(dlng_ref, _colsum(d_vn * vhat), first)
        _acc(dlnb_ref, _colsum(d_vn), first)
        dvh = d_vn * lng_ref[...]
        d_v = rstd * (dvh - jnp.mean(dvh, axis=-1, keepdims=True) - vhat * jnp.mean(dvh * vhat, axis=-1, keepdims=True))

        @pl.when(i == last)
        def _():
            dw_ref[...] = jnp.where(_sgu_mask(False), dw_ref[...], 0.0)

        halo = jnp.where(i > 0, halo_ref[...], 0.0)
        pooled = _pooled(p, halo, i * tm)
        pooled_b = _mx(pooled)
        mixedp = _dot(pooled_b, pw_ref[...])
        psilu, pdsilu = _silu_and_grad(pgate)
        d_pgate = dyb * (mixedp * ps_ref[...]) * pdsilu
        d_ms = dyb * psilu
        _acc(dps_ref, _colsum(d_ms * mixedp), first)
        dmpb = _mx(d_ms * ps_ref[...])
        _acc(dpw_ref, _dot_tn(pooled_b, dmpb), first)
        d_pooled = _dot_nt(dmpb, pw_ref[...])
        dmp_halo = _mx(dyn_ref[...] * _silu_and_grad(zn_ref[...])[0] * ps_ref[...])
        d_pooled_halo = jnp.where(i < last, _dot_nt(dmp_halo, pw_ref[...]), 0.0)
        d_p = _pooled_bwd(d_pooled, d_pooled_halo, i * tm)
        dz_ref[...] = _mx(jnp.concatenate([d_u, d_v, d_gate, d_p, d_pgate], axis=1))

    nxt = lambda i: jnp.minimum((i + 1) * (tm // 16), nt16 - 1)
    return pl.pallas_call(
        body, name=name, grid=(t // tm,),
        in_specs=[zrow, prev_halo, pl.BlockSpec((16, 256), lambda i: (nxt(i), 4)),
                  pl.BlockSpec((tm, 512), lambda i: (i, 0)), pl.BlockSpec((16, 256), lambda i: (nxt(i), 1)),
                  params[0], fixed(512, 128)] + params[1:],
        out_specs=[pl.BlockSpec((tm, Z_MIX), lambda i: (i, 0)), fixed(512, 128), fixed(128, 128), fixed(1, 256),
                   fixed(1, 256), fixed(256, 256), fixed(1, 256)],
        out_shape=[jax.ShapeDtypeStruct((t, Z_MIX), MXU_DTYPE), jax.ShapeDtypeStruct((512, 128), F32),
                   jax.ShapeDtypeStruct((128, 128), F32), jax.ShapeDtypeStruct((1, 256), F32),
                   jax.ShapeDtypeStruct((1, 256), F32), jax.ShapeDtypeStruct((256, 256), F32),
                   jax.ShapeDtypeStruct((1, 256), F32)],
        compiler_params=_cp())(z, z, z, dycat, dycat, sgu_w, sgu_wt, sgu_bias, ln_g, ln_b, pool_wbd, pool_scale)


def _rot_half(x, transpose):
    w = x.shape[1]
    lane = lax.broadcasted_iota(jnp.int32, x.shape, 1) % min(w, 256)
    base = 128 if w >= 256 else 0
    lo = jnp.logical_and(lane >= base, lane < base + 32)
    hi = jnp.logical_and(lane >= base + 32, lane < base + 64)
    up = pltpu.roll(x, w - 32, 1)
    down = pltpu.roll(x, 32, 1)
    if transpose:
        return jnp.where(lo, up, jnp.where(hi, -down, 0.0))
    return jnp.where(lo, -up, jnp.where(hi, down, 0.0))


def _rope(x, c, s):
    return x * c + _rot_half(x, False) * s


def _rope_bwd(dy, c, s):
    return dy * c + _rot_half(dy * s, True)


def _qkv_fwd(z, rc, rs, w_uq, w_ukv, gq, gkv, name, tm=256):
    t = z.shape[0]
    tm = min(tm, t)

    def body(zc_ref, zk_ref, rc_ref, rs_ref, wq_ref, wkv_ref, gq_ref, gkv_ref, q_ref, k_ref, v_ref):
        cq, ckv = zc_ref[:, 0:384], zc_ref[:, 384:640]
        c, s = rc_ref[...], rs_ref[...]
        qn, _ = _rms(cq, gq_ref[...])
        q = _rope(_dot(_mx(qn), wq_ref[...]), jnp.tile(c, (1, N_HEADS)), jnp.tile(s, (1, N_HEADS)))
        kvn, _ = _rms(ckv, gkv_ref[...])
        kv = _dot(_mx(kvn), wkv_ref[...])
        kpe = _rope(zk_ref[...], c[:, 128:256], s[:, 128:256])
        for hh in range(N_HEADS):
            q_ref[hh] = _mx(q[:, hh * QK_PAD:(hh + 1) * QK_PAD])
            k_ref[hh] = _mx(jnp.concatenate([kv[:, hh * 128:(hh + 1) * 128], kpe], axis=1))
            v_ref[hh] = _mx(kv[:, 512 + hh * 128:512 + (hh + 1) * 128])

    fixed = lambda a, b: pl.BlockSpec((a, b), lambda i: (0, 0))
    heads = lambda wd: pl.BlockSpec((N_HEADS, tm, wd), lambda i: (0, i, 0))
    return pl.pallas_call(
        body, name=name, grid=(t // tm,),
        in_specs=[pl.BlockSpec((tm, Z_C), lambda i: (i, Z_MIX // Z_C)),
                  pl.BlockSpec((tm, Z_KR), lambda i: (i, (Z_MIX + Z_C) // Z_KR)),
                  pl.BlockSpec((tm, 256), lambda i: (i, 0)), pl.BlockSpec((tm, 256), lambda i: (i, 0)),
                  fixed(384, 1024), fixed(256, 1024), fixed(1, 384), fixed(1, 256)],
        out_specs=[heads(QK_PAD), heads(QK_PAD), heads(V_DIM)],
        out_shape=[jax.ShapeDtypeStruct((N_HEADS, t, QK_PAD), MXU_DTYPE),
                   jax.ShapeDtypeStruct((N_HEADS, t, QK_PAD), MXU_DTYPE),
                   jax.ShapeDtypeStruct((N_HEADS, t, V_DIM), MXU_DTYPE)],
        compiler_params=_cp())(z, z, rc, rs, w_uq, w_ukv, gq, gkv)


def _qkv_bwd(dq, dk, dv, z, rc, rs, w_uq, w_ukv, gq, gkv, name, tm=256):
    t = z.shape[0]
    tm = min(tm, t)

    def body(dq_ref, dk_ref, dv_ref, zc_ref, rc_ref, rs_ref, wq_ref, wkv_ref, gq_ref, gkv_ref,
             dzc_ref, dzk_ref, dwq_ref, dwkv_ref, dgq_ref, dgkv_ref):
        first = pl.program_id(0) == 0
        cq, ckv = zc_ref[:, 0:384], zc_ref[:, 384:640]
        c, s = rc_ref[...], rs_ref[...]
        dq_all = jnp.concatenate([dq_ref[hh] for hh in range(N_HEADS)], axis=1)
        dqp = _mx(_rope_bwd(dq_all, jnp.tile(c, (1, N_HEADS)), jnp.tile(s, (1, N_HEADS))))
        qn, rq = _rms(cq, gq_ref[...])
        _acc(dwq_ref, _dot_tn(_mx(qn), dqp), first)
        d_cq, dgq_t = _rms_bwd(cq, rq, gq_ref[...], _dot_nt(dqp, wq_ref[...]))
        _acc(dgq_ref, _colsum(dgq_t), first)
        dkpe = dk_ref[0][:, 128:256]
        for hh in range(1, N_HEADS):
            dkpe = dkpe + dk_ref[hh][:, 128:256]
        dzk_ref[...] = _mx(_rope_bwd(dkpe, c[:, 128:256], s[:, 128:256]))
        dkv = _mx(jnp.concatenate([dk_ref[hh][:, 0:128] for hh in range(N_HEADS)]
                                  + [dv_ref[hh] for hh in range(N_HEADS)], axis=1))
        kvn, rkv = _rms(ckv, gkv_ref[...])
        _acc(dwkv_ref, _dot_tn(_mx(kvn), dkv), first)
        d_ckv, dgkv_t = _rms_bwd(ckv, rkv, gkv_ref[...], _dot_nt(dkv, wkv_ref[...]))
        _acc(dgkv_ref, _colsum(dgkv_t), first)
        dzc_ref[...] = _mx(jnp.concatenate([d_cq, d_ckv], axis=1))

    fixed = lambda a, b: pl.BlockSpec((a, b), lambda i: (0, 0))
    heads = lambda wd: pl.BlockSpec((N_HEADS, tm, wd), lambda i: (0, i, 0))
    return pl.pallas_call(
        body, name=name, grid=(t // tm,),
        in_specs=[heads(QK_PAD), heads(QK_PAD), heads(V_DIM), pl.BlockSpec((tm, Z_C), lambda i: (i, Z_MIX // Z_C)),
                  pl.BlockSpec((tm, 256), lambda i: (i, 0)), pl.BlockSpec((tm, 256), lambda i: (i, 0)),
                  fixed(384, 1024), fixed(256, 1024), fixed(1, 384), fixed(1, 256)],
        out_specs=[pl.BlockSpec((tm, Z_C), lambda i: (i, 0)), pl.BlockSpec((tm, Z_KR), lambda i: (i, 0)),
                   fixed(384, 1024), fixed(256, 1024), fixed(1, 384), fixed(1, 256)],
        out_shape=[jax.ShapeDtypeStruct((t, Z_C), MXU_DTYPE), jax.ShapeDtypeStruct((t, Z_KR), MXU_DTYPE),
                   jax.ShapeDtypeStruct((384, 1024), F32), jax.ShapeDtypeStruct((256, 1024), F32),
                   jax.ShapeDtypeStruct((1, 384), F32), jax.ShapeDtypeStruct((1, 256), F32)],
        compiler_params=_cp())(dq, dk, dv, z, rc, rs, w_uq, w_ukv, gq, gkv)


def _chunk_mask(tq):
    r = lax.broadcasted_iota(jnp.int32, (tq, tq), 0) // 64
    c = lax.broadcasted_iota(jnp.int32, (tq, tq), 1) // 64
    return c <= r


def _gate_block(tq):
    return pl.BlockSpec((tq, 128), lambda h, i: (i, (Z_MIX + Z_C + Z_KR) // 128 + h))


def _attn_fwd(qh, kh, vh, z, name, tq=512):
    t = qh.shape[1]
    tq = min(tq, t)

    def body(q_ref, g_ref, k_hbm, v_hbm, o_ref, yc_ref, lse_ref, k_v, v_v, m_s, acc_s, s_a, s_b, sem):
        h, i = pl.program_id(0), pl.program_id(1)

        @pl.when(i == 0)
        def _():
            ck = pltpu.make_async_copy(k_hbm.at[h], k_v, sem.at[0])
            cv = pltpu.make_async_copy(v_hbm.at[h], v_v.at[:, 0:V_DIM], sem.at[1])
            ck.start()
            cv.start()
            v_v[:, V_DIM:2 * V_DIM] = jnp.ones((t, V_DIM), MXU_DTYPE)
            ck.wait()
            cv.wait()

        q = q_ref[...]
        m_s[...] = jnp.full(m_s.shape, NEG_INF, F32)
        acc_s[...] = jnp.zeros(acc_s.shape, F32)

        def keys(j):
            return pl.ds(pl.multiple_of(j * tq, tq), tq)

        def scores(s_ref, j):
            s_ref[...] = _dot_nt(q, k_v[keys(j), :]) * (SCALE * LOG2E)

        def softmax_pv(s_ref, j, masked):
            s = s_ref[...]
            if masked:
                s = jnp.where(_chunk_mask(tq), s, NEG_INF)
            m_old = m_s[...]
            m_new = jnp.maximum(m_old, jnp.max(s, axis=-1, keepdims=True))
            p = jnp.exp2(s - jnp.tile(m_new, (1, tq // 128)))
            alpha = jnp.exp2(m_old - m_new)
            m_s[...] = m_new
            acc_s[...] = jnp.tile(alpha, (1, 2)) * acc_s[...] + _dot(_mx(p), v_v[keys(j), :])

        scores(s_a, 0)

        def pair(pp, carry):
            scores(s_b, 2 * pp + 1)
            softmax_pv(s_a, 2 * pp, False)
            scores(s_a, 2 * pp + 2)
            softmax_pv(s_b, 2 * pp + 1, False)
            return carry

        lax.fori_loop(0, i // 2, pair, 0)

        @pl.when(i % 2 == 1)
        def _():
            scores(s_b, i)
            softmax_pv(s_a, i - 1, False)
            softmax_pv(s_b, i, True)

        @pl.when(i % 2 == 0)
        def _():
            softmax_pv(s_a, i, True)

        l = acc_s[:, V_DIM:2 * V_DIM]
        o = acc_s[:, 0:V_DIM] / l
        o_ref[...] = o
        yc_ref[...] = _mx(o * _silu_and_grad(g_ref[...])[0])
        lse_ref[...] = m_s[...] + jnp.log2(l)

    return pl.pallas_call(
        body, name=name, grid=(N_HEADS, t // tq),
        in_specs=[pl.BlockSpec((None, tq, QK_PAD), lambda h, i: (h, i, 0)), _gate_block(tq), ANY, ANY],
        out_specs=[pl.BlockSpec((tq, 128), lambda h, i: (i, h)), pl.BlockSpec((tq, 128), lambda h, i: (i, h)),
                   pl.BlockSpec((None, tq, 128), lambda h, i: (h, i, 0))],
        out_shape=[jax.ShapeDtypeStruct((t, N_HEADS * V_DIM), F32), jax.ShapeDtypeStruct((t, N_HEADS * V_DIM), MXU_DTYPE),
                   jax.ShapeDtypeStruct((N_HEADS, t, 128), F32)],
        scratch_shapes=[pltpu.VMEM((t, QK_PAD), MXU_DTYPE), pltpu.VMEM((t, 2 * V_DIM), MXU_DTYPE),
                        pltpu.VMEM((tq, 128), F32), pltpu.VMEM((tq, 2 * V_DIM), F32),
                        pltpu.VMEM((tq, tq), F32), pltpu.VMEM((tq, tq), F32), pltpu.SemaphoreType.DMA((2,))],
        compiler_params=_cp(2))(qh, z, kh, vh)


def _attn_bwd(qh, kh, vh, o, lse, dycat, z, name, tq=512):
    t = qh.shape[1]
    tq = min(tq, t)
    nq = t // tq

    def body(q_ref, o_ref, lse_ref, dy_ref, g_ref, k_hbm, v_hbm, dq_ref, dgate_ref, dk_hbm, dv_hbm,
             k_v, v_v, dk_acc, dv_acc, dq_acc, delta_s, s_a, dp_a, s_b, dp_b, sem):
        h, i = pl.program_id(0), pl.program_id(1)

        @pl.when(i == 0)
        def _():
            ck = pltpu.make_async_copy(k_hbm.at[h], k_v, sem.at[0])
            cv = pltpu.make_async_copy(v_hbm.at[h], v_v, sem.at[1])
            ck.start()
            cv.start()
            dk_acc[...] = jnp.zeros(dk_acc.shape, F32)
            dv_acc[...] = jnp.zeros(dv_acc.shape, F32)
            ck.wait()
            cv.wait()

        gate, dy, of = g_ref[...], dy_ref[...], o_ref[...]
        silu, dsilu = _silu_and_grad(gate)
        do = dy * silu
        delta = jnp.sum(do * of, axis=-1, keepdims=True)
        dgate_ref[...] = _mx(dy * of * dsilu)
        dob = _mx(do)
        q = q_ref[...]
        delta_s[...] = jnp.broadcast_to(delta, delta_s.shape)
        dq_acc[...] = jnp.zeros(dq_acc.shape, F32)

        def keys(j):
            return pl.ds(pl.multiple_of(j * tq, tq), tq)

        def scores(s_ref, dp_ref, j):
            s_ref[...] = _dot_nt(q, k_v[keys(j), :]) * (SCALE * LOG2E)
            dp_ref[...] = _dot_nt(dob, v_v[keys(j), :])

        def grads(s_ref, dp_ref, j, masked):
            ks = keys(j)
            p = jnp.exp2(s_ref[...] - jnp.tile(lse_ref[...], (1, tq // 128)))
            if masked:
                p = jnp.where(_chunk_mask(tq), p, 0.0)
            ds = p * (dp_ref[...] - jnp.tile(delta_s[...], (1, tq // 128))) * SCALE
            pb, dsb = _mx(p), _mx(ds)
            dq_acc[...] += _dot(dsb, k_v[ks, :])
            dk_acc[ks, :] += _dot_tn(dsb, q)
            dv_acc[ks, :] += _dot_tn(pb, dob)

        scores(s_a, dp_a, 0)

        def pair(pp, carry):
            scores(s_b, dp_b, 2 * pp + 1)
            grads(s_a, dp_a, 2 * pp, False)
            scores(s_a, dp_a, 2 * pp + 2)
            grads(s_b, dp_b, 2 * pp + 1, False)
            return carry

        lax.fori_loop(0, i // 2, pair, 0)

        @pl.when(i % 2 == 1)
        def _():
            scores(s_b, dp_b, i)
            grads(s_a, dp_a, i - 1, False)
            grads(s_b, dp_b, i, True)

        @pl.when(i % 2 == 0)
        def _():
            grads(s_a, dp_a, i, True)

        dq_ref[...] = dq_acc[...]

        @pl.when(i == nq - 1)
        def _():
            ck = pltpu.make_async_copy(dk_acc, dk_hbm.at[h], sem.at[0])
            cv = pltpu.make_async_copy(dv_acc, dv_hbm.at[h], sem.at[1])
            ck.start()
            cv.start()
            ck.wait()
            cv.wait()

    return pl.pallas_call(
        body, name=name, grid=(N_HEADS, nq),
        in_specs=[pl.BlockSpec((None, tq, QK_PAD), lambda h, i: (h, i, 0)),
                  pl.BlockSpec((tq, 128), lambda h, i: (i, h)),
                  pl.BlockSpec((None, tq, 128), lambda h, i: (h, i, 0)),
                  pl.BlockSpec((tq, 128), lambda h, i: (i, N_HEADS + h)), _gate_block(tq), ANY, ANY],
        out_specs=[pl.BlockSpec((None, tq, QK_PAD), lambda h, i: (h, i, 0)),
                   pl.BlockSpec((tq, 128), lambda h, i: (i, h)), ANY, ANY],
        out_shape=[jax.ShapeDtypeStruct((N_HEADS, t, QK_PAD), F32), jax.ShapeDtypeStruct((t, Z_GATE), MXU_DTYPE),
                   jax.ShapeDtypeStruct((N_HEADS, t, QK_PAD), F32), jax.ShapeDtypeStruct((N_HEADS, t, V_DIM), F32)],
        scratch_shapes=[pltpu.VMEM((t, QK_PAD), MXU_DTYPE), pltpu.VMEM((t, V_DIM), MXU_DTYPE),
                        pltpu.VMEM((t, QK_PAD), F32), pltpu.VMEM((t, V_DIM), F32), pltpu.VMEM((tq, QK_PAD), F32),
                        pltpu.VMEM((tq, 128), F32)] + [pltpu.VMEM((tq, tq), F32)] * 4
        + [pltpu.SemaphoreType.DMA((2,))],
        compiler_params=_cp(2))(qh, o, lse, dycat, z, kh, vh)


def _out_proj_fwd(yab, yc, w, x, g, target, name, tm=512):
    t, d = x.shape
    tm = min(tm, t)
    is_last = target is not None

    def body(*refs):
        if is_last:
            yab_ref, yc_ref, w_ref, x_ref, g_ref, t_ref, y_ref, dout_ref, loss_ref = refs
        else:
            yab_ref, yc_ref, w_ref, x_ref, g_ref, y_ref, out_ref = refs
        y = _dot(jnp.concatenate([yab_ref[...], yc_ref[...]], axis=1), w_ref[...])
        y_ref[...] = y
        out = x_ref[...] + _rms(y, g_ref[...])[0]
        if is_last:
            diff = out - t_ref[...]
            dout_ref[...] = diff * (1.0 / d)
            part = jnp.sum(jnp.sum(diff * diff, axis=-1, keepdims=True), axis=0, keepdims=True) * (0.5 / d)
            _acc(loss_ref, jnp.broadcast_to(part, (1, 128)), pl.program_id(0) == 0)
        else:
            out_ref[...] = out

    row = lambda wd: pl.BlockSpec((tm, wd), lambda i: (i, 0))
    fixed = lambda a, b: pl.BlockSpec((a, b), lambda i: (0, 0))
    in_specs = [row(512), row(512), fixed(d, d), row(d), fixed(1, d)]
    args = [yab, yc, w, x, g]
    out_specs = [row(d), row(d)]
    out_shape = [jax.ShapeDtypeStruct((t, d), F32), jax.ShapeDtypeStruct((t, d), F32)]
    if is_last:
        in_specs.append(row(d))
        args.append(target)
        out_specs.append(fixed(1, 128))
        out_shape.append(jax.ShapeDtypeStruct((1, 128), F32))
    return pl.pallas_call(body, name=name, grid=(t // tm,), in_specs=in_specs, out_specs=out_specs,
                          out_shape=out_shape, compiler_params=_cp())(*args)


def _out_proj_bwd(dout, y, yab, yc, w, g, name, tm=512):
    t, d = y.shape
    tm = min(tm, t)

    def body(dout_ref, y_ref, yab_ref, yc_ref, w_ref, g_ref, dycat_ref, dw_ref, dg_ref):
        first = pl.program_id(0) == 0
        y = y_ref[...]
        r = lax.rsqrt(jnp.mean(y * y, axis=-1, keepdims=True) + EPS)
        dy, dgt = _rms_bwd(y, r, g_ref[...], dout_ref[...])
        _acc(dg_ref, _colsum(dgt), first)
        dyb = _mx(dy)
        _acc(dw_ref, _dot_tn(jnp.concatenate([yab_ref[...], yc_ref[...]], axis=1), dyb), first)
        dycat_ref[...] = _dot_nt(dyb, w_ref[...])

    row = lambda wd: pl.BlockSpec((tm, wd), lambda i: (i, 0))
    fixed = lambda a, b: pl.BlockSpec((a, b), lambda i: (0, 0))
    return pl.pallas_call(
        body, name=name, grid=(t // tm,),
        in_specs=[row(d), row(d), row(512), row(512), fixed(d, d), fixed(1, d)],
        out_specs=[row(d), fixed(d, d), fixed(1, d)],
        out_shape=[jax.ShapeDtypeStruct((t, d), F32), jax.ShapeDtypeStruct((d, d), F32),
                   jax.ShapeDtypeStruct((1, d), F32)],
        compiler_params=_cp())(dout, y, yab, yc, w, g)


def _mesh_pos():
    return lax.axis_index("x"), lax.axis_index("y"), lax.axis_index("c")


def _remote(src, dst, send_sem, recv_sem, to):
    return pltpu.make_async_remote_copy(src_ref=src, dst_ref=dst, send_sem=send_sem, recv_sem=recv_sem,
                                        device_id=to, device_id_type=MESH)


CHUNK_ROWS = 256


def _pieces(rows):
    return [(s, min(CHUNK_ROWS, rows - s)) for s in range(0, rows, CHUNK_ROWS)]


def _piece_table(shapes):
    return [(a, s, sz) for a, shp in enumerate(shapes) for s, sz in _pieces(shp[-2])]


def _gather_weights(shards):
    n = len(shards)
    table = _piece_table([s.shape for s in shards])
    npc = len(table)

    def body(*refs):
        ins, outs = refs[:n], refs[n:2 * n]
        send_sems, recv_sems, fwd_send, fwd_recv = refs[2 * n:]
        x, y, c = _mesh_pos()
        me, sibling = (x, y, c), (x, y, 1 - c)
        chips = [(1 - x, y), (x, 1 - y), (1 - x, 1 - y)]
        slot = lambda cx, cy, layer: 2 * (2 * cx + cy) + layer
        first = []
        for a in range(n):
            for j, (cx, cy) in enumerate(chips):
                first.append(_remote(ins[a].at[c], outs[a].at[slot(x, y, c)], send_sems.at[a, j], recv_sems.at[a, j],
                                     (cx, cy, c)))
                first[-1].start()
        passed = []
        for j, (cx, cy) in enumerate(chips):
            for a in range(n):
                blk = outs[a].at[slot(cx, cy, c)]
                _remote(blk, blk, send_sems.at[a, j], recv_sems.at[a, j], me).wait_recv()
            for q, (a, s, sz) in enumerate(table):
                rows = outs[a].at[slot(cx, cy, c), pl.ds(s, sz)]
                passed.append(_remote(rows, rows, fwd_send.at[j, q], fwd_recv.at[j, q], sibling))
                passed[-1].start()
        for j, (cx, cy) in enumerate(chips):
            for q, (a, s, sz) in enumerate(table):
                rows = outs[a].at[slot(cx, cy, 1 - c), pl.ds(s, sz)]
                _remote(rows, rows, fwd_send.at[j, q], fwd_recv.at[j, q], me).wait_recv()
        for cp in first + passed:
            cp.wait_send()

    return pl.pallas_call(
        body, name="gather_weights", in_specs=[ANY] * n, out_specs=[ANY] * n,
        out_shape=[jax.ShapeDtypeStruct((8,) + s.shape[1:], s.dtype) for s in shards],
        scratch_shapes=[pltpu.SemaphoreType.DMA((n, 3)), pltpu.SemaphoreType.DMA((n, 3)),
                        pltpu.SemaphoreType.DMA((3, npc)), pltpu.SemaphoreType.DMA((3, npc))])(*shards)


def _pair_exchange(parts, common):
    n = len(parts)
    table = _piece_table([p.shape for p in parts] + [common.shape])
    npc = len(table)

    def body(*refs):
        ins, outs = refs[:n + 1], refs[n + 1:2 * n + 2]
        send_sems, recv_sems = refs[2 * n + 2:]
        x, y, c = _mesh_pos()
        sent = []
        for k in range(4):
            for q, (a, s, sz) in enumerate(table):
                if a == n and k > 0:
                    continue
                src = ins[a].at[2 * k + 1 - c, pl.ds(s, sz)] if a < n else ins[a].at[pl.ds(s, sz)]
                dst = outs[a].at[k, pl.ds(s, sz)] if a < n else outs[a].at[pl.ds(s, sz)]
                sent.append(_remote(src, dst, send_sems.at[k, q], recv_sems.at[k, q], (x, y, 1 - c)))
                sent[-1].start()
        for k in range(4):
            for q, (a, s, sz) in enumerate(table):
                if a == n and k > 0:
                    continue
                dst = outs[a].at[k, pl.ds(s, sz)] if a < n else outs[a].at[pl.ds(s, sz)]
                _remote(dst, dst, send_sems.at[k, q], recv_sems.at[k, q], (x, y, c)).wait_recv()
        for cp in sent:
            cp.wait_send()

    return pl.pallas_call(
        body, name="grad_pair_exchange", in_specs=[ANY] * (n + 1), out_specs=[ANY] * (n + 1),
        out_shape=[jax.ShapeDtypeStruct((4,) + p.shape[1:], p.dtype) for p in parts]
        + [jax.ShapeDtypeStruct(common.shape, common.dtype)],
        scratch_shapes=[pltpu.SemaphoreType.DMA((4, npc)), pltpu.SemaphoreType.DMA((4, npc))])(*parts, common)


def _chip_exchange(parts, common):
    n = len(parts)
    table = _piece_table([p.shape for p in parts] + [common.shape])
    npc = len(table)

    def body(*refs):
        ins, outs = refs[:n + 1], refs[n + 1:2 * n + 2]
        send_sems, recv_sems = refs[2 * n + 2:]
        x, y, c = _mesh_pos()
        mine = 2 * x + y
        chips = [(1 - x, y), (x, 1 - y), (1 - x, 1 - y)]
        src = lambda a, k: ins[a].at[k] if a < n else ins[a]
        sent = []
        for j, (cx, cy) in enumerate(chips):
            for q, (a, s, sz) in enumerate(table):
                sent.append(_remote(src(a, 2 * cx + cy).at[pl.ds(s, sz)], outs[a].at[mine, pl.ds(s, sz)],
                                    send_sems.at[j, q], recv_sems.at[j, q], (cx, cy, c)))
                sent[-1].start()
        for j, (cx, cy) in enumerate(chips):
            for q, (a, s, sz) in enumerate(table):
                dst = outs[a].at[2 * cx + cy, pl.ds(s, sz)]
                _remote(dst, dst, send_sems.at[j, q], recv_sems.at[j, q], (x, y, c)).wait_recv()
        for cp in sent:
            cp.wait_send()

    return pl.pallas_call(
        body, name="grad_chip_exchange", in_specs=[ANY] * (n + 1), out_specs=[ANY] * (n + 1),
        out_shape=[jax.ShapeDtypeStruct(p.shape, p.dtype) for p in parts]
        + [jax.ShapeDtypeStruct((4,) + common.shape, common.dtype)],
        scratch_shapes=[pltpu.SemaphoreType.DMA((3, npc)), pltpu.SemaphoreType.DMA((3, npc))])(*parts, common)


def _sibling_exchange(sums):
    n = len(sums)
    table = _piece_table([s.shape for s in sums])
    npc = len(table)

    def body(*refs):
        ins, outs = refs[:n], refs[n:2 * n]
        send_sems, recv_sems = refs[2 * n:]
        x, y, c = _mesh_pos()
        sent = []
        for q, (a, s, sz) in enumerate(table):
            sent.append(_remote(ins[a].at[pl.ds(s, sz)], outs[a].at[pl.ds(s, sz)], send_sems.at[q], recv_sems.at[q],
                                (x, y, 1 - c)))
            sent[-1].start()
        for q, (a, s, sz) in enumerate(table):
            dst = outs[a].at[pl.ds(s, sz)]
            _remote(dst, dst, send_sems.at[q], recv_sems.at[q], (x, y, c)).wait_recv()
        for cp in sent:
            cp.wait_send()

    return pl.pallas_call(
        body, name="sibling_exchange", in_specs=[ANY] * n, out_specs=[ANY] * n,
        out_shape=[jax.ShapeDtypeStruct(s.shape, s.dtype) for s in sums],
        scratch_shapes=[pltpu.SemaphoreType.DMA((npc,)), pltpu.SemaphoreType.DMA((npc,))])(*sums)


def _pair_sum_wire(where, parts, got, name, tr):
    _, r, c = got.shape
    tr = min(tr, r)

    def body(where_ref, p_ref, g_ref, o_ref):
        o_ref[...] = (p_ref[...] + g_ref[...]).astype(WIRE_DTYPE)

    return pl.pallas_call(
        body, name=name,
        grid_spec=pltpu.PrefetchScalarGridSpec(
            num_scalar_prefetch=1, grid=(4, r // tr),
            in_specs=[pl.BlockSpec((None, tr, c), lambda k, i, where_ref: (2 * k + where_ref[0], i, 0)),
                      pl.BlockSpec((None, tr, c), lambda k, i, where_ref: (k, i, 0))],
            out_specs=pl.BlockSpec((None, tr, c), lambda k, i, where_ref: (k, i, 0))),
        out_shape=jax.ShapeDtypeStruct(got.shape, WIRE_DTYPE), compiler_params=_cp(2))(where, parts, got)


def _sum_chips(where, parts, got, recv, name, tr):
    _, r, c = got.shape
    tr = min(tr, r)

    def body(where_ref, p_ref, g_ref, r_ref, o_ref):
        own = p_ref[...] + g_ref[...]
        chip = where_ref[1]
        acc = jnp.where(chip == 0, own, r_ref[0].astype(F32))
        for k in range(1, 4):
            acc = acc + jnp.where(chip == k, own, r_ref[k].astype(F32))
        o_ref[...] = acc

    return pl.pallas_call(
        body, name=name,
        grid_spec=pltpu.PrefetchScalarGridSpec(
            num_scalar_prefetch=1, grid=(r // tr,),
            in_specs=[pl.BlockSpec((None, tr, c), lambda i, where_ref: (2 * where_ref[1] + where_ref[0], i, 0)),
                      pl.BlockSpec((None, tr, c), lambda i, where_ref: (where_ref[1], i, 0)),
                      pl.BlockSpec((4, tr, c), lambda i, where_ref: (0, i, 0))],
            out_specs=pl.BlockSpec((tr, c), lambda i, where_ref: (i, 0))),
        out_shape=jax.ShapeDtypeStruct((r, c), F32), compiler_params=_cp())(where, parts, got, recv)


def _sum_leading(parts, name, tr):
    nlead, r, c = parts.shape
    tr = min(tr, r)

    def body(p_ref, o_ref):
        acc = p_ref[0]
        for j in range(1, nlead):
            acc = acc + p_ref[j]
        o_ref[...] = acc

    return pl.pallas_call(
        body, name=name, grid=(r // tr,),
        in_specs=[pl.BlockSpec((nlead, tr, c), lambda i: (0, i, 0))], out_specs=pl.BlockSpec((tr, c), lambda i: (i, 0)),
        out_shape=jax.ShapeDtypeStruct((r, c), parts.dtype), compiler_params=_cp())(parts)


def _adamw(w, g, m, v, name, tr):
    r, c = w.shape
    tr = min(tr, r)

    def body(w_ref, g_ref, m_ref, v_ref, d_ref, nm_ref, nv_ref):
        gg = g_ref[...]
        nm = ADAM_B1 * m_ref[...] + (1.0 - ADAM_B1) * gg
        nv = ADAM_B2 * v_ref[...] + (1.0 - ADAM_B2) * jnp.square(gg)
        m_hat = nm / (1.0 - ADAM_B1 ** ADAM_STEP)
        v_hat = nv / (1.0 - ADAM_B2 ** ADAM_STEP)
        d_ref[...] = -ADAM_LR * (m_hat / (jnp.sqrt(v_hat) + ADAM_EPS) + ADAM_WD * w_ref[...])
        nm_ref[...] = nm
        nv_ref[...] = nv

    blk = pl.BlockSpec((tr, c), lambda i: (i, 0))
    return pl.pallas_call(
        body, name=name, grid=(r // tr,), in_specs=[blk] * 4, out_specs=[blk] * 3,
        out_shape=[jax.ShapeDtypeStruct((r, c), F32)] * 3, compiler_params=_cp())(w, g, m, v)


def _rope_tables(positions):
    inv_freq = ROPE_BASE ** (-jnp.arange(0, 64, 2, dtype=F32) / 64)
    ang = positions.astype(F32)[:, None] * inv_freq
    cos, sin = jnp.cos(ang), jnp.sin(ang)
    t = positions.shape[0]
    rc = jnp.concatenate([jnp.ones((t, 128), F32), cos, cos, jnp.ones((t, 64), F32)], axis=1)
    rs = jnp.concatenate([jnp.zeros((t, 128), F32), sin, sin, jnp.zeros((t, 64), F32)], axis=1)
    return rc, rs


def _layer_params(l, w_in, w_uq, w_ukv, w_out, small):
    p = {}
    p["w_in"] = jnp.concatenate([w_in[l][:, :1984], jnp.zeros((1024, 64), w_in.dtype), w_in[l][:, 1984:]], axis=1)
    p["w_uq"] = jnp.pad(w_uq[l].reshape(384, 4, 192), ((0, 0), (0, 0), (0, 64))).reshape(384, 1024)
    p["w_ukv"] = w_ukv[l].reshape(256, 4, 2, 128).transpose(0, 2, 1, 3).reshape(256, 1024)
    p["w_out"] = w_out[l]
    p["pre_g"] = small["pre_norm_g"][l][None]
    p["post_g"] = small["post_norm_g"][l][None]
    p["sgu_w"] = small["sgu_w"][l].reshape(512, 128)
    p["sgu_wt"] = small["sgu_w"][l].transpose(0, 2, 1).reshape(512, 128)
    p["sgu_bias"] = jnp.repeat(small["sgu_b"][l].T, 64, axis=1)
    p["ln_g"] = small["sgu_ln_g"][l][None]
    p["ln_b"] = small["sgu_ln_b"][l][None]
    p["pool_wbd"] = _mx(jax.scipy.linalg.block_diag(*[small["pool_w"][l][gi] for gi in range(4)]))
    p["pool_scale"] = small["pool_scale"][l][None]
    p["gq"] = small["q_norm_g"][l][None]
    p["gkv"] = small["kv_norm_g"][l][None]
    return p


def _layer_fwd(l, x, p, rc, rs, target):
    z, h = _in_proj_fwd(x, p["pre_g"], p["w_in"], f"in_proj_fwd_{l}")
    yab = _mix_fwd(z, p["sgu_w"], p["sgu_bias"], p["ln_g"], p["ln_b"], p["pool_wbd"], p["pool_scale"], f"mix_fwd_{l}")
    qh, kh, vh = _qkv_fwd(z, rc, rs, p["w_uq"], p["w_ukv"], p["gq"], p["gkv"], f"qkv_fwd_{l}")
    o, yc, lse = _attn_fwd(qh, kh, vh, z, f"attn_fwd_{l}")
    outs = _out_proj_fwd(yab, yc, p["w_out"], x, p["post_g"], target, f"out_proj_fwd_{l}")
    saved = dict(x=x, z=z, h=h, yab=yab, qh=qh, kh=kh, vh=vh, o=o, yc=yc, lse=lse, y=outs[0])
    return saved, outs[1:]


def _layer_bwd(l, dout, sv, p, rc, rs):
    dycat, dw_out, dpost = _out_proj_bwd(dout, sv["y"], sv["yab"], sv["yc"], p["w_out"], p["post_g"], f"out_proj_bwd_{l}")
    dq, dgate, dk, dv = _attn_bwd(sv["qh"], sv["kh"], sv["vh"], sv["o"], sv["lse"], dycat, sv["z"], f"attn_bwd_{l}")
    dzc, dzk, dwq, dwkv, dgq, dgkv = _qkv_bwd(dq, dk, dv, sv["z"], rc, rs, p["w_uq"], p["w_ukv"], p["gq"], p["gkv"],
                                              f"qkv_bwd_{l}")
    dzm, dsw, dsb, dlng, dlnb, dpw, dps = _mix_bwd(sv["z"], dycat, p["sgu_w"], p["sgu_wt"], p["sgu_bias"], p["ln_g"],
                                                   p["ln_b"], p["pool_wbd"], p["pool_scale"], f"mix_bwd_{l}")
    dx, dw_in, dpre = _in_proj_bwd(dzm, dzc, dzk, dgate, sv["h"], sv["x"], dout, p["w_in"], p["pre_g"], f"in_proj_bwd_{l}")
    grads = {
        "pre_norm_g": dpre[0], "post_norm_g": dpost[0],
        "w_in": jnp.concatenate([dw_in[:, :1984], dw_in[:, 2048:]], axis=1),
        "sgu_w": dsw.reshape(4, 128, 128), "sgu_b": dsb[:, :4].T, "sgu_ln_g": dlng[0], "sgu_ln_b": dlnb[0],
        "pool_w": jnp.stack([dpw[64 * gi:64 * gi + 64, 64 * gi:64 * gi + 64] for gi in range(4)]),
        "pool_scale": dps[0], "q_norm_g": dgq[0],
        "w_uq": dwq.reshape(384, 4, 256)[:, :, :192].reshape(384, 768), "kv_norm_g": dgkv[0],
        "w_ukv": dwkv.reshape(256, 2, 4, 128).transpose(0, 2, 1, 3).reshape(256, 1024), "w_out": dw_out,
    }
    return dx, grads


SMALL_NAMES = ["pre_norm_g", "post_norm_g", "sgu_w", "sgu_b", "sgu_ln_g", "sgu_ln_b", "pool_w", "pool_scale",
               "q_norm_g", "kv_norm_g"]
BIG_NAMES = ["w_in", "w_uq", "w_ukv", "w_out"]
WEIGHT_NAMES = ["pre_norm_g", "post_norm_g", "w_in", "sgu_w", "sgu_b", "sgu_ln_g", "sgu_ln_b", "pool_w", "pool_scale",
                "q_norm_g", "w_uq", "kv_norm_g", "w_ukv", "w_out"]


def _local_step(x, positions, target, w_in, w_uq, w_ukv, w_out, small):
    rc, rs = _rope_tables(positions)
    params = [_layer_params(l, w_in, w_uq, w_ukv, w_out, small) for l in range(DEPTH)]
    saved = []
    for l in range(DEPTH):
        sv, outs = _layer_fwd(l, x, params[l], rc, rs, target if l == DEPTH - 1 else None)
        saved.append(sv)
        if l < DEPTH - 1:
            x = outs[0]
    dout, loss = outs
    grads = [None] * DEPTH
    for l in reversed(range(DEPTH)):
        dout, grads[l] = _layer_bwd(l, dout, saved[l], params[l], rc, rs)
    return loss[0, 0], dout, {k: jnp.stack([grads[l][k] for l in range(DEPTH)]) for k in WEIGHT_NAMES}


def _pack_small(tree, extra=None):
    pieces = [tree[k].reshape(-1) for k in SMALL_NAMES]
    pieces.append(jnp.zeros((1,), F32) if extra is None else extra.reshape(1))
    flat = jnp.concatenate(pieces)
    rows = -(-flat.shape[0] // 1024) * 8
    return jnp.pad(flat, (0, rows * 128 - flat.shape[0])).reshape(rows, 128)


def _unpack_small(packed, like):
    flat = packed.reshape(-1)
    out, off = {}, 0
    for k in SMALL_NAMES:
        size = like[k].size
        out[k] = flat[off:off + size].reshape(like[k].shape)
        off += size
    return out, flat[off]


def kernel(x, positions, pre_norm_g, post_norm_g, w_in, sgu_w, sgu_b, sgu_ln_g, sgu_ln_b, pool_w, pool_scale, q_norm_g, w_uq, kv_norm_g, w_ukv, w_out, loss_target, m_pre_norm_g, m_post_norm_g, m_w_in, m_sgu_w, m_sgu_b, m_sgu_ln_g, m_sgu_ln_b, m_pool_w, m_pool_scale, m_q_norm_g, m_w_uq, m_kv_norm_g, m_w_ukv, m_w_out, v_pre_norm_g, v_post_norm_g, v_w_in, v_sgu_w, v_sgu_b, v_sgu_ln_g, v_sgu_ln_b, v_pool_w, v_pool_scale, v_q_norm_g, v_w_uq, v_kv_norm_g, v_w_ukv, v_w_out):
    w = dict(pre_norm_g=pre_norm_g, post_norm_g=post_norm_g, w_in=w_in, sgu_w=sgu_w, sgu_b=sgu_b, sgu_ln_g=sgu_ln_g,
             sgu_ln_b=sgu_ln_b, pool_w=pool_w, pool_scale=pool_scale, q_norm_g=q_norm_g, w_uq=w_uq, kv_norm_g=kv_norm_g,
             w_ukv=w_ukv, w_out=w_out)
    m = dict(pre_norm_g=m_pre_norm_g, post_norm_g=m_post_norm_g, w_in=m_w_in, sgu_w=m_sgu_w, sgu_b=m_sgu_b,
             sgu_ln_g=m_sgu_ln_g, sgu_ln_b=m_sgu_ln_b, pool_w=m_pool_w, pool_scale=m_pool_scale, q_norm_g=m_q_norm_g,
             w_uq=m_w_uq, kv_norm_g=m_kv_norm_g, w_ukv=m_w_ukv, w_out=m_w_out)
    v = dict(pre_norm_g=v_pre_norm_g, post_norm_g=v_post_norm_g, w_in=v_w_in, sgu_w=v_sgu_w, sgu_b=v_sgu_b,
             sgu_ln_g=v_sgu_ln_g, sgu_ln_b=v_sgu_ln_b, pool_w=v_pool_w, pool_scale=v_pool_scale, q_norm_g=v_q_norm_g,
             w_uq=v_w_uq, kv_norm_g=v_kv_norm_g, w_ukv=v_w_ukv, w_out=v_w_out)

    core = lax.axis_index("c")
    chip = 2 * lax.axis_index("x") + lax.axis_index("y")
    shards = [_mx(w[k]) for k in BIG_NAMES]
    gathered = _gather_weights(shards)
    g_in, g_uq, g_ukv, g_out = [lax.dynamic_update_slice(g, s, (2 * chip, 0, 0)) for g, s in zip(gathered, shards)]
    cols = lambda g: g.reshape((4, 2) + g.shape[1:]).transpose(1, 2, 0, 3).reshape(2, g.shape[1], 4 * g.shape[2])
    full_out = g_out.reshape(4, 2, 256, 1024).transpose(1, 0, 2, 3).reshape(2, 1024, 1024)
    loss, dx, grads = _local_step(x[0], positions[0], loss_target[0], cols(g_in), cols(g_uq), cols(g_ukv), full_out, w)

    split_cols = lambda g: g.reshape(2, g.shape[1], 4, g.shape[2] // 4).transpose(2, 0, 1, 3).reshape(8, g.shape[1], g.shape[2] // 4)
    parts = [split_cols(grads["w_in"]), split_cols(grads["w_uq"]), split_cols(grads["w_ukv"]),
             grads["w_out"].reshape(2, 4, 256, 1024).transpose(1, 0, 2, 3).reshape(8, 256, 1024)]
    common = _pack_small(grads, loss)
    got = _pair_exchange(parts, common)
    where = jnp.stack([core, chip]).astype(jnp.int32)
    wire_parts = [_pair_sum_wire(where, parts[a], got[a], f"pair_sum_{BIG_NAMES[a]}", 128) for a in range(4)]
    chip_common = _sum_leading(jnp.stack([common, got[4]]), "pair_sum_small", common.shape[0])
    received = _chip_exchange(wire_parts, chip_common)
    sums = [_sum_chips(where, parts[a], got[a], received[a], f"sum_{BIG_NAMES[a]}", 128) for a in range(4)]
    all_common = lax.dynamic_update_slice(received[4], chip_common[None], (chip, 0, 0))
    small_sum, loss = _unpack_small(_sum_leading(all_common, "sum_small", all_common.shape[1]), w)
    others = _sibling_exchange(sums)
    total = dict(small_sum)
    for a, k in enumerate(BIG_NAMES):
        total[k] = jnp.where(core == 0, jnp.stack([sums[a], others[a]]), jnp.stack([others[a], sums[a]]))

    packed = _adamw(_pack_small(w), _pack_small(total), _pack_small(m), _pack_small(v), "adamw_small", 2048)
    small_out = [_unpack_small(pk, w)[0] for pk in packed]
    delta, new_m, new_v = {}, {}, {}
    for k in SMALL_NAMES:
        delta[k], new_m[k], new_v[k] = (so[k] for so in small_out)
    for k in BIG_NAMES:
        shape = w[k].shape
        flat = lambda a: a.reshape(shape[0] * shape[1], shape[2])
        res = _adamw(flat(w[k]), flat(total[k]), flat(m[k]), flat(v[k]), f"adamw_{k}", 256)
        delta[k], new_m[k], new_v[k] = (r.reshape(shape) for r in res)

    return (loss, dx[None], *[total[k] for k in WEIGHT_NAMES], *[delta[k] for k in WEIGHT_NAMES],
            *[new_m[k] for k in WEIGHT_NAMES], *[new_v[k] for k in WEIGHT_NAMES])
```

```python
import jax
import jax.numpy as jnp
from jax import lax
from jax.experimental import pallas as pl
from jax.experimental.pallas import tpu as pltpu

F32 = jnp.float32
MXU_DTYPE = jnp.bfloat16
WIRE_DTYPE = jnp.bfloat16
EPS = 1e-6
NEG_INF = -1e30
DEPTH = 2
N_HEADS = 4
QK_PAD = 256
V_DIM = 128
SCALE = 192 ** -0.5
LOG2E = 1.4426950408889634
ROPE_BASE = 10000.0
ADAM_LR, ADAM_B1, ADAM_B2, ADAM_EPS, ADAM_WD, ADAM_STEP = 0.001, 0.9, 0.999, 1e-08, 0.01, 10
VMEM_LIMIT_BYTES = 56 * 1024 * 1024
MESH = pl.DeviceIdType.MESH
ANY = pl.BlockSpec(memory_space=pl.ANY)

Z_MIX, Z_C, Z_KR, Z_GATE = 1280, 640, 128, 512
Z_W = Z_MIX + Z_C + Z_KR + Z_GATE


def _cp(n_axes=1):
    return pltpu.CompilerParams(dimension_semantics=("arbitrary",) * n_axes, vmem_limit_bytes=VMEM_LIMIT_BYTES)


def _dot(a, b):
    return lax.dot_general(a, b, (((1,), (0,)), ((), ())), preferred_element_type=F32)


def _dot_nt(a, b):
    return lax.dot_general(a, b, (((1,), (1,)), ((), ())), preferred_element_type=F32)


def _dot_tn(a, b):
    return lax.dot_general(a, b, (((0,), (0,)), ((), ())), preferred_element_type=F32)


def _mx(a):
    return a.astype(MXU_DTYPE)


def _silu_and_grad(g):
    sg = jax.nn.sigmoid(g)
    return g * sg, sg * (1.0 + g * (1.0 - sg))


def _rms(x, g):
    r = lax.rsqrt(jnp.mean(x * x, axis=-1, keepdims=True) + EPS)
    return x * r * g, r


def _rms_bwd(x, r, g, dy):
    xhat = x * r
    dyg = dy * g
    dx = r * (dyg - xhat * jnp.mean(dyg * xhat, axis=-1, keepdims=True))
    return dx, dy * xhat


def _acc(ref, val, first):
    @pl.when(first)
    def _():
        ref[...] = val

    @pl.when(jnp.logical_not(first))
    def _():
        ref[...] += val


def _colsum(a):
    return jnp.sum(a, axis=0, keepdims=True)


def _in_proj_fwd(x, g, w, name, tm=512):
    t, d = x.shape
    n = w.shape[1]
    tm = min(tm, t)

    def body(x_ref, g_ref, w_ref, z_ref, h_ref):
        h, _ = _rms(x_ref[...], g_ref[...])
        h = _mx(h)
        h_ref[...] = h
        z_ref[...] = _dot(h, w_ref[...])

    return pl.pallas_call(
        body, name=name, grid=(t // tm,),
        in_specs=[pl.BlockSpec((tm, d), lambda i: (i, 0)), pl.BlockSpec((1, d), lambda i: (0, 0)),
                  pl.BlockSpec((d, n), lambda i: (0, 0))],
        out_specs=[pl.BlockSpec((tm, n), lambda i: (i, 0)), pl.BlockSpec((tm, d), lambda i: (i, 0))],
        out_shape=[jax.ShapeDtypeStruct((t, n), F32), jax.ShapeDtypeStruct((t, d), MXU_DTYPE)],
        compiler_params=_cp())(x, g, w)


def _in_proj_bwd(dz_mix, dz_c, dz_kr, dz_gate, h, x, d_res, w, g, name, tm=256):
    t, d = x.shape
    n = w.shape[1]
    tm = min(tm, t)

    def body(dm_ref, dc_ref, dk_ref, dg_ref, h_ref, x_ref, dres_ref, w_ref, g_ref, dx_ref, dw_ref, dgn_ref):
        first = pl.program_id(0) == 0
        dz = jnp.concatenate([dm_ref[...], dc_ref[...], dk_ref[...], dg_ref[...]], axis=1)

        @pl.when(first)
        def _():
            dw_ref[...] = jnp.zeros(dw_ref.shape, F32)

        hb = h_ref[...]
        for c0 in range(0, n, 512):
            dw_ref[:, c0:c0 + 512] += _dot_tn(hb, dz[:, c0:c0 + 512])
        dh = _dot_nt(dz, w_ref[...])
        xf = x_ref[...]
        r = lax.rsqrt(jnp.mean(xf * xf, axis=-1, keepdims=True) + EPS)
        dx, dgt = _rms_bwd(xf, r, g_ref[...], dh)
        dx_ref[...] = dx + dres_ref[...]
        _acc(dgn_ref, _colsum(dgt), first)

    row = lambda wd: pl.BlockSpec((tm, wd), lambda i: (i, 0))
    fixed = lambda a, b: pl.BlockSpec((a, b), lambda i: (0, 0))
    return pl.pallas_call(
        body, name=name, grid=(t // tm,),
        in_specs=[row(Z_MIX), row(Z_C), row(Z_KR), row(Z_GATE), row(d), row(d), row(d), fixed(d, n), fixed(1, d)],
        out_specs=[row(d), fixed(d, n), fixed(1, d)],
        out_shape=[jax.ShapeDtypeStruct((t, d), F32), jax.ShapeDtypeStruct((d, n), F32),
                   jax.ShapeDtypeStruct((1, d), F32)],
        compiler_params=_cp())(dz_mix, dz_c, dz_kr, dz_gate, h, x, d_res, w, g)


def _lane_group(shape):
    return lax.broadcasted_iota(jnp.int32, shape, 1) // 64


def _select_group(vals):
    grp = _lane_group(vals[0].shape)
    out = vals[3]
    for gi in (2, 1, 0):
        out = jnp.where(grp == gi, vals[gi], out)
    return out


def _sgu_mask(transposed):
    r = (lax.broadcasted_iota(jnp.int32, (512, 128), 0) % 128) // 64
    c = lax.broadcasted_iota(jnp.int32, (512, 128), 1) // 64
    return (r <= c) if transposed else (c <= r)


def _sgu_apply(wstack, vb, nblk):
    outs = []
    for n in range(nblk):
        r = _dot(wstack, vb[n * 128:(n + 1) * 128, :])
        outs.append(_select_group([r[hh * 128:(hh + 1) * 128, :] for hh in range(4)]))
    return jnp.concatenate(outs, axis=0)


def _layer_norm(v, g, b):
    mu = jnp.mean(v, axis=-1, keepdims=True)
    vc = v - mu
    rstd = lax.rsqrt(jnp.mean(vc * vc, axis=-1, keepdims=True) + EPS)
    vhat = vc * rstd
    return vhat * g + b, vhat, rstd


def _pool_counts(t0, n):
    t = t0 + lax.broadcasted_iota(jnp.int32, (n, 256), 0)
    w = _select_group([jnp.full((n, 256), wv, jnp.int32) for wv in (2, 4, 8, 16)])
    return jnp.minimum(t + 1, w).astype(F32)


def _pooled(p, halo, t0):
    tm = p.shape[0]
    ext = jnp.concatenate([halo, p], axis=0)
    s2 = ext + pltpu.roll(ext, 1, 0)
    s4 = s2 + pltpu.roll(s2, 2, 0)
    s8 = s4 + pltpu.roll(s4, 4, 0)
    s16 = s8 + pltpu.roll(s8, 8, 0)
    sel = _select_group([s2, s4, s8, s16])[16:, :]
    return sel / _pool_counts(t0, tm) - p


def _pooled_bwd(dpool, dpool_halo, t0):
    tm = dpool.shape[0]
    n = tm + 16
    ext = jnp.concatenate([dpool, dpool_halo], axis=0) / _pool_counts(t0, n)
    f2 = ext + pltpu.roll(ext, n - 1, 0)
    f4 = f2 + pltpu.roll(f2, n - 2, 0)
    f8 = f4 + pltpu.roll(f4, n - 4, 0)
    f16 = f8 + pltpu.roll(f8, n - 8, 0)
    return _select_group([f2, f4, f8, f16])[:tm, :] - dpool


def _mix_specs(t, tm):
    nt16 = t // 16
    zrow = pl.BlockSpec((tm, Z_MIX), lambda i: (i, 0))
    prev_halo = pl.BlockSpec((16, 256), lambda i: (jnp.maximum(i * (tm // 16) - 1, 0), 3))
    fixed = lambda a, b: pl.BlockSpec((a, b), lambda i: (0, 0))
    params = [fixed(512, 128), fixed(128, 256), fixed(1, 256), fixed(1, 256), fixed(256, 256), fixed(1, 256)]
    return nt16, zrow, prev_halo, fixed, params


def _mix_fwd(z, sgu_w, sgu_bias, ln_g, ln_b, pool_wbd, pool_scale, name, tm=256):
    t = z.shape[0]
    tm = min(tm, t)
    _, zrow, prev_halo, _, params = _mix_specs(t, tm)

    def body(z_ref, halo_ref, w_ref, bias_ref, lng_ref, lnb_ref, pw_ref, ps_ref, y_ref):
        i = pl.program_id(0)
        u, v, gate = z_ref[:, 0:256], z_ref[:, 256:512], z_ref[:, 512:768]
        p, pgate = z_ref[:, 768:1024], z_ref[:, 1024:1280]
        vn, _, _ = _layer_norm(v, lng_ref[...], lnb_ref[...])
        wm = _mx(jnp.where(_sgu_mask(False), w_ref[...], 0.0))
        mixed = _sgu_apply(wm, _mx(vn), tm // 128) + jnp.tile(bias_ref[...], (tm // 128, 1))
        ya = u * mixed * _silu_and_grad(gate)[0]
        halo = jnp.where(i > 0, halo_ref[...], 0.0)
        pooled = _pooled(p, halo, i * tm)
        yb = _dot(_mx(pooled), pw_ref[...]) * ps_ref[...] * _silu_and_grad(pgate)[0]
        y_ref[...] = _mx(jnp.concatenate([ya, yb], axis=1))

    return pl.pallas_call(
        body, name=name, grid=(t // tm,),
        in_specs=[zrow, prev_halo] + params,
        out_specs=pl.BlockSpec((tm, 512), lambda i: (i, 0)),
        out_shape=jax.ShapeDtypeStruct((t, 512), MXU_DTYPE),
        compiler_params=_cp())(z, z, sgu_w, sgu_bias, ln_g, ln_b, pool_wbd, pool_scale)


def _mix_bwd(z, dycat, sgu_w, sgu_wt, sgu_bias, ln_g, ln_b, pool_wbd, pool_scale, name, tm=256):
    t = z.shape[0]
    tm = min(tm, t)
    nt16, zrow, prev_halo, fixed, params = _mix_specs(t, tm)
    nblk = tm // 128
    last = t // tm - 1

    def body(z_ref, halo_ref, zn_ref, dy_ref, dyn_ref, w_ref, wt_ref, bias_ref, lng_ref, lnb_ref, pw_ref, ps_ref,
             dz_ref, dw_ref, db_ref, dlng_ref, dlnb_ref, dpw_ref, dps_ref):
        i = pl.program_id(0)
        first = i == 0
        u, v, gate = z_ref[:, 0:256], z_ref[:, 256:512], z_ref[:, 512:768]
        p, pgate = z_ref[:, 768:1024], z_ref[:, 1024:1280]
        dya, dyb = dy_ref[:, 0:256], dy_ref[:, 256:512]
        vn, vhat, rstd = _layer_norm(v, lng_ref[...], lnb_ref[...])
        vnb = _mx(vn)
        wm = _mx(jnp.where(_sgu_mask(False), w_ref[...], 0.0))
        wmt = _mx(jnp.where(_sgu_mask(True), wt_ref[...], 0.0))
        mixed = _sgu_apply(wm, vnb, nblk) + jnp.tile(bias_ref[...], (nblk, 1))
        silu, dsilu = _silu_and_grad(gate)
        t1 = u * mixed
        d_gate = dya * t1 * dsilu
        d_t1 = dya * silu
        d_u = d_t1 * mixed
        d_mixed = d_t1 * u
        dmb = _mx(d_mixed)
        d_vn = _sgu_apply(wmt, dmb, nblk)
        grp = _lane_group((128, 256))
        lane = lax.broadcasted_iota(jnp.int32, (128, 128), 1)
        dws = [jnp.zeros((128, 128), F32) for _ in range(4)]
        dbias = jnp.zeros((128, 128), F32)
        for n in range(nblk):
            dm_n, dmb_n, vnb_n = d_mixed[n * 128:(n + 1) * 128], dmb[n * 128:(n + 1) * 128], vnb[n * 128:(n + 1) * 128]
            for hh in range(4):
                dws[hh] = dws[hh] + _dot_nt(jnp.where(grp == hh, dmb_n, jnp.zeros_like(dmb_n)), vnb_n)
                rs = jnp.sum(jnp.where(grp == hh, dm_n, 0.0), axis=-1, keepdims=True)
                dbias = dbias + jnp.where(lane == hh, rs, 0.0)
        _acc(dw_ref, jnp.concatenate(dws, axis=0), first)
        _acc(db_ref, dbias, first)
        _acc(dlng_ref, _colsum(d_vn * vhat), first)
        _acc(dlnb_ref, _colsum(d_vn), first)
        dvh = d_vn * lng_ref[...]
        d_v = rstd * (dvh - jnp.mean(dvh, axis=-1, keepdims=True) - vhat * jnp.mean(dvh * vhat, axis=-1, keepdims=True))

        @pl.when(i == last)
        def _():
            dw_ref[...] = jnp.where(_sgu_mask(False), dw_ref[...], 0.0)

        halo = jnp.where(i > 0, halo_ref[...], 0.0)
        pooled = _pooled(p, halo, i * tm)
        pooled_b = _mx(pooled)
        mixedp = _dot(pooled_b, pw_ref[...])
        psilu, pdsilu = _silu_and_grad(pgate)
        d_pgate = dyb * (mixedp * ps_ref[...]) * pdsilu
        d_ms = dyb * psilu
        _acc(dps_ref, _colsum(d_ms * mixedp), first)
        dmpb = _mx(d_ms * ps_ref[...])
        _acc(dpw_ref, _dot_tn(pooled_b, dmpb), first)
        d_pooled = _dot_nt(dmpb, pw_ref[...])
        dmp_halo = _mx(dyn_ref[...] * _silu_and_grad(zn_ref[...])[0] * ps_ref[...])
        d_pooled_halo = jnp.where(i < last, _dot_nt(dmp_halo, pw_ref[...]), 0.0)
        d_p = _pooled_bwd(d_pooled, d_pooled_halo, i * tm)
        dz_ref[...] = _mx(jnp.concatenate([d_u, d_v, d_gate, d_p, d_pgate], axis=1))

    nxt = lambda i: jnp.minimum((i + 1) * (tm // 16), nt16 - 1)
    return pl.pallas_call(
        body, name=name, grid=(t // tm,),
        in_specs=[zrow, prev_halo, pl.BlockSpec((16, 256), lambda i: (nxt(i), 4)),
                  pl.BlockSpec((tm, 512), lambda i: (i, 0)), pl.BlockSpec((16, 256), lambda i: (nxt(i), 1)),
                  params[0], fixed(512, 128)] + params[1:],
        out_specs=[pl.BlockSpec((tm, Z_MIX), lambda i: (i, 0)), fixed(512, 128), fixed(128, 128), fixed(1, 256),
                   fixed(1, 256), fixed(256, 256), fixed(1, 256)],
        out_shape=[jax.ShapeDtypeStruct((t, Z_MIX), MXU_DTYPE), jax.ShapeDtypeStruct((512, 128), F32),
                   jax.ShapeDtypeStruct((128, 128), F32), jax.ShapeDtypeStruct((1, 256), F32),
                   jax.ShapeDtypeStruct((1, 256), F32), jax.ShapeDtypeStruct((256, 256), F32),
                   jax.ShapeDtypeStruct((1, 256), F32)],
        compiler_params=_cp())(z, z, z, dycat, dycat, sgu_w, sgu_wt, sgu_bias, ln_g, ln_b, pool_wbd, pool_scale)


def _rot_half(x, transpose):
    w = x.shape[1]
    lane = lax.broadcasted_iota(jnp.int32, x.shape, 1) % min(w, 256)
    base = 128 if w >= 256 else 0
    lo = jnp.logical_and(lane >= base, lane < base + 32)
    hi = jnp.logical_and(lane >= base + 32, lane < base + 64)
    up = pltpu.roll(x, w - 32, 1)
    down = pltpu.roll(x, 32, 1)
    if transpose:
        return jnp.where(lo, up, jnp.where(hi, -down, 0.0))
    return jnp.where(lo, -up, jnp.where(hi, down, 0.0))


def _rope(x, c, s):
    return x * c + _rot_half(x, False) * s


def _rope_bwd(dy, c, s):
    return dy * c + _rot_half(dy * s, True)


def _qkv_fwd(z, rc, rs, w_uq, w_ukv, gq, gkv, name, tm=256):
    t = z.shape[0]
    tm = min(tm, t)

    def body(zc_ref, zk_ref, rc_ref, rs_ref, wq_ref, wkv_ref, gq_ref, gkv_ref, q_ref, k_ref, v_ref):
        cq, ckv = zc_ref[:, 0:384], zc_ref[:, 384:640]
        c, s = rc_ref[...], rs_ref[...]
        qn, _ = _rms(cq, gq_ref[...])
        q = _rope(_dot(_mx(qn), wq_ref[...]), jnp.tile(c, (1, N_HEADS)), jnp.tile(s, (1, N_HEADS)))
        kvn, _ = _rms(ckv, gkv_ref[...])
        kv = _dot(_mx(kvn), wkv_ref[...])
        kpe = _rope(zk_ref[...], c[:, 128:256], s[:, 128:256])
        for hh in range(N_HEADS):
            q_ref[hh] = _mx(q[:, hh * QK_PAD:(hh + 1) * QK_PAD])
            k_ref[hh] = _mx(jnp.concatenate([kv[:, hh * 128:(hh + 1) * 128], kpe], axis=1))
            v_ref[hh] = _mx(kv[:, 512 + hh * 128:512 + (hh + 1) * 128])

    fixed = lambda a, b: pl.BlockSpec((a, b), lambda i: (0, 0))
    heads = lambda wd: pl.BlockSpec((N_HEADS, tm, wd), lambda i: (0, i, 0))
    return pl.pallas_call(
        body, name=name, grid=(t // tm,),
        in_specs=[pl.BlockSpec((tm, Z_C), lambda i: (i, Z_MIX // Z_C)),
                  pl.BlockSpec((tm, Z_KR), lambda i: (i, (Z_MIX + Z_C) // Z_KR)),
                  pl.BlockSpec((tm, 256), lambda i: (i, 0)), pl.BlockSpec((tm, 256), lambda i: (i, 0)),
                  fixed(384, 1024), fixed(256, 1024), fixed(1, 384), fixed(1, 256)],
        out_specs=[heads(QK_PAD), heads(QK_PAD), heads(V_DIM)],
        out_shape=[jax.ShapeDtypeStruct((N_HEADS, t, QK_PAD), MXU_DTYPE),
                   jax.ShapeDtypeStruct((N_HEADS, t, QK_PAD), MXU_DTYPE),
                   jax.ShapeDtypeStruct((N_HEADS, t, V_DIM), MXU_DTYPE)],
        compiler_params=_cp())(z, z, rc, rs, w_uq, w_ukv, gq, gkv)


def _qkv_bwd(dq, dk, dv, z, rc, rs, w_uq, w_ukv, gq, gkv, name, tm=256):
    t = z.shape[0]
    tm = min(tm, t)

    def body(dq_ref, dk_ref, dv_ref, zc_ref, rc_ref, rs_ref, wq_ref, wkv_ref, gq_ref, gkv_ref,
             dzc_ref, dzk_ref, dwq_ref, dwkv_ref, dgq_ref, dgkv_ref):
        first = pl.program_id(0) == 0
        cq, ckv = zc_ref[:, 0:384], zc_ref[:, 384:640]
        c, s = rc_ref[...], rs_ref[...]
        dq_all = jnp.concatenate([dq_ref[hh] for hh in range(N_HEADS)], axis=1)
        dqp = _mx(_rope_bwd(dq_all, jnp.tile(c, (1, N_HEADS)), jnp.tile(s, (1, N_HEADS))))
        qn, rq = _rms(cq, gq_ref[...])
        _acc(dwq_ref, _dot_tn(_mx(qn), dqp), first)
        d_cq, dgq_t = _rms_bwd(cq, rq, gq_ref[...], _dot_nt(dqp, wq_ref[...]))
        _acc(dgq_ref, _colsum(dgq_t), first)
        dkpe = dk_ref[0][:, 128:256]
        for hh in range(1, N_HEADS):
            dkpe = dkpe + dk_ref[hh][:, 128:256]
        dzk_ref[...] = _mx(_rope_bwd(dkpe, c[:, 128:256], s[:, 128:256]))
        dkv = _mx(jnp.concatenate([dk_ref[hh][:, 0:128] for hh in range(N_HEADS)]
                                  + [dv_ref[hh] for hh in range(N_HEADS)], axis=1))
        kvn, rkv = _rms(ckv, gkv_ref[...])
        _acc(dwkv_ref, _dot_tn(_mx(kvn), dkv), first)
        d_ckv, dgkv_t = _rms_bwd(ckv, rkv, gkv_ref[...], _dot_nt(dkv, wkv_ref[...]))
        _acc(dgkv_ref, _colsum(dgkv_t), first)
        dzc_ref[...] = _mx(jnp.concatenate([d_cq, d_ckv], axis=1))

    fixed = lambda a, b: pl.BlockSpec((a, b), lambda i: (0, 0))
    heads = lambda wd: pl.BlockSpec((N_HEADS, tm, wd), lambda i: (0, i, 0))
    return pl.pallas_call(
        body, name=name, grid=(t // tm,),
        in_specs=[heads(QK_PAD), heads(QK_PAD), heads(V_DIM), pl.BlockSpec((tm, Z_C), lambda i: (i, Z_MIX // Z_C)),
                  pl.BlockSpec((tm, 256), lambda i: (i, 0)), pl.BlockSpec((tm, 256), lambda i: (i, 0)),
                  fixed(384, 1024), fixed(256, 1024), fixed(1, 384), fixed(1, 256)],
        out_specs=[pl.BlockSpec((tm, Z_C), lambda i: (i, 0)), pl.BlockSpec((tm, Z_KR), lambda i: (i, 0)),
                   fixed(384, 1024), fixed(256, 1024), fixed(1, 384), fixed(1, 256)],
        out_shape=[jax.ShapeDtypeStruct((t, Z_C), MXU_DTYPE), jax.ShapeDtypeStruct((t, Z_KR), MXU_DTYPE),
                   jax.ShapeDtypeStruct((384, 1024), F32), jax.ShapeDtypeStruct((256, 1024), F32),
                   jax.ShapeDtypeStruct((1, 384), F32), jax.ShapeDtypeStruct((1, 256), F32)],
        compiler_params=_cp())(dq, dk, dv, z, rc, rs, w_uq, w_ukv, gq, gkv)


def _loop_in_long_trips(n, body):
    def two(t, carry):
        return body(2 * t + 1, body(2 * t, carry))

    def four(t, carry):
        return two(2 * t + 1, two(2 * t, carry))

    lax.fori_loop(0, n // 4, four, 0)
    lax.fori_loop(2 * (n // 4), n // 2, two, 0)
    lax.fori_loop(2 * (n // 2), n, body, 0)


def _chunk_mask(tq):
    r = lax.broadcasted_iota(jnp.int32, (tq, tq), 0) // 64
    c = lax.broadcasted_iota(jnp.int32, (tq, tq), 1) // 64
    return c <= r


def _gate_block(tq):
    return pl.BlockSpec((tq, 128), lambda h, i: (i, (Z_MIX + Z_C + Z_KR) // 128 + h))


def _attn_fwd(qh, kh, vh, z, name, tq=512):
    t = qh.shape[1]
    tq = min(tq, t)

    def body(q_ref, g_ref, k_hbm, v_hbm, o_ref, yc_ref, lse_ref, k_v, v_v, m_s, acc_s, s_a, s_b, mx_a, mx_b, sem):
        h, i = pl.program_id(0), pl.program_id(1)

        @pl.when(i == 0)
        def _():
            ck = pltpu.make_async_copy(k_hbm.at[h], k_v, sem.at[0])
            cv = pltpu.make_async_copy(v_hbm.at[h], v_v.at[:, 0:V_DIM], sem.at[1])
            ck.start()
            cv.start()
            v_v[:, V_DIM:2 * V_DIM] = jnp.ones((t, V_DIM), MXU_DTYPE)
            ck.wait()
            cv.wait()

        q = q_ref[...]
        m_s[...] = jnp.full(m_s.shape, NEG_INF, F32)
        acc_s[...] = jnp.zeros(acc_s.shape, F32)

        def keys(j):
            return pl.ds(pl.multiple_of(j * tq, tq), tq)

        def scores(s_ref, mx_ref, j):
            s = _dot_nt(q, k_v[keys(j), :]) * (SCALE * LOG2E)
            s_ref[...] = s
            mx_ref[...] = jnp.broadcast_to(jnp.max(s, axis=-1, keepdims=True), mx_ref.shape)

        def softmax_pv(s_ref, mx_ref, j, masked):
            s = s_ref[...]
            if masked:
                s = jnp.where(_chunk_mask(tq), s, NEG_INF)
                row_max = jnp.max(s, axis=-1, keepdims=True)
            else:
                row_max = mx_ref[...]
            m_old = m_s[...]
            m_new = jnp.maximum(m_old, row_max)
            p = jnp.exp2(s - jnp.tile(m_new, (1, tq // 128)))
            alpha = jnp.exp2(m_old - m_new)
            m_s[...] = m_new
            acc_s[...] = jnp.tile(alpha, (1, 2)) * acc_s[...] + _dot(_mx(p), v_v[keys(j), :])

        scores(s_a, mx_a, 0)

        def pair(pp, carry):
            scores(s_b, mx_b, 2 * pp + 1)
            softmax_pv(s_a, mx_a, 2 * pp, False)
            scores(s_a, mx_a, 2 * pp + 2)
            softmax_pv(s_b, mx_b, 2 * pp + 1, False)
            return carry

        _loop_in_long_trips(i // 2, pair)

        @pl.when(i % 2 == 1)
        def _():
            scores(s_b, mx_b, i)
            softmax_pv(s_a, mx_a, i - 1, False)
            softmax_pv(s_b, mx_b, i, True)

        @pl.when(i % 2 == 0)
        def _():
            softmax_pv(s_a, mx_a, i, True)

        l = acc_s[:, V_DIM:2 * V_DIM]
        o = acc_s[:, 0:V_DIM] / l
        o_ref[...] = o
        yc_ref[...] = _mx(o * _silu_and_grad(g_ref[...])[0])
        lse_ref[...] = m_s[...] + jnp.log2(l)

    return pl.pallas_call(
        body, name=name, grid=(N_HEADS, t // tq),
        in_specs=[pl.BlockSpec((None, tq, QK_PAD), lambda h, i: (h, i, 0)), _gate_block(tq), ANY, ANY],
        out_specs=[pl.BlockSpec((tq, 128), lambda h, i: (i, h)), pl.BlockSpec((tq, 128), lambda h, i: (i, h)),
                   pl.BlockSpec((None, tq, 128), lambda h, i: (h, i, 0))],
        out_shape=[jax.ShapeDtypeStruct((t, N_HEADS * V_DIM), F32), jax.ShapeDtypeStruct((t, N_HEADS * V_DIM), MXU_DTYPE),
                   jax.ShapeDtypeStruct((N_HEADS, t, 128), F32)],
        scratch_shapes=[pltpu.VMEM((t, QK_PAD), MXU_DTYPE), pltpu.VMEM((t, 2 * V_DIM), MXU_DTYPE),
                        pltpu.VMEM((tq, 128), F32), pltpu.VMEM((tq, 2 * V_DIM), F32),
                        pltpu.VMEM((tq, tq), F32), pltpu.VMEM((tq, tq), F32), pltpu.VMEM((tq, 128), F32),
                        pltpu.VMEM((tq, 128), F32), pltpu.SemaphoreType.DMA((2,))],
        compiler_params=_cp(2))(qh, z, kh, vh)


def _attn_bwd(qh, kh, vh, o, lse, dycat, z, name, tq=512):
    t = qh.shape[1]
    tq = min(tq, t)
    nq = t // tq

    def body(q_ref, o_ref, lse_ref, dy_ref, g_ref, k_hbm, v_hbm, dq_ref, dgate_ref, dk_hbm, dv_hbm,
             k_v, v_v, dk_acc, dv_acc, dq_acc, delta_s, s_a, dp_a, s_b, dp_b, sem):
        h, i = pl.program_id(0), pl.program_id(1)

        @pl.when(i == 0)
        def _():
            ck = pltpu.make_async_copy(k_hbm.at[h], k_v, sem.at[0])
            cv = pltpu.make_async_copy(v_hbm.at[h], v_v, sem.at[1])
            ck.start()
            cv.start()
            dk_acc[...] = jnp.zeros(dk_acc.shape, F32)
            dv_acc[...] = jnp.zeros(dv_acc.shape, F32)
            ck.wait()
            cv.wait()

        gate, dy, of = g_ref[...], dy_ref[...], o_ref[...]
        silu, dsilu = _silu_and_grad(gate)
        do = dy * silu
        delta = jnp.sum(do * of, axis=-1, keepdims=True)
        dgate_ref[...] = _mx(dy * of * dsilu)
        dob = _mx(do)
        q = q_ref[...]
        delta_s[...] = jnp.broadcast_to(delta, delta_s.shape)
        dq_acc[...] = jnp.zeros(dq_acc.shape, F32)

        def keys(j):
            return pl.ds(pl.multiple_of(j * tq, tq), tq)

        def scores(s_ref, dp_ref, j):
            s_ref[...] = _dot_nt(q, k_v[keys(j), :]) * (SCALE * LOG2E)
            dp_ref[...] = _dot_nt(dob, v_v[keys(j), :])

        def grads(s_ref, dp_ref, j, masked):
            ks = keys(j)
            p = jnp.exp2(s_ref[...] - jnp.tile(lse_ref[...], (1, tq // 128)))
            if masked:
                p = jnp.where(_chunk_mask(tq), p, 0.0)
            ds = p * (dp_ref[...] - jnp.tile(delta_s[...], (1, tq // 128))) * SCALE
            pb, dsb = _mx(p), _mx(ds)
            dq_acc[...] += _dot(dsb, k_v[ks, :])
            dk_acc[ks, :] += _dot_tn(dsb, q)
            dv_acc[ks, :] += _dot_tn(pb, dob)

        scores(s_a, dp_a, 0)

        def pair(pp, carry):
            scores(s_b, dp_b, 2 * pp + 1)
            grads(s_a, dp_a, 2 * pp, False)
            scores(s_a, dp_a, 2 * pp + 2)
            grads(s_b, dp_b, 2 * pp + 1, False)
            return carry

        _loop_in_long_trips(i // 2, pair)

        @pl.when(i % 2 == 1)
        def _():
            scores(s_b, dp_b, i)
            grads(s_a, dp_a, i - 1, False)
            grads(s_b, dp_b, i, True)

        @pl.when(i % 2 == 0)
        def _():
            grads(s_a, dp_a, i, True)

        dq_ref[...] = dq_acc[...]

        @pl.when(i == nq - 1)
        def _():
            ck = pltpu.make_async_copy(dk_acc, dk_hbm.at[h], sem.at[0])
            cv = pltpu.make_async_copy(dv_acc, dv_hbm.at[h], sem.at[1])
            ck.start()
            cv.start()
            ck.wait()
            cv.wait()

    return pl.pallas_call(
        body, name=name, grid=(N_HEADS, nq),
        in_specs=[pl.BlockSpec((None, tq, QK_PAD), lambda h, i: (h, i, 0)),
                  pl.BlockSpec((tq, 128), lambda h, i: (i, h)),
                  pl.BlockSpec((None, tq, 128), lambda h, i: (h, i, 0)),
                  pl.BlockSpec((tq, 128), lambda h, i: (i, N_HEADS + h)), _gate_block(tq), ANY, ANY],
        out_specs=[pl.BlockSpec((None, tq, QK_PAD), lambda h, i: (h, i, 0)),
                   pl.BlockSpec((tq, 128), lambda h, i: (i, h)), ANY, ANY],
        out_shape=[jax.ShapeDtypeStruct((N_HEADS, t, QK_PAD), F32), jax.ShapeDtypeStruct((t, Z_GATE), MXU_DTYPE),
                   jax.ShapeDtypeStruct((N_HEADS, t, QK_PAD), F32), jax.ShapeDtypeStruct((N_HEADS, t, V_DIM), F32)],
        scratch_shapes=[pltpu.VMEM((t, QK_PAD), MXU_DTYPE), pltpu.VMEM((t, V_DIM), MXU_DTYPE),
                        pltpu.VMEM((t, QK_PAD), F32), pltpu.VMEM((t, V_DIM), F32), pltpu.VMEM((tq, QK_PAD), F32),
                        pltpu.VMEM((tq, 128), F32)] + [pltpu.VMEM((tq, tq), F32)] * 4
        + [pltpu.SemaphoreType.DMA((2,))],
        compiler_params=_cp(2))(qh, o, lse, dycat, z, kh, vh)


def _out_proj_fwd(yab, yc, w, x, g, target, name, tm=512):
    t, d = x.shape
    tm = min(tm, t)
    is_last = target is not None

    def body(*refs):
        if is_last:
            yab_ref, yc_ref, w_ref, x_ref, g_ref, t_ref, y_ref, dout_ref, loss_ref = refs
        else:
            yab_ref, yc_ref, w_ref, x_ref, g_ref, y_ref, out_ref = refs
        y = _dot(jnp.concatenate([yab_ref[...], yc_ref[...]], axis=1), w_ref[...])
        y_ref[...] = y
        out = x_ref[...] + _rms(y, g_ref[...])[0]
        if is_last:
            diff = out - t_ref[...]
            dout_ref[...] = diff * (1.0 / d)
            part = jnp.sum(jnp.sum(diff * diff, axis=-1, keepdims=True), axis=0, keepdims=True) * (0.5 / d)
            _acc(loss_ref, jnp.broadcast_to(part, (1, 128)), pl.program_id(0) == 0)
        else:
            out_ref[...] = out

    row = lambda wd: pl.BlockSpec((tm, wd), lambda i: (i, 0))
    fixed = lambda a, b: pl.BlockSpec((a, b), lambda i: (0, 0))
    in_specs = [row(512), row(512), fixed(d, d), row(d), fixed(1, d)]
    args = [yab, yc, w, x, g]
    out_specs = [row(d), row(d)]
    out_shape = [jax.ShapeDtypeStruct((t, d), F32), jax.ShapeDtypeStruct((t, d), F32)]
    if is_last:
        in_specs.append(row(d))
        args.append(target)
        out_specs.append(fixed(1, 128))
        out_shape.append(jax.ShapeDtypeStruct((1, 128), F32))
    return pl.pallas_call(body, name=name, grid=(t // tm,), in_specs=in_specs, out_specs=out_specs,
                          out_shape=out_shape, compiler_params=_cp())(*args)


def _out_proj_bwd(dout, y, yab, yc, w, g, name, tm=512):
    t, d = y.shape
    tm = min(tm, t)

    def body(dout_ref, y_ref, yab_ref, yc_ref, w_ref, g_ref, dycat_ref, dw_ref, dg_ref):
        first = pl.program_id(0) == 0
        y = y_ref[...]
        r = lax.rsqrt(jnp.mean(y * y, axis=-1, keepdims=True) + EPS)
        dy, dgt = _rms_bwd(y, r, g_ref[...], dout_ref[...])
        _acc(dg_ref, _colsum(dgt), first)
        dyb = _mx(dy)
        _acc(dw_ref, _dot_tn(jnp.concatenate([yab_ref[...], yc_ref[...]], axis=1), dyb), first)
        dycat_ref[...] = _dot_nt(dyb, w_ref[...])

    row = lambda wd: pl.BlockSpec((tm, wd), lambda i: (i, 0))
    fixed = lambda a, b: pl.BlockSpec((a, b), lambda i: (0, 0))
    return pl.pallas_call(
        body, name=name, grid=(t // tm,),
        in_specs=[row(d), row(d), row(512), row(512), fixed(d, d), fixed(1, d)],
        out_specs=[row(d), fixed(d, d), fixed(1, d)],
        out_shape=[jax.ShapeDtypeStruct((t, d), F32), jax.ShapeDtypeStruct((d, d), F32),
                   jax.ShapeDtypeStruct((1, d), F32)],
        compiler_params=_cp())(dout, y, yab, yc, w, g)


def _mesh_pos():
    return lax.axis_index("x"), lax.axis_index("y"), lax.axis_index("c")


def _remote(src, dst, send_sem, recv_sem, to):
    return pltpu.make_async_remote_copy(src_ref=src, dst_ref=dst, send_sem=send_sem, recv_sem=recv_sem,
                                        device_id=to, device_id_type=MESH)


CHUNK_ROWS = 256


def _pieces(rows):
    return [(s, min(CHUNK_ROWS, rows - s)) for s in range(0, rows, CHUNK_ROWS)]


def _piece_table(shapes):
    return [(a, s, sz) for a, shp in enumerate(shapes) for s, sz in _pieces(shp[-2])]


def _gather_weights(shards):
    n = len(shards)
    table = _piece_table([s.shape for s in shards])
    npc = len(table)

    def body(*refs):
        ins, outs = refs[:n], refs[n:2 * n]
        send_sems, recv_sems, fwd_send, fwd_recv = refs[2 * n:]
        x, y, c = _mesh_pos()
        me, sibling = (x, y, c), (x, y, 1 - c)
        chips = [(1 - x, y), (x, 1 - y), (1 - x, 1 - y)]
        slot = lambda cx, cy, layer: 2 * (2 * cx + cy) + layer
        first = []
        for a in range(n):
            for j, (cx, cy) in enumerate(chips):
                first.append(_remote(ins[a].at[c], outs[a].at[slot(x, y, c)], send_sems.at[a, j], recv_sems.at[a, j],
                                     (cx, cy, c)))
                first[-1].start()
        passed = []
        for j, (cx, cy) in enumerate(chips):
            for a in range(n):
                blk = outs[a].at[slot(cx, cy, c)]
                _remote(blk, blk, send_sems.at[a, j], recv_sems.at[a, j], me).wait_recv()
            for q, (a, s, sz) in enumerate(table):
                rows = outs[a].at[slot(cx, cy, c), pl.ds(s, sz)]
                passed.append(_remote(rows, rows, fwd_send.at[j, q], fwd_recv.at[j, q], sibling))
                passed[-1].start()
        for j, (cx, cy) in enumerate(chips):
            for q, (a, s, sz) in enumerate(table):
                rows = outs[a].at[slot(cx, cy, 1 - c), pl.ds(s, sz)]
                _remote(rows, rows, fwd_send.at[j, q], fwd_recv.at[j, q], me).wait_recv()
        for cp in first + passed:
            cp.wait_send()

    return pl.pallas_call(
        body, name="gather_weights", in_specs=[ANY] * n, out_specs=[ANY] * n,
        out_shape=[jax.ShapeDtypeStruct((8,) + s.shape[1:], s.dtype) for s in shards],
        scratch_shapes=[pltpu.SemaphoreType.DMA((n, 3)), pltpu.SemaphoreType.DMA((n, 3)),
                        pltpu.SemaphoreType.DMA((3, npc)), pltpu.SemaphoreType.DMA((3, npc))])(*shards)


def _pair_exchange(parts, common):
    n = len(parts)
    table = _piece_table([p.shape for p in parts] + [common.shape])
    npc = len(table)

    def body(*refs):
        ins, outs = refs[:n + 1], refs[n + 1:2 * n + 2]
        send_sems, recv_sems = refs[2 * n + 2:]
        x, y, c = _mesh_pos()
        sent = []
        for k in range(4):
            for q, (a, s, sz) in enumerate(table):
                if a == n and k > 0:
                    continue
                src = ins[a].at[2 * k + 1 - c, pl.ds(s, sz)] if a < n else ins[a].at[pl.ds(s, sz)]
                dst = outs[a].at[k, pl.ds(s, sz)] if a < n else outs[a].at[pl.ds(s, sz)]
                sent.append(_remote(src, dst, send_sems.at[k, q], recv_sems.at[k, q], (x, y, 1 - c)))
                sent[-1].start()
        for k in range(4):
            for q, (a, s, sz) in enumerate(table):
                if a == n and k > 0:
                    continue
                dst = outs[a].at[k, pl.ds(s, sz)] if a < n else outs[a].at[pl.ds(s, sz)]
                _remote(dst, dst, send_sems.at[k, q], recv_sems.at[k, q], (x, y, c)).wait_recv()
        for cp in sent:
            cp.wait_send()

    return pl.pallas_call(
        body, name="grad_pair_exchange", in_specs=[ANY] * (n + 1), out_specs=[ANY] * (n + 1),
        out_shape=[jax.ShapeDtypeStruct((4,) + p.shape[1:], p.dtype) for p in parts]
        + [jax.ShapeDtypeStruct(common.shape, common.dtype)],
        scratch_shapes=[pltpu.SemaphoreType.DMA((4, npc)), pltpu.SemaphoreType.DMA((4, npc))])(*parts, common)


def _chip_exchange(parts, common):
    n = len(parts)
    table = _piece_table([p.shape for p in parts] + [common.shape])
    npc = len(table)

    def body(*refs):
        ins, outs = refs[:n + 1], refs[n + 1:2 * n + 2]
        send_sems, recv_sems = refs[2 * n + 2:]
        x, y, c = _mesh_pos()
        mine = 2 * x + y
        chips = [(1 - x, y), (x, 1 - y), (1 - x, 1 - y)]
        src = lambda a, k: ins[a].at[k] if a < n else ins[a]
        sent = []
        for j, (cx, cy) in enumerate(chips):
            for q, (a, s, sz) in enumerate(table):
                sent.append(_remote(src(a, 2 * cx + cy).at[pl.ds(s, sz)], outs[a].at[mine, pl.ds(s, sz)],
                                    send_sems.at[j, q], recv_sems.at[j, q], (cx, cy, c)))
                sent[-1].start()
        for j, (cx, cy) in enumerate(chips):
            for q, (a, s, sz) in enumerate(table):
                dst = outs[a].at[2 * cx + cy, pl.ds(s, sz)]
                _remote(dst, dst, send_sems.at[j, q], recv_sems.at[j, q], (x, y, c)).wait_recv()
        for cp in sent:
            cp.wait_send()

    return pl.pallas_call(
        body, name="grad_chip_exchange", in_specs=[ANY] * (n + 1), out_specs=[ANY] * (n + 1),
        out_shape=[jax.ShapeDtypeStruct(p.shape, p.dtype) for p in parts]
        + [jax.ShapeDtypeStruct((4,) + common.shape, common.dtype)],
        scratch_shapes=[pltpu.SemaphoreType.DMA((3, npc)), pltpu.SemaphoreType.DMA((3, npc))])(*parts, common)


def _sibling_exchange(sums):
    n = len(sums)
    table = _piece_table([s.shape for s in sums])
    npc = len(table)

    def body(*refs):
        ins, outs = refs[:n], refs[n:2 * n]
        send_sems, recv_sems = refs[2 * n:]
        x, y, c = _mesh_pos()
        sent = []
        for q, (a, s, sz) in enumerate(table):
            sent.append(_remote(ins[a].at[pl.ds(s, sz)], outs[a].at[pl.ds(s, sz)], send_sems.at[q], recv_sems.at[q],
                                (x, y, 1 - c)))
            sent[-1].start()
        for q, (a, s, sz) in enumerate(table):
            dst = outs[a].at[pl.ds(s, sz)]
            _remote(dst, dst, send_sems.at[q], recv_sems.at[q], (x, y, c)).wait_recv()
        for cp in sent:
            cp.wait_send()

    return pl.pallas_call(
        body, name="sibling_exchange", in_specs=[ANY] * n, out_specs=[ANY] * n,
        out_shape=[jax.ShapeDtypeStruct(s.shape, s.dtype) for s in sums],
        scratch_shapes=[pltpu.SemaphoreType.DMA((npc,)), pltpu.SemaphoreType.DMA((npc,))])(*sums)


def _pair_sum_wire(where, parts, got, name, tr):
    _, r, c = got.shape
    tr = min(tr, r)

    def body(where_ref, p_ref, g_ref, o_ref):
        o_ref[...] = (p_ref[...] + g_ref[...]).astype(WIRE_DTYPE)

    return pl.pallas_call(
        body, name=name,
        grid_spec=pltpu.PrefetchScalarGridSpec(
            num_scalar_prefetch=1, grid=(4, r // tr),
            in_specs=[pl.BlockSpec((None, tr, c), lambda k, i, where_ref: (2 * k + where_ref[0], i, 0)),
                      pl.BlockSpec((None, tr, c), lambda k, i, where_ref: (k, i, 0))],
            out_specs=pl.BlockSpec((None, tr, c), lambda k, i, where_ref: (k, i, 0))),
        out_shape=jax.ShapeDtypeStruct(got.shape, WIRE_DTYPE), compiler_params=_cp(2))(where, parts, got)


def _sum_chips(where, parts, got, recv, name, tr):
    _, r, c = got.shape
    tr = min(tr, r)

    def body(where_ref, p_ref, g_ref, r_ref, o_ref):
        own = p_ref[...] + g_ref[...]
        chip = where_ref[1]
        acc = jnp.where(chip == 0, own, r_ref[0].astype(F32))
        for k in range(1, 4):
            acc = acc + jnp.where(chip == k, own, r_ref[k].astype(F32))
        o_ref[...] = acc

    return pl.pallas_call(
        body, name=name,
        grid_spec=pltpu.PrefetchScalarGridSpec(
            num_scalar_prefetch=1, grid=(r // tr,),
            in_specs=[pl.BlockSpec((None, tr, c), lambda i, where_ref: (2 * where_ref[1] + where_ref[0], i, 0)),
                      pl.BlockSpec((None, tr, c), lambda i, where_ref: (where_ref[1], i, 0)),
                      pl.BlockSpec((4, tr, c), lambda i, where_ref: (0, i, 0))],
            out_specs=pl.BlockSpec((tr, c), lambda i, where_ref: (i, 0))),
        out_shape=jax.ShapeDtypeStruct((r, c), F32), compiler_params=_cp())(where, parts, got, recv)


def _sum_leading(parts, name, tr):
    nlead, r, c = parts.shape
    tr = min(tr, r)

    def body(p_ref, o_ref):
        acc = p_ref[0]
        for j in range(1, nlead):
            acc = acc + p_ref[j]
        o_ref[...] = acc

    return pl.pallas_call(
        body, name=name, grid=(r // tr,),
        in_specs=[pl.BlockSpec((nlead, tr, c), lambda i: (0, i, 0))], out_specs=pl.BlockSpec((tr, c), lambda i: (i, 0)),
        out_shape=jax.ShapeDtypeStruct((r, c), parts.dtype), compiler_params=_cp())(parts)


def _adamw(w, g, m, v, name, tr):
    r, c = w.shape
    tr = min(tr, r)

    def body(w_ref, g_ref, m_ref, v_ref, d_ref, nm_ref, nv_ref):
        gg = g_ref[...]
        nm = ADAM_B1 * m_ref[...] + (1.0 - ADAM_B1) * gg
        nv = ADAM_B2 * v_ref[...] + (1.0 - ADAM_B2) * jnp.square(gg)
        m_hat = nm / (1.0 - ADAM_B1 ** ADAM_STEP)
        v_hat = nv / (1.0 - ADAM_B2 ** ADAM_STEP)
        d_ref[...] = -ADAM_LR * (m_hat / (jnp.sqrt(v_hat) + ADAM_EPS) + ADAM_WD * w_ref[...])
        nm_ref[...] = nm
        nv_ref[...] = nv

    blk = pl.BlockSpec((tr, c), lambda i: (i, 0))
    return pl.pallas_call(
        body, name=name, grid=(r // tr,), in_specs=[blk] * 4, out_specs=[blk] * 3,
        out_shape=[jax.ShapeDtypeStruct((r, c), F32)] * 3, compiler_params=_cp())(w, g, m, v)


def _rope_tables(positions):
    inv_freq = ROPE_BASE ** (-jnp.arange(0, 64, 2, dtype=F32) / 64)
    ang = positions.astype(F32)[:, None] * inv_freq
    cos, sin = jnp.cos(ang), jnp.sin(ang)
    t = positions.shape[0]
    rc = jnp.concatenate([jnp.ones((t, 128), F32), cos, cos, jnp.ones((t, 64), F32)], axis=1)
    rs = jnp.concatenate([jnp.zeros((t, 128), F32), sin, sin, jnp.zeros((t, 64), F32)], axis=1)
    return rc, rs


def _layer_params(l, w_in, w_uq, w_ukv, w_out, small):
    p = {}
    p["w_in"] = jnp.concatenate([w_in[l][:, :1984], jnp.zeros((1024, 64), w_in.dtype), w_in[l][:, 1984:]], axis=1)
    p["w_uq"] = jnp.pad(w_uq[l].reshape(384, 4, 192), ((0, 0), (0, 0), (0, 64))).reshape(384, 1024)
    p["w_ukv"] = w_ukv[l].reshape(256, 4, 2, 128).transpose(0, 2, 1, 3).reshape(256, 1024)
    p["w_out"] = w_out[l]
    p["pre_g"] = small["pre_norm_g"][l][None]
    p["post_g"] = small["post_norm_g"][l][None]
    p["sgu_w"] = small["sgu_w"][l].reshape(512, 128)
    p["sgu_wt"] = small["sgu_w"][l].transpose(0, 2, 1).reshape(512, 128)
    p["sgu_bias"] = jnp.repeat(small["sgu_b"][l].T, 64, axis=1)
    p["ln_g"] = small["sgu_ln_g"][l][None]
    p["ln_b"] = small["sgu_ln_b"][l][None]
    p["pool_wbd"] = _mx(jax.scipy.linalg.block_diag(*[small["pool_w"][l][gi] for gi in range(4)]))
    p["pool_scale"] = small["pool_scale"][l][None]
    p["gq"] = small["q_norm_g"][l][None]
    p["gkv"] = small["kv_norm_g"][l][None]
    return p


def _layer_fwd(l, x, p, rc, rs, target):
    z, h = _in_proj_fwd(x, p["pre_g"], p["w_in"], f"in_proj_fwd_{l}")
    yab = _mix_fwd(z, p["sgu_w"], p["sgu_bias"], p["ln_g"], p["ln_b"], p["pool_wbd"], p["pool_scale"], f"mix_fwd_{l}")
    qh, kh, vh = _qkv_fwd(z, rc, rs, p["w_uq"], p["w_ukv"], p["gq"], p["gkv"], f"qkv_fwd_{l}")
    o, yc, lse = _attn_fwd(qh, kh, vh, z, f"attn_fwd_{l}")
    outs = _out_proj_fwd(yab, yc, p["w_out"], x, p["post_g"], target, f"out_proj_fwd_{l}")
    saved = dict(x=x, z=z, h=h, yab=yab, qh=qh, kh=kh, vh=vh, o=o, yc=yc, lse=lse, y=outs[0])
    return saved, outs[1:]


def _layer_bwd(l, dout, sv, p, rc, rs):
    dycat, dw_out, dpost = _out_proj_bwd(dout, sv["y"], sv["yab"], sv["yc"], p["w_out"], p["post_g"], f"out_proj_bwd_{l}")
    dq, dgate, dk, dv = _attn_bwd(sv["qh"], sv["kh"], sv["vh"], sv["o"], sv["lse"], dycat, sv["z"], f"attn_bwd_{l}")
    dzc, dzk, dwq, dwkv, dgq, dgkv = _qkv_bwd(dq, dk, dv, sv["z"], rc, rs, p["w_uq"], p["w_ukv"], p["gq"], p["gkv"],
                                              f"qkv_bwd_{l}")
    dzm, dsw, dsb, dlng, dlnb, dpw, dps = _mix_bwd(sv["z"], dycat, p["sgu_w"], p["sgu_wt"], p["sgu_bias"], p["ln_g"],
                                                   p["ln_b"], p["pool_wbd"], p["pool_scale"], f"mix_bwd_{l}")
    dx, dw_in, dpre = _in_proj_bwd(dzm, dzc, dzk, dgate, sv["h"], sv["x"], dout, p["w_in"], p["pre_g"], f"in_proj_bwd_{l}")
    grads = {
        "pre_norm_g": dpre[0], "post_norm_g": dpost[0],
        "w_in": jnp.concatenate([dw_in[:, :1984], dw_in[:, 2048:]], axis=1),
        "sgu_w": dsw.reshape(4, 128, 128), "sgu_b": dsb[:, :4].T, "sgu_ln_g": dlng[0], "sgu_ln_b": dlnb[0],
        "pool_w": jnp.stack([dpw[64 * gi:64 * gi + 64, 64 * gi:64 * gi + 64] for gi in range(4)]),
        "pool_scale": dps[0], "q_norm_g": dgq[0],
        "w_uq": dwq.reshape(384, 4, 256)[:, :, :192].reshape(384, 768), "kv_norm_g": dgkv[0],
        "w_ukv": dwkv.reshape(256, 2, 4, 128).transpose(0, 2, 1, 3).reshape(256, 1024), "w_out": dw_out,
    }
    return dx, grads


SMALL_NAMES = ["pre_norm_g", "post_norm_g", "sgu_w", "sgu_b", "sgu_ln_g", "sgu_ln_b", "pool_w", "pool_scale",
               "q_norm_g", "kv_norm_g"]
BIG_NAMES = ["w_in", "w_uq", "w_ukv", "w_out"]
WEIGHT_NAMES = ["pre_norm_g", "post_norm_g", "w_in", "sgu_w", "sgu_b", "sgu_ln_g", "sgu_ln_b", "pool_w", "pool_scale",
                "q_norm_g", "w_uq", "kv_norm_g", "w_ukv", "w_out"]


def _local_step(x, positions, target, w_in, w_uq, w_ukv, w_out, small):
    rc, rs = _rope_tables(positions)
    params = [_layer_params(l, w_in, w_uq, w_ukv, w_out, small) for l in range(DEPTH)]
    saved = []
    for l in range(DEPTH):
        sv, outs = _layer_fwd(l, x, params[l], rc, rs, target if l == DEPTH - 1 else None)
        saved.append(sv)
        if l < DEPTH - 1:
            x = outs[0]
    dout, loss = outs
    grads = [None] * DEPTH
    for l in reversed(range(DEPTH)):
        dout, grads[l] = _layer_bwd(l, dout, saved[l], params[l], rc, rs)
    return loss[0, 0], dout, {k: jnp.stack([grads[l][k] for l in range(DEPTH)]) for k in WEIGHT_NAMES}


def _pack_small(tree, extra=None):
    pieces = [tree[k].reshape(-1) for k in SMALL_NAMES]
    pieces.append(jnp.zeros((1,), F32) if extra is None else extra.reshape(1))
    flat = jnp.concatenate(pieces)
    rows = -(-flat.shape[0] // 1024) * 8
    return jnp.pad(flat, (0, rows * 128 - flat.shape[0])).reshape(rows, 128)


def _unpack_small(packed, like):
    flat = packed.reshape(-1)
    out, off = {}, 0
    for k in SMALL_NAMES:
        size = like[k].size
        out[k] = flat[off:off + size].reshape(like[k].shape)
        off += size
    return out, flat[off]


def kernel(x, positions, pre_norm_g, post_norm_g, w_in, sgu_w, sgu_b, sgu_ln_g, sgu_ln_b, pool_w, pool_scale, q_norm_g, w_uq, kv_norm_g, w_ukv, w_out, loss_target, m_pre_norm_g, m_post_norm_g, m_w_in, m_sgu_w, m_sgu_b, m_sgu_ln_g, m_sgu_ln_b, m_pool_w, m_pool_scale, m_q_norm_g, m_w_uq, m_kv_norm_g, m_w_ukv, m_w_out, v_pre_norm_g, v_post_norm_g, v_w_in, v_sgu_w, v_sgu_b, v_sgu_ln_g, v_sgu_ln_b, v_pool_w, v_pool_scale, v_q_norm_g, v_w_uq, v_kv_norm_g, v_w_ukv, v_w_out):
    w = dict(pre_norm_g=pre_norm_g, post_norm_g=post_norm_g, w_in=w_in, sgu_w=sgu_w, sgu_b=sgu_b, sgu_ln_g=sgu_ln_g,
             sgu_ln_b=sgu_ln_b, pool_w=pool_w, pool_scale=pool_scale, q_norm_g=q_norm_g, w_uq=w_uq, kv_norm_g=kv_norm_g,
             w_ukv=w_ukv, w_out=w_out)
    m = dict(pre_norm_g=m_pre_norm_g, post_norm_g=m_post_norm_g, w_in=m_w_in, sgu_w=m_sgu_w, sgu_b=m_sgu_b,
             sgu_ln_g=m_sgu_ln_g, sgu_ln_b=m_sgu_ln_b, pool_w=m_pool_w, pool_scale=m_pool_scale, q_norm_g=m_q_norm_g,
             w_uq=m_w_uq, kv_norm_g=m_kv_norm_g, w_ukv=m_w_ukv, w_out=m_w_out)
    v = dict(pre_norm_g=v_pre_norm_g, post_norm_g=v_post_norm_g, w_in=v_w_in, sgu_w=v_sgu_w, sgu_b=v_sgu_b,
             sgu_ln_g=v_sgu_ln_g, sgu_ln_b=v_sgu_ln_b, pool_w=v_pool_w, pool_scale=v_pool_scale, q_norm_g=v_q_norm_g,
             w_uq=v_w_uq, kv_norm_g=v_kv_norm_g, w_ukv=v_w_ukv, w_out=v_w_out)

    core = lax.axis_index("c")
    chip = 2 * lax.axis_index("x") + lax.axis_index("y")
    shards = [_mx(w[k]) for k in BIG_NAMES]
    gathered = _gather_weights(shards)
    g_in, g_uq, g_ukv, g_out = [lax.dynamic_update_slice(g, s, (2 * chip, 0, 0)) for g, s in zip(gathered, shards)]
    cols = lambda g: g.reshape((4, 2) + g.shape[1:]).transpose(1, 2, 0, 3).reshape(2, g.shape[1], 4 * g.shape[2])
    full_out = g_out.reshape(4, 2, 256, 1024).transpose(1, 0, 2, 3).reshape(2, 1024, 1024)
    loss, dx, grads = _local_step(x[0], positions[0], loss_target[0], cols(g_in), cols(g_uq), cols(g_ukv), full_out, w)

    split_cols = lambda g: g.reshape(2, g.shape[1], 4, g.shape[2] // 4).transpose(2, 0, 1, 3).reshape(8, g.shape[1], g.shape[2] // 4)
    parts = [split_cols(grads["w_in"]), split_cols(grads["w_uq"]), split_cols(grads["w_ukv"]),
             grads["w_out"].reshape(2, 4, 256, 1024).transpose(1, 0, 2, 3).reshape(8, 256, 1024)]
    common = _pack_small(grads, loss)
    got = _pair_exchange(parts, common)
    where = jnp.stack([core, chip]).astype(jnp.int32)
    wire_parts = [_pair_sum_wire(where, parts[a], got[a], f"pair_sum_{BIG_NAMES[a]}", 128) for a in range(4)]
    chip_common = _sum_leading(jnp.stack([common, got[4]]), "pair_sum_small", common.shape[0])
    received = _chip_exchange(wire_parts, chip_common)
    sums = [_sum_chips(where, parts[a], got[a], received[a], f"sum_{BIG_NAMES[a]}", 128) for a in range(4)]
    all_common = lax.dynamic_update_slice(received[4], chip_common[None], (chip, 0, 0))
    small_sum, loss = _unpack_small(_sum_leading(all_common, "sum_small", all_common.shape[1]), w)
    others = _sibling_exchange(sums)
    total = dict(small_sum)
    for a, k in enumerate(BIG_NAMES):
        total[k] = jnp.where(core == 0, jnp.stack([sums[a], others[a]]), jnp.stack([others[a], sums[a]]))

    packed = _adamw(_pack_small(w), _pack_small(total), _pack_small(m), _pack_small(v), "adamw_small", 2048)
    small_out = [_unpack_small(pk, w)[0] for pk in packed]
    delta, new_m, new_v = {}, {}, {}
    for k in SMALL_NAMES:
        delta[k], new_m[k], new_v[k] = (so[k] for so in small_out)
    for k in BIG_NAMES:
        shape = w[k].shape
        flat = lambda a: a.reshape(shape[0] * shape[1], shape[2])
        res = _adamw(flat(w[k]), flat(total[k]), flat(m[k]), flat(v[k]), f"adamw_{k}", 256)
        delta[k], new_m[k], new_v[k] = (r.reshape(shape) for r in res)

    return (loss, dx[None], *[total[k] for k in WEIGHT_NAMES], *[delta[k] for k in WEIGHT_NAMES],
            *[new_m[k] for k in WEIGHT_NAMES], *[new_v[k] for k in WEIGHT_NAMES])
```

```python
import jax
import jax.numpy as jnp
from jax import lax
from jax.experimental import pallas as pl
from jax.experimental.pallas import tpu as pltpu

F32 = jnp.float32
MXU_DTYPE = jnp.bfloat16
WIRE_DTYPE = jnp.bfloat16
EPS = 1e-6
NEG_INF = -1e30
DEPTH = 2
N_HEADS = 4
QK_PAD = 256
V_DIM = 128
SCALE = 192 ** -0.5
LOG2E = 1.4426950408889634
ROPE_BASE = 10000.0
ADAM_LR, ADAM_B1, ADAM_B2, ADAM_EPS, ADAM_WD, ADAM_STEP = 0.001, 0.9, 0.999, 1e-08, 0.01, 10
VMEM_LIMIT_BYTES = 56 * 1024 * 1024
MESH = pl.DeviceIdType.MESH
ANY = pl.BlockSpec(memory_space=pl.ANY)

Z_MIX, Z_C, Z_KR, Z_GATE = 1280, 640, 128, 512
Z_W = Z_MIX + Z_C + Z_KR + Z_GATE


def _cp(n_axes=1):
    return pltpu.CompilerParams(dimension_semantics=("arbitrary",) * n_axes, vmem_limit_bytes=VMEM_LIMIT_BYTES)


def _dot(a, b):
    return lax.dot_general(a, b, (((1,), (0,)), ((), ())), preferred_element_type=F32)


def _dot_nt(a, b):
    return lax.dot_general(a, b, (((1,), (1,)), ((), ())), preferred_element_type=F32)


def _dot_tn(a, b):
    return lax.dot_general(a, b, (((0,), (0,)), ((), ())), preferred_element_type=F32)


def _mx(a):
    return a.astype(MXU_DTYPE)


def _silu_and_grad(g):
    sg = jax.nn.sigmoid(g)
    return g * sg, sg * (1.0 + g * (1.0 - sg))


def _rms(x, g):
    r = lax.rsqrt(jnp.mean(x * x, axis=-1, keepdims=True) + EPS)
    return x * r * g, r


def _rms_bwd(x, r, g, dy):
    xhat = x * r
    dyg = dy * g
    dx = r * (dyg - xhat * jnp.mean(dyg * xhat, axis=-1, keepdims=True))
    return dx, dy * xhat


def _acc(ref, val, first):
    @pl.when(first)
    def _():
        ref[...] = val

    @pl.when(jnp.logical_not(first))
    def _():
        ref[...] += val


def _colsum(a):
    return jnp.sum(a, axis=0, keepdims=True)


def _in_proj_fwd(x, g, w, name, tm=512):
    t, d = x.shape
    n = w.shape[1]
    tm = min(tm, t)

    def body(x_ref, g_ref, w_ref, z_ref, h_ref):
        h, _ = _rms(x_ref[...], g_ref[...])
        h = _mx(h)
        h_ref[...] = h
        z_ref[...] = _dot(h, w_ref[...])

    return pl.pallas_call(
        body, name=name, grid=(t // tm,),
        in_specs=[pl.BlockSpec((tm, d), lambda i: (i, 0)), pl.BlockSpec((1, d), lambda i: (0, 0)),
                  pl.BlockSpec((d, n), lambda i: (0, 0))],
        out_specs=[pl.BlockSpec((tm, n), lambda i: (i, 0)), pl.BlockSpec((tm, d), lambda i: (i, 0))],
        out_shape=[jax.ShapeDtypeStruct((t, n), F32), jax.ShapeDtypeStruct((t, d), MXU_DTYPE)],
        compiler_params=_cp())(x, g, w)


def _in_proj_bwd(dz_mix, dz_c, dz_kr, dz_gate, h, x, d_res, w, g, name, tm=256):
    t, d = x.shape
    n = w.shape[1]
    tm = min(tm, t)

    def body(dm_ref, dc_ref, dk_ref, dg_ref, h_ref, x_ref, dres_ref, w_ref, g_ref, dx_ref, dw_ref, dgn_ref):
        first = pl.program_id(0) == 0
        dz = jnp.concatenate([dm_ref[...], dc_ref[...], dk_ref[...], dg_ref[...]], axis=1)

        @pl.when(first)
        def _():
            dw_ref[...] = jnp.zeros(dw_ref.shape, F32)

        hb = h_ref[...]
        for c0 in range(0, n, 512):
            dw_ref[:, c0:c0 + 512] += _dot_tn(hb, dz[:, c0:c0 + 512])
        dh = _dot_nt(dz, w_ref[...])
        xf = x_ref[...]
        r = lax.rsqrt(jnp.mean(xf * xf, axis=-1, keepdims=True) + EPS)
        dx, dgt = _rms_bwd(xf, r, g_ref[...], dh)
        dx_ref[...] = dx + dres_ref[...]
        _acc(dgn_ref, _colsum(dgt), first)

    row = lambda wd: pl.BlockSpec((tm, wd), lambda i: (i, 0))
    fixed = lambda a, b: pl.BlockSpec((a, b), lambda i: (0, 0))
    return pl.pallas_call(
        body, name=name, grid=(t // tm,),
        in_specs=[row(Z_MIX), row(Z_C), row(Z_KR), row(Z_GATE), row(d), row(d), row(d), fixed(d, n), fixed(1, d)],
        out_specs=[row(d), fixed(d, n), fixed(1, d)],
        out_shape=[jax.ShapeDtypeStruct((t, d), F32), jax.ShapeDtypeStruct((d, n), F32),
                   jax.ShapeDtypeStruct((1, d), F32)],
        compiler_params=_cp())(dz_mix, dz_c, dz_kr, dz_gate, h, x, d_res, w, g)


def _lane_group(shape):
    return lax.broadcasted_iota(jnp.int32, shape, 1) // 64


def _select_group(vals):
    grp = _lane_group(vals[0].shape)
    out = vals[3]
    for gi in (2, 1, 0):
        out = jnp.where(grp == gi, vals[gi], out)
    return out


def _sgu_mask(transposed):
    r = (lax.broadcasted_iota(jnp.int32, (512, 128), 0) % 128) // 64
    c = lax.broadcasted_iota(jnp.int32, (512, 128), 1) // 64
    return (r <= c) if transposed else (c <= r)


def _sgu_apply(wstack, vb, nblk):
    outs = []
    for n in range(nblk):
        r = _dot(wstack, vb[n * 128:(n + 1) * 128, :])
        outs.append(_select_group([r[hh * 128:(hh + 1) * 128, :] for hh in range(4)]))
    return jnp.concatenate(outs, axis=0)


def _layer_norm(v, g, b):
    mu = jnp.mean(v, axis=-1, keepdims=True)
    vc = v - mu
    rstd = lax.rsqrt(jnp.mean(vc * vc, axis=-1, keepdims=True) + EPS)
    vhat = vc * rstd
    return vhat * g + b, vhat, rstd


def _pool_counts(t0, n):
    t = t0 + lax.broadcasted_iota(jnp.int32, (n, 256), 0)
    w = _select_group([jnp.full((n, 256), wv, jnp.int32) for wv in (2, 4, 8, 16)])
    return jnp.minimum(t + 1, w).astype(F32)


def _pooled(p, halo, t0):
    tm = p.shape[0]
    ext = jnp.concatenate([halo, p], axis=0)
    s2 = ext + pltpu.roll(ext, 1, 0)
    s4 = s2 + pltpu.roll(s2, 2, 0)
    s8 = s4 + pltpu.roll(s4, 4, 0)
    s16 = s8 + pltpu.roll(s8, 8, 0)
    sel = _select_group([s2, s4, s8, s16])[16:, :]
    return sel / _pool_counts(t0, tm) - p


def _pooled_bwd(dpool, dpool_halo, t0):
    tm = dpool.shape[0]
    n = tm + 16
    ext = jnp.concatenate([dpool, dpool_halo], axis=0) / _pool_counts(t0, n)
    f2 = ext + pltpu.roll(ext, n - 1, 0)
    f4 = f2 + pltpu.roll(f2, n - 2, 0)
    f8 = f4 + pltpu.roll(f4, n - 4, 0)
    f16 = f8 + pltpu.roll(f8, n - 8, 0)
    return _select_group([f2, f4, f8, f16])[:tm, :] - dpool


def _mix_specs(t, tm):
    nt16 = t // 16
    zrow = pl.BlockSpec((tm, Z_MIX), lambda i: (i, 0))
    prev_halo = pl.BlockSpec((16, 256), lambda i: (jnp.maximum(i * (tm // 16) - 1, 0), 3))
    fixed = lambda a, b: pl.BlockSpec((a, b), lambda i: (0, 0))
    params = [fixed(512, 128), fixed(128, 256), fixed(1, 256), fixed(1, 256), fixed(256, 256), fixed(1, 256)]
    return nt16, zrow, prev_halo, fixed, params


def _mix_fwd(z, sgu_w, sgu_bias, ln_g, ln_b, pool_wbd, pool_scale, name, tm=256):
    t = z.shape[0]
    tm = min(tm, t)
    _, zrow, prev_halo, _, params = _mix_specs(t, tm)

    def body(z_ref, halo_ref, w_ref, bias_ref, lng_ref, lnb_ref, pw_ref, ps_ref, y_ref):
        i = pl.program_id(0)
        u, v, gate = z_ref[:, 0:256], z_ref[:, 256:512], z_ref[:, 512:768]
        p, pgate = z_ref[:, 768:1024], z_ref[:, 1024:1280]
        vn, _, _ = _layer_norm(v, lng_ref[...], lnb_ref[...])
        wm = _mx(jnp.where(_sgu_mask(False), w_ref[...], 0.0))
        mixed = _sgu_apply(wm, _mx(vn), tm // 128) + jnp.tile(bias_ref[...], (tm // 128, 1))
        ya = u * mixed * _silu_and_grad(gate)[0]
        halo = jnp.where(i > 0, halo_ref[...], 0.0)
        pooled = _pooled(p, halo, i * tm)
        yb = _dot(_mx(pooled), pw_ref[...]) * ps_ref[...] * _silu_and_grad(pgate)[0]
        y_ref[...] = _mx(jnp.concatenate([ya, yb], axis=1))

    return pl.pallas_call(
        body, name=name, grid=(t // tm,),
        in_specs=[zrow, prev_halo] + params,
        out_specs=pl.BlockSpec((tm, 512), lambda i: (i, 0)),
        out_shape=jax.ShapeDtypeStruct((t, 512), MXU_DTYPE),
        compiler_params=_cp())(z, z, sgu_w, sgu_bias, ln_g, ln_b, pool_wbd, pool_scale)


def _mix_bwd(z, dycat, sgu_w, sgu_wt, sgu_bias, ln_g, ln_b, pool_wbd, pool_scale, name, tm=256):
    t = z.shape[0]
    tm = min(tm, t)
    nt16, zrow, prev_halo, fixed, params = _mix_specs(t, tm)
    nblk = tm // 128
    last = t // tm - 1

    def body(z_ref, halo_ref, zn_ref, dy_ref, dyn_ref, w_ref, wt_ref, bias_ref, lng_ref, lnb_ref, pw_ref, ps_ref,
             dz_ref, dw_ref, db_ref, dlng_ref, dlnb_ref, dpw_ref, dps_ref):
        i = pl.program_id(0)
        first = i == 0
        u, v, gate = z_ref[:, 0:256], z_ref[:, 256:512], z_ref[:, 512:768]
        p, pgate = z_ref[:, 768:1024], z_ref[:, 1024:1280]
        dya, dyb = dy_ref[:, 0:256], dy_ref[:, 256:512]
        vn, vhat, rstd = _layer_norm(v, lng_ref[...], lnb_ref[...])
        vnb = _mx(vn)
        wm = _mx(jnp.where(_sgu_mask(False), w_ref[...], 0.0))
        wmt = _mx(jnp.where(_sgu_mask(True), wt_ref[...], 0.0))
        mixed = _sgu_apply(wm, vnb, nblk) + jnp.tile(bias_ref[...], (nblk, 1))
        silu, dsilu = _silu_and_grad(gate)
        t1 = u * mixed
        d_gate = dya * t1 * dsilu
        d_t1 = dya * silu
        d_u = d_t1 * mixed
        d_mixed = d_t1 * u
        dmb = _mx(d_mixed)
        d_vn = _sgu_apply(wmt, dmb, nblk)
        grp = _lane_group((128, 256))
        lane = lax.broadcasted_iota(jnp.int32, (128, 128), 1)
        dws = [jnp.zeros((128, 128), F32) for _ in range(4)]
        dbias = jnp.zeros((128, 128), F32)
        for n in range(nblk):
            dm_n, dmb_n, vnb_n = d_mixed[n * 128:(n + 1) * 128], dmb[n * 128:(n + 1) * 128], vnb[n * 128:(n + 1) * 128]
            for hh in range(4):
                dws[hh] = dws[hh] + _dot_nt(jnp.where(grp == hh, dmb_n, jnp.zeros_like(dmb_n)), vnb_n)
                rs = jnp.sum(jnp.where(grp == hh, dm_n, 0.0), axis=-1, keepdims=True)
                dbias = dbias + jnp.where(lane == hh, rs, 0.0)
        _acc(dw_ref, jnp.concatenate(dws, axis=0), first)
        _acc(db_ref, dbias, first)
        _acc(dlng_ref, _colsum(d_vn * vhat), first)
        _acc(dlnb_ref, _colsum(d_vn), first)
        dvh = d_vn * lng_ref[...]
        d_v = rstd * (dvh - jnp.mean(dvh, axis=-1, keepdims=True) - vhat * jnp.mean(dvh * vhat, axis=-1, keepdims=True))

        @pl.when(i == last)
        def _():
            dw_ref[...] = jnp.where(_sgu_mask(False), dw_ref[...], 0.0)

        halo = jnp.where(i > 0, halo_ref[...], 0.0)
        pooled = _pooled(p, halo, i * tm)
        pooled_b = _mx(pooled)
        mixedp = _dot(pooled_b, pw_ref[...])
        psilu, pdsilu = _silu_and_grad(pgate)
        d_pgate = dyb * (mixedp * ps_ref[...]) * pdsilu
        d_ms = dyb * psilu
        _acc(dps_ref, _colsum(d_ms * mixedp), first)
        dmpb = _mx(d_ms * ps_ref[...])
        _acc(dpw_ref, _dot_tn(pooled_b, dmpb), first)
        d_pooled = _dot_nt(dmpb, pw_ref[...])
        dmp_halo = _mx(dyn_ref[...] * _silu_and_grad(zn_ref[...])[0] * ps_ref[...])
        d_pooled_halo = jnp.where(i < last, _dot_nt(dmp_halo, pw_ref[...]), 0.0)
        d_p = _pooled_bwd(d_pooled, d_pooled_halo, i * tm)
        dz_ref[...] = _mx(jnp.concatenate([d_u, d_v, d_gate, d_p, d_pgate], axis=1))

    nxt = lambda i: jnp.minimum((i + 1) * (tm // 16), nt16 - 1)
    return pl.pallas_call(
        body, name=name, grid=(t // tm,),
        in_specs=[zrow, prev_halo, pl.BlockSpec((16, 256), lambda i: (nxt(i), 4)),
                  pl.BlockSpec((tm, 512), lambda i: (i, 0)), pl.BlockSpec((16, 256), lambda i: (nxt(i), 1)),
                  params[0], fixed(512, 128)] + params[1:],
        out_specs=[pl.BlockSpec((tm, Z_MIX), lambda i: (i, 0)), fixed(512, 128), fixed(128, 128), fixed(1, 256),
                   fixed(1, 256), fixed(256, 256), fixed(1, 256)],
        out_shape=[jax.ShapeDtypeStruct((t, Z_MIX), MXU_DTYPE), jax.ShapeDtypeStruct((512, 128), F32),
                   jax.ShapeDtypeStruct((128, 128), F32), jax.ShapeDtypeStruct((1, 256), F32),
                   jax.ShapeDtypeStruct((1, 256), F32), jax.ShapeDtypeStruct((256, 256), F32),
                   jax.ShapeDtypeStruct((1, 256), F32)],
        compiler_params=_cp())(z, z, z, dycat, dycat, sgu_w, sgu_wt, sgu_bias, ln_g, ln_b, pool_wbd, pool_scale)


def _rot_half(x, transpose):
    w = x.shape[1]
    lane = lax.broadcasted_iota(jnp.int32, x.shape, 1) % min(w, 256)
    base = 128 if w >= 256 else 0
    lo = jnp.logical_and(lane >= base, lane < base + 32)
    hi = jnp.logical_and(lane >= base + 32, lane < base + 64)
    up = pltpu.roll(x, w - 32, 1)
    down = pltpu.roll(x, 32, 1)
    if transpose:
        return jnp.where(lo, up, jnp.where(hi, -down, 0.0))
    return jnp.where(lo, -up, jnp.where(hi, down, 0.0))


def _rope(x, c, s):
    return x * c + _rot_half(x, False) * s


def _rope_bwd(dy, c, s):
    return dy * c + _rot_half(dy * s, True)


def _qkv_fwd(z, rc, rs, w_uq, w_ukv, gq, gkv, name, tm=256):
    t = z.shape[0]
    tm = min(tm, t)

    def body(zc_ref, zk_ref, rc_ref, rs_ref, wq_ref, wkv_ref, gq_ref, gkv_ref, q_ref, k_ref, v_ref):
        cq, ckv = zc_ref[:, 0:384], zc_ref[:, 384:640]
        c, s = rc_ref[...], rs_ref[...]
        qn, _ = _rms(cq, gq_ref[...])
        q = _rope(_dot(_mx(qn), wq_ref[...]), jnp.tile(c, (1, N_HEADS)), jnp.tile(s, (1, N_HEADS)))
        kvn, _ = _rms(ckv, gkv_ref[...])
        kv = _dot(_mx(kvn), wkv_ref[...])
        kpe = _rope(zk_ref[...], c[:, 128:256], s[:, 128:256])
        for hh in range(N_HEADS):
            q_ref[hh] = _mx(q[:, hh * QK_PAD:(hh + 1) * QK_PAD])
            k_ref[hh] = _mx(jnp.concatenate([kv[:, hh * 128:(hh + 1) * 128], kpe], axis=1))
            v_ref[hh] = _mx(kv[:, 512 + hh * 128:512 + (hh + 1) * 128])

    fixed = lambda a, b: pl.BlockSpec((a, b), lambda i: (0, 0))
    heads = lambda wd: pl.BlockSpec((N_HEADS, tm, wd), lambda i: (0, i, 0))
    return pl.pallas_call(
        body, name=name, grid=(t // tm,),
        in_specs=[pl.BlockSpec((tm, Z_C), lambda i: (i, Z_MIX // Z_C)),
                  pl.BlockSpec((tm, Z_KR), lambda i: (i, (Z_MIX + Z_C) // Z_KR)),
                  pl.BlockSpec((tm, 256), lambda i: (i, 0)), pl.BlockSpec((tm, 256), lambda i: (i, 0)),
                  fixed(384, 1024), fixed(256, 1024), fixed(1, 384), fixed(1, 256)],
        out_specs=[heads(QK_PAD), heads(QK_PAD), heads(V_DIM)],
        out_shape=[jax.ShapeDtypeStruct((N_HEADS, t, QK_PAD), MXU_DTYPE),
                   jax.ShapeDtypeStruct((N_HEADS, t, QK_PAD), MXU_DTYPE),
                   jax.ShapeDtypeStruct((N_HEADS, t, V_DIM), MXU_DTYPE)],
        compiler_params=_cp())(z, z, rc, rs, w_uq, w_ukv, gq, gkv)


def _qkv_bwd(dq, dk, dv, z, rc, rs, w_uq, w_ukv, gq, gkv, name, tm=256):
    t = z.shape[0]
    tm = min(tm, t)

    def body(dq_ref, dk_ref, dv_ref, zc_ref, rc_ref, rs_ref, wq_ref, wkv_ref, gq_ref, gkv_ref,
             dzc_ref, dzk_ref, dwq_ref, dwkv_ref, dgq_ref, dgkv_ref):
        first = pl.program_id(0) == 0
        cq, ckv = zc_ref[:, 0:384], zc_ref[:, 384:640]
        c, s = rc_ref[...], rs_ref[...]
        dq_all = jnp.concatenate([dq_ref[hh] for hh in range(N_HEADS)], axis=1)
        dqp = _mx(_rope_bwd(dq_all, jnp.tile(c, (1, N_HEADS)), jnp.tile(s, (1, N_HEADS))))
        qn, rq = _rms(cq, gq_ref[...])
        _acc(dwq_ref, _dot_tn(_mx(qn), dqp), first)
        d_cq, dgq_t = _rms_bwd(cq, rq, gq_ref[...], _dot_nt(dqp, wq_ref[...]))
        _acc(dgq_ref, _colsum(dgq_t), first)
        dkpe = dk_ref[0][:, 128:256]
        for hh in range(1, N_HEADS):
            dkpe = dkpe + dk_ref[hh][:, 128:256]
        dzk_ref[...] = _mx(_rope_bwd(dkpe, c[:, 128:256], s[:, 128:256]))
        dkv = _mx(jnp.concatenate([dk_ref[hh][:, 0:128] for hh in range(N_HEADS)]
                                  + [dv_ref[hh] for hh in range(N_HEADS)], axis=1))
        kvn, rkv = _rms(ckv, gkv_ref[...])
        _acc(dwkv_ref, _dot_tn(_mx(kvn), dkv), first)
        d_ckv, dgkv_t = _rms_bwd(ckv, rkv, gkv_ref[...], _dot_nt(dkv, wkv_ref[...]))
        _acc(dgkv_ref, _colsum(dgkv_t), first)
        dzc_ref[...] = _mx(jnp.concatenate([d_cq, d_ckv], axis=1))

    fixed = lambda a, b: pl.BlockSpec((a, b), lambda i: (0, 0))
    heads = lambda wd: pl.BlockSpec((N_HEADS, tm, wd), lambda i: (0, i, 0))
    return pl.pallas_call(
        body, name=name, grid=(t // tm,),
        in_specs=[heads(QK_PAD), heads(QK_PAD), heads(V_DIM), pl.BlockSpec((tm, Z_C), lambda i: (i, Z_MIX // Z_C)),
                  pl.BlockSpec((tm, 256), lambda i: (i, 0)), pl.BlockSpec((tm, 256), lambda i: (i, 0)),
                  fixed(384, 1024), fixed(256, 1024), fixed(1, 384), fixed(1, 256)],
        out_specs=[pl.BlockSpec((tm, Z_C), lambda i: (i, 0)), pl.BlockSpec((tm, Z_KR), lambda i: (i, 0)),
                   fixed(384, 1024), fixed(256, 1024), fixed(1, 384), fixed(1, 256)],
        out_shape=[jax.ShapeDtypeStruct((t, Z_C), MXU_DTYPE), jax.ShapeDtypeStruct((t, Z_KR), MXU_DTYPE),
                   jax.ShapeDtypeStruct((384, 1024), F32), jax.ShapeDtypeStruct((256, 1024), F32),
                   jax.ShapeDtypeStruct((1, 384), F32), jax.ShapeDtypeStruct((1, 256), F32)],
        compiler_params=_cp())(dq, dk, dv, z, rc, rs, w_uq, w_ukv, gq, gkv)


def _loop_in_long_trips(n, body):
    def two(t, carry):
        return body(2 * t + 1, body(2 * t, carry))

    def four(t, carry):
        return two(2 * t + 1, two(2 * t, carry))

    lax.fori_loop(0, n // 4, four, 0)
    lax.fori_loop(2 * (n // 4), n // 2, two, 0)
    lax.fori_loop(2 * (n // 2), n, body, 0)


def _init_mask_bias(bias_ref):
    tq = bias_ref.shape[1]
    r = lax.broadcasted_iota(jnp.int32, (tq, tq), 0) // 64
    c = lax.broadcasted_iota(jnp.int32, (tq, tq), 1) // 64
    bias_ref[0] = jnp.zeros((tq, tq), F32)
    bias_ref[1] = jnp.where(c <= r, 0.0, NEG_INF)


def _gate_block(tq):
    return pl.BlockSpec((tq, 128), lambda h, i: (i, (Z_MIX + Z_C + Z_KR) // 128 + h))


def _attn_fwd(qh, kh, vh, z, name, tq=512):
    t = qh.shape[1]
    tq = min(tq, t)

    def body(q_ref, g_ref, k_hbm, v_hbm, o_ref, yc_ref, lse_ref, k_v, v_v, m_s, acc_s, s_a, s_b, mx_a, mx_b, bias_s,
             sem):
        h, i = pl.program_id(0), pl.program_id(1)

        @pl.when(i == 0)
        def _():
            ck = pltpu.make_async_copy(k_hbm.at[h], k_v, sem.at[0])
            cv = pltpu.make_async_copy(v_hbm.at[h], v_v.at[:, 0:V_DIM], sem.at[1])
            ck.start()
            cv.start()
            v_v[:, V_DIM:2 * V_DIM] = jnp.ones((t, V_DIM), MXU_DTYPE)
            _init_mask_bias(bias_s)
            ck.wait()
            cv.wait()

        q = q_ref[...]
        m_s[...] = jnp.full(m_s.shape, NEG_INF, F32)
        acc_s[...] = jnp.zeros(acc_s.shape, F32)

        def keys(j):
            return pl.ds(pl.multiple_of(j * tq, tq), tq)

        def scores(s_ref, mx_ref, j):
            s = _dot_nt(q, k_v[keys(j), :]) * (SCALE * LOG2E) + bias_s[(j == i).astype(jnp.int32)]
            s_ref[...] = s
            mx_ref[...] = jnp.broadcast_to(jnp.max(s, axis=-1, keepdims=True), mx_ref.shape)

        def softmax_pv(s_ref, mx_ref, j):
            m_old = m_s[...]
            m_new = jnp.maximum(m_old, mx_ref[...])
            p = jnp.exp2(s_ref[...] - jnp.tile(m_new, (1, tq // 128)))
            alpha = jnp.exp2(m_old - m_new)
            m_s[...] = m_new
            acc_s[...] = jnp.tile(alpha, (1, 2)) * acc_s[...] + _dot(_mx(p), v_v[keys(j), :])

        scores(s_a, mx_a, 0)

        def pair(pp, carry):
            scores(s_b, mx_b, 2 * pp + 1)
            softmax_pv(s_a, mx_a, 2 * pp)
            scores(s_a, mx_a, jnp.minimum(2 * pp + 2, i))
            softmax_pv(s_b, mx_b, 2 * pp + 1)
            return carry

        _loop_in_long_trips((i + 1) // 2, pair)

        @pl.when(i % 2 == 0)
        def _():
            softmax_pv(s_a, mx_a, i)

        l = acc_s[:, V_DIM:2 * V_DIM]
        o = acc_s[:, 0:V_DIM] / l
        o_ref[...] = o
        yc_ref[...] = _mx(o * _silu_and_grad(g_ref[...])[0])
        lse_ref[...] = m_s[...] + jnp.log2(l)

    return pl.pallas_call(
        body, name=name, grid=(N_HEADS, t // tq),
        in_specs=[pl.BlockSpec((None, tq, QK_PAD), lambda h, i: (h, i, 0)), _gate_block(tq), ANY, ANY],
        out_specs=[pl.BlockSpec((tq, 128), lambda h, i: (i, h)), pl.BlockSpec((tq, 128), lambda h, i: (i, h)),
                   pl.BlockSpec((None, tq, 128), lambda h, i: (h, i, 0))],
        out_shape=[jax.ShapeDtypeStruct((t, N_HEADS * V_DIM), F32), jax.ShapeDtypeStruct((t, N_HEADS * V_DIM), MXU_DTYPE),
                   jax.ShapeDtypeStruct((N_HEADS, t, 128), F32)],
        scratch_shapes=[pltpu.VMEM((t, QK_PAD), MXU_DTYPE), pltpu.VMEM((t, 2 * V_DIM), MXU_DTYPE),
                        pltpu.VMEM((tq, 128), F32), pltpu.VMEM((tq, 2 * V_DIM), F32),
                        pltpu.VMEM((tq, tq), F32), pltpu.VMEM((tq, tq), F32), pltpu.VMEM((tq, 128), F32),
                        pltpu.VMEM((tq, 128), F32), pltpu.VMEM((2, tq, tq), F32), pltpu.SemaphoreType.DMA((2,))],
        compiler_params=_cp(2))(qh, z, kh, vh)


def _attn_bwd(qh, kh, vh, o, lse, dycat, z, name, tq=512):
    t = qh.shape[1]
    tq = min(tq, t)
    nq = t // tq

    def body(q_ref, o_ref, lse_ref, dy_ref, g_ref, k_hbm, v_hbm, dq_ref, dgate_ref, dk_hbm, dv_hbm,
             k_v, v_v, dk_acc, dv_acc, dq_acc, delta_s, s_a, dp_a, s_b, dp_b, bias_s, sem):
        h, i = pl.program_id(0), pl.program_id(1)

        @pl.when(i == 0)
        def _():
            ck = pltpu.make_async_copy(k_hbm.at[h], k_v, sem.at[0])
            cv = pltpu.make_async_copy(v_hbm.at[h], v_v, sem.at[1])
            ck.start()
            cv.start()
            _init_mask_bias(bias_s)
            dk_acc[...] = jnp.zeros(dk_acc.shape, F32)
            dv_acc[...] = jnp.zeros(dv_acc.shape, F32)
            ck.wait()
            cv.wait()

        gate, dy, of = g_ref[...], dy_ref[...], o_ref[...]
        silu, dsilu = _silu_and_grad(gate)
        do = dy * silu
        delta = jnp.sum(do * of, axis=-1, keepdims=True)
        dgate_ref[...] = _mx(dy * of * dsilu)
        dob = _mx(do)
        q = q_ref[...]
        delta_s[...] = jnp.broadcast_to(delta, delta_s.shape)
        dq_acc[...] = jnp.zeros(dq_acc.shape, F32)

        def keys(j):
            return pl.ds(pl.multiple_of(j * tq, tq), tq)

        def scores(s_ref, dp_ref, j):
            s = _dot_nt(q, k_v[keys(j), :]) * (SCALE * LOG2E) + bias_s[(j == i).astype(jnp.int32)]
            s_ref[...] = s - jnp.tile(lse_ref[...], (1, tq // 128))
            dp_ref[...] = _dot_nt(dob, v_v[keys(j), :]) - jnp.tile(delta_s[...], (1, tq // 128))

        def grads(s_ref, dp_ref, j):
            ks = keys(j)
            p = jnp.exp2(s_ref[...])
            ds = p * dp_ref[...] * SCALE
            pb, dsb = _mx(p), _mx(ds)
            dq_acc[...] += _dot(dsb, k_v[ks, :])
            dk_acc[ks, :] += _dot_tn(dsb, q)
            dv_acc[ks, :] += _dot_tn(pb, dob)

        scores(s_a, dp_a, 0)

        def pair(pp, carry):
            scores(s_b, dp_b, 2 * pp + 1)
            grads(s_a, dp_a, 2 * pp)
            scores(s_a, dp_a, jnp.minimum(2 * pp + 2, i))
            grads(s_b, dp_b, 2 * pp + 1)
            return carry

        _loop_in_long_trips((i + 1) // 2, pair)

        @pl.when(i % 2 == 0)
        def _():
            grads(s_a, dp_a, i)

        dq_ref[...] = dq_acc[...]

        @pl.when(i == nq - 1)
        def _():
            ck = pltpu.make_async_copy(dk_acc, dk_hbm.at[h], sem.at[0])
            cv = pltpu.make_async_copy(dv_acc, dv_hbm.at[h], sem.at[1])
            ck.start()
            cv.start()
            ck.wait()
            cv.wait()

    return pl.pallas_call(
        body, name=name, grid=(N_HEADS, nq),
        in_specs=[pl.BlockSpec((None, tq, QK_PAD), lambda h, i: (h, i, 0)),
                  pl.BlockSpec((tq, 128), lambda h, i: (i, h)),
                  pl.BlockSpec((None, tq, 128), lambda h, i: (h, i, 0)),
                  pl.BlockSpec((tq, 128), lambda h, i: (i, N_HEADS + h)), _gate_block(tq), ANY, ANY],
        out_specs=[pl.BlockSpec((None, tq, QK_PAD), lambda h, i: (h, i, 0)),
                   pl.BlockSpec((tq, 128), lambda h, i: (i, h)), ANY, ANY],
        out_shape=[jax.ShapeDtypeStruct((N_HEADS, t, QK_PAD), F32), jax.ShapeDtypeStruct((t, Z_GATE), MXU_DTYPE),
                   jax.ShapeDtypeStruct((N_HEADS, t, QK_PAD), F32), jax.ShapeDtypeStruct((N_HEADS, t, V_DIM), F32)],
        scratch_shapes=[pltpu.VMEM((t, QK_PAD), MXU_DTYPE), pltpu.VMEM((t, V_DIM), MXU_DTYPE),
                        pltpu.VMEM((t, QK_PAD), F32), pltpu.VMEM((t, V_DIM), F32), pltpu.VMEM((tq, QK_PAD), F32),
                        pltpu.VMEM((tq, 128), F32)] + [pltpu.VMEM((tq, tq), F32)] * 4
        + [pltpu.VMEM((2, tq, tq), F32), pltpu.SemaphoreType.DMA((2,))],
        compiler_params=_cp(2))(qh, o, lse, dycat, z, kh, vh)


def _out_proj_fwd(yab, yc, w, x, g, target, name, tm=512):
    t, d = x.shape
    tm = min(tm, t)
    is_last = target is not None

    def body(*refs):
        if is_last:
            yab_ref, yc_ref, w_ref, x_ref, g_ref, t_ref, y_ref, dout_ref, loss_ref = refs
        else:
            yab_ref, yc_ref, w_ref, x_ref, g_ref, y_ref, out_ref = refs
        y = _dot(jnp.concatenate([yab_ref[...], yc_ref[...]], axis=1), w_ref[...])
        y_ref[...] = y
        out = x_ref[...] + _rms(y, g_ref[...])[0]
        if is_last:
            diff = out - t_ref[...]
            dout_ref[...] = diff * (1.0 / d)
            part = jnp.sum(jnp.sum(diff * diff, axis=-1, keepdims=True), axis=0, keepdims=True) * (0.5 / d)
            _acc(loss_ref, jnp.broadcast_to(part, (1, 128)), pl.program_id(0) == 0)
        else:
            out_ref[...] = out

    row = lambda wd: pl.BlockSpec((tm, wd), lambda i: (i, 0))
    fixed = lambda a, b: pl.BlockSpec((a, b), lambda i: (0, 0))
    in_specs = [row(512), row(512), fixed(d, d), row(d), fixed(1, d)]
    args = [yab, yc, w, x, g]
    out_specs = [row(d), row(d)]
    out_shape = [jax.ShapeDtypeStruct((t, d), F32), jax.ShapeDtypeStruct((t, d), F32)]
    if is_last:
        in_specs.append(row(d))
        args.append(target)
        out_specs.append(fixed(1, 128))
        out_shape.append(jax.ShapeDtypeStruct((1, 128), F32))
    return pl.pallas_call(body, name=name, grid=(t // tm,), in_specs=in_specs, out_specs=out_specs,
                          out_shape=out_shape, compiler_params=_cp())(*args)


def _out_proj_bwd(dout, y, yab, yc, w, g, name, tm=512):
    t, d = y.shape
    tm = min(tm, t)

    def body(dout_ref, y_ref, yab_ref, yc_ref, w_ref, g_ref, dycat_ref, dw_ref, dg_ref):
        first = pl.program_id(0) == 0
        y = y_ref[...]
        r = lax.rsqrt(jnp.mean(y * y, axis=-1, keepdims=True) + EPS)
        dy, dgt = _rms_bwd(y, r, g_ref[...], dout_ref[...])
        _acc(dg_ref, _colsum(dgt), first)
        dyb = _mx(dy)
        _acc(dw_ref, _dot_tn(jnp.concatenate([yab_ref[...], yc_ref[...]], axis=1), dyb), first)
        dycat_ref[...] = _dot_nt(dyb, w_ref[...])

    row = lambda wd: pl.BlockSpec((tm, wd), lambda i: (i, 0))
    fixed = lambda a, b: pl.BlockSpec((a, b), lambda i: (0, 0))
    return pl.pallas_call(
        body, name=name, grid=(t // tm,),
        in_specs=[row(d), row(d), row(512), row(512), fixed(d, d), fixed(1, d)],
        out_specs=[row(d), fixed(d, d), fixed(1, d)],
        out_shape=[jax.ShapeDtypeStruct((t, d), F32), jax.ShapeDtypeStruct((d, d), F32),
                   jax.ShapeDtypeStruct((1, d), F32)],
        compiler_params=_cp())(dout, y, yab, yc, w, g)


def _mesh_pos():
    return lax.axis_index("x"), lax.axis_index("y"), lax.axis_index("c")


def _remote(src, dst, send_sem, recv_sem, to):
    return pltpu.make_async_remote_copy(src_ref=src, dst_ref=dst, send_sem=send_sem, recv_sem=recv_sem,
                                        device_id=to, device_id_type=MESH)


CHUNK_ROWS = 256


def _pieces(rows):
    return [(s, min(CHUNK_ROWS, rows - s)) for s in range(0, rows, CHUNK_ROWS)]


def _piece_table(shapes):
    return [(a, s, sz) for a, shp in enumerate(shapes) for s, sz in _pieces(shp[-2])]


def _gather_weights(shards):
    n = len(shards)
    table = _piece_table([s.shape for s in shards])
    npc = len(table)

    def body(*refs):
        ins, outs = refs[:n], refs[n:2 * n]
        send_sems, recv_sems, fwd_send, fwd_recv = refs[2 * n:]
        x, y, c = _mesh_pos()
        me, sibling = (x, y, c), (x, y, 1 - c)
        chips = [(1 - x, y), (x, 1 - y), (1 - x, 1 - y)]
        slot = lambda cx, cy, layer: 2 * (2 * cx + cy) + layer
        first = []
        for a in range(n):
            for j, (cx, cy) in enumerate(chips):
                first.append(_remote(ins[a].at[c], outs[a].at[slot(x, y, c)], send_sems.at[a, j], recv_sems.at[a, j],
                                     (cx, cy, c)))
                first[-1].start()
        passed = []
        for j, (cx, cy) in enumerate(chips):
            for a in range(n):
                blk = outs[a].at[slot(cx, cy, c)]
                _remote(blk, blk, send_sems.at[a, j], recv_sems.at[a, j], me).wait_recv()
            for q, (a, s, sz) in enumerate(table):
                rows = outs[a].at[slot(cx, cy, c), pl.ds(s, sz)]
                passed.append(_remote(rows, rows, fwd_send.at[j, q], fwd_recv.at[j, q], sibling))
                passed[-1].start()
        for j, (cx, cy) in enumerate(chips):
            for q, (a, s, sz) in enumerate(table):
                rows = outs[a].at[slot(cx, cy, 1 - c), pl.ds(s, sz)]
                _remote(rows, rows, fwd_send.at[j, q], fwd_recv.at[j, q], me).wait_recv()
        for cp in first + passed:
            cp.wait_send()

    return pl.pallas_call(
        body, name="gather_weights", in_specs=[ANY] * n, out_specs=[ANY] * n,
        out_shape=[jax.ShapeDtypeStruct((8,) + s.shape[1:], s.dtype) for s in shards],
        scratch_shapes=[pltpu.SemaphoreType.DMA((n, 3)), pltpu.SemaphoreType.DMA((n, 3)),
                        pltpu.SemaphoreType.DMA((3, npc)), pltpu.SemaphoreType.DMA((3, npc))])(*shards)


def _pair_exchange(parts, common):
    n = len(parts)
    table = _piece_table([p.shape for p in parts] + [common.shape])
    npc = len(table)

    def body(*refs):
        ins, outs = refs[:n + 1], refs[n + 1:2 * n + 2]
        send_sems, recv_sems = refs[2 * n + 2:]
        x, y, c = _mesh_pos()
        sent = []
        for k in range(4):
            for q, (a, s, sz) in enumerate(table):
                if a == n and k > 0:
                    continue
                src = ins[a].at[2 * k + 1 - c, pl.ds(s, sz)] if a < n else ins[a].at[pl.ds(s, sz)]
                dst = outs[a].at[k, pl.ds(s, sz)] if a < n else outs[a].at[pl.ds(s, sz)]
                sent.append(_remote(src, dst, send_sems.at[k, q], recv_sems.at[k, q], (x, y, 1 - c)))
                sent[-1].start()
        for k in range(4):
            for q, (a, s, sz) in enumerate(table):
                if a == n and k > 0:
                    continue
                dst = outs[a].at[k, pl.ds(s, sz)] if a < n else outs[a].at[pl.ds(s, sz)]
                _remote(dst, dst, send_sems.at[k, q], recv_sems.at[k, q], (x, y, c)).wait_recv()
        for cp in sent:
            cp.wait_send()

    return pl.pallas_call(
        body, name="grad_pair_exchange", in_specs=[ANY] * (n + 1), out_specs=[ANY] * (n + 1),
        out_shape=[jax.ShapeDtypeStruct((4,) + p.shape[1:], p.dtype) for p in parts]
        + [jax.ShapeDtypeStruct(common.shape, common.dtype)],
        scratch_shapes=[pltpu.SemaphoreType.DMA((4, npc)), pltpu.SemaphoreType.DMA((4, npc))])(*parts, common)


def _chip_exchange(parts, common):
    n = len(parts)
    table = _piece_table([p.shape for p in parts] + [common.shape])
    npc = len(table)

    def body(*refs):
        ins, outs = refs[:n + 1], refs[n + 1:2 * n + 2]
        send_sems, recv_sems = refs[2 * n + 2:]
        x, y, c = _mesh_pos()
        mine = 2 * x + y
        chips = [(1 - x, y), (x, 1 - y), (1 - x, 1 - y)]
        src = lambda a, k: ins[a].at[k] if a < n else ins[a]
        sent = []
        for j, (cx, cy) in enumerate(chips):
            for q, (a, s, sz) in enumerate(table):
                sent.append(_remote(src(a, 2 * cx + cy).at[pl.ds(s, sz)], outs[a].at[mine, pl.ds(s, sz)],
                                    send_sems.at[j, q], recv_sems.at[j, q], (cx, cy, c)))
                sent[-1].start()
        for j, (cx, cy) in enumerate(chips):
            for q, (a, s, sz) in enumerate(table):
                dst = outs[a].at[2 * cx + cy, pl.ds(s, sz)]
                _remote(dst, dst, send_sems.at[j, q], recv_sems.at[j, q], (x, y, c)).wait_recv()
        for cp in sent:
            cp.wait_send()

    return pl.pallas_call(
        body, name="grad_chip_exchange", in_specs=[ANY] * (n + 1), out_specs=[ANY] * (n + 1),
        out_shape=[jax.ShapeDtypeStruct(p.shape, p.dtype) for p in parts]
        + [jax.ShapeDtypeStruct((4,) + common.shape, common.dtype)],
        scratch_shapes=[pltpu.SemaphoreType.DMA((3, npc)), pltpu.SemaphoreType.DMA((3, npc))])(*parts, common)


def _sibling_exchange(sums):
    n = len(sums)
    table = _piece_table([s.shape for s in sums])
    npc = len(table)

    def body(*refs):
        ins, outs = refs[:n], refs[n:2 * n]
        send_sems, recv_sems = refs[2 * n:]
        x, y, c = _mesh_pos()
        sent = []
        for q, (a, s, sz) in enumerate(table):
            sent.append(_remote(ins[a].at[pl.ds(s, sz)], outs[a].at[pl.ds(s, sz)], send_sems.at[q], recv_sems.at[q],
                                (x, y, 1 - c)))
            sent[-1].start()
        for q, (a, s, sz) in enumerate(table):
            dst = outs[a].at[pl.ds(s, sz)]
            _remote(dst, dst, send_sems.at[q], recv_sems.at[q], (x, y, c)).wait_recv()
        for cp in sent:
            cp.wait_send()

    return pl.pallas_call(
        body, name="sibling_exchange", in_specs=[ANY] * n, out_specs=[ANY] * n,
        out_shape=[jax.ShapeDtypeStruct(s.shape, s.dtype) for s in sums],
        scratch_shapes=[pltpu.SemaphoreType.DMA((npc,)), pltpu.SemaphoreType.DMA((npc,))])(*sums)


def _pair_sum(mine, got, name, tr):
    _, r, c = got.shape
    tr = min(tr, r)

    def body(p_ref, g_ref, o_ref, w_ref):
        total = p_ref[...] + g_ref[...]
        o_ref[...] = total
        w_ref[...] = total.astype(WIRE_DTYPE)

    blk = pl.BlockSpec((None, tr, c), lambda k, i: (k, i, 0))
    return pl.pallas_call(
        body, name=name, grid=(4, r // tr), in_specs=[blk, blk], out_specs=[blk, blk],
        out_shape=[jax.ShapeDtypeStruct(got.shape, F32), jax.ShapeDtypeStruct(got.shape, WIRE_DTYPE)],
        compiler_params=_cp(2))(mine, got)


def _sum_chips(own, recv, name, tr):
    r, c = own.shape
    tr = min(tr, r)

    def body(own_ref, r_ref, o_ref):
        chip = 2 * lax.axis_index("x") + lax.axis_index("y")
        own_blk = own_ref[...]
        acc = jnp.where(chip == 0, own_blk, r_ref[0].astype(F32))
        for k in range(1, 4):
            acc = acc + jnp.where(chip == k, own_blk, r_ref[k].astype(F32))
        o_ref[...] = acc

    return pl.pallas_call(
        body, name=name, grid=(r // tr,),
        in_specs=[pl.BlockSpec((tr, c), lambda i: (i, 0)), pl.BlockSpec((4, tr, c), lambda i: (0, i, 0))],
        out_specs=pl.BlockSpec((tr, c), lambda i: (i, 0)),
        out_shape=jax.ShapeDtypeStruct((r, c), F32), compiler_params=_cp())(own, recv)


def _sum_leading(parts, name, tr):
    nlead, r, c = parts.shape
    tr = min(tr, r)

    def body(p_ref, o_ref):
        acc = p_ref[0]
        for j in range(1, nlead):
            acc = acc + p_ref[j]
        o_ref[...] = acc

    return pl.pallas_call(
        body, name=name, grid=(r // tr,),
        in_specs=[pl.BlockSpec((nlead, tr, c), lambda i: (0, i, 0))], out_specs=pl.BlockSpec((tr, c), lambda i: (i, 0)),
        out_shape=jax.ShapeDtypeStruct((r, c), parts.dtype), compiler_params=_cp())(parts)


def _adamw(w, g, m, v, name, tr):
    r, c = w.shape
    tr = min(tr, r)

    def body(w_ref, g_ref, m_ref, v_ref, d_ref, nm_ref, nv_ref):
        gg = g_ref[...]
        nm = ADAM_B1 * m_ref[...] + (1.0 - ADAM_B1) * gg
        nv = ADAM_B2 * v_ref[...] + (1.0 - ADAM_B2) * jnp.square(gg)
        m_hat = nm / (1.0 - ADAM_B1 ** ADAM_STEP)
        v_hat = nv / (1.0 - ADAM_B2 ** ADAM_STEP)
        d_ref[...] = -ADAM_LR * (m_hat / (jnp.sqrt(v_hat) + ADAM_EPS) + ADAM_WD * w_ref[...])
        nm_ref[...] = nm
        nv_ref[...] = nv

    blk = pl.BlockSpec((tr, c), lambda i: (i, 0))
    return pl.pallas_call(
        body, name=name, grid=(r // tr,), in_specs=[blk] * 4, out_specs=[blk] * 3,
        out_shape=[jax.ShapeDtypeStruct((r, c), F32)] * 3, compiler_params=_cp())(w, g, m, v)


def _rope_tables(positions):
    inv_freq = ROPE_BASE ** (-jnp.arange(0, 64, 2, dtype=F32) / 64)
    ang = positions.astype(F32)[:, None] * inv_freq
    cos, sin = jnp.cos(ang), jnp.sin(ang)
    t = positions.shape[0]
    rc = jnp.concatenate([jnp.ones((t, 128), F32), cos, cos, jnp.ones((t, 64), F32)], axis=1)
    rs = jnp.concatenate([jnp.zeros((t, 128), F32), sin, sin, jnp.zeros((t, 64), F32)], axis=1)
    return rc, rs


def _layer_params(l, w_in, w_uq, w_ukv, w_out, small):
    p = {}
    p["w_in"] = jnp.concatenate([w_in[l][:, :1984], jnp.zeros((1024, 64), w_in.dtype), w_in[l][:, 1984:]], axis=1)
    p["w_uq"] = jnp.pad(w_uq[l].reshape(384, 4, 192), ((0, 0), (0, 0), (0, 64))).reshape(384, 1024)
    p["w_ukv"] = w_ukv[l].reshape(256, 4, 2, 128).transpose(0, 2, 1, 3).reshape(256, 1024)
    p["w_out"] = w_out[l]
    p["pre_g"] = small["pre_norm_g"][l][None]
    p["post_g"] = small["post_norm_g"][l][None]
    p["sgu_w"] = small["sgu_w"][l].reshape(512, 128)
    p["sgu_wt"] = small["sgu_w"][l].transpose(0, 2, 1).reshape(512, 128)
    p["sgu_bias"] = jnp.repeat(small["sgu_b"][l].T, 64, axis=1)
    p["ln_g"] = small["sgu_ln_g"][l][None]
    p["ln_b"] = small["sgu_ln_b"][l][None]
    p["pool_wbd"] = _mx(jax.scipy.linalg.block_diag(*[small["pool_w"][l][gi] for gi in range(4)]))
    p["pool_scale"] = small["pool_scale"][l][None]
    p["gq"] = small["q_norm_g"][l][None]
    p["gkv"] = small["kv_norm_g"][l][None]
    return p


def _layer_fwd(l, x, p, rc, rs, target):
    z, h = _in_proj_fwd(x, p["pre_g"], p["w_in"], f"in_proj_fwd_{l}")
    yab = _mix_fwd(z, p["sgu_w"], p["sgu_bias"], p["ln_g"], p["ln_b"], p["pool_wbd"], p["pool_scale"], f"mix_fwd_{l}")
    qh, kh, vh = _qkv_fwd(z, rc, rs, p["w_uq"], p["w_ukv"], p["gq"], p["gkv"], f"qkv_fwd_{l}")
    o, yc, lse = _attn_fwd(qh, kh, vh, z, f"attn_fwd_{l}")
    outs = _out_proj_fwd(yab, yc, p["w_out"], x, p["post_g"], target, f"out_proj_fwd_{l}")
    saved = dict(x=x, z=z, h=h, yab=yab, qh=qh, kh=kh, vh=vh, o=o, yc=yc, lse=lse, y=outs[0])
    return saved, outs[1:]


def _layer_bwd(l, dout, sv, p, rc, rs):
    dycat, dw_out, dpost = _out_proj_bwd(dout, sv["y"], sv["yab"], sv["yc"], p["w_out"], p["post_g"], f"out_proj_bwd_{l}")
    dq, dgate, dk, dv = _attn_bwd(sv["qh"], sv["kh"], sv["vh"], sv["o"], sv["lse"], dycat, sv["z"], f"attn_bwd_{l}")
    dzc, dzk, dwq, dwkv, dgq, dgkv = _qkv_bwd(dq, dk, dv, sv["z"], rc, rs, p["w_uq"], p["w_ukv"], p["gq"], p["gkv"],
                                              f"qkv_bwd_{l}")
    dzm, dsw, dsb, dlng, dlnb, dpw, dps = _mix_bwd(sv["z"], dycat, p["sgu_w"], p["sgu_wt"], p["sgu_bias"], p["ln_g"],
                                                   p["ln_b"], p["pool_wbd"], p["pool_scale"], f"mix_bwd_{l}")
    dx, dw_in, dpre = _in_proj_bwd(dzm, dzc, dzk, dgate, sv["h"], sv["x"], dout, p["w_in"], p["pre_g"], f"in_proj_bwd_{l}")
    grads = {
        "pre_norm_g": dpre[0], "post_norm_g": dpost[0],
        "w_in": jnp.concatenate([dw_in[:, :1984], dw_in[:, 2048:]], axis=1),
        "sgu_w": dsw.reshape(4, 128, 128), "sgu_b": dsb[:, :4].T, "sgu_ln_g": dlng[0], "sgu_ln_b": dlnb[0],
        "pool_w": jnp.stack([dpw[64 * gi:64 * gi + 64, 64 * gi:64 * gi + 64] for gi in range(4)]),
        "pool_scale": dps[0], "q_norm_g": dgq[0],
        "w_uq": dwq.reshape(384, 4, 256)[:, :, :192].reshape(384, 768), "kv_norm_g": dgkv[0],
        "w_ukv": dwkv.reshape(256, 2, 4, 128).transpose(0, 2, 1, 3).reshape(256, 1024), "w_out": dw_out,
    }
    return dx, grads


SMALL_NAMES = ["pre_norm_g", "post_norm_g", "sgu_w", "sgu_b", "sgu_ln_g", "sgu_ln_b", "pool_w", "pool_scale",
               "q_norm_g", "kv_norm_g"]
BIG_NAMES = ["w_in", "w_uq", "w_ukv", "w_out"]
WEIGHT_NAMES = ["pre_norm_g", "post_norm_g", "w_in", "sgu_w", "sgu_b", "sgu_ln_g", "sgu_ln_b", "pool_w", "pool_scale",
                "q_norm_g", "w_uq", "kv_norm_g", "w_ukv", "w_out"]


def _local_step(x, positions, target, w_in, w_uq, w_ukv, w_out, small):
    rc, rs = _rope_tables(positions)
    params = [_layer_params(l, w_in, w_uq, w_ukv, w_out, small) for l in range(DEPTH)]
    saved = []
    for l in range(DEPTH):
        sv, outs = _layer_fwd(l, x, params[l], rc, rs, target if l == DEPTH - 1 else None)
        saved.append(sv)
        if l < DEPTH - 1:
            x = outs[0]
    dout, loss = outs
    grads = [None] * DEPTH
    for l in reversed(range(DEPTH)):
        dout, grads[l] = _layer_bwd(l, dout, saved[l], params[l], rc, rs)
    return loss[0, 0], dout, {k: jnp.stack([grads[l][k] for l in range(DEPTH)]) for k in WEIGHT_NAMES}


def _pack_small(tree, extra=None):
    pieces = [tree[k].reshape(-1) for k in SMALL_NAMES]
    pieces.append(jnp.zeros((1,), F32) if extra is None else extra.reshape(1))
    flat = jnp.concatenate(pieces)
    rows = -(-flat.shape[0] // 1024) * 8
    return jnp.pad(flat, (0, rows * 128 - flat.shape[0])).reshape(rows, 128)


def _unpack_small(packed, like):
    flat = packed.reshape(-1)
    out, off = {}, 0
    for k in SMALL_NAMES:
        size = like[k].size
        out[k] = flat[off:off + size].reshape(like[k].shape)
        off += size
    return out, flat[off]


def kernel(x, positions, pre_norm_g, post_norm_g, w_in, sgu_w, sgu_b, sgu_ln_g, sgu_ln_b, pool_w, pool_scale, q_norm_g, w_uq, kv_norm_g, w_ukv, w_out, loss_target, m_pre_norm_g, m_post_norm_g, m_w_in, m_sgu_w, m_sgu_b, m_sgu_ln_g, m_sgu_ln_b, m_pool_w, m_pool_scale, m_q_norm_g, m_w_uq, m_kv_norm_g, m_w_ukv, m_w_out, v_pre_norm_g, v_post_norm_g, v_w_in, v_sgu_w, v_sgu_b, v_sgu_ln_g, v_sgu_ln_b, v_pool_w, v_pool_scale, v_q_norm_g, v_w_uq, v_kv_norm_g, v_w_ukv, v_w_out):
    w = dict(pre_norm_g=pre_norm_g, post_norm_g=post_norm_g, w_in=w_in, sgu_w=sgu_w, sgu_b=sgu_b, sgu_ln_g=sgu_ln_g,
             sgu_ln_b=sgu_ln_b, pool_w=pool_w, pool_scale=pool_scale, q_norm_g=q_norm_g, w_uq=w_uq, kv_norm_g=kv_norm_g,
             w_ukv=w_ukv, w_out=w_out)
    m = dict(pre_norm_g=m_pre_norm_g, post_norm_g=m_post_norm_g, w_in=m_w_in, sgu_w=m_sgu_w, sgu_b=m_sgu_b,
             sgu_ln_g=m_sgu_ln_g, sgu_ln_b=m_sgu_ln_b, pool_w=m_pool_w, pool_scale=m_pool_scale, q_norm_g=m_q_norm_g,
             w_uq=m_w_uq, kv_norm_g=m_kv_norm_g, w_ukv=m_w_ukv, w_out=m_w_out)
    v = dict(pre_norm_g=v_pre_norm_g, post_norm_g=v_post_norm_g, w_in=v_w_in, sgu_w=v_sgu_w, sgu_b=v_sgu_b,
             sgu_ln_g=v_sgu_ln_g, sgu_ln_b=v_sgu_ln_b, pool_w=v_pool_w, pool_scale=v_pool_scale, q_norm_g=v_q_norm_g,
             w_uq=v_w_uq, kv_norm_g=v_kv_norm_g, w_ukv=v_w_ukv, w_out=v_w_out)

    core = lax.axis_index("c")
    chip = 2 * lax.axis_index("x") + lax.axis_index("y")
    shards = [_mx(w[k]) for k in BIG_NAMES]
    gathered = _gather_weights(shards)
    g_in, g_uq, g_ukv, g_out = [lax.dynamic_update_slice(g, s, (2 * chip, 0, 0)) for g, s in zip(gathered, shards)]
    cols = lambda g: g.reshape((4, 2) + g.shape[1:]).transpose(1, 2, 0, 3).reshape(2, g.shape[1], 4 * g.shape[2])
    full_out = g_out.reshape(4, 2, 256, 1024).transpose(1, 0, 2, 3).reshape(2, 1024, 1024)
    loss, dx, grads = _local_step(x[0], positions[0], loss_target[0], cols(g_in), cols(g_uq), cols(g_ukv), full_out, w)

    split_cols = lambda g: g.reshape(2, g.shape[1], 4, g.shape[2] // 4).transpose(2, 0, 1, 3).reshape(8, g.shape[1], g.shape[2] // 4)
    parts = [split_cols(grads["w_in"]), split_cols(grads["w_uq"]), split_cols(grads["w_ukv"]),
             grads["w_out"].reshape(2, 4, 256, 1024).transpose(1, 0, 2, 3).reshape(8, 256, 1024)]
    common = _pack_small(grads, loss)
    got = _pair_exchange(parts, common)
    mine = [lax.dynamic_index_in_dim(p.reshape((4, 2) + p.shape[1:]), core, axis=1, keepdims=False) for p in parts]
    pair_sums = [_pair_sum(mine[a], got[a], f"pair_sum_{BIG_NAMES[a]}", 128) for a in range(4)]
    chip_common = _sum_leading(jnp.stack([common, got[4]]), "pair_sum_small", common.shape[0])
    received = _chip_exchange([ps[1] for ps in pair_sums], chip_common)
    sums = [_sum_chips(lax.dynamic_index_in_dim(pair_sums[a][0], chip, axis=0, keepdims=False), received[a],
                       f"sum_{BIG_NAMES[a]}", 128) for a in range(4)]
    all_common = lax.dynamic_update_slice(received[4], chip_common[None], (chip, 0, 0))
    small_sum, loss = _unpack_small(_sum_leading(all_common, "sum_small", all_common.shape[1]), w)
    others = _sibling_exchange(sums)
    total = dict(small_sum)
    for a, k in enumerate(BIG_NAMES):
        total[k] = jnp.where(core == 0, jnp.stack([sums[a], others[a]]), jnp.stack([others[a], sums[a]]))

    packed = _adamw(_pack_small(w), _pack_small(total), _pack_small(m), _pack_small(v), "adamw_small", 2048)
    small_out = [_unpack_small(pk, w)[0] for pk in packed]
    delta, new_m, new_v = {}, {}, {}
    for k in SMALL_NAMES:
        delta[k], new_m[k], new_v[k] = (so[k] for so in small_out)
    for k in BIG_NAMES:
        shape = w[k].shape
        flat = lambda a: a.reshape(shape[0] * shape[1], shape[2])
        res = _adamw(flat(w[k]), flat(total[k]), flat(m[k]), flat(v[k]), f"adamw_{k}", 256)
        delta[k], new_m[k], new_v[k] = (r.reshape(shape) for r in res)

    return (loss, dx[None], *[total[k] for k in WEIGHT_NAMES], *[delta[k] for k in WEIGHT_NAMES],
            *[new_m[k] for k in WEIGHT_NAMES], *[new_v[k] for k in WEIGHT_NAMES])
```

```python
import jax
import jax.numpy as jnp
from jax import lax
from jax.experimental import pallas as pl
from jax.experimental.pallas import tpu as pltpu

F32 = jnp.float32
MXU_DTYPE = jnp.bfloat16
WIRE_DTYPE = jnp.bfloat16
EPS = 1e-6
NEG_INF = -1e30
DEPTH = 2
N_HEADS = 4
QK_PAD = 256
V_DIM = 128
SCALE = 192 ** -0.5
LOG2E = 1.4426950408889634
ROPE_BASE = 10000.0
ADAM_LR, ADAM_B1, ADAM_B2, ADAM_EPS, ADAM_WD, ADAM_STEP = 0.001, 0.9, 0.999, 1e-08, 0.01, 10
VMEM_LIMIT_BYTES = 56 * 1024 * 1024
MESH = pl.DeviceIdType.MESH
ANY = pl.BlockSpec(memory_space=pl.ANY)

Z_MIX, Z_C, Z_KR, Z_GATE = 1280, 640, 128, 512
Z_W = Z_MIX + Z_C + Z_KR + Z_GATE


def _cp(n_axes=1):
    return pltpu.CompilerParams(dimension_semantics=("arbitrary",) * n_axes, vmem_limit_bytes=VMEM_LIMIT_BYTES)


def _dot(a, b):
    return lax.dot_general(a, b, (((1,), (0,)), ((), ())), preferred_element_type=F32)


def _dot_nt(a, b):
    return lax.dot_general(a, b, (((1,), (1,)), ((), ())), preferred_element_type=F32)


def _dot_tn(a, b):
    return lax.dot_general(a, b, (((0,), (0,)), ((), ())), preferred_element_type=F32)


def _mx(a):
    return a.astype(MXU_DTYPE)


def _silu_and_grad(g):
    sg = jax.nn.sigmoid(g)
    return g * sg, sg * (1.0 + g * (1.0 - sg))


def _rms(x, g):
    r = lax.rsqrt(jnp.mean(x * x, axis=-1, keepdims=True) + EPS)
    return x * r * g, r


def _rms_bwd(x, r, g, dy):
    xhat = x * r
    dyg = dy * g
    dx = r * (dyg - xhat * jnp.mean(dyg * xhat, axis=-1, keepdims=True))
    return dx, dy * xhat


def _zero_when(first, *refs):
    @pl.when(first)
    def _():
        for ref in refs:
            ref[...] = jnp.zeros(ref.shape, ref.dtype)


def _acc(ref, val):
    ref[...] += val


def _colsum(a):
    return jnp.sum(a, axis=0, keepdims=True)


def _in_proj_fwd(x, g, w, name, tm=512):
    t, d = x.shape
    n = w.shape[1]
    tm = min(tm, t)

    def body(x_ref, g_ref, w_ref, z_ref, h_ref):
        h, _ = _rms(x_ref[...], g_ref[...])
        h = _mx(h)
        h_ref[...] = h
        z_ref[...] = _dot(h, w_ref[...])

    return pl.pallas_call(
        body, name=name, grid=(t // tm,),
        in_specs=[pl.BlockSpec((tm, d), lambda i: (i, 0)), pl.BlockSpec((1, d), lambda i: (0, 0)),
                  pl.BlockSpec((d, n), lambda i: (0, 0))],
        out_specs=[pl.BlockSpec((tm, n), lambda i: (i, 0)), pl.BlockSpec((tm, d), lambda i: (i, 0))],
        out_shape=[jax.ShapeDtypeStruct((t, n), F32), jax.ShapeDtypeStruct((t, d), MXU_DTYPE)],
        compiler_params=_cp())(x, g, w)


def _in_proj_bwd(dz_mix, dz_c, dz_kr, dz_gate, h, x, d_res, w, g, name, tm=256):
    t, d = x.shape
    n = w.shape[1]
    tm = min(tm, t)

    def body(dm_ref, dc_ref, dk_ref, dg_ref, h_ref, x_ref, dres_ref, w_ref, g_ref, dx_ref, dw_ref, dgn_ref):
        first = pl.program_id(0) == 0
        dz = jnp.concatenate([dm_ref[...], dc_ref[...], dk_ref[...], dg_ref[...]], axis=1)

        _zero_when(first, dw_ref, dgn_ref)
        hb = h_ref[...]
        for c0 in range(0, n, 512):
            dw_ref[:, c0:c0 + 512] += _dot_tn(hb, dz[:, c0:c0 + 512])
        dh = _dot_nt(dz, w_ref[...])
        xf = x_ref[...]
        r = lax.rsqrt(jnp.mean(xf * xf, axis=-1, keepdims=True) + EPS)
        dx, dgt = _rms_bwd(xf, r, g_ref[...], dh)
        dx_ref[...] = dx + dres_ref[...]
        _acc(dgn_ref, _colsum(dgt))

    row = lambda wd: pl.BlockSpec((tm, wd), lambda i: (i, 0))
    fixed = lambda a, b: pl.BlockSpec((a, b), lambda i: (0, 0))
    return pl.pallas_call(
        body, name=name, grid=(t // tm,),
        in_specs=[row(Z_MIX), row(Z_C), row(Z_KR), row(Z_GATE), row(d), row(d), row(d), fixed(d, n), fixed(1, d)],
        out_specs=[row(d), fixed(d, n), fixed(1, d)],
        out_shape=[jax.ShapeDtypeStruct((t, d), F32), jax.ShapeDtypeStruct((d, n), F32),
                   jax.ShapeDtypeStruct((1, d), F32)],
        compiler_params=_cp())(dz_mix, dz_c, dz_kr, dz_gate, h, x, d_res, w, g)


def _lane_group(shape):
    return lax.broadcasted_iota(jnp.int32, shape, 1) // 64


def _select_group(vals):
    grp = _lane_group(vals[0].shape)
    out = vals[3]
    for gi in (2, 1, 0):
        out = jnp.where(grp == gi, vals[gi], out)
    return out


def _sgu_mask(transposed):
    r = (lax.broadcasted_iota(jnp.int32, (512, 128), 0) % 128) // 64
    c = lax.broadcasted_iota(jnp.int32, (512, 128), 1) // 64
    return (r <= c) if transposed else (c <= r)


def _sgu_apply(wstack, vb, nblk):
    outs = []
    for n in range(nblk):
        r = _dot(wstack, vb[n * 128:(n + 1) * 128, :])
        outs.append(_select_group([r[hh * 128:(hh + 1) * 128, :] for hh in range(4)]))
    return jnp.concatenate(outs, axis=0)


def _layer_norm(v, g, b):
    mu = jnp.mean(v, axis=-1, keepdims=True)
    vc = v - mu
    rstd = lax.rsqrt(jnp.mean(vc * vc, axis=-1, keepdims=True) + EPS)
    vhat = vc * rstd
    return vhat * g + b, vhat, rstd


def _pool_counts(t0, n):
    t = t0 + lax.broadcasted_iota(jnp.int32, (n, 256), 0)
    w = _select_group([jnp.full((n, 256), wv, jnp.int32) for wv in (2, 4, 8, 16)])
    return jnp.minimum(t + 1, w).astype(F32)


def _pooled(p, halo, t0):
    tm = p.shape[0]
    ext = jnp.concatenate([halo, p], axis=0)
    s2 = ext + pltpu.roll(ext, 1, 0)
    s4 = s2 + pltpu.roll(s2, 2, 0)
    s8 = s4 + pltpu.roll(s4, 4, 0)
    s16 = s8 + pltpu.roll(s8, 8, 0)
    sel = _select_group([s2, s4, s8, s16])[16:, :]
    return sel / _pool_counts(t0, tm) - p


def _pooled_bwd(dpool, dpool_halo, t0):
    tm = dpool.shape[0]
    n = tm + 16
    ext = jnp.concatenate([dpool, dpool_halo], axis=0) / _pool_counts(t0, n)
    f2 = ext + pltpu.roll(ext, n - 1, 0)
    f4 = f2 + pltpu.roll(f2, n - 2, 0)
    f8 = f4 + pltpu.roll(f4, n - 4, 0)
    f16 = f8 + pltpu.roll(f8, n - 8, 0)
    return _select_group([f2, f4, f8, f16])[:tm, :] - dpool


def _mix_specs(t, tm):
    nt16 = t // 16
    zrow = pl.BlockSpec((tm, Z_MIX), lambda i: (i, 0))
    prev_halo = pl.BlockSpec((16, 256), lambda i: (jnp.maximum(i * (tm // 16) - 1, 0), 3))
    fixed = lambda a, b: pl.BlockSpec((a, b), lambda i: (0, 0))
    params = [fixed(512, 128), fixed(128, 256), fixed(1, 256), fixed(1, 256), fixed(256, 256), fixed(1, 256)]
    return nt16, zrow, prev_halo, fixed, params


def _mix_fwd(z, sgu_w, sgu_bias, ln_g, ln_b, pool_wbd, pool_scale, name, tm=256):
    t = z.shape[0]
    tm = min(tm, t)
    _, zrow, prev_halo, _, params = _mix_specs(t, tm)

    def body(z_ref, halo_ref, w_ref, bias_ref, lng_ref, lnb_ref, pw_ref, ps_ref, y_ref):
        i = pl.program_id(0)
        u, v, gate = z_ref[:, 0:256], z_ref[:, 256:512], z_ref[:, 512:768]
        p, pgate = z_ref[:, 768:1024], z_ref[:, 1024:1280]
        vn, _, _ = _layer_norm(v, lng_ref[...], lnb_ref[...])
        wm = _mx(jnp.where(_sgu_mask(False), w_ref[...], 0.0))
        mixed = _sgu_apply(wm, _mx(vn), tm // 128) + jnp.tile(bias_ref[...], (tm // 128, 1))
        ya = u * mixed * _silu_and_grad(gate)[0]
        halo = jnp.where(i > 0, halo_ref[...], 0.0)
        pooled = _pooled(p, halo, i * tm)
        yb = _dot(_mx(pooled), pw_ref[...]) * ps_ref[...] * _silu_and_grad(pgate)[0]
        y_ref[...] = _mx(jnp.concatenate([ya, yb], axis=1))

    return pl.pallas_call(
        body, name=name, grid=(t // tm,),
        in_specs=[zrow, prev_halo] + params,
        out_specs=pl.BlockSpec((tm, 512), lambda i: (i, 0)),
        out_shape=jax.ShapeDtypeStruct((t, 512), MXU_DTYPE),
        compiler_params=_cp())(z, z, sgu_w, sgu_bias, ln_g, ln_b, pool_wbd, pool_scale)


def _mix_bwd(z, dycat, sgu_w, sgu_wt, sgu_bias, ln_g, ln_b, pool_wbd, pool_scale, name, tm=256):
    t = z.shape[0]
    tm = min(tm, t)
    nt16, zrow, prev_halo, fixed, params = _mix_specs(t, tm)
    nblk = tm // 128
    last = t // tm - 1

    def body(z_ref, halo_ref, zn_ref, dy_ref, dyn_ref, w_ref, wt_ref, bias_ref, lng_ref, lnb_ref, pw_ref, ps_ref,
             dz_ref, dw_ref, db_ref, dlng_ref, dlnb_ref, dpw_ref, dps_ref):
        i = pl.program_id(0)
        _zero_when(i == 0, dw_ref, db_ref, dlng_ref, dlnb_ref, dpw_ref, dps_ref)
        u, v, gate = z_ref[:, 0:256], z_ref[:, 256:512], z_ref[:, 512:768]
        p, pgate = z_ref[:, 768:1024], z_ref[:, 1024:1280]
        dya, dyb = dy_ref[:, 0:256], dy_ref[:, 256:512]
        vn, vhat, rstd = _layer_norm(v, lng_ref[...], lnb_ref[...])
        vnb = _mx(vn)
        wm = _mx(jnp.where(_sgu_mask(False), w_ref[...], 0.0))
        wmt = _mx(jnp.where(_sgu_mask(True), wt_ref[...], 0.0))
        mixed = _sgu_apply(wm, vnb, nblk) + jnp.tile(bias_ref[...], (nblk, 1))
        silu, dsilu = _silu_and_grad(gate)
        t1 = u * mixed
        d_gate = dya * t1 * dsilu
        d_t1 = dya * silu
        d_u = d_t1 * mixed
        d_mixed = d_t1 * u
        dmb = _mx(d_mixed)
        d_vn = _sgu_apply(wmt, dmb, nblk)
        grp = _lane_group((128, 256))
        lane = lax.broadcasted_iota(jnp.int32, (128, 128), 1)
        dws = [jnp.zeros((128, 128), F32) for _ in range(4)]
        dbias = jnp.zeros((128, 128), F32)
        for n in range(nblk):
            dm_n, dmb_n, vnb_n = d_mixed[n * 128:(n + 1) * 128], dmb[n * 128:(n + 1) * 128], vnb[n * 128:(n + 1) * 128]
            for hh in range(4):
                dws[hh] = dws[hh] + _dot_nt(jnp.where(grp == hh, dmb_n, jnp.zeros_like(dmb_n)), vnb_n)
                rs = jnp.sum(jnp.where(grp == hh, dm_n, 0.0), axis=-1, keepdims=True)
                dbias = dbias + jnp.where(lane == hh, rs, 0.0)
        _acc(dw_ref, jnp.concatenate(dws, axis=0))
        _acc(db_ref, dbias)
        _acc(dlng_ref, _colsum(d_vn * vhat))
        _acc(dlnb_ref, _colsum(d_vn))
        dvh = d_vn * lng_ref[...]
        d_v = rstd * (dvh - jnp.mean(dvh, axis=-1, keepdims=True) - vhat * jnp.mean(dvh * vhat, axis=-1, keepdims=True))

        @pl.when(i == last)
        def _():
            dw_ref[...] = jnp.where(_sgu_mask(False), dw_ref[...], 0.0)

        halo = jnp.where(i > 0, halo_ref[...], 0.0)
        pooled = _pooled(p, halo, i * tm)
        pooled_b = _mx(pooled)
        mixedp = _dot(pooled_b, pw_ref[...])
        psilu, pdsilu = _silu_and_grad(pgate)
        d_pgate = dyb * (mixedp * ps_ref[...]) * pdsilu
        d_ms = dyb * psilu
        _acc(dps_ref, _colsum(d_ms * mixedp))
        dmpb = _mx(d_ms * ps_ref[...])
        _acc(dpw_ref, _dot_tn(pooled_b, dmpb))
        d_pooled = _dot_nt(dmpb, pw_ref[...])
        dmp_halo = _mx(dyn_ref[...] * _silu_and_grad(zn_ref[...])[0] * ps_ref[...])
        d_pooled_halo = jnp.where(i < last, _dot_nt(dmp_halo, pw_ref[...]), 0.0)
        d_p = _pooled_bwd(d_pooled, d_pooled_halo, i * tm)
        dz_ref[...] = _mx(jnp.concatenate([d_u, d_v, d_gate, d_p, d_pgate], axis=1))

    nxt = lambda i: jnp.minimum((i + 1) * (tm // 16), nt16 - 1)
    return pl.pallas_call(
        body, name=name, grid=(t // tm,),
        in_specs=[zrow, prev_halo, pl.BlockSpec((16, 256), lambda i: (nxt(i), 4)),
                  pl.BlockSpec((tm, 512), lambda i: (i, 0)), pl.BlockSpec((16, 256), lambda i: (nxt(i), 1)),
                  params[0], fixed(512, 128)] + params[1:],
        out_specs=[pl.BlockSpec((tm, Z_MIX), lambda i: (i, 0)), fixed(512, 128), fixed(128, 128), fixed(1, 256),
                   fixed(1, 256), fixed(256, 256), fixed(1, 256)],
        out_shape=[jax.ShapeDtypeStruct((t, Z_MIX), MXU_DTYPE), jax.ShapeDtypeStruct((512, 128), F32),
                   jax.ShapeDtypeStruct((128, 128), F32), jax.ShapeDtypeStruct((1, 256), F32),
                   jax.ShapeDtypeStruct((1, 256), F32), jax.ShapeDtypeStruct((256, 256), F32),
                   jax.ShapeDtypeStruct((1, 256), F32)],
        compiler_params=_cp())(z, z, z, dycat, dycat, sgu_w, sgu_wt, sgu_bias, ln_g, ln_b, pool_wbd, pool_scale)


def _rot_half(x, transpose):
    w = x.shape[1]
    lane = lax.broadcasted_iota(jnp.int32, x.shape, 1) % min(w, 256)
    base = 128 if w >= 256 else 0
    lo = jnp.logical_and(lane >= base, lane < base + 32)
    hi = jnp.logical_and(lane >= base + 32, lane < base + 64)
    up = pltpu.roll(x, w - 32, 1)
    down = pltpu.roll(x, 32, 1)
    if transpose:
        return jnp.where(lo, up, jnp.where(hi, -down, 0.0))
    return jnp.where(lo, -up, jnp.where(hi, down, 0.0))


def _rope(x, c, s):
    return x * c + _rot_half(x, False) * s


def _rope_bwd(dy, c, s):
    return dy * c + _rot_half(dy * s, True)


def _qkv_fwd(z, rc, rs, w_uq, w_ukv, gq, gkv, name, tm=256):
    t = z.shape[0]
    tm = min(tm, t)

    def body(zc_ref, zk_ref, rc_ref, rs_ref, wq_ref, wkv_ref, gq_ref, gkv_ref, q_ref, k_ref, v_ref):
        cq, ckv = zc_ref[:, 0:384], zc_ref[:, 384:640]
        c, s = rc_ref[...], rs_ref[...]
        qn, _ = _rms(cq, gq_ref[...])
        q = _rope(_dot(_mx(qn), wq_ref[...]), jnp.tile(c, (1, N_HEADS)), jnp.tile(s, (1, N_HEADS)))
        kvn, _ = _rms(ckv, gkv_ref[...])
        kv = _dot(_mx(kvn), wkv_ref[...])
        kpe = _rope(zk_ref[...], c[:, 128:256], s[:, 128:256])
        for hh in range(N_HEADS):
            q_ref[hh] = _mx(q[:, hh * QK_PAD:(hh + 1) * QK_PAD])
            k_ref[hh] = _mx(jnp.concatenate([kv[:, hh * 128:(hh + 1) * 128], kpe], axis=1))
            v_ref[hh] = _mx(kv[:, 512 + hh * 128:512 + (hh + 1) * 128])

    fixed = lambda a, b: pl.BlockSpec((a, b), lambda i: (0, 0))
    heads = lambda wd: pl.BlockSpec((N_HEADS, tm, wd), lambda i: (0, i, 0))
    return pl.pallas_call(
        body, name=name, grid=(t // tm,),
        in_specs=[pl.BlockSpec((tm, Z_C), lambda i: (i, Z_MIX // Z_C)),
                  pl.BlockSpec((tm, Z_KR), lambda i: (i, (Z_MIX + Z_C) // Z_KR)),
                  pl.BlockSpec((tm, 256), lambda i: (i, 0)), pl.BlockSpec((tm, 256), lambda i: (i, 0)),
                  fixed(384, 1024), fixed(256, 1024), fixed(1, 384), fixed(1, 256)],
        out_specs=[heads(QK_PAD), heads(QK_PAD), heads(V_DIM)],
        out_shape=[jax.ShapeDtypeStruct((N_HEADS, t, QK_PAD), MXU_DTYPE),
                   jax.ShapeDtypeStruct((N_HEADS, t, QK_PAD), MXU_DTYPE),
                   jax.ShapeDtypeStruct((N_HEADS, t, V_DIM), MXU_DTYPE)],
        compiler_params=_cp())(z, z, rc, rs, w_uq, w_ukv, gq, gkv)


def _qkv_bwd(dq, dk, dv, z, rc, rs, w_uq, w_ukv, gq, gkv, name, tm=256):
    t = z.shape[0]
    tm = min(tm, t)

    def body(dq_ref, dk_ref, dv_ref, zc_ref, rc_ref, rs_ref, wq_ref, wkv_ref, gq_ref, gkv_ref,
             dzc_ref, dzk_ref, dwq_ref, dwkv_ref, dgq_ref, dgkv_ref):
        _zero_when(pl.program_id(0) == 0, dwq_ref, dwkv_ref, dgq_ref, dgkv_ref)
        cq, ckv = zc_ref[:, 0:384], zc_ref[:, 384:640]
        c, s = rc_ref[...], rs_ref[...]
        dq_all = jnp.concatenate([dq_ref[hh] for hh in range(N_HEADS)], axis=1)
        dqp = _mx(_rope_bwd(dq_all, jnp.tile(c, (1, N_HEADS)), jnp.tile(s, (1, N_HEADS))))
        qn, rq = _rms(cq, gq_ref[...])
        _acc(dwq_ref, _dot_tn(_mx(qn), dqp))
        d_cq, dgq_t = _rms_bwd(cq, rq, gq_ref[...], _dot_nt(dqp, wq_ref[...]))
        _acc(dgq_ref, _colsum(dgq_t))
        dkpe = dk_ref[0][:, 128:256]
        for hh in range(1, N_HEADS):
            dkpe = dkpe + dk_ref[hh][:, 128:256]
        dzk_ref[...] = _mx(_rope_bwd(dkpe, c[:, 128:256], s[:, 128:256]))
        dkv = _mx(jnp.concatenate([dk_ref[hh][:, 0:128] for hh in range(N_HEADS)]
                                  + [dv_ref[hh] for hh in range(N_HEADS)], axis=1))
        kvn, rkv = _rms(ckv, gkv_ref[...])
        _acc(dwkv_ref, _dot_tn(_mx(kvn), dkv))
        d_ckv, dgkv_t = _rms_bwd(ckv, rkv, gkv_ref[...], _dot_nt(dkv, wkv_ref[...]))
        _acc(dgkv_ref, _colsum(dgkv_t))
        dzc_ref[...] = _mx(jnp.concatenate([d_cq, d_ckv], axis=1))

    fixed = lambda a, b: pl.BlockSpec((a, b), lambda i: (0, 0))
    heads = lambda wd: pl.BlockSpec((N_HEADS, tm, wd), lambda i: (0, i, 0))
    return pl.pallas_call(
        body, name=name, grid=(t // tm,),
        in_specs=[heads(QK_PAD), heads(QK_PAD), heads(V_DIM), pl.BlockSpec((tm, Z_C), lambda i: (i, Z_MIX // Z_C)),
                  pl.BlockSpec((tm, 256), lambda i: (i, 0)), pl.BlockSpec((tm, 256), lambda i: (i, 0)),
                  fixed(384, 1024), fixed(256, 1024), fixed(1, 384), fixed(1, 256)],
        out_specs=[pl.BlockSpec((tm, Z_C), lambda i: (i, 0)), pl.BlockSpec((tm, Z_KR), lambda i: (i, 0)),
                   fixed(384, 1024), fixed(256, 1024), fixed(1, 384), fixed(1, 256)],
        out_shape=[jax.ShapeDtypeStruct((t, Z_C), MXU_DTYPE), jax.ShapeDtypeStruct((t, Z_KR), MXU_DTYPE),
                   jax.ShapeDtypeStruct((384, 1024), F32), jax.ShapeDtypeStruct((256, 1024), F32),
                   jax.ShapeDtypeStruct((1, 384), F32), jax.ShapeDtypeStruct((1, 256), F32)],
        compiler_params=_cp())(dq, dk, dv, z, rc, rs, w_uq, w_ukv, gq, gkv)


def _loop_in_long_trips(n, body):
    def two(t, carry):
        return body(2 * t + 1, body(2 * t, carry))

    def four(t, carry):
        return two(2 * t + 1, two(2 * t, carry))

    lax.fori_loop(0, n // 4, four, 0)
    lax.fori_loop(2 * (n // 4), n // 2, two, 0)
    lax.fori_loop(2 * (n // 2), n, body, 0)


def _init_mask_bias(bias_ref):
    tq = bias_ref.shape[1]
    r = lax.broadcasted_iota(jnp.int32, (tq, tq), 0) // 64
    c = lax.broadcasted_iota(jnp.int32, (tq, tq), 1) // 64
    bias_ref[0] = jnp.zeros((tq, tq), F32)
    bias_ref[1] = jnp.where(c <= r, 0.0, NEG_INF)


def _gate_block(tq):
    return pl.BlockSpec((tq, 128), lambda h, i: (i, (Z_MIX + Z_C + Z_KR) // 128 + h))


def _attn_fwd(qh, kh, vh, z, name, tq=512):
    t = qh.shape[1]
    tq = min(tq, t)

    def body(q_ref, g_ref, k_hbm, v_hbm, o_ref, yc_ref, lse_ref, k_v, v_v, m_s, acc_s, s_a, s_b, mx_a, mx_b, bias_s,
             sem):
        h, i = pl.program_id(0), pl.program_id(1)

        @pl.when(i == 0)
        def _():
            ck = pltpu.make_async_copy(k_hbm.at[h], k_v, sem.at[0])
            cv = pltpu.make_async_copy(v_hbm.at[h], v_v.at[:, 0:V_DIM], sem.at[1])
            ck.start()
            cv.start()
            v_v[:, V_DIM:2 * V_DIM] = jnp.ones((t, V_DIM), MXU_DTYPE)
            _init_mask_bias(bias_s)
            ck.wait()
            cv.wait()

        q = q_ref[...]
        m_s[...] = jnp.full(m_s.shape, NEG_INF, F32)
        acc_s[...] = jnp.zeros(acc_s.shape, F32)

        def keys(j):
            return pl.ds(pl.multiple_of(j * tq, tq), tq)

        def scores(s_ref, mx_ref, j):
            s = _dot_nt(q, k_v[keys(j), :]) * (SCALE * LOG2E) + bias_s[(j == i).astype(jnp.int32)]
            s_ref[...] = s
            mx_ref[...] = jnp.broadcast_to(jnp.max(s, axis=-1, keepdims=True), mx_ref.shape)

        def softmax_pv(s_ref, mx_ref, j):
            m_old = m_s[...]
            m_new = jnp.maximum(m_old, mx_ref[...])
            p = jnp.exp2(s_ref[...] - jnp.tile(m_new, (1, tq // 128)))
            alpha = jnp.exp2(m_old - m_new)
            m_s[...] = m_new
            acc_s[...] = jnp.tile(alpha, (1, 2)) * acc_s[...] + _dot(_mx(p), v_v[keys(j), :])

        scores(s_a, mx_a, 0)

        def pair(pp, carry):
            scores(s_b, mx_b, 2 * pp + 1)
            softmax_pv(s_a, mx_a, 2 * pp)
            scores(s_a, mx_a, jnp.minimum(2 * pp + 2, i))
            softmax_pv(s_b, mx_b, 2 * pp + 1)
            return carry

        _loop_in_long_trips((i + 1) // 2, pair)

        @pl.when(i % 2 == 0)
        def _():
            softmax_pv(s_a, mx_a, i)

        l = acc_s[:, V_DIM:2 * V_DIM]
        o = acc_s[:, 0:V_DIM] / l
        o_ref[...] = o
        yc_ref[...] = _mx(o * _silu_and_grad(g_ref[...])[0])
        lse_ref[...] = m_s[...] + jnp.log2(l)

    return pl.pallas_call(
        body, name=name, grid=(N_HEADS, t // tq),
        in_specs=[pl.BlockSpec((None, tq, QK_PAD), lambda h, i: (h, i, 0)), _gate_block(tq), ANY, ANY],
        out_specs=[pl.BlockSpec((tq, 128), lambda h, i: (i, h)), pl.BlockSpec((tq, 128), lambda h, i: (i, h)),
                   pl.BlockSpec((None, tq, 128), lambda h, i: (h, i, 0))],
        out_shape=[jax.ShapeDtypeStruct((t, N_HEADS * V_DIM), F32), jax.ShapeDtypeStruct((t, N_HEADS * V_DIM), MXU_DTYPE),
                   jax.ShapeDtypeStruct((N_HEADS, t, 128), F32)],
        scratch_shapes=[pltpu.VMEM((t, QK_PAD), MXU_DTYPE), pltpu.VMEM((t, 2 * V_DIM), MXU_DTYPE),
                        pltpu.VMEM((tq, 128), F32), pltpu.VMEM((tq, 2 * V_DIM), F32),
                        pltpu.VMEM((tq, tq), F32), pltpu.VMEM((tq, tq), F32), pltpu.VMEM((tq, 128), F32),
                        pltpu.VMEM((tq, 128), F32), pltpu.VMEM((2, tq, tq), F32), pltpu.SemaphoreType.DMA((2,))],
        compiler_params=_cp(2))(qh, z, kh, vh)


def _attn_bwd(qh, kh, vh, o, lse, dycat, z, name, tq=512):
    t = qh.shape[1]
    tq = min(tq, t)
    nq = t // tq

    def body(q_ref, o_ref, lse_ref, dy_ref, g_ref, k_hbm, v_hbm, dq_ref, dgate_ref, dk_hbm, dv_hbm,
             k_v, v_v, dk_acc, dv_acc, dq_acc, delta_s, s_a, dp_a, s_b, dp_b, bias_s, sem):
        h, i = pl.program_id(0), pl.program_id(1)

        @pl.when(i == 0)
        def _():
            ck = pltpu.make_async_copy(k_hbm.at[h], k_v, sem.at[0])
            cv = pltpu.make_async_copy(v_hbm.at[h], v_v, sem.at[1])
            ck.start()
            cv.start()
            _init_mask_bias(bias_s)
            dk_acc[...] = jnp.zeros(dk_acc.shape, F32)
            dv_acc[...] = jnp.zeros(dv_acc.shape, F32)
            ck.wait()
            cv.wait()

        gate, dy, of = g_ref[...], dy_ref[...], o_ref[...]
        silu, dsilu = _silu_and_grad(gate)
        do = dy * silu
        delta = jnp.sum(do * of, axis=-1, keepdims=True)
        dgate_ref[...] = _mx(dy * of * dsilu)
        dob = _mx(do)
        q = q_ref[...]
        delta_s[...] = jnp.broadcast_to(delta, delta_s.shape)
        dq_acc[...] = jnp.zeros(dq_acc.shape, F32)

        def keys(j):
            return pl.ds(pl.multiple_of(j * tq, tq), tq)

        def scores(s_ref, dp_ref, j):
            s = _dot_nt(q, k_v[keys(j), :]) * (SCALE * LOG2E) + bias_s[(j == i).astype(jnp.int32)]
            s_ref[...] = s - jnp.tile(lse_ref[...], (1, tq // 128))
            dp_ref[...] = _dot_nt(dob, v_v[keys(j), :]) - jnp.tile(delta_s[...], (1, tq // 128))

        def grads(s_ref, dp_ref, j):
            ks = keys(j)
            p = jnp.exp2(s_ref[...])
            ds = p * dp_ref[...] * SCALE
            pb, dsb = _mx(p), _mx(ds)
            dq_acc[...] += _dot(dsb, k_v[ks, :])
            dk_acc[ks, :] += _dot_tn(dsb, q)
            dv_acc[ks, :] += _dot_tn(pb, dob)

        scores(s_a, dp_a, 0)

        def pair(pp, carry):
            scores(s_b, dp_b, 2 * pp + 1)
            grads(s_a, dp_a, 2 * pp)
            scores(s_a, dp_a, jnp.minimum(2 * pp + 2, i))
            grads(s_b, dp_b, 2 * pp + 1)
            return carry

        _loop_in_long_trips((i + 1) // 2, pair)

        @pl.when(i % 2 == 0)
        def _():
            grads(s_a, dp_a, i)

        dq_ref[...] = dq_acc[...]

        @pl.when(i == nq - 1)
        def _():
            ck = pltpu.make_async_copy(dk_acc, dk_hbm.at[h], sem.at[0])
            cv = pltpu.make_async_copy(dv_acc, dv_hbm.at[h], sem.at[1])
            ck.start()
            cv.start()
            ck.wait()
            cv.wait()

    return pl.pallas_call(
        body, name=name, grid=(N_HEADS, nq),
        in_specs=[pl.BlockSpec((None, tq, QK_PAD), lambda h, i: (h, i, 0)),
                  pl.BlockSpec((tq, 128), lambda h, i: (i, h)),
                  pl.BlockSpec((None, tq, 128), lambda h, i: (h, i, 0)),
                  pl.BlockSpec((tq, 128), lambda h, i: (i, N_HEADS + h)), _gate_block(tq), ANY, ANY],
        out_specs=[pl.BlockSpec((None, tq, QK_PAD), lambda h, i: (h, i, 0)),
                   pl.BlockSpec((tq, 128), lambda h, i: (i, h)), ANY, ANY],
        out_shape=[jax.ShapeDtypeStruct((N_HEADS, t, QK_PAD), F32), jax.ShapeDtypeStruct((t, Z_GATE), MXU_DTYPE),
                   jax.ShapeDtypeStruct((N_HEADS, t, QK_PAD), F32), jax.ShapeDtypeStruct((N_HEADS, t, V_DIM), F32)],
        scratch_shapes=[pltpu.VMEM((t, QK_PAD), MXU_DTYPE), pltpu.VMEM((t, V_DIM), MXU_DTYPE),
                        pltpu.VMEM((t, QK_PAD), F32), pltpu.VMEM((t, V_DIM), F32), pltpu.VMEM((tq, QK_PAD), F32),
                        pltpu.VMEM((tq, 128), F32)] + [pltpu.VMEM((tq, tq), F32)] * 4
        + [pltpu.VMEM((2, tq, tq), F32), pltpu.SemaphoreType.DMA((2,))],
        compiler_params=_cp(2))(qh, o, lse, dycat, z, kh, vh)


def _out_proj_fwd(yab, yc, w, x, g, target, name, tm=512):
    t, d = x.shape
    tm = min(tm, t)
    is_last = target is not None

    def body(*refs):
        if is_last:
            yab_ref, yc_ref, w_ref, x_ref, g_ref, t_ref, y_ref, dout_ref, loss_ref = refs
            _zero_when(pl.program_id(0) == 0, loss_ref)
        else:
            yab_ref, yc_ref, w_ref, x_ref, g_ref, y_ref, out_ref = refs
        y = _dot(jnp.concatenate([yab_ref[...], yc_ref[...]], axis=1), w_ref[...])
        y_ref[...] = y
        out = x_ref[...] + _rms(y, g_ref[...])[0]
        if is_last:
            diff = out - t_ref[...]
            dout_ref[...] = diff * (1.0 / d)
            part = jnp.sum(jnp.sum(diff * diff, axis=-1, keepdims=True), axis=0, keepdims=True) * (0.5 / d)
            _acc(loss_ref, jnp.broadcast_to(part, (1, 128)))
        else:
            out_ref[...] = out

    row = lambda wd: pl.BlockSpec((tm, wd), lambda i: (i, 0))
    fixed = lambda a, b: pl.BlockSpec((a, b), lambda i: (0, 0))
    in_specs = [row(512), row(512), fixed(d, d), row(d), fixed(1, d)]
    args = [yab, yc, w, x, g]
    out_specs = [row(d), row(d)]
    out_shape = [jax.ShapeDtypeStruct((t, d), F32), jax.ShapeDtypeStruct((t, d), F32)]
    if is_last:
        in_specs.append(row(d))
        args.append(target)
        out_specs.append(fixed(1, 128))
        out_shape.append(jax.ShapeDtypeStruct((1, 128), F32))
    return pl.pallas_call(body, name=name, grid=(t // tm,), in_specs=in_specs, out_specs=out_specs,
                          out_shape=out_shape, compiler_params=_cp())(*args)


def _out_proj_bwd(dout, y, yab, yc, w, g, name, tm=512):
    t, d = y.shape
    tm = min(tm, t)

    def body(dout_ref, y_ref, yab_ref, yc_ref, w_ref, g_ref, dycat_ref, dw_ref, dg_ref):
        _zero_when(pl.program_id(0) == 0, dw_ref, dg_ref)
        y = y_ref[...]
        r = lax.rsqrt(jnp.mean(y * y, axis=-1, keepdims=True) + EPS)
        dy, dgt = _rms_bwd(y, r, g_ref[...], dout_ref[...])
        _acc(dg_ref, _colsum(dgt))
        dyb = _mx(dy)
        _acc(dw_ref, _dot_tn(jnp.concatenate([yab_ref[...], yc_ref[...]], axis=1), dyb))
        dycat_ref[...] = _dot_nt(dyb, w_ref[...])

    row = lambda wd: pl.BlockSpec((tm, wd), lambda i: (i, 0))
    fixed = lambda a, b: pl.BlockSpec((a, b), lambda i: (0, 0))
    return pl.pallas_call(
        body, name=name, grid=(t // tm,),
        in_specs=[row(d), row(d), row(512), row(512), fixed(d, d), fixed(1, d)],
        out_specs=[row(d), fixed(d, d), fixed(1, d)],
        out_shape=[jax.ShapeDtypeStruct((t, d), F32), jax.ShapeDtypeStruct((d, d), F32),
                   jax.ShapeDtypeStruct((1, d), F32)],
        compiler_params=_cp())(dout, y, yab, yc, w, g)


def _mesh_pos():
    return lax.axis_index("x"), lax.axis_index("y"), lax.axis_index("c")


def _remote(src, dst, send_sem, recv_sem, to):
    return pltpu.make_async_remote_copy(src_ref=src, dst_ref=dst, send_sem=send_sem, recv_sem=recv_sem,
                                        device_id=to, device_id_type=MESH)


CHUNK_ROWS = 256


def _pieces(rows):
    return [(s, min(CHUNK_ROWS, rows - s)) for s in range(0, rows, CHUNK_ROWS)]


def _piece_table(shapes):
    return [(a, s, sz) for a, shp in enumerate(shapes) for s, sz in _pieces(shp[-2])]


def _gather_weights(shards):
    n = len(shards)
    table = _piece_table([s.shape for s in shards])
    npc = len(table)

    def body(*refs):
        ins, outs = refs[:n], refs[n:2 * n]
        send_sems, recv_sems, fwd_send, fwd_recv = refs[2 * n:]
        x, y, c = _mesh_pos()
        me, sibling = (x, y, c), (x, y, 1 - c)
        chips = [(1 - x, y), (x, 1 - y), (1 - x, 1 - y)]
        slot = lambda cx, cy, layer: 2 * (2 * cx + cy) + layer
        first = []
        for a in range(n):
            for j, (cx, cy) in enumerate(chips):
                first.append(_remote(ins[a].at[c], outs[a].at[slot(x, y, c)], send_sems.at[a, j], recv_sems.at[a, j],
                                     (cx, cy, c)))
                first[-1].start()
        passed = []
        for j, (cx, cy) in enumerate(chips):
            for a in range(n):
                blk = outs[a].at[slot(cx, cy, c)]
                _remote(blk, blk, send_sems.at[a, j], recv_sems.at[a, j], me).wait_recv()
            for q, (a, s, sz) in enumerate(table):
                rows = outs[a].at[slot(cx, cy, c), pl.ds(s, sz)]
                passed.append(_remote(rows, rows, fwd_send.at[j, q], fwd_recv.at[j, q], sibling))
                passed[-1].start()
        for j, (cx, cy) in enumerate(chips):
            for q, (a, s, sz) in enumerate(table):
                rows = outs[a].at[slot(cx, cy, 1 - c), pl.ds(s, sz)]
                _remote(rows, rows, fwd_send.at[j, q], fwd_recv.at[j, q], me).wait_recv()
        for cp in first + passed:
            cp.wait_send()

    return pl.pallas_call(
        body, name="gather_weights", in_specs=[ANY] * n, out_specs=[ANY] * n,
        out_shape=[jax.ShapeDtypeStruct((8,) + s.shape[1:], s.dtype) for s in shards],
        scratch_shapes=[pltpu.SemaphoreType.DMA((n, 3)), pltpu.SemaphoreType.DMA((n, 3)),
                        pltpu.SemaphoreType.DMA((3, npc)), pltpu.SemaphoreType.DMA((3, npc))])(*shards)


def _pair_exchange(parts, common):
    n = len(parts)
    table = _piece_table([p.shape for p in parts] + [common.shape])
    npc = len(table)

    def body(*refs):
        ins, outs = refs[:n + 1], refs[n + 1:2 * n + 2]
        send_sems, recv_sems = refs[2 * n + 2:]
        x, y, c = _mesh_pos()
        sent = []
        for k in range(4):
            for q, (a, s, sz) in enumerate(table):
                if a == n and k > 0:
                    continue
                src = ins[a].at[2 * k + 1 - c, pl.ds(s, sz)] if a < n else ins[a].at[pl.ds(s, sz)]
                dst = outs[a].at[k, pl.ds(s, sz)] if a < n else outs[a].at[pl.ds(s, sz)]
                sent.append(_remote(src, dst, send_sems.at[k, q], recv_sems.at[k, q], (x, y, 1 - c)))
                sent[-1].start()
        for k in range(4):
            for q, (a, s, sz) in enumerate(table):
                if a == n and k > 0:
                    continue
                dst = outs[a].at[k, pl.ds(s, sz)] if a < n else outs[a].at[pl.ds(s, sz)]
                _remote(dst, dst, send_sems.at[k, q], recv_sems.at[k, q], (x, y, c)).wait_recv()
        for cp in sent:
            cp.wait_send()

    return pl.pallas_call(
        body, name="grad_pair_exchange", in_specs=[ANY] * (n + 1), out_specs=[ANY] * (n + 1),
        out_shape=[jax.ShapeDtypeStruct((4,) + p.shape[1:], p.dtype) for p in parts]
        + [jax.ShapeDtypeStruct(common.shape, common.dtype)],
        scratch_shapes=[pltpu.SemaphoreType.DMA((4, npc)), pltpu.SemaphoreType.DMA((4, npc))])(*parts, common)


def _chip_exchange(parts, common):
    n = len(parts)
    table = _piece_table([p.shape for p in parts] + [common.shape])
    npc = len(table)

    def body(*refs):
        ins, outs = refs[:n + 1], refs[n + 1:2 * n + 2]
        send_sems, recv_sems = refs[2 * n + 2:]
        x, y, c = _mesh_pos()
        mine = 2 * x + y
        chips = [(1 - x, y), (x, 1 - y), (1 - x, 1 - y)]
        src = lambda a, k: ins[a].at[k] if a < n else ins[a]
        sent = []
        for j, (cx, cy) in enumerate(chips):
            for q, (a, s, sz) in enumerate(table):
                sent.append(_remote(src(a, 2 * cx + cy).at[pl.ds(s, sz)], outs[a].at[mine, pl.ds(s, sz)],
                                    send_sems.at[j, q], recv_sems.at[j, q], (cx, cy, c)))
                sent[-1].start()
        for j, (cx, cy) in enumerate(chips):
            for q, (a, s, sz) in enumerate(table):
                dst = outs[a].at[2 * cx + cy, pl.ds(s, sz)]
                _remote(dst, dst, send_sems.at[j, q], recv_sems.at[j, q], (x, y, c)).wait_recv()
        for cp in sent:
            cp.wait_send()

    return pl.pallas_call(
        body, name="grad_chip_exchange", in_specs=[ANY] * (n + 1), out_specs=[ANY] * (n + 1),
        out_shape=[jax.ShapeDtypeStruct(p.shape, p.dtype) for p in parts]
        + [jax.ShapeDtypeStruct((4,) + common.shape, common.dtype)],
        scratch_shapes=[pltpu.SemaphoreType.DMA((3, npc)), pltpu.SemaphoreType.DMA((3, npc))])(*parts, common)


def _sibling_exchange(sums):
    n = len(sums)
    table = _piece_table([s.shape for s in sums])
    npc = len(table)

    def body(*refs):
        ins, outs = refs[:n], refs[n:2 * n]
        send_sems, recv_sems = refs[2 * n:]
        x, y, c = _mesh_pos()
        sent = []
        for q, (a, s, sz) in enumerate(table):
            sent.append(_remote(ins[a].at[pl.ds(s, sz)], outs[a].at[pl.ds(s, sz)], send_sems.at[q], recv_sems.at[q],
                                (x, y, 1 - c)))
            sent[-1].start()
        for q, (a, s, sz) in enumerate(table):
            dst = outs[a].at[pl.ds(s, sz)]
            _remote(dst, dst, send_sems.at[q], recv_sems.at[q], (x, y, c)).wait_recv()
        for cp in sent:
            cp.wait_send()

    return pl.pallas_call(
        body, name="sibling_exchange", in_specs=[ANY] * n, out_specs=[ANY] * n,
        out_shape=[jax.ShapeDtypeStruct(s.shape, s.dtype) for s in sums],
        scratch_shapes=[pltpu.SemaphoreType.DMA((npc,)), pltpu.SemaphoreType.DMA((npc,))])(*sums)


def _pair_sum(mine, got, name, tr):
    _, r, c = got.shape
    tr = min(tr, r)

    def body(p_ref, g_ref, o_ref, w_ref):
        total = p_ref[...] + g_ref[...]
        o_ref[...] = total
        w_ref[...] = total.astype(WIRE_DTYPE)

    blk = pl.BlockSpec((None, tr, c), lambda k, i: (k, i, 0))
    return pl.pallas_call(
        body, name=name, grid=(4, r // tr), in_specs=[blk, blk], out_specs=[blk, blk],
        out_shape=[jax.ShapeDtypeStruct(got.shape, F32), jax.ShapeDtypeStruct(got.shape, WIRE_DTYPE)],
        compiler_params=_cp(2))(mine, got)


def _sum_chips(own, recv, name, tr):
    r, c = own.shape
    tr = min(tr, r)

    def body(own_ref, r_ref, o_ref):
        chip = 2 * lax.axis_index("x") + lax.axis_index("y")
        own_blk = own_ref[...]
        acc = jnp.where(chip == 0, own_blk, r_ref[0].astype(F32))
        for k in range(1, 4):
            acc = acc + jnp.where(chip == k, own_blk, r_ref[k].astype(F32))
        o_ref[...] = acc

    return pl.pallas_call(
        body, name=name, grid=(r // tr,),
        in_specs=[pl.BlockSpec((tr, c), lambda i: (i, 0)), pl.BlockSpec((4, tr, c), lambda i: (0, i, 0))],
        out_specs=pl.BlockSpec((tr, c), lambda i: (i, 0)),
        out_shape=jax.ShapeDtypeStruct((r, c), F32), compiler_params=_cp())(own, recv)


def _sum_leading(parts, name, tr):
    nlead, r, c = parts.shape
    tr = min(tr, r)

    def body(p_ref, o_ref):
        acc = p_ref[0]
        for j in range(1, nlead):
            acc = acc + p_ref[j]
        o_ref[...] = acc

    return pl.pallas_call(
        body, name=name, grid=(r // tr,),
        in_specs=[pl.BlockSpec((nlead, tr, c), lambda i: (0, i, 0))], out_specs=pl.BlockSpec((tr, c), lambda i: (i, 0)),
        out_shape=jax.ShapeDtypeStruct((r, c), parts.dtype), compiler_params=_cp())(parts)


def _adamw(w, g, m, v, name, tr):
    r, c = w.shape
    tr = min(tr, r)

    def body(w_ref, g_ref, m_ref, v_ref, d_ref, nm_ref, nv_ref):
        gg = g_ref[...]
        nm = ADAM_B1 * m_ref[...] + (1.0 - ADAM_B1) * gg
        nv = ADAM_B2 * v_ref[...] + (1.0 - ADAM_B2) * jnp.square(gg)
        m_hat = nm / (1.0 - ADAM_B1 ** ADAM_STEP)
        v_hat = nv / (1.0 - ADAM_B2 ** ADAM_STEP)
        d_ref[...] = -ADAM_LR * (m_hat / (jnp.sqrt(v_hat) + ADAM_EPS) + ADAM_WD * w_ref[...])
        nm_ref[...] = nm
        nv_ref[...] = nv

    blk = pl.BlockSpec((tr, c), lambda i: (i, 0))
    return pl.pallas_call(
        body, name=name, grid=(r // tr,), in_specs=[blk] * 4, out_specs=[blk] * 3,
        out_shape=[jax.ShapeDtypeStruct((r, c), F32)] * 3, compiler_params=_cp())(w, g, m, v)


def _rope_tables(positions):
    inv_freq = ROPE_BASE ** (-jnp.arange(0, 64, 2, dtype=F32) / 64)
    ang = positions.astype(F32)[:, None] * inv_freq
    cos, sin = jnp.cos(ang), jnp.sin(ang)
    t = positions.shape[0]
    rc = jnp.concatenate([jnp.ones((t, 128), F32), cos, cos, jnp.ones((t, 64), F32)], axis=1)
    rs = jnp.concatenate([jnp.zeros((t, 128), F32), sin, sin, jnp.zeros((t, 64), F32)], axis=1)
    return rc, rs


def _layer_params(l, w_in, w_uq, w_ukv, w_out, small):
    p = {}
    p["w_in"] = jnp.concatenate([w_in[l][:, :1984], jnp.zeros((1024, 64), w_in.dtype), w_in[l][:, 1984:]], axis=1)
    p["w_uq"] = jnp.pad(w_uq[l].reshape(384, 4, 192), ((0, 0), (0, 0), (0, 64))).reshape(384, 1024)
    p["w_ukv"] = w_ukv[l].reshape(256, 4, 2, 128).transpose(0, 2, 1, 3).reshape(256, 1024)
    p["w_out"] = w_out[l]
    p["pre_g"] = small["pre_norm_g"][l][None]
    p["post_g"] = small["post_norm_g"][l][None]
    p["sgu_w"] = small["sgu_w"][l].reshape(512, 128)
    p["sgu_wt"] = small["sgu_w"][l].transpose(0, 2, 1).reshape(512, 128)
    p["sgu_bias"] = jnp.repeat(small["sgu_b"][l].T, 64, axis=1)
    p["ln_g"] = small["sgu_ln_g"][l][None]
    p["ln_b"] = small["sgu_ln_b"][l][None]
    p["pool_wbd"] = _mx(jax.scipy.linalg.block_diag(*[small["pool_w"][l][gi] for gi in range(4)]))
    p["pool_scale"] = small["pool_scale"][l][None]
    p["gq"] = small["q_norm_g"][l][None]
    p["gkv"] = small["kv_norm_g"][l][None]
    return p


def _layer_fwd(l, x, p, rc, rs, target):
    z, h = _in_proj_fwd(x, p["pre_g"], p["w_in"], f"in_proj_fwd_{l}")
    yab = _mix_fwd(z, p["sgu_w"], p["sgu_bias"], p["ln_g"], p["ln_b"], p["pool_wbd"], p["pool_scale"], f"mix_fwd_{l}")
    qh, kh, vh = _qkv_fwd(z, rc, rs, p["w_uq"], p["w_ukv"], p["gq"], p["gkv"], f"qkv_fwd_{l}")
    o, yc, lse = _attn_fwd(qh, kh, vh, z, f"attn_fwd_{l}")
    outs = _out_proj_fwd(yab, yc, p["w_out"], x, p["post_g"], target, f"out_proj_fwd_{l}")
    saved = dict(x=x, z=z, h=h, yab=yab, qh=qh, kh=kh, vh=vh, o=o, yc=yc, lse=lse, y=outs[0])
    return saved, outs[1:]


def _layer_bwd(l, dout, sv, p, rc, rs):
    dycat, dw_out, dpost = _out_proj_bwd(dout, sv["y"], sv["yab"], sv["yc"], p["w_out"], p["post_g"], f"out_proj_bwd_{l}")
    dq, dgate, dk, dv = _attn_bwd(sv["qh"], sv["kh"], sv["vh"], sv["o"], sv["lse"], dycat, sv["z"], f"attn_bwd_{l}")
    dzc, dzk, dwq, dwkv, dgq, dgkv = _qkv_bwd(dq, dk, dv, sv["z"], rc, rs, p["w_uq"], p["w_ukv"], p["gq"], p["gkv"],
                                              f"qkv_bwd_{l}")
    dzm, dsw, dsb, dlng, dlnb, dpw, dps = _mix_bwd(sv["z"], dycat, p["sgu_w"], p["sgu_wt"], p["sgu_bias"], p["ln_g"],
                                                   p["ln_b"], p["pool_wbd"], p["pool_scale"], f"mix_bwd_{l}")
    dx, dw_in, dpre = _in_proj_bwd(dzm, dzc, dzk, dgate, sv["h"], sv["x"], dout, p["w_in"], p["pre_g"], f"in_proj_bwd_{l}")
    grads = {
        "pre_norm_g": dpre[0], "post_norm_g": dpost[0],
        "w_in": jnp.concatenate([dw_in[:, :1984], dw_in[:, 2048:]], axis=1),
        "sgu_w": dsw.reshape(4, 128, 128), "sgu_b": dsb[:, :4].T, "sgu_ln_g": dlng[0], "sgu_ln_b": dlnb[0],
        "pool_w": jnp.stack([dpw[64 * gi:64 * gi + 64, 64 * gi:64 * gi + 64] for gi in range(4)]),
        "pool_scale": dps[0], "q_norm_g": dgq[0],
        "w_uq": dwq.reshape(384, 4, 256)[:, :, :192].reshape(384, 768), "kv_norm_g": dgkv[0],
        "w_ukv": dwkv.reshape(256, 2, 4, 128).transpose(0, 2, 1, 3).reshape(256, 1024), "w_out": dw_out,
    }
    return dx, grads


SMALL_NAMES = ["pre_norm_g", "post_norm_g", "sgu_w", "sgu_b", "sgu_ln_g", "sgu_ln_b", "pool_w", "pool_scale",
               "q_norm_g", "kv_norm_g"]
BIG_NAMES = ["w_in", "w_uq", "w_ukv", "w_out"]
WEIGHT_NAMES = ["pre_norm_g", "post_norm_g", "w_in", "sgu_w", "sgu_b", "sgu_ln_g", "sgu_ln_b", "pool_w", "pool_scale",
                "q_norm_g", "w_uq", "kv_norm_g", "w_ukv", "w_out"]


def _local_step(x, positions, target, w_in, w_uq, w_ukv, w_out, small):
    rc, rs = _rope_tables(positions)
    params = [_layer_params(l, w_in, w_uq, w_ukv, w_out, small) for l in range(DEPTH)]
    saved = []
    for l in range(DEPTH):
        sv, outs = _layer_fwd(l, x, params[l], rc, rs, target if l == DEPTH - 1 else None)
        saved.append(sv)
        if l < DEPTH - 1:
            x = outs[0]
    dout, loss = outs
    grads = [None] * DEPTH
    for l in reversed(range(DEPTH)):
        dout, grads[l] = _layer_bwd(l, dout, saved[l], params[l], rc, rs)
    return loss[0, 0], dout, {k: jnp.stack([grads[l][k] for l in range(DEPTH)]) for k in WEIGHT_NAMES}


def _pack_small(tree, extra=None):
    pieces = [tree[k].reshape(-1) for k in SMALL_NAMES]
    pieces.append(jnp.zeros((1,), F32) if extra is None else extra.reshape(1))
    flat = jnp.concatenate(pieces)
    rows = -(-flat.shape[0] // 1024) * 8
    return jnp.pad(flat, (0, rows * 128 - flat.shape[0])).reshape(rows, 128)


def _unpack_small(packed, like):
    flat = packed.reshape(-1)
    out, off = {}, 0
    for k in SMALL_NAMES:
        size = like[k].size
        out[k] = flat[off:off + size].reshape(like[k].shape)
        off += size
    return out, flat[off]


def kernel(x, positions, pre_norm_g, post_norm_g, w_in, sgu_w, sgu_b, sgu_ln_g, sgu_ln_b, pool_w, pool_scale, q_norm_g, w_uq, kv_norm_g, w_ukv, w_out, loss_target, m_pre_norm_g, m_post_norm_g, m_w_in, m_sgu_w, m_sgu_b, m_sgu_ln_g, m_sgu_ln_b, m_pool_w, m_pool_scale, m_q_norm_g, m_w_uq, m_kv_norm_g, m_w_ukv, m_w_out, v_pre_norm_g, v_post_norm_g, v_w_in, v_sgu_w, v_sgu_b, v_sgu_ln_g, v_sgu_ln_b, v_pool_w, v_pool_scale, v_q_norm_g, v_w_uq, v_kv_norm_g, v_w_ukv, v_w_out):
    w = dict(pre_norm_g=pre_norm_g, post_norm_g=post_norm_g, w_in=w_in, sgu_w=sgu_w, sgu_b=sgu_b, sgu_ln_g=sgu_ln_g,
             sgu_ln_b=sgu_ln_b, pool_w=pool_w, pool_scale=pool_scale, q_norm_g=q_norm_g, w_uq=w_uq, kv_norm_g=kv_norm_g,
             w_ukv=w_ukv, w_out=w_out)
    m = dict(pre_norm_g=m_pre_norm_g, post_norm_g=m_post_norm_g, w_in=m_w_in, sgu_w=m_sgu_w, sgu_b=m_sgu_b,
             sgu_ln_g=m_sgu_ln_g, sgu_ln_b=m_sgu_ln_b, pool_w=m_pool_w, pool_scale=m_pool_scale, q_norm_g=m_q_norm_g,
             w_uq=m_w_uq, kv_norm_g=m_kv_norm_g, w_ukv=m_w_ukv, w_out=m_w_out)
    v = dict(pre_norm_g=v_pre_norm_g, post_norm_g=v_post_norm_g, w_in=v_w_in, sgu_w=v_sgu_w, sgu_b=v_sgu_b,
             sgu_ln_g=v_sgu_ln_g, sgu_ln_b=v_sgu_ln_b, pool_w=v_pool_w, pool_scale=v_pool_scale, q_norm_g=v_q_norm_g,
             w_uq=v_w_uq, kv_norm_g=v_kv_norm_g, w_ukv=v_w_ukv, w_out=v_w_out)

    core = lax.axis_index("c")
    chip = 2 * lax.axis_index("x") + lax.axis_index("y")
    shards = [_mx(w[k]) for k in BIG_NAMES]
    gathered = _gather_weights(shards)
    g_in, g_uq, g_ukv, g_out = [lax.dynamic_update_slice(g, s, (2 * chip, 0, 0)) for g, s in zip(gathered, shards)]
    cols = lambda g: g.reshape((4, 2) + g.shape[1:]).transpose(1, 2, 0, 3).reshape(2, g.shape[1], 4 * g.shape[2])
    full_out = g_out.reshape(4, 2, 256, 1024).transpose(1, 0, 2, 3).reshape(2, 1024, 1024)
    loss, dx, grads = _local_step(x[0], positions[0], loss_target[0], cols(g_in), cols(g_uq), cols(g_ukv), full_out, w)

    split_cols = lambda g: g.reshape(2, g.shape[1], 4, g.shape[2] // 4).transpose(2, 0, 1, 3).reshape(8, g.shape[1], g.shape[2] // 4)
    parts = [split_cols(grads["w_in"]), split_cols(grads["w_uq"]), split_cols(grads["w_ukv"]),
             grads["w_out"].reshape(2, 4, 256, 1024).transpose(1, 0, 2, 3).reshape(8, 256, 1024)]
    common = _pack_small(grads, loss)
    got = _pair_exchange(parts, common)
    mine = [lax.dynamic_index_in_dim(p.reshape((4, 2) + p.shape[1:]), core, axis=1, keepdims=False) for p in parts]
    pair_sums = [_pair_sum(mine[a], got[a], f"pair_sum_{BIG_NAMES[a]}", 128) for a in range(4)]
    chip_common = _sum_leading(jnp.stack([common, got[4]]), "pair_sum_small", common.shape[0])
    received = _chip_exchange([ps[1] for ps in pair_sums], chip_common)
    sums = [_sum_chips(lax.dynamic_index_in_dim(pair_sums[a][0], chip, axis=0, keepdims=False), received[a],
                       f"sum_{BIG_NAMES[a]}", 128) for a in range(4)]
    all_common = lax.dynamic_update_slice(received[4], chip_common[None], (chip, 0, 0))
    small_sum, loss = _unpack_small(_sum_leading(all_common, "sum_small", all_common.shape[1]), w)
    others = _sibling_exchange(sums)
    total = dict(small_sum)
    for a, k in enumerate(BIG_NAMES):
        total[k] = jnp.where(core == 0, jnp.stack([sums[a], others[a]]), jnp.stack([others[a], sums[a]]))

    packed = _adamw(_pack_small(w), _pack_small(total), _pack_small(m), _pack_small(v), "adamw_small", 2048)
    small_out = [_unpack_small(pk, w)[0] for pk in packed]
    delta, new_m, new_v = {}, {}, {}
    for k in SMALL_NAMES:
        delta[k], new_m[k], new_v[k] = (so[k] for so in small_out)
    for k in BIG_NAMES:
        shape = w[k].shape
        flat = lambda a: a.reshape(shape[0] * shape[1], shape[2])
        res = _adamw(flat(w[k]), flat(total[k]), flat(m[k]), flat(v[k]), f"adamw_{k}", 256)
        delta[k], new_m[k], new_v[k] = (r.reshape(shape) for r in res)

    return (loss, dx[None], *[total[k] for k in WEIGHT_NAMES], *[delta[k] for k in WEIGHT_NAMES],
            *[new_m[k] for k in WEIGHT_NAMES], *[new_v[k] for k in WEIGHT_NAMES])
```

```python
import jax
import jax.numpy as jnp
from jax import lax
from jax.experimental import pallas as pl
from jax.experimental.pallas import tpu as pltpu

F32 = jnp.float32
MXU_DTYPE = jnp.bfloat16
WIRE_DTYPE = jnp.bfloat16
EPS = 1e-6
NEG_INF = -1e30
DEPTH = 2
N_HEADS = 4
QK_PAD = 256
V_DIM = 128
SCALE = 192 ** -0.5
LOG2E = 1.4426950408889634
ROPE_BASE = 10000.0
ADAM_LR, ADAM_B1, ADAM_B2, ADAM_EPS, ADAM_WD, ADAM_STEP = 0.001, 0.9, 0.999, 1e-08, 0.01, 10
VMEM_LIMIT_BYTES = 56 * 1024 * 1024
MESH = pl.DeviceIdType.MESH
ANY = pl.BlockSpec(memory_space=pl.ANY)

Z_MIX, Z_C, Z_KR, Z_GATE = 1280, 640, 128, 512
Z_W = Z_MIX + Z_C + Z_KR + Z_GATE


def _cp(n_axes=1):
    return pltpu.CompilerParams(dimension_semantics=("arbitrary",) * n_axes, vmem_limit_bytes=VMEM_LIMIT_BYTES)


def _dot(a, b):
    return lax.dot_general(a, b, (((1,), (0,)), ((), ())), preferred_element_type=F32)


def _dot_nt(a, b):
    return lax.dot_general(a, b, (((1,), (1,)), ((), ())), preferred_element_type=F32)


def _dot_tn(a, b):
    return lax.dot_general(a, b, (((0,), (0,)), ((), ())), preferred_element_type=F32)


def _mx(a):
    return a.astype(MXU_DTYPE)


def _silu_and_grad(g):
    sg = jax.nn.sigmoid(g)
    return g * sg, sg * (1.0 + g * (1.0 - sg))


def _rms(x, g):
    r = lax.rsqrt(jnp.mean(x * x, axis=-1, keepdims=True) + EPS)
    return x * r * g, r


def _rms_bwd(x, r, g, dy):
    xhat = x * r
    dyg = dy * g
    dx = r * (dyg - xhat * jnp.mean(dyg * xhat, axis=-1, keepdims=True))
    return dx, dy * xhat


def _zero_when(first, *refs):
    @pl.when(first)
    def _():
        for ref in refs:
            ref[...] = jnp.zeros(ref.shape, ref.dtype)


def _acc(ref, val):
    ref[...] += val


def _colsum(a):
    return jnp.sum(a, axis=0, keepdims=True)


def _in_proj_fwd(x, g, w, name, tm=512):
    t, d = x.shape
    n = w.shape[1]
    tm = min(tm, t)

    def body(x_ref, g_ref, w_ref, z_ref, h_ref):
        h, _ = _rms(x_ref[...], g_ref[...])
        h = _mx(h)
        h_ref[...] = h
        z_ref[...] = _dot(h, w_ref[...])

    return pl.pallas_call(
        body, name=name, grid=(t // tm,),
        in_specs=[pl.BlockSpec((tm, d), lambda i: (i, 0)), pl.BlockSpec((1, d), lambda i: (0, 0)),
                  pl.BlockSpec((d, n), lambda i: (0, 0))],
        out_specs=[pl.BlockSpec((tm, n), lambda i: (i, 0)), pl.BlockSpec((tm, d), lambda i: (i, 0))],
        out_shape=[jax.ShapeDtypeStruct((t, n), F32), jax.ShapeDtypeStruct((t, d), MXU_DTYPE)],
        compiler_params=_cp())(x, g, w)


def _in_proj_bwd(dz_mix, dz_c, dz_kr, dz_gate, h, x, d_res, w, g, name, tm=512):
    t, d = x.shape
    n = w.shape[1]
    tm = min(tm, t)

    def body(dm_ref, dc_ref, dk_ref, dg_ref, h_ref, x_ref, dres_ref, w_ref, g_ref, dx_ref, dw_ref, dgn_ref):
        first = pl.program_id(0) == 0
        dz = jnp.concatenate([dm_ref[...], dc_ref[...], dk_ref[...], dg_ref[...]], axis=1)

        _zero_when(first, dw_ref, dgn_ref)
        hb = h_ref[...]
        for c0 in range(0, n, 512):
            dw_ref[:, c0:c0 + 512] += _dot_tn(hb, dz[:, c0:c0 + 512])
        dh = _dot_nt(dz, w_ref[...])
        xf = x_ref[...]
        r = lax.rsqrt(jnp.mean(xf * xf, axis=-1, keepdims=True) + EPS)
        dx, dgt = _rms_bwd(xf, r, g_ref[...], dh)
        dx_ref[...] = dx + dres_ref[...]
        _acc(dgn_ref, _colsum(dgt))

    row = lambda wd: pl.BlockSpec((tm, wd), lambda i: (i, 0))
    fixed = lambda a, b: pl.BlockSpec((a, b), lambda i: (0, 0), pipeline_mode=pl.Buffered(1))
    return pl.pallas_call(
        body, name=name, grid=(t // tm,),
        in_specs=[row(Z_MIX), row(Z_C), row(Z_KR), row(Z_GATE), row(d), row(d), row(d), fixed(d, n), fixed(1, d)],
        out_specs=[row(d), fixed(d, n), fixed(1, d)],
        out_shape=[jax.ShapeDtypeStruct((t, d), F32), jax.ShapeDtypeStruct((d, n), F32),
                   jax.ShapeDtypeStruct((1, d), F32)],
        compiler_params=_cp())(dz_mix, dz_c, dz_kr, dz_gate, h, x, d_res, w, g)


def _lane_group(shape):
    return lax.broadcasted_iota(jnp.int32, shape, 1) // 64


def _select_group(vals):
    grp = _lane_group(vals[0].shape)
    out = vals[3]
    for gi in (2, 1, 0):
        out = jnp.where(grp == gi, vals[gi], out)
    return out


def _sgu_mask(transposed):
    r = (lax.broadcasted_iota(jnp.int32, (512, 128), 0) % 128) // 64
    c = lax.broadcasted_iota(jnp.int32, (512, 128), 1) // 64
    return (r <= c) if transposed else (c <= r)


def _sgu_apply(wstack, vb, nblk):
    outs = []
    for n in range(nblk):
        r = _dot(wstack, vb[n * 128:(n + 1) * 128, :])
        outs.append(_select_group([r[hh * 128:(hh + 1) * 128, :] for hh in range(4)]))
    return jnp.concatenate(outs, axis=0)


def _layer_norm(v, g, b):
    mu = jnp.mean(v, axis=-1, keepdims=True)
    vc = v - mu
    rstd = lax.rsqrt(jnp.mean(vc * vc, axis=-1, keepdims=True) + EPS)
    vhat = vc * rstd
    return vhat * g + b, vhat, rstd


def _pool_counts(t0, n):
    t = t0 + lax.broadcasted_iota(jnp.int32, (n, 256), 0)
    w = _select_group([jnp.full((n, 256), wv, jnp.int32) for wv in (2, 4, 8, 16)])
    return jnp.minimum(t + 1, w).astype(F32)


def _pooled(p, halo, t0):
    tm = p.shape[0]
    ext = jnp.concatenate([halo, p], axis=0)
    s2 = ext + pltpu.roll(ext, 1, 0)
    s4 = s2 + pltpu.roll(s2, 2, 0)
    s8 = s4 + pltpu.roll(s4, 4, 0)
    s16 = s8 + pltpu.roll(s8, 8, 0)
    sel = _select_group([s2, s4, s8, s16])[16:, :]
    return sel / _pool_counts(t0, tm) - p


def _pooled_bwd(dpool, dpool_halo, t0):
    tm = dpool.shape[0]
    n = tm + 16
    ext = jnp.concatenate([dpool, dpool_halo], axis=0) / _pool_counts(t0, n)
    f2 = ext + pltpu.roll(ext, n - 1, 0)
    f4 = f2 + pltpu.roll(f2, n - 2, 0)
    f8 = f4 + pltpu.roll(f4, n - 4, 0)
    f16 = f8 + pltpu.roll(f8, n - 8, 0)
    return _select_group([f2, f4, f8, f16])[:tm, :] - dpool


def _mix_specs(t, tm):
    nt16 = t // 16
    zrow = pl.BlockSpec((tm, Z_MIX), lambda i: (i, 0))
    prev_halo = pl.BlockSpec((16, 256), lambda i: (jnp.maximum(i * (tm // 16) - 1, 0), 3))
    fixed = lambda a, b: pl.BlockSpec((a, b), lambda i: (0, 0))
    params = [fixed(512, 128), fixed(128, 256), fixed(1, 256), fixed(1, 256), fixed(256, 256), fixed(1, 256)]
    return nt16, zrow, prev_halo, fixed, params


def _mix_fwd(z, sgu_w, sgu_bias, ln_g, ln_b, pool_wbd, pool_scale, name, tm=512):
    t = z.shape[0]
    tm = min(tm, t)
    _, zrow, prev_halo, _, params = _mix_specs(t, tm)

    def body(z_ref, halo_ref, w_ref, bias_ref, lng_ref, lnb_ref, pw_ref, ps_ref, y_ref):
        i = pl.program_id(0)
        u, v, gate = z_ref[:, 0:256], z_ref[:, 256:512], z_ref[:, 512:768]
        p, pgate = z_ref[:, 768:1024], z_ref[:, 1024:1280]
        vn, _, _ = _layer_norm(v, lng_ref[...], lnb_ref[...])
        wm = _mx(jnp.where(_sgu_mask(False), w_ref[...], 0.0))
        mixed = _sgu_apply(wm, _mx(vn), tm // 128) + jnp.tile(bias_ref[...], (tm // 128, 1))
        ya = u * mixed * _silu_and_grad(gate)[0]
        halo = jnp.where(i > 0, halo_ref[...], 0.0)
        pooled = _pooled(p, halo, i * tm)
        yb = _dot(_mx(pooled), pw_ref[...]) * ps_ref[...] * _silu_and_grad(pgate)[0]
        y_ref[...] = _mx(jnp.concatenate([ya, yb], axis=1))

    return pl.pallas_call(
        body, name=name, grid=(t // tm,),
        in_specs=[zrow, prev_halo] + params,
        out_specs=pl.BlockSpec((tm, 512), lambda i: (i, 0)),
        out_shape=jax.ShapeDtypeStruct((t, 512), MXU_DTYPE),
        compiler_params=_cp())(z, z, sgu_w, sgu_bias, ln_g, ln_b, pool_wbd, pool_scale)


def _mix_bwd(z, dycat, sgu_w, sgu_wt, sgu_bias, ln_g, ln_b, pool_wbd, pool_scale, name, tm=512):
    t = z.shape[0]
    tm = min(tm, t)
    nt16, zrow, prev_halo, fixed, params = _mix_specs(t, tm)
    nblk = tm // 128
    last = t // tm - 1

    def body(z_ref, halo_ref, zn_ref, dy_ref, dyn_ref, w_ref, wt_ref, bias_ref, lng_ref, lnb_ref, pw_ref, ps_ref,
             dz_ref, dw_ref, db_ref, dlng_ref, dlnb_ref, dpw_ref, dps_ref):
        i = pl.program_id(0)
        _zero_when(i == 0, dw_ref, db_ref, dlng_ref, dlnb_ref, dpw_ref, dps_ref)
        u, v, gate = z_ref[:, 0:256], z_ref[:, 256:512], z_ref[:, 512:768]
        p, pgate = z_ref[:, 768:1024], z_ref[:, 1024:1280]
        dya, dyb = dy_ref[:, 0:256], dy_ref[:, 256:512]
        vn, vhat, rstd = _layer_norm(v, lng_ref[...], lnb_ref[...])
        vnb = _mx(vn)
        wm = _mx(jnp.where(_sgu_mask(False), w_ref[...], 0.0))
        wmt = _mx(jnp.where(_sgu_mask(True), wt_ref[...], 0.0))
        mixed = _sgu_apply(wm, vnb, nblk) + jnp.tile(bias_ref[...], (nblk, 1))
        silu, dsilu = _silu_and_grad(gate)
        t1 = u * mixed
        d_gate = dya * t1 * dsilu
        d_t1 = dya * silu
        d_u = d_t1 * mixed
        d_mixed = d_t1 * u
        dmb = _mx(d_mixed)
        d_vn = _sgu_apply(wmt, dmb, nblk)
        grp = _lane_group((128, 256))
        lane = lax.broadcasted_iota(jnp.int32, (128, 128), 1)
        dws = [jnp.zeros((128, 128), F32) for _ in range(4)]
        dbias = jnp.zeros((128, 128), F32)
        for n in range(nblk):
            dm_n, dmb_n, vnb_n = d_mixed[n * 128:(n + 1) * 128], dmb[n * 128:(n + 1) * 128], vnb[n * 128:(n + 1) * 128]
            for hh in range(4):
                dws[hh] = dws[hh] + _dot_nt(jnp.where(grp == hh, dmb_n, jnp.zeros_like(dmb_n)), vnb_n)
                rs = jnp.sum(jnp.where(grp == hh, dm_n, 0.0), axis=-1, keepdims=True)
                dbias = dbias + jnp.where(lane == hh, rs, 0.0)
        _acc(dw_ref, jnp.concatenate(dws, axis=0))
        _acc(db_ref, dbias)
        _acc(dlng_ref, _colsum(d_vn * vhat))
        _acc(dlnb_ref, _colsum(d_vn))
        dvh = d_vn * lng_ref[...]
        d_v = rstd * (dvh - jnp.mean(dvh, axis=-1, keepdims=True) - vhat * jnp.mean(dvh * vhat, axis=-1, keepdims=True))

        @pl.when(i == last)
        def _():
            dw_ref[...] = jnp.where(_sgu_mask(False), dw_ref[...], 0.0)

        halo = jnp.where(i > 0, halo_ref[...], 0.0)
        pooled = _pooled(p, halo, i * tm)
        pooled_b = _mx(pooled)
        mixedp = _dot(pooled_b, pw_ref[...])
        psilu, pdsilu = _silu_and_grad(pgate)
        d_pgate = dyb * (mixedp * ps_ref[...]) * pdsilu
        d_ms = dyb * psilu
        _acc(dps_ref, _colsum(d_ms * mixedp))
        dmpb = _mx(d_ms * ps_ref[...])
        _acc(dpw_ref, _dot_tn(pooled_b, dmpb))
        d_pooled = _dot_nt(dmpb, pw_ref[...])
        dmp_halo = _mx(dyn_ref[...] * _silu_and_grad(zn_ref[...])[0] * ps_ref[...])
        d_pooled_halo = jnp.where(i < last, _dot_nt(dmp_halo, pw_ref[...]), 0.0)
        d_p = _pooled_bwd(d_pooled, d_pooled_halo, i * tm)
        dz_ref[...] = _mx(jnp.concatenate([d_u, d_v, d_gate, d_p, d_pgate], axis=1))

    nxt = lambda i: jnp.minimum((i + 1) * (tm // 16), nt16 - 1)
    return pl.pallas_call(
        body, name=name, grid=(t // tm,),
        in_specs=[zrow, prev_halo, pl.BlockSpec((16, 256), lambda i: (nxt(i), 4)),
                  pl.BlockSpec((tm, 512), lambda i: (i, 0)), pl.BlockSpec((16, 256), lambda i: (nxt(i), 1)),
                  params[0], fixed(512, 128)] + params[1:],
        out_specs=[pl.BlockSpec((tm, Z_MIX), lambda i: (i, 0)), fixed(512, 128), fixed(128, 128), fixed(1, 256),
                   fixed(1, 256), fixed(256, 256), fixed(1, 256)],
        out_shape=[jax.ShapeDtypeStruct((t, Z_MIX), MXU_DTYPE), jax.ShapeDtypeStruct((512, 128), F32),
                   jax.ShapeDtypeStruct((128, 128), F32), jax.ShapeDtypeStruct((1, 256), F32),
                   jax.ShapeDtypeStruct((1, 256), F32), jax.ShapeDtypeStruct((256, 256), F32),
                   jax.ShapeDtypeStruct((1, 256), F32)],
        compiler_params=_cp())(z, z, z, dycat, dycat, sgu_w, sgu_wt, sgu_bias, ln_g, ln_b, pool_wbd, pool_scale)


def _rot_half(x, transpose):
    w = x.shape[1]
    lane = lax.broadcasted_iota(jnp.int32, x.shape, 1) % min(w, 256)
    base = 128 if w >= 256 else 0
    lo = jnp.logical_and(lane >= base, lane < base + 32)
    hi = jnp.logical_and(lane >= base + 32, lane < base + 64)
    up = pltpu.roll(x, w - 32, 1)
    down = pltpu.roll(x, 32, 1)
    if transpose:
        return jnp.where(lo, up, jnp.where(hi, -down, 0.0))
    return jnp.where(lo, -up, jnp.where(hi, down, 0.0))


def _rope(x, c, s):
    return x * c + _rot_half(x, False) * s


def _rope_bwd(dy, c, s):
    return dy * c + _rot_half(dy * s, True)


def _qkv_fwd(z, rc, rs, w_uq, w_ukv, gq, gkv, name, tm=512):
    t = z.shape[0]
    tm = min(tm, t)

    def body(zc_ref, zk_ref, rc_ref, rs_ref, wq_ref, wkv_ref, gq_ref, gkv_ref, q_ref, k_ref, v_ref):
        cq, ckv = zc_ref[:, 0:384], zc_ref[:, 384:640]
        c, s = rc_ref[...], rs_ref[...]
        qn, _ = _rms(cq, gq_ref[...])
        q = _rope(_dot(_mx(qn), wq_ref[...]), jnp.tile(c, (1, N_HEADS)), jnp.tile(s, (1, N_HEADS)))
        kvn, _ = _rms(ckv, gkv_ref[...])
        kv = _dot(_mx(kvn), wkv_ref[...])
        kpe = _rope(zk_ref[...], c[:, 128:256], s[:, 128:256])
        for hh in range(N_HEADS):
            q_ref[hh] = _mx(q[:, hh * QK_PAD:(hh + 1) * QK_PAD])
            k_ref[hh] = _mx(jnp.concatenate([kv[:, hh * 128:(hh + 1) * 128], kpe], axis=1))
            v_ref[hh] = _mx(kv[:, 512 + hh * 128:512 + (hh + 1) * 128])

    fixed = lambda a, b: pl.BlockSpec((a, b), lambda i: (0, 0))
    heads = lambda wd: pl.BlockSpec((N_HEADS, tm, wd), lambda i: (0, i, 0))
    return pl.pallas_call(
        body, name=name, grid=(t // tm,),
        in_specs=[pl.BlockSpec((tm, Z_C), lambda i: (i, Z_MIX // Z_C)),
                  pl.BlockSpec((tm, Z_KR), lambda i: (i, (Z_MIX + Z_C) // Z_KR)),
                  pl.BlockSpec((tm, 256), lambda i: (i, 0)), pl.BlockSpec((tm, 256), lambda i: (i, 0)),
                  fixed(384, 1024), fixed(256, 1024), fixed(1, 384), fixed(1, 256)],
        out_specs=[heads(QK_PAD), heads(QK_PAD), heads(V_DIM)],
        out_shape=[jax.ShapeDtypeStruct((N_HEADS, t, QK_PAD), MXU_DTYPE),
                   jax.ShapeDtypeStruct((N_HEADS, t, QK_PAD), MXU_DTYPE),
                   jax.ShapeDtypeStruct((N_HEADS, t, V_DIM), MXU_DTYPE)],
        compiler_params=_cp())(z, z, rc, rs, w_uq, w_ukv, gq, gkv)


def _qkv_bwd(dq, dk, dv, z, rc, rs, w_uq, w_ukv, gq, gkv, name, tm=512):
    t = z.shape[0]
    tm = min(tm, t)

    def body(dq_ref, dk_ref, dv_ref, zc_ref, rc_ref, rs_ref, wq_ref, wkv_ref, gq_ref, gkv_ref,
             dzc_ref, dzk_ref, dwq_ref, dwkv_ref, dgq_ref, dgkv_ref):
        _zero_when(pl.program_id(0) == 0, dwq_ref, dwkv_ref, dgq_ref, dgkv_ref)
        cq, ckv = zc_ref[:, 0:384], zc_ref[:, 384:640]
        c, s = rc_ref[...], rs_ref[...]
        dq_all = jnp.concatenate([dq_ref[hh] for hh in range(N_HEADS)], axis=1)
        dqp = _mx(_rope_bwd(dq_all, jnp.tile(c, (1, N_HEADS)), jnp.tile(s, (1, N_HEADS))))
        qn, rq = _rms(cq, gq_ref[...])
        _acc(dwq_ref, _dot_tn(_mx(qn), dqp))
        d_cq, dgq_t = _rms_bwd(cq, rq, gq_ref[...], _dot_nt(dqp, wq_ref[...]))
        _acc(dgq_ref, _colsum(dgq_t))
        dkpe = dk_ref[0][:, 128:256]
        for hh in range(1, N_HEADS):
            dkpe = dkpe + dk_ref[hh][:, 128:256]
        dzk_ref[...] = _mx(_rope_bwd(dkpe, c[:, 128:256], s[:, 128:256]))
        dkv = _mx(jnp.concatenate([dk_ref[hh][:, 0:128] for hh in range(N_HEADS)]
                                  + [dv_ref[hh] for hh in range(N_HEADS)], axis=1))
        kvn, rkv = _rms(ckv, gkv_ref[...])
        _acc(dwkv_ref, _dot_tn(_mx(kvn), dkv))
        d_ckv, dgkv_t = _rms_bwd(ckv, rkv, gkv_ref[...], _dot_nt(dkv, wkv_ref[...]))
        _acc(dgkv_ref, _colsum(dgkv_t))
        dzc_ref[...] = _mx(jnp.concatenate([d_cq, d_ckv], axis=1))

    fixed = lambda a, b: pl.BlockSpec((a, b), lambda i: (0, 0))
    heads = lambda wd: pl.BlockSpec((N_HEADS, tm, wd), lambda i: (0, i, 0))
    return pl.pallas_call(
        body, name=name, grid=(t // tm,),
        in_specs=[heads(QK_PAD), heads(QK_PAD), heads(V_DIM), pl.BlockSpec((tm, Z_C), lambda i: (i, Z_MIX // Z_C)),
                  pl.BlockSpec((tm, 256), lambda i: (i, 0)), pl.BlockSpec((tm, 256), lambda i: (i, 0)),
                  fixed(384, 1024), fixed(256, 1024), fixed(1, 384), fixed(1, 256)],
        out_specs=[pl.BlockSpec((tm, Z_C), lambda i: (i, 0)), pl.BlockSpec((tm, Z_KR), lambda i: (i, 0)),
                   fixed(384, 1024), fixed(256, 1024), fixed(1, 384), fixed(1, 256)],
        out_shape=[jax.ShapeDtypeStruct((t, Z_C), MXU_DTYPE), jax.ShapeDtypeStruct((t, Z_KR), MXU_DTYPE),
                   jax.ShapeDtypeStruct((384, 1024), F32), jax.ShapeDtypeStruct((256, 1024), F32),
                   jax.ShapeDtypeStruct((1, 384), F32), jax.ShapeDtypeStruct((1, 256), F32)],
        compiler_params=_cp())(dq, dk, dv, z, rc, rs, w_uq, w_ukv, gq, gkv)


def _loop_in_long_trips(n, body):
    def two(t, carry):
        return body(2 * t + 1, body(2 * t, carry))

    def four(t, carry):
        return two(2 * t + 1, two(2 * t, carry))

    lax.fori_loop(0, n // 4, four, 0)
    lax.fori_loop(2 * (n // 4), n // 2, two, 0)
    lax.fori_loop(2 * (n // 2), n, body, 0)


def _init_mask_bias(bias_ref):
    tq = bias_ref.shape[1]
    r = lax.broadcasted_iota(jnp.int32, (tq, tq), 0) // 64
    c = lax.broadcasted_iota(jnp.int32, (tq, tq), 1) // 64
    bias_ref[0] = jnp.zeros((tq, tq), F32)
    bias_ref[1] = jnp.where(c <= r, 0.0, NEG_INF)


def _gate_block(tq):
    return pl.BlockSpec((tq, 128), lambda h, i: (i, (Z_MIX + Z_C + Z_KR) // 128 + h))


def _attn_fwd(qh, kh, vh, z, name, tq=512):
    t = qh.shape[1]
    tq = min(tq, t)

    def body(q_ref, g_ref, k_hbm, v_hbm, o_ref, yc_ref, lse_ref, k_v, v_v, m_s, acc_s, s_a, s_b, mx_a, mx_b, bias_s,
             sem):
        h, i = pl.program_id(0), pl.program_id(1)

        @pl.when(i == 0)
        def _():
            ck = pltpu.make_async_copy(k_hbm.at[h], k_v, sem.at[0])
            cv = pltpu.make_async_copy(v_hbm.at[h], v_v.at[:, 0:V_DIM], sem.at[1])
            ck.start()
            cv.start()
            v_v[:, V_DIM:2 * V_DIM] = jnp.ones((t, V_DIM), MXU_DTYPE)
            _init_mask_bias(bias_s)
            ck.wait()
            cv.wait()

        q = q_ref[...]
        m_s[...] = jnp.full(m_s.shape, NEG_INF, F32)
        acc_s[...] = jnp.zeros(acc_s.shape, F32)

        def keys(j):
            return pl.ds(pl.multiple_of(j * tq, tq), tq)

        def scores(s_ref, mx_ref, j):
            s = _dot_nt(q, k_v[keys(j), :]) * (SCALE * LOG2E) + bias_s[(j == i).astype(jnp.int32)]
            s_ref[...] = s
            mx_ref[...] = jnp.broadcast_to(jnp.max(s, axis=-1, keepdims=True), mx_ref.shape)

        def softmax_pv(s_ref, mx_ref, j):
            m_old = m_s[...]
            m_new = jnp.maximum(m_old, mx_ref[...])
            p = jnp.exp2(s_ref[...] - jnp.tile(m_new, (1, tq // 128)))
            alpha = jnp.exp2(m_old - m_new)
            m_s[...] = m_new
            acc_s[...] = jnp.tile(alpha, (1, 2)) * acc_s[...] + _dot(_mx(p), v_v[keys(j), :])

        scores(s_a, mx_a, 0)

        def pair(pp, carry):
            scores(s_b, mx_b, 2 * pp + 1)
            softmax_pv(s_a, mx_a, 2 * pp)
            scores(s_a, mx_a, jnp.minimum(2 * pp + 2, i))
            softmax_pv(s_b, mx_b, 2 * pp + 1)
            return carry

        _loop_in_long_trips((i + 1) // 2, pair)

        @pl.when(i % 2 == 0)
        def _():
            softmax_pv(s_a, mx_a, i)

        l = acc_s[:, V_DIM:2 * V_DIM]
        o = acc_s[:, 0:V_DIM] / l
        o_ref[...] = o
        yc_ref[...] = _mx(o * _silu_and_grad(g_ref[...])[0])
        lse_ref[...] = m_s[...] + jnp.log2(l)

    return pl.pallas_call(
        body, name=name, grid=(N_HEADS, t // tq),
        in_specs=[pl.BlockSpec((None, tq, QK_PAD), lambda h, i: (h, i, 0)), _gate_block(tq), ANY, ANY],
        out_specs=[pl.BlockSpec((tq, 128), lambda h, i: (i, h)), pl.BlockSpec((tq, 128), lambda h, i: (i, h)),
                   pl.BlockSpec((None, tq, 128), lambda h, i: (h, i, 0))],
        out_shape=[jax.ShapeDtypeStruct((t, N_HEADS * V_DIM), F32), jax.ShapeDtypeStruct((t, N_HEADS * V_DIM), MXU_DTYPE),
                   jax.ShapeDtypeStruct((N_HEADS, t, 128), F32)],
        scratch_shapes=[pltpu.VMEM((t, QK_PAD), MXU_DTYPE), pltpu.VMEM((t, 2 * V_DIM), MXU_DTYPE),
                        pltpu.VMEM((tq, 128), F32), pltpu.VMEM((tq, 2 * V_DIM), F32),
                        pltpu.VMEM((tq, tq), F32), pltpu.VMEM((tq, tq), F32), pltpu.VMEM((tq, 128), F32),
                        pltpu.VMEM((tq, 128), F32), pltpu.VMEM((2, tq, tq), F32), pltpu.SemaphoreType.DMA((2,))],
        compiler_params=_cp(2))(qh, z, kh, vh)


def _attn_bwd(qh, kh, vh, o, lse, dycat, z, name, tq=512):
    t = qh.shape[1]
    tq = min(tq, t)
    nq = t // tq

    def body(q_ref, o_ref, lse_ref, dy_ref, g_ref, k_hbm, v_hbm, dq_ref, dgate_ref, dk_hbm, dv_hbm,
             k_v, v_v, dk_acc, dv_acc, dq_acc, delta_s, s_a, dp_a, s_b, dp_b, bias_s, sem):
        h, i = pl.program_id(0), pl.program_id(1)

        @pl.when(i == 0)
        def _():
            ck = pltpu.make_async_copy(k_hbm.at[h], k_v, sem.at[0])
            cv = pltpu.make_async_copy(v_hbm.at[h], v_v, sem.at[1])
            ck.start()
            cv.start()
            _init_mask_bias(bias_s)
            dk_acc[...] = jnp.zeros(dk_acc.shape, F32)
            dv_acc[...] = jnp.zeros(dv_acc.shape, F32)
            ck.wait()
            cv.wait()

        gate, dy, of = g_ref[...], dy_ref[...], o_ref[...]
        silu, dsilu = _silu_and_grad(gate)
        do = dy * silu
        delta = jnp.sum(do * of, axis=-1, keepdims=True)
        dgate_ref[...] = _mx(dy * of * dsilu)
        dob = _mx(do)
        q = q_ref[...]
        delta_s[...] = jnp.broadcast_to(delta, delta_s.shape)
        dq_acc[...] = jnp.zeros(dq_acc.shape, F32)

        def keys(j):
            return pl.ds(pl.multiple_of(j * tq, tq), tq)

        def scores(s_ref, dp_ref, j):
            s = _dot_nt(q, k_v[keys(j), :]) * (SCALE * LOG2E) + bias_s[(j == i).astype(jnp.int32)]
            s_ref[...] = s - jnp.tile(lse_ref[...], (1, tq // 128))
            dp_ref[...] = _dot_nt(dob, v_v[keys(j), :]) - jnp.tile(delta_s[...], (1, tq // 128))

        def grads(s_ref, dp_ref, j):
            ks = keys(j)
            p = jnp.exp2(s_ref[...])
            ds = p * dp_ref[...] * SCALE
            pb, dsb = _mx(p), _mx(ds)
            dq_acc[...] += _dot(dsb, k_v[ks, :])
            dk_acc[ks, :] += _dot_tn(dsb, q)
            dv_acc[ks, :] += _dot_tn(pb, dob)

        scores(s_a, dp_a, 0)

        def pair(pp, carry):
            scores(s_b, dp_b, 2 * pp + 1)
            grads(s_a, dp_a, 2 * pp)
            scores(s_a, dp_a, jnp.minimum(2 * pp + 2, i))
            grads(s_b, dp_b, 2 * pp + 1)
            return carry

        _loop_in_long_trips((i + 1) // 2, pair)

        @pl.when(i % 2 == 0)
        def _():
            grads(s_a, dp_a, i)

        dq_ref[...] = dq_acc[...]

        @pl.when(i == nq - 1)
        def _():
            ck = pltpu.make_async_copy(dk_acc, dk_hbm.at[h], sem.at[0])
            cv = pltpu.make_async_copy(dv_acc, dv_hbm.at[h], sem.at[1])
            ck.start()
            cv.start()
            ck.wait()
            cv.wait()

    return pl.pallas_call(
        body, name=name, grid=(N_HEADS, nq),
        in_specs=[pl.BlockSpec((None, tq, QK_PAD), lambda h, i: (h, i, 0)),
                  pl.BlockSpec((tq, 128), lambda h, i: (i, h)),
                  pl.BlockSpec((None, tq, 128), lambda h, i: (h, i, 0)),
                  pl.BlockSpec((tq, 128), lambda h, i: (i, N_HEADS + h)), _gate_block(tq), ANY, ANY],
        out_specs=[pl.BlockSpec((None, tq, QK_PAD), lambda h, i: (h, i, 0)),
                   pl.BlockSpec((tq, 128), lambda h, i: (i, h)), ANY, ANY],
        out_shape=[jax.ShapeDtypeStruct((N_HEADS, t, QK_PAD), F32), jax.ShapeDtypeStruct((t, Z_GATE), MXU_DTYPE),
                   jax.ShapeDtypeStruct((N_HEADS, t, QK_PAD), F32), jax.ShapeDtypeStruct((N_HEADS, t, V_DIM), F32)],
        scratch_shapes=[pltpu.VMEM((t, QK_PAD), MXU_DTYPE), pltpu.VMEM((t, V_DIM), MXU_DTYPE),
                        pltpu.VMEM((t, QK_PAD), F32), pltpu.VMEM((t, V_DIM), F32), pltpu.VMEM((tq, QK_PAD), F32),
                        pltpu.VMEM((tq, 128), F32)] + [pltpu.VMEM((tq, tq), F32)] * 4
        + [pltpu.VMEM((2, tq, tq), F32), pltpu.SemaphoreType.DMA((2,))],
        compiler_params=_cp(2))(qh, o, lse, dycat, z, kh, vh)


def _out_proj_fwd(yab, yc, w, x, g, target, name, tm=512):
    t, d = x.shape
    tm = min(tm, t)
    is_last = target is not None

    def body(*refs):
        if is_last:
            yab_ref, yc_ref, w_ref, x_ref, g_ref, t_ref, y_ref, dout_ref, loss_ref = refs
            _zero_when(pl.program_id(0) == 0, loss_ref)
        else:
            yab_ref, yc_ref, w_ref, x_ref, g_ref, y_ref, out_ref = refs
        y = _dot(jnp.concatenate([yab_ref[...], yc_ref[...]], axis=1), w_ref[...])
        y_ref[...] = y
        out = x_ref[...] + _rms(y, g_ref[...])[0]
        if is_last:
            diff = out - t_ref[...]
            dout_ref[...] = diff * (1.0 / d)
            part = jnp.sum(jnp.sum(diff * diff, axis=-1, keepdims=True), axis=0, keepdims=True) * (0.5 / d)
            _acc(loss_ref, jnp.broadcast_to(part, (1, 128)))
        else:
            out_ref[...] = out

    row = lambda wd: pl.BlockSpec((tm, wd), lambda i: (i, 0))
    fixed = lambda a, b: pl.BlockSpec((a, b), lambda i: (0, 0))
    in_specs = [row(512), row(512), fixed(d, d), row(d), fixed(1, d)]
    args = [yab, yc, w, x, g]
    out_specs = [row(d), row(d)]
    out_shape = [jax.ShapeDtypeStruct((t, d), F32), jax.ShapeDtypeStruct((t, d), F32)]
    if is_last:
        in_specs.append(row(d))
        args.append(target)
        out_specs.append(fixed(1, 128))
        out_shape.append(jax.ShapeDtypeStruct((1, 128), F32))
    return pl.pallas_call(body, name=name, grid=(t // tm,), in_specs=in_specs, out_specs=out_specs,
                          out_shape=out_shape, compiler_params=_cp())(*args)


def _out_proj_bwd(dout, y, yab, yc, w, g, name, tm=512):
    t, d = y.shape
    tm = min(tm, t)

    def body(dout_ref, y_ref, yab_ref, yc_ref, w_ref, g_ref, dycat_ref, dw_ref, dg_ref):
        _zero_when(pl.program_id(0) == 0, dw_ref, dg_ref)
        y = y_ref[...]
        r = lax.rsqrt(jnp.mean(y * y, axis=-1, keepdims=True) + EPS)
        dy, dgt = _rms_bwd(y, r, g_ref[...], dout_ref[...])
        _acc(dg_ref, _colsum(dgt))
        dyb = _mx(dy)
        _acc(dw_ref, _dot_tn(jnp.concatenate([yab_ref[...], yc_ref[...]], axis=1), dyb))
        dycat_ref[...] = _dot_nt(dyb, w_ref[...])

    row = lambda wd: pl.BlockSpec((tm, wd), lambda i: (i, 0))
    fixed = lambda a, b: pl.BlockSpec((a, b), lambda i: (0, 0))
    return pl.pallas_call(
        body, name=name, grid=(t // tm,),
        in_specs=[row(d), row(d), row(512), row(512), fixed(d, d), fixed(1, d)],
        out_specs=[row(d), fixed(d, d), fixed(1, d)],
        out_shape=[jax.ShapeDtypeStruct((t, d), F32), jax.ShapeDtypeStruct((d, d), F32),
                   jax.ShapeDtypeStruct((1, d), F32)],
        compiler_params=_cp())(dout, y, yab, yc, w, g)


def _mesh_pos():
    return lax.axis_index("x"), lax.axis_index("y"), lax.axis_index("c")


def _remote(src, dst, send_sem, recv_sem, to):
    return pltpu.make_async_remote_copy(src_ref=src, dst_ref=dst, send_sem=send_sem, recv_sem=recv_sem,
                                        device_id=to, device_id_type=MESH)


CHUNK_ROWS = 256


def _pieces(rows):
    return [(s, min(CHUNK_ROWS, rows - s)) for s in range(0, rows, CHUNK_ROWS)]


def _piece_table(shapes):
    return [(a, s, sz) for a, shp in enumerate(shapes) for s, sz in _pieces(shp[-2])]


def _gather_weights(shards):
    n = len(shards)
    table = _piece_table([s.shape for s in shards])
    npc = len(table)

    def body(*refs):
        ins, outs = refs[:n], refs[n:2 * n]
        send_sems, recv_sems, fwd_send, fwd_recv = refs[2 * n:]
        x, y, c = _mesh_pos()
        me, sibling = (x, y, c), (x, y, 1 - c)
        chips = [(1 - x, y), (x, 1 - y), (1 - x, 1 - y)]
        slot = lambda cx, cy, layer: 2 * (2 * cx + cy) + layer
        first = []
        for a in range(n):
            for j, (cx, cy) in enumerate(chips):
                first.append(_remote(ins[a].at[c], outs[a].at[slot(x, y, c)], send_sems.at[a, j], recv_sems.at[a, j],
                                     (cx, cy, c)))
                first[-1].start()
        passed = []
        for j, (cx, cy) in enumerate(chips):
            for a in range(n):
                blk = outs[a].at[slot(cx, cy, c)]
                _remote(blk, blk, send_sems.at[a, j], recv_sems.at[a, j], me).wait_recv()
            for q, (a, s, sz) in enumerate(table):
                rows = outs[a].at[slot(cx, cy, c), pl.ds(s, sz)]
                passed.append(_remote(rows, rows, fwd_send.at[j, q], fwd_recv.at[j, q], sibling))
                passed[-1].start()
        for j, (cx, cy) in enumerate(chips):
            for q, (a, s, sz) in enumerate(table):
                rows = outs[a].at[slot(cx, cy, 1 - c), pl.ds(s, sz)]
                _remote(rows, rows, fwd_send.at[j, q], fwd_recv.at[j, q], me).wait_recv()
        for cp in first + passed:
            cp.wait_send()

    return pl.pallas_call(
        body, name="gather_weights", in_specs=[ANY] * n, out_specs=[ANY] * n,
        out_shape=[jax.ShapeDtypeStruct((8,) + s.shape[1:], s.dtype) for s in shards],
        scratch_shapes=[pltpu.SemaphoreType.DMA((n, 3)), pltpu.SemaphoreType.DMA((n, 3)),
                        pltpu.SemaphoreType.DMA((3, npc)), pltpu.SemaphoreType.DMA((3, npc))])(*shards)


def _pair_exchange(parts, common):
    n = len(parts)
    table = _piece_table([p.shape for p in parts] + [common.shape])
    npc = len(table)

    def body(*refs):
        ins, outs = refs[:n + 1], refs[n + 1:2 * n + 2]
        send_sems, recv_sems = refs[2 * n + 2:]
        x, y, c = _mesh_pos()
        sent = []
        for k in range(4):
            for q, (a, s, sz) in enumerate(table):
                if a == n and k > 0:
                    continue
                src = ins[a].at[2 * k + 1 - c, pl.ds(s, sz)] if a < n else ins[a].at[pl.ds(s, sz)]
                dst = outs[a].at[k, pl.ds(s, sz)] if a < n else outs[a].at[pl.ds(s, sz)]
                sent.append(_remote(src, dst, send_sems.at[k, q], recv_sems.at[k, q], (x, y, 1 - c)))
                sent[-1].start()
        for k in range(4):
            for q, (a, s, sz) in enumerate(table):
                if a == n and k > 0:
                    continue
                dst = outs[a].at[k, pl.ds(s, sz)] if a < n else outs[a].at[pl.ds(s, sz)]
                _remote(dst, dst, send_sems.at[k, q], recv_sems.at[k, q], (x, y, c)).wait_recv()
        for cp in sent:
            cp.wait_send()

    return pl.pallas_call(
        body, name="grad_pair_exchange", in_specs=[ANY] * (n + 1), out_specs=[ANY] * (n + 1),
        out_shape=[jax.ShapeDtypeStruct((4,) + p.shape[1:], p.dtype) for p in parts]
        + [jax.ShapeDtypeStruct(common.shape, common.dtype)],
        scratch_shapes=[pltpu.SemaphoreType.DMA((4, npc)), pltpu.SemaphoreType.DMA((4, npc))])(*parts, common)


def _chip_exchange(parts, common):
    n = len(parts)
    table = _piece_table([p.shape for p in parts] + [common.shape])
    npc = len(table)

    def body(*refs):
        ins, outs = refs[:n + 1], refs[n + 1:2 * n + 2]
        send_sems, recv_sems = refs[2 * n + 2:]
        x, y, c = _mesh_pos()
        mine = 2 * x + y
        chips = [(1 - x, y), (x, 1 - y), (1 - x, 1 - y)]
        src = lambda a, k: ins[a].at[k] if a < n else ins[a]
        sent = []
        for j, (cx, cy) in enumerate(chips):
            for q, (a, s, sz) in enumerate(table):
                sent.append(_remote(src(a, 2 * cx + cy).at[pl.ds(s, sz)], outs[a].at[mine, pl.ds(s, sz)],
                                    send_sems.at[j, q], recv_sems.at[j, q], (cx, cy, c)))
                sent[-1].start()
        for j, (cx, cy) in enumerate(chips):
            for q, (a, s, sz) in enumerate(table):
                dst = outs[a].at[2 * cx + cy, pl.ds(s, sz)]
                _remote(dst, dst, send_sems.at[j, q], recv_sems.at[j, q], (x, y, c)).wait_recv()
        for cp in sent:
            cp.wait_send()

    return pl.pallas_call(
        body, name="grad_chip_exchange", in_specs=[ANY] * (n + 1), out_specs=[ANY] * (n + 1),
        out_shape=[jax.ShapeDtypeStruct(p.shape, p.dtype) for p in parts]
        + [jax.ShapeDtypeStruct((4,) + common.shape, common.dtype)],
        scratch_shapes=[pltpu.SemaphoreType.DMA((3, npc)), pltpu.SemaphoreType.DMA((3, npc))])(*parts, common)


def _sibling_exchange(sums):
    n = len(sums)
    table = _piece_table([s.shape for s in sums])
    npc = len(table)

    def body(*refs):
        ins, outs = refs[:n], refs[n:2 * n]
        send_sems, recv_sems = refs[2 * n:]
        x, y, c = _mesh_pos()
        sent = []
        for q, (a, s, sz) in enumerate(table):
            sent.append(_remote(ins[a].at[pl.ds(s, sz)], outs[a].at[pl.ds(s, sz)], send_sems.at[q], recv_sems.at[q],
                                (x, y, 1 - c)))
            sent[-1].start()
        for q, (a, s, sz) in enumerate(table):
            dst = outs[a].at[pl.ds(s, sz)]
            _remote(dst, dst, send_sems.at[q], recv_sems.at[q], (x, y, c)).wait_recv()
        for cp in sent:
            cp.wait_send()

    return pl.pallas_call(
        body, name="sibling_exchange", in_specs=[ANY] * n, out_specs=[ANY] * n,
        out_shape=[jax.ShapeDtypeStruct(s.shape, s.dtype) for s in sums],
        scratch_shapes=[pltpu.SemaphoreType.DMA((npc,)), pltpu.SemaphoreType.DMA((npc,))])(*sums)


def _pair_sum(mine, got, name, tr):
    _, r, c = got.shape
    tr = min(tr, r)

    def body(p_ref, g_ref, o_ref, w_ref):
        total = p_ref[...] + g_ref[...]
        o_ref[...] = total
        w_ref[...] = total.astype(WIRE_DTYPE)

    blk = pl.BlockSpec((None, tr, c), lambda k, i: (k, i, 0))
    return pl.pallas_call(
        body, name=name, grid=(4, r // tr), in_specs=[blk, blk], out_specs=[blk, blk],
        out_shape=[jax.ShapeDtypeStruct(got.shape, F32), jax.ShapeDtypeStruct(got.shape, WIRE_DTYPE)],
        compiler_params=_cp(2))(mine, got)


def _sum_chips(own, recv, name, tr):
    r, c = own.shape
    tr = min(tr, r)

    def body(own_ref, r_ref, o_ref):
        chip = 2 * lax.axis_index("x") + lax.axis_index("y")
        own_blk = own_ref[...]
        acc = jnp.where(chip == 0, own_blk, r_ref[0].astype(F32))
        for k in range(1, 4):
            acc = acc + jnp.where(chip == k, own_blk, r_ref[k].astype(F32))
        o_ref[...] = acc

    return pl.pallas_call(
        body, name=name, grid=(r // tr,),
        in_specs=[pl.BlockSpec((tr, c), lambda i: (i, 0)), pl.BlockSpec((4, tr, c), lambda i: (0, i, 0))],
        out_specs=pl.BlockSpec((tr, c), lambda i: (i, 0)),
        out_shape=jax.ShapeDtypeStruct((r, c), F32), compiler_params=_cp())(own, recv)


def _sum_leading(parts, name, tr):
    nlead, r, c = parts.shape
    tr = min(tr, r)

    def body(p_ref, o_ref):
        acc = p_ref[0]
        for j in range(1, nlead):
            acc = acc + p_ref[j]
        o_ref[...] = acc

    return pl.pallas_call(
        body, name=name, grid=(r // tr,),
        in_specs=[pl.BlockSpec((nlead, tr, c), lambda i: (0, i, 0))], out_specs=pl.BlockSpec((tr, c), lambda i: (i, 0)),
        out_shape=jax.ShapeDtypeStruct((r, c), parts.dtype), compiler_params=_cp())(parts)


def _adamw(w, g, m, v, name, tr):
    r, c = w.shape
    tr = min(tr, r)

    def body(w_ref, g_ref, m_ref, v_ref, d_ref, nm_ref, nv_ref):
        gg = g_ref[...]
        nm = ADAM_B1 * m_ref[...] + (1.0 - ADAM_B1) * gg
        nv = ADAM_B2 * v_ref[...] + (1.0 - ADAM_B2) * jnp.square(gg)
        m_hat = nm / (1.0 - ADAM_B1 ** ADAM_STEP)
        v_hat = nv / (1.0 - ADAM_B2 ** ADAM_STEP)
        d_ref[...] = -ADAM_LR * (m_hat / (jnp.sqrt(v_hat) + ADAM_EPS) + ADAM_WD * w_ref[...])
        nm_ref[...] = nm
        nv_ref[...] = nv

    blk = pl.BlockSpec((tr, c), lambda i: (i, 0))
    return pl.pallas_call(
        body, name=name, grid=(r // tr,), in_specs=[blk] * 4, out_specs=[blk] * 3,
        out_shape=[jax.ShapeDtypeStruct((r, c), F32)] * 3, compiler_params=_cp())(w, g, m, v)


def _rope_tables(positions):
    inv_freq = ROPE_BASE ** (-jnp.arange(0, 64, 2, dtype=F32) / 64)
    ang = positions.astype(F32)[:, None] * inv_freq
    cos, sin = jnp.cos(ang), jnp.sin(ang)
    t = positions.shape[0]
    rc = jnp.concatenate([jnp.ones((t, 128), F32), cos, cos, jnp.ones((t, 64), F32)], axis=1)
    rs = jnp.concatenate([jnp.zeros((t, 128), F32), sin, sin, jnp.zeros((t, 64), F32)], axis=1)
    return rc, rs


def _layer_params(l, w_in, w_uq, w_ukv, w_out, small):
    p = {}
    p["w_in"] = jnp.concatenate([w_in[l][:, :1984], jnp.zeros((1024, 64), w_in.dtype), w_in[l][:, 1984:]], axis=1)
    p["w_uq"] = jnp.pad(w_uq[l].reshape(384, 4, 192), ((0, 0), (0, 0), (0, 64))).reshape(384, 1024)
    p["w_ukv"] = w_ukv[l].reshape(256, 4, 2, 128).transpose(0, 2, 1, 3).reshape(256, 1024)
    p["w_out"] = w_out[l]
    p["pre_g"] = small["pre_norm_g"][l][None]
    p["post_g"] = small["post_norm_g"][l][None]
    p["sgu_w"] = small["sgu_w"][l].reshape(512, 128)
    p["sgu_wt"] = small["sgu_w"][l].transpose(0, 2, 1).reshape(512, 128)
    p["sgu_bias"] = jnp.repeat(small["sgu_b"][l].T, 64, axis=1)
    p["ln_g"] = small["sgu_ln_g"][l][None]
    p["ln_b"] = small["sgu_ln_b"][l][None]
    p["pool_wbd"] = _mx(jax.scipy.linalg.block_diag(*[small["pool_w"][l][gi] for gi in range(4)]))
    p["pool_scale"] = small["pool_scale"][l][None]
    p["gq"] = small["q_norm_g"][l][None]
    p["gkv"] = small["kv_norm_g"][l][None]
    return p


def _layer_fwd(l, x, p, rc, rs, target):
    z, h = _in_proj_fwd(x, p["pre_g"], p["w_in"], f"in_proj_fwd_{l}")
    yab = _mix_fwd(z, p["sgu_w"], p["sgu_bias"], p["ln_g"], p["ln_b"], p["pool_wbd"], p["pool_scale"], f"mix_fwd_{l}")
    qh, kh, vh = _qkv_fwd(z, rc, rs, p["w_uq"], p["w_ukv"], p["gq"], p["gkv"], f"qkv_fwd_{l}")
    o, yc, lse = _attn_fwd(qh, kh, vh, z, f"attn_fwd_{l}")
    outs = _out_proj_fwd(yab, yc, p["w_out"], x, p["post_g"], target, f"out_proj_fwd_{l}")
    saved = dict(x=x, z=z, h=h, yab=yab, qh=qh, kh=kh, vh=vh, o=o, yc=yc, lse=lse, y=outs[0])
    return saved, outs[1:]


def _layer_bwd(l, dout, sv, p, rc, rs):
    dycat, dw_out, dpost = _out_proj_bwd(dout, sv["y"], sv["yab"], sv["yc"], p["w_out"], p["post_g"], f"out_proj_bwd_{l}")
    dq, dgate, dk, dv = _attn_bwd(sv["qh"], sv["kh"], sv["vh"], sv["o"], sv["lse"], dycat, sv["z"], f"attn_bwd_{l}")
    dzc, dzk, dwq, dwkv, dgq, dgkv = _qkv_bwd(dq, dk, dv, sv["z"], rc, rs, p["w_uq"], p["w_ukv"], p["gq"], p["gkv"],
                                              f"qkv_bwd_{l}")
    dzm, dsw, dsb, dlng, dlnb, dpw, dps = _mix_bwd(sv["z"], dycat, p["sgu_w"], p["sgu_wt"], p["sgu_bias"], p["ln_g"],
                                                   p["ln_b"], p["pool_wbd"], p["pool_scale"], f"mix_bwd_{l}")
    dx, dw_in, dpre = _in_proj_bwd(dzm, dzc, dzk, dgate, sv["h"], sv["x"], dout, p["w_in"], p["pre_g"], f"in_proj_bwd_{l}")
    grads = {
        "pre_norm_g": dpre[0], "post_norm_g": dpost[0],
        "w_in": jnp.concatenate([dw_in[:, :1984], dw_in[:, 2048:]], axis=1),
        "sgu_w": dsw.reshape(4, 128, 128), "sgu_b": dsb[:, :4].T, "sgu_ln_g": dlng[0], "sgu_ln_b": dlnb[0],
        "pool_w": jnp.stack([dpw[64 * gi:64 * gi + 64, 64 * gi:64 * gi + 64] for gi in range(4)]),
        "pool_scale": dps[0], "q_norm_g": dgq[0],
        "w_uq": dwq.reshape(384, 4, 256)[:, :, :192].reshape(384, 768), "kv_norm_g": dgkv[0],
        "w_ukv": dwkv.reshape(256, 2, 4, 128).transpose(0, 2, 1, 3).reshape(256, 1024), "w_out": dw_out,
    }
    return dx, grads


SMALL_NAMES = ["pre_norm_g", "post_norm_g", "sgu_w", "sgu_b", "sgu_ln_g", "sgu_ln_b", "pool_w", "pool_scale",
               "q_norm_g", "kv_norm_g"]
BIG_NAMES = ["w_in", "w_uq", "w_ukv", "w_out"]
WEIGHT_NAMES = ["pre_norm_g", "post_norm_g", "w_in", "sgu_w", "sgu_b", "sgu_ln_g", "sgu_ln_b", "pool_w", "pool_scale",
                "q_norm_g", "w_uq", "kv_norm_g", "w_ukv", "w_out"]


def _local_step(x, positions, target, w_in, w_uq, w_ukv, w_out, small):
    rc, rs = _rope_tables(positions)
    params = [_layer_params(l, w_in, w_uq, w_ukv, w_out, small) for l in range(DEPTH)]
    saved = []
    for l in range(DEPTH):
        sv, outs = _layer_fwd(l, x, params[l], rc, rs, target if l == DEPTH - 1 else None)
        saved.append(sv)
        if l < DEPTH - 1:
            x = outs[0]
    dout, loss = outs
    grads = [None] * DEPTH
    for l in reversed(range(DEPTH)):
        dout, grads[l] = _layer_bwd(l, dout, saved[l], params[l], rc, rs)
    return loss[0, 0], dout, {k: jnp.stack([grads[l][k] for l in range(DEPTH)]) for k in WEIGHT_NAMES}


def _pack_small(tree, extra=None):
    pieces = [tree[k].reshape(-1) for k in SMALL_NAMES]
    pieces.append(jnp.zeros((1,), F32) if extra is None else extra.reshape(1))
    flat = jnp.concatenate(pieces)
    rows = -(-flat.shape[0] // 1024) * 8
    return jnp.pad(flat, (0, rows * 128 - flat.shape[0])).reshape(rows, 128)


def _unpack_small(packed, like):
    flat = packed.reshape(-1)
    out, off = {}, 0
    for k in SMALL_NAMES:
        size = like[k].size
        out[k] = flat[off:off + size].reshape(like[k].shape)
        off += size
    return out, flat[off]


def kernel(x, positions, pre_norm_g, post_norm_g, w_in, sgu_w, sgu_b, sgu_ln_g, sgu_ln_b, pool_w, pool_scale, q_norm_g, w_uq, kv_norm_g, w_ukv, w_out, loss_target, m_pre_norm_g, m_post_norm_g, m_w_in, m_sgu_w, m_sgu_b, m_sgu_ln_g, m_sgu_ln_b, m_pool_w, m_pool_scale, m_q_norm_g, m_w_uq, m_kv_norm_g, m_w_ukv, m_w_out, v_pre_norm_g, v_post_norm_g, v_w_in, v_sgu_w, v_sgu_b, v_sgu_ln_g, v_sgu_ln_b, v_pool_w, v_pool_scale, v_q_norm_g, v_w_uq, v_kv_norm_g, v_w_ukv, v_w_out):
    w = dict(pre_norm_g=pre_norm_g, post_norm_g=post_norm_g, w_in=w_in, sgu_w=sgu_w, sgu_b=sgu_b, sgu_ln_g=sgu_ln_g,
             sgu_ln_b=sgu_ln_b, pool_w=pool_w, pool_scale=pool_scale, q_norm_g=q_norm_g, w_uq=w_uq, kv_norm_g=kv_norm_g,
             w_ukv=w_ukv, w_out=w_out)
    m = dict(pre_norm_g=m_pre_norm_g, post_norm_g=m_post_norm_g, w_in=m_w_in, sgu_w=m_sgu_w, sgu_b=m_sgu_b,
             sgu_ln_g=m_sgu_ln_g, sgu_ln_b=m_sgu_ln_b, pool_w=m_pool_w, pool_scale=m_pool_scale, q_norm_g=m_q_norm_g,
             w_uq=m_w_uq, kv_norm_g=m_kv_norm_g, w_ukv=m_w_ukv, w_out=m_w_out)
    v = dict(pre_norm_g=v_pre_norm_g, post_norm_g=v_post_norm_g, w_in=v_w_in, sgu_w=v_sgu_w, sgu_b=v_sgu_b,
             sgu_ln_g=v_sgu_ln_g, sgu_ln_b=v_sgu_ln_b, pool_w=v_pool_w, pool_scale=v_pool_scale, q_norm_g=v_q_norm_g,
             w_uq=v_w_uq, kv_norm_g=v_kv_norm_g, w_ukv=v_w_ukv, w_out=v_w_out)

    core = lax.axis_index("c")
    chip = 2 * lax.axis_index("x") + lax.axis_index("y")
    shards = [_mx(w[k]) for k in BIG_NAMES]
    gathered = _gather_weights(shards)
    g_in, g_uq, g_ukv, g_out = [lax.dynamic_update_slice(g, s, (2 * chip, 0, 0)) for g, s in zip(gathered, shards)]
    cols = lambda g: g.reshape((4, 2) + g.shape[1:]).transpose(1, 2, 0, 3).reshape(2, g.shape[1], 4 * g.shape[2])
    full_out = g_out.reshape(4, 2, 256, 1024).transpose(1, 0, 2, 3).reshape(2, 1024, 1024)
    loss, dx, grads = _local_step(x[0], positions[0], loss_target[0], cols(g_in), cols(g_uq), cols(g_ukv), full_out, w)

    split_cols = lambda g: g.reshape(2, g.shape[1], 4, g.shape[2] // 4).transpose(2, 0, 1, 3).reshape(8, g.shape[1], g.shape[2] // 4)
    parts = [split_cols(grads["w_in"]), split_cols(grads["w_uq"]), split_cols(grads["w_ukv"]),
             grads["w_out"].reshape(2, 4, 256, 1024).transpose(1, 0, 2, 3).reshape(8, 256, 1024)]
    common = _pack_small(grads, loss)
    got = _pair_exchange(parts, common)
    mine = [lax.dynamic_index_in_dim(p.reshape((4, 2) + p.shape[1:]), core, axis=1, keepdims=False) for p in parts]
    pair_sums = [_pair_sum(mine[a], got[a], f"pair_sum_{BIG_NAMES[a]}", 128) for a in range(4)]
    chip_common = _sum_leading(jnp.stack([common, got[4]]), "pair_sum_small", common.shape[0])
    received = _chip_exchange([ps[1] for ps in pair_sums], chip_common)
    sums = [_sum_chips(lax.dynamic_index_in_dim(pair_sums[a][0], chip, axis=0, keepdims=False), received[a],
                       f"sum_{BIG_NAMES[a]}", 128) for a in range(4)]
    all_common = lax.dynamic_update_slice(received[4], chip_common[None], (chip, 0, 0))
    small_sum, loss = _unpack_small(_sum_leading(all_common, "sum_small", all_common.shape[1]), w)
    others = _sibling_exchange(sums)
    total = dict(small_sum)
    for a, k in enumerate(BIG_NAMES):
        total[k] = jnp.where(core == 0, jnp.stack([sums[a], others[a]]), jnp.stack([others[a], sums[a]]))

    packed = _adamw(_pack_small(w), _pack_small(total), _pack_small(m), _pack_small(v), "adamw_small", 2048)
    small_out = [_unpack_small(pk, w)[0] for pk in packed]
    delta, new_m, new_v = {}, {}, {}
    for k in SMALL_NAMES:
        delta[k], new_m[k], new_v[k] = (so[k] for so in small_out)
    for k in BIG_NAMES:
        shape = w[k].shape
        flat = lambda a: a.reshape(shape[0] * shape[1], shape[2])
        res = _adamw(flat(w[k]), flat(total[k]), flat(m[k]), flat(v[k]), f"adamw_{k}", 256)
        delta[k], new_m[k], new_v[k] = (r.reshape(shape) for r in res)

    return (loss, dx[None], *[total[k] for k in WEIGHT_NAMES], *[delta[k] for k in WEIGHT_NAMES],
            *[new_m[k] for k in WEIGHT_NAMES], *[new_v[k] for k in WEIGHT_NAMES])
```

```python
import jax
import jax.numpy as jnp
from jax import lax
from jax.experimental import pallas as pl
from jax.experimental.pallas import tpu as pltpu

F32 = jnp.float32
MXU_DTYPE = jnp.bfloat16
WIRE_DTYPE = jnp.bfloat16
EPS = 1e-6
NEG_INF = -1e30
DEPTH = 2
N_HEADS = 4
QK_PAD = 256
V_DIM = 128
SCALE = 192 ** -0.5
LOG2E = 1.4426950408889634
ROPE_BASE = 10000.0
ADAM_LR, ADAM_B1, ADAM_B2, ADAM_EPS, ADAM_WD, ADAM_STEP = 0.001, 0.9, 0.999, 1e-08, 0.01, 10
VMEM_LIMIT_BYTES = 56 * 1024 * 1024
MESH = pl.DeviceIdType.MESH
ANY = pl.BlockSpec(memory_space=pl.ANY)

Z_MIX, Z_C, Z_KR, Z_GATE = 1280, 640, 128, 512
Z_W = Z_MIX + Z_C + Z_KR + Z_GATE


def _cp(n_axes=1):
    return pltpu.CompilerParams(dimension_semantics=("arbitrary",) * n_axes, vmem_limit_bytes=VMEM_LIMIT_BYTES)


def _dot(a, b):
    return lax.dot_general(a, b, (((1,), (0,)), ((), ())), preferred_element_type=F32)


def _dot_nt(a, b):
    return lax.dot_general(a, b, (((1,), (1,)), ((), ())), preferred_element_type=F32)


def _dot_tn(a, b):
    return lax.dot_general(a, b, (((0,), (0,)), ((), ())), preferred_element_type=F32)


def _mx(a):
    return a.astype(MXU_DTYPE)


def _silu_and_grad(g):
    sg = jax.nn.sigmoid(g)
    return g * sg, sg * (1.0 + g * (1.0 - sg))


def _rms(x, g):
    r = lax.rsqrt(jnp.mean(x * x, axis=-1, keepdims=True) + EPS)
    return x * r * g, r


def _rms_bwd(x, r, g, dy):
    xhat = x * r
    dyg = dy * g
    dx = r * (dyg - xhat * jnp.mean(dyg * xhat, axis=-1, keepdims=True))
    return dx, dy * xhat


def _zero_when(first, *refs):
    @pl.when(first)
    def _():
        for ref in refs:
            ref[...] = jnp.zeros(ref.shape, ref.dtype)


def _acc(ref, val):
    ref[...] += val


def _colsum(a):
    return jnp.sum(a, axis=0, keepdims=True)


def _in_proj_fwd(x, g, w, name, tm=512):
    t, d = x.shape
    n = w.shape[1]
    tm = min(tm, t)

    def body(x_ref, g_ref, w_ref, z_ref, h_ref):
        h, _ = _rms(x_ref[...], g_ref[...])
        h = _mx(h)
        h_ref[...] = h
        z_ref[...] = _dot(h, w_ref[...])

    return pl.pallas_call(
        body, name=name, grid=(t // tm,),
        in_specs=[pl.BlockSpec((tm, d), lambda i: (i, 0)), pl.BlockSpec((1, d), lambda i: (0, 0)),
                  pl.BlockSpec((d, n), lambda i: (0, 0))],
        out_specs=[pl.BlockSpec((tm, n), lambda i: (i, 0)), pl.BlockSpec((tm, d), lambda i: (i, 0))],
        out_shape=[jax.ShapeDtypeStruct((t, n), F32), jax.ShapeDtypeStruct((t, d), MXU_DTYPE)],
        compiler_params=_cp())(x, g, w)


def _in_proj_bwd(dz_mix, dz_c, dz_kr, dz_gate, h, x, d_res, w, g, name, tm=512):
    t, d = x.shape
    n = w.shape[1]
    tm = min(tm, t)

    def body(dm_ref, dc_ref, dk_ref, dg_ref, h_ref, x_ref, dres_ref, w_ref, g_ref, dx_ref, dw_ref, dgn_ref):
        first = pl.program_id(0) == 0
        dz = jnp.concatenate([dm_ref[...], dc_ref[...], dk_ref[...], dg_ref[...]], axis=1)

        _zero_when(first, dw_ref, dgn_ref)
        hb = h_ref[...]
        for c0 in range(0, n, 512):
            dw_ref[:, c0:c0 + 512] += _dot_tn(hb, dz[:, c0:c0 + 512])
        dh = _dot_nt(dz, w_ref[...])
        xf = x_ref[...]
        r = lax.rsqrt(jnp.mean(xf * xf, axis=-1, keepdims=True) + EPS)
        dx, dgt = _rms_bwd(xf, r, g_ref[...], dh)
        dx_ref[...] = dx + dres_ref[...]
        _acc(dgn_ref, _colsum(dgt))

    row = lambda wd: pl.BlockSpec((tm, wd), lambda i: (i, 0))
    fixed = lambda a, b: pl.BlockSpec((a, b), lambda i: (0, 0), pipeline_mode=pl.Buffered(1))
    return pl.pallas_call(
        body, name=name, grid=(t // tm,),
        in_specs=[row(Z_MIX), row(Z_C), row(Z_KR), row(Z_GATE), row(d), row(d), row(d), fixed(d, n), fixed(1, d)],
        out_specs=[row(d), fixed(d, n), fixed(1, d)],
        out_shape=[jax.ShapeDtypeStruct((t, d), F32), jax.ShapeDtypeStruct((d, n), F32),
                   jax.ShapeDtypeStruct((1, d), F32)],
        compiler_params=_cp())(dz_mix, dz_c, dz_kr, dz_gate, h, x, d_res, w, g)


def _lane_group(shape):
    return lax.broadcasted_iota(jnp.int32, shape, 1) // 64


def _select_group(vals):
    grp = _lane_group(vals[0].shape)
    out = vals[3]
    for gi in (2, 1, 0):
        out = jnp.where(grp == gi, vals[gi], out)
    return out


def _sgu_mask(transposed):
    r = (lax.broadcasted_iota(jnp.int32, (512, 128), 0) % 128) // 64
    c = lax.broadcasted_iota(jnp.int32, (512, 128), 1) // 64
    return (r <= c) if transposed else (c <= r)


def _sgu_apply(wstack, vb, nblk):
    outs = []
    for n in range(nblk):
        r = _dot(wstack, vb[n * 128:(n + 1) * 128, :])
        outs.append(_select_group([r[hh * 128:(hh + 1) * 128, :] for hh in range(4)]))
    return jnp.concatenate(outs, axis=0)


def _layer_norm(v, g, b):
    mu = jnp.mean(v, axis=-1, keepdims=True)
    vc = v - mu
    rstd = lax.rsqrt(jnp.mean(vc * vc, axis=-1, keepdims=True) + EPS)
    vhat = vc * rstd
    return vhat * g + b, vhat, rstd


def _pool_counts(t0, n):
    t = t0 + lax.broadcasted_iota(jnp.int32, (n, 256), 0)
    w = _select_group([jnp.full((n, 256), wv, jnp.int32) for wv in (2, 4, 8, 16)])
    return jnp.minimum(t + 1, w).astype(F32)


def _pooled(p, halo, t0):
    tm = p.shape[0]
    ext = jnp.concatenate([halo, p], axis=0)
    s2 = ext + pltpu.roll(ext, 1, 0)
    s4 = s2 + pltpu.roll(s2, 2, 0)
    s8 = s4 + pltpu.roll(s4, 4, 0)
    s16 = s8 + pltpu.roll(s8, 8, 0)
    sel = _select_group([s2, s4, s8, s16])[16:, :]
    return sel / _pool_counts(t0, tm) - p


def _pooled_bwd(dpool, dpool_halo, t0):
    tm = dpool.shape[0]
    n = tm + 16
    ext = jnp.concatenate([dpool, dpool_halo], axis=0) / _pool_counts(t0, n)
    f2 = ext + pltpu.roll(ext, n - 1, 0)
    f4 = f2 + pltpu.roll(f2, n - 2, 0)
    f8 = f4 + pltpu.roll(f4, n - 4, 0)
    f16 = f8 + pltpu.roll(f8, n - 8, 0)
    return _select_group([f2, f4, f8, f16])[:tm, :] - dpool


def _mix_specs(t, tm):
    nt16 = t // 16
    zrow = pl.BlockSpec((tm, Z_MIX), lambda i: (i, 0))
    prev_halo = pl.BlockSpec((16, 256), lambda i: (jnp.maximum(i * (tm // 16) - 1, 0), 3))
    fixed = lambda a, b: pl.BlockSpec((a, b), lambda i: (0, 0))
    params = [fixed(512, 128), fixed(128, 256), fixed(1, 256), fixed(1, 256), fixed(256, 256), fixed(1, 256)]
    return nt16, zrow, prev_halo, fixed, params


def _mix_fwd(z, sgu_w, sgu_bias, ln_g, ln_b, pool_wbd, pool_scale, name, tm=512):
    t = z.shape[0]
    tm = min(tm, t)
    _, zrow, prev_halo, _, params = _mix_specs(t, tm)

    def body(z_ref, halo_ref, w_ref, bias_ref, lng_ref, lnb_ref, pw_ref, ps_ref, y_ref):
        i = pl.program_id(0)
        u, v, gate = z_ref[:, 0:256], z_ref[:, 256:512], z_ref[:, 512:768]
        p, pgate = z_ref[:, 768:1024], z_ref[:, 1024:1280]
        vn, _, _ = _layer_norm(v, lng_ref[...], lnb_ref[...])
        wm = _mx(jnp.where(_sgu_mask(False), w_ref[...], 0.0))
        mixed = _sgu_apply(wm, _mx(vn), tm // 128) + jnp.tile(bias_ref[...], (tm // 128, 1))
        ya = u * mixed * _silu_and_grad(gate)[0]
        halo = jnp.where(i > 0, halo_ref[...], 0.0)
        pooled = _pooled(p, halo, i * tm)
        yb = _dot(_mx(pooled), pw_ref[...]) * ps_ref[...] * _silu_and_grad(pgate)[0]
        y_ref[...] = _mx(jnp.concatenate([ya, yb], axis=1))

    return pl.pallas_call(
        body, name=name, grid=(t // tm,),
        in_specs=[zrow, prev_halo] + params,
        out_specs=pl.BlockSpec((tm, 512), lambda i: (i, 0)),
        out_shape=jax.ShapeDtypeStruct((t, 512), MXU_DTYPE),
        compiler_params=_cp())(z, z, sgu_w, sgu_bias, ln_g, ln_b, pool_wbd, pool_scale)


def _mix_bwd(z, dycat, sgu_w, sgu_wt, sgu_bias, ln_g, ln_b, pool_wbd, pool_scale, name, tm=512):
    t = z.shape[0]
    tm = min(tm, t)
    nt16, zrow, prev_halo, fixed, params = _mix_specs(t, tm)
    nblk = tm // 128
    last = t // tm - 1

    def body(z_ref, halo_ref, zn_ref, dy_ref, dyn_ref, w_ref, wt_ref, bias_ref, lng_ref, lnb_ref, pw_ref, ps_ref,
             dz_ref, dw_ref, db_ref, dlng_ref, dlnb_ref, dpw_ref, dps_ref):
        i = pl.program_id(0)
        _zero_when(i == 0, dw_ref, db_ref, dlng_ref, dlnb_ref, dpw_ref, dps_ref)
        u, v, gate = z_ref[:, 0:256], z_ref[:, 256:512], z_ref[:, 512:768]
        p, pgate = z_ref[:, 768:1024], z_ref[:, 1024:1280]
        dya, dyb = dy_ref[:, 0:256], dy_ref[:, 256:512]
        vn, vhat, rstd = _layer_norm(v, lng_ref[...], lnb_ref[...])
        vnb = _mx(vn)
        wm = _mx(jnp.where(_sgu_mask(False), w_ref[...], 0.0))
        wmt = _mx(jnp.where(_sgu_mask(True), wt_ref[...], 0.0))
        mixed = _sgu_apply(wm, vnb, nblk) + jnp.tile(bias_ref[...], (nblk, 1))
        silu, dsilu = _silu_and_grad(gate)
        t1 = u * mixed
        d_gate = dya * t1 * dsilu
        d_t1 = dya * silu
        d_u = d_t1 * mixed
        d_mixed = d_t1 * u
        dmb = _mx(d_mixed)
        d_vn = _sgu_apply(wmt, dmb, nblk)
        grp = _lane_group((128, 256))
        lane = lax.broadcasted_iota(jnp.int32, (128, 128), 1)
        dws = [jnp.zeros((128, 128), F32) for _ in range(4)]
        dbias = jnp.zeros((128, 128), F32)
        for n in range(nblk):
            dm_n, dmb_n, vnb_n = d_mixed[n * 128:(n + 1) * 128], dmb[n * 128:(n + 1) * 128], vnb[n * 128:(n + 1) * 128]
            for hh in range(4):
                dws[hh] = dws[hh] + _dot_nt(jnp.where(grp == hh, dmb_n, jnp.zeros_like(dmb_n)), vnb_n)
                rs = jnp.sum(jnp.where(grp == hh, dm_n, 0.0), axis=-1, keepdims=True)
                dbias = dbias + jnp.where(lane == hh, rs, 0.0)
        _acc(dw_ref, jnp.concatenate(dws, axis=0))
        _acc(db_ref, dbias)
        _acc(dlng_ref, _colsum(d_vn * vhat))
        _acc(dlnb_ref, _colsum(d_vn))
        dvh = d_vn * lng_ref[...]
        d_v = rstd * (dvh - jnp.mean(dvh, axis=-1, keepdims=True) - vhat * jnp.mean(dvh * vhat, axis=-1, keepdims=True))

        @pl.when(i == last)
        def _():
            dw_ref[...] = jnp.where(_sgu_mask(False), dw_ref[...], 0.0)

        halo = jnp.where(i > 0, halo_ref[...], 0.0)
        pooled = _pooled(p, halo, i * tm)
        pooled_b = _mx(pooled)
        mixedp = _dot(pooled_b, pw_ref[...])
        psilu, pdsilu = _silu_and_grad(pgate)
        d_pgate = dyb * (mixedp * ps_ref[...]) * pdsilu
        d_ms = dyb * psilu
        _acc(dps_ref, _colsum(d_ms * mixedp))
        dmpb = _mx(d_ms * ps_ref[...])
        _acc(dpw_ref, _dot_tn(pooled_b, dmpb))
        d_pooled = _dot_nt(dmpb, pw_ref[...])
        dmp_halo = _mx(dyn_ref[...] * _silu_and_grad(zn_ref[...])[0] * ps_ref[...])
        d_pooled_halo = jnp.where(i < last, _dot_nt(dmp_halo, pw_ref[...]), 0.0)
        d_p = _pooled_bwd(d_pooled, d_pooled_halo, i * tm)
        dz_ref[...] = _mx(jnp.concatenate([d_u, d_v, d_gate, d_p, d_pgate], axis=1))

    nxt = lambda i: jnp.minimum((i + 1) * (tm // 16), nt16 - 1)
    return pl.pallas_call(
        body, name=name, grid=(t // tm,),
        in_specs=[zrow, prev_halo, pl.BlockSpec((16, 256), lambda i: (nxt(i), 4)),
                  pl.BlockSpec((tm, 512), lambda i: (i, 0)), pl.BlockSpec((16, 256), lambda i: (nxt(i), 1)),
                  params[0], fixed(512, 128)] + params[1:],
        out_specs=[pl.BlockSpec((tm, Z_MIX), lambda i: (i, 0)), fixed(512, 128), fixed(128, 128), fixed(1, 256),
                   fixed(1, 256), fixed(256, 256), fixed(1, 256)],
        out_shape=[jax.ShapeDtypeStruct((t, Z_MIX), MXU_DTYPE), jax.ShapeDtypeStruct((512, 128), F32),
                   jax.ShapeDtypeStruct((128, 128), F32), jax.ShapeDtypeStruct((1, 256), F32),
                   jax.ShapeDtypeStruct((1, 256), F32), jax.ShapeDtypeStruct((256, 256), F32),
                   jax.ShapeDtypeStruct((1, 256), F32)],
        compiler_params=_cp())(z, z, z, dycat, dycat, sgu_w, sgu_wt, sgu_bias, ln_g, ln_b, pool_wbd, pool_scale)


def _rot_half(x, transpose):
    w = x.shape[1]
    lane = lax.broadcasted_iota(jnp.int32, x.shape, 1) % min(w, 256)
    base = 128 if w >= 256 else 0
    lo = jnp.logical_and(lane >= base, lane < base + 32)
    hi = jnp.logical_and(lane >= base + 32, lane < base + 64)
    up = pltpu.roll(x, w - 32, 1)
    down = pltpu.roll(x, 32, 1)
    if transpose:
        return jnp.where(lo, up, jnp.where(hi, -down, 0.0))
    return jnp.where(lo, -up, jnp.where(hi, down, 0.0))


def _rope(x, c, s):
    return x * c + _rot_half(x, False) * s


def _rope_bwd(dy, c, s):
    return dy * c + _rot_half(dy * s, True)


def _qkv_fwd(z, rc, rs, w_uq, w_ukv, gq, gkv, name, tm=512):
    t = z.shape[0]
    tm = min(tm, t)

    def body(zc_ref, zk_ref, rc_ref, rs_ref, wq_ref, wkv_ref, gq_ref, gkv_ref, q_ref, k_ref, v_ref):
        cq, ckv = zc_ref[:, 0:384], zc_ref[:, 384:640]
        c, s = rc_ref[...], rs_ref[...]
        qn, _ = _rms(cq, gq_ref[...])
        q = _rope(_dot(_mx(qn), wq_ref[...]), jnp.tile(c, (1, N_HEADS)), jnp.tile(s, (1, N_HEADS)))
        kvn, _ = _rms(ckv, gkv_ref[...])
        kv = _dot(_mx(kvn), wkv_ref[...])
        kpe = _rope(zk_ref[...], c[:, 128:256], s[:, 128:256])
        for hh in range(N_HEADS):
            q_ref[hh] = _mx(q[:, hh * QK_PAD:(hh + 1) * QK_PAD])
            k_ref[hh] = _mx(jnp.concatenate([kv[:, hh * 128:(hh + 1) * 128], kpe], axis=1))
            v_ref[hh] = _mx(kv[:, 512 + hh * 128:512 + (hh + 1) * 128])

    fixed = lambda a, b: pl.BlockSpec((a, b), lambda i: (0, 0))
    heads = lambda wd: pl.BlockSpec((N_HEADS, tm, wd), lambda i: (0, i, 0))
    return pl.pallas_call(
        body, name=name, grid=(t // tm,),
        in_specs=[pl.BlockSpec((tm, Z_C), lambda i: (i, Z_MIX // Z_C)),
                  pl.BlockSpec((tm, Z_KR), lambda i: (i, (Z_MIX + Z_C) // Z_KR)),
                  pl.BlockSpec((tm, 256), lambda i: (i, 0)), pl.BlockSpec((tm, 256), lambda i: (i, 0)),
                  fixed(384, 1024), fixed(256, 1024), fixed(1, 384), fixed(1, 256)],
        out_specs=[heads(QK_PAD), heads(QK_PAD), heads(V_DIM)],
        out_shape=[jax.ShapeDtypeStruct((N_HEADS, t, QK_PAD), MXU_DTYPE),
                   jax.ShapeDtypeStruct((N_HEADS, t, QK_PAD), MXU_DTYPE),
                   jax.ShapeDtypeStruct((N_HEADS, t, V_DIM), MXU_DTYPE)],
        compiler_params=_cp())(z, z, rc, rs, w_uq, w_ukv, gq, gkv)


def _qkv_bwd(dq, dk, dv, z, rc, rs, w_uq, w_ukv, gq, gkv, name, tm=512):
    t = z.shape[0]
    tm = min(tm, t)

    def body(dq_ref, dk_ref, dv_ref, zc_ref, rc_ref, rs_ref, wq_ref, wkv_ref, gq_ref, gkv_ref,
             dzc_ref, dzk_ref, dwq_ref, dwkv_ref, dgq_ref, dgkv_ref):
        _zero_when(pl.program_id(0) == 0, dwq_ref, dwkv_ref, dgq_ref, dgkv_ref)
        cq, ckv = zc_ref[:, 0:384], zc_ref[:, 384:640]
        c, s = rc_ref[...], rs_ref[...]
        dq_all = jnp.concatenate([dq_ref[hh] for hh in range(N_HEADS)], axis=1)
        dqp = _mx(_rope_bwd(dq_all, jnp.tile(c, (1, N_HEADS)), jnp.tile(s, (1, N_HEADS))))
        qn, rq = _rms(cq, gq_ref[...])
        _acc(dwq_ref, _dot_tn(_mx(qn), dqp))
        d_cq, dgq_t = _rms_bwd(cq, rq, gq_ref[...], _dot_nt(dqp, wq_ref[...]))
        _acc(dgq_ref, _colsum(dgq_t))
        dkpe = dk_ref[0][:, 128:256]
        for hh in range(1, N_HEADS):
            dkpe = dkpe + dk_ref[hh][:, 128:256]
        dzk_ref[...] = _mx(_rope_bwd(dkpe, c[:, 128:256], s[:, 128:256]))
        dkv = _mx(jnp.concatenate([dk_ref[hh][:, 0:128] for hh in range(N_HEADS)]
                                  + [dv_ref[hh] for hh in range(N_HEADS)], axis=1))
        kvn, rkv = _rms(ckv, gkv_ref[...])
        _acc(dwkv_ref, _dot_tn(_mx(kvn), dkv))
        d_ckv, dgkv_t = _rms_bwd(ckv, rkv, gkv_ref[...], _dot_nt(dkv, wkv_ref[...]))
        _acc(dgkv_ref, _colsum(dgkv_t))
        dzc_ref[...] = _mx(jnp.concatenate([d_cq, d_ckv], axis=1))

    fixed = lambda a, b: pl.BlockSpec((a, b), lambda i: (0, 0))
    heads = lambda wd: pl.BlockSpec((N_HEADS, tm, wd), lambda i: (0, i, 0))
    return pl.pallas_call(
        body, name=name, grid=(t // tm,),
        in_specs=[heads(QK_PAD), heads(QK_PAD), heads(V_DIM), pl.BlockSpec((tm, Z_C), lambda i: (i, Z_MIX // Z_C)),
                  pl.BlockSpec((tm, 256), lambda i: (i, 0)), pl.BlockSpec((tm, 256), lambda i: (i, 0)),
                  fixed(384, 1024), fixed(256, 1024), fixed(1, 384), fixed(1, 256)],
        out_specs=[pl.BlockSpec((tm, Z_C), lambda i: (i, 0)), pl.BlockSpec((tm, Z_KR), lambda i: (i, 0)),
                   fixed(384, 1024), fixed(256, 1024), fixed(1, 384), fixed(1, 256)],
        out_shape=[jax.ShapeDtypeStruct((t, Z_C), MXU_DTYPE), jax.ShapeDtypeStruct((t, Z_KR), MXU_DTYPE),
                   jax.ShapeDtypeStruct((384, 1024), F32), jax.ShapeDtypeStruct((256, 1024), F32),
                   jax.ShapeDtypeStruct((1, 384), F32), jax.ShapeDtypeStruct((1, 256), F32)],
        compiler_params=_cp())(dq, dk, dv, z, rc, rs, w_uq, w_ukv, gq, gkv)


def _loop_in_long_trips(n, body):
    def two(t, carry):
        return body(2 * t + 1, body(2 * t, carry))

    def four(t, carry):
        return two(2 * t + 1, two(2 * t, carry))

    lax.fori_loop(0, n // 4, four, 0)
    lax.fori_loop(2 * (n // 4), n // 2, two, 0)
    lax.fori_loop(2 * (n // 2), n, body, 0)


def _init_mask_bias(bias_ref):
    _, tq, tk = bias_ref.shape
    r = lax.broadcasted_iota(jnp.int32, (tq, tk), 0) // 64
    c = lax.broadcasted_iota(jnp.int32, (tq, tk), 1) // 64
    bias_ref[0] = jnp.zeros((tq, tk), F32)
    for d in range(tq // tk):
        bias_ref[1 + d] = jnp.where(c + d * (tk // 64) <= r, 0.0, NEG_INF)


def _gate_block(tq):
    return pl.BlockSpec((tq, 128), lambda h, i: (i, (Z_MIX + Z_C + Z_KR) // 128 + h))


def _attn_fwd(qh, kh, vh, z, name, tq=1024, tk=512):
    t = qh.shape[1]
    tq = min(tq, t)
    tk = min(tk, tq)
    ratio = tq // tk

    def body(q_ref, g_ref, k_hbm, v_hbm, o_ref, yc_ref, lse_ref, k_v, v_v, m_s, acc_s, s_a, s_b, mx_a, mx_b, bias_s,
             sem):
        h, i = pl.program_id(0), pl.program_id(1)

        @pl.when(i == 0)
        def _():
            ck = pltpu.make_async_copy(k_hbm.at[h], k_v, sem.at[0])
            cv = pltpu.make_async_copy(v_hbm.at[h], v_v.at[:, 0:V_DIM], sem.at[1])
            ck.start()
            cv.start()
            v_v[:, V_DIM:2 * V_DIM] = jnp.ones((t, V_DIM), MXU_DTYPE)
            _init_mask_bias(bias_s)
            ck.wait()
            cv.wait()

        q = q_ref[...]
        m_s[...] = jnp.full(m_s.shape, NEG_INF, F32)
        acc_s[...] = jnp.zeros(acc_s.shape, F32)

        last = ratio * (i + 1) - 1

        def keys(j):
            return pl.ds(pl.multiple_of(j * tk, tk), tk)

        def scores(s_ref, mx_ref, j):
            s = _dot_nt(q, k_v[keys(j), :]) * (SCALE * LOG2E) + bias_s[jnp.maximum(j - ratio * i + 1, 0)]
            s_ref[...] = s
            mx_ref[...] = jnp.broadcast_to(jnp.max(s, axis=-1, keepdims=True), mx_ref.shape)

        def softmax_pv(s_ref, mx_ref, j):
            m_old = m_s[...]
            m_new = jnp.maximum(m_old, mx_ref[...])
            p = jnp.exp2(s_ref[...] - jnp.tile(m_new, (1, tk // 128)))
            alpha = jnp.exp2(m_old - m_new)
            m_s[...] = m_new
            acc_s[...] = jnp.tile(alpha, (1, 2)) * acc_s[...] + _dot(_mx(p), v_v[keys(j), :])

        scores(s_a, mx_a, 0)

        def pair(pp, carry):
            scores(s_b, mx_b, 2 * pp + 1)
            softmax_pv(s_a, mx_a, 2 * pp)
            scores(s_a, mx_a, jnp.minimum(2 * pp + 2, last))
            softmax_pv(s_b, mx_b, 2 * pp + 1)
            return carry

        _loop_in_long_trips((last + 1) // 2, pair)
        if ratio % 2 == 1:
            @pl.when(last % 2 == 0)
            def _():
                softmax_pv(s_a, mx_a, last)

        l = acc_s[:, V_DIM:2 * V_DIM]
        o = acc_s[:, 0:V_DIM] / l
        o_ref[...] = o
        yc_ref[...] = _mx(o * _silu_and_grad(g_ref[...])[0])
        lse_ref[...] = m_s[...] + jnp.log2(l)

    return pl.pallas_call(
        body, name=name, grid=(N_HEADS, t // tq),
        in_specs=[pl.BlockSpec((None, tq, QK_PAD), lambda h, i: (h, i, 0)), _gate_block(tq), ANY, ANY],
        out_specs=[pl.BlockSpec((tq, 128), lambda h, i: (i, h)), pl.BlockSpec((tq, 128), lambda h, i: (i, h)),
                   pl.BlockSpec((None, tq, 128), lambda h, i: (h, i, 0))],
        out_shape=[jax.ShapeDtypeStruct((t, N_HEADS * V_DIM), F32), jax.ShapeDtypeStruct((t, N_HEADS * V_DIM), MXU_DTYPE),
                   jax.ShapeDtypeStruct((N_HEADS, t, 128), F32)],
        scratch_shapes=[pltpu.VMEM((t, QK_PAD), MXU_DTYPE), pltpu.VMEM((t, 2 * V_DIM), MXU_DTYPE),
                        pltpu.VMEM((tq, 128), F32), pltpu.VMEM((tq, 2 * V_DIM), F32),
                        pltpu.VMEM((tq, tk), F32), pltpu.VMEM((tq, tk), F32), pltpu.VMEM((tq, 128), F32),
                        pltpu.VMEM((tq, 128), F32), pltpu.VMEM((ratio + 1, tq, tk), F32),
                        pltpu.SemaphoreType.DMA((2,))],
        compiler_params=_cp(2))(qh, z, kh, vh)


def _attn_bwd(qh, kh, vh, o, lse, dycat, z, name, tq=512):
    t = qh.shape[1]
    tq = min(tq, t)
    nq = t // tq

    def body(q_ref, o_ref, lse_ref, dy_ref, g_ref, k_hbm, v_hbm, dq_ref, dgate_ref, dk_hbm, dv_hbm,
             k_v, v_v, dk_acc, dv_acc, dq_acc, delta_s, s_a, dp_a, s_b, dp_b, bias_s, sem):
        h, i = pl.program_id(0), pl.program_id(1)

        @pl.when(i == 0)
        def _():
            ck = pltpu.make_async_copy(k_hbm.at[h], k_v, sem.at[0])
            cv = pltpu.make_async_copy(v_hbm.at[h], v_v, sem.at[1])
            ck.start()
            cv.start()
            _init_mask_bias(bias_s)
            dk_acc[...] = jnp.zeros(dk_acc.shape, F32)
            dv_acc[...] = jnp.zeros(dv_acc.shape, F32)
            ck.wait()
            cv.wait()

        gate, dy, of = g_ref[...], dy_ref[...], o_ref[...]
        silu, dsilu = _silu_and_grad(gate)
        do = dy * silu
        delta = jnp.sum(do * of, axis=-1, keepdims=True)
        dgate_ref[...] = _mx(dy * of * dsilu)
        dob = _mx(do)
        q = q_ref[...]
        delta_s[...] = jnp.broadcast_to(delta, delta_s.shape)
        dq_acc[...] = jnp.zeros(dq_acc.shape, F32)

        def keys(j):
            return pl.ds(pl.multiple_of(j * tq, tq), tq)

        def scores(s_ref, dp_ref, j):
            s = _dot_nt(q, k_v[keys(j), :]) * (SCALE * LOG2E) + bias_s[(j == i).astype(jnp.int32)]
            s_ref[...] = s - jnp.tile(lse_ref[...], (1, tq // 128))
            dp_ref[...] = _dot_nt(dob, v_v[keys(j), :]) - jnp.tile(delta_s[...], (1, tq // 128))

        def grads(s_ref, dp_ref, j):
            ks = keys(j)
            p = jnp.exp2(s_ref[...])
            ds = p * dp_ref[...] * SCALE
            pb, dsb = _mx(p), _mx(ds)
            dq_acc[...] += _dot(dsb, k_v[ks, :])
            dk_acc[ks, :] += _dot_tn(dsb, q)
            dv_acc[ks, :] += _dot_tn(pb, dob)

        scores(s_a, dp_a, 0)

        def pair(pp, carry):
            scores(s_b, dp_b, 2 * pp + 1)
            grads(s_a, dp_a, 2 * pp)
            scores(s_a, dp_a, jnp.minimum(2 * pp + 2, i))
            grads(s_b, dp_b, 2 * pp + 1)
            return carry

        _loop_in_long_trips((i + 1) // 2, pair)

        @pl.when(i % 2 == 0)
        def _():
            grads(s_a, dp_a, i)

        dq_ref[...] = dq_acc[...]

        @pl.when(i == nq - 1)
        def _():
            ck = pltpu.make_async_copy(dk_acc, dk_hbm.at[h], sem.at[0])
            cv = pltpu.make_async_copy(dv_acc, dv_hbm.at[h], sem.at[1])
            ck.start()
            cv.start()
            ck.wait()
            cv.wait()

    return pl.pallas_call(
        body, name=name, grid=(N_HEADS, nq),
        in_specs=[pl.BlockSpec((None, tq, QK_PAD), lambda h, i: (h, i, 0)),
                  pl.BlockSpec((tq, 128), lambda h, i: (i, h)),
                  pl.BlockSpec((None, tq, 128), lambda h, i: (h, i, 0)),
                  pl.BlockSpec((tq, 128), lambda h, i: (i, N_HEADS + h)), _gate_block(tq), ANY, ANY],
        out_specs=[pl.BlockSpec((None, tq, QK_PAD), lambda h, i: (h, i, 0)),
                   pl.BlockSpec((tq, 128), lambda h, i: (i, h)), ANY, ANY],
        out_shape=[jax.ShapeDtypeStruct((N_HEADS, t, QK_PAD), F32), jax.ShapeDtypeStruct((t, Z_GATE), MXU_DTYPE),
                   jax.ShapeDtypeStruct((N_HEADS, t, QK_PAD), F32), jax.ShapeDtypeStruct((N_HEADS, t, V_DIM), F32)],
        scratch_shapes=[pltpu.VMEM((t, QK_PAD), MXU_DTYPE), pltpu.VMEM((t, V_DIM), MXU_DTYPE),
                        pltpu.VMEM((t, QK_PAD), F32), pltpu.VMEM((t, V_DIM), F32), pltpu.VMEM((tq, QK_PAD), F32),
                        pltpu.VMEM((tq, 128), F32)] + [pltpu.VMEM((tq, tq), F32)] * 4
        + [pltpu.VMEM((2, tq, tq), F32), pltpu.SemaphoreType.DMA((2,))],
        compiler_params=_cp(2))(qh, o, lse, dycat, z, kh, vh)


def _out_proj_fwd(yab, yc, w, x, g, target, name, tm=512):
    t, d = x.shape
    tm = min(tm, t)
    is_last = target is not None

    def body(*refs):
        if is_last:
            yab_ref, yc_ref, w_ref, x_ref, g_ref, t_ref, y_ref, dout_ref, loss_ref = refs
            _zero_when(pl.program_id(0) == 0, loss_ref)
        else:
            yab_ref, yc_ref, w_ref, x_ref, g_ref, y_ref, out_ref = refs
        y = _dot(jnp.concatenate([yab_ref[...], yc_ref[...]], axis=1), w_ref[...])
        y_ref[...] = y
        out = x_ref[...] + _rms(y, g_ref[...])[0]
        if is_last:
            diff = out - t_ref[...]
            dout_ref[...] = diff * (1.0 / d)
            part = jnp.sum(jnp.sum(diff * diff, axis=-1, keepdims=True), axis=0, keepdims=True) * (0.5 / d)
            _acc(loss_ref, jnp.broadcast_to(part, (1, 128)))
        else:
            out_ref[...] = out

    row = lambda wd: pl.BlockSpec((tm, wd), lambda i: (i, 0))
    fixed = lambda a, b: pl.BlockSpec((a, b), lambda i: (0, 0))
    in_specs = [row(512), row(512), fixed(d, d), row(d), fixed(1, d)]
    args = [yab, yc, w, x, g]
    out_specs = [row(d), row(d)]
    out_shape = [jax.ShapeDtypeStruct((t, d), F32), jax.ShapeDtypeStruct((t, d), F32)]
    if is_last:
        in_specs.append(row(d))
        args.append(target)
        out_specs.append(fixed(1, 128))
        out_shape.append(jax.ShapeDtypeStruct((1, 128), F32))
    return pl.pallas_call(body, name=name, grid=(t // tm,), in_specs=in_specs, out_specs=out_specs,
                          out_shape=out_shape, compiler_params=_cp())(*args)


def _out_proj_bwd(dout, y, yab, yc, w, g, name, tm=512):
    t, d = y.shape
    tm = min(tm, t)

    def body(dout_ref, y_ref, yab_ref, yc_ref, w_ref, g_ref, dycat_ref, dw_ref, dg_ref):
        _zero_when(pl.program_id(0) == 0, dw_ref, dg_ref)
        y = y_ref[...]
        r = lax.rsqrt(jnp.mean(y * y, axis=-1, keepdims=True) + EPS)
        dy, dgt = _rms_bwd(y, r, g_ref[...], dout_ref[...])
        _acc(dg_ref, _colsum(dgt))
        dyb = _mx(dy)
        _acc(dw_ref, _dot_tn(jnp.concatenate([yab_ref[...], yc_ref[...]], axis=1), dyb))
        dycat_ref[...] = _dot_nt(dyb, w_ref[...])

    row = lambda wd: pl.BlockSpec((tm, wd), lambda i: (i, 0))
    fixed = lambda a, b: pl.BlockSpec((a, b), lambda i: (0, 0))
    return pl.pallas_call(
        body, name=name, grid=(t // tm,),
        in_specs=[row(d), row(d), row(512), row(512), fixed(d, d), fixed(1, d)],
        out_specs=[row(d), fixed(d, d), fixed(1, d)],
        out_shape=[jax.ShapeDtypeStruct((t, d), F32), jax.ShapeDtypeStruct((d, d), F32),
                   jax.ShapeDtypeStruct((1, d), F32)],
        compiler_params=_cp())(dout, y, yab, yc, w, g)


def _mesh_pos():
    return lax.axis_index("x"), lax.axis_index("y"), lax.axis_index("c")


def _remote(src, dst, send_sem, recv_sem, to):
    return pltpu.make_async_remote_copy(src_ref=src, dst_ref=dst, send_sem=send_sem, recv_sem=recv_sem,
                                        device_id=to, device_id_type=MESH)


CHUNK_ROWS = 256


def _pieces(rows):
    return [(s, min(CHUNK_ROWS, rows - s)) for s in range(0, rows, CHUNK_ROWS)]


def _piece_table(shapes):
    return [(a, s, sz) for a, shp in enumerate(shapes) for s, sz in _pieces(shp[-2])]


def _gather_weights(shards):
    n = len(shards)
    table = _piece_table([s.shape for s in shards])
    npc = len(table)

    def body(*refs):
        ins, outs = refs[:n], refs[n:2 * n]
        send_sems, recv_sems, fwd_send, fwd_recv = refs[2 * n:]
        x, y, c = _mesh_pos()
        me, sibling = (x, y, c), (x, y, 1 - c)
        chips = [(1 - x, y), (x, 1 - y), (1 - x, 1 - y)]
        slot = lambda cx, cy, layer: 2 * (2 * cx + cy) + layer
        first = []
        for a in range(n):
            for j, (cx, cy) in enumerate(chips):
                first.append(_remote(ins[a].at[c], outs[a].at[slot(x, y, c)], send_sems.at[a, j], recv_sems.at[a, j],
                                     (cx, cy, c)))
                first[-1].start()
        passed = []
        for j, (cx, cy) in enumerate(chips):
            for a in range(n):
                blk = outs[a].at[slot(cx, cy, c)]
                _remote(blk, blk, send_sems.at[a, j], recv_sems.at[a, j], me).wait_recv()
            for q, (a, s, sz) in enumerate(table):
                rows = outs[a].at[slot(cx, cy, c), pl.ds(s, sz)]
                passed.append(_remote(rows, rows, fwd_send.at[j, q], fwd_recv.at[j, q], sibling))
                passed[-1].start()
        for j, (cx, cy) in enumerate(chips):
            for q, (a, s, sz) in enumerate(table):
                rows = outs[a].at[slot(cx, cy, 1 - c), pl.ds(s, sz)]
                _remote(rows, rows, fwd_send.at[j, q], fwd_recv.at[j, q], me).wait_recv()
        for cp in first + passed:
            cp.wait_send()

    return pl.pallas_call(
        body, name="gather_weights", in_specs=[ANY] * n, out_specs=[ANY] * n,
        out_shape=[jax.ShapeDtypeStruct((8,) + s.shape[1:], s.dtype) for s in shards],
        scratch_shapes=[pltpu.SemaphoreType.DMA((n, 3)), pltpu.SemaphoreType.DMA((n, 3)),
                        pltpu.SemaphoreType.DMA((3, npc)), pltpu.SemaphoreType.DMA((3, npc))])(*shards)


def _pair_exchange(parts, common):
    n = len(parts)
    table = _piece_table([p.shape for p in parts] + [common.shape])
    npc = len(table)

    def body(*refs):
        ins, outs = refs[:n + 1], refs[n + 1:2 * n + 2]
        send_sems, recv_sems = refs[2 * n + 2:]
        x, y, c = _mesh_pos()
        sent = []
        for k in range(4):
            for q, (a, s, sz) in enumerate(table):
                if a == n and k > 0:
                    continue
                src = ins[a].at[2 * k + 1 - c, pl.ds(s, sz)] if a < n else ins[a].at[pl.ds(s, sz)]
                dst = outs[a].at[k, pl.ds(s, sz)] if a < n else outs[a].at[pl.ds(s, sz)]
                sent.append(_remote(src, dst, send_sems.at[k, q], recv_sems.at[k, q], (x, y, 1 - c)))
                sent[-1].start()
        for k in range(4):
            for q, (a, s, sz) in enumerate(table):
                if a == n and k > 0:
                    continue
                dst = outs[a].at[k, pl.ds(s, sz)] if a < n else outs[a].at[pl.ds(s, sz)]
                _remote(dst, dst, send_sems.at[k, q], recv_sems.at[k, q], (x, y, c)).wait_recv()
        for cp in sent:
            cp.wait_send()

    return pl.pallas_call(
        body, name="grad_pair_exchange", in_specs=[ANY] * (n + 1), out_specs=[ANY] * (n + 1),
        out_shape=[jax.ShapeDtypeStruct((4,) + p.shape[1:], p.dtype) for p in parts]
        + [jax.ShapeDtypeStruct(common.shape, common.dtype)],
        scratch_shapes=[pltpu.SemaphoreType.DMA((4, npc)), pltpu.SemaphoreType.DMA((4, npc))])(*parts, common)


def _chip_exchange(parts, common):
    n = len(parts)
    table = _piece_table([p.shape for p in parts] + [common.shape])
    npc = len(table)

    def body(*refs):
        ins, outs = refs[:n + 1], refs[n + 1:2 * n + 2]
        send_sems, recv_sems = refs[2 * n + 2:]
        x, y, c = _mesh_pos()
        mine = 2 * x + y
        chips = [(1 - x, y), (x, 1 - y), (1 - x, 1 - y)]
        src = lambda a, k: ins[a].at[k] if a < n else ins[a]
        sent = []
        for j, (cx, cy) in enumerate(chips):
            for q, (a, s, sz) in enumerate(table):
                sent.append(_remote(src(a, 2 * cx + cy).at[pl.ds(s, sz)], outs[a].at[mine, pl.ds(s, sz)],
                                    send_sems.at[j, q], recv_sems.at[j, q], (cx, cy, c)))
                sent[-1].start()
        for j, (cx, cy) in enumerate(chips):
            for q, (a, s, sz) in enumerate(table):
                dst = outs[a].at[2 * cx + cy, pl.ds(s, sz)]
                _remote(dst, dst, send_sems.at[j, q], recv_sems.at[j, q], (x, y, c)).wait_recv()
        for cp in sent:
            cp.wait_send()

    return pl.pallas_call(
        body, name="grad_chip_exchange", in_specs=[ANY] * (n + 1), out_specs=[ANY] * (n + 1),
        out_shape=[jax.ShapeDtypeStruct(p.shape, p.dtype) for p in parts]
        + [jax.ShapeDtypeStruct((4,) + common.shape, common.dtype)],
        scratch_shapes=[pltpu.SemaphoreType.DMA((3, npc)), pltpu.SemaphoreType.DMA((3, npc))])(*parts, common)


def _sibling_exchange(sums):
    n = len(sums)
    table = _piece_table([s.shape for s in sums])
    npc = len(table)

    def body(*refs):
        ins, outs = refs[:n], refs[n:2 * n]
        send_sems, recv_sems = refs[2 * n:]
        x, y, c = _mesh_pos()
        sent = []
        for q, (a, s, sz) in enumerate(table):
            sent.append(_remote(ins[a].at[pl.ds(s, sz)], outs[a].at[pl.ds(s, sz)], send_sems.at[q], recv_sems.at[q],
                                (x, y, 1 - c)))
            sent[-1].start()
        for q, (a, s, sz) in enumerate(table):
            dst = outs[a].at[pl.ds(s, sz)]
            _remote(dst, dst, send_sems.at[q], recv_sems.at[q], (x, y, c)).wait_recv()
        for cp in sent:
            cp.wait_send()

    return pl.pallas_call(
        body, name="sibling_exchange", in_specs=[ANY] * n, out_specs=[ANY] * n,
        out_shape=[jax.ShapeDtypeStruct(s.shape, s.dtype) for s in sums],
        scratch_shapes=[pltpu.SemaphoreType.DMA((npc,)), pltpu.SemaphoreType.DMA((npc,))])(*sums)


def _pair_sum(mine, got, name, tr):
    _, r, c = got.shape
    tr = min(tr, r)

    def body(p_ref, g_ref, o_ref, w_ref):
        total = p_ref[...] + g_ref[...]
        o_ref[...] = total
        w_ref[...] = total.astype(WIRE_DTYPE)

    blk = pl.BlockSpec((None, tr, c), lambda k, i: (k, i, 0))
    return pl.pallas_call(
        body, name=name, grid=(4, r // tr), in_specs=[blk, blk], out_specs=[blk, blk],
        out_shape=[jax.ShapeDtypeStruct(got.shape, F32), jax.ShapeDtypeStruct(got.shape, WIRE_DTYPE)],
        compiler_params=_cp(2))(mine, got)


def _sum_chips(own, recv, name, tr):
    r, c = own.shape
    tr = min(tr, r)

    def body(own_ref, r_ref, o_ref):
        chip = 2 * lax.axis_index("x") + lax.axis_index("y")
        own_blk = own_ref[...]
        acc = jnp.where(chip == 0, own_blk, r_ref[0].astype(F32))
        for k in range(1, 4):
            acc = acc + jnp.where(chip == k, own_blk, r_ref[k].astype(F32))
        o_ref[...] = acc

    return pl.pallas_call(
        body, name=name, grid=(r // tr,),
        in_specs=[pl.BlockSpec((tr, c), lambda i: (i, 0)), pl.BlockSpec((4, tr, c), lambda i: (0, i, 0))],
        out_specs=pl.BlockSpec((tr, c), lambda i: (i, 0)),
        out_shape=jax.ShapeDtypeStruct((r, c), F32), compiler_params=_cp())(own, recv)


def _sum_leading(parts, name, tr):
    nlead, r, c = parts.shape
    tr = min(tr, r)

    def body(p_ref, o_ref):
        acc = p_ref[0]
        for j in range(1, nlead):
            acc = acc + p_ref[j]
        o_ref[...] = acc

    return pl.pallas_call(
        body, name=name, grid=(r // tr,),
        in_specs=[pl.BlockSpec((nlead, tr, c), lambda i: (0, i, 0))], out_specs=pl.BlockSpec((tr, c), lambda i: (i, 0)),
        out_shape=jax.ShapeDtypeStruct((r, c), parts.dtype), compiler_params=_cp())(parts)


def _adamw(w, g, m, v, name, tr):
    r, c = w.shape
    tr = min(tr, r)

    def body(w_ref, g_ref, m_ref, v_ref, d_ref, nm_ref, nv_ref):
        gg = g_ref[...]
        nm = ADAM_B1 * m_ref[...] + (1.0 - ADAM_B1) * gg
        nv = ADAM_B2 * v_ref[...] + (1.0 - ADAM_B2) * jnp.square(gg)
        m_hat = nm / (1.0 - ADAM_B1 ** ADAM_STEP)
        v_hat = nv / (1.0 - ADAM_B2 ** ADAM_STEP)
        d_ref[...] = -ADAM_LR * (m_hat / (jnp.sqrt(v_hat) + ADAM_EPS) + ADAM_WD * w_ref[...])
        nm_ref[...] = nm
        nv_ref[...] = nv

    blk = pl.BlockSpec((tr, c), lambda i: (i, 0))
    return pl.pallas_call(
        body, name=name, grid=(r // tr,), in_specs=[blk] * 4, out_specs=[blk] * 3,
        out_shape=[jax.ShapeDtypeStruct((r, c), F32)] * 3, compiler_params=_cp())(w, g, m, v)


def _rope_tables(positions):
    inv_freq = ROPE_BASE ** (-jnp.arange(0, 64, 2, dtype=F32) / 64)
    ang = positions.astype(F32)[:, None] * inv_freq
    cos, sin = jnp.cos(ang), jnp.sin(ang)
    t = positions.shape[0]
    rc = jnp.concatenate([jnp.ones((t, 128), F32), cos, cos, jnp.ones((t, 64), F32)], axis=1)
    rs = jnp.concatenate([jnp.zeros((t, 128), F32), sin, sin, jnp.zeros((t, 64), F32)], axis=1)
    return rc, rs


def _layer_params(l, w_in, w_uq, w_ukv, w_out, small):
    p = {}
    p["w_in"] = jnp.concatenate([w_in[l][:, :1984], jnp.zeros((1024, 64), w_in.dtype), w_in[l][:, 1984:]], axis=1)
    p["w_uq"] = jnp.pad(w_uq[l].reshape(384, 4, 192), ((0, 0), (0, 0), (0, 64))).reshape(384, 1024)
    p["w_ukv"] = w_ukv[l].reshape(256, 4, 2, 128).transpose(0, 2, 1, 3).reshape(256, 1024)
    p["w_out"] = w_out[l]
    p["pre_g"] = small["pre_norm_g"][l][None]
    p["post_g"] = small["post_norm_g"][l][None]
    p["sgu_w"] = small["sgu_w"][l].reshape(512, 128)
    p["sgu_wt"] = small["sgu_w"][l].transpose(0, 2, 1).reshape(512, 128)
    p["sgu_bias"] = jnp.repeat(small["sgu_b"][l].T, 64, axis=1)
    p["ln_g"] = small["sgu_ln_g"][l][None]
    p["ln_b"] = small["sgu_ln_b"][l][None]
    p["pool_wbd"] = _mx(jax.scipy.linalg.block_diag(*[small["pool_w"][l][gi] for gi in range(4)]))
    p["pool_scale"] = small["pool_scale"][l][None]
    p["gq"] = small["q_norm_g"][l][None]
    p["gkv"] = small["kv_norm_g"][l][None]
    return p


def _layer_fwd(l, x, p, rc, rs, target):
    z, h = _in_proj_fwd(x, p["pre_g"], p["w_in"], f"in_proj_fwd_{l}")
    yab = _mix_fwd(z, p["sgu_w"], p["sgu_bias"], p["ln_g"], p["ln_b"], p["pool_wbd"], p["pool_scale"], f"mix_fwd_{l}")
    qh, kh, vh = _qkv_fwd(z, rc, rs, p["w_uq"], p["w_ukv"], p["gq"], p["gkv"], f"qkv_fwd_{l}")
    o, yc, lse = _attn_fwd(qh, kh, vh, z, f"attn_fwd_{l}")
    outs = _out_proj_fwd(yab, yc, p["w_out"], x, p["post_g"], target, f"out_proj_fwd_{l}")
    saved = dict(x=x, z=z, h=h, yab=yab, qh=qh, kh=kh, vh=vh, o=o, yc=yc, lse=lse, y=outs[0])
    return saved, outs[1:]


def _layer_bwd(l, dout, sv, p, rc, rs):
    dycat, dw_out, dpost = _out_proj_bwd(dout, sv["y"], sv["yab"], sv["yc"], p["w_out"], p["post_g"], f"out_proj_bwd_{l}")
    dq, dgate, dk, dv = _attn_bwd(sv["qh"], sv["kh"], sv["vh"], sv["o"], sv["lse"], dycat, sv["z"], f"attn_bwd_{l}")
    dzc, dzk, dwq, dwkv, dgq, dgkv = _qkv_bwd(dq, dk, dv, sv["z"], rc, rs, p["w_uq"], p["w_ukv"], p["gq"], p["gkv"],
                                              f"qkv_bwd_{l}")
    dzm, dsw, dsb, dlng, dlnb, dpw, dps = _mix_bwd(sv["z"], dycat, p["sgu_w"], p["sgu_wt"], p["sgu_bias"], p["ln_g"],
                                                   p["ln_b"], p["pool_wbd"], p["pool_scale"], f"mix_bwd_{l}")
    dx, dw_in, dpre = _in_proj_bwd(dzm, dzc, dzk, dgate, sv["h"], sv["x"], dout, p["w_in"], p["pre_g"], f"in_proj_bwd_{l}")
    grads = {
        "pre_norm_g": dpre[0], "post_norm_g": dpost[0],
        "w_in": jnp.concatenate([dw_in[:, :1984], dw_in[:, 2048:]], axis=1),
        "sgu_w": dsw.reshape(4, 128, 128), "sgu_b": dsb[:, :4].T, "sgu_ln_g": dlng[0], "sgu_ln_b": dlnb[0],
        "pool_w": jnp.stack([dpw[64 * gi:64 * gi + 64, 64 * gi:64 * gi + 64] for gi in range(4)]),
        "pool_scale": dps[0], "q_norm_g": dgq[0],
        "w_uq": dwq.reshape(384, 4, 256)[:, :, :192].reshape(384, 768), "kv_norm_g": dgkv[0],
        "w_ukv": dwkv.reshape(256, 2, 4, 128).transpose(0, 2, 1, 3).reshape(256, 1024), "w_out": dw_out,
    }
    return dx, grads


SMALL_NAMES = ["pre_norm_g", "post_norm_g", "sgu_w", "sgu_b", "sgu_ln_g", "sgu_ln_b", "pool_w", "pool_scale",
               "q_norm_g", "kv_norm_g"]
BIG_NAMES = ["w_in", "w_uq", "w_ukv", "w_out"]
WEIGHT_NAMES = ["pre_norm_g", "post_norm_g", "w_in", "sgu_w", "sgu_b", "sgu_ln_g", "sgu_ln_b", "pool_w", "pool_scale",
                "q_norm_g", "w_uq", "kv_norm_g", "w_ukv", "w_out"]


def _local_step(x, positions, target, w_in, w_uq, w_ukv, w_out, small):
    rc, rs = _rope_tables(positions)
    params = [_layer_params(l, w_in, w_uq, w_ukv, w_out, small) for l in range(DEPTH)]
    saved = []
    for l in range(DEPTH):
        sv, outs = _layer_fwd(l, x, params[l], rc, rs, target if l == DEPTH - 1 else None)
        saved.append(sv)
        if l < DEPTH - 1:
            x = outs[0]
    dout, loss = outs
    grads = [None] * DEPTH
    for l in reversed(range(DEPTH)):
        dout, grads[l] = _layer_bwd(l, dout, saved[l], params[l], rc, rs)
    return loss[0, 0], dout, {k: jnp.stack([grads[l][k] for l in range(DEPTH)]) for k in WEIGHT_NAMES}


def _pack_small(tree, extra=None):
    pieces = [tree[k].reshape(-1) for k in SMALL_NAMES]
    pieces.append(jnp.zeros((1,), F32) if extra is None else extra.reshape(1))
    flat = jnp.concatenate(pieces)
    rows = -(-flat.shape[0] // 1024) * 8
    return jnp.pad(flat, (0, rows * 128 - flat.shape[0])).reshape(rows, 128)


def _unpack_small(packed, like):
    flat = packed.reshape(-1)
    out, off = {}, 0
    for k in SMALL_NAMES:
        size = like[k].size
        out[k] = flat[off:off + size].reshape(like[k].shape)
        off += size
    return out, flat[off]


def kernel(x, positions, pre_norm_g, post_norm_g, w_in, sgu_w, sgu_b, sgu_ln_g, sgu_ln_b, pool_w, pool_scale, q_norm_g, w_uq, kv_norm_g, w_ukv, w_out, loss_target, m_pre_norm_g, m_post_norm_g, m_w_in, m_sgu_w, m_sgu_b, m_sgu_ln_g, m_sgu_ln_b, m_pool_w, m_pool_scale, m_q_norm_g, m_w_uq, m_kv_norm_g, m_w_ukv, m_w_out, v_pre_norm_g, v_post_norm_g, v_w_in, v_sgu_w, v_sgu_b, v_sgu_ln_g, v_sgu_ln_b, v_pool_w, v_pool_scale, v_q_norm_g, v_w_uq, v_kv_norm_g, v_w_ukv, v_w_out):
    w = dict(pre_norm_g=pre_norm_g, post_norm_g=post_norm_g, w_in=w_in, sgu_w=sgu_w, sgu_b=sgu_b, sgu_ln_g=sgu_ln_g,
             sgu_ln_b=sgu_ln_b, pool_w=pool_w, pool_scale=pool_scale, q_norm_g=q_norm_g, w_uq=w_uq, kv_norm_g=kv_norm_g,
             w_ukv=w_ukv, w_out=w_out)
    m = dict(pre_norm_g=m_pre_norm_g, post_norm_g=m_post_norm_g, w_in=m_w_in, sgu_w=m_sgu_w, sgu_b=m_sgu_b,
             sgu_ln_g=m_sgu_ln_g, sgu_ln_b=m_sgu_ln_b, pool_w=m_pool_w, pool_scale=m_pool_scale, q_norm_g=m_q_norm_g,
             w_uq=m_w_uq, kv_norm_g=m_kv_norm_g, w_ukv=m_w_ukv, w_out=m_w_out)
    v = dict(pre_norm_g=v_pre_norm_g, post_norm_g=v_post_norm_g, w_in=v_w_in, sgu_w=v_sgu_w, sgu_b=v_sgu_b,
             sgu_ln_g=v_sgu_ln_g, sgu_ln_b=v_sgu_ln_b, pool_w=v_pool_w, pool_scale=v_pool_scale, q_norm_g=v_q_norm_g,
             w_uq=v_w_uq, kv_norm_g=v_kv_norm_g, w_ukv=v_w_ukv, w_out=v_w_out)

    core = lax.axis_index("c")
    chip = 2 * lax.axis_index("x") + lax.axis_index("y")
    shards = [_mx(w[k]) for k in BIG_NAMES]
    gathered = _gather_weights(shards)
    g_in, g_uq, g_ukv, g_out = [lax.dynamic_update_slice(g, s, (2 * chip, 0, 0)) for g, s in zip(gathered, shards)]
    cols = lambda g: g.reshape((4, 2) + g.shape[1:]).transpose(1, 2, 0, 3).reshape(2, g.shape[1], 4 * g.shape[2])
    full_out = g_out.reshape(4, 2, 256, 1024).transpose(1, 0, 2, 3).reshape(2, 1024, 1024)
    loss, dx, grads = _local_step(x[0], positions[0], loss_target[0], cols(g_in), cols(g_uq), cols(g_ukv), full_out, w)

    split_cols = lambda g: g.reshape(2, g.shape[1], 4, g.shape[2] // 4).transpose(2, 0, 1, 3).reshape(8, g.shape[1], g.shape[2] // 4)
    parts = [split_cols(grads["w_in"]), split_cols(grads["w_uq"]), split_cols(grads["w_ukv"]),
             grads["w_out"].reshape(2, 4, 256, 1024).transpose(1, 0, 2, 3).reshape(8, 256, 1024)]
    common = _pack_small(grads, loss)
    got = _pair_exchange(parts, common)
    mine = [lax.dynamic_index_in_dim(p.reshape((4, 2) + p.shape[1:]), core, axis=1, keepdims=False) for p in parts]
    pair_sums = [_pair_sum(mine[a], got[a], f"pair_sum_{BIG_NAMES[a]}", 128) for a in range(4)]
    chip_common = _sum_leading(jnp.stack([common, got[4]]), "pair_sum_small", common.shape[0])
    received = _chip_exchange([ps[1] for ps in pair_sums], chip_common)
    sums = [_sum_chips(lax.dynamic_index_in_dim(pair_sums[a][0], chip, axis=0, keepdims=False), received[a],
                       f"sum_{BIG_NAMES[a]}", 128) for a in range(4)]
    all_common = lax.dynamic_update_slice(received[4], chip_common[None], (chip, 0, 0))
    small_sum, loss = _unpack_small(_sum_leading(all_common, "sum_small", all_common.shape[1]), w)
    others = _sibling_exchange(sums)
    total = dict(small_sum)
    for a, k in enumerate(BIG_NAMES):
        total[k] = jnp.where(core == 0, jnp.stack([sums[a], others[a]]), jnp.stack([others[a], sums[a]]))

    packed = _adamw(_pack_small(w), _pack_small(total), _pack_small(m), _pack_small(v), "adamw_small", 2048)
    small_out = [_unpack_small(pk, w)[0] for pk in packed]
    delta, new_m, new_v = {}, {}, {}
    for k in SMALL_NAMES:
        delta[k], new_m[k], new_v[k] = (so[k] for so in small_out)
    for k in BIG_NAMES:
        shape = w[k].shape
        flat = lambda a: a.reshape(shape[0] * shape[1], shape[2])
        res = _adamw(flat(w[k]), flat(total[k]), flat(m[k]), flat(v[k]), f"adamw_{k}", 256)
        delta[k], new_m[k], new_v[k] = (r.reshape(shape) for r in res)

    return (loss, dx[None], *[total[k] for k in WEIGHT_NAMES], *[delta[k] for k in WEIGHT_NAMES],
            *[new_m[k] for k in WEIGHT_NAMES], *[new_v[k] for k in WEIGHT_NAMES])
```

```python
import jax
import jax.numpy as jnp
from jax import lax
from jax.experimental import pallas as pl
from jax.experimental.pallas import tpu as pltpu

F32 = jnp.float32
MXU_DTYPE = jnp.bfloat16
WIRE_DTYPE = jnp.bfloat16
EPS = 1e-6
NEG_INF = -1e30
DEPTH = 2
N_HEADS = 4
QK_PAD = 256
V_DIM = 128
SCALE = 192 ** -0.5
LOG2E = 1.4426950408889634
ROPE_BASE = 10000.0
ADAM_LR, ADAM_B1, ADAM_B2, ADAM_EPS, ADAM_WD, ADAM_STEP = 0.001, 0.9, 0.999, 1e-08, 0.01, 10
VMEM_LIMIT_BYTES = 56 * 1024 * 1024
MESH = pl.DeviceIdType.MESH
ANY = pl.BlockSpec(memory_space=pl.ANY)

Z_MIX, Z_C, Z_KR, Z_GATE = 1280, 640, 128, 512
Z_W = Z_MIX + Z_C + Z_KR + Z_GATE


def _cp(n_axes=1):
    return pltpu.CompilerParams(dimension_semantics=("arbitrary",) * n_axes, vmem_limit_bytes=VMEM_LIMIT_BYTES)


def _dot(a, b):
    return lax.dot_general(a, b, (((1,), (0,)), ((), ())), preferred_element_type=F32)


def _dot_nt(a, b):
    return lax.dot_general(a, b, (((1,), (1,)), ((), ())), preferred_element_type=F32)


def _dot_tn(a, b):
    return lax.dot_general(a, b, (((0,), (0,)), ((), ())), preferred_element_type=F32)


def _mx(a):
    return a.astype(MXU_DTYPE)


def _silu_and_grad(g):
    sg = jax.nn.sigmoid(g)
    return g * sg, sg * (1.0 + g * (1.0 - sg))


def _rms(x, g):
    r = lax.rsqrt(jnp.mean(x * x, axis=-1, keepdims=True) + EPS)
    return x * r * g, r


def _rms_bwd(x, r, g, dy):
    xhat = x * r
    dyg = dy * g
    dx = r * (dyg - xhat * jnp.mean(dyg * xhat, axis=-1, keepdims=True))
    return dx, dy * xhat


def _zero_when(first, *refs):
    @pl.when(first)
    def _():
        for ref in refs:
            ref[...] = jnp.zeros(ref.shape, ref.dtype)


def _acc(ref, val):
    ref[...] += val


def _colsum(a):
    return jnp.sum(a, axis=0, keepdims=True)


def _in_proj_fwd(x, g, w, name, tm=512):
    t, d = x.shape
    n = w.shape[1]
    tm = min(tm, t)

    def body(x_ref, g_ref, w_ref, z_ref, h_ref):
        h, _ = _rms(x_ref[...], g_ref[...])
        h = _mx(h)
        h_ref[...] = h
        z_ref[...] = _dot(h, w_ref[...])

    return pl.pallas_call(
        body, name=name, grid=(t // tm,),
        in_specs=[pl.BlockSpec((tm, d), lambda i: (i, 0)), pl.BlockSpec((1, d), lambda i: (0, 0)),
                  pl.BlockSpec((d, n), lambda i: (0, 0))],
        out_specs=[pl.BlockSpec((tm, n), lambda i: (i, 0)), pl.BlockSpec((tm, d), lambda i: (i, 0))],
        out_shape=[jax.ShapeDtypeStruct((t, n), F32), jax.ShapeDtypeStruct((t, d), MXU_DTYPE)],
        compiler_params=_cp())(x, g, w)


def _in_proj_bwd(dz_mix, dz_c, dz_kr, dz_gate, h, x, d_res, w, g, name, tm=512):
    t, d = x.shape
    n = w.shape[1]
    tm = min(tm, t)

    def body(dm_ref, dc_ref, dk_ref, dg_ref, h_ref, x_ref, dres_ref, w_ref, g_ref, dx_ref, dw_ref, dgn_ref):
        first = pl.program_id(0) == 0
        dz = jnp.concatenate([dm_ref[...], dc_ref[...], dk_ref[...], dg_ref[...]], axis=1)

        _zero_when(first, dw_ref, dgn_ref)
        hb = h_ref[...]
        for c0 in range(0, n, 512):
            dw_ref[:, c0:c0 + 512] += _dot_tn(hb, dz[:, c0:c0 + 512])
        dh = _dot_nt(dz, w_ref[...])
        xf = x_ref[...]
        r = lax.rsqrt(jnp.mean(xf * xf, axis=-1, keepdims=True) + EPS)
        dx, dgt = _rms_bwd(xf, r, g_ref[...], dh)
        dx_ref[...] = dx + dres_ref[...]
        _acc(dgn_ref, _colsum(dgt))

    row = lambda wd: pl.BlockSpec((tm, wd), lambda i: (i, 0))
    fixed = lambda a, b: pl.BlockSpec((a, b), lambda i: (0, 0), pipeline_mode=pl.Buffered(1))
    return pl.pallas_call(
        body, name=name, grid=(t // tm,),
        in_specs=[row(Z_MIX), row(Z_C), row(Z_KR), row(Z_GATE), row(d), row(d), row(d), fixed(d, n), fixed(1, d)],
        out_specs=[row(d), fixed(d, n), fixed(1, d)],
        out_shape=[jax.ShapeDtypeStruct((t, d), F32), jax.ShapeDtypeStruct((d, n), F32),
                   jax.ShapeDtypeStruct((1, d), F32)],
        compiler_params=_cp())(dz_mix, dz_c, dz_kr, dz_gate, h, x, d_res, w, g)


def _lane_group(shape):
    return lax.broadcasted_iota(jnp.int32, shape, 1) // 64


def _select_group(vals):
    grp = _lane_group(vals[0].shape)
    out = vals[3]
    for gi in (2, 1, 0):
        out = jnp.where(grp == gi, vals[gi], out)
    return out


def _sgu_mask(transposed):
    r = (lax.broadcasted_iota(jnp.int32, (512, 128), 0) % 128) // 64
    c = lax.broadcasted_iota(jnp.int32, (512, 128), 1) // 64
    return (r <= c) if transposed else (c <= r)


def _sgu_apply(wstack, vb, nblk):
    outs = []
    for n in range(nblk):
        r = _dot(wstack, vb[n * 128:(n + 1) * 128, :])
        outs.append(_select_group([r[hh * 128:(hh + 1) * 128, :] for hh in range(4)]))
    return jnp.concatenate(outs, axis=0)


def _layer_norm(v, g, b):
    mu = jnp.mean(v, axis=-1, keepdims=True)
    vc = v - mu
    rstd = lax.rsqrt(jnp.mean(vc * vc, axis=-1, keepdims=True) + EPS)
    vhat = vc * rstd
    return vhat * g + b, vhat, rstd


def _pool_counts(t0, n):
    t = t0 + lax.broadcasted_iota(jnp.int32, (n, 256), 0)
    w = _select_group([jnp.full((n, 256), wv, jnp.int32) for wv in (2, 4, 8, 16)])
    return jnp.minimum(t + 1, w).astype(F32)


def _pooled(p, halo, t0):
    tm = p.shape[0]
    ext = jnp.concatenate([halo, p], axis=0)
    s2 = ext + pltpu.roll(ext, 1, 0)
    s4 = s2 + pltpu.roll(s2, 2, 0)
    s8 = s4 + pltpu.roll(s4, 4, 0)
    s16 = s8 + pltpu.roll(s8, 8, 0)
    sel = _select_group([s2, s4, s8, s16])[16:, :]
    return sel / _pool_counts(t0, tm) - p


def _pooled_bwd(dpool, dpool_halo, t0):
    tm = dpool.shape[0]
    n = tm + 16
    ext = jnp.concatenate([dpool, dpool_halo], axis=0) / _pool_counts(t0, n)
    f2 = ext + pltpu.roll(ext, n - 1, 0)
    f4 = f2 + pltpu.roll(f2, n - 2, 0)
    f8 = f4 + pltpu.roll(f4, n - 4, 0)
    f16 = f8 + pltpu.roll(f8, n - 8, 0)
    return _select_group([f2, f4, f8, f16])[:tm, :] - dpool


def _mix_specs(t, tm):
    nt16 = t // 16
    zrow = pl.BlockSpec((tm, Z_MIX), lambda i: (i, 0))
    prev_halo = pl.BlockSpec((16, 256), lambda i: (jnp.maximum(i * (tm // 16) - 1, 0), 3))
    fixed = lambda a, b: pl.BlockSpec((a, b), lambda i: (0, 0))
    params = [fixed(512, 128), fixed(128, 256), fixed(1, 256), fixed(1, 256), fixed(256, 256), fixed(1, 256)]
    return nt16, zrow, prev_halo, fixed, params


def _mix_fwd(z, sgu_w, sgu_bias, ln_g, ln_b, pool_wbd, pool_scale, name, tm=512):
    t = z.shape[0]
    tm = min(tm, t)
    _, zrow, prev_halo, _, params = _mix_specs(t, tm)

    def body(z_ref, halo_ref, w_ref, bias_ref, lng_ref, lnb_ref, pw_ref, ps_ref, y_ref):
        i = pl.program_id(0)
        u, v, gate = z_ref[:, 0:256], z_ref[:, 256:512], z_ref[:, 512:768]
        p, pgate = z_ref[:, 768:1024], z_ref[:, 1024:1280]
        vn, _, _ = _layer_norm(v, lng_ref[...], lnb_ref[...])
        wm = _mx(jnp.where(_sgu_mask(False), w_ref[...], 0.0))
        mixed = _sgu_apply(wm, _mx(vn), tm // 128) + jnp.tile(bias_ref[...], (tm // 128, 1))
        ya = u * mixed * _silu_and_grad(gate)[0]
        halo = jnp.where(i > 0, halo_ref[...], 0.0)
        pooled = _pooled(p, halo, i * tm)
        yb = _dot(_mx(pooled), pw_ref[...]) * ps_ref[...] * _silu_and_grad(pgate)[0]
        y_ref[...] = _mx(jnp.concatenate([ya, yb], axis=1))

    return pl.pallas_call(
        body, name=name, grid=(t // tm,),
        in_specs=[zrow, prev_halo] + params,
        out_specs=pl.BlockSpec((tm, 512), lambda i: (i, 0)),
        out_shape=jax.ShapeDtypeStruct((t, 512), MXU_DTYPE),
        compiler_params=_cp())(z, z, sgu_w, sgu_bias, ln_g, ln_b, pool_wbd, pool_scale)


def _mix_bwd(z, dycat, sgu_w, sgu_wt, sgu_bias, ln_g, ln_b, pool_wbd, pool_scale, name, tm=512):
    t = z.shape[0]
    tm = min(tm, t)
    nt16, zrow, prev_halo, fixed, params = _mix_specs(t, tm)
    nblk = tm // 128
    last = t // tm - 1

    def body(z_ref, halo_ref, zn_ref, dy_ref, dyn_ref, w_ref, wt_ref, bias_ref, lng_ref, lnb_ref, pw_ref, ps_ref,
             dz_ref, dw_ref, db_ref, dlng_ref, dlnb_ref, dpw_ref, dps_ref):
        i = pl.program_id(0)
        _zero_when(i == 0, dw_ref, db_ref, dlng_ref, dlnb_ref, dpw_ref, dps_ref)
        u, v, gate = z_ref[:, 0:256], z_ref[:, 256:512], z_ref[:, 512:768]
        p, pgate = z_ref[:, 768:1024], z_ref[:, 1024:1280]
        dya, dyb = dy_ref[:, 0:256], dy_ref[:, 256:512]
        vn, vhat, rstd = _layer_norm(v, lng_ref[...], lnb_ref[...])
        vnb = _mx(vn)
        wm = _mx(jnp.where(_sgu_mask(False), w_ref[...], 0.0))
        wmt = _mx(jnp.where(_sgu_mask(True), wt_ref[...], 0.0))
        mixed = _sgu_apply(wm, vnb, nblk) + jnp.tile(bias_ref[...], (nblk, 1))
        silu, dsilu = _silu_and_grad(gate)
        t1 = u * mixed
        d_gate = dya * t1 * dsilu
        d_t1 = dya * silu
        d_u = d_t1 * mixed
        d_mixed = d_t1 * u
        dmb = _mx(d_mixed)
        d_vn = _sgu_apply(wmt, dmb, nblk)
        grp = _lane_group((128, 256))
        lane = lax.broadcasted_iota(jnp.int32, (128, 128), 1)
        dws = [jnp.zeros((128, 128), F32) for _ in range(4)]
        dbias = jnp.zeros((128, 128), F32)
        for n in range(nblk):
            dm_n, dmb_n, vnb_n = d_mixed[n * 128:(n + 1) * 128], dmb[n * 128:(n + 1) * 128], vnb[n * 128:(n + 1) * 128]
            for hh in range(4):
                dws[hh] = dws[hh] + _dot_nt(jnp.where(grp == hh, dmb_n, jnp.zeros_like(dmb_n)), vnb_n)
                rs = jnp.sum(jnp.where(grp == hh, dm_n, 0.0), axis=-1, keepdims=True)
                dbias = dbias + jnp.where(lane == hh, rs, 0.0)
        _acc(dw_ref, jnp.concatenate(dws, axis=0))
        _acc(db_ref, dbias)
        _acc(dlng_ref, _colsum(d_vn * vhat))
        _acc(dlnb_ref, _colsum(d_vn))
        dvh = d_vn * lng_ref[...]
        d_v = rstd * (dvh - jnp.mean(dvh, axis=-1, keepdims=True) - vhat * jnp.mean(dvh * vhat, axis=-1, keepdims=True))

        @pl.when(i == last)
        def _():
            dw_ref[...] = jnp.where(_sgu_mask(False), dw_ref[...], 0.0)

        halo = jnp.where(i > 0, halo_ref[...], 0.0)
        pooled = _pooled(p, halo, i * tm)
        pooled_b = _mx(pooled)
        mixedp = _dot(pooled_b, pw_ref[...])
        psilu, pdsilu = _silu_and_grad(pgate)
        d_pgate = dyb * (mixedp * ps_ref[...]) * pdsilu
        d_ms = dyb * psilu
        _acc(dps_ref, _colsum(d_ms * mixedp))
        dmpb = _mx(d_ms * ps_ref[...])
        _acc(dpw_ref, _dot_tn(pooled_b, dmpb))
        d_pooled = _dot_nt(dmpb, pw_ref[...])
        dmp_halo = _mx(dyn_ref[...] * _silu_and_grad(zn_ref[...])[0] * ps_ref[...])
        d_pooled_halo = jnp.where(i < last, _dot_nt(dmp_halo, pw_ref[...]), 0.0)
        d_p = _pooled_bwd(d_pooled, d_pooled_halo, i * tm)
        dz_ref[...] = _mx(jnp.concatenate([d_u, d_v, d_gate, d_p, d_pgate], axis=1))

    nxt = lambda i: jnp.minimum((i + 1) * (tm // 16), nt16 - 1)
    return pl.pallas_call(
        body, name=name, grid=(t // tm,),
        in_specs=[zrow, prev_halo, pl.BlockSpec((16, 256), lambda i: (nxt(i), 4)),
                  pl.BlockSpec((tm, 512), lambda i: (i, 0)), pl.BlockSpec((16, 256), lambda i: (nxt(i), 1)),
                  params[0], fixed(512, 128)] + params[1:],
        out_specs=[pl.BlockSpec((tm, Z_MIX), lambda i: (i, 0)), fixed(512, 128), fixed(128, 128), fixed(1, 256),
                   fixed(1, 256), fixed(256, 256), fixed(1, 256)],
        out_shape=[jax.ShapeDtypeStruct((t, Z_MIX), MXU_DTYPE), jax.ShapeDtypeStruct((512, 128), F32),
                   jax.ShapeDtypeStruct((128, 128), F32), jax.ShapeDtypeStruct((1, 256), F32),
                   jax.ShapeDtypeStruct((1, 256), F32), jax.ShapeDtypeStruct((256, 256), F32),
                   jax.ShapeDtypeStruct((1, 256), F32)],
        compiler_params=_cp())(z, z, z, dycat, dycat, sgu_w, sgu_wt, sgu_bias, ln_g, ln_b, pool_wbd, pool_scale)


def _rot_half(x, transpose):
    w = x.shape[1]
    lane = lax.broadcasted_iota(jnp.int32, x.shape, 1) % min(w, 256)
    base = 128 if w >= 256 else 0
    lo = jnp.logical_and(lane >= base, lane < base + 32)
    hi = jnp.logical_and(lane >= base + 32, lane < base + 64)
    up = pltpu.roll(x, w - 32, 1)
    down = pltpu.roll(x, 32, 1)
    if transpose:
        return jnp.where(lo, up, jnp.where(hi, -down, 0.0))
    return jnp.where(lo, -up, jnp.where(hi, down, 0.0))


def _rope(x, c, s):
    return x * c + _rot_half(x, False) * s


def _rope_bwd(dy, c, s):
    return dy * c + _rot_half(dy * s, True)


def _qkv_fwd(z, rc, rs, w_uq, w_ukv, gq, gkv, name, tm=512):
    t = z.shape[0]
    tm = min(tm, t)

    def body(zc_ref, zk_ref, rc_ref, rs_ref, wq_ref, wkv_ref, gq_ref, gkv_ref, q_ref, k_ref, v_ref):
        cq, ckv = zc_ref[:, 0:384], zc_ref[:, 384:640]
        c, s = rc_ref[...], rs_ref[...]
        qn, _ = _rms(cq, gq_ref[...])
        q = _rope(_dot(_mx(qn), wq_ref[...]), jnp.tile(c, (1, N_HEADS)), jnp.tile(s, (1, N_HEADS)))
        kvn, _ = _rms(ckv, gkv_ref[...])
        kv = _dot(_mx(kvn), wkv_ref[...])
        kpe = _rope(zk_ref[...], c[:, 128:256], s[:, 128:256])
        for hh in range(N_HEADS):
            q_ref[hh] = _mx(q[:, hh * QK_PAD:(hh + 1) * QK_PAD])
            k_ref[hh] = _mx(jnp.concatenate([kv[:, hh * 128:(hh + 1) * 128], kpe], axis=1))
            v_ref[hh] = _mx(kv[:, 512 + hh * 128:512 + (hh + 1) * 128])

    fixed = lambda a, b: pl.BlockSpec((a, b), lambda i: (0, 0))
    heads = lambda wd: pl.BlockSpec((N_HEADS, tm, wd), lambda i: (0, i, 0))
    return pl.pallas_call(
        body, name=name, grid=(t // tm,),
        in_specs=[pl.BlockSpec((tm, Z_C), lambda i: (i, Z_MIX // Z_C)),
                  pl.BlockSpec((tm, Z_KR), lambda i: (i, (Z_MIX + Z_C) // Z_KR)),
                  pl.BlockSpec((tm, 256), lambda i: (i, 0)), pl.BlockSpec((tm, 256), lambda i: (i, 0)),
                  fixed(384, 1024), fixed(256, 1024), fixed(1, 384), fixed(1, 256)],
        out_specs=[heads(QK_PAD), heads(QK_PAD), heads(V_DIM)],
        out_shape=[jax.ShapeDtypeStruct((N_HEADS, t, QK_PAD), MXU_DTYPE),
                   jax.ShapeDtypeStruct((N_HEADS, t, QK_PAD), MXU_DTYPE),
                   jax.ShapeDtypeStruct((N_HEADS, t, V_DIM), MXU_DTYPE)],
        compiler_params=_cp())(z, z, rc, rs, w_uq, w_ukv, gq, gkv)


def _qkv_bwd(dq, dk, dv, z, rc, rs, w_uq, w_ukv, gq, gkv, name, tm=512):
    t = z.shape[0]
    tm = min(tm, t)

    def body(dq_ref, dk_ref, dv_ref, zc_ref, rc_ref, rs_ref, wq_ref, wkv_ref, gq_ref, gkv_ref,
             dzc_ref, dzk_ref, dwq_ref, dwkv_ref, dgq_ref, dgkv_ref):
        _zero_when(pl.program_id(0) == 0, dwq_ref, dwkv_ref, dgq_ref, dgkv_ref)
        cq, ckv = zc_ref[:, 0:384], zc_ref[:, 384:640]
        c, s = rc_ref[...], rs_ref[...]
        dq_all = jnp.concatenate([dq_ref[hh] for hh in range(N_HEADS)], axis=1)
        dqp = _mx(_rope_bwd(dq_all, jnp.tile(c, (1, N_HEADS)), jnp.tile(s, (1, N_HEADS))))
        qn, rq = _rms(cq, gq_ref[...])
        _acc(dwq_ref, _dot_tn(_mx(qn), dqp))
        d_qn = _dot_nt(dqp, wq_ref[...])
        dkpe = dk_ref[0][:, 128:256]
        for hh in range(1, N_HEADS):
            dkpe = dkpe + dk_ref[hh][:, 128:256]
        dzk_ref[...] = _mx(_rope_bwd(dkpe, c[:, 128:256], s[:, 128:256]))
        dkv = _mx(jnp.concatenate([dk_ref[hh][:, 0:128] for hh in range(N_HEADS)]
                                  + [dv_ref[hh] for hh in range(N_HEADS)], axis=1))
        kvn, rkv = _rms(ckv, gkv_ref[...])
        _acc(dwkv_ref, _dot_tn(_mx(kvn), dkv))
        d_kvn = _dot_nt(dkv, wkv_ref[...])
        d_cq, dgq_t = _rms_bwd(cq, rq, gq_ref[...], d_qn)
        _acc(dgq_ref, _colsum(dgq_t))
        d_ckv, dgkv_t = _rms_bwd(ckv, rkv, gkv_ref[...], d_kvn)
        _acc(dgkv_ref, _colsum(dgkv_t))
        dzc_ref[...] = _mx(jnp.concatenate([d_cq, d_ckv], axis=1))

    fixed = lambda a, b: pl.BlockSpec((a, b), lambda i: (0, 0))
    heads = lambda wd: pl.BlockSpec((N_HEADS, tm, wd), lambda i: (0, i, 0))
    return pl.pallas_call(
        body, name=name, grid=(t // tm,),
        in_specs=[heads(QK_PAD), heads(QK_PAD), heads(V_DIM), pl.BlockSpec((tm, Z_C), lambda i: (i, Z_MIX // Z_C)),
                  pl.BlockSpec((tm, 256), lambda i: (i, 0)), pl.BlockSpec((tm, 256), lambda i: (i, 0)),
                  fixed(384, 1024), fixed(256, 1024), fixed(1, 384), fixed(1, 256)],
        out_specs=[pl.BlockSpec((tm, Z_C), lambda i: (i, 0)), pl.BlockSpec((tm, Z_KR), lambda i: (i, 0)),
                   fixed(384, 1024), fixed(256, 1024), fixed(1, 384), fixed(1, 256)],
        out_shape=[jax.ShapeDtypeStruct((t, Z_C), MXU_DTYPE), jax.ShapeDtypeStruct((t, Z_KR), MXU_DTYPE),
                   jax.ShapeDtypeStruct((384, 1024), F32), jax.ShapeDtypeStruct((256, 1024), F32),
                   jax.ShapeDtypeStruct((1, 384), F32), jax.ShapeDtypeStruct((1, 256), F32)],
        compiler_params=_cp())(dq, dk, dv, z, rc, rs, w_uq, w_ukv, gq, gkv)


def _loop_in_long_trips(n, body):
    def two(t, carry):
        return body(2 * t + 1, body(2 * t, carry))

    def four(t, carry):
        return two(2 * t + 1, two(2 * t, carry))

    lax.fori_loop(0, n // 4, four, 0)
    lax.fori_loop(2 * (n // 4), n // 2, two, 0)
    lax.fori_loop(2 * (n // 2), n, body, 0)


def _init_mask_bias(bias_ref):
    _, tq, tk = bias_ref.shape
    r = lax.broadcasted_iota(jnp.int32, (tq, tk), 0) // 64
    c = lax.broadcasted_iota(jnp.int32, (tq, tk), 1) // 64
    bias_ref[0] = jnp.zeros((tq, tk), F32)
    for d in range(tq // tk):
        bias_ref[1 + d] = jnp.where(c + d * (tk // 64) <= r, 0.0, NEG_INF)


def _gate_block(tq):
    return pl.BlockSpec((tq, 128), lambda h, i: (i, (Z_MIX + Z_C + Z_KR) // 128 + h))


def _attn_fwd(qh, kh, vh, z, name, tq=1024, tk=512):
    t = qh.shape[1]
    tq = min(tq, t)
    tk = min(tk, tq)
    ratio = tq // tk

    def body(q_ref, g_ref, k_hbm, v_hbm, o_ref, yc_ref, lse_ref, k_v, v_v, m_s, acc_s, s_a, s_b, mx_a, mx_b, bias_s,
             sem):
        h, i = pl.program_id(0), pl.program_id(1)

        @pl.when(i == 0)
        def _():
            ck = pltpu.make_async_copy(k_hbm.at[h], k_v, sem.at[0])
            cv = pltpu.make_async_copy(v_hbm.at[h], v_v.at[:, 0:V_DIM], sem.at[1])
            ck.start()
            cv.start()
            v_v[:, V_DIM:2 * V_DIM] = jnp.ones((t, V_DIM), MXU_DTYPE)
            _init_mask_bias(bias_s)
            ck.wait()
            cv.wait()

        q = q_ref[...]
        m_s[...] = jnp.full(m_s.shape, NEG_INF, F32)
        acc_s[...] = jnp.zeros(acc_s.shape, F32)

        last = ratio * (i + 1) - 1

        def keys(j):
            return pl.ds(pl.multiple_of(j * tk, tk), tk)

        def scores(s_ref, mx_ref, j):
            s = _dot_nt(q, k_v[keys(j), :]) * (SCALE * LOG2E) + bias_s[jnp.maximum(j - ratio * i + 1, 0)]
            s_ref[...] = s
            mx_ref[...] = jnp.broadcast_to(jnp.max(s, axis=-1, keepdims=True), mx_ref.shape)

        def softmax_pv(s_ref, mx_ref, j):
            m_old = m_s[...]
            m_new = jnp.maximum(m_old, mx_ref[...])
            p = jnp.exp2(s_ref[...] - jnp.tile(m_new, (1, tk // 128)))
            alpha = jnp.exp2(m_old - m_new)
            m_s[...] = m_new
            acc_s[...] = jnp.tile(alpha, (1, 2)) * acc_s[...] + _dot(_mx(p), v_v[keys(j), :])

        scores(s_a, mx_a, 0)

        def pair(pp, carry):
            scores(s_b, mx_b, 2 * pp + 1)
            softmax_pv(s_a, mx_a, 2 * pp)
            scores(s_a, mx_a, jnp.minimum(2 * pp + 2, last))
            softmax_pv(s_b, mx_b, 2 * pp + 1)
            return carry

        _loop_in_long_trips((last + 1) // 2, pair)
        if ratio % 2 == 1:
            @pl.when(last % 2 == 0)
            def _():
                softmax_pv(s_a, mx_a, last)

        l = acc_s[:, V_DIM:2 * V_DIM]
        o = acc_s[:, 0:V_DIM] / l
        o_ref[...] = o
        yc_ref[...] = _mx(o * _silu_and_grad(g_ref[...])[0])
        lse_ref[...] = m_s[...] + jnp.log2(l)

    return pl.pallas_call(
        body, name=name, grid=(N_HEADS, t // tq),
        in_specs=[pl.BlockSpec((None, tq, QK_PAD), lambda h, i: (h, i, 0)), _gate_block(tq), ANY, ANY],
        out_specs=[pl.BlockSpec((tq, 128), lambda h, i: (i, h)), pl.BlockSpec((tq, 128), lambda h, i: (i, h)),
                   pl.BlockSpec((None, tq, 128), lambda h, i: (h, i, 0))],
        out_shape=[jax.ShapeDtypeStruct((t, N_HEADS * V_DIM), F32), jax.ShapeDtypeStruct((t, N_HEADS * V_DIM), MXU_DTYPE),
                   jax.ShapeDtypeStruct((N_HEADS, t, 128), F32)],
        scratch_shapes=[pltpu.VMEM((t, QK_PAD), MXU_DTYPE), pltpu.VMEM((t, 2 * V_DIM), MXU_DTYPE),
                        pltpu.VMEM((tq, 128), F32), pltpu.VMEM((tq, 2 * V_DIM), F32),
                        pltpu.VMEM((tq, tk), F32), pltpu.VMEM((tq, tk), F32), pltpu.VMEM((tq, 128), F32),
                        pltpu.VMEM((tq, 128), F32), pltpu.VMEM((ratio + 1, tq, tk), F32),
                        pltpu.SemaphoreType.DMA((2,))],
        compiler_params=_cp(2))(qh, z, kh, vh)


def _attn_bwd(qh, kh, vh, o, lse, dycat, z, name, tq=512):
    t = qh.shape[1]
    tq = min(tq, t)
    nq = t // tq

    def body(q_ref, o_ref, lse_ref, dy_ref, g_ref, k_hbm, v_hbm, dq_ref, dgate_ref, dk_hbm, dv_hbm,
             k_v, v_v, dk_acc, dv_acc, dq_acc, delta_s, s_a, dp_a, s_b, dp_b, bias_s, sem):
        h, i = pl.program_id(0), pl.program_id(1)

        @pl.when(i == 0)
        def _():
            ck = pltpu.make_async_copy(k_hbm.at[h], k_v, sem.at[0])
            cv = pltpu.make_async_copy(v_hbm.at[h], v_v, sem.at[1])
            ck.start()
            cv.start()
            _init_mask_bias(bias_s)
            dk_acc[...] = jnp.zeros(dk_acc.shape, F32)
            dv_acc[...] = jnp.zeros(dv_acc.shape, F32)
            ck.wait()
            cv.wait()

        gate, dy, of = g_ref[...], dy_ref[...], o_ref[...]
        silu, dsilu = _silu_and_grad(gate)
        do = dy * silu
        delta = jnp.sum(do * of, axis=-1, keepdims=True)
        dgate_ref[...] = _mx(dy * of * dsilu)
        dob = _mx(do)
        q = q_ref[...]
        delta_s[...] = jnp.broadcast_to(delta, delta_s.shape)
        dq_acc[...] = jnp.zeros(dq_acc.shape, F32)

        def keys(j):
            return pl.ds(pl.multiple_of(j * tq, tq), tq)

        def scores(s_ref, dp_ref, j):
            s = _dot_nt(q, k_v[keys(j), :]) * (SCALE * LOG2E) + bias_s[(j == i).astype(jnp.int32)]
            s_ref[...] = s - jnp.tile(lse_ref[...], (1, tq // 128))
            dp_ref[...] = _dot_nt(dob, v_v[keys(j), :]) - jnp.tile(delta_s[...], (1, tq // 128))

        def grads(s_ref, dp_ref, j):
            ks = keys(j)
            p = jnp.exp2(s_ref[...])
            ds = p * dp_ref[...] * SCALE
            pb, dsb = _mx(p), _mx(ds)
            dq_acc[...] += _dot(dsb, k_v[ks, :])
            dk_acc[ks, :] += _dot_tn(dsb, q)
            dv_acc[ks, :] += _dot_tn(pb, dob)

        scores(s_a, dp_a, 0)

        def pair(pp, carry):
            scores(s_b, dp_b, 2 * pp + 1)
            grads(s_a, dp_a, 2 * pp)
            scores(s_a, dp_a, jnp.minimum(2 * pp + 2, i))
            grads(s_b, dp_b, 2 * pp + 1)
            return carry

        _loop_in_long_trips((i + 1) // 2, pair)

        @pl.when(i % 2 == 0)
        def _():
            grads(s_a, dp_a, i)

        dq_ref[...] = dq_acc[...]

        @pl.when(i == nq - 1)
        def _():
            ck = pltpu.make_async_copy(dk_acc, dk_hbm.at[h], sem.at[0])
            cv = pltpu.make_async_copy(dv_acc, dv_hbm.at[h], sem.at[1])
            ck.start()
            cv.start()
            ck.wait()
            cv.wait()

    return pl.pallas_call(
        body, name=name, grid=(N_HEADS, nq),
        in_specs=[pl.BlockSpec((None, tq, QK_PAD), lambda h, i: (h, i, 0)),
                  pl.BlockSpec((tq, 128), lambda h, i: (i, h)),
                  pl.BlockSpec((None, tq, 128), lambda h, i: (h, i, 0)),
                  pl.BlockSpec((tq, 128), lambda h, i: (i, N_HEADS + h)), _gate_block(tq), ANY, ANY],
        out_specs=[pl.BlockSpec((None, tq, QK_PAD), lambda h, i: (h, i, 0)),
                   pl.BlockSpec((tq, 128), lambda h, i: (i, h)), ANY, ANY],
        out_shape=[jax.ShapeDtypeStruct((N_HEADS, t, QK_PAD), F32), jax.ShapeDtypeStruct((t, Z_GATE), MXU_DTYPE),
                   jax.ShapeDtypeStruct((N_HEADS, t, QK_PAD), F32), jax.ShapeDtypeStruct((N_HEADS, t, V_DIM), F32)],
        scratch_shapes=[pltpu.VMEM((t, QK_PAD), MXU_DTYPE), pltpu.VMEM((t, V_DIM), MXU_DTYPE),
                        pltpu.VMEM((t, QK_PAD), F32), pltpu.VMEM((t, V_DIM), F32), pltpu.VMEM((tq, QK_PAD), F32),
                        pltpu.VMEM((tq, 128), F32)] + [pltpu.VMEM((tq, tq), F32)] * 4
        + [pltpu.VMEM((2, tq, tq), F32), pltpu.SemaphoreType.DMA((2,))],
        compiler_params=_cp(2))(qh, o, lse, dycat, z, kh, vh)


def _out_proj_fwd(yab, yc, w, x, g, target, name, tm=512):
    t, d = x.shape
    tm = min(tm, t)
    is_last = target is not None

    def body(*refs):
        if is_last:
            yab_ref, yc_ref, w_ref, x_ref, g_ref, t_ref, y_ref, dout_ref, loss_ref = refs
            _zero_when(pl.program_id(0) == 0, loss_ref)
        else:
            yab_ref, yc_ref, w_ref, x_ref, g_ref, y_ref, out_ref = refs
        y = _dot(jnp.concatenate([yab_ref[...], yc_ref[...]], axis=1), w_ref[...])
        y_ref[...] = y
        out = x_ref[...] + _rms(y, g_ref[...])[0]
        if is_last:
            diff = out - t_ref[...]
            dout_ref[...] = diff * (1.0 / d)
            part = jnp.sum(jnp.sum(diff * diff, axis=-1, keepdims=True), axis=0, keepdims=True) * (0.5 / d)
            _acc(loss_ref, jnp.broadcast_to(part, (1, 128)))
        else:
            out_ref[...] = out

    row = lambda wd: pl.BlockSpec((tm, wd), lambda i: (i, 0))
    fixed = lambda a, b: pl.BlockSpec((a, b), lambda i: (0, 0))
    in_specs = [row(512), row(512), fixed(d, d), row(d), fixed(1, d)]
    args = [yab, yc, w, x, g]
    out_specs = [row(d), row(d)]
    out_shape = [jax.ShapeDtypeStruct((t, d), F32), jax.ShapeDtypeStruct((t, d), F32)]
    if is_last:
        in_specs.append(row(d))
        args.append(target)
        out_specs.append(fixed(1, 128))
        out_shape.append(jax.ShapeDtypeStruct((1, 128), F32))
    return pl.pallas_call(body, name=name, grid=(t // tm,), in_specs=in_specs, out_specs=out_specs,
                          out_shape=out_shape, compiler_params=_cp())(*args)


def _halves(tm):
    half = tm // 2 if tm >= 512 else tm
    return [pl.ds(s, half) for s in range(0, tm, half)]


def _out_proj_bwd(dout, y, yab, yc, w, g, name, tm=1024):
    t, d = y.shape
    tm = min(tm, t)

    def body(dout_ref, y_ref, yab_ref, yc_ref, w_ref, g_ref, dycat_ref, dw_ref, dg_ref):
        _zero_when(pl.program_id(0) == 0, dw_ref, dg_ref)
        dybs = []
        for rows in _halves(tm):
            y = y_ref[rows, :]
            r = lax.rsqrt(jnp.mean(y * y, axis=-1, keepdims=True) + EPS)
            dy, dgt = _rms_bwd(y, r, g_ref[...], dout_ref[rows, :])
            _acc(dg_ref, _colsum(dgt))
            dybs.append(_mx(dy))
        for rows, dyb in zip(_halves(tm), dybs):
            _acc(dw_ref, _dot_tn(jnp.concatenate([yab_ref[rows, :], yc_ref[rows, :]], axis=1), dyb))
            dycat_ref[rows, :] = _dot_nt(dyb, w_ref[...])

    row = lambda wd: pl.BlockSpec((tm, wd), lambda i: (i, 0))
    fixed = lambda a, b: pl.BlockSpec((a, b), lambda i: (0, 0), pipeline_mode=pl.Buffered(1))
    return pl.pallas_call(
        body, name=name, grid=(t // tm,),
        in_specs=[row(d), row(d), row(512), row(512), fixed(d, d), fixed(1, d)],
        out_specs=[row(d), fixed(d, d), fixed(1, d)],
        out_shape=[jax.ShapeDtypeStruct((t, d), F32), jax.ShapeDtypeStruct((d, d), F32),
                   jax.ShapeDtypeStruct((1, d), F32)],
        compiler_params=_cp())(dout, y, yab, yc, w, g)


def _mesh_pos():
    return lax.axis_index("x"), lax.axis_index("y"), lax.axis_index("c")


def _remote(src, dst, send_sem, recv_sem, to):
    return pltpu.make_async_remote_copy(src_ref=src, dst_ref=dst, send_sem=send_sem, recv_sem=recv_sem,
                                        device_id=to, device_id_type=MESH)


CHUNK_ROWS = 256


def _pieces(rows):
    return [(s, min(CHUNK_ROWS, rows - s)) for s in range(0, rows, CHUNK_ROWS)]


def _piece_table(shapes):
    return [(a, s, sz) for a, shp in enumerate(shapes) for s, sz in _pieces(shp[-2])]


def _gather_weights(shards):
    n = len(shards)
    table = _piece_table([s.shape for s in shards])
    npc = len(table)

    def body(*refs):
        ins, outs = refs[:n], refs[n:2 * n]
        send_sems, recv_sems, fwd_send, fwd_recv = refs[2 * n:]
        x, y, c = _mesh_pos()
        me, sibling = (x, y, c), (x, y, 1 - c)
        chips = [(1 - x, y), (x, 1 - y), (1 - x, 1 - y)]
        slot = lambda cx, cy, layer: 2 * (2 * cx + cy) + layer
        first = []
        for a in range(n):
            for j, (cx, cy) in enumerate(chips):
                first.append(_remote(ins[a].at[c], outs[a].at[slot(x, y, c)], send_sems.at[a, j], recv_sems.at[a, j],
                                     (cx, cy, c)))
                first[-1].start()
        passed = []
        for j, (cx, cy) in enumerate(chips):
            for a in range(n):
                blk = outs[a].at[slot(cx, cy, c)]
                _remote(blk, blk, send_sems.at[a, j], recv_sems.at[a, j], me).wait_recv()
            for q, (a, s, sz) in enumerate(table):
                rows = outs[a].at[slot(cx, cy, c), pl.ds(s, sz)]
                passed.append(_remote(rows, rows, fwd_send.at[j, q], fwd_recv.at[j, q], sibling))
                passed[-1].start()
        for j, (cx, cy) in enumerate(chips):
            for q, (a, s, sz) in enumerate(table):
                rows = outs[a].at[slot(cx, cy, 1 - c), pl.ds(s, sz)]
                _remote(rows, rows, fwd_send.at[j, q], fwd_recv.at[j, q], me).wait_recv()
        for cp in first + passed:
            cp.wait_send()

    return pl.pallas_call(
        body, name="gather_weights", in_specs=[ANY] * n, out_specs=[ANY] * n,
        out_shape=[jax.ShapeDtypeStruct((8,) + s.shape[1:], s.dtype) for s in shards],
        scratch_shapes=[pltpu.SemaphoreType.DMA((n, 3)), pltpu.SemaphoreType.DMA((n, 3)),
                        pltpu.SemaphoreType.DMA((3, npc)), pltpu.SemaphoreType.DMA((3, npc))])(*shards)


def _pair_exchange(parts, common):
    n = len(parts)
    table = _piece_table([p.shape for p in parts] + [common.shape])
    npc = len(table)

    def body(*refs):
        ins, outs = refs[:n + 1], refs[n + 1:2 * n + 2]
        send_sems, recv_sems = refs[2 * n + 2:]
        x, y, c = _mesh_pos()
        sent = []
        for k in range(4):
            for q, (a, s, sz) in enumerate(table):
                if a == n and k > 0:
                    continue
                src = ins[a].at[2 * k + 1 - c, pl.ds(s, sz)] if a < n else ins[a].at[pl.ds(s, sz)]
                dst = outs[a].at[k, pl.ds(s, sz)] if a < n else outs[a].at[pl.ds(s, sz)]
                sent.append(_remote(src, dst, send_sems.at[k, q], recv_sems.at[k, q], (x, y, 1 - c)))
                sent[-1].start()
        for k in range(4):
            for q, (a, s, sz) in enumerate(table):
                if a == n and k > 0:
                    continue
                dst = outs[a].at[k, pl.ds(s, sz)] if a < n else outs[a].at[pl.ds(s, sz)]
                _remote(dst, dst, send_sems.at[k, q], recv_sems.at[k, q], (x, y, c)).wait_recv()
        for cp in sent:
            cp.wait_send()

    return pl.pallas_call(
        body, name="grad_pair_exchange", in_specs=[ANY] * (n + 1), out_specs=[ANY] * (n + 1),
        out_shape=[jax.ShapeDtypeStruct((4,) + p.shape[1:], p.dtype) for p in parts]
        + [jax.ShapeDtypeStruct(common.shape, common.dtype)],
        scratch_shapes=[pltpu.SemaphoreType.DMA((4, npc)), pltpu.SemaphoreType.DMA((4, npc))])(*parts, common)


def _chip_exchange(parts, common):
    n = len(parts)
    table = _piece_table([p.shape for p in parts] + [common.shape])
    npc = len(table)

    def body(*refs):
        ins, outs = refs[:n + 1], refs[n + 1:2 * n + 2]
        send_sems, recv_sems = refs[2 * n + 2:]
        x, y, c = _mesh_pos()
        mine = 2 * x + y
        chips = [(1 - x, y), (x, 1 - y), (1 - x, 1 - y)]
        src = lambda a, k: ins[a].at[k] if a < n else ins[a]
        sent = []
        for j, (cx, cy) in enumerate(chips):
            for q, (a, s, sz) in enumerate(table):
                sent.append(_remote(src(a, 2 * cx + cy).at[pl.ds(s, sz)], outs[a].at[mine, pl.ds(s, sz)],
                                    send_sems.at[j, q], recv_sems.at[j, q], (cx, cy, c)))
                sent[-1].start()
        for j, (cx, cy) in enumerate(chips):
            for q, (a, s, sz) in enumerate(table):
                dst = outs[a].at[2 * cx + cy, pl.ds(s, sz)]
                _remote(dst, dst, send_sems.at[j, q], recv_sems.at[j, q], (x, y, c)).wait_recv()
        for cp in sent:
            cp.wait_send()

    return pl.pallas_call(
        body, name="grad_chip_exchange", in_specs=[ANY] * (n + 1), out_specs=[ANY] * (n + 1),
        out_shape=[jax.ShapeDtypeStruct(p.shape, p.dtype) for p in parts]
        + [jax.ShapeDtypeStruct((4,) + common.shape, common.dtype)],
        scratch_shapes=[pltpu.SemaphoreType.DMA((3, npc)), pltpu.SemaphoreType.DMA((3, npc))])(*parts, common)


def _sibling_exchange(sums):
    n = len(sums)
    table = _piece_table([s.shape for s in sums])
    npc = len(table)

    def body(*refs):
        ins, outs = refs[:n], refs[n:2 * n]
        send_sems, recv_sems = refs[2 * n:]
        x, y, c = _mesh_pos()
        sent = []
        for q, (a, s, sz) in enumerate(table):
            sent.append(_remote(ins[a].at[pl.ds(s, sz)], outs[a].at[pl.ds(s, sz)], send_sems.at[q], recv_sems.at[q],
                                (x, y, 1 - c)))
            sent[-1].start()
        for q, (a, s, sz) in enumerate(table):
            dst = outs[a].at[pl.ds(s, sz)]
            _remote(dst, dst, send_sems.at[q], recv_sems.at[q], (x, y, c)).wait_recv()
        for cp in sent:
            cp.wait_send()

    return pl.pallas_call(
        body, name="sibling_exchange", in_specs=[ANY] * n, out_specs=[ANY] * n,
        out_shape=[jax.ShapeDtypeStruct(s.shape, s.dtype) for s in sums],
        scratch_shapes=[pltpu.SemaphoreType.DMA((npc,)), pltpu.SemaphoreType.DMA((npc,))])(*sums)


def _pair_sum(mine, got, name, tr):
    _, r, c = got.shape
    tr = min(tr, r)

    def body(p_ref, g_ref, o_ref, w_ref):
        total = p_ref[...] + g_ref[...]
        o_ref[...] = total
        w_ref[...] = total.astype(WIRE_DTYPE)

    blk = pl.BlockSpec((None, tr, c), lambda k, i: (k, i, 0))
    return pl.pallas_call(
        body, name=name, grid=(4, r // tr), in_specs=[blk, blk], out_specs=[blk, blk],
        out_shape=[jax.ShapeDtypeStruct(got.shape, F32), jax.ShapeDtypeStruct(got.shape, WIRE_DTYPE)],
        compiler_params=_cp(2))(mine, got)


def _sum_chips(own, recv, name, tr):
    r, c = own.shape
    tr = min(tr, r)

    def body(own_ref, r_ref, o_ref):
        chip = 2 * lax.axis_index("x") + lax.axis_index("y")
        own_blk = own_ref[...]
        acc = jnp.where(chip == 0, own_blk, r_ref[0].astype(F32))
        for k in range(1, 4):
            acc = acc + jnp.where(chip == k, own_blk, r_ref[k].astype(F32))
        o_ref[...] = acc

    return pl.pallas_call(
        body, name=name, grid=(r // tr,),
        in_specs=[pl.BlockSpec((tr, c), lambda i: (i, 0)), pl.BlockSpec((4, tr, c), lambda i: (0, i, 0))],
        out_specs=pl.BlockSpec((tr, c), lambda i: (i, 0)),
        out_shape=jax.ShapeDtypeStruct((r, c), F32), compiler_params=_cp())(own, recv)


def _sum_leading(parts, name, tr):
    nlead, r, c = parts.shape
    tr = min(tr, r)

    def body(p_ref, o_ref):
        acc = p_ref[0]
        for j in range(1, nlead):
            acc = acc + p_ref[j]
        o_ref[...] = acc

    return pl.pallas_call(
        body, name=name, grid=(r // tr,),
        in_specs=[pl.BlockSpec((nlead, tr, c), lambda i: (0, i, 0))], out_specs=pl.BlockSpec((tr, c), lambda i: (i, 0)),
        out_shape=jax.ShapeDtypeStruct((r, c), parts.dtype), compiler_params=_cp())(parts)


def _adamw(w, g, m, v, name, tr):
    r, c = w.shape
    tr = min(tr, r)

    def body(w_ref, g_ref, m_ref, v_ref, d_ref, nm_ref, nv_ref):
        gg = g_ref[...]
        nm = ADAM_B1 * m_ref[...] + (1.0 - ADAM_B1) * gg
        nv = ADAM_B2 * v_ref[...] + (1.0 - ADAM_B2) * jnp.square(gg)
        m_hat = nm / (1.0 - ADAM_B1 ** ADAM_STEP)
        v_hat = nv / (1.0 - ADAM_B2 ** ADAM_STEP)
        d_ref[...] = -ADAM_LR * (m_hat / (jnp.sqrt(v_hat) + ADAM_EPS) + ADAM_WD * w_ref[...])
        nm_ref[...] = nm
        nv_ref[...] = nv

    blk = pl.BlockSpec((tr, c), lambda i: (i, 0))
    return pl.pallas_call(
        body, name=name, grid=(r // tr,), in_specs=[blk] * 4, out_specs=[blk] * 3,
        out_shape=[jax.ShapeDtypeStruct((r, c), F32)] * 3, compiler_params=_cp())(w, g, m, v)


def _rope_tables(positions):
    inv_freq = ROPE_BASE ** (-jnp.arange(0, 64, 2, dtype=F32) / 64)
    ang = positions.astype(F32)[:, None] * inv_freq
    cos, sin = jnp.cos(ang), jnp.sin(ang)
    t = positions.shape[0]
    rc = jnp.concatenate([jnp.ones((t, 128), F32), cos, cos, jnp.ones((t, 64), F32)], axis=1)
    rs = jnp.concatenate([jnp.zeros((t, 128), F32), sin, sin, jnp.zeros((t, 64), F32)], axis=1)
    return rc, rs


def _layer_params(l, w_in, w_uq, w_ukv, w_out, small):
    p = {}
    p["w_in"] = jnp.concatenate([w_in[l][:, :1984], jnp.zeros((1024, 64), w_in.dtype), w_in[l][:, 1984:]], axis=1)
    p["w_uq"] = jnp.pad(w_uq[l].reshape(384, 4, 192), ((0, 0), (0, 0), (0, 64))).reshape(384, 1024)
    p["w_ukv"] = w_ukv[l].reshape(256, 4, 2, 128).transpose(0, 2, 1, 3).reshape(256, 1024)
    p["w_out"] = w_out[l]
    p["pre_g"] = small["pre_norm_g"][l][None]
    p["post_g"] = small["post_norm_g"][l][None]
    p["sgu_w"] = small["sgu_w"][l].reshape(512, 128)
    p["sgu_wt"] = small["sgu_w"][l].transpose(0, 2, 1).reshape(512, 128)
    p["sgu_bias"] = jnp.repeat(small["sgu_b"][l].T, 64, axis=1)
    p["ln_g"] = small["sgu_ln_g"][l][None]
    p["ln_b"] = small["sgu_ln_b"][l][None]
    p["pool_wbd"] = _mx(jax.scipy.linalg.block_diag(*[small["pool_w"][l][gi] for gi in range(4)]))
    p["pool_scale"] = small["pool_scale"][l][None]
    p["gq"] = small["q_norm_g"][l][None]
    p["gkv"] = small["kv_norm_g"][l][None]
    return p


def _layer_fwd(l, x, p, rc, rs, target):
    z, h = _in_proj_fwd(x, p["pre_g"], p["w_in"], f"in_proj_fwd_{l}")
    yab = _mix_fwd(z, p["sgu_w"], p["sgu_bias"], p["ln_g"], p["ln_b"], p["pool_wbd"], p["pool_scale"], f"mix_fwd_{l}")
    qh, kh, vh = _qkv_fwd(z, rc, rs, p["w_uq"], p["w_ukv"], p["gq"], p["gkv"], f"qkv_fwd_{l}")
    o, yc, lse = _attn_fwd(qh, kh, vh, z, f"attn_fwd_{l}")
    outs = _out_proj_fwd(yab, yc, p["w_out"], x, p["post_g"], target, f"out_proj_fwd_{l}")
    saved = dict(x=x, z=z, h=h, yab=yab, qh=qh, kh=kh, vh=vh, o=o, yc=yc, lse=lse, y=outs[0])
    return saved, outs[1:]


def _layer_bwd(l, dout, sv, p, rc, rs):
    dycat, dw_out, dpost = _out_proj_bwd(dout, sv["y"], sv["yab"], sv["yc"], p["w_out"], p["post_g"], f"out_proj_bwd_{l}")
    dq, dgate, dk, dv = _attn_bwd(sv["qh"], sv["kh"], sv["vh"], sv["o"], sv["lse"], dycat, sv["z"], f"attn_bwd_{l}")
    dzc, dzk, dwq, dwkv, dgq, dgkv = _qkv_bwd(dq, dk, dv, sv["z"], rc, rs, p["w_uq"], p["w_ukv"], p["gq"], p["gkv"],
                                              f"qkv_bwd_{l}")
    dzm, dsw, dsb, dlng, dlnb, dpw, dps = _mix_bwd(sv["z"], dycat, p["sgu_w"], p["sgu_wt"], p["sgu_bias"], p["ln_g"],
                                                   p["ln_b"], p["pool_wbd"], p["pool_scale"], f"mix_bwd_{l}")
    dx, dw_in, dpre = _in_proj_bwd(dzm, dzc, dzk, dgate, sv["h"], sv["x"], dout, p["w_in"], p["pre_g"], f"in_proj_bwd_{l}")
    grads = {
        "pre_norm_g": dpre[0], "post_norm_g": dpost[0],
        "w_in": jnp.concatenate([dw_in[:, :1984], dw_in[:, 2048:]], axis=1),
        "sgu_w": dsw.reshape(4, 128, 128), "sgu_b": dsb[:, :4].T, "sgu_ln_g": dlng[0], "sgu_ln_b": dlnb[0],
        "pool_w": jnp.stack([dpw[64 * gi:64 * gi + 64, 64 * gi:64 * gi + 64] for gi in range(4)]),
        "pool_scale": dps[0], "q_norm_g": dgq[0],
        "w_uq": dwq.reshape(384, 4, 256)[:, :, :192].reshape(384, 768), "kv_norm_g": dgkv[0],
        "w_ukv": dwkv.reshape(256, 2, 4, 128).transpose(0, 2, 1, 3).reshape(256, 1024), "w_out": dw_out,
    }
    return dx, grads


SMALL_NAMES = ["pre_norm_g", "post_norm_g", "sgu_w", "sgu_b", "sgu_ln_g", "sgu_ln_b", "pool_w", "pool_scale",
               "q_norm_g", "kv_norm_g"]
BIG_NAMES = ["w_in", "w_uq", "w_ukv", "w_out"]
WEIGHT_NAMES = ["pre_norm_g", "post_norm_g", "w_in", "sgu_w", "sgu_b", "sgu_ln_g", "sgu_ln_b", "pool_w", "pool_scale",
                "q_norm_g", "w_uq", "kv_norm_g", "w_ukv", "w_out"]


def _local_step(x, positions, target, w_in, w_uq, w_ukv, w_out, small):
    rc, rs = _rope_tables(positions)
    params = [_layer_params(l, w_in, w_uq, w_ukv, w_out, small) for l in range(DEPTH)]
    saved = []
    for l in range(DEPTH):
        sv, outs = _layer_fwd(l, x, params[l], rc, rs, target if l == DEPTH - 1 else None)
        saved.append(sv)
        if l < DEPTH - 1:
            x = outs[0]
    dout, loss = outs
    grads = [None] * DEPTH
    for l in reversed(range(DEPTH)):
        dout, grads[l] = _layer_bwd(l, dout, saved[l], params[l], rc, rs)
    return loss[0, 0], dout, {k: jnp.stack([grads[l][k] for l in range(DEPTH)]) for k in WEIGHT_NAMES}


def _pack_small(tree, extra=None):
    pieces = [tree[k].reshape(-1) for k in SMALL_NAMES]
    pieces.append(jnp.zeros((1,), F32) if extra is None else extra.reshape(1))
    flat = jnp.concatenate(pieces)
    rows = -(-flat.shape[0] // 1024) * 8
    return jnp.pad(flat, (0, rows * 128 - flat.shape[0])).reshape(rows, 128)


def _unpack_small(packed, like):
    flat = packed.reshape(-1)
    out, off = {}, 0
    for k in SMALL_NAMES:
        size = like[k].size
        out[k] = flat[off:off + size].reshape(like[k].shape)
        off += size
    return out, flat[off]


def kernel(x, positions, pre_norm_g, post_norm_g, w_in, sgu_w, sgu_b, sgu_ln_g, sgu_ln_b, pool_w, pool_scale, q_norm_g, w_uq, kv_norm_g, w_ukv, w_out, loss_target, m_pre_norm_g, m_post_norm_g, m_w_in, m_sgu_w, m_sgu_b, m_sgu_ln_g, m_sgu_ln_b, m_pool_w, m_pool_scale, m_q_norm_g, m_w_uq, m_kv_norm_g, m_w_ukv, m_w_out, v_pre_norm_g, v_post_norm_g, v_w_in, v_sgu_w, v_sgu_b, v_sgu_ln_g, v_sgu_ln_b, v_pool_w, v_pool_scale, v_q_norm_g, v_w_uq, v_kv_norm_g, v_w_ukv, v_w_out):
    w = dict(pre_norm_g=pre_norm_g, post_norm_g=post_norm_g, w_in=w_in, sgu_w=sgu_w, sgu_b=sgu_b, sgu_ln_g=sgu_ln_g,
             sgu_ln_b=sgu_ln_b, pool_w=pool_w, pool_scale=pool_scale, q_norm_g=q_norm_g, w_uq=w_uq, kv_norm_g=kv_norm_g,
             w_ukv=w_ukv, w_out=w_out)
    m = dict(pre_norm_g=m_pre_norm_g, post_norm_g=m_post_norm_g, w_in=m_w_in, sgu_w=m_sgu_w, sgu_b=m_sgu_b,
             sgu_ln_g=m_sgu_ln_g, sgu_ln_b=m_sgu_ln_b, pool_w=m_pool_w, pool_scale=m_pool_scale, q_norm_g=m_q_norm_g,
             w_uq=m_w_uq, kv_norm_g=m_kv_norm_g, w_ukv=m_w_ukv, w_out=m_w_out)
    v = dict(pre_norm_g=v_pre_norm_g, post_norm_g=v_post_norm_g, w_in=v_w_in, sgu_w=v_sgu_w, sgu_b=v_sgu_b,
             sgu_ln_g=v_sgu_ln_g, sgu_ln_b=v_sgu_ln_b, pool_w=v_pool_w, pool_scale=v_pool_scale, q_norm_g=v_q_norm_g,
             w_uq=v_w_uq, kv_norm_g=v_kv_norm_g, w_ukv=v_w_ukv, w_out=v_w_out)

    core = lax.axis_index("c")
    chip = 2 * lax.axis_index("x") + lax.axis_index("y")
    shards = [_mx(w[k]) for k in BIG_NAMES]
    gathered = _gather_weights(shards)
    g_in, g_uq, g_ukv, g_out = [lax.dynamic_update_slice(g, s, (2 * chip, 0, 0)) for g, s in zip(gathered, shards)]
    cols = lambda g: g.reshape((4, 2) + g.shape[1:]).transpose(1, 2, 0, 3).reshape(2, g.shape[1], 4 * g.shape[2])
    full_out = g_out.reshape(4, 2, 256, 1024).transpose(1, 0, 2, 3).reshape(2, 1024, 1024)
    loss, dx, grads = _local_step(x[0], positions[0], loss_target[0], cols(g_in), cols(g_uq), cols(g_ukv), full_out, w)

    split_cols = lambda g: g.reshape(2, g.shape[1], 4, g.shape[2] // 4).transpose(2, 0, 1, 3).reshape(8, g.shape[1], g.shape[2] // 4)
    parts = [split_cols(grads["w_in"]), split_cols(grads["w_uq"]), split_cols(grads["w_ukv"]),
             grads["w_out"].reshape(2, 4, 256, 1024).transpose(1, 0, 2, 3).reshape(8, 256, 1024)]
    common = _pack_small(grads, loss)
    got = _pair_exchange(parts, common)
    mine = [lax.dynamic_index_in_dim(p.reshape((4, 2) + p.shape[1:]), core, axis=1, keepdims=False) for p in parts]
    pair_sums = [_pair_sum(mine[a], got[a], f"pair_sum_{BIG_NAMES[a]}", 128) for a in range(4)]
    chip_common = _sum_leading(jnp.stack([common, got[4]]), "pair_sum_small", common.shape[0])
    received = _chip_exchange([ps[1] for ps in pair_sums], chip_common)
    sums = [_sum_chips(lax.dynamic_index_in_dim(pair_sums[a][0], chip, axis=0, keepdims=False), received[a],
                       f"sum_{BIG_NAMES[a]}", 128) for a in range(4)]
    all_common = lax.dynamic_update_slice(received[4], chip_common[None], (chip, 0, 0))
    small_sum, loss = _unpack_small(_sum_leading(all_common, "sum_small", all_common.shape[1]), w)
    others = _sibling_exchange(sums)
    total = dict(small_sum)
    for a, k in enumerate(BIG_NAMES):
        total[k] = jnp.where(core == 0, jnp.stack([sums[a], others[a]]), jnp.stack([others[a], sums[a]]))

    packed = _adamw(_pack_small(w), _pack_small(total), _pack_small(m), _pack_small(v), "adamw_small", 2048)
    small_out = [_unpack_small(pk, w)[0] for pk in packed]
    delta, new_m, new_v = {}, {}, {}
    for k in SMALL_NAMES:
        delta[k], new_m[k], new_v[k] = (so[k] for so in small_out)
    for k in BIG_NAMES:
        shape = w[k].shape
        flat = lambda a: a.reshape(shape[0] * shape[1], shape[2])
        res = _adamw(flat(w[k]), flat(total[k]), flat(m[k]), flat(v[k]), f"adamw_{k}", 256)
        delta[k], new_m[k], new_v[k] = (r.reshape(shape) for r in res)

    return (loss, dx[None], *[total[k] for k in WEIGHT_NAMES], *[delta[k] for k in WEIGHT_NAMES],
            *[new_m[k] for k in WEIGHT_NAMES], *[new_v[k] for k in WEIGHT_NAMES])
```

```python
import jax
import jax.numpy as jnp
from jax import lax
from jax.experimental import pallas as pl
from jax.experimental.pallas import tpu as pltpu

F32 = jnp.float32
MXU_DTYPE = jnp.bfloat16
WIRE_DTYPE = jnp.bfloat16
EPS = 1e-6
NEG_INF = -1e30
DEPTH = 2
N_HEADS = 4
QK_PAD = 256
V_DIM = 128
SCALE = 192 ** -0.5
LOG2E = 1.4426950408889634
ROPE_BASE = 10000.0
ADAM_LR, ADAM_B1, ADAM_B2, ADAM_EPS, ADAM_WD, ADAM_STEP = 0.001, 0.9, 0.999, 1e-08, 0.01, 10
VMEM_LIMIT_BYTES = 56 * 1024 * 1024
MESH = pl.DeviceIdType.MESH
ANY = pl.BlockSpec(memory_space=pl.ANY)

Z_MIX, Z_C, Z_KR, Z_GATE = 1280, 640, 128, 512
Z_W = Z_MIX + Z_C + Z_KR + Z_GATE


def _cp(n_axes=1):
    return pltpu.CompilerParams(dimension_semantics=("arbitrary",) * n_axes, vmem_limit_bytes=VMEM_LIMIT_BYTES)


def _dot(a, b):
    return lax.dot_general(a, b, (((1,), (0,)), ((), ())), preferred_element_type=F32)


def _dot_nt(a, b):
    return lax.dot_general(a, b, (((1,), (1,)), ((), ())), preferred_element_type=F32)


def _dot_tn(a, b):
    return lax.dot_general(a, b, (((0,), (0,)), ((), ())), preferred_element_type=F32)


def _mx(a):
    return a.astype(MXU_DTYPE)


def _silu_and_grad(g):
    sg = jax.nn.sigmoid(g)
    return g * sg, sg * (1.0 + g * (1.0 - sg))


def _rms(x, g):
    r = lax.rsqrt(jnp.mean(x * x, axis=-1, keepdims=True) + EPS)
    return x * r * g, r


def _rms_bwd(x, r, g, dy):
    xhat = x * r
    dyg = dy * g
    dx = r * (dyg - xhat * jnp.mean(dyg * xhat, axis=-1, keepdims=True))
    return dx, dy * xhat


def _zero_when(first, *refs):
    @pl.when(first)
    def _():
        for ref in refs:
            ref[...] = jnp.zeros(ref.shape, ref.dtype)


def _acc(ref, val):
    ref[...] += val


def _colsum(a):
    return jnp.sum(a, axis=0, keepdims=True)


def _in_proj_fwd(x, g, w, name, tm=512):
    t, d = x.shape
    n = w.shape[1]
    tm = min(tm, t)

    def body(x_ref, g_ref, w_ref, z_ref, h_ref):
        h, _ = _rms(x_ref[...], g_ref[...])
        h = _mx(h)
        h_ref[...] = h
        z_ref[...] = _dot(h, w_ref[...])

    return pl.pallas_call(
        body, name=name, grid=(t // tm,),
        in_specs=[pl.BlockSpec((tm, d), lambda i: (i, 0)), pl.BlockSpec((1, d), lambda i: (0, 0)),
                  pl.BlockSpec((d, n), lambda i: (0, 0))],
        out_specs=[pl.BlockSpec((tm, n), lambda i: (i, 0)), pl.BlockSpec((tm, d), lambda i: (i, 0))],
        out_shape=[jax.ShapeDtypeStruct((t, n), F32), jax.ShapeDtypeStruct((t, d), MXU_DTYPE)],
        compiler_params=_cp())(x, g, w)


def _in_proj_bwd(dz_mix, dz_c, dz_kr, dz_gate, h, x, d_res, w, g, name, tm=512):
    t, d = x.shape
    n = w.shape[1]
    tm = min(tm, t)

    def body(dm_ref, dc_ref, dk_ref, dg_ref, h_ref, x_ref, dres_ref, w_ref, g_ref, dx_ref, dw_ref, dgn_ref):
        first = pl.program_id(0) == 0
        dz = jnp.concatenate([dm_ref[...], dc_ref[...], dk_ref[...], dg_ref[...]], axis=1)

        _zero_when(first, dw_ref, dgn_ref)
        hb = h_ref[...]
        for c0 in range(0, n, 512):
            dw_ref[:, c0:c0 + 512] += _dot_tn(hb, dz[:, c0:c0 + 512])
        dh = _dot_nt(dz, w_ref[...])
        xf = x_ref[...]
        r = lax.rsqrt(jnp.mean(xf * xf, axis=-1, keepdims=True) + EPS)
        dx, dgt = _rms_bwd(xf, r, g_ref[...], dh)
        dx_ref[...] = dx + dres_ref[...]
        _acc(dgn_ref, _colsum(dgt))

    row = lambda wd: pl.BlockSpec((tm, wd), lambda i: (i, 0))
    fixed = lambda a, b: pl.BlockSpec((a, b), lambda i: (0, 0), pipeline_mode=pl.Buffered(1))
    return pl.pallas_call(
        body, name=name, grid=(t // tm,),
        in_specs=[row(Z_MIX), row(Z_C), row(Z_KR), row(Z_GATE), row(d), row(d), row(d), fixed(d, n), fixed(1, d)],
        out_specs=[row(d), fixed(d, n), fixed(1, d)],
        out_shape=[jax.ShapeDtypeStruct((t, d), F32), jax.ShapeDtypeStruct((d, n), F32),
                   jax.ShapeDtypeStruct((1, d), F32)],
        compiler_params=_cp())(dz_mix, dz_c, dz_kr, dz_gate, h, x, d_res, w, g)


def _lane_group(shape):
    return lax.broadcasted_iota(jnp.int32, shape, 1) // 64


def _select_group(vals):
    grp = _lane_group(vals[0].shape)
    out = vals[3]
    for gi in (2, 1, 0):
        out = jnp.where(grp == gi, vals[gi], out)
    return out


def _sgu_mask(transposed):
    r = (lax.broadcasted_iota(jnp.int32, (512, 128), 0) % 128) // 64
    c = lax.broadcasted_iota(jnp.int32, (512, 128), 1) // 64
    return (r <= c) if transposed else (c <= r)


def _sgu_apply(wstack, vb, nblk):
    outs = []
    for n in range(nblk):
        r = _dot(wstack, vb[n * 128:(n + 1) * 128, :])
        outs.append(_select_group([r[hh * 128:(hh + 1) * 128, :] for hh in range(4)]))
    return jnp.concatenate(outs, axis=0)


def _layer_norm(v, g, b):
    mu = jnp.mean(v, axis=-1, keepdims=True)
    vc = v - mu
    rstd = lax.rsqrt(jnp.mean(vc * vc, axis=-1, keepdims=True) + EPS)
    vhat = vc * rstd
    return vhat * g + b, vhat, rstd


def _pool_counts(t0, n):
    t = t0 + lax.broadcasted_iota(jnp.int32, (n, 256), 0)
    w = _select_group([jnp.full((n, 256), wv, jnp.int32) for wv in (2, 4, 8, 16)])
    return jnp.minimum(t + 1, w).astype(F32)


def _pooled(p, halo, t0):
    tm = p.shape[0]
    ext = jnp.concatenate([halo, p], axis=0)
    s2 = ext + pltpu.roll(ext, 1, 0)
    s4 = s2 + pltpu.roll(s2, 2, 0)
    s8 = s4 + pltpu.roll(s4, 4, 0)
    s16 = s8 + pltpu.roll(s8, 8, 0)
    sel = _select_group([s2, s4, s8, s16])[16:, :]
    return sel / _pool_counts(t0, tm) - p


def _pooled_bwd(dpool, dpool_halo, t0):
    tm = dpool.shape[0]
    n = tm + 16
    ext = jnp.concatenate([dpool, dpool_halo], axis=0) / _pool_counts(t0, n)
    f2 = ext + pltpu.roll(ext, n - 1, 0)
    f4 = f2 + pltpu.roll(f2, n - 2, 0)
    f8 = f4 + pltpu.roll(f4, n - 4, 0)
    f16 = f8 + pltpu.roll(f8, n - 8, 0)
    return _select_group([f2, f4, f8, f16])[:tm, :] - dpool


def _mix_specs(t, tm):
    nt16 = t // 16
    zrow = pl.BlockSpec((tm, Z_MIX), lambda i: (i, 0))
    prev_halo = pl.BlockSpec((16, 256), lambda i: (jnp.maximum(i * (tm // 16) - 1, 0), 3))
    fixed = lambda a, b: pl.BlockSpec((a, b), lambda i: (0, 0))
    params = [fixed(512, 128), fixed(128, 256), fixed(1, 256), fixed(1, 256), fixed(256, 256), fixed(1, 256)]
    return nt16, zrow, prev_halo, fixed, params


def _mix_fwd(z, sgu_w, sgu_bias, ln_g, ln_b, pool_wbd, pool_scale, name, tm=512):
    t = z.shape[0]
    tm = min(tm, t)
    _, zrow, prev_halo, _, params = _mix_specs(t, tm)

    def body(z_ref, halo_ref, w_ref, bias_ref, lng_ref, lnb_ref, pw_ref, ps_ref, y_ref):
        i = pl.program_id(0)
        u, v, gate = z_ref[:, 0:256], z_ref[:, 256:512], z_ref[:, 512:768]
        p, pgate = z_ref[:, 768:1024], z_ref[:, 1024:1280]
        vn, _, _ = _layer_norm(v, lng_ref[...], lnb_ref[...])
        wm = _mx(jnp.where(_sgu_mask(False), w_ref[...], 0.0))
        mixed = _sgu_apply(wm, _mx(vn), tm // 128) + jnp.tile(bias_ref[...], (tm // 128, 1))
        ya = u * mixed * _silu_and_grad(gate)[0]
        halo = jnp.where(i > 0, halo_ref[...], 0.0)
        pooled = _pooled(p, halo, i * tm)
        yb = _dot(_mx(pooled), pw_ref[...]) * ps_ref[...] * _silu_and_grad(pgate)[0]
        y_ref[...] = _mx(jnp.concatenate([ya, yb], axis=1))

    return pl.pallas_call(
        body, name=name, grid=(t // tm,),
        in_specs=[zrow, prev_halo] + params,
        out_specs=pl.BlockSpec((tm, 512), lambda i: (i, 0)),
        out_shape=jax.ShapeDtypeStruct((t, 512), MXU_DTYPE),
        compiler_params=_cp())(z, z, sgu_w, sgu_bias, ln_g, ln_b, pool_wbd, pool_scale)


def _mix_bwd(z, dycat, sgu_w, sgu_wt, sgu_bias, ln_g, ln_b, pool_wbd, pool_scale, name, tm=512):
    t = z.shape[0]
    tm = min(tm, t)
    nt16, zrow, prev_halo, fixed, params = _mix_specs(t, tm)
    nblk = tm // 128
    last = t // tm - 1

    def body(z_ref, halo_ref, zn_ref, dy_ref, dyn_ref, w_ref, wt_ref, bias_ref, lng_ref, lnb_ref, pw_ref, ps_ref,
             dz_ref, dw_ref, db_ref, dlng_ref, dlnb_ref, dpw_ref, dps_ref):
        i = pl.program_id(0)
        _zero_when(i == 0, dw_ref, db_ref, dlng_ref, dlnb_ref, dpw_ref, dps_ref)
        u, v, gate = z_ref[:, 0:256], z_ref[:, 256:512], z_ref[:, 512:768]
        p, pgate = z_ref[:, 768:1024], z_ref[:, 1024:1280]
        dya, dyb = dy_ref[:, 0:256], dy_ref[:, 256:512]
        vn, vhat, rstd = _layer_norm(v, lng_ref[...], lnb_ref[...])
        vnb = _mx(vn)
        wm = _mx(jnp.where(_sgu_mask(False), w_ref[...], 0.0))
        wmt = _mx(jnp.where(_sgu_mask(True), wt_ref[...], 0.0))
        mixed = _sgu_apply(wm, vnb, nblk) + jnp.tile(bias_ref[...], (nblk, 1))
        silu, dsilu = _silu_and_grad(gate)
        t1 = u * mixed
        d_gate = dya * t1 * dsilu
        d_t1 = dya * silu
        d_u = d_t1 * mixed
        d_mixed = d_t1 * u
        dmb = _mx(d_mixed)
        d_vn = _sgu_apply(wmt, dmb, nblk)
        grp = _lane_group((128, 256))
        lane = lax.broadcasted_iota(jnp.int32, (128, 128), 1)
        dws = [jnp.zeros((128, 128), F32) for _ in range(4)]
        dbias = jnp.zeros((128, 128), F32)
        for n in range(nblk):
            dm_n, dmb_n, vnb_n = d_mixed[n * 128:(n + 1) * 128], dmb[n * 128:(n + 1) * 128], vnb[n * 128:(n + 1) * 128]
            for hh in range(4):
                dws[hh] = dws[hh] + _dot_nt(jnp.where(grp == hh, dmb_n, jnp.zeros_like(dmb_n)), vnb_n)
                rs = jnp.sum(jnp.where(grp == hh, dm_n, 0.0), axis=-1, keepdims=True)
                dbias = dbias + jnp.where(lane == hh, rs, 0.0)
        _acc(dw_ref, jnp.concatenate(dws, axis=0))
        _acc(db_ref, dbias)
        _acc(dlng_ref, _colsum(d_vn * vhat))
        _acc(dlnb_ref, _colsum(d_vn))
        dvh = d_vn * lng_ref[...]
        d_v = rstd * (dvh - jnp.mean(dvh, axis=-1, keepdims=True) - vhat * jnp.mean(dvh * vhat, axis=-1, keepdims=True))

        @pl.when(i == last)
        def _():
            dw_ref[...] = jnp.where(_sgu_mask(False), dw_ref[...], 0.0)

        halo = jnp.where(i > 0, halo_ref[...], 0.0)
        pooled = _pooled(p, halo, i * tm)
        pooled_b = _mx(pooled)
        mixedp = _dot(pooled_b, pw_ref[...])
        psilu, pdsilu = _silu_and_grad(pgate)
        d_pgate = dyb * (mixedp * ps_ref[...]) * pdsilu
        d_ms = dyb * psilu
        _acc(dps_ref, _colsum(d_ms * mixedp))
        dmpb = _mx(d_ms * ps_ref[...])
        _acc(dpw_ref, _dot_tn(pooled_b, dmpb))
        d_pooled = _dot_nt(dmpb, pw_ref[...])
        dmp_halo = _mx(dyn_ref[...] * _silu_and_grad(zn_ref[...])[0] * ps_ref[...])
        d_pooled_halo = jnp.where(i < last, _dot_nt(dmp_halo, pw_ref[...]), 0.0)
        d_p = _pooled_bwd(d_pooled, d_pooled_halo, i * tm)
        dz_ref[...] = _mx(jnp.concatenate([d_u, d_v, d_gate, d_p, d_pgate], axis=1))

    nxt = lambda i: jnp.minimum((i + 1) * (tm // 16), nt16 - 1)
    return pl.pallas_call(
        body, name=name, grid=(t // tm,),
        in_specs=[zrow, prev_halo, pl.BlockSpec((16, 256), lambda i: (nxt(i), 4)),
                  pl.BlockSpec((tm, 512), lambda i: (i, 0)), pl.BlockSpec((16, 256), lambda i: (nxt(i), 1)),
                  params[0], fixed(512, 128)] + params[1:],
        out_specs=[pl.BlockSpec((tm, Z_MIX), lambda i: (i, 0)), fixed(512, 128), fixed(128, 128), fixed(1, 256),
                   fixed(1, 256), fixed(256, 256), fixed(1, 256)],
        out_shape=[jax.ShapeDtypeStruct((t, Z_MIX), MXU_DTYPE), jax.ShapeDtypeStruct((512, 128), F32),
                   jax.ShapeDtypeStruct((128, 128), F32), jax.ShapeDtypeStruct((1, 256), F32),
                   jax.ShapeDtypeStruct((1, 256), F32), jax.ShapeDtypeStruct((256, 256), F32),
                   jax.ShapeDtypeStruct((1, 256), F32)],
        compiler_params=_cp())(z, z, z, dycat, dycat, sgu_w, sgu_wt, sgu_bias, ln_g, ln_b, pool_wbd, pool_scale)


def _rot_half(x, transpose):
    w = x.shape[1]
    lane = lax.broadcasted_iota(jnp.int32, x.shape, 1) % min(w, 256)
    base = 128 if w >= 256 else 0
    lo = jnp.logical_and(lane >= base, lane < base + 32)
    hi = jnp.logical_and(lane >= base + 32, lane < base + 64)
    up = pltpu.roll(x, w - 32, 1)
    down = pltpu.roll(x, 32, 1)
    if transpose:
        return jnp.where(lo, up, jnp.where(hi, -down, 0.0))
    return jnp.where(lo, -up, jnp.where(hi, down, 0.0))


def _rope(x, c, s):
    return x * c + _rot_half(x, False) * s


def _rope_bwd(dy, c, s):
    return dy * c + _rot_half(dy * s, True)


def _qkv_fwd(z, rc, rs, w_uq, w_ukv, gq, gkv, name, tm=512):
    t = z.shape[0]
    tm = min(tm, t)

    def body(zc_ref, zk_ref, rc_ref, rs_ref, wq_ref, wkv_ref, gq_ref, gkv_ref, q_ref, k_ref, v_ref):
        cq, ckv = zc_ref[:, 0:384], zc_ref[:, 384:640]
        c, s = rc_ref[...], rs_ref[...]
        qn, _ = _rms(cq, gq_ref[...])
        q = _rope(_dot(_mx(qn), wq_ref[...]), jnp.tile(c, (1, N_HEADS)), jnp.tile(s, (1, N_HEADS)))
        kvn, _ = _rms(ckv, gkv_ref[...])
        kv = _dot(_mx(kvn), wkv_ref[...])
        kpe = _rope(zk_ref[...], c[:, 128:256], s[:, 128:256])
        for hh in range(N_HEADS):
            q_ref[hh] = _mx(q[:, hh * QK_PAD:(hh + 1) * QK_PAD])
            k_ref[hh] = _mx(jnp.concatenate([kv[:, hh * 128:(hh + 1) * 128], kpe], axis=1))
            v_ref[hh] = _mx(kv[:, 512 + hh * 128:512 + (hh + 1) * 128])

    fixed = lambda a, b: pl.BlockSpec((a, b), lambda i: (0, 0))
    heads = lambda wd: pl.BlockSpec((N_HEADS, tm, wd), lambda i: (0, i, 0))
    return pl.pallas_call(
        body, name=name, grid=(t // tm,),
        in_specs=[pl.BlockSpec((tm, Z_C), lambda i: (i, Z_MIX // Z_C)),
                  pl.BlockSpec((tm, Z_KR), lambda i: (i, (Z_MIX + Z_C) // Z_KR)),
                  pl.BlockSpec((tm, 256), lambda i: (i, 0)), pl.BlockSpec((tm, 256), lambda i: (i, 0)),
                  fixed(384, 1024), fixed(256, 1024), fixed(1, 384), fixed(1, 256)],
        out_specs=[heads(QK_PAD), heads(QK_PAD), heads(V_DIM)],
        out_shape=[jax.ShapeDtypeStruct((N_HEADS, t, QK_PAD), MXU_DTYPE),
                   jax.ShapeDtypeStruct((N_HEADS, t, QK_PAD), MXU_DTYPE),
                   jax.ShapeDtypeStruct((N_HEADS, t, V_DIM), MXU_DTYPE)],
        compiler_params=_cp())(z, z, rc, rs, w_uq, w_ukv, gq, gkv)


def _qkv_bwd(dq, dk, dv, z, rc, rs, w_uq, w_ukv, gq, gkv, name, tm=512):
    t = z.shape[0]
    tm = min(tm, t)

    def body(dq_ref, dk_ref, dv_ref, zc_ref, rc_ref, rs_ref, wq_ref, wkv_ref, gq_ref, gkv_ref,
             dzc_ref, dzk_ref, dwq_ref, dwkv_ref, dgq_ref, dgkv_ref):
        _zero_when(pl.program_id(0) == 0, dwq_ref, dwkv_ref, dgq_ref, dgkv_ref)
        cq, ckv = zc_ref[:, 0:384], zc_ref[:, 384:640]
        c, s = rc_ref[...], rs_ref[...]
        dq_all = jnp.concatenate([dq_ref[hh] for hh in range(N_HEADS)], axis=1)
        dqp = _mx(_rope_bwd(dq_all, jnp.tile(c, (1, N_HEADS)), jnp.tile(s, (1, N_HEADS))))
        qn, rq = _rms(cq, gq_ref[...])
        _acc(dwq_ref, _dot_tn(_mx(qn), dqp))
        d_qn = _dot_nt(dqp, wq_ref[...])
        dkpe = dk_ref[0][:, 128:256]
        for hh in range(1, N_HEADS):
            dkpe = dkpe + dk_ref[hh][:, 128:256]
        dzk_ref[...] = _mx(_rope_bwd(dkpe, c[:, 128:256], s[:, 128:256]))
        dkv = _mx(jnp.concatenate([dk_ref[hh][:, 0:128] for hh in range(N_HEADS)]
                                  + [dv_ref[hh] for hh in range(N_HEADS)], axis=1))
        kvn, rkv = _rms(ckv, gkv_ref[...])
        _acc(dwkv_ref, _dot_tn(_mx(kvn), dkv))
        d_kvn = _dot_nt(dkv, wkv_ref[...])
        d_cq, dgq_t = _rms_bwd(cq, rq, gq_ref[...], d_qn)
        _acc(dgq_ref, _colsum(dgq_t))
        d_ckv, dgkv_t = _rms_bwd(ckv, rkv, gkv_ref[...], d_kvn)
        _acc(dgkv_ref, _colsum(dgkv_t))
        dzc_ref[...] = _mx(jnp.concatenate([d_cq, d_ckv], axis=1))

    fixed = lambda a, b: pl.BlockSpec((a, b), lambda i: (0, 0))
    heads = lambda wd: pl.BlockSpec((N_HEADS, tm, wd), lambda i: (0, i, 0))
    return pl.pallas_call(
        body, name=name, grid=(t // tm,),
        in_specs=[heads(QK_PAD), heads(QK_PAD), heads(V_DIM), pl.BlockSpec((tm, Z_C), lambda i: (i, Z_MIX // Z_C)),
                  pl.BlockSpec((tm, 256), lambda i: (i, 0)), pl.BlockSpec((tm, 256), lambda i: (i, 0)),
                  fixed(384, 1024), fixed(256, 1024), fixed(1, 384), fixed(1, 256)],
        out_specs=[pl.BlockSpec((tm, Z_C), lambda i: (i, 0)), pl.BlockSpec((tm, Z_KR), lambda i: (i, 0)),
                   fixed(384, 1024), fixed(256, 1024), fixed(1, 384), fixed(1, 256)],
        out_shape=[jax.ShapeDtypeStruct((t, Z_C), MXU_DTYPE), jax.ShapeDtypeStruct((t, Z_KR), MXU_DTYPE),
                   jax.ShapeDtypeStruct((384, 1024), F32), jax.ShapeDtypeStruct((256, 1024), F32),
                   jax.ShapeDtypeStruct((1, 384), F32), jax.ShapeDtypeStruct((1, 256), F32)],
        compiler_params=_cp())(dq, dk, dv, z, rc, rs, w_uq, w_ukv, gq, gkv)


def _loop_in_long_trips(n, body):
    def two(t, carry):
        return body(2 * t + 1, body(2 * t, carry))

    def four(t, carry):
        return two(2 * t + 1, two(2 * t, carry))

    lax.fori_loop(0, n // 4, four, 0)
    lax.fori_loop(2 * (n // 4), n // 2, two, 0)
    lax.fori_loop(2 * (n // 2), n, body, 0)


def _init_mask_bias(bias_ref):
    _, tq, tk = bias_ref.shape
    r = lax.broadcasted_iota(jnp.int32, (tq, tk), 0) // 64
    c = lax.broadcasted_iota(jnp.int32, (tq, tk), 1) // 64
    bias_ref[0] = jnp.zeros((tq, tk), F32)
    for d in range(tq // tk):
        bias_ref[1 + d] = jnp.where(c + d * (tk // 64) <= r, 0.0, NEG_INF)


def _gate_block(tq):
    return pl.BlockSpec((tq, 128), lambda h, i: (i, (Z_MIX + Z_C + Z_KR) // 128 + h))


def _attn_fwd(qh, kh, vh, z, name, tq=1024, tk=512):
    t = qh.shape[1]
    tq = min(tq, t)
    tk = min(tk, tq)
    ratio = tq // tk

    def body(q_ref, g_ref, k_hbm, v_hbm, o_ref, yc_ref, lse_ref, k_v, v_v, m_s, acc_s, s_a, s_b, mx_a, mx_b, bias_s,
             sem):
        h, i = pl.program_id(0), pl.program_id(1)

        @pl.when(i == 0)
        def _():
            ck = pltpu.make_async_copy(k_hbm.at[h], k_v, sem.at[0])
            cv = pltpu.make_async_copy(v_hbm.at[h], v_v.at[:, 0:V_DIM], sem.at[1])
            ck.start()
            cv.start()
            v_v[:, V_DIM:2 * V_DIM] = jnp.ones((t, V_DIM), MXU_DTYPE)
            _init_mask_bias(bias_s)
            ck.wait()
            cv.wait()

        q = q_ref[...]
        m_s[...] = jnp.full(m_s.shape, NEG_INF, F32)
        acc_s[...] = jnp.zeros(acc_s.shape, F32)

        last = ratio * (i + 1) - 1

        def keys(j):
            return pl.ds(pl.multiple_of(j * tk, tk), tk)

        def scores(s_ref, mx_ref, j):
            s = _dot_nt(q, k_v[keys(j), :]) * (SCALE * LOG2E) + bias_s[jnp.maximum(j - ratio * i + 1, 0)]
            s_ref[...] = s
            mx_ref[...] = jnp.broadcast_to(jnp.max(s, axis=-1, keepdims=True), mx_ref.shape)

        def softmax_pv(s_ref, mx_ref, j):
            m_old = m_s[...]
            m_new = jnp.maximum(m_old, mx_ref[...])
            p = jnp.exp2(s_ref[...] - jnp.tile(m_new, (1, tk // 128)))
            alpha = jnp.exp2(m_old - m_new)
            m_s[...] = m_new
            acc_s[...] = jnp.tile(alpha, (1, 2)) * acc_s[...] + _dot(_mx(p), v_v[keys(j), :])

        scores(s_a, mx_a, 0)

        def pair(pp, carry):
            scores(s_b, mx_b, 2 * pp + 1)
            softmax_pv(s_a, mx_a, 2 * pp)
            scores(s_a, mx_a, jnp.minimum(2 * pp + 2, last))
            softmax_pv(s_b, mx_b, 2 * pp + 1)
            return carry

        _loop_in_long_trips((last + 1) // 2, pair)
        if ratio % 2 == 1:
            @pl.when(last % 2 == 0)
            def _():
                softmax_pv(s_a, mx_a, last)

        l = acc_s[:, V_DIM:2 * V_DIM]
        o = acc_s[:, 0:V_DIM] / l
        o_ref[...] = o
        yc_ref[...] = _mx(o * _silu_and_grad(g_ref[...])[0])
        lse_ref[...] = m_s[...] + jnp.log2(l)

    return pl.pallas_call(
        body, name=name, grid=(N_HEADS, t // tq),
        in_specs=[pl.BlockSpec((None, tq, QK_PAD), lambda h, i: (h, i, 0)), _gate_block(tq), ANY, ANY],
        out_specs=[pl.BlockSpec((tq, 128), lambda h, i: (i, h)), pl.BlockSpec((tq, 128), lambda h, i: (i, h)),
                   pl.BlockSpec((None, tq, 128), lambda h, i: (h, i, 0))],
        out_shape=[jax.ShapeDtypeStruct((t, N_HEADS * V_DIM), F32), jax.ShapeDtypeStruct((t, N_HEADS * V_DIM), MXU_DTYPE),
                   jax.ShapeDtypeStruct((N_HEADS, t, 128), F32)],
        scratch_shapes=[pltpu.VMEM((t, QK_PAD), MXU_DTYPE), pltpu.VMEM((t, 2 * V_DIM), MXU_DTYPE),
                        pltpu.VMEM((tq, 128), F32), pltpu.VMEM((tq, 2 * V_DIM), F32),
                        pltpu.VMEM((tq, tk), F32), pltpu.VMEM((tq, tk), F32), pltpu.VMEM((tq, 128), F32),
                        pltpu.VMEM((tq, 128), F32), pltpu.VMEM((ratio + 1, tq, tk), F32),
                        pltpu.SemaphoreType.DMA((2,))],
        compiler_params=_cp(2))(qh, z, kh, vh)


def _attn_bwd(qh, kh, vh, o, lse, dycat, z, name, tq=512):
    t = qh.shape[1]
    tq = min(tq, t)
    nq = t // tq

    def body(q_ref, o_ref, lse_ref, dy_ref, g_ref, k_hbm, v_hbm, dq_ref, dgate_ref, dk_hbm, dv_hbm,
             k_v, v_v, dk_acc, dv_acc, dq_acc, delta_s, s_a, dp_a, s_b, dp_b, bias_s, sem):
        h, i = pl.program_id(0), pl.program_id(1)

        @pl.when(i == 0)
        def _():
            ck = pltpu.make_async_copy(k_hbm.at[h], k_v, sem.at[0])
            cv = pltpu.make_async_copy(v_hbm.at[h], v_v, sem.at[1])
            ck.start()
            cv.start()
            _init_mask_bias(bias_s)
            dk_acc[...] = jnp.zeros(dk_acc.shape, F32)
            dv_acc[...] = jnp.zeros(dv_acc.shape, F32)
            ck.wait()
            cv.wait()

        gate, dy, of = g_ref[...], dy_ref[...], o_ref[...]
        silu, dsilu = _silu_and_grad(gate)
        do = dy * silu
        delta = jnp.sum(do * of, axis=-1, keepdims=True)
        dgate_ref[...] = _mx(dy * of * dsilu)
        dob = _mx(do)
        q = q_ref[...]
        delta_s[...] = jnp.broadcast_to(delta, delta_s.shape)
        dq_acc[...] = jnp.zeros(dq_acc.shape, F32)

        def keys(j):
            return pl.ds(pl.multiple_of(j * tq, tq), tq)

        def scores(s_ref, dp_ref, j):
            s = _dot_nt(q, k_v[keys(j), :]) * (SCALE * LOG2E) + bias_s[(j == i).astype(jnp.int32)]
            s_ref[...] = s - jnp.tile(lse_ref[...], (1, tq // 128))
            dp_ref[...] = _dot_nt(dob, v_v[keys(j), :]) - jnp.tile(delta_s[...], (1, tq // 128))

        def grads(s_ref, dp_ref, j):
            ks = keys(j)
            p = jnp.exp2(s_ref[...])
            ds = p * dp_ref[...] * SCALE
            pb, dsb = _mx(p), _mx(ds)
            dq_acc[...] += _dot(dsb, k_v[ks, :])
            dk_acc[ks, :] += _dot_tn(dsb, q)
            dv_acc[ks, :] += _dot_tn(pb, dob)

        scores(s_a, dp_a, 0)

        def pair(pp, carry):
            scores(s_b, dp_b, 2 * pp + 1)
            grads(s_a, dp_a, 2 * pp)
            scores(s_a, dp_a, jnp.minimum(2 * pp + 2, i))
            grads(s_b, dp_b, 2 * pp + 1)
            return carry

        _loop_in_long_trips((i + 1) // 2, pair)

        @pl.when(i % 2 == 0)
        def _():
            grads(s_a, dp_a, i)

        dq_ref[...] = dq_acc[...]

        @pl.when(i == nq - 1)
        def _():
            ck = pltpu.make_async_copy(dk_acc, dk_hbm.at[h], sem.at[0])
            cv = pltpu.make_async_copy(dv_acc, dv_hbm.at[h], sem.at[1])
            ck.start()
            cv.start()
            ck.wait()
            cv.wait()

    return pl.pallas_call(
        body, name=name, grid=(N_HEADS, nq),
        in_specs=[pl.BlockSpec((None, tq, QK_PAD), lambda h, i: (h, i, 0)),
                  pl.BlockSpec((tq, 128), lambda h, i: (i, h)),
                  pl.BlockSpec((None, tq, 128), lambda h, i: (h, i, 0)),
                  pl.BlockSpec((tq, 128), lambda h, i: (i, N_HEADS + h)), _gate_block(tq), ANY, ANY],
        out_specs=[pl.BlockSpec((None, tq, QK_PAD), lambda h, i: (h, i, 0)),
                   pl.BlockSpec((tq, 128), lambda h, i: (i, h)), ANY, ANY],
        out_shape=[jax.ShapeDtypeStruct((N_HEADS, t, QK_PAD), F32), jax.ShapeDtypeStruct((t, Z_GATE), MXU_DTYPE),
                   jax.ShapeDtypeStruct((N_HEADS, t, QK_PAD), F32), jax.ShapeDtypeStruct((N_HEADS, t, V_DIM), F32)],
        scratch_shapes=[pltpu.VMEM((t, QK_PAD), MXU_DTYPE), pltpu.VMEM((t, V_DIM), MXU_DTYPE),
                        pltpu.VMEM((t, QK_PAD), F32), pltpu.VMEM((t, V_DIM), F32), pltpu.VMEM((tq, QK_PAD), F32),
                        pltpu.VMEM((tq, 128), F32)] + [pltpu.VMEM((tq, tq), F32)] * 4
        + [pltpu.VMEM((2, tq, tq), F32), pltpu.SemaphoreType.DMA((2,))],
        compiler_params=_cp(2))(qh, o, lse, dycat, z, kh, vh)


def _out_proj_fwd(yab, yc, w, x, g, target, name, tm=512):
    t, d = x.shape
    tm = min(tm, t)
    is_last = target is not None

    def body(*refs):
        if is_last:
            yab_ref, yc_ref, w_ref, x_ref, g_ref, t_ref, y_ref, dout_ref, loss_ref = refs
            _zero_when(pl.program_id(0) == 0, loss_ref)
        else:
            yab_ref, yc_ref, w_ref, x_ref, g_ref, y_ref, out_ref = refs
        y = _dot(jnp.concatenate([yab_ref[...], yc_ref[...]], axis=1), w_ref[...])
        y_ref[...] = y
        out = x_ref[...] + _rms(y, g_ref[...])[0]
        if is_last:
            diff = out - t_ref[...]
            dout_ref[...] = diff * (1.0 / d)
            part = jnp.sum(jnp.sum(diff * diff, axis=-1, keepdims=True), axis=0, keepdims=True) * (0.5 / d)
            _acc(loss_ref, jnp.broadcast_to(part, (1, 128)))
        else:
            out_ref[...] = out

    row = lambda wd: pl.BlockSpec((tm, wd), lambda i: (i, 0))
    fixed = lambda a, b: pl.BlockSpec((a, b), lambda i: (0, 0))
    in_specs = [row(512), row(512), fixed(d, d), row(d), fixed(1, d)]
    args = [yab, yc, w, x, g]
    out_specs = [row(d), row(d)]
    out_shape = [jax.ShapeDtypeStruct((t, d), F32), jax.ShapeDtypeStruct((t, d), F32)]
    if is_last:
        in_specs.append(row(d))
        args.append(target)
        out_specs.append(fixed(1, 128))
        out_shape.append(jax.ShapeDtypeStruct((1, 128), F32))
    return pl.pallas_call(body, name=name, grid=(t // tm,), in_specs=in_specs, out_specs=out_specs,
                          out_shape=out_shape, compiler_params=_cp())(*args)


def _halves(tm):
    half = tm // 2 if tm >= 512 else tm
    return [pl.ds(s, half) for s in range(0, tm, half)]


def _out_proj_bwd(dout, y, yab, yc, w, g, name, tm=1024):
    t, d = y.shape
    tm = min(tm, t)

    def body(dout_ref, y_ref, yab_ref, yc_ref, w_ref, g_ref, dycat_ref, dw_ref, dg_ref):
        _zero_when(pl.program_id(0) == 0, dw_ref, dg_ref)
        dybs = []
        for rows in _halves(tm):
            y = y_ref[rows, :]
            r = lax.rsqrt(jnp.mean(y * y, axis=-1, keepdims=True) + EPS)
            dy, dgt = _rms_bwd(y, r, g_ref[...], dout_ref[rows, :])
            _acc(dg_ref, _colsum(dgt))
            dybs.append(_mx(dy))
        for rows, dyb in zip(_halves(tm), dybs):
            _acc(dw_ref, _dot_tn(jnp.concatenate([yab_ref[rows, :], yc_ref[rows, :]], axis=1), dyb))
            dycat_ref[rows, :] = _dot_nt(dyb, w_ref[...])

    row = lambda wd: pl.BlockSpec((tm, wd), lambda i: (i, 0))
    fixed = lambda a, b: pl.BlockSpec((a, b), lambda i: (0, 0), pipeline_mode=pl.Buffered(1))
    return pl.pallas_call(
        body, name=name, grid=(t // tm,),
        in_specs=[row(d), row(d), row(512), row(512), fixed(d, d), fixed(1, d)],
        out_specs=[row(d), fixed(d, d), fixed(1, d)],
        out_shape=[jax.ShapeDtypeStruct((t, d), F32), jax.ShapeDtypeStruct((d, d), F32),
                   jax.ShapeDtypeStruct((1, d), F32)],
        compiler_params=_cp())(dout, y, yab, yc, w, g)


def _mesh_pos():
    return lax.axis_index("x"), lax.axis_index("y"), lax.axis_index("c")


def _remote(src, dst, send_sem, recv_sem, to):
    return pltpu.make_async_remote_copy(src_ref=src, dst_ref=dst, send_sem=send_sem, recv_sem=recv_sem,
                                        device_id=to, device_id_type=MESH)


CHUNK_ROWS = 256


def _pieces(rows):
    return [(s, min(CHUNK_ROWS, rows - s)) for s in range(0, rows, CHUNK_ROWS)]


def _piece_table(shapes):
    return [(a, s, sz) for a, shp in enumerate(shapes) for s, sz in _pieces(shp[-2])]


def _gather_weights(shards):
    n = len(shards)
    table = _piece_table([s.shape for s in shards])
    npc = len(table)

    def body(*refs):
        ins, outs = refs[:n], refs[n:2 * n]
        send_sems, recv_sems, fwd_send, fwd_recv = refs[2 * n:]
        x, y, c = _mesh_pos()
        me, sibling = (x, y, c), (x, y, 1 - c)
        chips = [(1 - x, y), (x, 1 - y), (1 - x, 1 - y)]
        slot = lambda cx, cy, layer: 2 * (2 * cx + cy) + layer
        first = []
        for a in range(n):
            for j, (cx, cy) in enumerate(chips):
                first.append(_remote(ins[a].at[c], outs[a].at[slot(x, y, c)], send_sems.at[a, j], recv_sems.at[a, j],
                                     (cx, cy, c)))
                first[-1].start()
        passed = []
        for j, (cx, cy) in enumerate(chips):
            for a in range(n):
                blk = outs[a].at[slot(cx, cy, c)]
                _remote(blk, blk, send_sems.at[a, j], recv_sems.at[a, j], me).wait_recv()
            for q, (a, s, sz) in enumerate(table):
                rows = outs[a].at[slot(cx, cy, c), pl.ds(s, sz)]
                passed.append(_remote(rows, rows, fwd_send.at[j, q], fwd_recv.at[j, q], sibling))
                passed[-1].start()
        for j, (cx, cy) in enumerate(chips):
            for q, (a, s, sz) in enumerate(table):
                rows = outs[a].at[slot(cx, cy, 1 - c), pl.ds(s, sz)]
                _remote(rows, rows, fwd_send.at[j, q], fwd_recv.at[j, q], me).wait_recv()
        for cp in first + passed:
            cp.wait_send()

    return pl.pallas_call(
        body, name="gather_weights", in_specs=[ANY] * n, out_specs=[ANY] * n,
        out_shape=[jax.ShapeDtypeStruct((8,) + s.shape[1:], s.dtype) for s in shards],
        scratch_shapes=[pltpu.SemaphoreType.DMA((n, 3)), pltpu.SemaphoreType.DMA((n, 3)),
                        pltpu.SemaphoreType.DMA((3, npc)), pltpu.SemaphoreType.DMA((3, npc))])(*shards)


def _pair_exchange(parts, common):
    n = len(parts)
    table = _piece_table([p.shape for p in parts] + [common.shape])
    npc = len(table)

    def body(*refs):
        ins, outs = refs[:n + 1], refs[n + 1:2 * n + 2]
        send_sems, recv_sems = refs[2 * n + 2:]
        x, y, c = _mesh_pos()
        sent = []
        for k in range(4):
            for q, (a, s, sz) in enumerate(table):
                if a == n and k > 0:
                    continue
                src = ins[a].at[2 * k + 1 - c, pl.ds(s, sz)] if a < n else ins[a].at[pl.ds(s, sz)]
                dst = outs[a].at[k, pl.ds(s, sz)] if a < n else outs[a].at[pl.ds(s, sz)]
                sent.append(_remote(src, dst, send_sems.at[k, q], recv_sems.at[k, q], (x, y, 1 - c)))
                sent[-1].start()
        for k in range(4):
            for q, (a, s, sz) in enumerate(table):
                if a == n and k > 0:
                    continue
                dst = outs[a].at[k, pl.ds(s, sz)] if a < n else outs[a].at[pl.ds(s, sz)]
                _remote(dst, dst, send_sems.at[k, q], recv_sems.at[k, q], (x, y, c)).wait_recv()
        for cp in sent:
            cp.wait_send()

    return pl.pallas_call(
        body, name="grad_pair_exchange", in_specs=[ANY] * (n + 1), out_specs=[ANY] * (n + 1),
        out_shape=[jax.ShapeDtypeStruct((4,) + p.shape[1:], p.dtype) for p in parts]
        + [jax.ShapeDtypeStruct(common.shape, common.dtype)],
        scratch_shapes=[pltpu.SemaphoreType.DMA((4, npc)), pltpu.SemaphoreType.DMA((4, npc))])(*parts, common)


def _chip_exchange(parts, common):
    n = len(parts)
    table = _piece_table([p.shape for p in parts] + [common.shape])
    npc = len(table)

    def body(*refs):
        ins, outs = refs[:n + 1], refs[n + 1:2 * n + 2]
        send_sems, recv_sems = refs[2 * n + 2:]
        x, y, c = _mesh_pos()
        mine = 2 * x + y
        chips = [(1 - x, y), (x, 1 - y), (1 - x, 1 - y)]
        src = lambda a, k: ins[a].at[k] if a < n else ins[a]
        sent = []
        for j, (cx, cy) in enumerate(chips):
            for q, (a, s, sz) in enumerate(table):
                sent.append(_remote(src(a, 2 * cx + cy).at[pl.ds(s, sz)], outs[a].at[mine, pl.ds(s, sz)],
                                    send_sems.at[j, q], recv_sems.at[j, q], (cx, cy, c)))
                sent[-1].start()
        for j, (cx, cy) in enumerate(chips):
            for q, (a, s, sz) in enumerate(table):
                dst = outs[a].at[2 * cx + cy, pl.ds(s, sz)]
                _remote(dst, dst, send_sems.at[j, q], recv_sems.at[j, q], (x, y, c)).wait_recv()
        for cp in sent:
            cp.wait_send()

    return pl.pallas_call(
        body, name="grad_chip_exchange", in_specs=[ANY] * (n + 1), out_specs=[ANY] * (n + 1),
        out_shape=[jax.ShapeDtypeStruct(p.shape, p.dtype) for p in parts]
        + [jax.ShapeDtypeStruct((4,) + common.shape, common.dtype)],
        scratch_shapes=[pltpu.SemaphoreType.DMA((3, npc)), pltpu.SemaphoreType.DMA((3, npc))])(*parts, common)


def _sibling_exchange(sums):
    n = len(sums)
    table = _piece_table([s.shape for s in sums])
    npc = len(table)

    def body(*refs):
        ins, outs = refs[:n], refs[n:2 * n]
        send_sems, recv_sems = refs[2 * n:]
        x, y, c = _mesh_pos()
        sent = []
        for q, (a, s, sz) in enumerate(table):
            sent.append(_remote(ins[a].at[pl.ds(s, sz)], outs[a].at[pl.ds(s, sz)], send_sems.at[q], recv_sems.at[q],
                                (x, y, 1 - c)))
            sent[-1].start()
        for q, (a, s, sz) in enumerate(table):
            dst = outs[a].at[pl.ds(s, sz)]
            _remote(dst, dst, send_sems.at[q], recv_sems.at[q], (x, y, c)).wait_recv()
        for cp in sent:
            cp.wait_send()

    return pl.pallas_call(
        body, name="sibling_exchange", in_specs=[ANY] * n, out_specs=[ANY] * n,
        out_shape=[jax.ShapeDtypeStruct(s.shape, s.dtype) for s in sums],
        scratch_shapes=[pltpu.SemaphoreType.DMA((npc,)), pltpu.SemaphoreType.DMA((npc,))])(*sums)


def _pair_sum(mine, got, name, tr):
    _, r, c = got.shape
    tr = min(tr, r)

    def body(p_ref, g_ref, o_ref, w_ref):
        total = p_ref[...] + g_ref[...]
        o_ref[...] = total
        w_ref[...] = total.astype(WIRE_DTYPE)

    blk = pl.BlockSpec((None, tr, c), lambda k, i: (k, i, 0))
    return pl.pallas_call(
        body, name=name, grid=(4, r // tr), in_specs=[blk, blk], out_specs=[blk, blk],
        out_shape=[jax.ShapeDtypeStruct(got.shape, F32), jax.ShapeDtypeStruct(got.shape, WIRE_DTYPE)],
        compiler_params=_cp(2))(mine, got)


def _sum_chips(own, recv, name, tr):
    r, c = own.shape
    tr = min(tr, r)

    def body(own_ref, r_ref, o_ref):
        chip = 2 * lax.axis_index("x") + lax.axis_index("y")
        own_blk = own_ref[...]
        acc = jnp.where(chip == 0, own_blk, r_ref[0].astype(F32))
        for k in range(1, 4):
            acc = acc + jnp.where(chip == k, own_blk, r_ref[k].astype(F32))
        o_ref[...] = acc

    return pl.pallas_call(
        body, name=name, grid=(r // tr,),
        in_specs=[pl.BlockSpec((tr, c), lambda i: (i, 0)), pl.BlockSpec((4, tr, c), lambda i: (0, i, 0))],
        out_specs=pl.BlockSpec((tr, c), lambda i: (i, 0)),
        out_shape=jax.ShapeDtypeStruct((r, c), F32), compiler_params=_cp())(own, recv)


def _sum_leading(parts, name, tr):
    nlead, r, c = parts.shape
    tr = min(tr, r)

    def body(p_ref, o_ref):
        acc = p_ref[0]
        for j in range(1, nlead):
            acc = acc + p_ref[j]
        o_ref[...] = acc

    return pl.pallas_call(
        body, name=name, grid=(r // tr,),
        in_specs=[pl.BlockSpec((nlead, tr, c), lambda i: (0, i, 0))], out_specs=pl.BlockSpec((tr, c), lambda i: (i, 0)),
        out_shape=jax.ShapeDtypeStruct((r, c), parts.dtype), compiler_params=_cp())(parts)


def _adamw_update(w_ref, g_ref, m_ref, v_ref, d_ref, nm_ref, nv_ref):
    gg = g_ref[...]
    nm = ADAM_B1 * m_ref[...] + (1.0 - ADAM_B1) * gg
    nv = ADAM_B2 * v_ref[...] + (1.0 - ADAM_B2) * jnp.square(gg)
    m_hat = nm / (1.0 - ADAM_B1 ** ADAM_STEP)
    v_hat = nv / (1.0 - ADAM_B2 ** ADAM_STEP)
    d_ref[...] = -ADAM_LR * (m_hat / (jnp.sqrt(v_hat) + ADAM_EPS) + ADAM_WD * w_ref[...])
    nm_ref[...] = nm
    nv_ref[...] = nv


def _adamw(w, g, m, v, name, tr):
    r, c = w.shape
    tr = min(tr, r)

    def body(w_ref, g_ref, m_ref, v_ref, d_ref, nm_ref, nv_ref):
        _adamw_update(w_ref, g_ref, m_ref, v_ref, d_ref, nm_ref, nv_ref)

    blk = pl.BlockSpec((tr, c), lambda i: (i, 0))
    return pl.pallas_call(
        body, name=name, grid=(r // tr,), in_specs=[blk] * 4, out_specs=[blk] * 3,
        out_shape=[jax.ShapeDtypeStruct((r, c), F32)] * 3, compiler_params=_cp())(w, g, m, v)


def _adamw_many(ws, gs, ms, vs, name):
    n = len(ws)

    def body(*refs):
        ins, outs = refs[:4 * n], refs[4 * n:]
        for k in range(n):
            _adamw_update(ins[k], ins[n + k], ins[2 * n + k], ins[3 * n + k], outs[k], outs[n + k], outs[2 * n + k])

    vmem = pl.BlockSpec(memory_space=pltpu.VMEM)
    res = pl.pallas_call(
        body, name=name, in_specs=[vmem] * (4 * n), out_specs=[vmem] * (3 * n),
        out_shape=[jax.ShapeDtypeStruct(a.shape, F32) for a in ws] * 3)(*ws, *gs, *ms, *vs)
    return res[:n], res[n:2 * n], res[2 * n:]


def _rope_tables(positions):
    inv_freq = ROPE_BASE ** (-jnp.arange(0, 64, 2, dtype=F32) / 64)
    ang = positions.astype(F32)[:, None] * inv_freq
    cos, sin = jnp.cos(ang), jnp.sin(ang)
    t = positions.shape[0]
    rc = jnp.concatenate([jnp.ones((t, 128), F32), cos, cos, jnp.ones((t, 64), F32)], axis=1)
    rs = jnp.concatenate([jnp.zeros((t, 128), F32), sin, sin, jnp.zeros((t, 64), F32)], axis=1)
    return rc, rs


def _layer_params(l, w_in, w_uq, w_ukv, w_out, small):
    p = {}
    p["w_in"] = jnp.concatenate([w_in[l][:, :1984], jnp.zeros((1024, 64), w_in.dtype), w_in[l][:, 1984:]], axis=1)
    p["w_uq"] = jnp.pad(w_uq[l].reshape(384, 4, 192), ((0, 0), (0, 0), (0, 64))).reshape(384, 1024)
    p["w_ukv"] = w_ukv[l].reshape(256, 4, 2, 128).transpose(0, 2, 1, 3).reshape(256, 1024)
    p["w_out"] = w_out[l]
    p["pre_g"] = small["pre_norm_g"][l][None]
    p["post_g"] = small["post_norm_g"][l][None]
    p["sgu_w"] = small["sgu_w"][l].reshape(512, 128)
    p["sgu_wt"] = small["sgu_w"][l].transpose(0, 2, 1).reshape(512, 128)
    p["sgu_bias"] = jnp.repeat(small["sgu_b"][l].T, 64, axis=1)
    p["ln_g"] = small["sgu_ln_g"][l][None]
    p["ln_b"] = small["sgu_ln_b"][l][None]
    p["pool_wbd"] = _mx(jax.scipy.linalg.block_diag(*[small["pool_w"][l][gi] for gi in range(4)]))
    p["pool_scale"] = small["pool_scale"][l][None]
    p["gq"] = small["q_norm_g"][l][None]
    p["gkv"] = small["kv_norm_g"][l][None]
    return p


def _layer_fwd(l, x, p, rc, rs, target):
    z, h = _in_proj_fwd(x, p["pre_g"], p["w_in"], f"in_proj_fwd_{l}")
    yab = _mix_fwd(z, p["sgu_w"], p["sgu_bias"], p["ln_g"], p["ln_b"], p["pool_wbd"], p["pool_scale"], f"mix_fwd_{l}")
    qh, kh, vh = _qkv_fwd(z, rc, rs, p["w_uq"], p["w_ukv"], p["gq"], p["gkv"], f"qkv_fwd_{l}")
    o, yc, lse = _attn_fwd(qh, kh, vh, z, f"attn_fwd_{l}")
    outs = _out_proj_fwd(yab, yc, p["w_out"], x, p["post_g"], target, f"out_proj_fwd_{l}")
    saved = dict(x=x, z=z, h=h, yab=yab, qh=qh, kh=kh, vh=vh, o=o, yc=yc, lse=lse, y=outs[0])
    return saved, outs[1:]


def _layer_bwd(l, dout, sv, p, rc, rs):
    dycat, dw_out, dpost = _out_proj_bwd(dout, sv["y"], sv["yab"], sv["yc"], p["w_out"], p["post_g"], f"out_proj_bwd_{l}")
    dq, dgate, dk, dv = _attn_bwd(sv["qh"], sv["kh"], sv["vh"], sv["o"], sv["lse"], dycat, sv["z"], f"attn_bwd_{l}")
    dzc, dzk, dwq, dwkv, dgq, dgkv = _qkv_bwd(dq, dk, dv, sv["z"], rc, rs, p["w_uq"], p["w_ukv"], p["gq"], p["gkv"],
                                              f"qkv_bwd_{l}")
    dzm, dsw, dsb, dlng, dlnb, dpw, dps = _mix_bwd(sv["z"], dycat, p["sgu_w"], p["sgu_wt"], p["sgu_bias"], p["ln_g"],
                                                   p["ln_b"], p["pool_wbd"], p["pool_scale"], f"mix_bwd_{l}")
    dx, dw_in, dpre = _in_proj_bwd(dzm, dzc, dzk, dgate, sv["h"], sv["x"], dout, p["w_in"], p["pre_g"], f"in_proj_bwd_{l}")
    grads = {
        "pre_norm_g": dpre[0], "post_norm_g": dpost[0],
        "w_in": jnp.concatenate([dw_in[:, :1984], dw_in[:, 2048:]], axis=1),
        "sgu_w": dsw.reshape(4, 128, 128), "sgu_b": dsb[:, :4].T, "sgu_ln_g": dlng[0], "sgu_ln_b": dlnb[0],
        "pool_w": jnp.stack([dpw[64 * gi:64 * gi + 64, 64 * gi:64 * gi + 64] for gi in range(4)]),
        "pool_scale": dps[0], "q_norm_g": dgq[0],
        "w_uq": dwq.reshape(384, 4, 256)[:, :, :192].reshape(384, 768), "kv_norm_g": dgkv[0],
        "w_ukv": dwkv.reshape(256, 2, 4, 128).transpose(0, 2, 1, 3).reshape(256, 1024), "w_out": dw_out,
    }
    return dx, grads


SMALL_NAMES = ["pre_norm_g", "post_norm_g", "sgu_w", "sgu_b", "sgu_ln_g", "sgu_ln_b", "pool_w", "pool_scale",
               "q_norm_g", "kv_norm_g"]
BIG_NAMES = ["w_in", "w_uq", "w_ukv", "w_out"]
WEIGHT_NAMES = ["pre_norm_g", "post_norm_g", "w_in", "sgu_w", "sgu_b", "sgu_ln_g", "sgu_ln_b", "pool_w", "pool_scale",
                "q_norm_g", "w_uq", "kv_norm_g", "w_ukv", "w_out"]


def _local_step(x, positions, target, w_in, w_uq, w_ukv, w_out, small):
    rc, rs = _rope_tables(positions)
    params = [_layer_params(l, w_in, w_uq, w_ukv, w_out, small) for l in range(DEPTH)]
    saved = []
    for l in range(DEPTH):
        sv, outs = _layer_fwd(l, x, params[l], rc, rs, target if l == DEPTH - 1 else None)
        saved.append(sv)
        if l < DEPTH - 1:
            x = outs[0]
    dout, loss = outs
    grads = [None] * DEPTH
    for l in reversed(range(DEPTH)):
        dout, grads[l] = _layer_bwd(l, dout, saved[l], params[l], rc, rs)
    return loss[0, 0], dout, {k: jnp.stack([grads[l][k] for l in range(DEPTH)]) for k in WEIGHT_NAMES}


def _pack_small(tree, extra=None):
    pieces = [tree[k].reshape(-1) for k in SMALL_NAMES]
    pieces.append(jnp.zeros((1,), F32) if extra is None else extra.reshape(1))
    flat = jnp.concatenate(pieces)
    rows = -(-flat.shape[0] // 1024) * 8
    return jnp.pad(flat, (0, rows * 128 - flat.shape[0])).reshape(rows, 128)


def _unpack_small(packed, like):
    flat = packed.reshape(-1)
    out, off = {}, 0
    for k in SMALL_NAMES:
        size = like[k].size
        out[k] = flat[off:off + size].reshape(like[k].shape)
        off += size
    return out, flat[off]


def kernel(x, positions, pre_norm_g, post_norm_g, w_in, sgu_w, sgu_b, sgu_ln_g, sgu_ln_b, pool_w, pool_scale, q_norm_g, w_uq, kv_norm_g, w_ukv, w_out, loss_target, m_pre_norm_g, m_post_norm_g, m_w_in, m_sgu_w, m_sgu_b, m_sgu_ln_g, m_sgu_ln_b, m_pool_w, m_pool_scale, m_q_norm_g, m_w_uq, m_kv_norm_g, m_w_ukv, m_w_out, v_pre_norm_g, v_post_norm_g, v_w_in, v_sgu_w, v_sgu_b, v_sgu_ln_g, v_sgu_ln_b, v_pool_w, v_pool_scale, v_q_norm_g, v_w_uq, v_kv_norm_g, v_w_ukv, v_w_out):
    w = dict(pre_norm_g=pre_norm_g, post_norm_g=post_norm_g, w_in=w_in, sgu_w=sgu_w, sgu_b=sgu_b, sgu_ln_g=sgu_ln_g,
             sgu_ln_b=sgu_ln_b, pool_w=pool_w, pool_scale=pool_scale, q_norm_g=q_norm_g, w_uq=w_uq, kv_norm_g=kv_norm_g,
             w_ukv=w_ukv, w_out=w_out)
    m = dict(pre_norm_g=m_pre_norm_g, post_norm_g=m_post_norm_g, w_in=m_w_in, sgu_w=m_sgu_w, sgu_b=m_sgu_b,
             sgu_ln_g=m_sgu_ln_g, sgu_ln_b=m_sgu_ln_b, pool_w=m_pool_w, pool_scale=m_pool_scale, q_norm_g=m_q_norm_g,
             w_uq=m_w_uq, kv_norm_g=m_kv_norm_g, w_ukv=m_w_ukv, w_out=m_w_out)
    v = dict(pre_norm_g=v_pre_norm_g, post_norm_g=v_post_norm_g, w_in=v_w_in, sgu_w=v_sgu_w, sgu_b=v_sgu_b,
             sgu_ln_g=v_sgu_ln_g, sgu_ln_b=v_sgu_ln_b, pool_w=v_pool_w, pool_scale=v_pool_scale, q_norm_g=v_q_norm_g,
             w_uq=v_w_uq, kv_norm_g=v_kv_norm_g, w_ukv=v_w_ukv, w_out=v_w_out)

    core = lax.axis_index("c")
    chip = 2 * lax.axis_index("x") + lax.axis_index("y")
    shards = [_mx(w[k]) for k in BIG_NAMES]
    gathered = _gather_weights(shards)
    g_in, g_uq, g_ukv, g_out = [lax.dynamic_update_slice(g, s, (2 * chip, 0, 0)) for g, s in zip(gathered, shards)]
    cols = lambda g: g.reshape((4, 2) + g.shape[1:]).transpose(1, 2, 0, 3).reshape(2, g.shape[1], 4 * g.shape[2])
    full_out = g_out.reshape(4, 2, 256, 1024).transpose(1, 0, 2, 3).reshape(2, 1024, 1024)
    loss, dx, grads = _local_step(x[0], positions[0], loss_target[0], cols(g_in), cols(g_uq), cols(g_ukv), full_out, w)

    split_cols = lambda g: g.reshape(2, g.shape[1], 4, g.shape[2] // 4).transpose(2, 0, 1, 3).reshape(8, g.shape[1], g.shape[2] // 4)
    parts = [split_cols(grads["w_in"]), split_cols(grads["w_uq"]), split_cols(grads["w_ukv"]),
             grads["w_out"].reshape(2, 4, 256, 1024).transpose(1, 0, 2, 3).reshape(8, 256, 1024)]
    common = _pack_small(grads, loss)
    got = _pair_exchange(parts, common)
    mine = [lax.dynamic_index_in_dim(p.reshape((4, 2) + p.shape[1:]), core, axis=1, keepdims=False) for p in parts]
    pair_sums = [_pair_sum(mine[a], got[a], f"pair_sum_{BIG_NAMES[a]}", 128) for a in range(4)]
    chip_common = _sum_leading(jnp.stack([common, got[4]]), "pair_sum_small", common.shape[0])
    received = _chip_exchange([ps[1] for ps in pair_sums], chip_common)
    sums = [_sum_chips(lax.dynamic_index_in_dim(pair_sums[a][0], chip, axis=0, keepdims=False), received[a],
                       f"sum_{BIG_NAMES[a]}", 128) for a in range(4)]
    all_common = lax.dynamic_update_slice(received[4], chip_common[None], (chip, 0, 0))
    small_sum, loss = _unpack_small(_sum_leading(all_common, "sum_small", all_common.shape[1]), w)
    others = _sibling_exchange(sums)
    total = dict(small_sum)
    for a, k in enumerate(BIG_NAMES):
        total[k] = jnp.where(core == 0, jnp.stack([sums[a], others[a]]), jnp.stack([others[a], sums[a]]))

    rows2d = lambda a: a.reshape(-1, a.shape[-1])
    small_out = _adamw_many(*[[rows2d(tree[k]) for k in SMALL_NAMES] for tree in (w, total, m, v)], "adamw_small")
    delta, new_m, new_v = ({k: r.reshape(w[k].shape) for k, r in zip(SMALL_NAMES, res)} for res in small_out)
    for k in BIG_NAMES:
        shape = w[k].shape
        flat = lambda a: a.reshape(shape[0] * shape[1], shape[2])
        res = _adamw(flat(w[k]), flat(total[k]), flat(m[k]), flat(v[k]), f"adamw_{k}", 256)
        delta[k], new_m[k], new_v[k] = (r.reshape(shape) for r in res)

    return (loss, dx[None], *[total[k] for k in WEIGHT_NAMES], *[delta[k] for k in WEIGHT_NAMES],
            *[new_m[k] for k in WEIGHT_NAMES], *[new_v[k] for k in WEIGHT_NAMES])
```

```python
import jax
import jax.numpy as jnp
from jax import lax
from jax.experimental import pallas as pl
from jax.experimental.pallas import tpu as pltpu

F32 = jnp.float32
MXU_DTYPE = jnp.bfloat16
WIRE_DTYPE = jnp.bfloat16
EPS = 1e-6
NEG_INF = -1e30
DEPTH = 2
N_HEADS = 4
QK_PAD = 256
V_DIM = 128
SCALE = 192 ** -0.5
LOG2E = 1.4426950408889634
ROPE_BASE = 10000.0
ADAM_LR, ADAM_B1, ADAM_B2, ADAM_EPS, ADAM_WD, ADAM_STEP = 0.001, 0.9, 0.999, 1e-08, 0.01, 10
VMEM_LIMIT_BYTES = 56 * 1024 * 1024
MESH = pl.DeviceIdType.MESH
ANY = pl.BlockSpec(memory_space=pl.ANY)

Z_MIX, Z_C, Z_KR, Z_GATE = 1280, 640, 128, 512
Z_W = Z_MIX + Z_C + Z_KR + Z_GATE


def _cp(n_axes=1):
    return pltpu.CompilerParams(dimension_semantics=("arbitrary",) * n_axes, vmem_limit_bytes=VMEM_LIMIT_BYTES)


def _dot(a, b):
    return lax.dot_general(a, b, (((1,), (0,)), ((), ())), preferred_element_type=F32)


def _dot_nt(a, b):
    return lax.dot_general(a, b, (((1,), (1,)), ((), ())), preferred_element_type=F32)


def _dot_tn(a, b):
    return lax.dot_general(a, b, (((0,), (0,)), ((), ())), preferred_element_type=F32)


def _mx(a):
    return a.astype(MXU_DTYPE)


def _silu_and_grad(g):
    sg = jax.nn.sigmoid(g)
    return g * sg, sg * (1.0 + g * (1.0 - sg))


def _rms(x, g):
    r = lax.rsqrt(jnp.mean(x * x, axis=-1, keepdims=True) + EPS)
    return x * r * g, r


def _rms_bwd(x, r, g, dy):
    xhat = x * r
    dyg = dy * g
    dx = r * (dyg - xhat * jnp.mean(dyg * xhat, axis=-1, keepdims=True))
    return dx, dy * xhat


def _zero_when(first, *refs):
    @pl.when(first)
    def _():
        for ref in refs:
            ref[...] = jnp.zeros(ref.shape, ref.dtype)


def _acc(ref, val):
    ref[...] += val


def _colsum(a):
    return jnp.sum(a, axis=0, keepdims=True)


def _in_proj_fwd(x, g, w, name, tm=512):
    t, d = x.shape
    n = w.shape[1]
    tm = min(tm, t)

    def body(x_ref, g_ref, w_ref, z_ref, h_ref):
        h, _ = _rms(x_ref[...], g_ref[...])
        h = _mx(h)
        h_ref[...] = h
        z_ref[...] = _dot(h, w_ref[...])

    return pl.pallas_call(
        body, name=name, grid=(t // tm,),
        in_specs=[pl.BlockSpec((tm, d), lambda i: (i, 0)), pl.BlockSpec((1, d), lambda i: (0, 0)),
                  pl.BlockSpec((d, n), lambda i: (0, 0))],
        out_specs=[pl.BlockSpec((tm, n), lambda i: (i, 0)), pl.BlockSpec((tm, d), lambda i: (i, 0))],
        out_shape=[jax.ShapeDtypeStruct((t, n), F32), jax.ShapeDtypeStruct((t, d), MXU_DTYPE)],
        compiler_params=_cp())(x, g, w)


def _in_proj_bwd(dz_mix, dz_c, dz_kr, dz_gate, h, x, d_res, w, g, name, tm=512):
    t, d = x.shape
    n = w.shape[1]
    tm = min(tm, t)

    def body(dm_ref, dc_ref, dk_ref, dg_ref, h_ref, x_ref, dres_ref, w_ref, g_ref, dx_ref, dw_ref, dgn_ref):
        first = pl.program_id(0) == 0
        dz = jnp.concatenate([dm_ref[...], dc_ref[...], dk_ref[...], dg_ref[...]], axis=1)

        _zero_when(first, dw_ref, dgn_ref)
        hb = h_ref[...]
        for c0 in range(0, n, 512):
            dw_ref[:, c0:c0 + 512] += _dot_tn(hb, dz[:, c0:c0 + 512])
        dh = _dot_nt(dz, w_ref[...])
        xf = x_ref[...]
        r = lax.rsqrt(jnp.mean(xf * xf, axis=-1, keepdims=True) + EPS)
        dx, dgt = _rms_bwd(xf, r, g_ref[...], dh)
        dx_ref[...] = dx + dres_ref[...]
        _acc(dgn_ref, _colsum(dgt))

    row = lambda wd: pl.BlockSpec((tm, wd), lambda i: (i, 0))
    fixed = lambda a, b: pl.BlockSpec((a, b), lambda i: (0, 0), pipeline_mode=pl.Buffered(1))
    return pl.pallas_call(
        body, name=name, grid=(t // tm,),
        in_specs=[row(Z_MIX), row(Z_C), row(Z_KR), row(Z_GATE), row(d), row(d), row(d), fixed(d, n), fixed(1, d)],
        out_specs=[row(d), fixed(d, n), fixed(1, d)],
        out_shape=[jax.ShapeDtypeStruct((t, d), F32), jax.ShapeDtypeStruct((d, n), F32),
                   jax.ShapeDtypeStruct((1, d), F32)],
        compiler_params=_cp())(dz_mix, dz_c, dz_kr, dz_gate, h, x, d_res, w, g)


def _lane_group(shape):
    return lax.broadcasted_iota(jnp.int32, shape, 1) // 64


def _select_group(vals):
    grp = _lane_group(vals[0].shape)
    out = vals[3]
    for gi in (2, 1, 0):
        out = jnp.where(grp == gi, vals[gi], out)
    return out


def _sgu_mask(transposed):
    r = (lax.broadcasted_iota(jnp.int32, (512, 128), 0) % 128) // 64
    c = lax.broadcasted_iota(jnp.int32, (512, 128), 1) // 64
    return (r <= c) if transposed else (c <= r)


def _sgu_apply(wstack, vb, nblk):
    outs = []
    for n in range(nblk):
        r = _dot(wstack, vb[n * 128:(n + 1) * 128, :])
        outs.append(_select_group([r[hh * 128:(hh + 1) * 128, :] for hh in range(4)]))
    return jnp.concatenate(outs, axis=0)


def _layer_norm(v, g, b):
    mu = jnp.mean(v, axis=-1, keepdims=True)
    vc = v - mu
    rstd = lax.rsqrt(jnp.mean(vc * vc, axis=-1, keepdims=True) + EPS)
    vhat = vc * rstd
    return vhat * g + b, vhat, rstd


def _pool_counts(t0, n):
    t = t0 + lax.broadcasted_iota(jnp.int32, (n, 256), 0)
    w = _select_group([jnp.full((n, 256), wv, jnp.int32) for wv in (2, 4, 8, 16)])
    return jnp.minimum(t + 1, w).astype(F32)


def _pooled(p, halo, t0):
    tm = p.shape[0]
    ext = jnp.concatenate([halo, p], axis=0)
    s2 = ext + pltpu.roll(ext, 1, 0)
    s4 = s2 + pltpu.roll(s2, 2, 0)
    s8 = s4 + pltpu.roll(s4, 4, 0)
    s16 = s8 + pltpu.roll(s8, 8, 0)
    sel = _select_group([s2, s4, s8, s16])[16:, :]
    return sel / _pool_counts(t0, tm) - p


def _pooled_bwd(dpool, dpool_halo, t0):
    tm = dpool.shape[0]
    n = tm + 16
    ext = jnp.concatenate([dpool, dpool_halo], axis=0) / _pool_counts(t0, n)
    f2 = ext + pltpu.roll(ext, n - 1, 0)
    f4 = f2 + pltpu.roll(f2, n - 2, 0)
    f8 = f4 + pltpu.roll(f4, n - 4, 0)
    f16 = f8 + pltpu.roll(f8, n - 8, 0)
    return _select_group([f2, f4, f8, f16])[:tm, :] - dpool


def _mix_specs(t, tm):
    nt16 = t // 16
    zrow = pl.BlockSpec((tm, Z_MIX), lambda i: (i, 0))
    prev_halo = pl.BlockSpec((16, 256), lambda i: (jnp.maximum(i * (tm // 16) - 1, 0), 3))
    fixed = lambda a, b: pl.BlockSpec((a, b), lambda i: (0, 0))
    params = [fixed(512, 128), fixed(128, 256), fixed(1, 256), fixed(1, 256), fixed(256, 256), fixed(1, 256)]
    return nt16, zrow, prev_halo, fixed, params


def _mix_fwd(z, sgu_w, sgu_bias, ln_g, ln_b, pool_wbd, pool_scale, name, tm=512):
    t = z.shape[0]
    tm = min(tm, t)
    _, zrow, prev_halo, _, params = _mix_specs(t, tm)

    def body(z_ref, halo_ref, w_ref, bias_ref, lng_ref, lnb_ref, pw_ref, ps_ref, y_ref):
        i = pl.program_id(0)
        u, v, gate = z_ref[:, 0:256], z_ref[:, 256:512], z_ref[:, 512:768]
        p, pgate = z_ref[:, 768:1024], z_ref[:, 1024:1280]
        vn, _, _ = _layer_norm(v, lng_ref[...], lnb_ref[...])
        wm = _mx(jnp.where(_sgu_mask(False), w_ref[...], 0.0))
        mixed = _sgu_apply(wm, _mx(vn), tm // 128) + jnp.tile(bias_ref[...], (tm // 128, 1))
        ya = u * mixed * _silu_and_grad(gate)[0]
        halo = jnp.where(i > 0, halo_ref[...], 0.0)
        pooled = _pooled(p, halo, i * tm)
        yb = _dot(_mx(pooled), pw_ref[...]) * ps_ref[...] * _silu_and_grad(pgate)[0]
        y_ref[...] = _mx(jnp.concatenate([ya, yb], axis=1))

    return pl.pallas_call(
        body, name=name, grid=(t // tm,),
        in_specs=[zrow, prev_halo] + params,
        out_specs=pl.BlockSpec((tm, 512), lambda i: (i, 0)),
        out_shape=jax.ShapeDtypeStruct((t, 512), MXU_DTYPE),
        compiler_params=_cp())(z, z, sgu_w, sgu_bias, ln_g, ln_b, pool_wbd, pool_scale)


def _mix_bwd(z, dycat, sgu_w, sgu_wt, sgu_bias, ln_g, ln_b, pool_wbd, pool_scale, name, tm=512):
    t = z.shape[0]
    tm = min(tm, t)
    nt16, zrow, prev_halo, fixed, params = _mix_specs(t, tm)
    nblk = tm // 128
    last = t // tm - 1

    def body(z_ref, halo_ref, zn_ref, dy_ref, dyn_ref, w_ref, wt_ref, bias_ref, lng_ref, lnb_ref, pw_ref, ps_ref,
             dz_ref, dw_ref, db_ref, dlng_ref, dlnb_ref, dpw_ref, dps_ref):
        i = pl.program_id(0)
        _zero_when(i == 0, dw_ref, db_ref, dlng_ref, dlnb_ref, dpw_ref, dps_ref)
        u, v, gate = z_ref[:, 0:256], z_ref[:, 256:512], z_ref[:, 512:768]
        p, pgate = z_ref[:, 768:1024], z_ref[:, 1024:1280]
        dya, dyb = dy_ref[:, 0:256], dy_ref[:, 256:512]
        vn, vhat, rstd = _layer_norm(v, lng_ref[...], lnb_ref[...])
        vnb = _mx(vn)
        wm = _mx(jnp.where(_sgu_mask(False), w_ref[...], 0.0))
        wmt = _mx(jnp.where(_sgu_mask(True), wt_ref[...], 0.0))
        mixed = _sgu_apply(wm, vnb, nblk) + jnp.tile(bias_ref[...], (nblk, 1))
        silu, dsilu = _silu_and_grad(gate)
        t1 = u * mixed
        d_gate = dya * t1 * dsilu
        d_t1 = dya * silu
        d_u = d_t1 * mixed
        d_mixed = d_t1 * u
        dmb = _mx(d_mixed)
        d_vn = _sgu_apply(wmt, dmb, nblk)
        grp = _lane_group((128, 256))
        lane = lax.broadcasted_iota(jnp.int32, (128, 128), 1)
        dws = [jnp.zeros((128, 128), F32) for _ in range(4)]
        dbias = jnp.zeros((128, 128), F32)
        for n in range(nblk):
            dm_n, dmb_n, vnb_n = d_mixed[n * 128:(n + 1) * 128], dmb[n * 128:(n + 1) * 128], vnb[n * 128:(n + 1) * 128]
            for hh in range(4):
                dws[hh] = dws[hh] + _dot_nt(jnp.where(grp == hh, dmb_n, jnp.zeros_like(dmb_n)), vnb_n)
                rs = jnp.sum(jnp.where(grp == hh, dm_n, 0.0), axis=-1, keepdims=True)
                dbias = dbias + jnp.where(lane == hh, rs, 0.0)
        _acc(dw_ref, jnp.concatenate(dws, axis=0))
        _acc(db_ref, dbias)
        _acc(dlng_ref, _colsum(d_vn * vhat))
        _acc(dlnb_ref, _colsum(d_vn))
        dvh = d_vn * lng_ref[...]
        d_v = rstd * (dvh - jnp.mean(dvh, axis=-1, keepdims=True) - vhat * jnp.mean(dvh * vhat, axis=-1, keepdims=True))

        @pl.when(i == last)
        def _():
            dw_ref[...] = jnp.where(_sgu_mask(False), dw_ref[...], 0.0)

        halo = jnp.where(i > 0, halo_ref[...], 0.0)
        pooled = _pooled(p, halo, i * tm)
        pooled_b = _mx(pooled)
        mixedp = _dot(pooled_b, pw_ref[...])
        psilu, pdsilu = _silu_and_grad(pgate)
        d_pgate = dyb * (mixedp * ps_ref[...]) * pdsilu
        d_ms = dyb * psilu
        _acc(dps_ref, _colsum(d_ms * mixedp))
        dmpb = _mx(d_ms * ps_ref[...])
        _acc(dpw_ref, _dot_tn(pooled_b, dmpb))
        d_pooled = _dot_nt(dmpb, pw_ref[...])
        dmp_halo = _mx(dyn_ref[...] * _silu_and_grad(zn_ref[...])[0] * ps_ref[...])
        d_pooled_halo = jnp.where(i < last, _dot_nt(dmp_halo, pw_ref[...]), 0.0)
        d_p = _pooled_bwd(d_pooled, d_pooled_halo, i * tm)
        dz_ref[...] = _mx(jnp.concatenate([d_u, d_v, d_gate, d_p, d_pgate], axis=1))

    nxt = lambda i: jnp.minimum((i + 1) * (tm // 16), nt16 - 1)
    return pl.pallas_call(
        body, name=name, grid=(t // tm,),
        in_specs=[zrow, prev_halo, pl.BlockSpec((16, 256), lambda i: (nxt(i), 4)),
                  pl.BlockSpec((tm, 512), lambda i: (i, 0)), pl.BlockSpec((16, 256), lambda i: (nxt(i), 1)),
                  params[0], fixed(512, 128)] + params[1:],
        out_specs=[pl.BlockSpec((tm, Z_MIX), lambda i: (i, 0)), fixed(512, 128), fixed(128, 128), fixed(1, 256),
                   fixed(1, 256), fixed(256, 256), fixed(1, 256)],
        out_shape=[jax.ShapeDtypeStruct((t, Z_MIX), MXU_DTYPE), jax.ShapeDtypeStruct((512, 128), F32),
                   jax.ShapeDtypeStruct((128, 128), F32), jax.ShapeDtypeStruct((1, 256), F32),
                   jax.ShapeDtypeStruct((1, 256), F32), jax.ShapeDtypeStruct((256, 256), F32),
                   jax.ShapeDtypeStruct((1, 256), F32)],
        compiler_params=_cp())(z, z, z, dycat, dycat, sgu_w, sgu_wt, sgu_bias, ln_g, ln_b, pool_wbd, pool_scale)


def _rot_half(x, transpose):
    w = x.shape[1]
    lane = lax.broadcasted_iota(jnp.int32, x.shape, 1) % min(w, 256)
    base = 128 if w >= 256 else 0
    lo = jnp.logical_and(lane >= base, lane < base + 32)
    hi = jnp.logical_and(lane >= base + 32, lane < base + 64)
    up = pltpu.roll(x, w - 32, 1)
    down = pltpu.roll(x, 32, 1)
    if transpose:
        return jnp.where(lo, up, jnp.where(hi, -down, 0.0))
    return jnp.where(lo, -up, jnp.where(hi, down, 0.0))


def _rope(x, c, s):
    return x * c + _rot_half(x, False) * s


def _rope_bwd(dy, c, s):
    return dy * c + _rot_half(dy * s, True)


def _qkv_fwd(z, rc, rs, w_uq, w_ukv, gq, gkv, name, tm=512):
    t = z.shape[0]
    tm = min(tm, t)

    def body(zc_ref, zk_ref, rc_ref, rs_ref, wq_ref, wkv_ref, gq_ref, gkv_ref, q_ref, k_ref, v_ref):
        cq, ckv = zc_ref[:, 0:384], zc_ref[:, 384:640]
        c, s = rc_ref[...], rs_ref[...]
        qn, _ = _rms(cq, gq_ref[...])
        q = _rope(_dot(_mx(qn), wq_ref[...]), jnp.tile(c, (1, N_HEADS)), jnp.tile(s, (1, N_HEADS)))
        kvn, _ = _rms(ckv, gkv_ref[...])
        kv = _dot(_mx(kvn), wkv_ref[...])
        kpe = _rope(zk_ref[...], c[:, 128:256], s[:, 128:256])
        for hh in range(N_HEADS):
            q_ref[hh] = _mx(q[:, hh * QK_PAD:(hh + 1) * QK_PAD])
            k_ref[hh] = _mx(jnp.concatenate([kv[:, hh * 128:(hh + 1) * 128], kpe], axis=1))
            v_ref[hh] = _mx(kv[:, 512 + hh * 128:512 + (hh + 1) * 128])

    fixed = lambda a, b: pl.BlockSpec((a, b), lambda i: (0, 0))
    heads = lambda wd: pl.BlockSpec((N_HEADS, tm, wd), lambda i: (0, i, 0))
    return pl.pallas_call(
        body, name=name, grid=(t // tm,),
        in_specs=[pl.BlockSpec((tm, Z_C), lambda i: (i, Z_MIX // Z_C)),
                  pl.BlockSpec((tm, Z_KR), lambda i: (i, (Z_MIX + Z_C) // Z_KR)),
                  pl.BlockSpec((tm, 256), lambda i: (i, 0)), pl.BlockSpec((tm, 256), lambda i: (i, 0)),
                  fixed(384, 1024), fixed(256, 1024), fixed(1, 384), fixed(1, 256)],
        out_specs=[heads(QK_PAD), heads(QK_PAD), heads(V_DIM)],
        out_shape=[jax.ShapeDtypeStruct((N_HEADS, t, QK_PAD), MXU_DTYPE),
                   jax.ShapeDtypeStruct((N_HEADS, t, QK_PAD), MXU_DTYPE),
                   jax.ShapeDtypeStruct((N_HEADS, t, V_DIM), MXU_DTYPE)],
        compiler_params=_cp())(z, z, rc, rs, w_uq, w_ukv, gq, gkv)


def _qkv_bwd(dq, dk, dv, z, rc, rs, w_uq, w_ukv, gq, gkv, name, tm=512):
    t = z.shape[0]
    tm = min(tm, t)

    def body(dq_ref, dk_ref, dv_ref, zc_ref, rc_ref, rs_ref, wq_ref, wkv_ref, gq_ref, gkv_ref,
             dzc_ref, dzk_ref, dwq_ref, dwkv_ref, dgq_ref, dgkv_ref):
        _zero_when(pl.program_id(0) == 0, dwq_ref, dwkv_ref, dgq_ref, dgkv_ref)
        cq, ckv = zc_ref[:, 0:384], zc_ref[:, 384:640]
        c, s = rc_ref[...], rs_ref[...]
        dq_all = jnp.concatenate([dq_ref[hh] for hh in range(N_HEADS)], axis=1)
        dqp = _mx(_rope_bwd(dq_all, jnp.tile(c, (1, N_HEADS)), jnp.tile(s, (1, N_HEADS))))
        qn, rq = _rms(cq, gq_ref[...])
        _acc(dwq_ref, _dot_tn(_mx(qn), dqp))
        d_qn = _dot_nt(dqp, wq_ref[...])
        dkpe = dk_ref[0][:, 128:256]
        for hh in range(1, N_HEADS):
            dkpe = dkpe + dk_ref[hh][:, 128:256]
        dzk_ref[...] = _mx(_rope_bwd(dkpe, c[:, 128:256], s[:, 128:256]))
        dkv = _mx(jnp.concatenate([dk_ref[hh][:, 0:128] for hh in range(N_HEADS)]
                                  + [dv_ref[hh] for hh in range(N_HEADS)], axis=1))
        kvn, rkv = _rms(ckv, gkv_ref[...])
        _acc(dwkv_ref, _dot_tn(_mx(kvn), dkv))
        d_kvn = _dot_nt(dkv, wkv_ref[...])
        d_cq, dgq_t = _rms_bwd(cq, rq, gq_ref[...], d_qn)
        _acc(dgq_ref, _colsum(dgq_t))
        d_ckv, dgkv_t = _rms_bwd(ckv, rkv, gkv_ref[...], d_kvn)
        _acc(dgkv_ref, _colsum(dgkv_t))
        dzc_ref[...] = _mx(jnp.concatenate([d_cq, d_ckv], axis=1))

    fixed = lambda a, b: pl.BlockSpec((a, b), lambda i: (0, 0))
    heads = lambda wd: pl.BlockSpec((N_HEADS, tm, wd), lambda i: (0, i, 0))
    return pl.pallas_call(
        body, name=name, grid=(t // tm,),
        in_specs=[heads(QK_PAD), heads(QK_PAD), heads(V_DIM), pl.BlockSpec((tm, Z_C), lambda i: (i, Z_MIX // Z_C)),
                  pl.BlockSpec((tm, 256), lambda i: (i, 0)), pl.BlockSpec((tm, 256), lambda i: (i, 0)),
                  fixed(384, 1024), fixed(256, 1024), fixed(1, 384), fixed(1, 256)],
        out_specs=[pl.BlockSpec((tm, Z_C), lambda i: (i, 0)), pl.BlockSpec((tm, Z_KR), lambda i: (i, 0)),
                   fixed(384, 1024), fixed(256, 1024), fixed(1, 384), fixed(1, 256)],
        out_shape=[jax.ShapeDtypeStruct((t, Z_C), MXU_DTYPE), jax.ShapeDtypeStruct((t, Z_KR), MXU_DTYPE),
                   jax.ShapeDtypeStruct((384, 1024), F32), jax.ShapeDtypeStruct((256, 1024), F32),
                   jax.ShapeDtypeStruct((1, 384), F32), jax.ShapeDtypeStruct((1, 256), F32)],
        compiler_params=_cp())(dq, dk, dv, z, rc, rs, w_uq, w_ukv, gq, gkv)


def _loop_in_long_trips(n, body):
    def two(t, carry):
        return body(2 * t + 1, body(2 * t, carry))

    def four(t, carry):
        return two(2 * t + 1, two(2 * t, carry))

    lax.fori_loop(0, n // 4, four, 0)
    lax.fori_loop(2 * (n // 4), n // 2, two, 0)
    lax.fori_loop(2 * (n // 2), n, body, 0)


def _init_mask_bias(bias_ref):
    _, tq, tk = bias_ref.shape
    r = lax.broadcasted_iota(jnp.int32, (tq, tk), 0) // 64
    c = lax.broadcasted_iota(jnp.int32, (tq, tk), 1) // 64
    bias_ref[0] = jnp.zeros((tq, tk), F32)
    for d in range(tq // tk):
        bias_ref[1 + d] = jnp.where(c + d * (tk // 64) <= r, 0.0, NEG_INF)


def _gate_block(tq):
    return pl.BlockSpec((tq, 128), lambda h, i: (i, (Z_MIX + Z_C + Z_KR) // 128 + h))


def _attn_fwd(qh, kh, vh, z, name, tq=1024, tk=512):
    t = qh.shape[1]
    tq = min(tq, t)
    tk = min(tk, tq)
    ratio = tq // tk

    def body(q_ref, g_ref, k_hbm, v_hbm, o_ref, yc_ref, lse_ref, k_v, v_v, m_s, acc_s, s_a, s_b, mx_a, mx_b, bias_s,
             sem):
        h, i = pl.program_id(0), pl.program_id(1)

        @pl.when(i == 0)
        def _():
            ck = pltpu.make_async_copy(k_hbm.at[h], k_v, sem.at[0])
            cv = pltpu.make_async_copy(v_hbm.at[h], v_v.at[:, 0:V_DIM], sem.at[1])
            ck.start()
            cv.start()
            v_v[:, V_DIM:2 * V_DIM] = jnp.ones((t, V_DIM), MXU_DTYPE)
            _init_mask_bias(bias_s)
            ck.wait()
            cv.wait()

        q = q_ref[...]
        m_s[...] = jnp.full(m_s.shape, NEG_INF, F32)
        acc_s[...] = jnp.zeros(acc_s.shape, F32)

        last = ratio * (i + 1) - 1

        def keys(j):
            return pl.ds(pl.multiple_of(j * tk, tk), tk)

        def scores(s_ref, mx_ref, j):
            s = _dot_nt(q, k_v[keys(j), :]) * (SCALE * LOG2E) + bias_s[jnp.maximum(j - ratio * i + 1, 0)]
            s_ref[...] = s
            mx_ref[...] = jnp.broadcast_to(jnp.max(s, axis=-1, keepdims=True), mx_ref.shape)

        def softmax_pv(s_ref, mx_ref, j):
            m_old = m_s[...]
            m_new = jnp.maximum(m_old, mx_ref[...])
            p = jnp.exp2(s_ref[...] - jnp.tile(m_new, (1, tk // 128)))
            alpha = jnp.exp2(m_old - m_new)
            m_s[...] = m_new
            acc_s[...] = jnp.tile(alpha, (1, 2)) * acc_s[...] + _dot(_mx(p), v_v[keys(j), :])

        scores(s_a, mx_a, 0)

        def pair(pp, carry):
            scores(s_b, mx_b, 2 * pp + 1)
            softmax_pv(s_a, mx_a, 2 * pp)
            scores(s_a, mx_a, jnp.minimum(2 * pp + 2, last))
            softmax_pv(s_b, mx_b, 2 * pp + 1)
            return carry

        _loop_in_long_trips((last + 1) // 2, pair)
        if ratio % 2 == 1:
            @pl.when(last % 2 == 0)
            def _():
                softmax_pv(s_a, mx_a, last)

        l = acc_s[:, V_DIM:2 * V_DIM]
        o = acc_s[:, 0:V_DIM] / l
        o_ref[...] = o
        yc_ref[...] = _mx(o * _silu_and_grad(g_ref[...])[0])
        lse_ref[...] = m_s[...] + jnp.log2(l)

    return pl.pallas_call(
        body, name=name, grid=(N_HEADS, t // tq),
        in_specs=[pl.BlockSpec((None, tq, QK_PAD), lambda h, i: (h, i, 0)), _gate_block(tq), ANY, ANY],
        out_specs=[pl.BlockSpec((tq, 128), lambda h, i: (i, h)), pl.BlockSpec((tq, 128), lambda h, i: (i, h)),
                   pl.BlockSpec((None, tq, 128), lambda h, i: (h, i, 0))],
        out_shape=[jax.ShapeDtypeStruct((t, N_HEADS * V_DIM), F32), jax.ShapeDtypeStruct((t, N_HEADS * V_DIM), MXU_DTYPE),
                   jax.ShapeDtypeStruct((N_HEADS, t, 128), F32)],
        scratch_shapes=[pltpu.VMEM((t, QK_PAD), MXU_DTYPE), pltpu.VMEM((t, 2 * V_DIM), MXU_DTYPE),
                        pltpu.VMEM((tq, 128), F32), pltpu.VMEM((tq, 2 * V_DIM), F32),
                        pltpu.VMEM((tq, tk), F32), pltpu.VMEM((tq, tk), F32), pltpu.VMEM((tq, 128), F32),
                        pltpu.VMEM((tq, 128), F32), pltpu.VMEM((ratio + 1, tq, tk), F32),
                        pltpu.SemaphoreType.DMA((2,))],
        compiler_params=_cp(2))(qh, z, kh, vh)


def _attn_bwd(qh, kh, vh, o, lse, dycat, z, name, tq=512):
    t = qh.shape[1]
    tq = min(tq, t)
    nq = t // tq

    def body(q_ref, o_ref, lse_ref, dy_ref, g_ref, k_hbm, v_hbm, dq_ref, dgate_ref, dk_hbm, dv_hbm,
             k_v, v_v, dk_acc, dv_acc, dq_acc, delta_s, s_a, dp_a, s_b, dp_b, bias_s, sem):
        h, i = pl.program_id(0), pl.program_id(1)

        @pl.when(i == 0)
        def _():
            ck = pltpu.make_async_copy(k_hbm.at[h], k_v, sem.at[0])
            cv = pltpu.make_async_copy(v_hbm.at[h], v_v, sem.at[1])
            ck.start()
            cv.start()
            _init_mask_bias(bias_s)
            dk_acc[...] = jnp.zeros(dk_acc.shape, F32)
            dv_acc[...] = jnp.zeros(dv_acc.shape, F32)
            ck.wait()
            cv.wait()

        gate, dy, of = g_ref[...], dy_ref[...], o_ref[...]
        silu, dsilu = _silu_and_grad(gate)
        do = dy * silu
        delta = jnp.sum(do * of, axis=-1, keepdims=True)
        dgate_ref[...] = _mx(dy * of * dsilu)
        dob = _mx(do)
        q = q_ref[...]
        delta_s[...] = jnp.broadcast_to(delta, delta_s.shape)
        dq_acc[...] = jnp.zeros(dq_acc.shape, F32)

        def keys(j):
            return pl.ds(pl.multiple_of(j * tq, tq), tq)

        def scores(s_ref, dp_ref, j):
            s = _dot_nt(q, k_v[keys(j), :]) * (SCALE * LOG2E) + bias_s[(j == i).astype(jnp.int32)]
            s_ref[...] = s - jnp.tile(lse_ref[...], (1, tq // 128))
            dp_ref[...] = _dot_nt(dob, v_v[keys(j), :]) - jnp.tile(delta_s[...], (1, tq // 128))

        def grads(s_ref, dp_ref, j):
            ks = keys(j)
            p = jnp.exp2(s_ref[...])
            ds = p * dp_ref[...] * SCALE
            pb, dsb = _mx(p), _mx(ds)
            dq_acc[...] += _dot(dsb, k_v[ks, :])
            dk_acc[ks, :] += _dot_tn(dsb, q)
            dv_acc[ks, :] += _dot_tn(pb, dob)

        scores(s_a, dp_a, 0)

        def pair(pp, carry):
            scores(s_b, dp_b, 2 * pp + 1)
            grads(s_a, dp_a, 2 * pp)
            scores(s_a, dp_a, jnp.minimum(2 * pp + 2, i))
            grads(s_b, dp_b, 2 * pp + 1)
            return carry

        _loop_in_long_trips((i + 1) // 2, pair)

        @pl.when(i % 2 == 0)
        def _():
            grads(s_a, dp_a, i)

        dq_ref[...] = dq_acc[...]

        @pl.when(i == nq - 1)
        def _():
            ck = pltpu.make_async_copy(dk_acc, dk_hbm.at[h], sem.at[0])
            cv = pltpu.make_async_copy(dv_acc, dv_hbm.at[h], sem.at[1])
            ck.start()
            cv.start()
            ck.wait()
            cv.wait()

    return pl.pallas_call(
        body, name=name, grid=(N_HEADS, nq),
        in_specs=[pl.BlockSpec((None, tq, QK_PAD), lambda h, i: (h, i, 0)),
                  pl.BlockSpec((tq, 128), lambda h, i: (i, h)),
                  pl.BlockSpec((None, tq, 128), lambda h, i: (h, i, 0)),
                  pl.BlockSpec((tq, 128), lambda h, i: (i, N_HEADS + h)), _gate_block(tq), ANY, ANY],
        out_specs=[pl.BlockSpec((None, tq, QK_PAD), lambda h, i: (h, i, 0)),
                   pl.BlockSpec((tq, 128), lambda h, i: (i, h)), ANY, ANY],
        out_shape=[jax.ShapeDtypeStruct((N_HEADS, t, QK_PAD), F32), jax.ShapeDtypeStruct((t, Z_GATE), MXU_DTYPE),
                   jax.ShapeDtypeStruct((N_HEADS, t, QK_PAD), F32), jax.ShapeDtypeStruct((N_HEADS, t, V_DIM), F32)],
        scratch_shapes=[pltpu.VMEM((t, QK_PAD), MXU_DTYPE), pltpu.VMEM((t, V_DIM), MXU_DTYPE),
                        pltpu.VMEM((t, QK_PAD), F32), pltpu.VMEM((t, V_DIM), F32), pltpu.VMEM((tq, QK_PAD), F32),
                        pltpu.VMEM((tq, 128), F32)] + [pltpu.VMEM((tq, tq), F32)] * 4
        + [pltpu.VMEM((2, tq, tq), F32), pltpu.SemaphoreType.DMA((2,))],
        compiler_params=_cp(2))(qh, o, lse, dycat, z, kh, vh)


def _out_proj_fwd(yab, yc, w, x, g, target, name, tm=512):
    t, d = x.shape
    tm = min(tm, t)
    is_last = target is not None

    def body(*refs):
        if is_last:
            yab_ref, yc_ref, w_ref, x_ref, g_ref, t_ref, y_ref, dout_ref, loss_ref = refs
            _zero_when(pl.program_id(0) == 0, loss_ref)
        else:
            yab_ref, yc_ref, w_ref, x_ref, g_ref, y_ref, out_ref = refs
        y = _dot(jnp.concatenate([yab_ref[...], yc_ref[...]], axis=1), w_ref[...])
        y_ref[...] = y
        out = x_ref[...] + _rms(y, g_ref[...])[0]
        if is_last:
            diff = out - t_ref[...]
            dout_ref[...] = diff * (1.0 / d)
            part = jnp.sum(jnp.sum(diff * diff, axis=-1, keepdims=True), axis=0, keepdims=True) * (0.5 / d)
            _acc(loss_ref, jnp.broadcast_to(part, (1, 128)))
        else:
            out_ref[...] = out

    row = lambda wd: pl.BlockSpec((tm, wd), lambda i: (i, 0))
    fixed = lambda a, b: pl.BlockSpec((a, b), lambda i: (0, 0))
    in_specs = [row(512), row(512), fixed(d, d), row(d), fixed(1, d)]
    args = [yab, yc, w, x, g]
    out_specs = [row(d), row(d)]
    out_shape = [jax.ShapeDtypeStruct((t, d), F32), jax.ShapeDtypeStruct((t, d), F32)]
    if is_last:
        in_specs.append(row(d))
        args.append(target)
        out_specs.append(fixed(1, 128))
        out_shape.append(jax.ShapeDtypeStruct((1, 128), F32))
    return pl.pallas_call(body, name=name, grid=(t // tm,), in_specs=in_specs, out_specs=out_specs,
                          out_shape=out_shape, compiler_params=_cp())(*args)


def _halves(tm):
    half = tm // 2 if tm >= 512 else tm
    return [pl.ds(s, half) for s in range(0, tm, half)]


def _out_proj_bwd(dout, y, yab, yc, w, g, name, tm=1024):
    t, d = y.shape
    tm = min(tm, t)

    def body(dout_ref, y_ref, yab_ref, yc_ref, w_ref, g_ref, dycat_ref, dw_ref, dg_ref):
        _zero_when(pl.program_id(0) == 0, dw_ref, dg_ref)
        dybs = []
        for rows in _halves(tm):
            y = y_ref[rows, :]
            r = lax.rsqrt(jnp.mean(y * y, axis=-1, keepdims=True) + EPS)
            dy, dgt = _rms_bwd(y, r, g_ref[...], dout_ref[rows, :])
            _acc(dg_ref, _colsum(dgt))
            dybs.append(_mx(dy))
        for rows, dyb in zip(_halves(tm), dybs):
            _acc(dw_ref, _dot_tn(jnp.concatenate([yab_ref[rows, :], yc_ref[rows, :]], axis=1), dyb))
            dycat_ref[rows, :] = _dot_nt(dyb, w_ref[...])

    row = lambda wd: pl.BlockSpec((tm, wd), lambda i: (i, 0))
    fixed = lambda a, b: pl.BlockSpec((a, b), lambda i: (0, 0), pipeline_mode=pl.Buffered(1))
    return pl.pallas_call(
        body, name=name, grid=(t // tm,),
        in_specs=[row(d), row(d), row(512), row(512), fixed(d, d), fixed(1, d)],
        out_specs=[row(d), fixed(d, d), fixed(1, d)],
        out_shape=[jax.ShapeDtypeStruct((t, d), F32), jax.ShapeDtypeStruct((d, d), F32),
                   jax.ShapeDtypeStruct((1, d), F32)],
        compiler_params=_cp())(dout, y, yab, yc, w, g)


def _mesh_pos():
    return lax.axis_index("x"), lax.axis_index("y"), lax.axis_index("c")


def _remote(src, dst, send_sem, recv_sem, to):
    return pltpu.make_async_remote_copy(src_ref=src, dst_ref=dst, send_sem=send_sem, recv_sem=recv_sem,
                                        device_id=to, device_id_type=MESH)


CHUNK_ROWS = 256


def _pieces(rows):
    return [(s, min(CHUNK_ROWS, rows - s)) for s in range(0, rows, CHUNK_ROWS)]


def _piece_table(shapes):
    return [(a, s, sz) for a, shp in enumerate(shapes) for s, sz in _pieces(shp[-2])]


def _gather_weights(shards):
    n = len(shards)
    table = _piece_table([s.shape for s in shards])
    npc = len(table)

    def body(*refs):
        ins, outs = refs[:n], refs[n:2 * n]
        send_sems, recv_sems, fwd_send, fwd_recv = refs[2 * n:]
        x, y, c = _mesh_pos()
        me, sibling = (x, y, c), (x, y, 1 - c)
        chips = [(1 - x, y), (x, 1 - y), (1 - x, 1 - y)]
        slot = lambda cx, cy, layer: 2 * (2 * cx + cy) + layer
        first = []
        for a in range(n):
            for j, (cx, cy) in enumerate(chips):
                first.append(_remote(ins[a].at[c], outs[a].at[slot(x, y, c)], send_sems.at[a, j], recv_sems.at[a, j],
                                     (cx, cy, c)))
                first[-1].start()
        passed = []
        for j, (cx, cy) in enumerate(chips):
            for a in range(n):
                blk = outs[a].at[slot(cx, cy, c)]
                _remote(blk, blk, send_sems.at[a, j], recv_sems.at[a, j], me).wait_recv()
            for q, (a, s, sz) in enumerate(table):
                rows = outs[a].at[slot(cx, cy, c), pl.ds(s, sz)]
                passed.append(_remote(rows, rows, fwd_send.at[j, q], fwd_recv.at[j, q], sibling))
                passed[-1].start()
        for j, (cx, cy) in enumerate(chips):
            for q, (a, s, sz) in enumerate(table):
                rows = outs[a].at[slot(cx, cy, 1 - c), pl.ds(s, sz)]
                _remote(rows, rows, fwd_send.at[j, q], fwd_recv.at[j, q], me).wait_recv()
        for cp in first + passed:
            cp.wait_send()

    return pl.pallas_call(
        body, name="gather_weights", in_specs=[ANY] * n, out_specs=[ANY] * n,
        out_shape=[jax.ShapeDtypeStruct((8,) + s.shape[1:], s.dtype) for s in shards],
        scratch_shapes=[pltpu.SemaphoreType.DMA((n, 3)), pltpu.SemaphoreType.DMA((n, 3)),
                        pltpu.SemaphoreType.DMA((3, npc)), pltpu.SemaphoreType.DMA((3, npc))])(*shards)


def _pair_exchange(parts, common):
    n = len(parts)
    table = _piece_table([p.shape for p in parts] + [common.shape])
    npc = len(table)

    def body(*refs):
        ins, outs = refs[:n + 1], refs[n + 1:2 * n + 2]
        send_sems, recv_sems = refs[2 * n + 2:]
        x, y, c = _mesh_pos()
        sent = []
        for k in range(4):
            for q, (a, s, sz) in enumerate(table):
                if a == n and k > 0:
                    continue
                src = ins[a].at[2 * k + 1 - c, pl.ds(s, sz)] if a < n else ins[a].at[pl.ds(s, sz)]
                dst = outs[a].at[k, pl.ds(s, sz)] if a < n else outs[a].at[pl.ds(s, sz)]
                sent.append(_remote(src, dst, send_sems.at[k, q], recv_sems.at[k, q], (x, y, 1 - c)))
                sent[-1].start()
        for k in range(4):
            for q, (a, s, sz) in enumerate(table):
                if a == n and k > 0:
                    continue
                dst = outs[a].at[k, pl.ds(s, sz)] if a < n else outs[a].at[pl.ds(s, sz)]
                _remote(dst, dst, send_sems.at[k, q], recv_sems.at[k, q], (x, y, c)).wait_recv()
        for cp in sent:
            cp.wait_send()

    return pl.pallas_call(
        body, name="grad_pair_exchange", in_specs=[ANY] * (n + 1), out_specs=[ANY] * (n + 1),
        out_shape=[jax.ShapeDtypeStruct((4,) + p.shape[1:], p.dtype) for p in parts]
        + [jax.ShapeDtypeStruct(common.shape, common.dtype)],
        scratch_shapes=[pltpu.SemaphoreType.DMA((4, npc)), pltpu.SemaphoreType.DMA((4, npc))])(*parts, common)


def _chip_exchange(parts, common):
    n = len(parts)
    table = _piece_table([p.shape for p in parts] + [common.shape])
    npc = len(table)

    def body(*refs):
        ins, outs = refs[:n + 1], refs[n + 1:2 * n + 2]
        send_sems, recv_sems = refs[2 * n + 2:]
        x, y, c = _mesh_pos()
        mine = 2 * x + y
        chips = [(1 - x, y), (x, 1 - y), (1 - x, 1 - y)]
        src = lambda a, k: ins[a].at[k] if a < n else ins[a]
        sent = []
        for j, (cx, cy) in enumerate(chips):
            for q, (a, s, sz) in enumerate(table):
                sent.append(_remote(src(a, 2 * cx + cy).at[pl.ds(s, sz)], outs[a].at[mine, pl.ds(s, sz)],
                                    send_sems.at[j, q], recv_sems.at[j, q], (cx, cy, c)))
                sent[-1].start()
        for j, (cx, cy) in enumerate(chips):
            for q, (a, s, sz) in enumerate(table):
                dst = outs[a].at[2 * cx + cy, pl.ds(s, sz)]
                _remote(dst, dst, send_sems.at[j, q], recv_sems.at[j, q], (x, y, c)).wait_recv()
        for cp in sent:
            cp.wait_send()

    return pl.pallas_call(
        body, name="grad_chip_exchange", in_specs=[ANY] * (n + 1), out_specs=[ANY] * (n + 1),
        out_shape=[jax.ShapeDtypeStruct(p.shape, p.dtype) for p in parts]
        + [jax.ShapeDtypeStruct((4,) + common.shape, common.dtype)],
        scratch_shapes=[pltpu.SemaphoreType.DMA((3, npc)), pltpu.SemaphoreType.DMA((3, npc))])(*parts, common)


def _sibling_exchange(sums):
    n = len(sums)
    table = _piece_table([s.shape for s in sums])
    npc = len(table)

    def body(*refs):
        ins, outs = refs[:n], refs[n:2 * n]
        send_sems, recv_sems = refs[2 * n:]
        x, y, c = _mesh_pos()
        sent = []
        for q, (a, s, sz) in enumerate(table):
            sent.append(_remote(ins[a].at[pl.ds(s, sz)], outs[a].at[pl.ds(s, sz)], send_sems.at[q], recv_sems.at[q],
                                (x, y, 1 - c)))
            sent[-1].start()
        for q, (a, s, sz) in enumerate(table):
            dst = outs[a].at[pl.ds(s, sz)]
            _remote(dst, dst, send_sems.at[q], recv_sems.at[q], (x, y, c)).wait_recv()
        for cp in sent:
            cp.wait_send()

    return pl.pallas_call(
        body, name="sibling_exchange", in_specs=[ANY] * n, out_specs=[ANY] * n,
        out_shape=[jax.ShapeDtypeStruct(s.shape, s.dtype) for s in sums],
        scratch_shapes=[pltpu.SemaphoreType.DMA((npc,)), pltpu.SemaphoreType.DMA((npc,))])(*sums)


def _pair_sum(parts, got, name, tr):
    _, r, c = got.shape
    tr = min(tr, r)

    def body(p_ref, g_ref, o_ref, w_ref):
        total = p_ref[...] + g_ref[...]
        o_ref[...] = total
        w_ref[...] = total.astype(WIRE_DTYPE)

    blk = pl.BlockSpec((None, tr, c), lambda k, i: (k, i, 0))
    mine = pl.BlockSpec((None, tr, c), lambda k, i: (2 * k + lax.axis_index("c"), i, 0))
    return pl.pallas_call(
        body, name=name, grid=(4, r // tr), in_specs=[mine, blk], out_specs=[blk, blk],
        out_shape=[jax.ShapeDtypeStruct(got.shape, F32), jax.ShapeDtypeStruct(got.shape, WIRE_DTYPE)],
        compiler_params=_cp(2))(parts, got)


def _sum_chips(pair_sums, recv, name, tr):
    _, r, c = pair_sums.shape
    tr = min(tr, r)

    def body(own_ref, r_ref, o_ref):
        chip = 2 * lax.axis_index("x") + lax.axis_index("y")
        own_blk = own_ref[...]
        acc = jnp.where(chip == 0, own_blk, r_ref[0].astype(F32))
        for k in range(1, 4):
            acc = acc + jnp.where(chip == k, own_blk, r_ref[k].astype(F32))
        o_ref[...] = acc

    return pl.pallas_call(
        body, name=name, grid=(r // tr,),
        in_specs=[pl.BlockSpec((None, tr, c), lambda i: (2 * lax.axis_index("x") + lax.axis_index("y"), i, 0)),
                  pl.BlockSpec((4, tr, c), lambda i: (0, i, 0))],
        out_specs=pl.BlockSpec((tr, c), lambda i: (i, 0)),
        out_shape=jax.ShapeDtypeStruct((r, c), F32), compiler_params=_cp())(pair_sums, recv)


def _sum_leading(parts, name, tr):
    nlead, r, c = parts.shape
    tr = min(tr, r)

    def body(p_ref, o_ref):
        acc = p_ref[0]
        for j in range(1, nlead):
            acc = acc + p_ref[j]
        o_ref[...] = acc

    return pl.pallas_call(
        body, name=name, grid=(r // tr,),
        in_specs=[pl.BlockSpec((nlead, tr, c), lambda i: (0, i, 0))], out_specs=pl.BlockSpec((tr, c), lambda i: (i, 0)),
        out_shape=jax.ShapeDtypeStruct((r, c), parts.dtype), compiler_params=_cp())(parts)


def _adamw_update(w_ref, g_ref, m_ref, v_ref, d_ref, nm_ref, nv_ref):
    gg = g_ref[...]
    nm = ADAM_B1 * m_ref[...] + (1.0 - ADAM_B1) * gg
    nv = ADAM_B2 * v_ref[...] + (1.0 - ADAM_B2) * jnp.square(gg)
    m_hat = nm / (1.0 - ADAM_B1 ** ADAM_STEP)
    v_hat = nv / (1.0 - ADAM_B2 ** ADAM_STEP)
    d_ref[...] = -ADAM_LR * (m_hat / (jnp.sqrt(v_hat) + ADAM_EPS) + ADAM_WD * w_ref[...])
    nm_ref[...] = nm
    nv_ref[...] = nv


def _adamw(w, g, m, v, name, tr):
    r, c = w.shape
    tr = min(tr, r)

    def body(w_ref, g_ref, m_ref, v_ref, d_ref, nm_ref, nv_ref):
        _adamw_update(w_ref, g_ref, m_ref, v_ref, d_ref, nm_ref, nv_ref)

    blk = pl.BlockSpec((tr, c), lambda i: (i, 0))
    return pl.pallas_call(
        body, name=name, grid=(r // tr,), in_specs=[blk] * 4, out_specs=[blk] * 3,
        out_shape=[jax.ShapeDtypeStruct((r, c), F32)] * 3, compiler_params=_cp())(w, g, m, v)


def _adamw_many(ws, gs, ms, vs, name):
    n = len(ws)

    def body(*refs):
        ins, outs = refs[:4 * n], refs[4 * n:]
        for k in range(n):
            _adamw_update(ins[k], ins[n + k], ins[2 * n + k], ins[3 * n + k], outs[k], outs[n + k], outs[2 * n + k])

    vmem = pl.BlockSpec(memory_space=pltpu.VMEM)
    res = pl.pallas_call(
        body, name=name, in_specs=[vmem] * (4 * n), out_specs=[vmem] * (3 * n),
        out_shape=[jax.ShapeDtypeStruct(a.shape, F32) for a in ws] * 3)(*ws, *gs, *ms, *vs)
    return res[:n], res[n:2 * n], res[2 * n:]


def _rope_tables(positions):
    inv_freq = ROPE_BASE ** (-jnp.arange(0, 64, 2, dtype=F32) / 64)
    ang = positions.astype(F32)[:, None] * inv_freq
    cos, sin = jnp.cos(ang), jnp.sin(ang)
    t = positions.shape[0]
    rc = jnp.concatenate([jnp.ones((t, 128), F32), cos, cos, jnp.ones((t, 64), F32)], axis=1)
    rs = jnp.concatenate([jnp.zeros((t, 128), F32), sin, sin, jnp.zeros((t, 64), F32)], axis=1)
    return rc, rs


def _layer_params(l, w_in, w_uq, w_ukv, w_out, small):
    p = {}
    p["w_in"] = jnp.concatenate([w_in[l][:, :1984], jnp.zeros((1024, 64), w_in.dtype), w_in[l][:, 1984:]], axis=1)
    p["w_uq"] = jnp.pad(w_uq[l].reshape(384, 4, 192), ((0, 0), (0, 0), (0, 64))).reshape(384, 1024)
    p["w_ukv"] = w_ukv[l].reshape(256, 4, 2, 128).transpose(0, 2, 1, 3).reshape(256, 1024)
    p["w_out"] = w_out[l]
    p["pre_g"] = small["pre_norm_g"][l][None]
    p["post_g"] = small["post_norm_g"][l][None]
    p["sgu_w"] = small["sgu_w"][l].reshape(512, 128)
    p["sgu_wt"] = small["sgu_w"][l].transpose(0, 2, 1).reshape(512, 128)
    p["sgu_bias"] = jnp.repeat(small["sgu_b"][l].T, 64, axis=1)
    p["ln_g"] = small["sgu_ln_g"][l][None]
    p["ln_b"] = small["sgu_ln_b"][l][None]
    p["pool_wbd"] = _mx(jax.scipy.linalg.block_diag(*[small["pool_w"][l][gi] for gi in range(4)]))
    p["pool_scale"] = small["pool_scale"][l][None]
    p["gq"] = small["q_norm_g"][l][None]
    p["gkv"] = small["kv_norm_g"][l][None]
    return p


def _layer_fwd(l, x, p, rc, rs, target):
    z, h = _in_proj_fwd(x, p["pre_g"], p["w_in"], f"in_proj_fwd_{l}")
    yab = _mix_fwd(z, p["sgu_w"], p["sgu_bias"], p["ln_g"], p["ln_b"], p["pool_wbd"], p["pool_scale"], f"mix_fwd_{l}")
    qh, kh, vh = _qkv_fwd(z, rc, rs, p["w_uq"], p["w_ukv"], p["gq"], p["gkv"], f"qkv_fwd_{l}")
    o, yc, lse = _attn_fwd(qh, kh, vh, z, f"attn_fwd_{l}")
    outs = _out_proj_fwd(yab, yc, p["w_out"], x, p["post_g"], target, f"out_proj_fwd_{l}")
    saved = dict(x=x, z=z, h=h, yab=yab, qh=qh, kh=kh, vh=vh, o=o, yc=yc, lse=lse, y=outs[0])
    return saved, outs[1:]


def _layer_bwd(l, dout, sv, p, rc, rs):
    dycat, dw_out, dpost = _out_proj_bwd(dout, sv["y"], sv["yab"], sv["yc"], p["w_out"], p["post_g"], f"out_proj_bwd_{l}")
    dq, dgate, dk, dv = _attn_bwd(sv["qh"], sv["kh"], sv["vh"], sv["o"], sv["lse"], dycat, sv["z"], f"attn_bwd_{l}")
    dzc, dzk, dwq, dwkv, dgq, dgkv = _qkv_bwd(dq, dk, dv, sv["z"], rc, rs, p["w_uq"], p["w_ukv"], p["gq"], p["gkv"],
                                              f"qkv_bwd_{l}")
    dzm, dsw, dsb, dlng, dlnb, dpw, dps = _mix_bwd(sv["z"], dycat, p["sgu_w"], p["sgu_wt"], p["sgu_bias"], p["ln_g"],
                                                   p["ln_b"], p["pool_wbd"], p["pool_scale"], f"mix_bwd_{l}")
    dx, dw_in, dpre = _in_proj_bwd(dzm, dzc, dzk, dgate, sv["h"], sv["x"], dout, p["w_in"], p["pre_g"], f"in_proj_bwd_{l}")
    grads = {
        "pre_norm_g": dpre[0], "post_norm_g": dpost[0],
        "w_in": jnp.concatenate([dw_in[:, :1984], dw_in[:, 2048:]], axis=1),
        "sgu_w": dsw.reshape(4, 128, 128), "sgu_b": dsb[:, :4].T, "sgu_ln_g": dlng[0], "sgu_ln_b": dlnb[0],
        "pool_w": jnp.stack([dpw[64 * gi:64 * gi + 64, 64 * gi:64 * gi + 64] for gi in range(4)]),
        "pool_scale": dps[0], "q_norm_g": dgq[0],
        "w_uq": dwq.reshape(384, 4, 256)[:, :, :192].reshape(384, 768), "kv_norm_g": dgkv[0],
        "w_ukv": dwkv.reshape(256, 2, 4, 128).transpose(0, 2, 1, 3).reshape(256, 1024), "w_out": dw_out,
    }
    return dx, grads


SMALL_NAMES = ["pre_norm_g", "post_norm_g", "sgu_w", "sgu_b", "sgu_ln_g", "sgu_ln_b", "pool_w", "pool_scale",
               "q_norm_g", "kv_norm_g"]
BIG_NAMES = ["w_in", "w_uq", "w_ukv", "w_out"]
WEIGHT_NAMES = ["pre_norm_g", "post_norm_g", "w_in", "sgu_w", "sgu_b", "sgu_ln_g", "sgu_ln_b", "pool_w", "pool_scale",
                "q_norm_g", "w_uq", "kv_norm_g", "w_ukv", "w_out"]


def _local_step(x, positions, target, w_in, w_uq, w_ukv, w_out, small):
    rc, rs = _rope_tables(positions)
    params = [_layer_params(l, w_in, w_uq, w_ukv, w_out, small) for l in range(DEPTH)]
    saved = []
    for l in range(DEPTH):
        sv, outs = _layer_fwd(l, x, params[l], rc, rs, target if l == DEPTH - 1 else None)
        saved.append(sv)
        if l < DEPTH - 1:
            x = outs[0]
    dout, loss = outs
    grads = [None] * DEPTH
    for l in reversed(range(DEPTH)):
        dout, grads[l] = _layer_bwd(l, dout, saved[l], params[l], rc, rs)
    return loss[0, 0], dout, {k: jnp.stack([grads[l][k] for l in range(DEPTH)]) for k in WEIGHT_NAMES}


def _pack_small(tree, extra=None):
    pieces = [tree[k].reshape(-1) for k in SMALL_NAMES]
    pieces.append(jnp.zeros((1,), F32) if extra is None else extra.reshape(1))
    flat = jnp.concatenate(pieces)
    rows = -(-flat.shape[0] // 1024) * 8
    return jnp.pad(flat, (0, rows * 128 - flat.shape[0])).reshape(rows, 128)


def _unpack_small(packed, like):
    flat = packed.reshape(-1)
    out, off = {}, 0
    for k in SMALL_NAMES:
        size = like[k].size
        out[k] = flat[off:off + size].reshape(like[k].shape)
        off += size
    return out, flat[off]


def kernel(x, positions, pre_norm_g, post_norm_g, w_in, sgu_w, sgu_b, sgu_ln_g, sgu_ln_b, pool_w, pool_scale, q_norm_g, w_uq, kv_norm_g, w_ukv, w_out, loss_target, m_pre_norm_g, m_post_norm_g, m_w_in, m_sgu_w, m_sgu_b, m_sgu_ln_g, m_sgu_ln_b, m_pool_w, m_pool_scale, m_q_norm_g, m_w_uq, m_kv_norm_g, m_w_ukv, m_w_out, v_pre_norm_g, v_post_norm_g, v_w_in, v_sgu_w, v_sgu_b, v_sgu_ln_g, v_sgu_ln_b, v_pool_w, v_pool_scale, v_q_norm_g, v_w_uq, v_kv_norm_g, v_w_ukv, v_w_out):
    w = dict(pre_norm_g=pre_norm_g, post_norm_g=post_norm_g, w_in=w_in, sgu_w=sgu_w, sgu_b=sgu_b, sgu_ln_g=sgu_ln_g,
             sgu_ln_b=sgu_ln_b, pool_w=pool_w, pool_scale=pool_scale, q_norm_g=q_norm_g, w_uq=w_uq, kv_norm_g=kv_norm_g,
             w_ukv=w_ukv, w_out=w_out)
    m = dict(pre_norm_g=m_pre_norm_g, post_norm_g=m_post_norm_g, w_in=m_w_in, sgu_w=m_sgu_w, sgu_b=m_sgu_b,
             sgu_ln_g=m_sgu_ln_g, sgu_ln_b=m_sgu_ln_b, pool_w=m_pool_w, pool_scale=m_pool_scale, q_norm_g=m_q_norm_g,
             w_uq=m_w_uq, kv_norm_g=m_kv_norm_g, w_ukv=m_w_ukv, w_out=m_w_out)
    v = dict(pre_norm_g=v_pre_norm_g, post_norm_g=v_post_norm_g, w_in=v_w_in, sgu_w=v_sgu_w, sgu_b=v_sgu_b,
             sgu_ln_g=v_sgu_ln_g, sgu_ln_b=v_sgu_ln_b, pool_w=v_pool_w, pool_scale=v_pool_scale, q_norm_g=v_q_norm_g,
             w_uq=v_w_uq, kv_norm_g=v_kv_norm_g, w_ukv=v_w_ukv, w_out=v_w_out)

    core = lax.axis_index("c")
    chip = 2 * lax.axis_index("x") + lax.axis_index("y")
    shards = [_mx(w[k]) for k in BIG_NAMES]
    gathered = _gather_weights(shards)
    g_in, g_uq, g_ukv, g_out = [lax.dynamic_update_slice(g, s, (2 * chip, 0, 0)) for g, s in zip(gathered, shards)]
    cols = lambda g: g.reshape((4, 2) + g.shape[1:]).transpose(1, 2, 0, 3).reshape(2, g.shape[1], 4 * g.shape[2])
    full_out = g_out.reshape(4, 2, 256, 1024).transpose(1, 0, 2, 3).reshape(2, 1024, 1024)
    loss, dx, grads = _local_step(x[0], positions[0], loss_target[0], cols(g_in), cols(g_uq), cols(g_ukv), full_out, w)

    split_cols = lambda g: g.reshape(2, g.shape[1], 4, g.shape[2] // 4).transpose(2, 0, 1, 3).reshape(8, g.shape[1], g.shape[2] // 4)
    parts = [split_cols(grads["w_in"]), split_cols(grads["w_uq"]), split_cols(grads["w_ukv"]),
             grads["w_out"].reshape(2, 4, 256, 1024).transpose(1, 0, 2, 3).reshape(8, 256, 1024)]
    common = _pack_small(grads, loss)
    got = _pair_exchange(parts, common)
    pair_sums = [_pair_sum(parts[a], got[a], f"pair_sum_{BIG_NAMES[a]}", 128) for a in range(4)]
    chip_common = _sum_leading(jnp.stack([common, got[4]]), "pair_sum_small", common.shape[0])
    received = _chip_exchange([ps[1] for ps in pair_sums], chip_common)
    sums = [_sum_chips(pair_sums[a][0], received[a], f"sum_{BIG_NAMES[a]}", 128) for a in range(4)]
    all_common = lax.dynamic_update_slice(received[4], chip_common[None], (chip, 0, 0))
    small_sum, loss = _unpack_small(_sum_leading(all_common, "sum_small", all_common.shape[1]), w)
    others = _sibling_exchange(sums)
    total = dict(small_sum)
    for a, k in enumerate(BIG_NAMES):
        total[k] = jnp.where(core == 0, jnp.stack([sums[a], others[a]]), jnp.stack([others[a], sums[a]]))

    rows2d = lambda a: a.reshape(-1, a.shape[-1])
    small_out = _adamw_many(*[[rows2d(tree[k]) for k in SMALL_NAMES] for tree in (w, total, m, v)], "adamw_small")
    delta, new_m, new_v = ({k: r.reshape(w[k].shape) for k, r in zip(SMALL_NAMES, res)} for res in small_out)
    for k in BIG_NAMES:
        shape = w[k].shape
        flat = lambda a: a.reshape(shape[0] * shape[1], shape[2])
        res = _adamw(flat(w[k]), flat(total[k]), flat(m[k]), flat(v[k]), f"adamw_{k}", 256)
        delta[k], new_m[k], new_v[k] = (r.reshape(shape) for r in res)

    return (loss, dx[None], *[total[k] for k in WEIGHT_NAMES], *[delta[k] for k in WEIGHT_NAMES],
            *[new_m[k] for k in WEIGHT_NAMES], *[new_v[k] for k in WEIGHT_NAMES])
```

```python
import jax
import jax.numpy as jnp
from jax import lax
from jax.experimental import pallas as pl
from jax.experimental.pallas import tpu as pltpu

F32 = jnp.float32
MXU_DTYPE = jnp.bfloat16
WIRE_DTYPE = jnp.bfloat16
EPS = 1e-6
NEG_INF = -1e30
DEPTH = 2
N_HEADS = 4
QK_PAD = 256
V_DIM = 128
SCALE = 192 ** -0.5
LOG2E = 1.4426950408889634
ROPE_BASE = 10000.0
ADAM_LR, ADAM_B1, ADAM_B2, ADAM_EPS, ADAM_WD, ADAM_STEP = 0.001, 0.9, 0.999, 1e-08, 0.01, 10
VMEM_LIMIT_BYTES = 56 * 1024 * 1024
MESH = pl.DeviceIdType.MESH
ANY = pl.BlockSpec(memory_space=pl.ANY)

Z_MIX, Z_C, Z_KR, Z_GATE = 1280, 640, 128, 512
Z_W = Z_MIX + Z_C + Z_KR + Z_GATE


def _cp(n_axes=1):
    return pltpu.CompilerParams(dimension_semantics=("arbitrary",) * n_axes, vmem_limit_bytes=VMEM_LIMIT_BYTES)


def _dot(a, b):
    return lax.dot_general(a, b, (((1,), (0,)), ((), ())), preferred_element_type=F32)


def _dot_nt(a, b):
    return lax.dot_general(a, b, (((1,), (1,)), ((), ())), preferred_element_type=F32)


def _dot_tn(a, b):
    return lax.dot_general(a, b, (((0,), (0,)), ((), ())), preferred_element_type=F32)


def _mx(a):
    return a.astype(MXU_DTYPE)


def _silu_and_grad(g):
    sg = jax.nn.sigmoid(g)
    return g * sg, sg * (1.0 + g * (1.0 - sg))


def _rms(x, g):
    r = lax.rsqrt(jnp.mean(x * x, axis=-1, keepdims=True) + EPS)
    return x * r * g, r


def _rms_bwd(x, r, g, dy):
    xhat = x * r
    dyg = dy * g
    dx = r * (dyg - xhat * jnp.mean(dyg * xhat, axis=-1, keepdims=True))
    return dx, dy * xhat


def _zero_when(first, *refs):
    @pl.when(first)
    def _():
        for ref in refs:
            ref[...] = jnp.zeros(ref.shape, ref.dtype)


def _acc(ref, val):
    ref[...] += val


def _colsum(a):
    return jnp.sum(a, axis=0, keepdims=True)


def _in_proj_fwd(x, g, w, name, tm=512):
    t, d = x.shape
    n = w.shape[1]
    tm = min(tm, t)

    def body(x_ref, g_ref, w_ref, z_ref, h_ref):
        h, _ = _rms(x_ref[...], g_ref[...])
        h = _mx(h)
        h_ref[...] = h
        z_ref[...] = _dot(h, w_ref[...])

    return pl.pallas_call(
        body, name=name, grid=(t // tm,),
        in_specs=[pl.BlockSpec((tm, d), lambda i: (i, 0)), pl.BlockSpec((1, d), lambda i: (0, 0)),
                  pl.BlockSpec((d, n), lambda i: (0, 0))],
        out_specs=[pl.BlockSpec((tm, n), lambda i: (i, 0)), pl.BlockSpec((tm, d), lambda i: (i, 0))],
        out_shape=[jax.ShapeDtypeStruct((t, n), F32), jax.ShapeDtypeStruct((t, d), MXU_DTYPE)],
        compiler_params=_cp())(x, g, w)


def _in_proj_bwd(dz_mix, dz_c, dz_kr, dz_gate, h, x, d_res, w, g, name, tm=512):
    t, d = x.shape
    n = w.shape[1]
    tm = min(tm, t)

    def body(dm_ref, dc_ref, dk_ref, dg_ref, h_ref, x_ref, dres_ref, w_ref, g_ref, dx_ref, dw_ref, dgn_ref):
        first = pl.program_id(0) == 0
        dz = jnp.concatenate([dm_ref[...], dc_ref[...], dk_ref[...], dg_ref[...]], axis=1)

        _zero_when(first, dw_ref, dgn_ref)
        hb = h_ref[...]
        for c0 in range(0, n, 512):
            dw_ref[:, c0:c0 + 512] += _dot_tn(hb, dz[:, c0:c0 + 512])
        dh = _dot_nt(dz, w_ref[...])
        xf = x_ref[...]
        r = lax.rsqrt(jnp.mean(xf * xf, axis=-1, keepdims=True) + EPS)
        dx, dgt = _rms_bwd(xf, r, g_ref[...], dh)
        dx_ref[...] = dx + dres_ref[...]
        _acc(dgn_ref, _colsum(dgt))

    row = lambda wd: pl.BlockSpec((tm, wd), lambda i: (i, 0))
    fixed = lambda a, b: pl.BlockSpec((a, b), lambda i: (0, 0), pipeline_mode=pl.Buffered(1))
    return pl.pallas_call(
        body, name=name, grid=(t // tm,),
        in_specs=[row(Z_MIX), row(Z_C), row(Z_KR), row(Z_GATE), row(d), row(d), row(d), fixed(d, n), fixed(1, d)],
        out_specs=[row(d), fixed(d, n), fixed(1, d)],
        out_shape=[jax.ShapeDtypeStruct((t, d), F32), jax.ShapeDtypeStruct((d, n), F32),
                   jax.ShapeDtypeStruct((1, d), F32)],
        compiler_params=_cp())(dz_mix, dz_c, dz_kr, dz_gate, h, x, d_res, w, g)


def _lane_group(shape):
    return lax.broadcasted_iota(jnp.int32, shape, 1) // 64


def _select_group(vals):
    grp = _lane_group(vals[0].shape)
    out = vals[3]
    for gi in (2, 1, 0):
        out = jnp.where(grp == gi, vals[gi], out)
    return out


def _sgu_mask(transposed):
    r = (lax.broadcasted_iota(jnp.int32, (512, 128), 0) % 128) // 64
    c = lax.broadcasted_iota(jnp.int32, (512, 128), 1) // 64
    return (r <= c) if transposed else (c <= r)


def _sgu_apply(wstack, vb, nblk):
    outs = []
    for n in range(nblk):
        r = _dot(wstack, vb[n * 128:(n + 1) * 128, :])
        outs.append(_select_group([r[hh * 128:(hh + 1) * 128, :] for hh in range(4)]))
    return jnp.concatenate(outs, axis=0)


def _layer_norm(v, g, b):
    mu = jnp.mean(v, axis=-1, keepdims=True)
    vc = v - mu
    rstd = lax.rsqrt(jnp.mean(vc * vc, axis=-1, keepdims=True) + EPS)
    vhat = vc * rstd
    return vhat * g + b, vhat, rstd


def _pool_counts(t0, n):
    t = t0 + lax.broadcasted_iota(jnp.int32, (n, 256), 0)
    w = _select_group([jnp.full((n, 256), wv, jnp.int32) for wv in (2, 4, 8, 16)])
    return jnp.minimum(t + 1, w).astype(F32)


def _pooled(p, halo, t0):
    tm = p.shape[0]
    ext = jnp.concatenate([halo, p], axis=0)
    s2 = ext + pltpu.roll(ext, 1, 0)
    s4 = s2 + pltpu.roll(s2, 2, 0)
    s8 = s4 + pltpu.roll(s4, 4, 0)
    s16 = s8 + pltpu.roll(s8, 8, 0)
    sel = _select_group([s2, s4, s8, s16])[16:, :]
    return sel / _pool_counts(t0, tm) - p


def _pooled_bwd(dpool, dpool_halo, t0):
    tm = dpool.shape[0]
    n = tm + 16
    ext = jnp.concatenate([dpool, dpool_halo], axis=0) / _pool_counts(t0, n)
    f2 = ext + pltpu.roll(ext, n - 1, 0)
    f4 = f2 + pltpu.roll(f2, n - 2, 0)
    f8 = f4 + pltpu.roll(f4, n - 4, 0)
    f16 = f8 + pltpu.roll(f8, n - 8, 0)
    return _select_group([f2, f4, f8, f16])[:tm, :] - dpool


def _mix_specs(t, tm):
    nt16 = t // 16
    zrow = pl.BlockSpec((tm, Z_MIX), lambda i: (i, 0))
    prev_halo = pl.BlockSpec((16, 256), lambda i: (jnp.maximum(i * (tm // 16) - 1, 0), 3))
    fixed = lambda a, b: pl.BlockSpec((a, b), lambda i: (0, 0))
    params = [fixed(512, 128), fixed(128, 256), fixed(1, 256), fixed(1, 256), fixed(256, 256), fixed(1, 256)]
    return nt16, zrow, prev_halo, fixed, params


def _mix_fwd(z, sgu_w, sgu_bias, ln_g, ln_b, pool_wbd, pool_scale, name, tm=512):
    t = z.shape[0]
    tm = min(tm, t)
    _, zrow, prev_halo, _, params = _mix_specs(t, tm)

    def body(z_ref, halo_ref, w_ref, bias_ref, lng_ref, lnb_ref, pw_ref, ps_ref, y_ref):
        i = pl.program_id(0)
        u, v, gate = z_ref[:, 0:256], z_ref[:, 256:512], z_ref[:, 512:768]
        p, pgate = z_ref[:, 768:1024], z_ref[:, 1024:1280]
        vn, _, _ = _layer_norm(v, lng_ref[...], lnb_ref[...])
        wm = _mx(jnp.where(_sgu_mask(False), w_ref[...], 0.0))
        mixed = _sgu_apply(wm, _mx(vn), tm // 128) + jnp.tile(bias_ref[...], (tm // 128, 1))
        ya = u * mixed * _silu_and_grad(gate)[0]
        halo = jnp.where(i > 0, halo_ref[...], 0.0)
        pooled = _pooled(p, halo, i * tm)
        yb = _dot(_mx(pooled), pw_ref[...]) * ps_ref[...] * _silu_and_grad(pgate)[0]
        y_ref[...] = _mx(jnp.concatenate([ya, yb], axis=1))

    return pl.pallas_call(
        body, name=name, grid=(t // tm,),
        in_specs=[zrow, prev_halo] + params,
        out_specs=pl.BlockSpec((tm, 512), lambda i: (i, 0)),
        out_shape=jax.ShapeDtypeStruct((t, 512), MXU_DTYPE),
        compiler_params=_cp())(z, z, sgu_w, sgu_bias, ln_g, ln_b, pool_wbd, pool_scale)


def _mix_bwd(z, dycat, sgu_w, sgu_wt, sgu_bias, ln_g, ln_b, pool_wbd, pool_scale, name, tm=512):
    t = z.shape[0]
    tm = min(tm, t)
    nt16, zrow, prev_halo, fixed, params = _mix_specs(t, tm)
    nblk = tm // 128
    last = t // tm - 1

    def body(z_ref, halo_ref, zn_ref, dy_ref, dyn_ref, w_ref, wt_ref, bias_ref, lng_ref, lnb_ref, pw_ref, ps_ref,
             dz_ref, dw_ref, db_ref, dlng_ref, dlnb_ref, dpw_ref, dps_ref):
        i = pl.program_id(0)
        _zero_when(i == 0, dw_ref, db_ref, dlng_ref, dlnb_ref, dpw_ref, dps_ref)
        u, v, gate = z_ref[:, 0:256], z_ref[:, 256:512], z_ref[:, 512:768]
        p, pgate = z_ref[:, 768:1024], z_ref[:, 1024:1280]
        dya, dyb = dy_ref[:, 0:256], dy_ref[:, 256:512]
        vn, vhat, rstd = _layer_norm(v, lng_ref[...], lnb_ref[...])
        vnb = _mx(vn)
        wm = _mx(jnp.where(_sgu_mask(False), w_ref[...], 0.0))
        wmt = _mx(jnp.where(_sgu_mask(True), wt_ref[...], 0.0))
        mixed = _sgu_apply(wm, vnb, nblk) + jnp.tile(bias_ref[...], (nblk, 1))
        silu, dsilu = _silu_and_grad(gate)
        t1 = u * mixed
        d_gate = dya * t1 * dsilu
        d_t1 = dya * silu
        d_u = d_t1 * mixed
        d_mixed = d_t1 * u
        dmb = _mx(d_mixed)
        d_vn = _sgu_apply(wmt, dmb, nblk)
        grp = _lane_group((128, 256))
        lane = lax.broadcasted_iota(jnp.int32, (128, 128), 1)
        dws = [jnp.zeros((128, 128), F32) for _ in range(4)]
        dbias = jnp.zeros((128, 128), F32)
        for n in range(nblk):
            dm_n, dmb_n, vnb_n = d_mixed[n * 128:(n + 1) * 128], dmb[n * 128:(n + 1) * 128], vnb[n * 128:(n + 1) * 128]
            for hh in range(4):
                dws[hh] = dws[hh] + _dot_nt(jnp.where(grp == hh, dmb_n, jnp.zeros_like(dmb_n)), vnb_n)
                rs = jnp.sum(jnp.where(grp == hh, dm_n, 0.0), axis=-1, keepdims=True)
                dbias = dbias + jnp.where(lane == hh, rs, 0.0)
        _acc(dw_ref, jnp.concatenate(dws, axis=0))
        _acc(db_ref, dbias)
        _acc(dlng_ref, _colsum(d_vn * vhat))
        _acc(dlnb_ref, _colsum(d_vn))
        dvh = d_vn * lng_ref[...]
        d_v = rstd * (dvh - jnp.mean(dvh, axis=-1, keepdims=True) - vhat * jnp.mean(dvh * vhat, axis=-1, keepdims=True))

        @pl.when(i == last)
        def _():
            dw_ref[...] = jnp.where(_sgu_mask(False), dw_ref[...], 0.0)

        halo = jnp.where(i > 0, halo_ref[...], 0.0)
        pooled = _pooled(p, halo, i * tm)
        pooled_b = _mx(pooled)
        mixedp = _dot(pooled_b, pw_ref[...])
        psilu, pdsilu = _silu_and_grad(pgate)
        d_pgate = dyb * (mixedp * ps_ref[...]) * pdsilu
        d_ms = dyb * psilu
        _acc(dps_ref, _colsum(d_ms * mixedp))
        dmpb = _mx(d_ms * ps_ref[...])
        _acc(dpw_ref, _dot_tn(pooled_b, dmpb))
        d_pooled = _dot_nt(dmpb, pw_ref[...])
        dmp_halo = _mx(dyn_ref[...] * _silu_and_grad(zn_ref[...])[0] * ps_ref[...])
        d_pooled_halo = jnp.where(i < last, _dot_nt(dmp_halo, pw_ref[...]), 0.0)
        d_p = _pooled_bwd(d_pooled, d_pooled_halo, i * tm)
        dz_ref[...] = _mx(jnp.concatenate([d_u, d_v, d_gate, d_p, d_pgate], axis=1))

    nxt = lambda i: jnp.minimum((i + 1) * (tm // 16), nt16 - 1)
    return pl.pallas_call(
        body, name=name, grid=(t // tm,),
        in_specs=[zrow, prev_halo, pl.BlockSpec((16, 256), lambda i: (nxt(i), 4)),
                  pl.BlockSpec((tm, 512), lambda i: (i, 0)), pl.BlockSpec((16, 256), lambda i: (nxt(i), 1)),
                  params[0], fixed(512, 128)] + params[1:],
        out_specs=[pl.BlockSpec((tm, Z_MIX), lambda i: (i, 0)), fixed(512, 128), fixed(128, 128), fixed(1, 256),
                   fixed(1, 256), fixed(256, 256), fixed(1, 256)],
        out_shape=[jax.ShapeDtypeStruct((t, Z_MIX), MXU_DTYPE), jax.ShapeDtypeStruct((512, 128), F32),
                   jax.ShapeDtypeStruct((128, 128), F32), jax.ShapeDtypeStruct((1, 256), F32),
                   jax.ShapeDtypeStruct((1, 256), F32), jax.ShapeDtypeStruct((256, 256), F32),
                   jax.ShapeDtypeStruct((1, 256), F32)],
        compiler_params=_cp())(z, z, z, dycat, dycat, sgu_w, sgu_wt, sgu_bias, ln_g, ln_b, pool_wbd, pool_scale)


def _rot_half(x, transpose):
    w = x.shape[1]
    lane = lax.broadcasted_iota(jnp.int32, x.shape, 1) % min(w, 256)
    base = 128 if w >= 256 else 0
    lo = jnp.logical_and(lane >= base, lane < base + 32)
    hi = jnp.logical_and(lane >= base + 32, lane < base + 64)
    up = pltpu.roll(x, w - 32, 1)
    down = pltpu.roll(x, 32, 1)
    if transpose:
        return jnp.where(lo, up, jnp.where(hi, -down, 0.0))
    return jnp.where(lo, -up, jnp.where(hi, down, 0.0))


def _rope(x, c, s):
    return x * c + _rot_half(x, False) * s


def _rope_bwd(dy, c, s):
    return dy * c + _rot_half(dy * s, True)


def _qkv_fwd(z, rc, rs, w_uq, w_ukv, gq, gkv, name, tm=512):
    t = z.shape[0]
    tm = min(tm, t)

    def body(zc_ref, zk_ref, rc_ref, rs_ref, wq_ref, wkv_ref, gq_ref, gkv_ref, q_ref, k_ref, v_ref):
        cq, ckv = zc_ref[:, 0:384], zc_ref[:, 384:640]
        c, s = rc_ref[...], rs_ref[...]
        qn, _ = _rms(cq, gq_ref[...])
        q = _rope(_dot(_mx(qn), wq_ref[...]), jnp.tile(c, (1, N_HEADS)), jnp.tile(s, (1, N_HEADS)))
        kvn, _ = _rms(ckv, gkv_ref[...])
        kv = _dot(_mx(kvn), wkv_ref[...])
        kpe = _rope(zk_ref[...], c[:, 128:256], s[:, 128:256])
        for hh in range(N_HEADS):
            q_ref[hh] = _mx(q[:, hh * QK_PAD:(hh + 1) * QK_PAD])
            k_ref[hh] = _mx(jnp.concatenate([kv[:, hh * 128:(hh + 1) * 128], kpe], axis=1))
            v_ref[hh] = _mx(kv[:, 512 + hh * 128:512 + (hh + 1) * 128])

    fixed = lambda a, b: pl.BlockSpec((a, b), lambda i: (0, 0))
    heads = lambda wd: pl.BlockSpec((N_HEADS, tm, wd), lambda i: (0, i, 0))
    return pl.pallas_call(
        body, name=name, grid=(t // tm,),
        in_specs=[pl.BlockSpec((tm, Z_C), lambda i: (i, Z_MIX // Z_C)),
                  pl.BlockSpec((tm, Z_KR), lambda i: (i, (Z_MIX + Z_C) // Z_KR)),
                  pl.BlockSpec((tm, 256), lambda i: (i, 0)), pl.BlockSpec((tm, 256), lambda i: (i, 0)),
                  fixed(384, 1024), fixed(256, 1024), fixed(1, 384), fixed(1, 256)],
        out_specs=[heads(QK_PAD), heads(QK_PAD), heads(V_DIM)],
        out_shape=[jax.ShapeDtypeStruct((N_HEADS, t, QK_PAD), MXU_DTYPE),
                   jax.ShapeDtypeStruct((N_HEADS, t, QK_PAD), MXU_DTYPE),
                   jax.ShapeDtypeStruct((N_HEADS, t, V_DIM), MXU_DTYPE)],
        compiler_params=_cp())(z, z, rc, rs, w_uq, w_ukv, gq, gkv)


def _qkv_bwd(dq, dk, dv, z, rc, rs, w_uq, w_ukv, gq, gkv, name, tm=512):
    t = z.shape[0]
    tm = min(tm, t)

    def body(dq_ref, dk_ref, dv_ref, zc_ref, rc_ref, rs_ref, wq_ref, wkv_ref, gq_ref, gkv_ref,
             dzc_ref, dzk_ref, dwq_ref, dwkv_ref, dgq_ref, dgkv_ref):
        _zero_when(pl.program_id(0) == 0, dwq_ref, dwkv_ref, dgq_ref, dgkv_ref)
        cq, ckv = zc_ref[:, 0:384], zc_ref[:, 384:640]
        c, s = rc_ref[...], rs_ref[...]
        dq_all = jnp.concatenate([dq_ref[hh] for hh in range(N_HEADS)], axis=1)
        dqp = _mx(_rope_bwd(dq_all, jnp.tile(c, (1, N_HEADS)), jnp.tile(s, (1, N_HEADS))))
        qn, rq = _rms(cq, gq_ref[...])
        _acc(dwq_ref, _dot_tn(_mx(qn), dqp))
        d_qn = _dot_nt(dqp, wq_ref[...])
        dkpe = dk_ref[0][:, 128:256]
        for hh in range(1, N_HEADS):
            dkpe = dkpe + dk_ref[hh][:, 128:256]
        dzk_ref[...] = _mx(_rope_bwd(dkpe, c[:, 128:256], s[:, 128:256]))
        dkv = _mx(jnp.concatenate([dk_ref[hh][:, 0:128] for hh in range(N_HEADS)]
                                  + [dv_ref[hh] for hh in range(N_HEADS)], axis=1))
        kvn, rkv = _rms(ckv, gkv_ref[...])
        _acc(dwkv_ref, _dot_tn(_mx(kvn), dkv))
        d_kvn = _dot_nt(dkv, wkv_ref[...])
        d_cq, dgq_t = _rms_bwd(cq, rq, gq_ref[...], d_qn)
        _acc(dgq_ref, _colsum(dgq_t))
        d_ckv, dgkv_t = _rms_bwd(ckv, rkv, gkv_ref[...], d_kvn)
        _acc(dgkv_ref, _colsum(dgkv_t))
        dzc_ref[...] = _mx(jnp.concatenate([d_cq, d_ckv], axis=1))

    fixed = lambda a, b: pl.BlockSpec((a, b), lambda i: (0, 0))
    heads = lambda wd: pl.BlockSpec((N_HEADS, tm, wd), lambda i: (0, i, 0))
    return pl.pallas_call(
        body, name=name, grid=(t // tm,),
        in_specs=[heads(QK_PAD), heads(QK_PAD), heads(V_DIM), pl.BlockSpec((tm, Z_C), lambda i: (i, Z_MIX // Z_C)),
                  pl.BlockSpec((tm, 256), lambda i: (i, 0)), pl.BlockSpec((tm, 256), lambda i: (i, 0)),
                  fixed(384, 1024), fixed(256, 1024), fixed(1, 384), fixed(1, 256)],
        out_specs=[pl.BlockSpec((tm, Z_C), lambda i: (i, 0)), pl.BlockSpec((tm, Z_KR), lambda i: (i, 0)),
                   fixed(384, 1024), fixed(256, 1024), fixed(1, 384), fixed(1, 256)],
        out_shape=[jax.ShapeDtypeStruct((t, Z_C), MXU_DTYPE), jax.ShapeDtypeStruct((t, Z_KR), MXU_DTYPE),
                   jax.ShapeDtypeStruct((384, 1024), F32), jax.ShapeDtypeStruct((256, 1024), F32),
                   jax.ShapeDtypeStruct((1, 384), F32), jax.ShapeDtypeStruct((1, 256), F32)],
        compiler_params=_cp())(dq, dk, dv, z, rc, rs, w_uq, w_ukv, gq, gkv)


def _loop_in_long_trips(n, body):
    def two(t, carry):
        return body(2 * t + 1, body(2 * t, carry))

    def four(t, carry):
        return two(2 * t + 1, two(2 * t, carry))

    lax.fori_loop(0, n // 4, four, 0)
    lax.fori_loop(2 * (n // 4), n // 2, two, 0)
    lax.fori_loop(2 * (n // 2), n, body, 0)


def _init_mask_bias(bias_ref):
    _, tq, tk = bias_ref.shape
    r = lax.broadcasted_iota(jnp.int32, (tq, tk), 0) // 64
    c = lax.broadcasted_iota(jnp.int32, (tq, tk), 1) // 64
    bias_ref[0] = jnp.zeros((tq, tk), F32)
    for d in range(tq // tk):
        bias_ref[1 + d] = jnp.where(c + d * (tk // 64) <= r, 0.0, NEG_INF)


def _gate_block(tq):
    return pl.BlockSpec((tq, 128), lambda h, i: (i, (Z_MIX + Z_C + Z_KR) // 128 + h))


def _attn_fwd(qh, kh, vh, z, name, tq=1024, tk=512):
    t = qh.shape[1]
    tq = min(tq, t)
    tk = min(tk, tq)
    ratio = tq // tk

    def body(q_ref, g_ref, k_hbm, v_hbm, o_ref, yc_ref, lse_ref, k_v, v_v, m_s, acc_s, s_a, s_b, mx_a, mx_b, bias_s,
             sem):
        h, i = pl.program_id(0), pl.program_id(1)

        @pl.when(i == 0)
        def _():
            ck = pltpu.make_async_copy(k_hbm.at[h], k_v, sem.at[0])
            cv = pltpu.make_async_copy(v_hbm.at[h], v_v.at[:, 0:V_DIM], sem.at[1])
            ck.start()
            cv.start()
            v_v[:, V_DIM:2 * V_DIM] = jnp.ones((t, V_DIM), MXU_DTYPE)
            _init_mask_bias(bias_s)
            ck.wait()
            cv.wait()

        q = q_ref[...]
        m_s[...] = jnp.full(m_s.shape, NEG_INF, F32)
        acc_s[...] = jnp.zeros(acc_s.shape, F32)

        last = ratio * (i + 1) - 1

        def keys(j):
            return pl.ds(pl.multiple_of(j * tk, tk), tk)

        def scores(s_ref, mx_ref, j, biased):
            s = _dot_nt(q, k_v[keys(j), :]) * (SCALE * LOG2E)
            if biased:
                s = s + bias_s[jnp.maximum(j - ratio * i + 1, 0)]
            s_ref[...] = s
            mx_ref[...] = jnp.broadcast_to(jnp.max(s, axis=-1, keepdims=True), mx_ref.shape)

        def softmax_pv(s_ref, mx_ref, j):
            m_old = m_s[...]
            m_new = jnp.maximum(m_old, mx_ref[...])
            p = jnp.exp2(s_ref[...] - jnp.tile(m_new, (1, tk // 128)))
            alpha = jnp.exp2(m_old - m_new)
            m_s[...] = m_new
            acc_s[...] = jnp.tile(alpha, (1, 2)) * acc_s[...] + _dot(_mx(p), v_v[keys(j), :])

        scores(s_a, mx_a, 0, True)

        def pair(pp, carry, biased):
            scores(s_b, mx_b, 2 * pp + 1, biased)
            softmax_pv(s_a, mx_a, 2 * pp)
            scores(s_a, mx_a, jnp.minimum(2 * pp + 2, last), biased)
            softmax_pv(s_b, mx_b, 2 * pp + 1)
            return carry

        n_pairs = (last + 1) // 2
        n_plain = jnp.maximum(ratio * i // 2 - 1, 0)
        _loop_in_long_trips(n_plain, lambda pp, carry: pair(pp, carry, False))
        lax.fori_loop(n_plain, n_pairs, lambda pp, carry: pair(pp, carry, True), 0)
        if ratio % 2 == 1:
            @pl.when(last % 2 == 0)
            def _():
                softmax_pv(s_a, mx_a, last)

        l = acc_s[:, V_DIM:2 * V_DIM]
        o = acc_s[:, 0:V_DIM] / l
        o_ref[...] = o
        yc_ref[...] = _mx(o * _silu_and_grad(g_ref[...])[0])
        lse_ref[...] = m_s[...] + jnp.log2(l)

    return pl.pallas_call(
        body, name=name, grid=(N_HEADS, t // tq),
        in_specs=[pl.BlockSpec((None, tq, QK_PAD), lambda h, i: (h, i, 0)), _gate_block(tq), ANY, ANY],
        out_specs=[pl.BlockSpec((tq, 128), lambda h, i: (i, h)), pl.BlockSpec((tq, 128), lambda h, i: (i, h)),
                   pl.BlockSpec((None, tq, 128), lambda h, i: (h, i, 0))],
        out_shape=[jax.ShapeDtypeStruct((t, N_HEADS * V_DIM), F32), jax.ShapeDtypeStruct((t, N_HEADS * V_DIM), MXU_DTYPE),
                   jax.ShapeDtypeStruct((N_HEADS, t, 128), F32)],
        scratch_shapes=[pltpu.VMEM((t, QK_PAD), MXU_DTYPE), pltpu.VMEM((t, 2 * V_DIM), MXU_DTYPE),
                        pltpu.VMEM((tq, 128), F32), pltpu.VMEM((tq, 2 * V_DIM), F32),
                        pltpu.VMEM((tq, tk), F32), pltpu.VMEM((tq, tk), F32), pltpu.VMEM((tq, 128), F32),
                        pltpu.VMEM((tq, 128), F32), pltpu.VMEM((ratio + 1, tq, tk), F32),
                        pltpu.SemaphoreType.DMA((2,))],
        compiler_params=_cp(2))(qh, z, kh, vh)


def _attn_bwd(qh, kh, vh, o, lse, dycat, z, name, tq=512):
    t = qh.shape[1]
    tq = min(tq, t)
    nq = t // tq

    def body(q_ref, o_ref, lse_ref, dy_ref, g_ref, k_hbm, v_hbm, dq_ref, dgate_ref, dk_hbm, dv_hbm,
             k_v, v_v, dk_acc, dv_acc, dq_acc, delta_s, s_a, dp_a, s_b, dp_b, bias_s, sem):
        h, i = pl.program_id(0), pl.program_id(1)

        @pl.when(i == 0)
        def _():
            ck = pltpu.make_async_copy(k_hbm.at[h], k_v, sem.at[0])
            cv = pltpu.make_async_copy(v_hbm.at[h], v_v, sem.at[1])
            ck.start()
            cv.start()
            _init_mask_bias(bias_s)
            dk_acc[...] = jnp.zeros(dk_acc.shape, F32)
            dv_acc[...] = jnp.zeros(dv_acc.shape, F32)
            ck.wait()
            cv.wait()

        gate, dy, of = g_ref[...], dy_ref[...], o_ref[...]
        silu, dsilu = _silu_and_grad(gate)
        do = dy * silu
        delta = jnp.sum(do * of, axis=-1, keepdims=True)
        dgate_ref[...] = _mx(dy * of * dsilu)
        dob = _mx(do)
        q = q_ref[...]
        delta_s[...] = jnp.broadcast_to(delta, delta_s.shape)
        dq_acc[...] = jnp.zeros(dq_acc.shape, F32)

        def keys(j):
            return pl.ds(pl.multiple_of(j * tq, tq), tq)

        def scores(s_ref, dp_ref, j):
            s = _dot_nt(q, k_v[keys(j), :]) * (SCALE * LOG2E) + bias_s[(j == i).astype(jnp.int32)]
            s_ref[...] = s - jnp.tile(lse_ref[...], (1, tq // 128))
            dp_ref[...] = _dot_nt(dob, v_v[keys(j), :]) - jnp.tile(delta_s[...], (1, tq // 128))

        def grads(s_ref, dp_ref, j):
            ks = keys(j)
            p = jnp.exp2(s_ref[...])
            ds = p * dp_ref[...] * SCALE
            pb, dsb = _mx(p), _mx(ds)
            dq_acc[...] += _dot(dsb, k_v[ks, :])
            dk_acc[ks, :] += _dot_tn(dsb, q)
            dv_acc[ks, :] += _dot_tn(pb, dob)

        scores(s_a, dp_a, 0)

        def pair(pp, carry):
            scores(s_b, dp_b, 2 * pp + 1)
            grads(s_a, dp_a, 2 * pp)
            scores(s_a, dp_a, jnp.minimum(2 * pp + 2, i))
            grads(s_b, dp_b, 2 * pp + 1)
            return carry

        _loop_in_long_trips((i + 1) // 2, pair)

        @pl.when(i % 2 == 0)
        def _():
            grads(s_a, dp_a, i)

        dq_ref[...] = dq_acc[...]

        @pl.when(i == nq - 1)
        def _():
            ck = pltpu.make_async_copy(dk_acc, dk_hbm.at[h], sem.at[0])
            cv = pltpu.make_async_copy(dv_acc, dv_hbm.at[h], sem.at[1])
            ck.start()
            cv.start()
            ck.wait()
            cv.wait()

    return pl.pallas_call(
        body, name=name, grid=(N_HEADS, nq),
        in_specs=[pl.BlockSpec((None, tq, QK_PAD), lambda h, i: (h, i, 0)),
                  pl.BlockSpec((tq, 128), lambda h, i: (i, h)),
                  pl.BlockSpec((None, tq, 128), lambda h, i: (h, i, 0)),
                  pl.BlockSpec((tq, 128), lambda h, i: (i, N_HEADS + h)), _gate_block(tq), ANY, ANY],
        out_specs=[pl.BlockSpec((None, tq, QK_PAD), lambda h, i: (h, i, 0)),
                   pl.BlockSpec((tq, 128), lambda h, i: (i, h)), ANY, ANY],
        out_shape=[jax.ShapeDtypeStruct((N_HEADS, t, QK_PAD), F32), jax.ShapeDtypeStruct((t, Z_GATE), MXU_DTYPE),
                   jax.ShapeDtypeStruct((N_HEADS, t, QK_PAD), F32), jax.ShapeDtypeStruct((N_HEADS, t, V_DIM), F32)],
        scratch_shapes=[pltpu.VMEM((t, QK_PAD), MXU_DTYPE), pltpu.VMEM((t, V_DIM), MXU_DTYPE),
                        pltpu.VMEM((t, QK_PAD), F32), pltpu.VMEM((t, V_DIM), F32), pltpu.VMEM((tq, QK_PAD), F32),
                        pltpu.VMEM((tq, 128), F32)] + [pltpu.VMEM((tq, tq), F32)] * 4
        + [pltpu.VMEM((2, tq, tq), F32), pltpu.SemaphoreType.DMA((2,))],
        compiler_params=_cp(2))(qh, o, lse, dycat, z, kh, vh)


def _out_proj_fwd(yab, yc, w, x, g, target, name, tm=512):
    t, d = x.shape
    tm = min(tm, t)
    is_last = target is not None

    def body(*refs):
        if is_last:
            yab_ref, yc_ref, w_ref, x_ref, g_ref, t_ref, y_ref, dout_ref, loss_ref = refs
            _zero_when(pl.program_id(0) == 0, loss_ref)
        else:
            yab_ref, yc_ref, w_ref, x_ref, g_ref, y_ref, out_ref = refs
        y = _dot(jnp.concatenate([yab_ref[...], yc_ref[...]], axis=1), w_ref[...])
        y_ref[...] = y
        out = x_ref[...] + _rms(y, g_ref[...])[0]
        if is_last:
            diff = out - t_ref[...]
            dout_ref[...] = diff * (1.0 / d)
            part = jnp.sum(jnp.sum(diff * diff, axis=-1, keepdims=True), axis=0, keepdims=True) * (0.5 / d)
            _acc(loss_ref, jnp.broadcast_to(part, (1, 128)))
        else:
            out_ref[...] = out

    row = lambda wd: pl.BlockSpec((tm, wd), lambda i: (i, 0))
    fixed = lambda a, b: pl.BlockSpec((a, b), lambda i: (0, 0))
    in_specs = [row(512), row(512), fixed(d, d), row(d), fixed(1, d)]
    args = [yab, yc, w, x, g]
    out_specs = [row(d), row(d)]
    out_shape = [jax.ShapeDtypeStruct((t, d), F32), jax.ShapeDtypeStruct((t, d), F32)]
    if is_last:
        in_specs.append(row(d))
        args.append(target)
        out_specs.append(fixed(1, 128))
        out_shape.append(jax.ShapeDtypeStruct((1, 128), F32))
    return pl.pallas_call(body, name=name, grid=(t // tm,), in_specs=in_specs, out_specs=out_specs,
                          out_shape=out_shape, compiler_params=_cp())(*args)


def _halves(tm):
    half = tm // 2 if tm >= 512 else tm
    return [pl.ds(s, half) for s in range(0, tm, half)]


def _out_proj_bwd(dout, y, yab, yc, w, g, name, tm=1024):
    t, d = y.shape
    tm = min(tm, t)

    def body(dout_ref, y_ref, yab_ref, yc_ref, w_ref, g_ref, dycat_ref, dw_ref, dg_ref):
        _zero_when(pl.program_id(0) == 0, dw_ref, dg_ref)
        dybs = []
        for rows in _halves(tm):
            y = y_ref[rows, :]
            r = lax.rsqrt(jnp.mean(y * y, axis=-1, keepdims=True) + EPS)
            dy, dgt = _rms_bwd(y, r, g_ref[...], dout_ref[rows, :])
            _acc(dg_ref, _colsum(dgt))
            dybs.append(_mx(dy))
        for rows, dyb in zip(_halves(tm), dybs):
            _acc(dw_ref, _dot_tn(jnp.concatenate([yab_ref[rows, :], yc_ref[rows, :]], axis=1), dyb))
            dycat_ref[rows, :] = _dot_nt(dyb, w_ref[...])

    row = lambda wd: pl.BlockSpec((tm, wd), lambda i: (i, 0))
    fixed = lambda a, b: pl.BlockSpec((a, b), lambda i: (0, 0), pipeline_mode=pl.Buffered(1))
    return pl.pallas_call(
        body, name=name, grid=(t // tm,),
        in_specs=[row(d), row(d), row(512), row(512), fixed(d, d), fixed(1, d)],
        out_specs=[row(d), fixed(d, d), fixed(1, d)],
        out_shape=[jax.ShapeDtypeStruct((t, d), F32), jax.ShapeDtypeStruct((d, d), F32),
                   jax.ShapeDtypeStruct((1, d), F32)],
        compiler_params=_cp())(dout, y, yab, yc, w, g)


def _mesh_pos():
    return lax.axis_index("x"), lax.axis_index("y"), lax.axis_index("c")


def _remote(src, dst, send_sem, recv_sem, to):
    return pltpu.make_async_remote_copy(src_ref=src, dst_ref=dst, send_sem=send_sem, recv_sem=recv_sem,
                                        device_id=to, device_id_type=MESH)


CHUNK_ROWS = 256


def _pieces(rows):
    return [(s, min(CHUNK_ROWS, rows - s)) for s in range(0, rows, CHUNK_ROWS)]


def _piece_table(shapes):
    return [(a, s, sz) for a, shp in enumerate(shapes) for s, sz in _pieces(shp[-2])]


def _gather_weights(shards):
    n = len(shards)
    table = _piece_table([s.shape for s in shards])
    npc = len(table)

    def body(*refs):
        ins, outs = refs[:n], refs[n:2 * n]
        send_sems, recv_sems, fwd_send, fwd_recv = refs[2 * n:]
        x, y, c = _mesh_pos()
        me, sibling = (x, y, c), (x, y, 1 - c)
        chips = [(1 - x, y), (x, 1 - y), (1 - x, 1 - y)]
        slot = lambda cx, cy, layer: 2 * (2 * cx + cy) + layer
        first = []
        for a in range(n):
            for j, (cx, cy) in enumerate(chips):
                first.append(_remote(ins[a].at[c], outs[a].at[slot(x, y, c)], send_sems.at[a, j], recv_sems.at[a, j],
                                     (cx, cy, c)))
                first[-1].start()
        passed = []
        for j, (cx, cy) in enumerate(chips):
            for a in range(n):
                blk = outs[a].at[slot(cx, cy, c)]
                _remote(blk, blk, send_sems.at[a, j], recv_sems.at[a, j], me).wait_recv()
            for q, (a, s, sz) in enumerate(table):
                rows = outs[a].at[slot(cx, cy, c), pl.ds(s, sz)]
                passed.append(_remote(rows, rows, fwd_send.at[j, q], fwd_recv.at[j, q], sibling))
                passed[-1].start()
        for j, (cx, cy) in enumerate(chips):
            for q, (a, s, sz) in enumerate(table):
                rows = outs[a].at[slot(cx, cy, 1 - c), pl.ds(s, sz)]
                _remote(rows, rows, fwd_send.at[j, q], fwd_recv.at[j, q], me).wait_recv()
        for cp in first + passed:
            cp.wait_send()

    return pl.pallas_call(
        body, name="gather_weights", in_specs=[ANY] * n, out_specs=[ANY] * n,
        out_shape=[jax.ShapeDtypeStruct((8,) + s.shape[1:], s.dtype) for s in shards],
        scratch_shapes=[pltpu.SemaphoreType.DMA((n, 3)), pltpu.SemaphoreType.DMA((n, 3)),
                        pltpu.SemaphoreType.DMA((3, npc)), pltpu.SemaphoreType.DMA((3, npc))])(*shards)


def _pair_exchange(parts, common):
    n = len(parts)
    table = _piece_table([p.shape for p in parts] + [common.shape])
    npc = len(table)

    def body(*refs):
        ins, outs = refs[:n + 1], refs[n + 1:2 * n + 2]
        send_sems, recv_sems = refs[2 * n + 2:]
        x, y, c = _mesh_pos()
        sent = []
        for k in range(4):
            for q, (a, s, sz) in enumerate(table):
                if a == n and k > 0:
                    continue
                src = ins[a].at[2 * k + 1 - c, pl.ds(s, sz)] if a < n else ins[a].at[pl.ds(s, sz)]
                dst = outs[a].at[k, pl.ds(s, sz)] if a < n else outs[a].at[pl.ds(s, sz)]
                sent.append(_remote(src, dst, send_sems.at[k, q], recv_sems.at[k, q], (x, y, 1 - c)))
                sent[-1].start()
        for k in range(4):
            for q, (a, s, sz) in enumerate(table):
                if a == n and k > 0:
                    continue
                dst = outs[a].at[k, pl.ds(s, sz)] if a < n else outs[a].at[pl.ds(s, sz)]
                _remote(dst, dst, send_sems.at[k, q], recv_sems.at[k, q], (x, y, c)).wait_recv()
        for cp in sent:
            cp.wait_send()

    return pl.pallas_call(
        body, name="grad_pair_exchange", in_specs=[ANY] * (n + 1), out_specs=[ANY] * (n + 1),
        out_shape=[jax.ShapeDtypeStruct((4,) + p.shape[1:], p.dtype) for p in parts]
        + [jax.ShapeDtypeStruct(common.shape, common.dtype)],
        scratch_shapes=[pltpu.SemaphoreType.DMA((4, npc)), pltpu.SemaphoreType.DMA((4, npc))])(*parts, common)


def _chip_exchange(parts, common):
    n = len(parts)
    table = _piece_table([p.shape for p in parts] + [common.shape])
    npc = len(table)

    def body(*refs):
        ins, outs = refs[:n + 1], refs[n + 1:2 * n + 2]
        send_sems, recv_sems = refs[2 * n + 2:]
        x, y, c = _mesh_pos()
        mine = 2 * x + y
        chips = [(1 - x, y), (x, 1 - y), (1 - x, 1 - y)]
        src = lambda a, k: ins[a].at[k] if a < n else ins[a]
        sent = []
        for j, (cx, cy) in enumerate(chips):
            for q, (a, s, sz) in enumerate(table):
                sent.append(_remote(src(a, 2 * cx + cy).at[pl.ds(s, sz)], outs[a].at[mine, pl.ds(s, sz)],
                                    send_sems.at[j, q], recv_sems.at[j, q], (cx, cy, c)))
                sent[-1].start()
        for j, (cx, cy) in enumerate(chips):
            for q, (a, s, sz) in enumerate(table):
                dst = outs[a].at[2 * cx + cy, pl.ds(s, sz)]
                _remote(dst, dst, send_sems.at[j, q], recv_sems.at[j, q], (x, y, c)).wait_recv()
        for cp in sent:
            cp.wait_send()

    return pl.pallas_call(
        body, name="grad_chip_exchange", in_specs=[ANY] * (n + 1), out_specs=[ANY] * (n + 1),
        out_shape=[jax.ShapeDtypeStruct(p.shape, p.dtype) for p in parts]
        + [jax.ShapeDtypeStruct((4,) + common.shape, common.dtype)],
        scratch_shapes=[pltpu.SemaphoreType.DMA((3, npc)), pltpu.SemaphoreType.DMA((3, npc))])(*parts, common)


def _sibling_exchange(sums):
    n = len(sums)
    table = _piece_table([s.shape for s in sums])
    npc = len(table)

    def body(*refs):
        ins, outs = refs[:n], refs[n:2 * n]
        send_sems, recv_sems = refs[2 * n:]
        x, y, c = _mesh_pos()
        sent = []
        for q, (a, s, sz) in enumerate(table):
            sent.append(_remote(ins[a].at[pl.ds(s, sz)], outs[a].at[pl.ds(s, sz)], send_sems.at[q], recv_sems.at[q],
                                (x, y, 1 - c)))
            sent[-1].start()
        for q, (a, s, sz) in enumerate(table):
            dst = outs[a].at[pl.ds(s, sz)]
            _remote(dst, dst, send_sems.at[q], recv_sems.at[q], (x, y, c)).wait_recv()
        for cp in sent:
            cp.wait_send()

    return pl.pallas_call(
        body, name="sibling_exchange", in_specs=[ANY] * n, out_specs=[ANY] * n,
        out_shape=[jax.ShapeDtypeStruct(s.shape, s.dtype) for s in sums],
        scratch_shapes=[pltpu.SemaphoreType.DMA((npc,)), pltpu.SemaphoreType.DMA((npc,))])(*sums)


def _pair_sum(parts, got, name, tr):
    _, r, c = got.shape
    tr = min(tr, r)

    def body(p_ref, g_ref, o_ref, w_ref):
        total = p_ref[...] + g_ref[...]
        o_ref[...] = total
        w_ref[...] = total.astype(WIRE_DTYPE)

    blk = pl.BlockSpec((None, tr, c), lambda k, i: (k, i, 0))
    mine = pl.BlockSpec((None, tr, c), lambda k, i: (2 * k + lax.axis_index("c"), i, 0))
    return pl.pallas_call(
        body, name=name, grid=(4, r // tr), in_specs=[mine, blk], out_specs=[blk, blk],
        out_shape=[jax.ShapeDtypeStruct(got.shape, F32), jax.ShapeDtypeStruct(got.shape, WIRE_DTYPE)],
        compiler_params=_cp(2))(parts, got)


def _sum_chips(pair_sums, recv, name, tr):
    _, r, c = pair_sums.shape
    tr = min(tr, r)

    def body(own_ref, r_ref, o_ref):
        chip = 2 * lax.axis_index("x") + lax.axis_index("y")
        own_blk = own_ref[...]
        acc = jnp.where(chip == 0, own_blk, r_ref[0].astype(F32))
        for k in range(1, 4):
            acc = acc + jnp.where(chip == k, own_blk, r_ref[k].astype(F32))
        o_ref[...] = acc

    return pl.pallas_call(
        body, name=name, grid=(r // tr,),
        in_specs=[pl.BlockSpec((None, tr, c), lambda i: (2 * lax.axis_index("x") + lax.axis_index("y"), i, 0)),
                  pl.BlockSpec((4, tr, c), lambda i: (0, i, 0))],
        out_specs=pl.BlockSpec((tr, c), lambda i: (i, 0)),
        out_shape=jax.ShapeDtypeStruct((r, c), F32), compiler_params=_cp())(pair_sums, recv)


def _sum_leading(parts, name, tr):
    nlead, r, c = parts.shape
    tr = min(tr, r)

    def body(p_ref, o_ref):
        acc = p_ref[0]
        for j in range(1, nlead):
            acc = acc + p_ref[j]
        o_ref[...] = acc

    return pl.pallas_call(
        body, name=name, grid=(r // tr,),
        in_specs=[pl.BlockSpec((nlead, tr, c), lambda i: (0, i, 0))], out_specs=pl.BlockSpec((tr, c), lambda i: (i, 0)),
        out_shape=jax.ShapeDtypeStruct((r, c), parts.dtype), compiler_params=_cp())(parts)


def _adamw_update(w_ref, g_ref, m_ref, v_ref, d_ref, nm_ref, nv_ref):
    gg = g_ref[...]
    nm = ADAM_B1 * m_ref[...] + (1.0 - ADAM_B1) * gg
    nv = ADAM_B2 * v_ref[...] + (1.0 - ADAM_B2) * jnp.square(gg)
    m_hat = nm / (1.0 - ADAM_B1 ** ADAM_STEP)
    v_hat = nv / (1.0 - ADAM_B2 ** ADAM_STEP)
    d_ref[...] = -ADAM_LR * (m_hat / (jnp.sqrt(v_hat) + ADAM_EPS) + ADAM_WD * w_ref[...])
    nm_ref[...] = nm
    nv_ref[...] = nv


def _adamw(w, g, m, v, name, tr):
    r, c = w.shape
    tr = min(tr, r)

    def body(w_ref, g_ref, m_ref, v_ref, d_ref, nm_ref, nv_ref):
        _adamw_update(w_ref, g_ref, m_ref, v_ref, d_ref, nm_ref, nv_ref)

    blk = pl.BlockSpec((tr, c), lambda i: (i, 0))
    return pl.pallas_call(
        body, name=name, grid=(r // tr,), in_specs=[blk] * 4, out_specs=[blk] * 3,
        out_shape=[jax.ShapeDtypeStruct((r, c), F32)] * 3, compiler_params=_cp())(w, g, m, v)


def _adamw_many(ws, gs, ms, vs, name):
    n = len(ws)

    def body(*refs):
        ins, outs = refs[:4 * n], refs[4 * n:]
        for k in range(n):
            _adamw_update(ins[k], ins[n + k], ins[2 * n + k], ins[3 * n + k], outs[k], outs[n + k], outs[2 * n + k])

    vmem = pl.BlockSpec(memory_space=pltpu.VMEM)
    res = pl.pallas_call(
        body, name=name, in_specs=[vmem] * (4 * n), out_specs=[vmem] * (3 * n),
        out_shape=[jax.ShapeDtypeStruct(a.shape, F32) for a in ws] * 3)(*ws, *gs, *ms, *vs)
    return res[:n], res[n:2 * n], res[2 * n:]


def _rope_tables(positions):
    inv_freq = ROPE_BASE ** (-jnp.arange(0, 64, 2, dtype=F32) / 64)
    ang = positions.astype(F32)[:, None] * inv_freq
    cos, sin = jnp.cos(ang), jnp.sin(ang)
    t = positions.shape[0]
    rc = jnp.concatenate([jnp.ones((t, 128), F32), cos, cos, jnp.ones((t, 64), F32)], axis=1)
    rs = jnp.concatenate([jnp.zeros((t, 128), F32), sin, sin, jnp.zeros((t, 64), F32)], axis=1)
    return rc, rs


def _layer_params(l, w_in, w_uq, w_ukv, w_out, small):
    p = {}
    p["w_in"] = jnp.concatenate([w_in[l][:, :1984], jnp.zeros((1024, 64), w_in.dtype), w_in[l][:, 1984:]], axis=1)
    p["w_uq"] = jnp.pad(w_uq[l].reshape(384, 4, 192), ((0, 0), (0, 0), (0, 64))).reshape(384, 1024)
    p["w_ukv"] = w_ukv[l].reshape(256, 4, 2, 128).transpose(0, 2, 1, 3).reshape(256, 1024)
    p["w_out"] = w_out[l]
    p["pre_g"] = small["pre_norm_g"][l][None]
    p["post_g"] = small["post_norm_g"][l][None]
    p["sgu_w"] = small["sgu_w"][l].reshape(512, 128)
    p["sgu_wt"] = small["sgu_w"][l].transpose(0, 2, 1).reshape(512, 128)
    p["sgu_bias"] = jnp.repeat(small["sgu_b"][l].T, 64, axis=1)
    p["ln_g"] = small["sgu_ln_g"][l][None]
    p["ln_b"] = small["sgu_ln_b"][l][None]
    p["pool_wbd"] = _mx(jax.scipy.linalg.block_diag(*[small["pool_w"][l][gi] for gi in range(4)]))
    p["pool_scale"] = small["pool_scale"][l][None]
    p["gq"] = small["q_norm_g"][l][None]
    p["gkv"] = small["kv_norm_g"][l][None]
    return p


def _layer_fwd(l, x, p, rc, rs, target):
    z, h = _in_proj_fwd(x, p["pre_g"], p["w_in"], f"in_proj_fwd_{l}")
    yab = _mix_fwd(z, p["sgu_w"], p["sgu_bias"], p["ln_g"], p["ln_b"], p["pool_wbd"], p["pool_scale"], f"mix_fwd_{l}")
    qh, kh, vh = _qkv_fwd(z, rc, rs, p["w_uq"], p["w_ukv"], p["gq"], p["gkv"], f"qkv_fwd_{l}")
    o, yc, lse = _attn_fwd(qh, kh, vh, z, f"attn_fwd_{l}")
    outs = _out_proj_fwd(yab, yc, p["w_out"], x, p["post_g"], target, f"out_proj_fwd_{l}")
    saved = dict(x=x, z=z, h=h, yab=yab, qh=qh, kh=kh, vh=vh, o=o, yc=yc, lse=lse, y=outs[0])
    return saved, outs[1:]


def _layer_bwd(l, dout, sv, p, rc, rs):
    dycat, dw_out, dpost = _out_proj_bwd(dout, sv["y"], sv["yab"], sv["yc"], p["w_out"], p["post_g"], f"out_proj_bwd_{l}")
    dq, dgate, dk, dv = _attn_bwd(sv["qh"], sv["kh"], sv["vh"], sv["o"], sv["lse"], dycat, sv["z"], f"attn_bwd_{l}")
    dzc, dzk, dwq, dwkv, dgq, dgkv = _qkv_bwd(dq, dk, dv, sv["z"], rc, rs, p["w_uq"], p["w_ukv"], p["gq"], p["gkv"],
                                              f"qkv_bwd_{l}")
    dzm, dsw, dsb, dlng, dlnb, dpw, dps = _mix_bwd(sv["z"], dycat, p["sgu_w"], p["sgu_wt"], p["sgu_bias"], p["ln_g"],
                                                   p["ln_b"], p["pool_wbd"], p["pool_scale"], f"mix_bwd_{l}")
    dx, dw_in, dpre = _in_proj_bwd(dzm, dzc, dzk, dgate, sv["h"], sv["x"], dout, p["w_in"], p["pre_g"], f"in_proj_bwd_{l}")
    grads = {
        "pre_norm_g": dpre[0], "post_norm_g": dpost[0],
        "w_in": jnp.concatenate([dw_in[:, :1984], dw_in[:, 2048:]], axis=1),
        "sgu_w": dsw.reshape(4, 128, 128), "sgu_b": dsb[:, :4].T, "sgu_ln_g": dlng[0], "sgu_ln_b": dlnb[0],
        "pool_w": jnp.stack([dpw[64 * gi:64 * gi + 64, 64 * gi:64 * gi + 64] for gi in range(4)]),
        "pool_scale": dps[0], "q_norm_g": dgq[0],
        "w_uq": dwq.reshape(384, 4, 256)[:, :, :192].reshape(384, 768), "kv_norm_g": dgkv[0],
        "w_ukv": dwkv.reshape(256, 2, 4, 128).transpose(0, 2, 1, 3).reshape(256, 1024), "w_out": dw_out,
    }
    return dx, grads


SMALL_NAMES = ["pre_norm_g", "post_norm_g", "sgu_w", "sgu_b", "sgu_ln_g", "sgu_ln_b", "pool_w", "pool_scale",
               "q_norm_g", "kv_norm_g"]
BIG_NAMES = ["w_in", "w_uq", "w_ukv", "w_out"]
WEIGHT_NAMES = ["pre_norm_g", "post_norm_g", "w_in", "sgu_w", "sgu_b", "sgu_ln_g", "sgu_ln_b", "pool_w", "pool_scale",
                "q_norm_g", "w_uq", "kv_norm_g", "w_ukv", "w_out"]


def _local_step(x, positions, target, w_in, w_uq, w_ukv, w_out, small):
    rc, rs = _rope_tables(positions)
    params = [_layer_params(l, w_in, w_uq, w_ukv, w_out, small) for l in range(DEPTH)]
    saved = []
    for l in range(DEPTH):
        sv, outs = _layer_fwd(l, x, params[l], rc, rs, target if l == DEPTH - 1 else None)
        saved.append(sv)
        if l < DEPTH - 1:
            x = outs[0]
    dout, loss = outs
    grads = [None] * DEPTH
    for l in reversed(range(DEPTH)):
        dout, grads[l] = _layer_bwd(l, dout, saved[l], params[l], rc, rs)
    return loss[0, 0], dout, {k: jnp.stack([grads[l][k] for l in range(DEPTH)]) for k in WEIGHT_NAMES}


def _pack_small(tree, extra=None):
    pieces = [tree[k].reshape(-1) for k in SMALL_NAMES]
    pieces.append(jnp.zeros((1,), F32) if extra is None else extra.reshape(1))
    flat = jnp.concatenate(pieces)
    rows = -(-flat.shape[0] // 1024) * 8
    return jnp.pad(flat, (0, rows * 128 - flat.shape[0])).reshape(rows, 128)


def _unpack_small(packed, like):
    flat = packed.reshape(-1)
    out, off = {}, 0
    for k in SMALL_NAMES:
        size = like[k].size
        out[k] = flat[off:off + size].reshape(like[k].shape)
        off += size
    return out, flat[off]


def kernel(x, positions, pre_norm_g, post_norm_g, w_in, sgu_w, sgu_b, sgu_ln_g, sgu_ln_b, pool_w, pool_scale, q_norm_g, w_uq, kv_norm_g, w_ukv, w_out, loss_target, m_pre_norm_g, m_post_norm_g, m_w_in, m_sgu_w, m_sgu_b, m_sgu_ln_g, m_sgu_ln_b, m_pool_w, m_pool_scale, m_q_norm_g, m_w_uq, m_kv_norm_g, m_w_ukv, m_w_out, v_pre_norm_g, v_post_norm_g, v_w_in, v_sgu_w, v_sgu_b, v_sgu_ln_g, v_sgu_ln_b, v_pool_w, v_pool_scale, v_q_norm_g, v_w_uq, v_kv_norm_g, v_w_ukv, v_w_out):
    w = dict(pre_norm_g=pre_norm_g, post_norm_g=post_norm_g, w_in=w_in, sgu_w=sgu_w, sgu_b=sgu_b, sgu_ln_g=sgu_ln_g,
             sgu_ln_b=sgu_ln_b, pool_w=pool_w, pool_scale=pool_scale, q_norm_g=q_norm_g, w_uq=w_uq, kv_norm_g=kv_norm_g,
             w_ukv=w_ukv, w_out=w_out)
    m = dict(pre_norm_g=m_pre_norm_g, post_norm_g=m_post_norm_g, w_in=m_w_in, sgu_w=m_sgu_w, sgu_b=m_sgu_b,
             sgu_ln_g=m_sgu_ln_g, sgu_ln_b=m_sgu_ln_b, pool_w=m_pool_w, pool_scale=m_pool_scale, q_norm_g=m_q_norm_g,
             w_uq=m_w_uq, kv_norm_g=m_kv_norm_g, w_ukv=m_w_ukv, w_out=m_w_out)
    v = dict(pre_norm_g=v_pre_norm_g, post_norm_g=v_post_norm_g, w_in=v_w_in, sgu_w=v_sgu_w, sgu_b=v_sgu_b,
             sgu_ln_g=v_sgu_ln_g, sgu_ln_b=v_sgu_ln_b, pool_w=v_pool_w, pool_scale=v_pool_scale, q_norm_g=v_q_norm_g,
             w_uq=v_w_uq, kv_norm_g=v_kv_norm_g, w_ukv=v_w_ukv, w_out=v_w_out)

    core = lax.axis_index("c")
    chip = 2 * lax.axis_index("x") + lax.axis_index("y")
    shards = [_mx(w[k]) for k in BIG_NAMES]
    gathered = _gather_weights(shards)
    g_in, g_uq, g_ukv, g_out = [lax.dynamic_update_slice(g, s, (2 * chip, 0, 0)) for g, s in zip(gathered, shards)]
    cols = lambda g: g.reshape((4, 2) + g.shape[1:]).transpose(1, 2, 0, 3).reshape(2, g.shape[1], 4 * g.shape[2])
    full_out = g_out.reshape(4, 2, 256, 1024).transpose(1, 0, 2, 3).reshape(2, 1024, 1024)
    loss, dx, grads = _local_step(x[0], positions[0], loss_target[0], cols(g_in), cols(g_uq), cols(g_ukv), full_out, w)

    split_cols = lambda g: g.reshape(2, g.shape[1], 4, g.shape[2] // 4).transpose(2, 0, 1, 3).reshape(8, g.shape[1], g.shape[2] // 4)
    parts = [split_cols(grads["w_in"]), split_cols(grads["w_uq"]), split_cols(grads["w_ukv"]),
             grads["w_out"].reshape(2, 4, 256, 1024).transpose(1, 0, 2, 3).reshape(8, 256, 1024)]
    common = _pack_small(grads, loss)
    got = _pair_exchange(parts, common)
    pair_sums = [_pair_sum(parts[a], got[a], f"pair_sum_{BIG_NAMES[a]}", 128) for a in range(4)]
    chip_common = _sum_leading(jnp.stack([common, got[4]]), "pair_sum_small", common.shape[0])
    received = _chip_exchange([ps[1] for ps in pair_sums], chip_common)
    sums = [_sum_chips(pair_sums[a][0], received[a], f"sum_{BIG_NAMES[a]}", 128) for a in range(4)]
    all_common = lax.dynamic_update_slice(received[4], chip_common[None], (chip, 0, 0))
    small_sum, loss = _unpack_small(_sum_leading(all_common, "sum_small", all_common.shape[1]), w)
    others = _sibling_exchange(sums)
    total = dict(small_sum)
    for a, k in enumerate(BIG_NAMES):
        total[k] = jnp.where(core == 0, jnp.stack([sums[a], others[a]]), jnp.stack([others[a], sums[a]]))

    rows2d = lambda a: a.reshape(-1, a.shape[-1])
    small_out = _adamw_many(*[[rows2d(tree[k]) for k in SMALL_NAMES] for tree in (w, total, m, v)], "adamw_small")
    delta, new_m, new_v = ({k: r.reshape(w[k].shape) for k, r in zip(SMALL_NAMES, res)} for res in small_out)
    for k in BIG_NAMES:
        shape = w[k].shape
        flat = lambda a: a.reshape(shape[0] * shape[1], shape[2])
        res = _adamw(flat(w[k]), flat(total[k]), flat(m[k]), flat(v[k]), f"adamw_{k}", 256)
        delta[k], new_m[k], new_v[k] = (r.reshape(shape) for r in res)

    return (loss, dx[None], *[total[k] for k in WEIGHT_NAMES], *[delta[k] for k in WEIGHT_NAMES],
            *[new_m[k] for k in WEIGHT_NAMES], *[new_v[k] for k in WEIGHT_NAMES])
```

```python
import jax
import jax.numpy as jnp
from jax import lax
from jax.experimental import pallas as pl
from jax.experimental.pallas import tpu as pltpu

F32 = jnp.float32
MXU_DTYPE = jnp.bfloat16
WIRE_DTYPE = jnp.bfloat16
EPS = 1e-6
NEG_INF = -1e30
DEPTH = 2
N_HEADS = 4
QK_PAD = 256
V_DIM = 128
SCALE = 192 ** -0.5
LOG2E = 1.4426950408889634
ROPE_BASE = 10000.0
ADAM_LR, ADAM_B1, ADAM_B2, ADAM_EPS, ADAM_WD, ADAM_STEP = 0.001, 0.9, 0.999, 1e-08, 0.01, 10
VMEM_LIMIT_BYTES = 56 * 1024 * 1024
MESH = pl.DeviceIdType.MESH
ANY = pl.BlockSpec(memory_space=pl.ANY)

Z_MIX, Z_C, Z_KR, Z_GATE = 1280, 640, 128, 512
Z_W = Z_MIX + Z_C + Z_KR + Z_GATE


def _cp(n_axes=1):
    return pltpu.CompilerParams(dimension_semantics=("arbitrary",) * n_axes, vmem_limit_bytes=VMEM_LIMIT_BYTES)


def _dot(a, b):
    return lax.dot_general(a, b, (((1,), (0,)), ((), ())), preferred_element_type=F32)


def _dot_nt(a, b):
    return lax.dot_general(a, b, (((1,), (1,)), ((), ())), preferred_element_type=F32)


def _dot_tn(a, b):
    return lax.dot_general(a, b, (((0,), (0,)), ((), ())), preferred_element_type=F32)


def _mx(a):
    return a.astype(MXU_DTYPE)


def _silu_and_grad(g):
    sg = jax.nn.sigmoid(g)
    return g * sg, sg * (1.0 + g * (1.0 - sg))


def _rms(x, g):
    r = lax.rsqrt(jnp.mean(x * x, axis=-1, keepdims=True) + EPS)
    return x * r * g, r


def _rms_bwd(x, r, g, dy):
    xhat = x * r
    dyg = dy * g
    dx = r * (dyg - xhat * jnp.mean(dyg * xhat, axis=-1, keepdims=True))
    return dx, dy * xhat


def _zero_when(first, *refs):
    @pl.when(first)
    def _():
        for ref in refs:
            ref[...] = jnp.zeros(ref.shape, ref.dtype)


def _acc(ref, val):
    ref[...] += val


def _colsum(a):
    return jnp.sum(a, axis=0, keepdims=True)


def _in_proj_fwd(x, g, w, name, tm=512):
    t, d = x.shape
    n = w.shape[1]
    tm = min(tm, t)

    def body(x_ref, g_ref, w_ref, z_ref, h_ref):
        h, _ = _rms(x_ref[...], g_ref[...])
        h = _mx(h)
        h_ref[...] = h
        z_ref[...] = _dot(h, w_ref[...])

    return pl.pallas_call(
        body, name=name, grid=(t // tm,),
        in_specs=[pl.BlockSpec((tm, d), lambda i: (i, 0)), pl.BlockSpec((1, d), lambda i: (0, 0)),
                  pl.BlockSpec((d, n), lambda i: (0, 0))],
        out_specs=[pl.BlockSpec((tm, n), lambda i: (i, 0)), pl.BlockSpec((tm, d), lambda i: (i, 0))],
        out_shape=[jax.ShapeDtypeStruct((t, n), F32), jax.ShapeDtypeStruct((t, d), MXU_DTYPE)],
        compiler_params=_cp())(x, g, w)


def _in_proj_bwd(dz_mix, dz_c, dz_kr, dz_gate, h, x, d_res, w, g, name, tm=512):
    t, d = x.shape
    n = w.shape[1]
    tm = min(tm, t)

    def body(dm_ref, dc_ref, dk_ref, dg_ref, h_ref, x_ref, dres_ref, w_ref, g_ref, dx_ref, dw_ref, dgn_ref):
        first = pl.program_id(0) == 0
        dz = jnp.concatenate([dm_ref[...], dc_ref[...], dk_ref[...], dg_ref[...]], axis=1)

        _zero_when(first, dw_ref, dgn_ref)
        hb = h_ref[...]
        for c0 in range(0, n, 512):
            dw_ref[:, c0:c0 + 512] += _dot_tn(hb, dz[:, c0:c0 + 512])
        dh = _dot_nt(dz, w_ref[...])
        xf = x_ref[...]
        r = lax.rsqrt(jnp.mean(xf * xf, axis=-1, keepdims=True) + EPS)
        dx, dgt = _rms_bwd(xf, r, g_ref[...], dh)
        dx_ref[...] = dx + dres_ref[...]
        _acc(dgn_ref, _colsum(dgt))

    row = lambda wd: pl.BlockSpec((tm, wd), lambda i: (i, 0))
    fixed = lambda a, b: pl.BlockSpec((a, b), lambda i: (0, 0), pipeline_mode=pl.Buffered(1))
    return pl.pallas_call(
        body, name=name, grid=(t // tm,),
        in_specs=[row(Z_MIX), row(Z_C), row(Z_KR), row(Z_GATE), row(d), row(d), row(d), fixed(d, n), fixed(1, d)],
        out_specs=[row(d), fixed(d, n), fixed(1, d)],
        out_shape=[jax.ShapeDtypeStruct((t, d), F32), jax.ShapeDtypeStruct((d, n), F32),
                   jax.ShapeDtypeStruct((1, d), F32)],
        compiler_params=_cp())(dz_mix, dz_c, dz_kr, dz_gate, h, x, d_res, w, g)


def _lane_group(shape):
    return lax.broadcasted_iota(jnp.int32, shape, 1) // 64


def _select_group(vals):
    grp = _lane_group(vals[0].shape)
    out = vals[3]
    for gi in (2, 1, 0):
        out = jnp.where(grp == gi, vals[gi], out)
    return out


def _sgu_mask(transposed):
    r = (lax.broadcasted_iota(jnp.int32, (512, 128), 0) % 128) // 64
    c = lax.broadcasted_iota(jnp.int32, (512, 128), 1) // 64
    return (r <= c) if transposed else (c <= r)


def _sgu_apply(wstack, vb, nblk):
    outs = []
    for n in range(nblk):
        r = _dot(wstack, vb[n * 128:(n + 1) * 128, :])
        outs.append(_select_group([r[hh * 128:(hh + 1) * 128, :] for hh in range(4)]))
    return jnp.concatenate(outs, axis=0)


def _layer_norm(v, g, b):
    mu = jnp.mean(v, axis=-1, keepdims=True)
    vc = v - mu
    rstd = lax.rsqrt(jnp.mean(vc * vc, axis=-1, keepdims=True) + EPS)
    vhat = vc * rstd
    return vhat * g + b, vhat, rstd


def _pool_counts(t0, n):
    t = t0 + lax.broadcasted_iota(jnp.int32, (n, 256), 0)
    w = _select_group([jnp.full((n, 256), wv, jnp.int32) for wv in (2, 4, 8, 16)])
    return jnp.minimum(t + 1, w).astype(F32)


def _pooled(p, halo, t0):
    tm = p.shape[0]
    ext = jnp.concatenate([halo, p], axis=0)
    s2 = ext + pltpu.roll(ext, 1, 0)
    s4 = s2 + pltpu.roll(s2, 2, 0)
    s8 = s4 + pltpu.roll(s4, 4, 0)
    s16 = s8 + pltpu.roll(s8, 8, 0)
    sel = _select_group([s2, s4, s8, s16])[16:, :]
    return sel / _pool_counts(t0, tm) - p


def _pooled_bwd(dpool, dpool_halo, t0):
    tm = dpool.shape[0]
    n = tm + 16
    ext = jnp.concatenate([dpool, dpool_halo], axis=0) / _pool_counts(t0, n)
    f2 = ext + pltpu.roll(ext, n - 1, 0)
    f4 = f2 + pltpu.roll(f2, n - 2, 0)
    f8 = f4 + pltpu.roll(f4, n - 4, 0)
    f16 = f8 + pltpu.roll(f8, n - 8, 0)
    return _select_group([f2, f4, f8, f16])[:tm, :] - dpool


def _mix_specs(t, tm):
    nt16 = t // 16
    zrow = pl.BlockSpec((tm, Z_MIX), lambda i: (i, 0))
    prev_halo = pl.BlockSpec((16, 256), lambda i: (jnp.maximum(i * (tm // 16) - 1, 0), 3))
    fixed = lambda a, b: pl.BlockSpec((a, b), lambda i: (0, 0))
    params = [fixed(512, 128), fixed(128, 256), fixed(1, 256), fixed(1, 256), fixed(256, 256), fixed(1, 256)]
    return nt16, zrow, prev_halo, fixed, params


def _mix_fwd(z, sgu_w, sgu_bias, ln_g, ln_b, pool_wbd, pool_scale, name, tm=512):
    t = z.shape[0]
    tm = min(tm, t)
    _, zrow, prev_halo, _, params = _mix_specs(t, tm)

    def body(z_ref, halo_ref, w_ref, bias_ref, lng_ref, lnb_ref, pw_ref, ps_ref, y_ref):
        i = pl.program_id(0)
        u, v, gate = z_ref[:, 0:256], z_ref[:, 256:512], z_ref[:, 512:768]
        p, pgate = z_ref[:, 768:1024], z_ref[:, 1024:1280]
        vn, _, _ = _layer_norm(v, lng_ref[...], lnb_ref[...])
        wm = _mx(jnp.where(_sgu_mask(False), w_ref[...], 0.0))
        mixed = _sgu_apply(wm, _mx(vn), tm // 128) + jnp.tile(bias_ref[...], (tm // 128, 1))
        ya = u * mixed * _silu_and_grad(gate)[0]
        halo = jnp.where(i > 0, halo_ref[...], 0.0)
        pooled = _pooled(p, halo, i * tm)
        yb = _dot(_mx(pooled), pw_ref[...]) * ps_ref[...] * _silu_and_grad(pgate)[0]
        y_ref[...] = _mx(jnp.concatenate([ya, yb], axis=1))

    return pl.pallas_call(
        body, name=name, grid=(t // tm,),
        in_specs=[zrow, prev_halo] + params,
        out_specs=pl.BlockSpec((tm, 512), lambda i: (i, 0)),
        out_shape=jax.ShapeDtypeStruct((t, 512), MXU_DTYPE),
        compiler_params=_cp())(z, z, sgu_w, sgu_bias, ln_g, ln_b, pool_wbd, pool_scale)


def _mix_bwd(z, dycat, sgu_w, sgu_wt, sgu_bias, ln_g, ln_b, pool_wbd, pool_scale, name, tm=512):
    t = z.shape[0]
    tm = min(tm, t)
    nt16, zrow, prev_halo, fixed, params = _mix_specs(t, tm)
    nblk = tm // 128
    last = t // tm - 1

    def body(z_ref, halo_ref, zn_ref, dy_ref, dyn_ref, w_ref, wt_ref, bias_ref, lng_ref, lnb_ref, pw_ref, ps_ref,
             dz_ref, dw_ref, db_ref, dlng_ref, dlnb_ref, dpw_ref, dps_ref):
        i = pl.program_id(0)
        _zero_when(i == 0, dw_ref, db_ref, dlng_ref, dlnb_ref, dpw_ref, dps_ref)
        u, v, gate = z_ref[:, 0:256], z_ref[:, 256:512], z_ref[:, 512:768]
        p, pgate = z_ref[:, 768:1024], z_ref[:, 1024:1280]
        dya, dyb = dy_ref[:, 0:256], dy_ref[:, 256:512]
        vn, vhat, rstd = _layer_norm(v, lng_ref[...], lnb_ref[...])
        vnb = _mx(vn)
        wm = _mx(jnp.where(_sgu_mask(False), w_ref[...], 0.0))
        wmt = _mx(jnp.where(_sgu_mask(True), wt_ref[...], 0.0))
        mixed = _sgu_apply(wm, vnb, nblk) + jnp.tile(bias_ref[...], (nblk, 1))
        silu, dsilu = _silu_and_grad(gate)
        t1 = u * mixed
        d_gate = dya * t1 * dsilu
        d_t1 = dya * silu
        d_u = d_t1 * mixed
        d_mixed = d_t1 * u
        dmb = _mx(d_mixed)
        d_vn = _sgu_apply(wmt, dmb, nblk)
        grp = _lane_group((128, 256))
        lane = lax.broadcasted_iota(jnp.int32, (128, 128), 1)
        dws = [jnp.zeros((128, 128), F32) for _ in range(4)]
        dbias = jnp.zeros((128, 128), F32)
        for n in range(nblk):
            dm_n, dmb_n, vnb_n = d_mixed[n * 128:(n + 1) * 128], dmb[n * 128:(n + 1) * 128], vnb[n * 128:(n + 1) * 128]
            for hh in range(4):
                dws[hh] = dws[hh] + _dot_nt(jnp.where(grp == hh, dmb_n, jnp.zeros_like(dmb_n)), vnb_n)
                rs = jnp.sum(jnp.where(grp == hh, dm_n, 0.0), axis=-1, keepdims=True)
                dbias = dbias + jnp.where(lane == hh, rs, 0.0)
        _acc(dw_ref, jnp.concatenate(dws, axis=0))
        _acc(db_ref, dbias)
        _acc(dlng_ref, _colsum(d_vn * vhat))
        _acc(dlnb_ref, _colsum(d_vn))
        dvh = d_vn * lng_ref[...]
        d_v = rstd * (dvh - jnp.mean(dvh, axis=-1, keepdims=True) - vhat * jnp.mean(dvh * vhat, axis=-1, keepdims=True))

        @pl.when(i == last)
        def _():
            dw_ref[...] = jnp.where(_sgu_mask(False), dw_ref[...], 0.0)

        halo = jnp.where(i > 0, halo_ref[...], 0.0)
        pooled = _pooled(p, halo, i * tm)
        pooled_b = _mx(pooled)
        mixedp = _dot(pooled_b, pw_ref[...])
        psilu, pdsilu = _silu_and_grad(pgate)
        d_pgate = dyb * (mixedp * ps_ref[...]) * pdsilu
        d_ms = dyb * psilu
        _acc(dps_ref, _colsum(d_ms * mixedp))
        dmpb = _mx(d_ms * ps_ref[...])
        _acc(dpw_ref, _dot_tn(pooled_b, dmpb))
        d_pooled = _dot_nt(dmpb, pw_ref[...])
        dmp_halo = _mx(dyn_ref[...] * _silu_and_grad(zn_ref[...])[0] * ps_ref[...])
        d_pooled_halo = jnp.where(i < last, _dot_nt(dmp_halo, pw_ref[...]), 0.0)
        d_p = _pooled_bwd(d_pooled, d_pooled_halo, i * tm)
        dz_ref[...] = _mx(jnp.concatenate([d_u, d_v, d_gate, d_p, d_pgate], axis=1))

    nxt = lambda i: jnp.minimum((i + 1) * (tm // 16), nt16 - 1)
    return pl.pallas_call(
        body, name=name, grid=(t // tm,),
        in_specs=[zrow, prev_halo, pl.BlockSpec((16, 256), lambda i: (nxt(i), 4)),
                  pl.BlockSpec((tm, 512), lambda i: (i, 0)), pl.BlockSpec((16, 256), lambda i: (nxt(i), 1)),
                  params[0], fixed(512, 128)] + params[1:],
        out_specs=[pl.BlockSpec((tm, Z_MIX), lambda i: (i, 0)), fixed(512, 128), fixed(128, 128), fixed(1, 256),
                   fixed(1, 256), fixed(256, 256), fixed(1, 256)],
        out_shape=[jax.ShapeDtypeStruct((t, Z_MIX), MXU_DTYPE), jax.ShapeDtypeStruct((512, 128), F32),
                   jax.ShapeDtypeStruct((128, 128), F32), jax.ShapeDtypeStruct((1, 256), F32),
                   jax.ShapeDtypeStruct((1, 256), F32), jax.ShapeDtypeStruct((256, 256), F32),
                   jax.ShapeDtypeStruct((1, 256), F32)],
        compiler_params=_cp())(z, z, z, dycat, dycat, sgu_w, sgu_wt, sgu_bias, ln_g, ln_b, pool_wbd, pool_scale)


def _rot_half(x, transpose):
    w = x.shape[1]
    lane = lax.broadcasted_iota(jnp.int32, x.shape, 1) % min(w, 256)
    base = 128 if w >= 256 else 0
    lo = jnp.logical_and(lane >= base, lane < base + 32)
    hi = jnp.logical_and(lane >= base + 32, lane < base + 64)
    up = pltpu.roll(x, w - 32, 1)
    down = pltpu.roll(x, 32, 1)
    if transpose:
        return jnp.where(lo, up, jnp.where(hi, -down, 0.0))
    return jnp.where(lo, -up, jnp.where(hi, down, 0.0))


def _rope(x, c, s):
    return x * c + _rot_half(x, False) * s


def _rope_bwd(dy, c, s):
    return dy * c + _rot_half(dy * s, True)


def _qkv_fwd(z, rc, rs, w_uq, w_ukv, gq, gkv, name, tm=512):
    t = z.shape[0]
    tm = min(tm, t)

    def body(zc_ref, zk_ref, rc_ref, rs_ref, wq_ref, wkv_ref, gq_ref, gkv_ref, q_ref, k_ref, v_ref):
        cq, ckv = zc_ref[:, 0:384], zc_ref[:, 384:640]
        c, s = rc_ref[...], rs_ref[...]
        qn, _ = _rms(cq, gq_ref[...])
        q = _rope(_dot(_mx(qn), wq_ref[...]), jnp.tile(c, (1, N_HEADS)), jnp.tile(s, (1, N_HEADS)))
        kvn, _ = _rms(ckv, gkv_ref[...])
        kv = _dot(_mx(kvn), wkv_ref[...])
        kpe = _rope(zk_ref[...], c[:, 128:256], s[:, 128:256])
        for hh in range(N_HEADS):
            q_ref[hh] = _mx(q[:, hh * QK_PAD:(hh + 1) * QK_PAD])
            k_ref[hh] = _mx(jnp.concatenate([kv[:, hh * 128:(hh + 1) * 128], kpe], axis=1))
            v_ref[hh] = _mx(kv[:, 512 + hh * 128:512 + (hh + 1) * 128])

    fixed = lambda a, b: pl.BlockSpec((a, b), lambda i: (0, 0))
    heads = lambda wd: pl.BlockSpec((N_HEADS, tm, wd), lambda i: (0, i, 0))
    return pl.pallas_call(
        body, name=name, grid=(t // tm,),
        in_specs=[pl.BlockSpec((tm, Z_C), lambda i: (i, Z_MIX // Z_C)),
                  pl.BlockSpec((tm, Z_KR), lambda i: (i, (Z_MIX + Z_C) // Z_KR)),
                  pl.BlockSpec((tm, 256), lambda i: (i, 0)), pl.BlockSpec((tm, 256), lambda i: (i, 0)),
                  fixed(384, 1024), fixed(256, 1024), fixed(1, 384), fixed(1, 256)],
        out_specs=[heads(QK_PAD), heads(QK_PAD), heads(V_DIM)],
        out_shape=[jax.ShapeDtypeStruct((N_HEADS, t, QK_PAD), MXU_DTYPE),
                   jax.ShapeDtypeStruct((N_HEADS, t, QK_PAD), MXU_DTYPE),
                   jax.ShapeDtypeStruct((N_HEADS, t, V_DIM), MXU_DTYPE)],
        compiler_params=_cp())(z, z, rc, rs, w_uq, w_ukv, gq, gkv)


def _qkv_bwd(dq, dk, dv, z, rc, rs, w_uq, w_ukv, gq, gkv, name, tm=512):
    t = z.shape[0]
    tm = min(tm, t)

    def body(dq_ref, dk_ref, dv_ref, zc_ref, rc_ref, rs_ref, wq_ref, wkv_ref, gq_ref, gkv_ref,
             dzc_ref, dzk_ref, dwq_ref, dwkv_ref, dgq_ref, dgkv_ref):
        _zero_when(pl.program_id(0) == 0, dwq_ref, dwkv_ref, dgq_ref, dgkv_ref)
        cq, ckv = zc_ref[:, 0:384], zc_ref[:, 384:640]
        c, s = rc_ref[...], rs_ref[...]
        dq_all = jnp.concatenate([dq_ref[hh] for hh in range(N_HEADS)], axis=1)
        dqp = _mx(_rope_bwd(dq_all, jnp.tile(c, (1, N_HEADS)), jnp.tile(s, (1, N_HEADS))))
        qn, rq = _rms(cq, gq_ref[...])
        _acc(dwq_ref, _dot_tn(_mx(qn), dqp))
        d_qn = _dot_nt(dqp, wq_ref[...])
        dkpe = dk_ref[0][:, 128:256]
        for hh in range(1, N_HEADS):
            dkpe = dkpe + dk_ref[hh][:, 128:256]
        dzk_ref[...] = _mx(_rope_bwd(dkpe, c[:, 128:256], s[:, 128:256]))
        dkv = _mx(jnp.concatenate([dk_ref[hh][:, 0:128] for hh in range(N_HEADS)]
                                  + [dv_ref[hh] for hh in range(N_HEADS)], axis=1))
        kvn, rkv = _rms(ckv, gkv_ref[...])
        _acc(dwkv_ref, _dot_tn(_mx(kvn), dkv))
        d_kvn = _dot_nt(dkv, wkv_ref[...])
        d_cq, dgq_t = _rms_bwd(cq, rq, gq_ref[...], d_qn)
        _acc(dgq_ref, _colsum(dgq_t))
        d_ckv, dgkv_t = _rms_bwd(ckv, rkv, gkv_ref[...], d_kvn)
        _acc(dgkv_ref, _colsum(dgkv_t))
        dzc_ref[...] = _mx(jnp.concatenate([d_cq, d_ckv], axis=1))

    fixed = lambda a, b: pl.BlockSpec((a, b), lambda i: (0, 0))
    heads = lambda wd: pl.BlockSpec((N_HEADS, tm, wd), lambda i: (0, i, 0))
    return pl.pallas_call(
        body, name=name, grid=(t // tm,),
        in_specs=[heads(QK_PAD), heads(QK_PAD), heads(V_DIM), pl.BlockSpec((tm, Z_C), lambda i: (i, Z_MIX // Z_C)),
                  pl.BlockSpec((tm, 256), lambda i: (i, 0)), pl.BlockSpec((tm, 256), lambda i: (i, 0)),
                  fixed(384, 1024), fixed(256, 1024), fixed(1, 384), fixed(1, 256)],
        out_specs=[pl.BlockSpec((tm, Z_C), lambda i: (i, 0)), pl.BlockSpec((tm, Z_KR), lambda i: (i, 0)),
                   fixed(384, 1024), fixed(256, 1024), fixed(1, 384), fixed(1, 256)],
        out_shape=[jax.ShapeDtypeStruct((t, Z_C), MXU_DTYPE), jax.ShapeDtypeStruct((t, Z_KR), MXU_DTYPE),
                   jax.ShapeDtypeStruct((384, 1024), F32), jax.ShapeDtypeStruct((256, 1024), F32),
                   jax.ShapeDtypeStruct((1, 384), F32), jax.ShapeDtypeStruct((1, 256), F32)],
        compiler_params=_cp())(dq, dk, dv, z, rc, rs, w_uq, w_ukv, gq, gkv)


def _loop_in_long_trips(n, body):
    def two(t, carry):
        return body(2 * t + 1, body(2 * t, carry))

    def four(t, carry):
        return two(2 * t + 1, two(2 * t, carry))

    lax.fori_loop(0, n // 4, four, 0)
    lax.fori_loop(2 * (n // 4), n // 2, two, 0)
    lax.fori_loop(2 * (n // 2), n, body, 0)


def _init_mask_bias(bias_ref):
    _, tq, tk = bias_ref.shape
    r = lax.broadcasted_iota(jnp.int32, (tq, tk), 0) // 64
    c = lax.broadcasted_iota(jnp.int32, (tq, tk), 1) // 64
    bias_ref[0] = jnp.zeros((tq, tk), F32)
    for d in range(tq // tk):
        bias_ref[1 + d] = jnp.where(c + d * (tk // 64) <= r, 0.0, NEG_INF)


def _gate_block(tq):
    return pl.BlockSpec((tq, 128), lambda h, i: (i, (Z_MIX + Z_C + Z_KR) // 128 + h))


def _attn_fwd(qh, kh, vh, z, name, tq=1024, tk=512):
    t = qh.shape[1]
    tq = min(tq, t)
    tk = min(tk, tq)
    ratio = tq // tk

    def body(q_ref, g_ref, k_hbm, v_hbm, o_ref, yc_ref, lse_ref, k_v, v_v, m_s, acc_s, s_a, s_b, mx_a, mx_b, bias_s,
             sem):
        h, i = pl.program_id(0), pl.program_id(1)

        @pl.when(i == 0)
        def _():
            ck = pltpu.make_async_copy(k_hbm.at[h], k_v, sem.at[0])
            cv = pltpu.make_async_copy(v_hbm.at[h], v_v.at[:, 0:V_DIM], sem.at[1])
            ck.start()
            cv.start()
            v_v[:, V_DIM:2 * V_DIM] = jnp.ones((t, V_DIM), MXU_DTYPE)
            _init_mask_bias(bias_s)
            ck.wait()
            cv.wait()

        q = q_ref[...]
        m_s[...] = jnp.full(m_s.shape, NEG_INF, F32)
        acc_s[...] = jnp.zeros(acc_s.shape, F32)

        last = ratio * (i + 1) - 1

        def keys(j):
            return pl.ds(pl.multiple_of(j * tk, tk), tk)

        def scores(s_ref, mx_ref, j, biased):
            s = _dot_nt(q, k_v[keys(j), :]) * (SCALE * LOG2E)
            if biased:
                s = s + bias_s[jnp.maximum(j - ratio * i + 1, 0)]
            s_ref[...] = s
            mx_ref[...] = jnp.broadcast_to(jnp.max(s, axis=-1, keepdims=True), mx_ref.shape)

        def softmax_pv(s_ref, mx_ref, j):
            m_old = m_s[...]
            m_new = jnp.maximum(m_old, mx_ref[...])
            p = jnp.exp2(s_ref[...] - jnp.tile(m_new, (1, tk // 128)))
            alpha = jnp.exp2(m_old - m_new)
            m_s[...] = m_new
            acc_s[...] = jnp.tile(alpha, (1, 2)) * acc_s[...] + _dot(_mx(p), v_v[keys(j), :])

        scores(s_a, mx_a, 0, True)

        def pair(pp, carry, biased):
            scores(s_b, mx_b, 2 * pp + 1, biased)
            softmax_pv(s_a, mx_a, 2 * pp)
            scores(s_a, mx_a, jnp.minimum(2 * pp + 2, last), biased)
            softmax_pv(s_b, mx_b, 2 * pp + 1)
            return carry

        n_pairs = (last + 1) // 2
        n_plain = jnp.maximum(ratio * i // 2 - 1, 0)
        _loop_in_long_trips(n_plain, lambda pp, carry: pair(pp, carry, False))
        _loop_in_long_trips(n_pairs - n_plain, lambda pp, carry: pair(n_plain + pp, carry, True))
        if ratio % 2 == 1:
            @pl.when(last % 2 == 0)
            def _():
                softmax_pv(s_a, mx_a, last)

        l = acc_s[:, V_DIM:2 * V_DIM]
        o = acc_s[:, 0:V_DIM] / l
        o_ref[...] = o
        yc_ref[...] = _mx(o * _silu_and_grad(g_ref[...])[0])
        lse_ref[...] = m_s[...] + jnp.log2(l)

    return pl.pallas_call(
        body, name=name, grid=(N_HEADS, t // tq),
        in_specs=[pl.BlockSpec((None, tq, QK_PAD), lambda h, i: (h, i, 0)), _gate_block(tq), ANY, ANY],
        out_specs=[pl.BlockSpec((tq, 128), lambda h, i: (i, h)), pl.BlockSpec((tq, 128), lambda h, i: (i, h)),
                   pl.BlockSpec((None, tq, 128), lambda h, i: (h, i, 0))],
        out_shape=[jax.ShapeDtypeStruct((t, N_HEADS * V_DIM), F32), jax.ShapeDtypeStruct((t, N_HEADS * V_DIM), MXU_DTYPE),
                   jax.ShapeDtypeStruct((N_HEADS, t, 128), F32)],
        scratch_shapes=[pltpu.VMEM((t, QK_PAD), MXU_DTYPE), pltpu.VMEM((t, 2 * V_DIM), MXU_DTYPE),
                        pltpu.VMEM((tq, 128), F32), pltpu.VMEM((tq, 2 * V_DIM), F32),
                        pltpu.VMEM((tq, tk), F32), pltpu.VMEM((tq, tk), F32), pltpu.VMEM((tq, 128), F32),
                        pltpu.VMEM((tq, 128), F32), pltpu.VMEM((ratio + 1, tq, tk), F32),
                        pltpu.SemaphoreType.DMA((2,))],
        compiler_params=_cp(2))(qh, z, kh, vh)


def _attn_bwd(qh, kh, vh, o, lse, dycat, z, name, tq=512):
    t = qh.shape[1]
    tq = min(tq, t)
    nq = t // tq

    def body(q_ref, o_ref, lse_ref, dy_ref, g_ref, k_hbm, v_hbm, dq_ref, dgate_ref, dk_hbm, dv_hbm,
             k_v, v_v, dk_acc, dv_acc, dq_acc, delta_s, s_a, dp_a, s_b, dp_b, bias_s, sem):
        h, i = pl.program_id(0), pl.program_id(1)

        @pl.when(i == 0)
        def _():
            ck = pltpu.make_async_copy(k_hbm.at[h], k_v, sem.at[0])
            cv = pltpu.make_async_copy(v_hbm.at[h], v_v, sem.at[1])
            ck.start()
            cv.start()
            _init_mask_bias(bias_s)
            dk_acc[...] = jnp.zeros(dk_acc.shape, F32)
            dv_acc[...] = jnp.zeros(dv_acc.shape, F32)
            ck.wait()
            cv.wait()

        gate, dy, of = g_ref[...], dy_ref[...], o_ref[...]
        silu, dsilu = _silu_and_grad(gate)
        do = dy * silu
        delta = jnp.sum(do * of, axis=-1, keepdims=True)
        dgate_ref[...] = _mx(dy * of * dsilu)
        dob = _mx(do)
        q = q_ref[...]
        delta_s[...] = jnp.broadcast_to(delta, delta_s.shape)
        dq_acc[...] = jnp.zeros(dq_acc.shape, F32)

        def keys(j):
            return pl.ds(pl.multiple_of(j * tq, tq), tq)

        def scores(s_ref, dp_ref, j):
            s = _dot_nt(q, k_v[keys(j), :]) * (SCALE * LOG2E) + bias_s[(j == i).astype(jnp.int32)]
            s_ref[...] = s - jnp.tile(lse_ref[...], (1, tq // 128))
            dp_ref[...] = _dot_nt(dob, v_v[keys(j), :]) - jnp.tile(delta_s[...], (1, tq // 128))

        def grads(s_ref, dp_ref, j):
            ks = keys(j)
            p = jnp.exp2(s_ref[...])
            ds = p * dp_ref[...] * SCALE
            pb, dsb = _mx(p), _mx(ds)
            dq_acc[...] += _dot(dsb, k_v[ks, :])
            dk_acc[ks, :] += _dot_tn(dsb, q)
            dv_acc[ks, :] += _dot_tn(pb, dob)

        scores(s_a, dp_a, 0)

        def pair(pp, carry):
            scores(s_b, dp_b, 2 * pp + 1)
            grads(s_a, dp_a, 2 * pp)
            scores(s_a, dp_a, jnp.minimum(2 * pp + 2, i))
            grads(s_b, dp_b, 2 * pp + 1)
            return carry

        _loop_in_long_trips((i + 1) // 2, pair)

        @pl.when(i % 2 == 0)
        def _():
            grads(s_a, dp_a, i)

        dq_ref[...] = dq_acc[...]

        @pl.when(i == nq - 1)
        def _():
            ck = pltpu.make_async_copy(dk_acc, dk_hbm.at[h], sem.at[0])
            cv = pltpu.make_async_copy(dv_acc, dv_hbm.at[h], sem.at[1])
            ck.start()
            cv.start()
            ck.wait()
            cv.wait()

    return pl.pallas_call(
        body, name=name, grid=(N_HEADS, nq),
        in_specs=[pl.BlockSpec((None, tq, QK_PAD), lambda h, i: (h, i, 0)),
                  pl.BlockSpec((tq, 128), lambda h, i: (i, h)),
                  pl.BlockSpec((None, tq, 128), lambda h, i: (h, i, 0)),
                  pl.BlockSpec((tq, 128), lambda h, i: (i, N_HEADS + h)), _gate_block(tq), ANY, ANY],
        out_specs=[pl.BlockSpec((None, tq, QK_PAD), lambda h, i: (h, i, 0)),
                   pl.BlockSpec((tq, 128), lambda h, i: (i, h)), ANY, ANY],
        out_shape=[jax.ShapeDtypeStruct((N_HEADS, t, QK_PAD), F32), jax.ShapeDtypeStruct((t, Z_GATE), MXU_DTYPE),
                   jax.ShapeDtypeStruct((N_HEADS, t, QK_PAD), F32), jax.ShapeDtypeStruct((N_HEADS, t, V_DIM), F32)],
        scratch_shapes=[pltpu.VMEM((t, QK_PAD), MXU_DTYPE), pltpu.VMEM((t, V_DIM), MXU_DTYPE),
                        pltpu.VMEM((t, QK_PAD), F32), pltpu.VMEM((t, V_DIM), F32), pltpu.VMEM((tq, QK_PAD), F32),
                        pltpu.VMEM((tq, 128), F32)] + [pltpu.VMEM((tq, tq), F32)] * 4
        + [pltpu.VMEM((2, tq, tq), F32), pltpu.SemaphoreType.DMA((2,))],
        compiler_params=_cp(2))(qh, o, lse, dycat, z, kh, vh)


def _out_proj_fwd(yab, yc, w, x, g, target, name, tm=512):
    t, d = x.shape
    tm = min(tm, t)
    is_last = target is not None

    def body(*refs):
        if is_last:
            yab_ref, yc_ref, w_ref, x_ref, g_ref, t_ref, y_ref, dout_ref, loss_ref = refs
            _zero_when(pl.program_id(0) == 0, loss_ref)
        else:
            yab_ref, yc_ref, w_ref, x_ref, g_ref, y_ref, out_ref = refs
        y = _dot(jnp.concatenate([yab_ref[...], yc_ref[...]], axis=1), w_ref[...])
        y_ref[...] = y
        out = x_ref[...] + _rms(y, g_ref[...])[0]
        if is_last:
            diff = out - t_ref[...]
            dout_ref[...] = diff * (1.0 / d)
            part = jnp.sum(jnp.sum(diff * diff, axis=-1, keepdims=True), axis=0, keepdims=True) * (0.5 / d)
            _acc(loss_ref, jnp.broadcast_to(part, (1, 128)))
        else:
            out_ref[...] = out

    row = lambda wd: pl.BlockSpec((tm, wd), lambda i: (i, 0))
    fixed = lambda a, b: pl.BlockSpec((a, b), lambda i: (0, 0))
    in_specs = [row(512), row(512), fixed(d, d), row(d), fixed(1, d)]
    args = [yab, yc, w, x, g]
    out_specs = [row(d), row(d)]
    out_shape = [jax.ShapeDtypeStruct((t, d), F32), jax.ShapeDtypeStruct((t, d), F32)]
    if is_last:
        in_specs.append(row(d))
        args.append(target)
        out_specs.append(fixed(1, 128))
        out_shape.append(jax.ShapeDtypeStruct((1, 128), F32))
    return pl.pallas_call(body, name=name, grid=(t // tm,), in_specs=in_specs, out_specs=out_specs,
                          out_shape=out_shape, compiler_params=_cp())(*args)


def _halves(tm):
    half = tm // 2 if tm >= 512 else tm
    return [pl.ds(s, half) for s in range(0, tm, half)]


def _out_proj_bwd(dout, y, yab, yc, w, g, name, tm=1024):
    t, d = y.shape
    tm = min(tm, t)

    def body(dout_ref, y_ref, yab_ref, yc_ref, w_ref, g_ref, dycat_ref, dw_ref, dg_ref):
        _zero_when(pl.program_id(0) == 0, dw_ref, dg_ref)
        dybs = []
        for rows in _halves(tm):
            y = y_ref[rows, :]
            r = lax.rsqrt(jnp.mean(y * y, axis=-1, keepdims=True) + EPS)
            dy, dgt = _rms_bwd(y, r, g_ref[...], dout_ref[rows, :])
            _acc(dg_ref, _colsum(dgt))
            dybs.append(_mx(dy))
        for rows, dyb in zip(_halves(tm), dybs):
            _acc(dw_ref, _dot_tn(jnp.concatenate([yab_ref[rows, :], yc_ref[rows, :]], axis=1), dyb))
            dycat_ref[rows, :] = _dot_nt(dyb, w_ref[...])

    row = lambda wd: pl.BlockSpec((tm, wd), lambda i: (i, 0))
    fixed = lambda a, b: pl.BlockSpec((a, b), lambda i: (0, 0), pipeline_mode=pl.Buffered(1))
    return pl.pallas_call(
        body, name=name, grid=(t // tm,),
        in_specs=[row(d), row(d), row(512), row(512), fixed(d, d), fixed(1, d)],
        out_specs=[row(d), fixed(d, d), fixed(1, d)],
        out_shape=[jax.ShapeDtypeStruct((t, d), F32), jax.ShapeDtypeStruct((d, d), F32),
                   jax.ShapeDtypeStruct((1, d), F32)],
        compiler_params=_cp())(dout, y, yab, yc, w, g)


def _mesh_pos():
    return lax.axis_index("x"), lax.axis_index("y"), lax.axis_index("c")


def _remote(src, dst, send_sem, recv_sem, to):
    return pltpu.make_async_remote_copy(src_ref=src, dst_ref=dst, send_sem=send_sem, recv_sem=recv_sem,
                                        device_id=to, device_id_type=MESH)


CHUNK_ROWS = 256


def _pieces(rows):
    return [(s, min(CHUNK_ROWS, rows - s)) for s in range(0, rows, CHUNK_ROWS)]


def _piece_table(shapes):
    return [(a, s, sz) for a, shp in enumerate(shapes) for s, sz in _pieces(shp[-2])]


def _gather_weights(shards):
    n = len(shards)
    table = _piece_table([s.shape for s in shards])
    npc = len(table)

    def body(*refs):
        ins, outs = refs[:n], refs[n:2 * n]
        send_sems, recv_sems, fwd_send, fwd_recv = refs[2 * n:]
        x, y, c = _mesh_pos()
        me, sibling = (x, y, c), (x, y, 1 - c)
        chips = [(1 - x, y), (x, 1 - y), (1 - x, 1 - y)]
        slot = lambda cx, cy, layer: 2 * (2 * cx + cy) + layer
        first = []
        for a in range(n):
            for j, (cx, cy) in enumerate(chips):
                first.append(_remote(ins[a].at[c], outs[a].at[slot(x, y, c)], send_sems.at[a, j], recv_sems.at[a, j],
                                     (cx, cy, c)))
                first[-1].start()
        passed = []
        for j, (cx, cy) in enumerate(chips):
            for a in range(n):
                blk = outs[a].at[slot(cx, cy, c)]
                _remote(blk, blk, send_sems.at[a, j], recv_sems.at[a, j], me).wait_recv()
            for q, (a, s, sz) in enumerate(table):
                rows = outs[a].at[slot(cx, cy, c), pl.ds(s, sz)]
                passed.append(_remote(rows, rows, fwd_send.at[j, q], fwd_recv.at[j, q], sibling))
                passed[-1].start()
        for j, (cx, cy) in enumerate(chips):
            for q, (a, s, sz) in enumerate(table):
                rows = outs[a].at[slot(cx, cy, 1 - c), pl.ds(s, sz)]
                _remote(rows, rows, fwd_send.at[j, q], fwd_recv.at[j, q], me).wait_recv()
        for cp in first + passed:
            cp.wait_send()

    return pl.pallas_call(
        body, name="gather_weights", in_specs=[ANY] * n, out_specs=[ANY] * n,
        out_shape=[jax.ShapeDtypeStruct((8,) + s.shape[1:], s.dtype) for s in shards],
        scratch_shapes=[pltpu.SemaphoreType.DMA((n, 3)), pltpu.SemaphoreType.DMA((n, 3)),
                        pltpu.SemaphoreType.DMA((3, npc)), pltpu.SemaphoreType.DMA((3, npc))])(*shards)


def _pair_exchange(parts, common):
    n = len(parts)
    table = _piece_table([p.shape for p in parts] + [common.shape])
    npc = len(table)

    def body(*refs):
        ins, outs = refs[:n + 1], refs[n + 1:2 * n + 2]
        send_sems, recv_sems = refs[2 * n + 2:]
        x, y, c = _mesh_pos()
        sent = []
        for k in range(4):
            for q, (a, s, sz) in enumerate(table):
                if a == n and k > 0:
                    continue
                src = ins[a].at[2 * k + 1 - c, pl.ds(s, sz)] if a < n else ins[a].at[pl.ds(s, sz)]
                dst = outs[a].at[k, pl.ds(s, sz)] if a < n else outs[a].at[pl.ds(s, sz)]
                sent.append(_remote(src, dst, send_sems.at[k, q], recv_sems.at[k, q], (x, y, 1 - c)))
                sent[-1].start()
        for k in range(4):
            for q, (a, s, sz) in enumerate(table):
                if a == n and k > 0:
                    continue
                dst = outs[a].at[k, pl.ds(s, sz)] if a < n else outs[a].at[pl.ds(s, sz)]
                _remote(dst, dst, send_sems.at[k, q], recv_sems.at[k, q], (x, y, c)).wait_recv()
        for cp in sent:
            cp.wait_send()

    return pl.pallas_call(
        body, name="grad_pair_exchange", in_specs=[ANY] * (n + 1), out_specs=[ANY] * (n + 1),
        out_shape=[jax.ShapeDtypeStruct((4,) + p.shape[1:], p.dtype) for p in parts]
        + [jax.ShapeDtypeStruct(common.shape, common.dtype)],
        scratch_shapes=[pltpu.SemaphoreType.DMA((4, npc)), pltpu.SemaphoreType.DMA((4, npc))])(*parts, common)


def _chip_exchange(parts, common):
    n = len(parts)
    table = _piece_table([p.shape for p in parts] + [common.shape])
    npc = len(table)

    def body(*refs):
        ins, outs = refs[:n + 1], refs[n + 1:2 * n + 2]
        send_sems, recv_sems = refs[2 * n + 2:]
        x, y, c = _mesh_pos()
        mine = 2 * x + y
        chips = [(1 - x, y), (x, 1 - y), (1 - x, 1 - y)]
        src = lambda a, k: ins[a].at[k] if a < n else ins[a]
        sent = []
        for j, (cx, cy) in enumerate(chips):
            for q, (a, s, sz) in enumerate(table):
                sent.append(_remote(src(a, 2 * cx + cy).at[pl.ds(s, sz)], outs[a].at[mine, pl.ds(s, sz)],
                                    send_sems.at[j, q], recv_sems.at[j, q], (cx, cy, c)))
                sent[-1].start()
        for j, (cx, cy) in enumerate(chips):
            for q, (a, s, sz) in enumerate(table):
                dst = outs[a].at[2 * cx + cy, pl.ds(s, sz)]
                _remote(dst, dst, send_sems.at[j, q], recv_sems.at[j, q], (x, y, c)).wait_recv()
        for cp in sent:
            cp.wait_send()

    return pl.pallas_call(
        body, name="grad_chip_exchange", in_specs=[ANY] * (n + 1), out_specs=[ANY] * (n + 1),
        out_shape=[jax.ShapeDtypeStruct(p.shape, p.dtype) for p in parts]
        + [jax.ShapeDtypeStruct((4,) + common.shape, common.dtype)],
        scratch_shapes=[pltpu.SemaphoreType.DMA((3, npc)), pltpu.SemaphoreType.DMA((3, npc))])(*parts, common)


def _sibling_exchange(sums):
    n = len(sums)
    table = _piece_table([s.shape for s in sums])
    npc = len(table)

    def body(*refs):
        ins, outs = refs[:n], refs[n:2 * n]
        send_sems, recv_sems = refs[2 * n:]
        x, y, c = _mesh_pos()
        sent = []
        for q, (a, s, sz) in enumerate(table):
            sent.append(_remote(ins[a].at[pl.ds(s, sz)], outs[a].at[pl.ds(s, sz)], send_sems.at[q], recv_sems.at[q],
                                (x, y, 1 - c)))
            sent[-1].start()
        for q, (a, s, sz) in enumerate(table):
            dst = outs[a].at[pl.ds(s, sz)]
            _remote(dst, dst, send_sems.at[q], recv_sems.at[q], (x, y, c)).wait_recv()
        for cp in sent:
            cp.wait_send()

    return pl.pallas_call(
        body, name="sibling_exchange", in_specs=[ANY] * n, out_specs=[ANY] * n,
        out_shape=[jax.ShapeDtypeStruct(s.shape, s.dtype) for s in sums],
        scratch_shapes=[pltpu.SemaphoreType.DMA((npc,)), pltpu.SemaphoreType.DMA((npc,))])(*sums)


def _pair_sum(parts, got, name, tr):
    _, r, c = got.shape
    tr = min(tr, r)

    def body(p_ref, g_ref, o_ref, w_ref):
        total = p_ref[...] + g_ref[...]
        o_ref[...] = total
        w_ref[...] = total.astype(WIRE_DTYPE)

    blk = pl.BlockSpec((None, tr, c), lambda k, i: (k, i, 0))
    mine = pl.BlockSpec((None, tr, c), lambda k, i: (2 * k + lax.axis_index("c"), i, 0))
    return pl.pallas_call(
        body, name=name, grid=(4, r // tr), in_specs=[mine, blk], out_specs=[blk, blk],
        out_shape=[jax.ShapeDtypeStruct(got.shape, F32), jax.ShapeDtypeStruct(got.shape, WIRE_DTYPE)],
        compiler_params=_cp(2))(parts, got)


def _sum_chips(pair_sums, recv, name, tr):
    _, r, c = pair_sums.shape
    tr = min(tr, r)

    def body(own_ref, r_ref, o_ref):
        chip = 2 * lax.axis_index("x") + lax.axis_index("y")
        own_blk = own_ref[...]
        acc = jnp.where(chip == 0, own_blk, r_ref[0].astype(F32))
        for k in range(1, 4):
            acc = acc + jnp.where(chip == k, own_blk, r_ref[k].astype(F32))
        o_ref[...] = acc

    return pl.pallas_call(
        body, name=name, grid=(r // tr,),
        in_specs=[pl.BlockSpec((None, tr, c), lambda i: (2 * lax.axis_index("x") + lax.axis_index("y"), i, 0)),
                  pl.BlockSpec((4, tr, c), lambda i: (0, i, 0))],
        out_specs=pl.BlockSpec((tr, c), lambda i: (i, 0)),
        out_shape=jax.ShapeDtypeStruct((r, c), F32), compiler_params=_cp())(pair_sums, recv)


def _sum_leading(parts, name, tr):
    nlead, r, c = parts.shape
    tr = min(tr, r)

    def body(p_ref, o_ref):
        acc = p_ref[0]
        for j in range(1, nlead):
            acc = acc + p_ref[j]
        o_ref[...] = acc

    return pl.pallas_call(
        body, name=name, grid=(r // tr,),
        in_specs=[pl.BlockSpec((nlead, tr, c), lambda i: (0, i, 0))], out_specs=pl.BlockSpec((tr, c), lambda i: (i, 0)),
        out_shape=jax.ShapeDtypeStruct((r, c), parts.dtype), compiler_params=_cp())(parts)


def _adamw_update(w_ref, g_ref, m_ref, v_ref, d_ref, nm_ref, nv_ref):
    gg = g_ref[...]
    nm = ADAM_B1 * m_ref[...] + (1.0 - ADAM_B1) * gg
    nv = ADAM_B2 * v_ref[...] + (1.0 - ADAM_B2) * jnp.square(gg)
    m_hat = nm / (1.0 - ADAM_B1 ** ADAM_STEP)
    v_hat = nv / (1.0 - ADAM_B2 ** ADAM_STEP)
    d_ref[...] = -ADAM_LR * (m_hat / (jnp.sqrt(v_hat) + ADAM_EPS) + ADAM_WD * w_ref[...])
    nm_ref[...] = nm
    nv_ref[...] = nv


def _adamw(w, g, m, v, name, tr):
    r, c = w.shape
    tr = min(tr, r)

    def body(w_ref, g_ref, m_ref, v_ref, d_ref, nm_ref, nv_ref):
        _adamw_update(w_ref, g_ref, m_ref, v_ref, d_ref, nm_ref, nv_ref)

    blk = pl.BlockSpec((tr, c), lambda i: (i, 0))
    return pl.pallas_call(
        body, name=name, grid=(r // tr,), in_specs=[blk] * 4, out_specs=[blk] * 3,
        out_shape=[jax.ShapeDtypeStruct((r, c), F32)] * 3, compiler_params=_cp())(w, g, m, v)


def _adamw_many(ws, gs, ms, vs, name):
    n = len(ws)

    def body(*refs):
        ins, outs = refs[:4 * n], refs[4 * n:]
        for k in range(n):
            _adamw_update(ins[k], ins[n + k], ins[2 * n + k], ins[3 * n + k], outs[k], outs[n + k], outs[2 * n + k])

    vmem = pl.BlockSpec(memory_space=pltpu.VMEM)
    res = pl.pallas_call(
        body, name=name, in_specs=[vmem] * (4 * n), out_specs=[vmem] * (3 * n),
        out_shape=[jax.ShapeDtypeStruct(a.shape, F32) for a in ws] * 3)(*ws, *gs, *ms, *vs)
    return res[:n], res[n:2 * n], res[2 * n:]


def _rope_tables(positions):
    inv_freq = ROPE_BASE ** (-jnp.arange(0, 64, 2, dtype=F32) / 64)
    ang = positions.astype(F32)[:, None] * inv_freq
    cos, sin = jnp.cos(ang), jnp.sin(ang)
    t = positions.shape[0]
    rc = jnp.concatenate([jnp.ones((t, 128), F32), cos, cos, jnp.ones((t, 64), F32)], axis=1)
    rs = jnp.concatenate([jnp.zeros((t, 128), F32), sin, sin, jnp.zeros((t, 64), F32)], axis=1)
    return rc, rs


def _layer_params(l, w_in, w_uq, w_ukv, w_out, small):
    p = {}
    p["w_in"] = jnp.concatenate([w_in[l][:, :1984], jnp.zeros((1024, 64), w_in.dtype), w_in[l][:, 1984:]], axis=1)
    p["w_uq"] = jnp.pad(w_uq[l].reshape(384, 4, 192), ((0, 0), (0, 0), (0, 64))).reshape(384, 1024)
    p["w_ukv"] = w_ukv[l].reshape(256, 4, 2, 128).transpose(0, 2, 1, 3).reshape(256, 1024)
    p["w_out"] = w_out[l]
    p["pre_g"] = small["pre_norm_g"][l][None]
    p["post_g"] = small["post_norm_g"][l][None]
    p["sgu_w"] = small["sgu_w"][l].reshape(512, 128)
    p["sgu_wt"] = small["sgu_w"][l].transpose(0, 2, 1).reshape(512, 128)
    p["sgu_bias"] = jnp.repeat(small["sgu_b"][l].T, 64, axis=1)
    p["ln_g"] = small["sgu_ln_g"][l][None]
    p["ln_b"] = small["sgu_ln_b"][l][None]
    p["pool_wbd"] = _mx(jax.scipy.linalg.block_diag(*[small["pool_w"][l][gi] for gi in range(4)]))
    p["pool_scale"] = small["pool_scale"][l][None]
    p["gq"] = small["q_norm_g"][l][None]
    p["gkv"] = small["kv_norm_g"][l][None]
    return p


def _layer_fwd(l, x, p, rc, rs, target):
    z, h = _in_proj_fwd(x, p["pre_g"], p["w_in"], f"in_proj_fwd_{l}")
    yab = _mix_fwd(z, p["sgu_w"], p["sgu_bias"], p["ln_g"], p["ln_b"], p["pool_wbd"], p["pool_scale"], f"mix_fwd_{l}")
    qh, kh, vh = _qkv_fwd(z, rc, rs, p["w_uq"], p["w_ukv"], p["gq"], p["gkv"], f"qkv_fwd_{l}")
    o, yc, lse = _attn_fwd(qh, kh, vh, z, f"attn_fwd_{l}")
    outs = _out_proj_fwd(yab, yc, p["w_out"], x, p["post_g"], target, f"out_proj_fwd_{l}")
    saved = dict(x=x, z=z, h=h, yab=yab, qh=qh, kh=kh, vh=vh, o=o, yc=yc, lse=lse, y=outs[0])
    return saved, outs[1:]


def _layer_bwd(l, dout, sv, p, rc, rs):
    dycat, dw_out, dpost = _out_proj_bwd(dout, sv["y"], sv["yab"], sv["yc"], p["w_out"], p["post_g"], f"out_proj_bwd_{l}")
    dq, dgate, dk, dv = _attn_bwd(sv["qh"], sv["kh"], sv["vh"], sv["o"], sv["lse"], dycat, sv["z"], f"attn_bwd_{l}")
    dzc, dzk, dwq, dwkv, dgq, dgkv = _qkv_bwd(dq, dk, dv, sv["z"], rc, rs, p["w_uq"], p["w_ukv"], p["gq"], p["gkv"],
                                              f"qkv_bwd_{l}")
    dzm, dsw, dsb, dlng, dlnb, dpw, dps = _mix_bwd(sv["z"], dycat, p["sgu_w"], p["sgu_wt"], p["sgu_bias"], p["ln_g"],
                                                   p["ln_b"], p["pool_wbd"], p["pool_scale"], f"mix_bwd_{l}")
    dx, dw_in, dpre = _in_proj_bwd(dzm, dzc, dzk, dgate, sv["h"], sv["x"], dout, p["w_in"], p["pre_g"], f"in_proj_bwd_{l}")
    grads = {
        "pre_norm_g": dpre[0], "post_norm_g": dpost[0],
        "w_in": jnp.concatenate([dw_in[:, :1984], dw_in[:, 2048:]], axis=1),
        "sgu_w": dsw.reshape(4, 128, 128), "sgu_b": dsb[:, :4].T, "sgu_ln_g": dlng[0], "sgu_ln_b": dlnb[0],
        "pool_w": jnp.stack([dpw[64 * gi:64 * gi + 64, 64 * gi:64 * gi + 64] for gi in range(4)]),
        "pool_scale": dps[0], "q_norm_g": dgq[0],
        "w_uq": dwq.reshape(384, 4, 256)[:, :, :192].reshape(384, 768), "kv_norm_g": dgkv[0],
        "w_ukv": dwkv.reshape(256, 2, 4, 128).transpose(0, 2, 1, 3).reshape(256, 1024), "w_out": dw_out,
    }
    return dx, grads


SMALL_NAMES = ["pre_norm_g", "post_norm_g", "sgu_w", "sgu_b", "sgu_ln_g", "sgu_ln_b", "pool_w", "pool_scale",
               "q_norm_g", "kv_norm_g"]
BIG_NAMES = ["w_in", "w_uq", "w_ukv", "w_out"]
WEIGHT_NAMES = ["pre_norm_g", "post_norm_g", "w_in", "sgu_w", "sgu_b", "sgu_ln_g", "sgu_ln_b", "pool_w", "pool_scale",
                "q_norm_g", "w_uq", "kv_norm_g", "w_ukv", "w_out"]


def _local_step(x, positions, target, w_in, w_uq, w_ukv, w_out, small):
    rc, rs = _rope_tables(positions)
    params = [_layer_params(l, w_in, w_uq, w_ukv, w_out, small) for l in range(DEPTH)]
    saved = []
    for l in range(DEPTH):
        sv, outs = _layer_fwd(l, x, params[l], rc, rs, target if l == DEPTH - 1 else None)
        saved.append(sv)
        if l < DEPTH - 1:
            x = outs[0]
    dout, loss = outs
    grads = [None] * DEPTH
    for l in reversed(range(DEPTH)):
        dout, grads[l] = _layer_bwd(l, dout, saved[l], params[l], rc, rs)
    return loss[0, 0], dout, {k: jnp.stack([grads[l][k] for l in range(DEPTH)]) for k in WEIGHT_NAMES}


def _pack_small(tree, extra=None):
    pieces = [tree[k].reshape(-1) for k in SMALL_NAMES]
    pieces.append(jnp.zeros((1,), F32) if extra is None else extra.reshape(1))
    flat = jnp.concatenate(pieces)
    rows = -(-flat.shape[0] // 1024) * 8
    return jnp.pad(flat, (0, rows * 128 - flat.shape[0])).reshape(rows, 128)


def _unpack_small(packed, like):
    flat = packed.reshape(-1)
    out, off = {}, 0
    for k in SMALL_NAMES:
        size = like[k].size
        out[k] = flat[off:off + size].reshape(like[k].shape)
        off += size
    return out, flat[off]


def kernel(x, positions, pre_norm_g, post_norm_g, w_in, sgu_w, sgu_b, sgu_ln_g, sgu_ln_b, pool_w, pool_scale, q_norm_g, w_uq, kv_norm_g, w_ukv, w_out, loss_target, m_pre_norm_g, m_post_norm_g, m_w_in, m_sgu_w, m_sgu_b, m_sgu_ln_g, m_sgu_ln_b, m_pool_w, m_pool_scale, m_q_norm_g, m_w_uq, m_kv_norm_g, m_w_ukv, m_w_out, v_pre_norm_g, v_post_norm_g, v_w_in, v_sgu_w, v_sgu_b, v_sgu_ln_g, v_sgu_ln_b, v_pool_w, v_pool_scale, v_q_norm_g, v_w_uq, v_kv_norm_g, v_w_ukv, v_w_out):
    w = dict(pre_norm_g=pre_norm_g, post_norm_g=post_norm_g, w_in=w_in, sgu_w=sgu_w, sgu_b=sgu_b, sgu_ln_g=sgu_ln_g,
             sgu_ln_b=sgu_ln_b, pool_w=pool_w, pool_scale=pool_scale, q_norm_g=q_norm_g, w_uq=w_uq, kv_norm_g=kv_norm_g,
             w_ukv=w_ukv, w_out=w_out)
    m = dict(pre_norm_g=m_pre_norm_g, post_norm_g=m_post_norm_g, w_in=m_w_in, sgu_w=m_sgu_w, sgu_b=m_sgu_b,
             sgu_ln_g=m_sgu_ln_g, sgu_ln_b=m_sgu_ln_b, pool_w=m_pool_w, pool_scale=m_pool_scale, q_norm_g=m_q_norm_g,
             w_uq=m_w_uq, kv_norm_g=m_kv_norm_g, w_ukv=m_w_ukv, w_out=m_w_out)
    v = dict(pre_norm_g=v_pre_norm_g, post_norm_g=v_post_norm_g, w_in=v_w_in, sgu_w=v_sgu_w, sgu_b=v_sgu_b,
             sgu_ln_g=v_sgu_ln_g, sgu_ln_b=v_sgu_ln_b, pool_w=v_pool_w, pool_scale=v_pool_scale, q_norm_g=v_q_norm_g,
             w_uq=v_w_uq, kv_norm_g=v_kv_norm_g, w_ukv=v_w_ukv, w_out=v_w_out)

    core = lax.axis_index("c")
    chip = 2 * lax.axis_index("x") + lax.axis_index("y")
    shards = [_mx(w[k]) for k in BIG_NAMES]
    gathered = _gather_weights(shards)
    g_in, g_uq, g_ukv, g_out = [lax.dynamic_update_slice(g, s, (2 * chip, 0, 0)) for g, s in zip(gathered, shards)]
    cols = lambda g: g.reshape((4, 2) + g.shape[1:]).transpose(1, 2, 0, 3).reshape(2, g.shape[1], 4 * g.shape[2])
    full_out = g_out.reshape(4, 2, 256, 1024).transpose(1, 0, 2, 3).reshape(2, 1024, 1024)
    loss, dx, grads = _local_step(x[0], positions[0], loss_target[0], cols(g_in), cols(g_uq), cols(g_ukv), full_out, w)

    split_cols = lambda g: g.reshape(2, g.shape[1], 4, g.shape[2] // 4).transpose(2, 0, 1, 3).reshape(8, g.shape[1], g.shape[2] // 4)
    parts = [split_cols(grads["w_in"]), split_cols(grads["w_uq"]), split_cols(grads["w_ukv"]),
             grads["w_out"].reshape(2, 4, 256, 1024).transpose(1, 0, 2, 3).reshape(8, 256, 1024)]
    common = _pack_small(grads, loss)
    got = _pair_exchange(parts, common)
    pair_sums = [_pair_sum(parts[a], got[a], f"pair_sum_{BIG_NAMES[a]}", 128) for a in range(4)]
    chip_common = _sum_leading(jnp.stack([common, got[4]]), "pair_sum_small", common.shape[0])
    received = _chip_exchange([ps[1] for ps in pair_sums], chip_common)
    sums = [_sum_chips(pair_sums[a][0], received[a], f"sum_{BIG_NAMES[a]}", 128) for a in range(4)]
    all_common = lax.dynamic_update_slice(received[4], chip_common[None], (chip, 0, 0))
    small_sum, loss = _unpack_small(_sum_leading(all_common, "sum_small", all_common.shape[1]), w)
    others = _sibling_exchange(sums)
    total = dict(small_sum)
    for a, k in enumerate(BIG_NAMES):
        total[k] = jnp.where(core == 0, jnp.stack([sums[a], others[a]]), jnp.stack([others[a], sums[a]]))

    rows2d = lambda a: a.reshape(-1, a.shape[-1])
    small_out = _adamw_many(*[[rows2d(tree[k]) for k in SMALL_NAMES] for tree in (w, total, m, v)], "adamw_small")
    delta, new_m, new_v = ({k: r.reshape(w[k].shape) for k, r in zip(SMALL_NAMES, res)} for res in small_out)
    for k in BIG_NAMES:
        shape = w[k].shape
        flat = lambda a: a.reshape(shape[0] * shape[1], shape[2])
        res = _adamw(flat(w[k]), flat(total[k]), flat(m[k]), flat(v[k]), f"adamw_{k}", 256)
        delta[k], new_m[k], new_v[k] = (r.reshape(shape) for r in res)

    return (loss, dx[None], *[total[k] for k in WEIGHT_NAMES], *[delta[k] for k in WEIGHT_NAMES],
            *[new_m[k] for k in WEIGHT_NAMES], *[new_v[k] for k in WEIGHT_NAMES])
```

```python
import jax
import jax.numpy as jnp
from jax import lax
from jax.experimental import pallas as pl
from jax.experimental.pallas import tpu as pltpu

F32 = jnp.float32
MXU_DTYPE = jnp.bfloat16
WIRE_DTYPE = jnp.bfloat16
EPS = 1e-6
NEG_INF = -1e30
CHUNK = 64
DEPTH = 2
N_HEADS = 4
QK_PAD = 256
V_DIM = 128
SCALE = 192 ** -0.5
LOG2E = 1.4426950408889634
ROPE_BASE = 10000.0
ADAM_LR, ADAM_B1, ADAM_B2, ADAM_EPS, ADAM_WD, ADAM_STEP = 0.001, 0.9, 0.999, 1e-08, 0.01, 10
VMEM_LIMIT_BYTES = 56 * 1024 * 1024
MESH = pl.DeviceIdType.MESH
ANY = pl.BlockSpec(memory_space=pl.ANY)

Z_MIX, Z_C, Z_KR, Z_GATE = 1280, 640, 128, 512
Z_W = Z_MIX + Z_C + Z_KR + Z_GATE


def _cp(n_axes=1):
    return pltpu.CompilerParams(dimension_semantics=("arbitrary",) * n_axes, vmem_limit_bytes=VMEM_LIMIT_BYTES)


def _dot(a, b):
    return lax.dot_general(a, b, (((1,), (0,)), ((), ())), preferred_element_type=F32)


def _dot_nt(a, b):
    return lax.dot_general(a, b, (((1,), (1,)), ((), ())), preferred_element_type=F32)


def _dot_tn(a, b):
    return lax.dot_general(a, b, (((0,), (0,)), ((), ())), preferred_element_type=F32)


def _mx(a):
    return a.astype(MXU_DTYPE)


def _silu_and_grad(g):
    sg = jax.nn.sigmoid(g)
    return g * sg, sg * (1.0 + g * (1.0 - sg))


def _rms(x, g):
    r = lax.rsqrt(jnp.mean(x * x, axis=-1, keepdims=True) + EPS)
    return x * r * g, r


def _rms_bwd(x, r, g, dy):
    xhat = x * r
    dyg = dy * g
    dx = r * (dyg - xhat * jnp.mean(dyg * xhat, axis=-1, keepdims=True))
    return dx, dy * xhat


def _zero_when(first, *refs):
    @pl.when(first)
    def _():
        for ref in refs:
            ref[...] = jnp.zeros(ref.shape, ref.dtype)


def _acc(ref, val):
    ref[...] += val


def _colsum(a):
    return jnp.sum(a, axis=0, keepdims=True)


def _in_proj_fwd(x, g, w, name, tm=512):
    t, d = x.shape
    n = w.shape[1]
    tm = min(tm, t)

    def body(x_ref, g_ref, w_ref, z_ref, h_ref):
        h, _ = _rms(x_ref[...], g_ref[...])
        h = _mx(h)
        h_ref[...] = h
        z_ref[...] = _dot(h, w_ref[...])

    return pl.pallas_call(
        body, name=name, grid=(t // tm,),
        in_specs=[pl.BlockSpec((tm, d), lambda i: (i, 0)), pl.BlockSpec((1, d), lambda i: (0, 0)),
                  pl.BlockSpec((d, n), lambda i: (0, 0))],
        out_specs=[pl.BlockSpec((tm, n), lambda i: (i, 0)), pl.BlockSpec((tm, d), lambda i: (i, 0))],
        out_shape=[jax.ShapeDtypeStruct((t, n), F32), jax.ShapeDtypeStruct((t, d), MXU_DTYPE)],
        compiler_params=_cp())(x, g, w)


def _in_proj_bwd(dz_mix, dz_c, dz_kr, dz_gate, h, x, d_res, w, g, name, tm=512):
    t, d = x.shape
    n = w.shape[1]
    tm = min(tm, t)

    def body(dm_ref, dc_ref, dk_ref, dg_ref, h_ref, x_ref, dres_ref, w_ref, g_ref, dx_ref, dw_ref, dgn_ref):
        first = pl.program_id(0) == 0
        dz = jnp.concatenate([dm_ref[...], dc_ref[...], dk_ref[...], dg_ref[...]], axis=1)

        _zero_when(first, dw_ref, dgn_ref)
        hb = h_ref[...]
        for c0 in range(0, n, 512):
            dw_ref[:, c0:c0 + 512] += _dot_tn(hb, dz[:, c0:c0 + 512])
        dh = _dot_nt(dz, w_ref[...])
        xf = x_ref[...]
        r = lax.rsqrt(jnp.mean(xf * xf, axis=-1, keepdims=True) + EPS)
        dx, dgt = _rms_bwd(xf, r, g_ref[...], dh)
        dx_ref[...] = dx + dres_ref[...]
        _acc(dgn_ref, _colsum(dgt))

    row = lambda wd: pl.BlockSpec((tm, wd), lambda i: (i, 0))
    fixed = lambda a, b: pl.BlockSpec((a, b), lambda i: (0, 0), pipeline_mode=pl.Buffered(1))
    return pl.pallas_call(
        body, name=name, grid=(t // tm,),
        in_specs=[row(Z_MIX), row(Z_C), row(Z_KR), row(Z_GATE), row(d), row(d), row(d), fixed(d, n), fixed(1, d)],
        out_specs=[row(d), fixed(d, n), fixed(1, d)],
        out_shape=[jax.ShapeDtypeStruct((t, d), F32), jax.ShapeDtypeStruct((d, n), F32),
                   jax.ShapeDtypeStruct((1, d), F32)],
        compiler_params=_cp())(dz_mix, dz_c, dz_kr, dz_gate, h, x, d_res, w, g)


def _lane_group(shape):
    return lax.broadcasted_iota(jnp.int32, shape, 1) // 64


def _select_group(vals):
    grp = _lane_group(vals[0].shape)
    out = vals[3]
    for gi in (2, 1, 0):
        out = jnp.where(grp == gi, vals[gi], out)
    return out


def _sgu_mask(transposed):
    r = (lax.broadcasted_iota(jnp.int32, (512, 128), 0) % 128) // CHUNK
    c = lax.broadcasted_iota(jnp.int32, (512, 128), 1) // CHUNK
    return (r <= c) if transposed else (c <= r)


def _sgu_apply(wstack, vb, nblk):
    outs = []
    for n in range(nblk):
        r = _dot(wstack, vb[n * 128:(n + 1) * 128, :])
        outs.append(_select_group([r[hh * 128:(hh + 1) * 128, :] for hh in range(4)]))
    return jnp.concatenate(outs, axis=0)


def _layer_norm(v, g, b):
    mu = jnp.mean(v, axis=-1, keepdims=True)
    vc = v - mu
    rstd = lax.rsqrt(jnp.mean(vc * vc, axis=-1, keepdims=True) + EPS)
    vhat = vc * rstd
    return vhat * g + b, vhat, rstd


def _pool_counts(t0, n):
    t = t0 + lax.broadcasted_iota(jnp.int32, (n, 256), 0)
    w = _select_group([jnp.full((n, 256), wv, jnp.int32) for wv in (2, 4, 8, 16)])
    return jnp.minimum(t + 1, w).astype(F32)


def _pooled(p, halo, t0):
    tm = p.shape[0]
    ext = jnp.concatenate([halo, p], axis=0)
    s2 = ext + pltpu.roll(ext, 1, 0)
    s4 = s2 + pltpu.roll(s2, 2, 0)
    s8 = s4 + pltpu.roll(s4, 4, 0)
    s16 = s8 + pltpu.roll(s8, 8, 0)
    sel = _select_group([s2, s4, s8, s16])[16:, :]
    return sel / _pool_counts(t0, tm) - p


def _pooled_bwd(dpool, dpool_halo, t0):
    tm = dpool.shape[0]
    n = tm + 16
    ext = jnp.concatenate([dpool, dpool_halo], axis=0) / _pool_counts(t0, n)
    f2 = ext + pltpu.roll(ext, n - 1, 0)
    f4 = f2 + pltpu.roll(f2, n - 2, 0)
    f8 = f4 + pltpu.roll(f4, n - 4, 0)
    f16 = f8 + pltpu.roll(f8, n - 8, 0)
    return _select_group([f2, f4, f8, f16])[:tm, :] - dpool


def _mix_specs(t, tm):
    nt16 = t // 16
    zrow = pl.BlockSpec((tm, Z_MIX), lambda i: (i, 0))
    prev_halo = pl.BlockSpec((16, 256), lambda i: (jnp.maximum(i * (tm // 16) - 1, 0), 3))
    fixed = lambda a, b: pl.BlockSpec((a, b), lambda i: (0, 0))
    params = [fixed(512, 128), fixed(128, 256), fixed(1, 256), fixed(1, 256), fixed(256, 256), fixed(1, 256)]
    return nt16, zrow, prev_halo, fixed, params


def _mix_fwd(z, sgu_w, sgu_bias, ln_g, ln_b, pool_wbd, pool_scale, name, tm=512):
    t = z.shape[0]
    tm = min(tm, t)
    _, zrow, prev_halo, _, params = _mix_specs(t, tm)

    def body(z_ref, halo_ref, w_ref, bias_ref, lng_ref, lnb_ref, pw_ref, ps_ref, y_ref):
        i = pl.program_id(0)
        u, v, gate = z_ref[:, 0:256], z_ref[:, 256:512], z_ref[:, 512:768]
        p, pgate = z_ref[:, 768:1024], z_ref[:, 1024:1280]
        vn, _, _ = _layer_norm(v, lng_ref[...], lnb_ref[...])
        wm = _mx(jnp.where(_sgu_mask(False), w_ref[...], 0.0))
        mixed = _sgu_apply(wm, _mx(vn), tm // 128) + jnp.tile(bias_ref[...], (tm // 128, 1))
        ya = u * mixed * _silu_and_grad(gate)[0]
        halo = jnp.where(i > 0, halo_ref[...], 0.0)
        pooled = _pooled(p, halo, i * tm)
        yb = _dot(_mx(pooled), pw_ref[...]) * ps_ref[...] * _silu_and_grad(pgate)[0]
        y_ref[...] = _mx(jnp.concatenate([ya, yb], axis=1))

    return pl.pallas_call(
        body, name=name, grid=(t // tm,),
        in_specs=[zrow, prev_halo] + params,
        out_specs=pl.BlockSpec((tm, 512), lambda i: (i, 0)),
        out_shape=jax.ShapeDtypeStruct((t, 512), MXU_DTYPE),
        compiler_params=_cp())(z, z, sgu_w, sgu_bias, ln_g, ln_b, pool_wbd, pool_scale)


def _mix_bwd(z, dycat, sgu_w, sgu_wt, sgu_bias, ln_g, ln_b, pool_wbd, pool_scale, name, tm=512):
    t = z.shape[0]
    tm = min(tm, t)
    nt16, zrow, prev_halo, fixed, params = _mix_specs(t, tm)
    nblk = tm // 128
    last = t // tm - 1

    def body(z_ref, halo_ref, zn_ref, dy_ref, dyn_ref, w_ref, wt_ref, bias_ref, lng_ref, lnb_ref, pw_ref, ps_ref,
             dz_ref, dw_ref, db_ref, dlng_ref, dlnb_ref, dpw_ref, dps_ref):
        i = pl.program_id(0)
        _zero_when(i == 0, dw_ref, db_ref, dlng_ref, dlnb_ref, dpw_ref, dps_ref)
        u, v, gate = z_ref[:, 0:256], z_ref[:, 256:512], z_ref[:, 512:768]
        p, pgate = z_ref[:, 768:1024], z_ref[:, 1024:1280]
        dya, dyb = dy_ref[:, 0:256], dy_ref[:, 256:512]
        vn, vhat, rstd = _layer_norm(v, lng_ref[...], lnb_ref[...])
        vnb = _mx(vn)
        wm = _mx(jnp.where(_sgu_mask(False), w_ref[...], 0.0))
        wmt = _mx(jnp.where(_sgu_mask(True), wt_ref[...], 0.0))
        mixed = _sgu_apply(wm, vnb, nblk) + jnp.tile(bias_ref[...], (nblk, 1))
        silu, dsilu = _silu_and_grad(gate)
        t1 = u * mixed
        d_gate = dya * t1 * dsilu
        d_t1 = dya * silu
        d_u = d_t1 * mixed
        d_mixed = d_t1 * u
        dmb = _mx(d_mixed)
        d_vn = _sgu_apply(wmt, dmb, nblk)
        grp = _lane_group((128, 256))
        lane = lax.broadcasted_iota(jnp.int32, (128, 128), 1)
        dws = [jnp.zeros((128, 128), F32) for _ in range(4)]
        dbias = jnp.zeros((128, 128), F32)
        for n in range(nblk):
            dm_n, dmb_n, vnb_n = d_mixed[n * 128:(n + 1) * 128], dmb[n * 128:(n + 1) * 128], vnb[n * 128:(n + 1) * 128]
            for hh in range(4):
                dws[hh] = dws[hh] + _dot_nt(jnp.where(grp == hh, dmb_n, jnp.zeros_like(dmb_n)), vnb_n)
                rs = jnp.sum(jnp.where(grp == hh, dm_n, 0.0), axis=-1, keepdims=True)
                dbias = dbias + jnp.where(lane == hh, rs, 0.0)
        _acc(dw_ref, jnp.concatenate(dws, axis=0))
        _acc(db_ref, dbias)
        _acc(dlng_ref, _colsum(d_vn * vhat))
        _acc(dlnb_ref, _colsum(d_vn))
        dvh = d_vn * lng_ref[...]
        d_v = rstd * (dvh - jnp.mean(dvh, axis=-1, keepdims=True) - vhat * jnp.mean(dvh * vhat, axis=-1, keepdims=True))

        @pl.when(i == last)
        def _():
            dw_ref[...] = jnp.where(_sgu_mask(False), dw_ref[...], 0.0)

        halo = jnp.where(i > 0, halo_ref[...], 0.0)
        pooled = _pooled(p, halo, i * tm)
        pooled_b = _mx(pooled)
        mixedp = _dot(pooled_b, pw_ref[...])
        psilu, pdsilu = _silu_and_grad(pgate)
        d_pgate = dyb * (mixedp * ps_ref[...]) * pdsilu
        d_ms = dyb * psilu
        _acc(dps_ref, _colsum(d_ms * mixedp))
        dmpb = _mx(d_ms * ps_ref[...])
        _acc(dpw_ref, _dot_tn(pooled_b, dmpb))
        d_pooled = _dot_nt(dmpb, pw_ref[...])
        dmp_halo = _mx(dyn_ref[...] * _silu_and_grad(zn_ref[...])[0] * ps_ref[...])
        d_pooled_halo = jnp.where(i < last, _dot_nt(dmp_halo, pw_ref[...]), 0.0)
        d_p = _pooled_bwd(d_pooled, d_pooled_halo, i * tm)
        dz_ref[...] = _mx(jnp.concatenate([d_u, d_v, d_gate, d_p, d_pgate], axis=1))

    nxt = lambda i: jnp.minimum((i + 1) * (tm // 16), nt16 - 1)
    return pl.pallas_call(
        body, name=name, grid=(t // tm,),
        in_specs=[zrow, prev_halo, pl.BlockSpec((16, 256), lambda i: (nxt(i), 4)),
                  pl.BlockSpec((tm, 512), lambda i: (i, 0)), pl.BlockSpec((16, 256), lambda i: (nxt(i), 1)),
                  params[0], fixed(512, 128)] + params[1:],
        out_specs=[pl.BlockSpec((tm, Z_MIX), lambda i: (i, 0)), fixed(512, 128), fixed(128, 128), fixed(1, 256),
                   fixed(1, 256), fixed(256, 256), fixed(1, 256)],
        out_shape=[jax.ShapeDtypeStruct((t, Z_MIX), MXU_DTYPE), jax.ShapeDtypeStruct((512, 128), F32),
                   jax.ShapeDtypeStruct((128, 128), F32), jax.ShapeDtypeStruct((1, 256), F32),
                   jax.ShapeDtypeStruct((1, 256), F32), jax.ShapeDtypeStruct((256, 256), F32),
                   jax.ShapeDtypeStruct((1, 256), F32)],
        compiler_params=_cp())(z, z, z, dycat, dycat, sgu_w, sgu_wt, sgu_bias, ln_g, ln_b, pool_wbd, pool_scale)


def _rot_half(x, transpose):
    w = x.shape[1]
    lane = lax.broadcasted_iota(jnp.int32, x.shape, 1) % min(w, 256)
    base = 128 if w >= 256 else 0
    lo = jnp.logical_and(lane >= base, lane < base + 32)
    hi = jnp.logical_and(lane >= base + 32, lane < base + 64)
    up = pltpu.roll(x, w - 32, 1)
    down = pltpu.roll(x, 32, 1)
    if transpose:
        return jnp.where(lo, up, jnp.where(hi, -down, 0.0))
    return jnp.where(lo, -up, jnp.where(hi, down, 0.0))


def _rope(x, c, s):
    return x * c + _rot_half(x, False) * s


def _rope_bwd(dy, c, s):
    return dy * c + _rot_half(dy * s, True)


def _qkv_fwd(z, rc, rs, w_uq, w_ukv, gq, gkv, name, tm=512):
    t = z.shape[0]
    tm = min(tm, t)

    def body(zc_ref, zk_ref, rc_ref, rs_ref, wq_ref, wkv_ref, gq_ref, gkv_ref, q_ref, k_ref, v_ref):
        cq, ckv = zc_ref[:, 0:384], zc_ref[:, 384:640]
        c, s = rc_ref[...], rs_ref[...]
        qn, _ = _rms(cq, gq_ref[...])
        q = _rope(_dot(_mx(qn), wq_ref[...]), jnp.tile(c, (1, N_HEADS)), jnp.tile(s, (1, N_HEADS)))
        kvn, _ = _rms(ckv, gkv_ref[...])
        kv = _dot(_mx(kvn), wkv_ref[...])
        kpe = _rope(zk_ref[...], c[:, 128:256], s[:, 128:256])
        for hh in range(N_HEADS):
            q_ref[hh] = _mx(q[:, hh * QK_PAD:(hh + 1) * QK_PAD])
            k_ref[hh] = _mx(jnp.concatenate([kv[:, hh * 128:(hh + 1) * 128], kpe], axis=1))
            v_ref[hh] = _mx(kv[:, 512 + hh * 128:512 + (hh + 1) * 128])

    fixed = lambda a, b: pl.BlockSpec((a, b), lambda i: (0, 0))
    heads = lambda wd: pl.BlockSpec((N_HEADS, tm, wd), lambda i: (0, i, 0))
    return pl.pallas_call(
        body, name=name, grid=(t // tm,),
        in_specs=[pl.BlockSpec((tm, Z_C), lambda i: (i, Z_MIX // Z_C)),
                  pl.BlockSpec((tm, Z_KR), lambda i: (i, (Z_MIX + Z_C) // Z_KR)),
                  pl.BlockSpec((tm, 256), lambda i: (i, 0)), pl.BlockSpec((tm, 256), lambda i: (i, 0)),
                  fixed(384, 1024), fixed(256, 1024), fixed(1, 384), fixed(1, 256)],
        out_specs=[heads(QK_PAD), heads(QK_PAD), heads(V_DIM)],
        out_shape=[jax.ShapeDtypeStruct((N_HEADS, t, QK_PAD), MXU_DTYPE),
                   jax.ShapeDtypeStruct((N_HEADS, t, QK_PAD), MXU_DTYPE),
                   jax.ShapeDtypeStruct((N_HEADS, t, V_DIM), MXU_DTYPE)],
        compiler_params=_cp())(z, z, rc, rs, w_uq, w_ukv, gq, gkv)


def _qkv_bwd(dq, dk, dv, z, rc, rs, w_uq, w_ukv, gq, gkv, name, tm=512):
    t = z.shape[0]
    tm = min(tm, t)

    def body(dq_ref, dk_ref, dv_ref, zc_ref, rc_ref, rs_ref, wq_ref, wkv_ref, gq_ref, gkv_ref,
             dzc_ref, dzk_ref, dwq_ref, dwkv_ref, dgq_ref, dgkv_ref):
        _zero_when(pl.program_id(0) == 0, dwq_ref, dwkv_ref, dgq_ref, dgkv_ref)
        cq, ckv = zc_ref[:, 0:384], zc_ref[:, 384:640]
        c, s = rc_ref[...], rs_ref[...]
        dq_all = jnp.concatenate([dq_ref[hh] for hh in range(N_HEADS)], axis=1)
        dqp = _mx(_rope_bwd(dq_all, jnp.tile(c, (1, N_HEADS)), jnp.tile(s, (1, N_HEADS))))
        qn, rq = _rms(cq, gq_ref[...])
        _acc(dwq_ref, _dot_tn(_mx(qn), dqp))
        d_qn = _dot_nt(dqp, wq_ref[...])
        dkpe = dk_ref[0][:, 128:256]
        for hh in range(1, N_HEADS):
            dkpe = dkpe + dk_ref[hh][:, 128:256]
        dzk_ref[...] = _mx(_rope_bwd(dkpe, c[:, 128:256], s[:, 128:256]))
        dkv = _mx(jnp.concatenate([dk_ref[hh][:, 0:128] for hh in range(N_HEADS)]
                                  + [dv_ref[hh] for hh in range(N_HEADS)], axis=1))
        kvn, rkv = _rms(ckv, gkv_ref[...])
        _acc(dwkv_ref, _dot_tn(_mx(kvn), dkv))
        d_kvn = _dot_nt(dkv, wkv_ref[...])
        d_cq, dgq_t = _rms_bwd(cq, rq, gq_ref[...], d_qn)
        _acc(dgq_ref, _colsum(dgq_t))
        d_ckv, dgkv_t = _rms_bwd(ckv, rkv, gkv_ref[...], d_kvn)
        _acc(dgkv_ref, _colsum(dgkv_t))
        dzc_ref[...] = _mx(jnp.concatenate([d_cq, d_ckv], axis=1))

    fixed = lambda a, b: pl.BlockSpec((a, b), lambda i: (0, 0))
    heads = lambda wd: pl.BlockSpec((N_HEADS, tm, wd), lambda i: (0, i, 0))
    return pl.pallas_call(
        body, name=name, grid=(t // tm,),
        in_specs=[heads(QK_PAD), heads(QK_PAD), heads(V_DIM), pl.BlockSpec((tm, Z_C), lambda i: (i, Z_MIX // Z_C)),
                  pl.BlockSpec((tm, 256), lambda i: (i, 0)), pl.BlockSpec((tm, 256), lambda i: (i, 0)),
                  fixed(384, 1024), fixed(256, 1024), fixed(1, 384), fixed(1, 256)],
        out_specs=[pl.BlockSpec((tm, Z_C), lambda i: (i, 0)), pl.BlockSpec((tm, Z_KR), lambda i: (i, 0)),
                   fixed(384, 1024), fixed(256, 1024), fixed(1, 384), fixed(1, 256)],
        out_shape=[jax.ShapeDtypeStruct((t, Z_C), MXU_DTYPE), jax.ShapeDtypeStruct((t, Z_KR), MXU_DTYPE),
                   jax.ShapeDtypeStruct((384, 1024), F32), jax.ShapeDtypeStruct((256, 1024), F32),
                   jax.ShapeDtypeStruct((1, 384), F32), jax.ShapeDtypeStruct((1, 256), F32)],
        compiler_params=_cp())(dq, dk, dv, z, rc, rs, w_uq, w_ukv, gq, gkv)


def _loop_in_long_trips(n, body, longest=4):
    def doubled(inner):
        return lambda t, carry: inner(2 * t + 1, inner(2 * t, carry))

    trips = [body]
    while 2 ** (len(trips) - 1) < longest:
        trips.append(doubled(trips[-1]))
    done = 0
    for level in reversed(range(len(trips))):
        size = 2 ** level
        end = n // size
        lax.fori_loop(done, end, trips[level], 0)
        done = 2 * end if level else end


def _init_mask_bias(bias_ref):
    _, tq, tk = bias_ref.shape
    r = lax.broadcasted_iota(jnp.int32, (tq, tk), 0) // CHUNK
    c = lax.broadcasted_iota(jnp.int32, (tq, tk), 1) // CHUNK
    bias_ref[0] = jnp.zeros((tq, tk), F32)
    for d in range(tq // tk):
        bias_ref[1 + d] = jnp.where(c + d * (tk // CHUNK) <= r, 0.0, NEG_INF)


def _gate_block(tq):
    return pl.BlockSpec((tq, 128), lambda h, i: (i, (Z_MIX + Z_C + Z_KR) // 128 + h))


def _attn_fwd(qh, kh, vh, z, name, tq=1024, tk=512):
    t = qh.shape[1]
    tq = min(tq, t)
    tk = min(tk, tq)
    ratio = tq // tk

    def body(q_ref, g_ref, k_hbm, v_hbm, o_ref, yc_ref, lse_ref, k_v, v_v, m_s, acc_s, s_a, s_b, mx_a, mx_b, bias_s,
             sem):
        h, i = pl.program_id(0), pl.program_id(1)

        @pl.when(i == 0)
        def _():
            ck = pltpu.make_async_copy(k_hbm.at[h], k_v, sem.at[0])
            cv = pltpu.make_async_copy(v_hbm.at[h], v_v.at[:, 0:V_DIM], sem.at[1])
            ck.start()
            cv.start()
            v_v[:, V_DIM:2 * V_DIM] = jnp.ones((t, V_DIM), MXU_DTYPE)
            _init_mask_bias(bias_s)
            ck.wait()
            cv.wait()

        q = q_ref[...]
        m_s[...] = jnp.full(m_s.shape, NEG_INF, F32)
        acc_s[...] = jnp.zeros(acc_s.shape, F32)

        last = ratio * (i + 1) - 1

        def keys(j):
            return pl.ds(pl.multiple_of(j * tk, tk), tk)

        def scores(s_ref, mx_ref, j, biased):
            s = _dot_nt(q, k_v[keys(j), :]) * (SCALE * LOG2E)
            if biased:
                s = s + bias_s[jnp.maximum(j - ratio * i + 1, 0)]
            s_ref[...] = s
            mx_ref[...] = jnp.broadcast_to(jnp.max(s, axis=-1, keepdims=True), mx_ref.shape)

        def softmax_pv(s_ref, mx_ref, j):
            m_old = m_s[...]
            m_new = jnp.maximum(m_old, mx_ref[...])
            p = jnp.exp2(s_ref[...] - jnp.tile(m_new, (1, tk // 128)))
            alpha = jnp.exp2(m_old - m_new)
            m_s[...] = m_new
            acc_s[...] = jnp.tile(alpha, (1, 2)) * acc_s[...] + _dot(_mx(p), v_v[keys(j), :])

        scores(s_a, mx_a, 0, True)

        def pair(pp, carry, biased):
            scores(s_b, mx_b, 2 * pp + 1, biased)
            softmax_pv(s_a, mx_a, 2 * pp)
            scores(s_a, mx_a, jnp.minimum(2 * pp + 2, last), biased)
            softmax_pv(s_b, mx_b, 2 * pp + 1)
            return carry

        n_pairs = (last + 1) // 2
        n_plain = jnp.maximum(ratio * i // 2 - 1, 0)
        _loop_in_long_trips(n_plain, lambda pp, carry: pair(pp, carry, False))
        _loop_in_long_trips(n_pairs - n_plain, lambda pp, carry: pair(n_plain + pp, carry, True))
        if ratio % 2 == 1:
            @pl.when(last % 2 == 0)
            def _():
                softmax_pv(s_a, mx_a, last)

        l = acc_s[:, V_DIM:2 * V_DIM]
        o = acc_s[:, 0:V_DIM] / l
        o_ref[...] = o
        yc_ref[...] = _mx(o * _silu_and_grad(g_ref[...])[0])
        lse_ref[...] = m_s[...] + jnp.log2(l)

    return pl.pallas_call(
        body, name=name, grid=(N_HEADS, t // tq),
        in_specs=[pl.BlockSpec((None, tq, QK_PAD), lambda h, i: (h, i, 0)), _gate_block(tq), ANY, ANY],
        out_specs=[pl.BlockSpec((tq, 128), lambda h, i: (i, h)), pl.BlockSpec((tq, 128), lambda h, i: (i, h)),
                   pl.BlockSpec((None, tq, 128), lambda h, i: (h, i, 0))],
        out_shape=[jax.ShapeDtypeStruct((t, N_HEADS * V_DIM), F32), jax.ShapeDtypeStruct((t, N_HEADS * V_DIM), MXU_DTYPE),
                   jax.ShapeDtypeStruct((N_HEADS, t, 128), F32)],
        scratch_shapes=[pltpu.VMEM((t, QK_PAD), MXU_DTYPE), pltpu.VMEM((t, 2 * V_DIM), MXU_DTYPE),
                        pltpu.VMEM((tq, 128), F32), pltpu.VMEM((tq, 2 * V_DIM), F32),
                        pltpu.VMEM((tq, tk), F32), pltpu.VMEM((tq, tk), F32), pltpu.VMEM((tq, 128), F32),
                        pltpu.VMEM((tq, 128), F32), pltpu.VMEM((ratio + 1, tq, tk), F32),
                        pltpu.SemaphoreType.DMA((2,))],
        compiler_params=_cp(2))(qh, z, kh, vh)


def _attn_bwd(qh, kh, vh, o, lse, dycat, z, name, tq=512):
    t = qh.shape[1]
    tq = min(tq, t)
    nq = t // tq

    def body(q_ref, o_ref, lse_ref, dy_ref, g_ref, k_hbm, v_hbm, dq_ref, dgate_ref, dk_hbm, dv_hbm,
             k_v, v_v, dk_acc, dv_acc, dq_acc, delta_s, s_a, dp_a, s_b, dp_b, bias_s, sem):
        h, i = pl.program_id(0), pl.program_id(1)

        @pl.when(i == 0)
        def _():
            ck = pltpu.make_async_copy(k_hbm.at[h], k_v, sem.at[0])
            cv = pltpu.make_async_copy(v_hbm.at[h], v_v, sem.at[1])
            ck.start()
            cv.start()
            _init_mask_bias(bias_s)
            dk_acc[...] = jnp.zeros(dk_acc.shape, F32)
            dv_acc[...] = jnp.zeros(dv_acc.shape, F32)
            ck.wait()
            cv.wait()

        gate, dy, of = g_ref[...], dy_ref[...], o_ref[...]
        silu, dsilu = _silu_and_grad(gate)
        do = dy * silu
        delta = jnp.sum(do * of, axis=-1, keepdims=True)
        dgate_ref[...] = _mx(dy * of * dsilu)
        dob = _mx(do)
        q = q_ref[...]
        delta_s[...] = jnp.broadcast_to(delta, delta_s.shape)
        dq_acc[...] = jnp.zeros(dq_acc.shape, F32)

        def keys(j):
            return pl.ds(pl.multiple_of(j * tq, tq), tq)

        def scores(s_ref, dp_ref, j):
            s = _dot_nt(q, k_v[keys(j), :]) * (SCALE * LOG2E) + bias_s[(j == i).astype(jnp.int32)]
            s_ref[...] = s - jnp.tile(lse_ref[...], (1, tq // 128))
            dp_ref[...] = _dot_nt(dob, v_v[keys(j), :]) - jnp.tile(delta_s[...], (1, tq // 128))

        def grads(s_ref, dp_ref, j):
            ks = keys(j)
            p = jnp.exp2(s_ref[...])
            ds = p * dp_ref[...] * SCALE
            pb, dsb = _mx(p), _mx(ds)
            dq_acc[...] += _dot(dsb, k_v[ks, :])
            dk_acc[ks, :] += _dot_tn(dsb, q)
            dv_acc[ks, :] += _dot_tn(pb, dob)

        scores(s_a, dp_a, 0)

        def pair(pp, carry):
            scores(s_b, dp_b, 2 * pp + 1)
            grads(s_a, dp_a, 2 * pp)
            scores(s_a, dp_a, jnp.minimum(2 * pp + 2, i))
            grads(s_b, dp_b, 2 * pp + 1)
            return carry

        _loop_in_long_trips((i + 1) // 2, pair, longest=8)

        @pl.when(i % 2 == 0)
        def _():
            grads(s_a, dp_a, i)

        dq_ref[...] = dq_acc[...]

        @pl.when(i == nq - 1)
        def _():
            ck = pltpu.make_async_copy(dk_acc, dk_hbm.at[h], sem.at[0])
            cv = pltpu.make_async_copy(dv_acc, dv_hbm.at[h], sem.at[1])
            ck.start()
            cv.start()
            ck.wait()
            cv.wait()

    return pl.pallas_call(
        body, name=name, grid=(N_HEADS, nq),
        in_specs=[pl.BlockSpec((None, tq, QK_PAD), lambda h, i: (h, i, 0)),
                  pl.BlockSpec((tq, 128), lambda h, i: (i, h)),
                  pl.BlockSpec((None, tq, 128), lambda h, i: (h, i, 0)),
                  pl.BlockSpec((tq, 128), lambda h, i: (i, N_HEADS + h)), _gate_block(tq), ANY, ANY],
        out_specs=[pl.BlockSpec((None, tq, QK_PAD), lambda h, i: (h, i, 0)),
                   pl.BlockSpec((tq, 128), lambda h, i: (i, h)), ANY, ANY],
        out_shape=[jax.ShapeDtypeStruct((N_HEADS, t, QK_PAD), F32), jax.ShapeDtypeStruct((t, Z_GATE), MXU_DTYPE),
                   jax.ShapeDtypeStruct((N_HEADS, t, QK_PAD), F32), jax.ShapeDtypeStruct((N_HEADS, t, V_DIM), F32)],
        scratch_shapes=[pltpu.VMEM((t, QK_PAD), MXU_DTYPE), pltpu.VMEM((t, V_DIM), MXU_DTYPE),
                        pltpu.VMEM((t, QK_PAD), F32), pltpu.VMEM((t, V_DIM), F32), pltpu.VMEM((tq, QK_PAD), F32),
                        pltpu.VMEM((tq, 128), F32)] + [pltpu.VMEM((tq, tq), F32)] * 4
        + [pltpu.VMEM((2, tq, tq), F32), pltpu.SemaphoreType.DMA((2,))],
        compiler_params=_cp(2))(qh, o, lse, dycat, z, kh, vh)


def _out_proj_fwd(yab, yc, w, x, g, target, name, tm=512):
    t, d = x.shape
    tm = min(tm, t)
    is_last = target is not None

    def body(*refs):
        if is_last:
            yab_ref, yc_ref, w_ref, x_ref, g_ref, t_ref, y_ref, dout_ref, loss_ref = refs
            _zero_when(pl.program_id(0) == 0, loss_ref)
        else:
            yab_ref, yc_ref, w_ref, x_ref, g_ref, y_ref, out_ref = refs
        y = _dot(jnp.concatenate([yab_ref[...], yc_ref[...]], axis=1), w_ref[...])
        y_ref[...] = y
        out = x_ref[...] + _rms(y, g_ref[...])[0]
        if is_last:
            diff = out - t_ref[...]
            dout_ref[...] = diff * (1.0 / d)
            part = jnp.sum(jnp.sum(diff * diff, axis=-1, keepdims=True), axis=0, keepdims=True) * (0.5 / d)
            _acc(loss_ref, jnp.broadcast_to(part, (1, 128)))
        else:
            out_ref[...] = out

    row = lambda wd: pl.BlockSpec((tm, wd), lambda i: (i, 0))
    fixed = lambda a, b: pl.BlockSpec((a, b), lambda i: (0, 0))
    in_specs = [row(512), row(512), fixed(d, d), row(d), fixed(1, d)]
    args = [yab, yc, w, x, g]
    out_specs = [row(d), row(d)]
    out_shape = [jax.ShapeDtypeStruct((t, d), F32), jax.ShapeDtypeStruct((t, d), F32)]
    if is_last:
        in_specs.append(row(d))
        args.append(target)
        out_specs.append(fixed(1, 128))
        out_shape.append(jax.ShapeDtypeStruct((1, 128), F32))
    return pl.pallas_call(body, name=name, grid=(t // tm,), in_specs=in_specs, out_specs=out_specs,
                          out_shape=out_shape, compiler_params=_cp())(*args)


def _halves(tm):
    half = tm // 2 if tm >= 512 else tm
    return [pl.ds(s, half) for s in range(0, tm, half)]


def _out_proj_bwd(dout, y, yab, yc, w, g, name, tm=1024):
    t, d = y.shape
    tm = min(tm, t)

    def body(dout_ref, y_ref, yab_ref, yc_ref, w_ref, g_ref, dycat_ref, dw_ref, dg_ref):
        _zero_when(pl.program_id(0) == 0, dw_ref, dg_ref)
        dybs = []
        for rows in _halves(tm):
            y = y_ref[rows, :]
            r = lax.rsqrt(jnp.mean(y * y, axis=-1, keepdims=True) + EPS)
            dy, dgt = _rms_bwd(y, r, g_ref[...], dout_ref[rows, :])
            _acc(dg_ref, _colsum(dgt))
            dybs.append(_mx(dy))
        for rows, dyb in zip(_halves(tm), dybs):
            _acc(dw_ref, _dot_tn(jnp.concatenate([yab_ref[rows, :], yc_ref[rows, :]], axis=1), dyb))
            dycat_ref[rows, :] = _dot_nt(dyb, w_ref[...])

    row = lambda wd: pl.BlockSpec((tm, wd), lambda i: (i, 0))
    fixed = lambda a, b: pl.BlockSpec((a, b), lambda i: (0, 0), pipeline_mode=pl.Buffered(1))
    return pl.pallas_call(
        body, name=name, grid=(t // tm,),
        in_specs=[row(d), row(d), row(512), row(512), fixed(d, d), fixed(1, d)],
        out_specs=[row(d), fixed(d, d), fixed(1, d)],
        out_shape=[jax.ShapeDtypeStruct((t, d), F32), jax.ShapeDtypeStruct((d, d), F32),
                   jax.ShapeDtypeStruct((1, d), F32)],
        compiler_params=_cp())(dout, y, yab, yc, w, g)


def _mesh_pos():
    return lax.axis_index("x"), lax.axis_index("y"), lax.axis_index("c")


def _remote(src, dst, send_sem, recv_sem, to):
    return pltpu.make_async_remote_copy(src_ref=src, dst_ref=dst, send_sem=send_sem, recv_sem=recv_sem,
                                        device_id=to, device_id_type=MESH)


CHUNK_ROWS = 256


def _pieces(rows):
    return [(s, min(CHUNK_ROWS, rows - s)) for s in range(0, rows, CHUNK_ROWS)]


def _piece_table(shapes):
    return [(a, s, sz) for a, shp in enumerate(shapes) for s, sz in _pieces(shp[-2])]


def _gather_weights(shards):
    n = len(shards)
    table = _piece_table([s.shape for s in shards])
    npc = len(table)

    def body(*refs):
        ins, outs = refs[:n], refs[n:2 * n]
        send_sems, recv_sems, fwd_send, fwd_recv = refs[2 * n:]
        x, y, c = _mesh_pos()
        me, sibling = (x, y, c), (x, y, 1 - c)
        chips = [(1 - x, y), (x, 1 - y), (1 - x, 1 - y)]
        slot = lambda cx, cy, layer: 2 * (2 * cx + cy) + layer
        first = []
        for a in range(n):
            for j, (cx, cy) in enumerate(chips):
                first.append(_remote(ins[a].at[c], outs[a].at[slot(x, y, c)], send_sems.at[a, j], recv_sems.at[a, j],
                                     (cx, cy, c)))
                first[-1].start()
        passed = []
        for j, (cx, cy) in enumerate(chips):
            for a in range(n):
                blk = outs[a].at[slot(cx, cy, c)]
                _remote(blk, blk, send_sems.at[a, j], recv_sems.at[a, j], me).wait_recv()
            for q, (a, s, sz) in enumerate(table):
                rows = outs[a].at[slot(cx, cy, c), pl.ds(s, sz)]
                passed.append(_remote(rows, rows, fwd_send.at[j, q], fwd_recv.at[j, q], sibling))
                passed[-1].start()
        for j, (cx, cy) in enumerate(chips):
            for q, (a, s, sz) in enumerate(table):
                rows = outs[a].at[slot(cx, cy, 1 - c), pl.ds(s, sz)]
                _remote(rows, rows, fwd_send.at[j, q], fwd_recv.at[j, q], me).wait_recv()
        for cp in first + passed:
            cp.wait_send()

    return pl.pallas_call(
        body, name="gather_weights", in_specs=[ANY] * n, out_specs=[ANY] * n,
        out_shape=[jax.ShapeDtypeStruct((8,) + s.shape[1:], s.dtype) for s in shards],
        scratch_shapes=[pltpu.SemaphoreType.DMA((n, 3)), pltpu.SemaphoreType.DMA((n, 3)),
                        pltpu.SemaphoreType.DMA((3, npc)), pltpu.SemaphoreType.DMA((3, npc))])(*shards)


def _pair_exchange(parts, common):
    n = len(parts)
    table = _piece_table([p.shape for p in parts] + [common.shape])
    npc = len(table)

    def body(*refs):
        ins, outs = refs[:n + 1], refs[n + 1:2 * n + 2]
        send_sems, recv_sems = refs[2 * n + 2:]
        x, y, c = _mesh_pos()
        sent = []
        for k in range(4):
            for q, (a, s, sz) in enumerate(table):
                if a == n and k > 0:
                    continue
                src = ins[a].at[2 * k + 1 - c, pl.ds(s, sz)] if a < n else ins[a].at[pl.ds(s, sz)]
                dst = outs[a].at[k, pl.ds(s, sz)] if a < n else outs[a].at[pl.ds(s, sz)]
                sent.append(_remote(src, dst, send_sems.at[k, q], recv_sems.at[k, q], (x, y, 1 - c)))
                sent[-1].start()
        for k in range(4):
            for q, (a, s, sz) in enumerate(table):
                if a == n and k > 0:
                    continue
                dst = outs[a].at[k, pl.ds(s, sz)] if a < n else outs[a].at[pl.ds(s, sz)]
                _remote(dst, dst, send_sems.at[k, q], recv_sems.at[k, q], (x, y, c)).wait_recv()
        for cp in sent:
            cp.wait_send()

    return pl.pallas_call(
        body, name="grad_pair_exchange", in_specs=[ANY] * (n + 1), out_specs=[ANY] * (n + 1),
        out_shape=[jax.ShapeDtypeStruct((4,) + p.shape[1:], p.dtype) for p in parts]
        + [jax.ShapeDtypeStruct(common.shape, common.dtype)],
        scratch_shapes=[pltpu.SemaphoreType.DMA((4, npc)), pltpu.SemaphoreType.DMA((4, npc))])(*parts, common)


def _chip_exchange(parts, common):
    n = len(parts)
    table = _piece_table([p.shape for p in parts] + [common.shape])
    npc = len(table)

    def body(*refs):
        ins, outs = refs[:n + 1], refs[n + 1:2 * n + 2]
        send_sems, recv_sems = refs[2 * n + 2:]
        x, y, c = _mesh_pos()
        mine = 2 * x + y
        chips = [(1 - x, y), (x, 1 - y), (1 - x, 1 - y)]
        src = lambda a, k: ins[a].at[k] if a < n else ins[a]
        sent = []
        for j, (cx, cy) in enumerate(chips):
            for q, (a, s, sz) in enumerate(table):
                sent.append(_remote(src(a, 2 * cx + cy).at[pl.ds(s, sz)], outs[a].at[mine, pl.ds(s, sz)],
                                    send_sems.at[j, q], recv_sems.at[j, q], (cx, cy, c)))
                sent[-1].start()
        for j, (cx, cy) in enumerate(chips):
            for q, (a, s, sz) in enumerate(table):
                dst = outs[a].at[2 * cx + cy, pl.ds(s, sz)]
                _remote(dst, dst, send_sems.at[j, q], recv_sems.at[j, q], (x, y, c)).wait_recv()
        for cp in sent:
            cp.wait_send()

    return pl.pallas_call(
        body, name="grad_chip_exchange", in_specs=[ANY] * (n + 1), out_specs=[ANY] * (n + 1),
        out_shape=[jax.ShapeDtypeStruct(p.shape, p.dtype) for p in parts]
        + [jax.ShapeDtypeStruct((4,) + common.shape, common.dtype)],
        scratch_shapes=[pltpu.SemaphoreType.DMA((3, npc)), pltpu.SemaphoreType.DMA((3, npc))])(*parts, common)


def _sibling_exchange(sums):
    n = len(sums)
    table = _piece_table([s.shape for s in sums])
    npc = len(table)

    def body(*refs):
        ins, outs = refs[:n], refs[n:2 * n]
        send_sems, recv_sems = refs[2 * n:]
        x, y, c = _mesh_pos()
        sent = []
        for q, (a, s, sz) in enumerate(table):
            sent.append(_remote(ins[a].at[pl.ds(s, sz)], outs[a].at[pl.ds(s, sz)], send_sems.at[q], recv_sems.at[q],
                                (x, y, 1 - c)))
            sent[-1].start()
        for q, (a, s, sz) in enumerate(table):
            dst = outs[a].at[pl.ds(s, sz)]
            _remote(dst, dst, send_sems.at[q], recv_sems.at[q], (x, y, c)).wait_recv()
        for cp in sent:
            cp.wait_send()

    return pl.pallas_call(
        body, name="sibling_exchange", in_specs=[ANY] * n, out_specs=[ANY] * n,
        out_shape=[jax.ShapeDtypeStruct(s.shape, s.dtype) for s in sums],
        scratch_shapes=[pltpu.SemaphoreType.DMA((npc,)), pltpu.SemaphoreType.DMA((npc,))])(*sums)


def _pair_sum(parts, got, name, tr):
    _, r, c = got.shape
    tr = min(tr, r)

    def body(p_ref, g_ref, o_ref, w_ref):
        total = p_ref[...] + g_ref[...]
        o_ref[...] = total
        w_ref[...] = total.astype(WIRE_DTYPE)

    blk = pl.BlockSpec((None, tr, c), lambda k, i: (k, i, 0))
    mine = pl.BlockSpec((None, tr, c), lambda k, i: (2 * k + lax.axis_index("c"), i, 0))
    return pl.pallas_call(
        body, name=name, grid=(4, r // tr), in_specs=[mine, blk], out_specs=[blk, blk],
        out_shape=[jax.ShapeDtypeStruct(got.shape, F32), jax.ShapeDtypeStruct(got.shape, WIRE_DTYPE)],
        compiler_params=_cp(2))(parts, got)


def _sum_chips(pair_sums, recv, name, tr):
    _, r, c = pair_sums.shape
    tr = min(tr, r)

    def body(own_ref, r_ref, o_ref):
        chip = 2 * lax.axis_index("x") + lax.axis_index("y")
        own_blk = own_ref[...]
        acc = jnp.where(chip == 0, own_blk, r_ref[0].astype(F32))
        for k in range(1, 4):
            acc = acc + jnp.where(chip == k, own_blk, r_ref[k].astype(F32))
        o_ref[...] = acc

    return pl.pallas_call(
        body, name=name, grid=(r // tr,),
        in_specs=[pl.BlockSpec((None, tr, c), lambda i: (2 * lax.axis_index("x") + lax.axis_index("y"), i, 0)),
                  pl.BlockSpec((4, tr, c), lambda i: (0, i, 0))],
        out_specs=pl.BlockSpec((tr, c), lambda i: (i, 0)),
        out_shape=jax.ShapeDtypeStruct((r, c), F32), compiler_params=_cp())(pair_sums, recv)


def _sum_leading(parts, name, tr):
    nlead, r, c = parts.shape
    tr = min(tr, r)

    def body(p_ref, o_ref):
        acc = p_ref[0]
        for j in range(1, nlead):
            acc = acc + p_ref[j]
        o_ref[...] = acc

    return pl.pallas_call(
        body, name=name, grid=(r // tr,),
        in_specs=[pl.BlockSpec((nlead, tr, c), lambda i: (0, i, 0))], out_specs=pl.BlockSpec((tr, c), lambda i: (i, 0)),
        out_shape=jax.ShapeDtypeStruct((r, c), parts.dtype), compiler_params=_cp())(parts)


def _adamw_update(w_ref, g_ref, m_ref, v_ref, d_ref, nm_ref, nv_ref):
    gg = g_ref[...]
    nm = ADAM_B1 * m_ref[...] + (1.0 - ADAM_B1) * gg
    nv = ADAM_B2 * v_ref[...] + (1.0 - ADAM_B2) * jnp.square(gg)
    m_hat = nm / (1.0 - ADAM_B1 ** ADAM_STEP)
    v_hat = nv / (1.0 - ADAM_B2 ** ADAM_STEP)
    d_ref[...] = -ADAM_LR * (m_hat / (jnp.sqrt(v_hat) + ADAM_EPS) + ADAM_WD * w_ref[...])
    nm_ref[...] = nm
    nv_ref[...] = nv


def _adamw(w, g, m, v, name, tr):
    r, c = w.shape
    tr = min(tr, r)

    def body(w_ref, g_ref, m_ref, v_ref, d_ref, nm_ref, nv_ref):
        _adamw_update(w_ref, g_ref, m_ref, v_ref, d_ref, nm_ref, nv_ref)

    blk = pl.BlockSpec((tr, c), lambda i: (i, 0))
    return pl.pallas_call(
        body, name=name, grid=(r // tr,), in_specs=[blk] * 4, out_specs=[blk] * 3,
        out_shape=[jax.ShapeDtypeStruct((r, c), F32)] * 3, compiler_params=_cp())(w, g, m, v)


def _adamw_many(ws, gs, ms, vs, name):
    n = len(ws)

    def body(*refs):
        ins, outs = refs[:4 * n], refs[4 * n:]
        for k in range(n):
            _adamw_update(ins[k], ins[n + k], ins[2 * n + k], ins[3 * n + k], outs[k], outs[n + k], outs[2 * n + k])

    vmem = pl.BlockSpec(memory_space=pltpu.VMEM)
    res = pl.pallas_call(
        body, name=name, in_specs=[vmem] * (4 * n), out_specs=[vmem] * (3 * n),
        out_shape=[jax.ShapeDtypeStruct(a.shape, F32) for a in ws] * 3)(*ws, *gs, *ms, *vs)
    return res[:n], res[n:2 * n], res[2 * n:]


def _rope_tables(positions):
    inv_freq = ROPE_BASE ** (-jnp.arange(0, 64, 2, dtype=F32) / 64)
    ang = positions.astype(F32)[:, None] * inv_freq
    cos, sin = jnp.cos(ang), jnp.sin(ang)
    t = positions.shape[0]
    rc = jnp.concatenate([jnp.ones((t, 128), F32), cos, cos, jnp.ones((t, 64), F32)], axis=1)
    rs = jnp.concatenate([jnp.zeros((t, 128), F32), sin, sin, jnp.zeros((t, 64), F32)], axis=1)
    return rc, rs


def _layer_params(l, w_in, w_uq, w_ukv, w_out, small):
    p = {}
    p["w_in"] = jnp.concatenate([w_in[l][:, :1984], jnp.zeros((1024, 64), w_in.dtype), w_in[l][:, 1984:]], axis=1)
    p["w_uq"] = jnp.pad(w_uq[l].reshape(384, 4, 192), ((0, 0), (0, 0), (0, 64))).reshape(384, 1024)
    p["w_ukv"] = w_ukv[l].reshape(256, 4, 2, 128).transpose(0, 2, 1, 3).reshape(256, 1024)
    p["w_out"] = w_out[l]
    p["pre_g"] = small["pre_norm_g"][l][None]
    p["post_g"] = small["post_norm_g"][l][None]
    p["sgu_w"] = small["sgu_w"][l].reshape(512, 128)
    p["sgu_wt"] = small["sgu_w"][l].transpose(0, 2, 1).reshape(512, 128)
    p["sgu_bias"] = jnp.repeat(small["sgu_b"][l].T, 64, axis=1)
    p["ln_g"] = small["sgu_ln_g"][l][None]
    p["ln_b"] = small["sgu_ln_b"][l][None]
    p["pool_wbd"] = _mx(jax.scipy.linalg.block_diag(*[small["pool_w"][l][gi] for gi in range(4)]))
    p["pool_scale"] = small["pool_scale"][l][None]
    p["gq"] = small["q_norm_g"][l][None]
    p["gkv"] = small["kv_norm_g"][l][None]
    return p


def _layer_fwd(l, x, p, rc, rs, target):
    z, h = _in_proj_fwd(x, p["pre_g"], p["w_in"], f"in_proj_fwd_{l}")
    yab = _mix_fwd(z, p["sgu_w"], p["sgu_bias"], p["ln_g"], p["ln_b"], p["pool_wbd"], p["pool_scale"], f"mix_fwd_{l}")
    qh, kh, vh = _qkv_fwd(z, rc, rs, p["w_uq"], p["w_ukv"], p["gq"], p["gkv"], f"qkv_fwd_{l}")
    o, yc, lse = _attn_fwd(qh, kh, vh, z, f"attn_fwd_{l}")
    outs = _out_proj_fwd(yab, yc, p["w_out"], x, p["post_g"], target, f"out_proj_fwd_{l}")
    saved = dict(x=x, z=z, h=h, yab=yab, qh=qh, kh=kh, vh=vh, o=o, yc=yc, lse=lse, y=outs[0])
    return saved, outs[1:]


def _layer_bwd(l, dout, sv, p, rc, rs):
    dycat, dw_out, dpost = _out_proj_bwd(dout, sv["y"], sv["yab"], sv["yc"], p["w_out"], p["post_g"], f"out_proj_bwd_{l}")
    dq, dgate, dk, dv = _attn_bwd(sv["qh"], sv["kh"], sv["vh"], sv["o"], sv["lse"], dycat, sv["z"], f"attn_bwd_{l}")
    dzc, dzk, dwq, dwkv, dgq, dgkv = _qkv_bwd(dq, dk, dv, sv["z"], rc, rs, p["w_uq"], p["w_ukv"], p["gq"], p["gkv"],
                                              f"qkv_bwd_{l}")
    dzm, dsw, dsb, dlng, dlnb, dpw, dps = _mix_bwd(sv["z"], dycat, p["sgu_w"], p["sgu_wt"], p["sgu_bias"], p["ln_g"],
                                                   p["ln_b"], p["pool_wbd"], p["pool_scale"], f"mix_bwd_{l}")
    dx, dw_in, dpre = _in_proj_bwd(dzm, dzc, dzk, dgate, sv["h"], sv["x"], dout, p["w_in"], p["pre_g"], f"in_proj_bwd_{l}")
    grads = {
        "pre_norm_g": dpre[0], "post_norm_g": dpost[0],
        "w_in": jnp.concatenate([dw_in[:, :1984], dw_in[:, 2048:]], axis=1),
        "sgu_w": dsw.reshape(4, 128, 128), "sgu_b": dsb[:, :4].T, "sgu_ln_g": dlng[0], "sgu_ln_b": dlnb[0],
        "pool_w": jnp.stack([dpw[64 * gi:64 * gi + 64, 64 * gi:64 * gi + 64] for gi in range(4)]),
        "pool_scale": dps[0], "q_norm_g": dgq[0],
        "w_uq": dwq.reshape(384, 4, 256)[:, :, :192].reshape(384, 768), "kv_norm_g": dgkv[0],
        "w_ukv": dwkv.reshape(256, 2, 4, 128).transpose(0, 2, 1, 3).reshape(256, 1024), "w_out": dw_out,
    }
    return dx, grads


SMALL_NAMES = ["pre_norm_g", "post_norm_g", "sgu_w", "sgu_b", "sgu_ln_g", "sgu_ln_b", "pool_w", "pool_scale",
               "q_norm_g", "kv_norm_g"]
BIG_NAMES = ["w_in", "w_uq", "w_ukv", "w_out"]
WEIGHT_NAMES = ["pre_norm_g", "post_norm_g", "w_in", "sgu_w", "sgu_b", "sgu_ln_g", "sgu_ln_b", "pool_w", "pool_scale",
                "q_norm_g", "w_uq", "kv_norm_g", "w_ukv", "w_out"]


def _local_step(x, positions, target, w_in, w_uq, w_ukv, w_out, small):
    rc, rs = _rope_tables(positions)
    params = [_layer_params(l, w_in, w_uq, w_ukv, w_out, small) for l in range(DEPTH)]
    saved = []
    for l in range(DEPTH):
        sv, outs = _layer_fwd(l, x, params[l], rc, rs, target if l == DEPTH - 1 else None)
        saved.append(sv)
        if l < DEPTH - 1:
            x = outs[0]
    dout, loss = outs
    grads = [None] * DEPTH
    for l in reversed(range(DEPTH)):
        dout, grads[l] = _layer_bwd(l, dout, saved[l], params[l], rc, rs)
    return loss[0, 0], dout, {k: jnp.stack([grads[l][k] for l in range(DEPTH)]) for k in WEIGHT_NAMES}


def _pack_small(tree, extra=None):
    pieces = [tree[k].reshape(-1) for k in SMALL_NAMES]
    pieces.append(jnp.zeros((1,), F32) if extra is None else extra.reshape(1))
    flat = jnp.concatenate(pieces)
    rows = -(-flat.shape[0] // 1024) * 8
    return jnp.pad(flat, (0, rows * 128 - flat.shape[0])).reshape(rows, 128)


def _unpack_small(packed, like):
    flat = packed.reshape(-1)
    out, off = {}, 0
    for k in SMALL_NAMES:
        size = like[k].size
        out[k] = flat[off:off + size].reshape(like[k].shape)
        off += size
    return out, flat[off]


def kernel(x, positions, pre_norm_g, post_norm_g, w_in, sgu_w, sgu_b, sgu_ln_g, sgu_ln_b, pool_w, pool_scale, q_norm_g, w_uq, kv_norm_g, w_ukv, w_out, loss_target, m_pre_norm_g, m_post_norm_g, m_w_in, m_sgu_w, m_sgu_b, m_sgu_ln_g, m_sgu_ln_b, m_pool_w, m_pool_scale, m_q_norm_g, m_w_uq, m_kv_norm_g, m_w_ukv, m_w_out, v_pre_norm_g, v_post_norm_g, v_w_in, v_sgu_w, v_sgu_b, v_sgu_ln_g, v_sgu_ln_b, v_pool_w, v_pool_scale, v_q_norm_g, v_w_uq, v_kv_norm_g, v_w_ukv, v_w_out):
    w = dict(pre_norm_g=pre_norm_g, post_norm_g=post_norm_g, w_in=w_in, sgu_w=sgu_w, sgu_b=sgu_b, sgu_ln_g=sgu_ln_g,
             sgu_ln_b=sgu_ln_b, pool_w=pool_w, pool_scale=pool_scale, q_norm_g=q_norm_g, w_uq=w_uq, kv_norm_g=kv_norm_g,
             w_ukv=w_ukv, w_out=w_out)
    m = dict(pre_norm_g=m_pre_norm_g, post_norm_g=m_post_norm_g, w_in=m_w_in, sgu_w=m_sgu_w, sgu_b=m_sgu_b,
             sgu_ln_g=m_sgu_ln_g, sgu_ln_b=m_sgu_ln_b, pool_w=m_pool_w, pool_scale=m_pool_scale, q_norm_g=m_q_norm_g,
             w_uq=m_w_uq, kv_norm_g=m_kv_norm_g, w_ukv=m_w_ukv, w_out=m_w_out)
    v = dict(pre_norm_g=v_pre_norm_g, post_norm_g=v_post_norm_g, w_in=v_w_in, sgu_w=v_sgu_w, sgu_b=v_sgu_b,
             sgu_ln_g=v_sgu_ln_g, sgu_ln_b=v_sgu_ln_b, pool_w=v_pool_w, pool_scale=v_pool_scale, q_norm_g=v_q_norm_g,
             w_uq=v_w_uq, kv_norm_g=v_kv_norm_g, w_ukv=v_w_ukv, w_out=v_w_out)

    core = lax.axis_index("c")
    chip = 2 * lax.axis_index("x") + lax.axis_index("y")
    shards = [_mx(w[k]) for k in BIG_NAMES]
    gathered = _gather_weights(shards)
    g_in, g_uq, g_ukv, g_out = [lax.dynamic_update_slice(g, s, (2 * chip, 0, 0)) for g, s in zip(gathered, shards)]
    cols = lambda g: g.reshape((4, 2) + g.shape[1:]).transpose(1, 2, 0, 3).reshape(2, g.shape[1], 4 * g.shape[2])
    full_out = g_out.reshape(4, 2, 256, 1024).transpose(1, 0, 2, 3).reshape(2, 1024, 1024)
    loss, dx, grads = _local_step(x[0], positions[0], loss_target[0], cols(g_in), cols(g_uq), cols(g_ukv), full_out, w)

    split_cols = lambda g: g.reshape(2, g.shape[1], 4, g.shape[2] // 4).transpose(2, 0, 1, 3).reshape(8, g.shape[1], g.shape[2] // 4)
    parts = [split_cols(grads["w_in"]), split_cols(grads["w_uq"]), split_cols(grads["w_ukv"]),
             grads["w_out"].reshape(2, 4, 256, 1024).transpose(1, 0, 2, 3).reshape(8, 256, 1024)]
    common = _pack_small(grads, loss)
    got = _pair_exchange(parts, common)
    pair_sums = [_pair_sum(parts[a], got[a], f"pair_sum_{BIG_NAMES[a]}", 128) for a in range(4)]
    chip_common = _sum_leading(jnp.stack([common, got[4]]), "pair_sum_small", common.shape[0])
    received = _chip_exchange([ps[1] for ps in pair_sums], chip_common)
    sums = [_sum_chips(pair_sums[a][0], received[a], f"sum_{BIG_NAMES[a]}", 128) for a in range(4)]
    all_common = lax.dynamic_update_slice(received[4], chip_common[None], (chip, 0, 0))
    small_sum, loss = _unpack_small(_sum_leading(all_common, "sum_small", all_common.shape[1]), w)
    others = _sibling_exchange(sums)
    total = dict(small_sum)
    for a, k in enumerate(BIG_NAMES):
        total[k] = jnp.where(core == 0, jnp.stack([sums[a], others[a]]), jnp.stack([others[a], sums[a]]))

    rows2d = lambda a: a.reshape(-1, a.shape[-1])
    small_out = _adamw_many(*[[rows2d(tree[k]) for k in SMALL_NAMES] for tree in (w, total, m, v)], "adamw_small")
    delta, new_m, new_v = ({k: r.reshape(w[k].shape) for k, r in zip(SMALL_NAMES, res)} for res in small_out)
    for k in BIG_NAMES:
        shape = w[k].shape
        flat = lambda a: a.reshape(shape[0] * shape[1], shape[2])
        res = _adamw(flat(w[k]), flat(total[k]), flat(m[k]), flat(v[k]), f"adamw_{k}", 256)
        delta[k], new_m[k], new_v[k] = (r.reshape(shape) for r in res)

    return (loss, dx[None], *[total[k] for k in WEIGHT_NAMES], *[delta[k] for k in WEIGHT_NAMES],
            *[new_m[k] for k in WEIGHT_NAMES], *[new_v[k] for k in WEIGHT_NAMES])
```

```python
import jax
import jax.numpy as jnp
from jax import lax
from jax.experimental import pallas as pl
from jax.experimental.pallas import tpu as pltpu

F32 = jnp.float32
MXU_DTYPE = jnp.bfloat16
WIRE_DTYPE = jnp.bfloat16
EPS = 1e-6
NEG_INF = -1e30
CHUNK = 64
DEPTH = 2
N_HEADS = 4
QK_PAD = 256
V_DIM = 128
SCALE = 192 ** -0.5
LOG2E = 1.4426950408889634
ROPE_BASE = 10000.0
ADAM_LR, ADAM_B1, ADAM_B2, ADAM_EPS, ADAM_WD, ADAM_STEP = 0.001, 0.9, 0.999, 1e-08, 0.01, 10
VMEM_LIMIT_BYTES = 56 * 1024 * 1024
MESH = pl.DeviceIdType.MESH
ANY = pl.BlockSpec(memory_space=pl.ANY)

Z_MIX, Z_C, Z_KR, Z_GATE = 1280, 640, 128, 512
Z_W = Z_MIX + Z_C + Z_KR + Z_GATE


def _cp(n_axes=1):
    return pltpu.CompilerParams(dimension_semantics=("arbitrary",) * n_axes, vmem_limit_bytes=VMEM_LIMIT_BYTES)


def _dot(a, b):
    return lax.dot_general(a, b, (((1,), (0,)), ((), ())), preferred_element_type=F32)


def _dot_nt(a, b):
    return lax.dot_general(a, b, (((1,), (1,)), ((), ())), preferred_element_type=F32)


def _dot_tn(a, b):
    return lax.dot_general(a, b, (((0,), (0,)), ((), ())), preferred_element_type=F32)


def _mx(a):
    return a.astype(MXU_DTYPE)


def _silu_and_grad(g):
    sg = jax.nn.sigmoid(g)
    return g * sg, sg * (1.0 + g * (1.0 - sg))


def _rms(x, g):
    r = lax.rsqrt(jnp.mean(x * x, axis=-1, keepdims=True) + EPS)
    return x * r * g, r


def _rms_bwd(x, r, g, dy):
    xhat = x * r
    dyg = dy * g
    dx = r * (dyg - xhat * jnp.mean(dyg * xhat, axis=-1, keepdims=True))
    return dx, dy * xhat


def _zero_when(first, *refs):
    @pl.when(first)
    def _():
        for ref in refs:
            ref[...] = jnp.zeros(ref.shape, ref.dtype)


def _acc(ref, val):
    ref[...] += val


def _colsum(a):
    return jnp.sum(a, axis=0, keepdims=True)


def _in_proj_fwd(x, g, w, name, tm=512):
    t, d = x.shape
    n = w.shape[1]
    tm = min(tm, t)

    def body(x_ref, g_ref, w_ref, z_ref, h_ref):
        h, _ = _rms(x_ref[...], g_ref[...])
        h = _mx(h)
        h_ref[...] = h
        z_ref[...] = _dot(h, w_ref[...])

    return pl.pallas_call(
        body, name=name, grid=(t // tm,),
        in_specs=[pl.BlockSpec((tm, d), lambda i: (i, 0)), pl.BlockSpec((1, d), lambda i: (0, 0)),
                  pl.BlockSpec((d, n), lambda i: (0, 0))],
        out_specs=[pl.BlockSpec((tm, n), lambda i: (i, 0)), pl.BlockSpec((tm, d), lambda i: (i, 0))],
        out_shape=[jax.ShapeDtypeStruct((t, n), F32), jax.ShapeDtypeStruct((t, d), MXU_DTYPE)],
        compiler_params=_cp())(x, g, w)


def _in_proj_bwd(dz_mix, dz_c, dz_kr, dz_gate, h, x, d_res, w, g, name, tm=512):
    t, d = x.shape
    n = w.shape[1]
    tm = min(tm, t)

    def body(dm_ref, dc_ref, dk_ref, dg_ref, h_ref, x_ref, dres_ref, w_ref, g_ref, dx_ref, dw_ref, dgn_ref):
        first = pl.program_id(0) == 0
        dz = jnp.concatenate([dm_ref[...], dc_ref[...], dk_ref[...], dg_ref[...]], axis=1)

        _zero_when(first, dw_ref, dgn_ref)
        dh = _dot_nt(dz, w_ref[...])
        hb = h_ref[...]
        for c0 in range(0, n, 512):
            dw_ref[:, c0:c0 + 512] += _dot_tn(hb, dz[:, c0:c0 + 512])
        xf = x_ref[...]
        r = lax.rsqrt(jnp.mean(xf * xf, axis=-1, keepdims=True) + EPS)
        dx, dgt = _rms_bwd(xf, r, g_ref[...], dh)
        dx_ref[...] = dx + dres_ref[...]
        _acc(dgn_ref, _colsum(dgt))

    row = lambda wd: pl.BlockSpec((tm, wd), lambda i: (i, 0))
    fixed = lambda a, b: pl.BlockSpec((a, b), lambda i: (0, 0), pipeline_mode=pl.Buffered(1))
    return pl.pallas_call(
        body, name=name, grid=(t // tm,),
        in_specs=[row(Z_MIX), row(Z_C), row(Z_KR), row(Z_GATE), row(d), row(d), row(d), fixed(d, n), fixed(1, d)],
        out_specs=[row(d), fixed(d, n), fixed(1, d)],
        out_shape=[jax.ShapeDtypeStruct((t, d), F32), jax.ShapeDtypeStruct((d, n), F32),
                   jax.ShapeDtypeStruct((1, d), F32)],
        compiler_params=_cp())(dz_mix, dz_c, dz_kr, dz_gate, h, x, d_res, w, g)


def _lane_group(shape):
    return lax.broadcasted_iota(jnp.int32, shape, 1) // 64


def _select_group(vals):
    grp = _lane_group(vals[0].shape)
    out = vals[3]
    for gi in (2, 1, 0):
        out = jnp.where(grp == gi, vals[gi], out)
    return out


def _sgu_mask(transposed):
    r = (lax.broadcasted_iota(jnp.int32, (512, 128), 0) % 128) // CHUNK
    c = lax.broadcasted_iota(jnp.int32, (512, 128), 1) // CHUNK
    return (r <= c) if transposed else (c <= r)


def _sgu_apply(wstack, vb, nblk):
    outs = []
    for n in range(nblk):
        r = _dot(wstack, vb[n * 128:(n + 1) * 128, :])
        outs.append(_select_group([r[hh * 128:(hh + 1) * 128, :] for hh in range(4)]))
    return jnp.concatenate(outs, axis=0)


def _layer_norm(v, g, b):
    mu = jnp.mean(v, axis=-1, keepdims=True)
    vc = v - mu
    rstd = lax.rsqrt(jnp.mean(vc * vc, axis=-1, keepdims=True) + EPS)
    vhat = vc * rstd
    return vhat * g + b, vhat, rstd


def _pool_counts(t0, n):
    t = t0 + lax.broadcasted_iota(jnp.int32, (n, 256), 0)
    w = _select_group([jnp.full((n, 256), wv, jnp.int32) for wv in (2, 4, 8, 16)])
    return jnp.minimum(t + 1, w).astype(F32)


def _pooled(p, halo, t0):
    tm = p.shape[0]
    ext = jnp.concatenate([halo, p], axis=0)
    s2 = ext + pltpu.roll(ext, 1, 0)
    s4 = s2 + pltpu.roll(s2, 2, 0)
    s8 = s4 + pltpu.roll(s4, 4, 0)
    s16 = s8 + pltpu.roll(s8, 8, 0)
    sel = _select_group([s2, s4, s8, s16])[16:, :]
    return sel / _pool_counts(t0, tm) - p


def _pooled_bwd(dpool, dpool_halo, t0):
    tm = dpool.shape[0]
    n = tm + 16
    ext = jnp.concatenate([dpool, dpool_halo], axis=0) / _pool_counts(t0, n)
    f2 = ext + pltpu.roll(ext, n - 1, 0)
    f4 = f2 + pltpu.roll(f2, n - 2, 0)
    f8 = f4 + pltpu.roll(f4, n - 4, 0)
    f16 = f8 + pltpu.roll(f8, n - 8, 0)
    return _select_group([f2, f4, f8, f16])[:tm, :] - dpool


def _mix_specs(t, tm):
    nt16 = t // 16
    zrow = pl.BlockSpec((tm, Z_MIX), lambda i: (i, 0))
    prev_halo = pl.BlockSpec((16, 256), lambda i: (jnp.maximum(i * (tm // 16) - 1, 0), 3))
    fixed = lambda a, b: pl.BlockSpec((a, b), lambda i: (0, 0))
    params = [fixed(512, 128), fixed(128, 256), fixed(1, 256), fixed(1, 256), fixed(256, 256), fixed(1, 256)]
    return nt16, zrow, prev_halo, fixed, params


def _mix_fwd(z, sgu_w, sgu_bias, ln_g, ln_b, pool_wbd, pool_scale, name, tm=512):
    t = z.shape[0]
    tm = min(tm, t)
    _, zrow, prev_halo, _, params = _mix_specs(t, tm)

    def body(z_ref, halo_ref, w_ref, bias_ref, lng_ref, lnb_ref, pw_ref, ps_ref, y_ref):
        i = pl.program_id(0)
        u, v, gate = z_ref[:, 0:256], z_ref[:, 256:512], z_ref[:, 512:768]
        p, pgate = z_ref[:, 768:1024], z_ref[:, 1024:1280]
        vn, _, _ = _layer_norm(v, lng_ref[...], lnb_ref[...])
        wm = _mx(jnp.where(_sgu_mask(False), w_ref[...], 0.0))
        halo = jnp.where(i > 0, halo_ref[...], 0.0)
        pooled = _pooled(p, halo, i * tm)
        mixed = _sgu_apply(wm, _mx(vn), tm // 128) + jnp.tile(bias_ref[...], (tm // 128, 1))
        mixedp = _dot(_mx(pooled), pw_ref[...])
        ya = u * mixed * _silu_and_grad(gate)[0]
        yb = mixedp * ps_ref[...] * _silu_and_grad(pgate)[0]
        y_ref[...] = _mx(jnp.concatenate([ya, yb], axis=1))

    return pl.pallas_call(
        body, name=name, grid=(t // tm,),
        in_specs=[zrow, prev_halo] + params,
        out_specs=pl.BlockSpec((tm, 512), lambda i: (i, 0)),
        out_shape=jax.ShapeDtypeStruct((t, 512), MXU_DTYPE),
        compiler_params=_cp())(z, z, sgu_w, sgu_bias, ln_g, ln_b, pool_wbd, pool_scale)


def _mix_bwd(z, dycat, sgu_w, sgu_wt, sgu_bias, ln_g, ln_b, pool_wbd, pool_scale, name, tm=512):
    t = z.shape[0]
    tm = min(tm, t)
    nt16, zrow, prev_halo, fixed, params = _mix_specs(t, tm)
    nblk = tm // 128
    last = t // tm - 1

    def body(z_ref, halo_ref, zn_ref, dy_ref, dyn_ref, w_ref, wt_ref, bias_ref, lng_ref, lnb_ref, pw_ref, ps_ref,
             dz_ref, dw_ref, db_ref, dlng_ref, dlnb_ref, dpw_ref, dps_ref):
        i = pl.program_id(0)
        _zero_when(i == 0, dw_ref, db_ref, dlng_ref, dlnb_ref, dpw_ref, dps_ref)
        u, v, gate = z_ref[:, 0:256], z_ref[:, 256:512], z_ref[:, 512:768]
        p, pgate = z_ref[:, 768:1024], z_ref[:, 1024:1280]
        dya, dyb = dy_ref[:, 0:256], dy_ref[:, 256:512]
        vn, vhat, rstd = _layer_norm(v, lng_ref[...], lnb_ref[...])
        vnb = _mx(vn)
        wm = _mx(jnp.where(_sgu_mask(False), w_ref[...], 0.0))
        wmt = _mx(jnp.where(_sgu_mask(True), wt_ref[...], 0.0))
        halo = jnp.where(i > 0, halo_ref[...], 0.0)
        pooled_b = _mx(_pooled(p, halo, i * tm))
        silu, dsilu = _silu_and_grad(gate)
        psilu, pdsilu = _silu_and_grad(pgate)
        mixed = _sgu_apply(wm, vnb, nblk) + jnp.tile(bias_ref[...], (nblk, 1))
        mixedp = _dot(pooled_b, pw_ref[...])
        t1 = u * mixed
        d_gate = dya * t1 * dsilu
        d_t1 = dya * silu
        d_u = d_t1 * mixed
        d_mixed = d_t1 * u
        dmb = _mx(d_mixed)
        d_pgate = dyb * (mixedp * ps_ref[...]) * pdsilu
        d_ms = dyb * psilu
        _acc(dps_ref, _colsum(d_ms * mixedp))
        dmpb = _mx(d_ms * ps_ref[...])
        dmp_halo = _mx(dyn_ref[...] * _silu_and_grad(zn_ref[...])[0] * ps_ref[...])
        d_vn = _sgu_apply(wmt, dmb, nblk)
        grp = _lane_group((128, 256))
        lane = lax.broadcasted_iota(jnp.int32, (128, 128), 1)
        dws = [jnp.zeros((128, 128), F32) for _ in range(4)]
        dbias = jnp.zeros((128, 128), F32)
        for n in range(nblk):
            dm_n, dmb_n, vnb_n = d_mixed[n * 128:(n + 1) * 128], dmb[n * 128:(n + 1) * 128], vnb[n * 128:(n + 1) * 128]
            for hh in range(4):
                dws[hh] = dws[hh] + _dot_nt(jnp.where(grp == hh, dmb_n, jnp.zeros_like(dmb_n)), vnb_n)
                rs = jnp.sum(jnp.where(grp == hh, dm_n, 0.0), axis=-1, keepdims=True)
                dbias = dbias + jnp.where(lane == hh, rs, 0.0)
        _acc(dw_ref, jnp.concatenate(dws, axis=0))
        _acc(db_ref, dbias)
        _acc(dpw_ref, _dot_tn(pooled_b, dmpb))
        d_pooled = _dot_nt(dmpb, pw_ref[...])
        d_pooled_halo = jnp.where(i < last, _dot_nt(dmp_halo, pw_ref[...]), 0.0)
        _acc(dlng_ref, _colsum(d_vn * vhat))
        _acc(dlnb_ref, _colsum(d_vn))
        dvh = d_vn * lng_ref[...]
        d_v = rstd * (dvh - jnp.mean(dvh, axis=-1, keepdims=True) - vhat * jnp.mean(dvh * vhat, axis=-1, keepdims=True))
        d_p = _pooled_bwd(d_pooled, d_pooled_halo, i * tm)
        dz_ref[...] = _mx(jnp.concatenate([d_u, d_v, d_gate, d_p, d_pgate], axis=1))

        @pl.when(i == last)
        def _():
            dw_ref[...] = jnp.where(_sgu_mask(False), dw_ref[...], 0.0)

    nxt = lambda i: jnp.minimum((i + 1) * (tm // 16), nt16 - 1)
    return pl.pallas_call(
        body, name=name, grid=(t // tm,),
        in_specs=[zrow, prev_halo, pl.BlockSpec((16, 256), lambda i: (nxt(i), 4)),
                  pl.BlockSpec((tm, 512), lambda i: (i, 0)), pl.BlockSpec((16, 256), lambda i: (nxt(i), 1)),
                  params[0], fixed(512, 128)] + params[1:],
        out_specs=[pl.BlockSpec((tm, Z_MIX), lambda i: (i, 0)), fixed(512, 128), fixed(128, 128), fixed(1, 256),
                   fixed(1, 256), fixed(256, 256), fixed(1, 256)],
        out_shape=[jax.ShapeDtypeStruct((t, Z_MIX), MXU_DTYPE), jax.ShapeDtypeStruct((512, 128), F32),
                   jax.ShapeDtypeStruct((128, 128), F32), jax.ShapeDtypeStruct((1, 256), F32),
                   jax.ShapeDtypeStruct((1, 256), F32), jax.ShapeDtypeStruct((256, 256), F32),
                   jax.ShapeDtypeStruct((1, 256), F32)],
        compiler_params=_cp())(z, z, z, dycat, dycat, sgu_w, sgu_wt, sgu_bias, ln_g, ln_b, pool_wbd, pool_scale)


def _rot_half(x, transpose):
    w = x.shape[1]
    lane = lax.broadcasted_iota(jnp.int32, x.shape, 1) % min(w, 256)
    base = 128 if w >= 256 else 0
    lo = jnp.logical_and(lane >= base, lane < base + 32)
    hi = jnp.logical_and(lane >= base + 32, lane < base + 64)
    up = pltpu.roll(x, w - 32, 1)
    down = pltpu.roll(x, 32, 1)
    if transpose:
        return jnp.where(lo, up, jnp.where(hi, -down, 0.0))
    return jnp.where(lo, -up, jnp.where(hi, down, 0.0))


def _rope(x, c, s):
    return x * c + _rot_half(x, False) * s


def _rope_bwd(dy, c, s):
    return dy * c + _rot_half(dy * s, True)


def _qkv_fwd(z, rc, rs, w_uq, w_ukv, gq, gkv, name, tm=512):
    t = z.shape[0]
    tm = min(tm, t)

    def body(zc_ref, zk_ref, rc_ref, rs_ref, wq_ref, wkv_ref, gq_ref, gkv_ref, q_ref, k_ref, v_ref):
        cq, ckv = zc_ref[:, 0:384], zc_ref[:, 384:640]
        c, s = rc_ref[...], rs_ref[...]
        qn, _ = _rms(cq, gq_ref[...])
        kvn, _ = _rms(ckv, gkv_ref[...])
        q_pre = _dot(_mx(qn), wq_ref[...])
        kv = _dot(_mx(kvn), wkv_ref[...])
        kpe = _rope(zk_ref[...], c[:, 128:256], s[:, 128:256])
        q = _rope(q_pre, jnp.tile(c, (1, N_HEADS)), jnp.tile(s, (1, N_HEADS)))
        for hh in range(N_HEADS):
            q_ref[hh] = _mx(q[:, hh * QK_PAD:(hh + 1) * QK_PAD])
            k_ref[hh] = _mx(jnp.concatenate([kv[:, hh * 128:(hh + 1) * 128], kpe], axis=1))
            v_ref[hh] = _mx(kv[:, 512 + hh * 128:512 + (hh + 1) * 128])

    fixed = lambda a, b: pl.BlockSpec((a, b), lambda i: (0, 0))
    heads = lambda wd: pl.BlockSpec((N_HEADS, tm, wd), lambda i: (0, i, 0))
    return pl.pallas_call(
        body, name=name, grid=(t // tm,),
        in_specs=[pl.BlockSpec((tm, Z_C), lambda i: (i, Z_MIX // Z_C)),
                  pl.BlockSpec((tm, Z_KR), lambda i: (i, (Z_MIX + Z_C) // Z_KR)),
                  pl.BlockSpec((tm, 256), lambda i: (i, 0)), pl.BlockSpec((tm, 256), lambda i: (i, 0)),
                  fixed(384, 1024), fixed(256, 1024), fixed(1, 384), fixed(1, 256)],
        out_specs=[heads(QK_PAD), heads(QK_PAD), heads(V_DIM)],
        out_shape=[jax.ShapeDtypeStruct((N_HEADS, t, QK_PAD), MXU_DTYPE),
                   jax.ShapeDtypeStruct((N_HEADS, t, QK_PAD), MXU_DTYPE),
                   jax.ShapeDtypeStruct((N_HEADS, t, V_DIM), MXU_DTYPE)],
        compiler_params=_cp())(z, z, rc, rs, w_uq, w_ukv, gq, gkv)


def _qkv_bwd(dq, dk, dv, z, rc, rs, w_uq, w_ukv, gq, gkv, name, tm=512):
    t = z.shape[0]
    tm = min(tm, t)

    def body(dq_ref, dk_ref, dv_ref, zc_ref, rc_ref, rs_ref, wq_ref, wkv_ref, gq_ref, gkv_ref,
             dzc_ref, dzk_ref, dwq_ref, dwkv_ref, dgq_ref, dgkv_ref):
        _zero_when(pl.program_id(0) == 0, dwq_ref, dwkv_ref, dgq_ref, dgkv_ref)
        cq, ckv = zc_ref[:, 0:384], zc_ref[:, 384:640]
        c, s = rc_ref[...], rs_ref[...]
        dq_all = jnp.concatenate([dq_ref[hh] for hh in range(N_HEADS)], axis=1)
        dqp = _mx(_rope_bwd(dq_all, jnp.tile(c, (1, N_HEADS)), jnp.tile(s, (1, N_HEADS))))
        qn, rq = _rms(cq, gq_ref[...])
        _acc(dwq_ref, _dot_tn(_mx(qn), dqp))
        d_qn = _dot_nt(dqp, wq_ref[...])
        dkpe = dk_ref[0][:, 128:256]
        for hh in range(1, N_HEADS):
            dkpe = dkpe + dk_ref[hh][:, 128:256]
        dzk_ref[...] = _mx(_rope_bwd(dkpe, c[:, 128:256], s[:, 128:256]))
        dkv = _mx(jnp.concatenate([dk_ref[hh][:, 0:128] for hh in range(N_HEADS)]
                                  + [dv_ref[hh] for hh in range(N_HEADS)], axis=1))
        kvn, rkv = _rms(ckv, gkv_ref[...])
        _acc(dwkv_ref, _dot_tn(_mx(kvn), dkv))
        d_kvn = _dot_nt(dkv, wkv_ref[...])
        d_cq, dgq_t = _rms_bwd(cq, rq, gq_ref[...], d_qn)
        _acc(dgq_ref, _colsum(dgq_t))
        d_ckv, dgkv_t = _rms_bwd(ckv, rkv, gkv_ref[...], d_kvn)
        _acc(dgkv_ref, _colsum(dgkv_t))
        dzc_ref[...] = _mx(jnp.concatenate([d_cq, d_ckv], axis=1))

    fixed = lambda a, b: pl.BlockSpec((a, b), lambda i: (0, 0))
    heads = lambda wd: pl.BlockSpec((N_HEADS, tm, wd), lambda i: (0, i, 0))
    return pl.pallas_call(
        body, name=name, grid=(t // tm,),
        in_specs=[heads(QK_PAD), heads(QK_PAD), heads(V_DIM), pl.BlockSpec((tm, Z_C), lambda i: (i, Z_MIX // Z_C)),
                  pl.BlockSpec((tm, 256), lambda i: (i, 0)), pl.BlockSpec((tm, 256), lambda i: (i, 0)),
                  fixed(384, 1024), fixed(256, 1024), fixed(1, 384), fixed(1, 256)],
        out_specs=[pl.BlockSpec((tm, Z_C), lambda i: (i, 0)), pl.BlockSpec((tm, Z_KR), lambda i: (i, 0)),
                   fixed(384, 1024), fixed(256, 1024), fixed(1, 384), fixed(1, 256)],
        out_shape=[jax.ShapeDtypeStruct((t, Z_C), MXU_DTYPE), jax.ShapeDtypeStruct((t, Z_KR), MXU_DTYPE),
                   jax.ShapeDtypeStruct((384, 1024), F32), jax.ShapeDtypeStruct((256, 1024), F32),
                   jax.ShapeDtypeStruct((1, 384), F32), jax.ShapeDtypeStruct((1, 256), F32)],
        compiler_params=_cp())(dq, dk, dv, z, rc, rs, w_uq, w_ukv, gq, gkv)


def _loop_in_long_trips(n, body, longest=4):
    def doubled(inner):
        return lambda t, carry: inner(2 * t + 1, inner(2 * t, carry))

    trips = [body]
    while 2 ** (len(trips) - 1) < longest:
        trips.append(doubled(trips[-1]))
    done = 0
    for level in reversed(range(len(trips))):
        size = 2 ** level
        end = n // size
        lax.fori_loop(done, end, trips[level], 0)
        done = 2 * end if level else end


def _init_mask_bias(bias_ref):
    _, tq, tk = bias_ref.shape
    r = lax.broadcasted_iota(jnp.int32, (tq, tk), 0) // CHUNK
    c = lax.broadcasted_iota(jnp.int32, (tq, tk), 1) // CHUNK
    bias_ref[0] = jnp.zeros((tq, tk), F32)
    for d in range(tq // tk):
        bias_ref[1 + d] = jnp.where(c + d * (tk // CHUNK) <= r, 0.0, NEG_INF)


def _gate_block(tq):
    return pl.BlockSpec((tq, 128), lambda h, i: (i, (Z_MIX + Z_C + Z_KR) // 128 + h))


def _attn_fwd(qh, kh, vh, z, name, tq=1024, tk=512):
    t = qh.shape[1]
    tq = min(tq, t)
    tk = min(tk, tq)
    ratio = tq // tk

    def body(q_ref, g_ref, k_hbm, v_hbm, o_ref, yc_ref, lse_ref, k_v, v_v, m_s, acc_s, s_a, s_b, mx_a, mx_b, bias_s,
             sem):
        h, i = pl.program_id(0), pl.program_id(1)

        @pl.when(i == 0)
        def _():
            ck = pltpu.make_async_copy(k_hbm.at[h], k_v, sem.at[0])
            cv = pltpu.make_async_copy(v_hbm.at[h], v_v.at[:, 0:V_DIM], sem.at[1])
            ck.start()
            cv.start()
            v_v[:, V_DIM:2 * V_DIM] = jnp.ones((t, V_DIM), MXU_DTYPE)
            _init_mask_bias(bias_s)
            ck.wait()
            cv.wait()

        q = q_ref[...]
        m_s[...] = jnp.full(m_s.shape, NEG_INF, F32)
        acc_s[...] = jnp.zeros(acc_s.shape, F32)

        last = ratio * (i + 1) - 1

        def keys(j):
            return pl.ds(pl.multiple_of(j * tk, tk), tk)

        def scores(s_ref, mx_ref, j, biased):
            s = _dot_nt(q, k_v[keys(j), :]) * (SCALE * LOG2E)
            if biased:
                s = s + bias_s[jnp.maximum(j - ratio * i + 1, 0)]
            s_ref[...] = s
            mx_ref[...] = jnp.broadcast_to(jnp.max(s, axis=-1, keepdims=True), mx_ref.shape)

        def softmax_pv(s_ref, mx_ref, j):
            m_old = m_s[...]
            m_new = jnp.maximum(m_old, mx_ref[...])
            p = jnp.exp2(s_ref[...] - jnp.tile(m_new, (1, tk // 128)))
            alpha = jnp.exp2(m_old - m_new)
            m_s[...] = m_new
            acc_s[...] = jnp.tile(alpha, (1, 2)) * acc_s[...] + _dot(_mx(p), v_v[keys(j), :])

        scores(s_a, mx_a, 0, True)

        def pair(pp, carry, biased):
            scores(s_b, mx_b, 2 * pp + 1, biased)
            softmax_pv(s_a, mx_a, 2 * pp)
            scores(s_a, mx_a, jnp.minimum(2 * pp + 2, last), biased)
            softmax_pv(s_b, mx_b, 2 * pp + 1)
            return carry

        n_pairs = (last + 1) // 2
        n_plain = jnp.maximum(ratio * i // 2 - 1, 0)
        _loop_in_long_trips(n_plain, lambda pp, carry: pair(pp, carry, False))
        _loop_in_long_trips(n_pairs - n_plain, lambda pp, carry: pair(n_plain + pp, carry, True))
        if ratio % 2 == 1:
            @pl.when(last % 2 == 0)
            def _():
                softmax_pv(s_a, mx_a, last)

        l = acc_s[:, V_DIM:2 * V_DIM]
        o = acc_s[:, 0:V_DIM] / l
        o_ref[...] = o
        yc_ref[...] = _mx(o * _silu_and_grad(g_ref[...])[0])
        lse_ref[...] = m_s[...] + jnp.log2(l)

    return pl.pallas_call(
        body, name=name, grid=(N_HEADS, t // tq),
        in_specs=[pl.BlockSpec((None, tq, QK_PAD), lambda h, i: (h, i, 0)), _gate_block(tq), ANY, ANY],
        out_specs=[pl.BlockSpec((tq, 128), lambda h, i: (i, h)), pl.BlockSpec((tq, 128), lambda h, i: (i, h)),
                   pl.BlockSpec((None, tq, 128), lambda h, i: (h, i, 0))],
        out_shape=[jax.ShapeDtypeStruct((t, N_HEADS * V_DIM), F32), jax.ShapeDtypeStruct((t, N_HEADS * V_DIM), MXU_DTYPE),
                   jax.ShapeDtypeStruct((N_HEADS, t, 128), F32)],
        scratch_shapes=[pltpu.VMEM((t, QK_PAD), MXU_DTYPE), pltpu.VMEM((t, 2 * V_DIM), MXU_DTYPE),
                        pltpu.VMEM((tq, 128), F32), pltpu.VMEM((tq, 2 * V_DIM), F32),
                        pltpu.VMEM((tq, tk), F32), pltpu.VMEM((tq, tk), F32), pltpu.VMEM((tq, 128), F32),
                        pltpu.VMEM((tq, 128), F32), pltpu.VMEM((ratio + 1, tq, tk), F32),
                        pltpu.SemaphoreType.DMA((2,))],
        compiler_params=_cp(2))(qh, z, kh, vh)


def _attn_bwd(qh, kh, vh, o, lse, dycat, z, name, tq=512):
    t = qh.shape[1]
    tq = min(tq, t)
    nq = t // tq

    def body(q_ref, o_ref, lse_ref, dy_ref, g_ref, k_hbm, v_hbm, dq_ref, dgate_ref, dk_hbm, dv_hbm,
             k_v, v_v, dk_acc, dv_acc, dq_acc, delta_s, s_a, dp_a, s_b, dp_b, bias_s, sem):
        h, i = pl.program_id(0), pl.program_id(1)

        @pl.when(i == 0)
        def _():
            ck = pltpu.make_async_copy(k_hbm.at[h], k_v, sem.at[0])
            cv = pltpu.make_async_copy(v_hbm.at[h], v_v, sem.at[1])
            ck.start()
            cv.start()
            _init_mask_bias(bias_s)
            dk_acc[...] = jnp.zeros(dk_acc.shape, F32)
            dv_acc[...] = jnp.zeros(dv_acc.shape, F32)
            ck.wait()
            cv.wait()

        gate, dy, of = g_ref[...], dy_ref[...], o_ref[...]
        silu, dsilu = _silu_and_grad(gate)
        do = dy * silu
        delta = jnp.sum(do * of, axis=-1, keepdims=True)
        dgate_ref[...] = _mx(dy * of * dsilu)
        dob = _mx(do)
        q = q_ref[...]
        delta_s[...] = jnp.broadcast_to(delta, delta_s.shape)
        dq_acc[...] = jnp.zeros(dq_acc.shape, F32)

        def keys(j):
            return pl.ds(pl.multiple_of(j * tq, tq), tq)

        def scores(s_ref, dp_ref, j):
            s = _dot_nt(q, k_v[keys(j), :]) * (SCALE * LOG2E) + bias_s[(j == i).astype(jnp.int32)]
            s_ref[...] = s - jnp.tile(lse_ref[...], (1, tq // 128))
            dp_ref[...] = _dot_nt(dob, v_v[keys(j), :]) - jnp.tile(delta_s[...], (1, tq // 128))

        def grads(s_ref, dp_ref, j):
            ks = keys(j)
            p = jnp.exp2(s_ref[...])
            ds = p * dp_ref[...] * SCALE
            pb, dsb = _mx(p), _mx(ds)
            dq_acc[...] += _dot(dsb, k_v[ks, :])
            dk_acc[ks, :] += _dot_tn(dsb, q)
            dv_acc[ks, :] += _dot_tn(pb, dob)

        scores(s_a, dp_a, 0)

        def pair(pp, carry):
            scores(s_b, dp_b, 2 * pp + 1)
            grads(s_a, dp_a, 2 * pp)
            scores(s_a, dp_a, jnp.minimum(2 * pp + 2, i))
            grads(s_b, dp_b, 2 * pp + 1)
            return carry

        _loop_in_long_trips((i + 1) // 2, pair, longest=8)

        @pl.when(i % 2 == 0)
        def _():
            grads(s_a, dp_a, i)

        dq_ref[...] = dq_acc[...]

        @pl.when(i == nq - 1)
        def _():
            ck = pltpu.make_async_copy(dk_acc, dk_hbm.at[h], sem.at[0])
            cv = pltpu.make_async_copy(dv_acc, dv_hbm.at[h], sem.at[1])
            ck.start()
            cv.start()
            ck.wait()
            cv.wait()

    return pl.pallas_call(
        body, name=name, grid=(N_HEADS, nq),
        in_specs=[pl.BlockSpec((None, tq, QK_PAD), lambda h, i: (h, i, 0)),
                  pl.BlockSpec((tq, 128), lambda h, i: (i, h)),
                  pl.BlockSpec((None, tq, 128), lambda h, i: (h, i, 0)),
                  pl.BlockSpec((tq, 128), lambda h, i: (i, N_HEADS + h)), _gate_block(tq), ANY, ANY],
        out_specs=[pl.BlockSpec((None, tq, QK_PAD), lambda h, i: (h, i, 0)),
                   pl.BlockSpec((tq, 128), lambda h, i: (i, h)), ANY, ANY],
        out_shape=[jax.ShapeDtypeStruct((N_HEADS, t, QK_PAD), F32), jax.ShapeDtypeStruct((t, Z_GATE), MXU_DTYPE),
                   jax.ShapeDtypeStruct((N_HEADS, t, QK_PAD), F32), jax.ShapeDtypeStruct((N_HEADS, t, V_DIM), F32)],
        scratch_shapes=[pltpu.VMEM((t, QK_PAD), MXU_DTYPE), pltpu.VMEM((t, V_DIM), MXU_DTYPE),
                        pltpu.VMEM((t, QK_PAD), F32), pltpu.VMEM((t, V_DIM), F32), pltpu.VMEM((tq, QK_PAD), F32),
                        pltpu.VMEM((tq, 128), F32)] + [pltpu.VMEM((tq, tq), F32)] * 4
        + [pltpu.VMEM((2, tq, tq), F32), pltpu.SemaphoreType.DMA((2,))],
        compiler_params=_cp(2))(qh, o, lse, dycat, z, kh, vh)


def _out_proj_fwd(yab, yc, w, x, g, target, name, tm=512):
    t, d = x.shape
    tm = min(tm, t)
    is_last = target is not None

    def body(*refs):
        if is_last:
            yab_ref, yc_ref, w_ref, x_ref, g_ref, t_ref, y_ref, dout_ref, loss_ref = refs
            _zero_when(pl.program_id(0) == 0, loss_ref)
        else:
            yab_ref, yc_ref, w_ref, x_ref, g_ref, y_ref, out_ref = refs
        y = _dot(jnp.concatenate([yab_ref[...], yc_ref[...]], axis=1), w_ref[...])
        y_ref[...] = y
        out = x_ref[...] + _rms(y, g_ref[...])[0]
        if is_last:
            diff = out - t_ref[...]
            dout_ref[...] = diff * (1.0 / d)
            part = jnp.sum(jnp.sum(diff * diff, axis=-1, keepdims=True), axis=0, keepdims=True) * (0.5 / d)
            _acc(loss_ref, jnp.broadcast_to(part, (1, 128)))
        else:
            out_ref[...] = out

    row = lambda wd: pl.BlockSpec((tm, wd), lambda i: (i, 0))
    fixed = lambda a, b: pl.BlockSpec((a, b), lambda i: (0, 0))
    in_specs = [row(512), row(512), fixed(d, d), row(d), fixed(1, d)]
    args = [yab, yc, w, x, g]
    out_specs = [row(d), row(d)]
    out_shape = [jax.ShapeDtypeStruct((t, d), F32), jax.ShapeDtypeStruct((t, d), F32)]
    if is_last:
        in_specs.append(row(d))
        args.append(target)
        out_specs.append(fixed(1, 128))
        out_shape.append(jax.ShapeDtypeStruct((1, 128), F32))
    return pl.pallas_call(body, name=name, grid=(t // tm,), in_specs=in_specs, out_specs=out_specs,
                          out_shape=out_shape, compiler_params=_cp())(*args)


def _halves(tm):
    half = tm // 2 if tm >= 512 else tm
    return [pl.ds(s, half) for s in range(0, tm, half)]


def _out_proj_bwd(dout, y, yab, yc, w, g, name, tm=1024):
    t, d = y.shape
    tm = min(tm, t)

    def body(dout_ref, y_ref, yab_ref, yc_ref, w_ref, g_ref, dycat_ref, dw_ref, dg_ref):
        _zero_when(pl.program_id(0) == 0, dw_ref, dg_ref)
        dybs = []
        for rows in _halves(tm):
            y = y_ref[rows, :]
            r = lax.rsqrt(jnp.mean(y * y, axis=-1, keepdims=True) + EPS)
            dy, dgt = _rms_bwd(y, r, g_ref[...], dout_ref[rows, :])
            _acc(dg_ref, _colsum(dgt))
            dybs.append(_mx(dy))
        for rows, dyb in zip(_halves(tm), dybs):
            _acc(dw_ref, _dot_tn(jnp.concatenate([yab_ref[rows, :], yc_ref[rows, :]], axis=1), dyb))
            dycat_ref[rows, :] = _dot_nt(dyb, w_ref[...])

    row = lambda wd: pl.BlockSpec((tm, wd), lambda i: (i, 0))
    fixed = lambda a, b: pl.BlockSpec((a, b), lambda i: (0, 0), pipeline_mode=pl.Buffered(1))
    return pl.pallas_call(
        body, name=name, grid=(t // tm,),
        in_specs=[row(d), row(d), row(512), row(512), fixed(d, d), fixed(1, d)],
        out_specs=[row(d), fixed(d, d), fixed(1, d)],
        out_shape=[jax.ShapeDtypeStruct((t, d), F32), jax.ShapeDtypeStruct((d, d), F32),
                   jax.ShapeDtypeStruct((1, d), F32)],
        compiler_params=_cp())(dout, y, yab, yc, w, g)


def _mesh_pos():
    return lax.axis_index("x"), lax.axis_index("y"), lax.axis_index("c")


def _remote(src, dst, send_sem, recv_sem, to):
    return pltpu.make_async_remote_copy(src_ref=src, dst_ref=dst, send_sem=send_sem, recv_sem=recv_sem,
                                        device_id=to, device_id_type=MESH)


CHUNK_ROWS = 256


def _pieces(rows):
    return [(s, min(CHUNK_ROWS, rows - s)) for s in range(0, rows, CHUNK_ROWS)]


def _piece_table(shapes):
    return [(a, s, sz) for a, shp in enumerate(shapes) for s, sz in _pieces(shp[-2])]


def _gather_weights(shards):
    n = len(shards)
    table = _piece_table([s.shape for s in shards])
    npc = len(table)

    def body(*refs):
        ins, outs = refs[:n], refs[n:2 * n]
        send_sems, recv_sems, fwd_send, fwd_recv = refs[2 * n:]
        x, y, c = _mesh_pos()
        me, sibling = (x, y, c), (x, y, 1 - c)
        chips = [(1 - x, y), (x, 1 - y), (1 - x, 1 - y)]
        slot = lambda cx, cy, layer: 2 * (2 * cx + cy) + layer
        first = []
        for a in range(n):
            for j, (cx, cy) in enumerate(chips):
                first.append(_remote(ins[a].at[c], outs[a].at[slot(x, y, c)], send_sems.at[a, j], recv_sems.at[a, j],
                                     (cx, cy, c)))
                first[-1].start()
        passed = []
        for j, (cx, cy) in enumerate(chips):
            for a in range(n):
                blk = outs[a].at[slot(cx, cy, c)]
                _remote(blk, blk, send_sems.at[a, j], recv_sems.at[a, j], me).wait_recv()
            for q, (a, s, sz) in enumerate(table):
                rows = outs[a].at[slot(cx, cy, c), pl.ds(s, sz)]
                passed.append(_remote(rows, rows, fwd_send.at[j, q], fwd_recv.at[j, q], sibling))
                passed[-1].start()
        for j, (cx, cy) in enumerate(chips):
            for q, (a, s, sz) in enumerate(table):
                rows = outs[a].at[slot(cx, cy, 1 - c), pl.ds(s, sz)]
                _remote(rows, rows, fwd_send.at[j, q], fwd_recv.at[j, q], me).wait_recv()
        for cp in first + passed:
            cp.wait_send()

    return pl.pallas_call(
        body, name="gather_weights", in_specs=[ANY] * n, out_specs=[ANY] * n,
        out_shape=[jax.ShapeDtypeStruct((8,) + s.shape[1:], s.dtype) for s in shards],
        scratch_shapes=[pltpu.SemaphoreType.DMA((n, 3)), pltpu.SemaphoreType.DMA((n, 3)),
                        pltpu.SemaphoreType.DMA((3, npc)), pltpu.SemaphoreType.DMA((3, npc))])(*shards)


def _pair_exchange(parts, common):
    n = len(parts)
    table = _piece_table([p.shape for p in parts] + [common.shape])
    npc = len(table)

    def body(*refs):
        ins, outs = refs[:n + 1], refs[n + 1:2 * n + 2]
        send_sems, recv_sems = refs[2 * n + 2:]
        x, y, c = _mesh_pos()
        sent = []
        for k in range(4):
            for q, (a, s, sz) in enumerate(table):
                if a == n and k > 0:
                    continue
                src = ins[a].at[2 * k + 1 - c, pl.ds(s, sz)] if a < n else ins[a].at[pl.ds(s, sz)]
                dst = outs[a].at[k, pl.ds(s, sz)] if a < n else outs[a].at[pl.ds(s, sz)]
                sent.append(_remote(src, dst, send_sems.at[k, q], recv_sems.at[k, q], (x, y, 1 - c)))
                sent[-1].start()
        for k in range(4):
            for q, (a, s, sz) in enumerate(table):
                if a == n and k > 0:
                    continue
                dst = outs[a].at[k, pl.ds(s, sz)] if a < n else outs[a].at[pl.ds(s, sz)]
                _remote(dst, dst, send_sems.at[k, q], recv_sems.at[k, q], (x, y, c)).wait_recv()
        for cp in sent:
            cp.wait_send()

    return pl.pallas_call(
        body, name="grad_pair_exchange", in_specs=[ANY] * (n + 1), out_specs=[ANY] * (n + 1),
        out_shape=[jax.ShapeDtypeStruct((4,) + p.shape[1:], p.dtype) for p in parts]
        + [jax.ShapeDtypeStruct(common.shape, common.dtype)],
        scratch_shapes=[pltpu.SemaphoreType.DMA((4, npc)), pltpu.SemaphoreType.DMA((4, npc))])(*parts, common)


def _chip_exchange(parts, common):
    n = len(parts)
    table = _piece_table([p.shape for p in parts] + [common.shape])
    npc = len(table)

    def body(*refs):
        ins, outs = refs[:n + 1], refs[n + 1:2 * n + 2]
        send_sems, recv_sems = refs[2 * n + 2:]
        x, y, c = _mesh_pos()
        mine = 2 * x + y
        chips = [(1 - x, y), (x, 1 - y), (1 - x, 1 - y)]
        src = lambda a, k: ins[a].at[k] if a < n else ins[a]
        sent = []
        for j, (cx, cy) in enumerate(chips):
            for q, (a, s, sz) in enumerate(table):
                sent.append(_remote(src(a, 2 * cx + cy).at[pl.ds(s, sz)], outs[a].at[mine, pl.ds(s, sz)],
                                    send_sems.at[j, q], recv_sems.at[j, q], (cx, cy, c)))
                sent[-1].start()
        for j, (cx, cy) in enumerate(chips):
            for q, (a, s, sz) in enumerate(table):
                dst = outs[a].at[2 * cx + cy, pl.ds(s, sz)]
                _remote(dst, dst, send_sems.at[j, q], recv_sems.at[j, q], (x, y, c)).wait_recv()
        for cp in sent:
            cp.wait_send()

    return pl.pallas_call(
        body, name="grad_chip_exchange", in_specs=[ANY] * (n + 1), out_specs=[ANY] * (n + 1),
        out_shape=[jax.ShapeDtypeStruct(p.shape, p.dtype) for p in parts]
        + [jax.ShapeDtypeStruct((4,) + common.shape, common.dtype)],
        scratch_shapes=[pltpu.SemaphoreType.DMA((3, npc)), pltpu.SemaphoreType.DMA((3, npc))])(*parts, common)


def _sibling_exchange(sums):
    n = len(sums)
    table = _piece_table([s.shape for s in sums])
    npc = len(table)

    def body(*refs):
        ins, outs = refs[:n], refs[n:2 * n]
        send_sems, recv_sems = refs[2 * n:]
        x, y, c = _mesh_pos()
        sent = []
        for q, (a, s, sz) in enumerate(table):
            sent.append(_remote(ins[a].at[pl.ds(s, sz)], outs[a].at[pl.ds(s, sz)], send_sems.at[q], recv_sems.at[q],
                                (x, y, 1 - c)))
            sent[-1].start()
        for q, (a, s, sz) in enumerate(table):
            dst = outs[a].at[pl.ds(s, sz)]
            _remote(dst, dst, send_sems.at[q], recv_sems.at[q], (x, y, c)).wait_recv()
        for cp in sent:
            cp.wait_send()

    return pl.pallas_call(
        body, name="sibling_exchange", in_specs=[ANY] * n, out_specs=[ANY] * n,
        out_shape=[jax.ShapeDtypeStruct(s.shape, s.dtype) for s in sums],
        scratch_shapes=[pltpu.SemaphoreType.DMA((npc,)), pltpu.SemaphoreType.DMA((npc,))])(*sums)


def _pair_sum(parts, got, name, tr):
    _, r, c = got.shape
    tr = min(tr, r)

    def body(p_ref, g_ref, o_ref, w_ref):
        total = p_ref[...] + g_ref[...]
        o_ref[...] = total
        w_ref[...] = total.astype(WIRE_DTYPE)

    blk = pl.BlockSpec((None, tr, c), lambda k, i: (k, i, 0))
    mine = pl.BlockSpec((None, tr, c), lambda k, i: (2 * k + lax.axis_index("c"), i, 0))
    return pl.pallas_call(
        body, name=name, grid=(4, r // tr), in_specs=[mine, blk], out_specs=[blk, blk],
        out_shape=[jax.ShapeDtypeStruct(got.shape, F32), jax.ShapeDtypeStruct(got.shape, WIRE_DTYPE)],
        compiler_params=_cp(2))(parts, got)


def _sum_chips(pair_sums, recv, name, tr):
    _, r, c = pair_sums.shape
    tr = min(tr, r)

    def body(own_ref, r_ref, o_ref):
        chip = 2 * lax.axis_index("x") + lax.axis_index("y")
        own_blk = own_ref[...]
        acc = jnp.where(chip == 0, own_blk, r_ref[0].astype(F32))
        for k in range(1, 4):
            acc = acc + jnp.where(chip == k, own_blk, r_ref[k].astype(F32))
        o_ref[...] = acc

    return pl.pallas_call(
        body, name=name, grid=(r // tr,),
        in_specs=[pl.BlockSpec((None, tr, c), lambda i: (2 * lax.axis_index("x") + lax.axis_index("y"), i, 0)),
                  pl.BlockSpec((4, tr, c), lambda i: (0, i, 0))],
        out_specs=pl.BlockSpec((tr, c), lambda i: (i, 0)),
        out_shape=jax.ShapeDtypeStruct((r, c), F32), compiler_params=_cp())(pair_sums, recv)


def _sum_leading(parts, name, tr):
    nlead, r, c = parts.shape
    tr = min(tr, r)

    def body(p_ref, o_ref):
        acc = p_ref[0]
        for j in range(1, nlead):
            acc = acc + p_ref[j]
        o_ref[...] = acc

    return pl.pallas_call(
        body, name=name, grid=(r // tr,),
        in_specs=[pl.BlockSpec((nlead, tr, c), lambda i: (0, i, 0))], out_specs=pl.BlockSpec((tr, c), lambda i: (i, 0)),
        out_shape=jax.ShapeDtypeStruct((r, c), parts.dtype), compiler_params=_cp())(parts)


def _adamw_update(w_ref, g_ref, m_ref, v_ref, d_ref, nm_ref, nv_ref):
    gg = g_ref[...]
    nm = ADAM_B1 * m_ref[...] + (1.0 - ADAM_B1) * gg
    nv = ADAM_B2 * v_ref[...] + (1.0 - ADAM_B2) * jnp.square(gg)
    m_hat = nm / (1.0 - ADAM_B1 ** ADAM_STEP)
    v_hat = nv / (1.0 - ADAM_B2 ** ADAM_STEP)
    d_ref[...] = -ADAM_LR * (m_hat / (jnp.sqrt(v_hat) + ADAM_EPS) + ADAM_WD * w_ref[...])
    nm_ref[...] = nm
    nv_ref[...] = nv


def _adamw(w, g, m, v, name, tr):
    r, c = w.shape
    tr = min(tr, r)

    def body(w_ref, g_ref, m_ref, v_ref, d_ref, nm_ref, nv_ref):
        _adamw_update(w_ref, g_ref, m_ref, v_ref, d_ref, nm_ref, nv_ref)

    blk = pl.BlockSpec((tr, c), lambda i: (i, 0))
    return pl.pallas_call(
        body, name=name, grid=(r // tr,), in_specs=[blk] * 4, out_specs=[blk] * 3,
        out_shape=[jax.ShapeDtypeStruct((r, c), F32)] * 3, compiler_params=_cp())(w, g, m, v)


def _adamw_many(ws, gs, ms, vs, name):
    n = len(ws)

    def body(*refs):
        ins, outs = refs[:4 * n], refs[4 * n:]
        for k in range(n):
            _adamw_update(ins[k], ins[n + k], ins[2 * n + k], ins[3 * n + k], outs[k], outs[n + k], outs[2 * n + k])

    vmem = pl.BlockSpec(memory_space=pltpu.VMEM)
    res = pl.pallas_call(
        body, name=name, in_specs=[vmem] * (4 * n), out_specs=[vmem] * (3 * n),
        out_shape=[jax.ShapeDtypeStruct(a.shape, F32) for a in ws] * 3)(*ws, *gs, *ms, *vs)
    return res[:n], res[n:2 * n], res[2 * n:]


def _rope_tables(positions):
    inv_freq = ROPE_BASE ** (-jnp.arange(0, 64, 2, dtype=F32) / 64)
    ang = positions.astype(F32)[:, None] * inv_freq
    cos, sin = jnp.cos(ang), jnp.sin(ang)
    t = positions.shape[0]
    rc = jnp.concatenate([jnp.ones((t, 128), F32), cos, cos, jnp.ones((t, 64), F32)], axis=1)
    rs = jnp.concatenate([jnp.zeros((t, 128), F32), sin, sin, jnp.zeros((t, 64), F32)], axis=1)
    return rc, rs


def _layer_params(l, w_in, w_uq, w_ukv, w_out, small):
    p = {}
    p["w_in"] = jnp.concatenate([w_in[l][:, :1984], jnp.zeros((1024, 64), w_in.dtype), w_in[l][:, 1984:]], axis=1)
    p["w_uq"] = jnp.pad(w_uq[l].reshape(384, 4, 192), ((0, 0), (0, 0), (0, 64))).reshape(384, 1024)
    p["w_ukv"] = w_ukv[l].reshape(256, 4, 2, 128).transpose(0, 2, 1, 3).reshape(256, 1024)
    p["w_out"] = w_out[l]
    p["pre_g"] = small["pre_norm_g"][l][None]
    p["post_g"] = small["post_norm_g"][l][None]
    p["sgu_w"] = small["sgu_w"][l].reshape(512, 128)
    p["sgu_wt"] = small["sgu_w"][l].transpose(0, 2, 1).reshape(512, 128)
    p["sgu_bias"] = jnp.repeat(small["sgu_b"][l].T, 64, axis=1)
    p["ln_g"] = small["sgu_ln_g"][l][None]
    p["ln_b"] = small["sgu_ln_b"][l][None]
    p["pool_wbd"] = _mx(jax.scipy.linalg.block_diag(*[small["pool_w"][l][gi] for gi in range(4)]))
    p["pool_scale"] = small["pool_scale"][l][None]
    p["gq"] = small["q_norm_g"][l][None]
    p["gkv"] = small["kv_norm_g"][l][None]
    return p


def _layer_fwd(l, x, p, rc, rs, target):
    z, h = _in_proj_fwd(x, p["pre_g"], p["w_in"], f"in_proj_fwd_{l}")
    yab = _mix_fwd(z, p["sgu_w"], p["sgu_bias"], p["ln_g"], p["ln_b"], p["pool_wbd"], p["pool_scale"], f"mix_fwd_{l}")
    qh, kh, vh = _qkv_fwd(z, rc, rs, p["w_uq"], p["w_ukv"], p["gq"], p["gkv"], f"qkv_fwd_{l}")
    o, yc, lse = _attn_fwd(qh, kh, vh, z, f"attn_fwd_{l}")
    outs = _out_proj_fwd(yab, yc, p["w_out"], x, p["post_g"], target, f"out_proj_fwd_{l}")
    saved = dict(x=x, z=z, h=h, yab=yab, qh=qh, kh=kh, vh=vh, o=o, yc=yc, lse=lse, y=outs[0])
    return saved, outs[1:]


def _layer_bwd(l, dout, sv, p, rc, rs):
    dycat, dw_out, dpost = _out_proj_bwd(dout, sv["y"], sv["yab"], sv["yc"], p["w_out"], p["post_g"], f"out_proj_bwd_{l}")
    dq, dgate, dk, dv = _attn_bwd(sv["qh"], sv["kh"], sv["vh"], sv["o"], sv["lse"], dycat, sv["z"], f"attn_bwd_{l}")
    dzc, dzk, dwq, dwkv, dgq, dgkv = _qkv_bwd(dq, dk, dv, sv["z"], rc, rs, p["w_uq"], p["w_ukv"], p["gq"], p["gkv"],
                                              f"qkv_bwd_{l}")
    dzm, dsw, dsb, dlng, dlnb, dpw, dps = _mix_bwd(sv["z"], dycat, p["sgu_w"], p["sgu_wt"], p["sgu_bias"], p["ln_g"],
                                                   p["ln_b"], p["pool_wbd"], p["pool_scale"], f"mix_bwd_{l}")
    dx, dw_in, dpre = _in_proj_bwd(dzm, dzc, dzk, dgate, sv["h"], sv["x"], dout, p["w_in"], p["pre_g"], f"in_proj_bwd_{l}")
    grads = {
        "pre_norm_g": dpre[0], "post_norm_g": dpost[0],
        "w_in": jnp.concatenate([dw_in[:, :1984], dw_in[:, 2048:]], axis=1),
        "sgu_w": dsw.reshape(4, 128, 128), "sgu_b": dsb[:, :4].T, "sgu_ln_g": dlng[0], "sgu_ln_b": dlnb[0],
        "pool_w": jnp.stack([dpw[64 * gi:64 * gi + 64, 64 * gi:64 * gi + 64] for gi in range(4)]),
        "pool_scale": dps[0], "q_norm_g": dgq[0],
        "w_uq": dwq.reshape(384, 4, 256)[:, :, :192].reshape(384, 768), "kv_norm_g": dgkv[0],
        "w_ukv": dwkv.reshape(256, 2, 4, 128).transpose(0, 2, 1, 3).reshape(256, 1024), "w_out": dw_out,
    }
    return dx, grads


SMALL_NAMES = ["pre_norm_g", "post_norm_g", "sgu_w", "sgu_b", "sgu_ln_g", "sgu_ln_b", "pool_w", "pool_scale",
               "q_norm_g", "kv_norm_g"]
BIG_NAMES = ["w_in", "w_uq", "w_ukv", "w_out"]
WEIGHT_NAMES = ["pre_norm_g", "post_norm_g", "w_in", "sgu_w", "sgu_b", "sgu_ln_g", "sgu_ln_b", "pool_w", "pool_scale",
                "q_norm_g", "w_uq", "kv_norm_g", "w_ukv", "w_out"]


def _local_step(x, positions, target, w_in, w_uq, w_ukv, w_out, small):
    rc, rs = _rope_tables(positions)
    params = [_layer_params(l, w_in, w_uq, w_ukv, w_out, small) for l in range(DEPTH)]
    saved = []
    for l in range(DEPTH):
        sv, outs = _layer_fwd(l, x, params[l], rc, rs, target if l == DEPTH - 1 else None)
        saved.append(sv)
        if l < DEPTH - 1:
            x = outs[0]
    dout, loss = outs
    grads = [None] * DEPTH
    for l in reversed(range(DEPTH)):
        dout, grads[l] = _layer_bwd(l, dout, saved[l], params[l], rc, rs)
    return loss[0, 0], dout, {k: jnp.stack([grads[l][k] for l in range(DEPTH)]) for k in WEIGHT_NAMES}


def _pack_small(tree, extra=None):
    pieces = [tree[k].reshape(-1) for k in SMALL_NAMES]
    pieces.append(jnp.zeros((1,), F32) if extra is None else extra.reshape(1))
    flat = jnp.concatenate(pieces)
    rows = -(-flat.shape[0] // 1024) * 8
    return jnp.pad(flat, (0, rows * 128 - flat.shape[0])).reshape(rows, 128)


def _unpack_small(packed, like):
    flat = packed.reshape(-1)
    out, off = {}, 0
    for k in SMALL_NAMES:
        size = like[k].size
        out[k] = flat[off:off + size].reshape(like[k].shape)
        off += size
    return out, flat[off]


def kernel(x, positions, pre_norm_g, post_norm_g, w_in, sgu_w, sgu_b, sgu_ln_g, sgu_ln_b, pool_w, pool_scale, q_norm_g, w_uq, kv_norm_g, w_ukv, w_out, loss_target, m_pre_norm_g, m_post_norm_g, m_w_in, m_sgu_w, m_sgu_b, m_sgu_ln_g, m_sgu_ln_b, m_pool_w, m_pool_scale, m_q_norm_g, m_w_uq, m_kv_norm_g, m_w_ukv, m_w_out, v_pre_norm_g, v_post_norm_g, v_w_in, v_sgu_w, v_sgu_b, v_sgu_ln_g, v_sgu_ln_b, v_pool_w, v_pool_scale, v_q_norm_g, v_w_uq, v_kv_norm_g, v_w_ukv, v_w_out):
    w = dict(pre_norm_g=pre_norm_g, post_norm_g=post_norm_g, w_in=w_in, sgu_w=sgu_w, sgu_b=sgu_b, sgu_ln_g=sgu_ln_g,
             sgu_ln_b=sgu_ln_b, pool_w=pool_w, pool_scale=pool_scale, q_norm_g=q_norm_g, w_uq=w_uq, kv_norm_g=kv_norm_g,
             w_ukv=w_ukv, w_out=w_out)
    m = dict(pre_norm_g=m_pre_norm_g, post_norm_g=m_post_norm_g, w_in=m_w_in, sgu_w=m_sgu_w, sgu_b=m_sgu_b,
             sgu_ln_g=m_sgu_ln_g, sgu_ln_b=m_sgu_ln_b, pool_w=m_pool_w, pool_scale=m_pool_scale, q_norm_g=m_q_norm_g,
             w_uq=m_w_uq, kv_norm_g=m_kv_norm_g, w_ukv=m_w_ukv, w_out=m_w_out)
    v = dict(pre_norm_g=v_pre_norm_g, post_norm_g=v_post_norm_g, w_in=v_w_in, sgu_w=v_sgu_w, sgu_b=v_sgu_b,
             sgu_ln_g=v_sgu_ln_g, sgu_ln_b=v_sgu_ln_b, pool_w=v_pool_w, pool_scale=v_pool_scale, q_norm_g=v_q_norm_g,
             w_uq=v_w_uq, kv_norm_g=v_kv_norm_g, w_ukv=v_w_ukv, w_out=v_w_out)

    core = lax.axis_index("c")
    chip = 2 * lax.axis_index("x") + lax.axis_index("y")
    shards = [_mx(w[k]) for k in BIG_NAMES]
    gathered = _gather_weights(shards)
    g_in, g_uq, g_ukv, g_out = [lax.dynamic_update_slice(g, s, (2 * chip, 0, 0)) for g, s in zip(gathered, shards)]
    cols = lambda g: g.reshape((4, 2) + g.shape[1:]).transpose(1, 2, 0, 3).reshape(2, g.shape[1], 4 * g.shape[2])
    full_out = g_out.reshape(4, 2, 256, 1024).transpose(1, 0, 2, 3).reshape(2, 1024, 1024)
    loss, dx, grads = _local_step(x[0], positions[0], loss_target[0], cols(g_in), cols(g_uq), cols(g_ukv), full_out, w)

    split_cols = lambda g: g.reshape(2, g.shape[1], 4, g.shape[2] // 4).transpose(2, 0, 1, 3).reshape(8, g.shape[1], g.shape[2] // 4)
    parts = [split_cols(grads["w_in"]), split_cols(grads["w_uq"]), split_cols(grads["w_ukv"]),
             grads["w_out"].reshape(2, 4, 256, 1024).transpose(1, 0, 2, 3).reshape(8, 256, 1024)]
    common = _pack_small(grads, loss)
    got = _pair_exchange(parts, common)
    pair_sums = [_pair_sum(parts[a], got[a], f"pair_sum_{BIG_NAMES[a]}", 128) for a in range(4)]
    chip_common = _sum_leading(jnp.stack([common, got[4]]), "pair_sum_small", common.shape[0])
    received = _chip_exchange([ps[1] for ps in pair_sums], chip_common)
    sums = [_sum_chips(pair_sums[a][0], received[a], f"sum_{BIG_NAMES[a]}", 128) for a in range(4)]
    all_common = lax.dynamic_update_slice(received[4], chip_common[None], (chip, 0, 0))
    small_sum, loss = _unpack_small(_sum_leading(all_common, "sum_small", all_common.shape[1]), w)
    others = _sibling_exchange(sums)
    total = dict(small_sum)
    for a, k in enumerate(BIG_NAMES):
        total[k] = jnp.where(core == 0, jnp.stack([sums[a], others[a]]), jnp.stack([others[a], sums[a]]))

    rows2d = lambda a: a.reshape(-1, a.shape[-1])
    small_out = _adamw_many(*[[rows2d(tree[k]) for k in SMALL_NAMES] for tree in (w, total, m, v)], "adamw_small")
    delta, new_m, new_v = ({k: r.reshape(w[k].shape) for k, r in zip(SMALL_NAMES, res)} for res in small_out)
    for k in BIG_NAMES:
        shape = w[k].shape
        flat = lambda a: a.reshape(shape[0] * shape[1], shape[2])
        res = _adamw(flat(w[k]), flat(total[k]), flat(m[k]), flat(v[k]), f"adamw_{k}", 256)
        delta[k], new_m[k], new_v[k] = (r.reshape(shape) for r in res)

    return (loss, dx[None], *[total[k] for k in WEIGHT_NAMES], *[delta[k] for k in WEIGHT_NAMES],
            *[new_m[k] for k in WEIGHT_NAMES], *[new_v[k] for k in WEIGHT_NAMES])
```

```python
import jax
import jax.numpy as jnp
from jax import lax
from jax.experimental import pallas as pl
from jax.experimental.pallas import tpu as pltpu

F32 = jnp.float32
MXU_DTYPE = jnp.bfloat16
WIRE_DTYPE = jnp.bfloat16
EPS = 1e-6
NEG_INF = -1e30
CHUNK = 64
DEPTH = 2
N_HEADS = 4
QK_PAD = 256
V_DIM = 128
SCALE = 192 ** -0.5
LOG2E = 1.4426950408889634
ROPE_BASE = 10000.0
ADAM_LR, ADAM_B1, ADAM_B2, ADAM_EPS, ADAM_WD, ADAM_STEP = 0.001, 0.9, 0.999, 1e-08, 0.01, 10
VMEM_LIMIT_BYTES = 56 * 1024 * 1024
MESH = pl.DeviceIdType.MESH
ANY = pl.BlockSpec(memory_space=pl.ANY)

Z_MIX, Z_C, Z_KR, Z_GATE = 1280, 640, 128, 512
Z_W = Z_MIX + Z_C + Z_KR + Z_GATE


def _cp(n_axes=1):
    return pltpu.CompilerParams(dimension_semantics=("arbitrary",) * n_axes, vmem_limit_bytes=VMEM_LIMIT_BYTES)


def _dot(a, b):
    return lax.dot_general(a, b, (((1,), (0,)), ((), ())), preferred_element_type=F32)


def _dot_nt(a, b):
    return lax.dot_general(a, b, (((1,), (1,)), ((), ())), preferred_element_type=F32)


def _dot_tn(a, b):
    return lax.dot_general(a, b, (((0,), (0,)), ((), ())), preferred_element_type=F32)


def _mx(a):
    return a.astype(MXU_DTYPE)


def _silu_and_grad(g):
    sg = jax.nn.sigmoid(g)
    return g * sg, sg * (1.0 + g * (1.0 - sg))


def _rms(x, g):
    r = lax.rsqrt(jnp.mean(x * x, axis=-1, keepdims=True) + EPS)
    return x * r * g, r


def _rms_bwd(x, r, g, dy):
    xhat = x * r
    dyg = dy * g
    dx = r * (dyg - xhat * jnp.mean(dyg * xhat, axis=-1, keepdims=True))
    return dx, dy * xhat


def _zero_when(first, *refs):
    @pl.when(first)
    def _():
        for ref in refs:
            ref[...] = jnp.zeros(ref.shape, ref.dtype)


def _acc(ref, val):
    ref[...] += val


def _colsum(a):
    return jnp.sum(a, axis=0, keepdims=True)


def _in_proj_fwd(x, g, w, name, tm=1024):
    t, d = x.shape
    n = w.shape[1]
    tm = min(tm, t)

    def body(x_ref, g_ref, w_ref, z_ref, h_ref):
        hs = []
        for rows in _halves(tm):
            h, _ = _rms(x_ref[rows, :], g_ref[...])
            hs.append(_mx(h))
            h_ref[rows, :] = hs[-1]
        for rows, h in zip(_halves(tm), hs):
            z_ref[rows, :] = _dot(h, w_ref[...])

    return pl.pallas_call(
        body, name=name, grid=(t // tm,),
        in_specs=[pl.BlockSpec((tm, d), lambda i: (i, 0)), pl.BlockSpec((1, d), lambda i: (0, 0)),
                  pl.BlockSpec((d, n), lambda i: (0, 0), pipeline_mode=pl.Buffered(1))],
        out_specs=[pl.BlockSpec((tm, n), lambda i: (i, 0)), pl.BlockSpec((tm, d), lambda i: (i, 0))],
        out_shape=[jax.ShapeDtypeStruct((t, n), F32), jax.ShapeDtypeStruct((t, d), MXU_DTYPE)],
        compiler_params=_cp())(x, g, w)


def _in_proj_bwd(dz_mix, dz_c, dz_kr, dz_gate, h, x, d_res, w, g, name, tm=512):
    t, d = x.shape
    n = w.shape[1]
    tm = min(tm, t)

    def body(dm_ref, dc_ref, dk_ref, dg_ref, h_ref, x_ref, dres_ref, w_ref, g_ref, dx_ref, dw_ref, dgn_ref):
        first = pl.program_id(0) == 0
        dz = jnp.concatenate([dm_ref[...], dc_ref[...], dk_ref[...], dg_ref[...]], axis=1)

        _zero_when(first, dw_ref, dgn_ref)
        dh = _dot_nt(dz, w_ref[...])
        hb = h_ref[...]
        for c0 in range(0, n, 512):
            dw_ref[:, c0:c0 + 512] += _dot_tn(hb, dz[:, c0:c0 + 512])
        xf = x_ref[...]
        r = lax.rsqrt(jnp.mean(xf * xf, axis=-1, keepdims=True) + EPS)
        dx, dgt = _rms_bwd(xf, r, g_ref[...], dh)
        dx_ref[...] = dx + dres_ref[...]
        _acc(dgn_ref, _colsum(dgt))

    row = lambda wd: pl.BlockSpec((tm, wd), lambda i: (i, 0))
    fixed = lambda a, b: pl.BlockSpec((a, b), lambda i: (0, 0), pipeline_mode=pl.Buffered(1))
    return pl.pallas_call(
        body, name=name, grid=(t // tm,),
        in_specs=[row(Z_MIX), row(Z_C), row(Z_KR), row(Z_GATE), row(d), row(d), row(d), fixed(d, n), fixed(1, d)],
        out_specs=[row(d), fixed(d, n), fixed(1, d)],
        out_shape=[jax.ShapeDtypeStruct((t, d), F32), jax.ShapeDtypeStruct((d, n), F32),
                   jax.ShapeDtypeStruct((1, d), F32)],
        compiler_params=_cp())(dz_mix, dz_c, dz_kr, dz_gate, h, x, d_res, w, g)


def _lane_group(shape):
    return lax.broadcasted_iota(jnp.int32, shape, 1) // 64


def _select_group(vals):
    grp = _lane_group(vals[0].shape)
    out = vals[3]
    for gi in (2, 1, 0):
        out = jnp.where(grp == gi, vals[gi], out)
    return out


def _sgu_mask(transposed):
    r = (lax.broadcasted_iota(jnp.int32, (512, 128), 0) % 128) // CHUNK
    c = lax.broadcasted_iota(jnp.int32, (512, 128), 1) // CHUNK
    return (r <= c) if transposed else (c <= r)


def _sgu_apply(wstack, vb, nblk):
    outs = []
    for n in range(nblk):
        r = _dot(wstack, vb[n * 128:(n + 1) * 128, :])
        outs.append(_select_group([r[hh * 128:(hh + 1) * 128, :] for hh in range(4)]))
    return jnp.concatenate(outs, axis=0)


def _layer_norm(v, g, b):
    mu = jnp.mean(v, axis=-1, keepdims=True)
    vc = v - mu
    rstd = lax.rsqrt(jnp.mean(vc * vc, axis=-1, keepdims=True) + EPS)
    vhat = vc * rstd
    return vhat * g + b, vhat, rstd


def _pool_counts(t0, n):
    t = t0 + lax.broadcasted_iota(jnp.int32, (n, 256), 0)
    w = _select_group([jnp.full((n, 256), wv, jnp.int32) for wv in (2, 4, 8, 16)])
    return jnp.minimum(t + 1, w).astype(F32)


def _pooled(p, halo, t0):
    tm = p.shape[0]
    ext = jnp.concatenate([halo, p], axis=0)
    s2 = ext + pltpu.roll(ext, 1, 0)
    s4 = s2 + pltpu.roll(s2, 2, 0)
    s8 = s4 + pltpu.roll(s4, 4, 0)
    s16 = s8 + pltpu.roll(s8, 8, 0)
    sel = _select_group([s2, s4, s8, s16])[16:, :]
    return sel / _pool_counts(t0, tm) - p


def _pooled_bwd(dpool, dpool_halo, t0):
    tm = dpool.shape[0]
    n = tm + 16
    ext = jnp.concatenate([dpool, dpool_halo], axis=0) / _pool_counts(t0, n)
    f2 = ext + pltpu.roll(ext, n - 1, 0)
    f4 = f2 + pltpu.roll(f2, n - 2, 0)
    f8 = f4 + pltpu.roll(f4, n - 4, 0)
    f16 = f8 + pltpu.roll(f8, n - 8, 0)
    return _select_group([f2, f4, f8, f16])[:tm, :] - dpool


def _mix_specs(t, tm):
    nt16 = t // 16
    zrow = pl.BlockSpec((tm, Z_MIX), lambda i: (i, 0))
    prev_halo = pl.BlockSpec((16, 256), lambda i: (jnp.maximum(i * (tm // 16) - 1, 0), 3))
    fixed = lambda a, b: pl.BlockSpec((a, b), lambda i: (0, 0))
    params = [fixed(512, 128), fixed(128, 256), fixed(1, 256), fixed(1, 256), fixed(256, 256), fixed(1, 256)]
    return nt16, zrow, prev_halo, fixed, params


def _mix_fwd(z, sgu_w, sgu_bias, ln_g, ln_b, pool_wbd, pool_scale, name, tm=512):
    t = z.shape[0]
    tm = min(tm, t)
    _, zrow, prev_halo, _, params = _mix_specs(t, tm)

    def body(z_ref, halo_ref, w_ref, bias_ref, lng_ref, lnb_ref, pw_ref, ps_ref, y_ref):
        i = pl.program_id(0)
        u, v, gate = z_ref[:, 0:256], z_ref[:, 256:512], z_ref[:, 512:768]
        p, pgate = z_ref[:, 768:1024], z_ref[:, 1024:1280]
        vn, _, _ = _layer_norm(v, lng_ref[...], lnb_ref[...])
        wm = _mx(jnp.where(_sgu_mask(False), w_ref[...], 0.0))
        halo = jnp.where(i > 0, halo_ref[...], 0.0)
        pooled = _pooled(p, halo, i * tm)
        mixed = _sgu_apply(wm, _mx(vn), tm // 128) + jnp.tile(bias_ref[...], (tm // 128, 1))
        mixedp = _dot(_mx(pooled), pw_ref[...])
        ya = u * mixed * _silu_and_grad(gate)[0]
        yb = mixedp * ps_ref[...] * _silu_and_grad(pgate)[0]
        y_ref[...] = _mx(jnp.concatenate([ya, yb], axis=1))

    return pl.pallas_call(
        body, name=name, grid=(t // tm,),
        in_specs=[zrow, prev_halo] + params,
        out_specs=pl.BlockSpec((tm, 512), lambda i: (i, 0)),
        out_shape=jax.ShapeDtypeStruct((t, 512), MXU_DTYPE),
        compiler_params=_cp())(z, z, sgu_w, sgu_bias, ln_g, ln_b, pool_wbd, pool_scale)


def _mix_bwd(z, dycat, sgu_w, sgu_wt, sgu_bias, ln_g, ln_b, pool_wbd, pool_scale, name, tm=512):
    t = z.shape[0]
    tm = min(tm, t)
    nt16, zrow, prev_halo, fixed, params = _mix_specs(t, tm)
    nblk = tm // 128
    last = t // tm - 1

    def body(z_ref, halo_ref, zn_ref, dy_ref, dyn_ref, w_ref, wt_ref, bias_ref, lng_ref, lnb_ref, pw_ref, ps_ref,
             dz_ref, dw_ref, db_ref, dlng_ref, dlnb_ref, dpw_ref, dps_ref):
        i = pl.program_id(0)
        _zero_when(i == 0, dw_ref, db_ref, dlng_ref, dlnb_ref, dpw_ref, dps_ref)
        u, v, gate = z_ref[:, 0:256], z_ref[:, 256:512], z_ref[:, 512:768]
        p, pgate = z_ref[:, 768:1024], z_ref[:, 1024:1280]
        dya, dyb = dy_ref[:, 0:256], dy_ref[:, 256:512]
        vn, vhat, rstd = _layer_norm(v, lng_ref[...], lnb_ref[...])
        vnb = _mx(vn)
        wm = _mx(jnp.where(_sgu_mask(False), w_ref[...], 0.0))
        wmt = _mx(jnp.where(_sgu_mask(True), wt_ref[...], 0.0))
        halo = jnp.where(i > 0, halo_ref[...], 0.0)
        pooled_b = _mx(_pooled(p, halo, i * tm))
        silu, dsilu = _silu_and_grad(gate)
        psilu, pdsilu = _silu_and_grad(pgate)
        mixed = _sgu_apply(wm, vnb, nblk) + jnp.tile(bias_ref[...], (nblk, 1))
        mixedp = _dot(pooled_b, pw_ref[...])
        t1 = u * mixed
        d_gate = dya * t1 * dsilu
        d_t1 = dya * silu
        d_u = d_t1 * mixed
        d_mixed = d_t1 * u
        dmb = _mx(d_mixed)
        d_pgate = dyb * (mixedp * ps_ref[...]) * pdsilu
        d_ms = dyb * psilu
        _acc(dps_ref, _colsum(d_ms * mixedp))
        dmpb = _mx(d_ms * ps_ref[...])
        dmp_halo = _mx(dyn_ref[...] * _silu_and_grad(zn_ref[...])[0] * ps_ref[...])
        d_vn = _sgu_apply(wmt, dmb, nblk)
        grp = _lane_group((128, 256))
        lane = lax.broadcasted_iota(jnp.int32, (128, 128), 1)
        dws = [jnp.zeros((128, 128), F32) for _ in range(4)]
        dbias = jnp.zeros((128, 128), F32)
        for n in range(nblk):
            dm_n, dmb_n, vnb_n = d_mixed[n * 128:(n + 1) * 128], dmb[n * 128:(n + 1) * 128], vnb[n * 128:(n + 1) * 128]
            for hh in range(4):
                dws[hh] = dws[hh] + _dot_nt(jnp.where(grp == hh, dmb_n, jnp.zeros_like(dmb_n)), vnb_n)
                rs = jnp.sum(jnp.where(grp == hh, dm_n, 0.0), axis=-1, keepdims=True)
                dbias = dbias + jnp.where(lane == hh, rs, 0.0)
        _acc(dw_ref, jnp.concatenate(dws, axis=0))
        _acc(db_ref, dbias)
        _acc(dpw_ref, _dot_tn(pooled_b, dmpb))
        d_pooled = _dot_nt(dmpb, pw_ref[...])
        d_pooled_halo = jnp.where(i < last, _dot_nt(dmp_halo, pw_ref[...]), 0.0)
        _acc(dlng_ref, _colsum(d_vn * vhat))
        _acc(dlnb_ref, _colsum(d_vn))
        dvh = d_vn * lng_ref[...]
        d_v = rstd * (dvh - jnp.mean(dvh, axis=-1, keepdims=True) - vhat * jnp.mean(dvh * vhat, axis=-1, keepdims=True))
        d_p = _pooled_bwd(d_pooled, d_pooled_halo, i * tm)
        dz_ref[...] = _mx(jnp.concatenate([d_u, d_v, d_gate, d_p, d_pgate], axis=1))

        @pl.when(i == last)
        def _():
            dw_ref[...] = jnp.where(_sgu_mask(False), dw_ref[...], 0.0)

    nxt = lambda i: jnp.minimum((i + 1) * (tm // 16), nt16 - 1)
    return pl.pallas_call(
        body, name=name, grid=(t // tm,),
        in_specs=[zrow, prev_halo, pl.BlockSpec((16, 256), lambda i: (nxt(i), 4)),
                  pl.BlockSpec((tm, 512), lambda i: (i, 0)), pl.BlockSpec((16, 256), lambda i: (nxt(i), 1)),
                  params[0], fixed(512, 128)] + params[1:],
        out_specs=[pl.BlockSpec((tm, Z_MIX), lambda i: (i, 0)), fixed(512, 128), fixed(128, 128), fixed(1, 256),
                   fixed(1, 256), fixed(256, 256), fixed(1, 256)],
        out_shape=[jax.ShapeDtypeStruct((t, Z_MIX), MXU_DTYPE), jax.ShapeDtypeStruct((512, 128), F32),
                   jax.ShapeDtypeStruct((128, 128), F32), jax.ShapeDtypeStruct((1, 256), F32),
                   jax.ShapeDtypeStruct((1, 256), F32), jax.ShapeDtypeStruct((256, 256), F32),
                   jax.ShapeDtypeStruct((1, 256), F32)],
        compiler_params=_cp())(z, z, z, dycat, dycat, sgu_w, sgu_wt, sgu_bias, ln_g, ln_b, pool_wbd, pool_scale)


def _rot_half(x, transpose):
    w = x.shape[1]
    lane = lax.broadcasted_iota(jnp.int32, x.shape, 1) % min(w, 256)
    base = 128 if w >= 256 else 0
    lo = jnp.logical_and(lane >= base, lane < base + 32)
    hi = jnp.logical_and(lane >= base + 32, lane < base + 64)
    up = pltpu.roll(x, w - 32, 1)
    down = pltpu.roll(x, 32, 1)
    if transpose:
        return jnp.where(lo, up, jnp.where(hi, -down, 0.0))
    return jnp.where(lo, -up, jnp.where(hi, down, 0.0))


def _rope(x, c, s):
    return x * c + _rot_half(x, False) * s


def _rope_bwd(dy, c, s):
    return dy * c + _rot_half(dy * s, True)


def _qkv_fwd(z, rc, rs, w_uq, w_ukv, gq, gkv, name, tm=512):
    t = z.shape[0]
    tm = min(tm, t)

    def body(zc_ref, zk_ref, rc_ref, rs_ref, wq_ref, wkv_ref, gq_ref, gkv_ref, q_ref, k_ref, v_ref):
        cq, ckv = zc_ref[:, 0:384], zc_ref[:, 384:640]
        c, s = rc_ref[...], rs_ref[...]
        qn, _ = _rms(cq, gq_ref[...])
        kvn, _ = _rms(ckv, gkv_ref[...])
        q_pre = _dot(_mx(qn), wq_ref[...])
        kv = _dot(_mx(kvn), wkv_ref[...])
        kpe = _rope(zk_ref[...], c[:, 128:256], s[:, 128:256])
        q = _rope(q_pre, jnp.tile(c, (1, N_HEADS)), jnp.tile(s, (1, N_HEADS)))
        for hh in range(N_HEADS):
            q_ref[hh] = _mx(q[:, hh * QK_PAD:(hh + 1) * QK_PAD])
            k_ref[hh] = _mx(jnp.concatenate([kv[:, hh * 128:(hh + 1) * 128], kpe], axis=1))
            v_ref[hh] = _mx(kv[:, 512 + hh * 128:512 + (hh + 1) * 128])

    fixed = lambda a, b: pl.BlockSpec((a, b), lambda i: (0, 0))
    heads = lambda wd: pl.BlockSpec((N_HEADS, tm, wd), lambda i: (0, i, 0))
    return pl.pallas_call(
        body, name=name, grid=(t // tm,),
        in_specs=[pl.BlockSpec((tm, Z_C), lambda i: (i, Z_MIX // Z_C)),
                  pl.BlockSpec((tm, Z_KR), lambda i: (i, (Z_MIX + Z_C) // Z_KR)),
                  pl.BlockSpec((tm, 256), lambda i: (i, 0)), pl.BlockSpec((tm, 256), lambda i: (i, 0)),
                  fixed(384, 1024), fixed(256, 1024), fixed(1, 384), fixed(1, 256)],
        out_specs=[heads(QK_PAD), heads(QK_PAD), heads(V_DIM)],
        out_shape=[jax.ShapeDtypeStruct((N_HEADS, t, QK_PAD), MXU_DTYPE),
                   jax.ShapeDtypeStruct((N_HEADS, t, QK_PAD), MXU_DTYPE),
                   jax.ShapeDtypeStruct((N_HEADS, t, V_DIM), MXU_DTYPE)],
        compiler_params=_cp())(z, z, rc, rs, w_uq, w_ukv, gq, gkv)


def _qkv_bwd(dq, dk, dv, z, rc, rs, w_uq, w_ukv, gq, gkv, name, tm=512):
    t = z.shape[0]
    tm = min(tm, t)

    def body(dq_ref, dk_ref, dv_ref, zc_ref, rc_ref, rs_ref, wq_ref, wkv_ref, gq_ref, gkv_ref,
             dzc_ref, dzk_ref, dwq_ref, dwkv_ref, dgq_ref, dgkv_ref):
        _zero_when(pl.program_id(0) == 0, dwq_ref, dwkv_ref, dgq_ref, dgkv_ref)
        cq, ckv = zc_ref[:, 0:384], zc_ref[:, 384:640]
        c, s = rc_ref[...], rs_ref[...]
        dkv = _mx(jnp.concatenate([dk_ref[hh][:, 0:128] for hh in range(N_HEADS)]
                                  + [dv_ref[hh] for hh in range(N_HEADS)], axis=1))
        kvn, rkv = _rms(ckv, gkv_ref[...])
        _acc(dwkv_ref, _dot_tn(_mx(kvn), dkv))
        d_kvn = _dot_nt(dkv, wkv_ref[...])
        dq_all = jnp.concatenate([dq_ref[hh] for hh in range(N_HEADS)], axis=1)
        dqp = _mx(_rope_bwd(dq_all, jnp.tile(c, (1, N_HEADS)), jnp.tile(s, (1, N_HEADS))))
        qn, rq = _rms(cq, gq_ref[...])
        dkpe = dk_ref[0][:, 128:256]
        for hh in range(1, N_HEADS):
            dkpe = dkpe + dk_ref[hh][:, 128:256]
        dzk_ref[...] = _mx(_rope_bwd(dkpe, c[:, 128:256], s[:, 128:256]))
        _acc(dwq_ref, _dot_tn(_mx(qn), dqp))
        d_qn = _dot_nt(dqp, wq_ref[...])
        d_ckv, dgkv_t = _rms_bwd(ckv, rkv, gkv_ref[...], d_kvn)
        _acc(dgkv_ref, _colsum(dgkv_t))
        d_cq, dgq_t = _rms_bwd(cq, rq, gq_ref[...], d_qn)
        _acc(dgq_ref, _colsum(dgq_t))
        dzc_ref[...] = _mx(jnp.concatenate([d_cq, d_ckv], axis=1))

    fixed = lambda a, b: pl.BlockSpec((a, b), lambda i: (0, 0))
    heads = lambda wd: pl.BlockSpec((N_HEADS, tm, wd), lambda i: (0, i, 0))
    return pl.pallas_call(
        body, name=name, grid=(t // tm,),
        in_specs=[heads(QK_PAD), heads(QK_PAD), heads(V_DIM), pl.BlockSpec((tm, Z_C), lambda i: (i, Z_MIX // Z_C)),
                  pl.BlockSpec((tm, 256), lambda i: (i, 0)), pl.BlockSpec((tm, 256), lambda i: (i, 0)),
                  fixed(384, 1024), fixed(256, 1024), fixed(1, 384), fixed(1, 256)],
        out_specs=[pl.BlockSpec((tm, Z_C), lambda i: (i, 0)), pl.BlockSpec((tm, Z_KR), lambda i: (i, 0)),
                   fixed(384, 1024), fixed(256, 1024), fixed(1, 384), fixed(1, 256)],
        out_shape=[jax.ShapeDtypeStruct((t, Z_C), MXU_DTYPE), jax.ShapeDtypeStruct((t, Z_KR), MXU_DTYPE),
                   jax.ShapeDtypeStruct((384, 1024), F32), jax.ShapeDtypeStruct((256, 1024), F32),
                   jax.ShapeDtypeStruct((1, 384), F32), jax.ShapeDtypeStruct((1, 256), F32)],
        compiler_params=_cp())(dq, dk, dv, z, rc, rs, w_uq, w_ukv, gq, gkv)


def _loop_in_long_trips(n, body, longest=4):
    def doubled(inner):
        return lambda t, carry: inner(2 * t + 1, inner(2 * t, carry))

    trips = [body]
    while 2 ** (len(trips) - 1) < longest:
        trips.append(doubled(trips[-1]))
    done = 0
    for level in reversed(range(len(trips))):
        size = 2 ** level
        end = n // size
        lax.fori_loop(done, end, trips[level], 0)
        done = 2 * end if level else end


def _init_mask_bias(bias_ref):
    _, tq, tk = bias_ref.shape
    r = lax.broadcasted_iota(jnp.int32, (tq, tk), 0) // CHUNK
    c = lax.broadcasted_iota(jnp.int32, (tq, tk), 1) // CHUNK
    bias_ref[0] = jnp.zeros((tq, tk), F32)
    for d in range(tq // tk):
        bias_ref[1 + d] = jnp.where(c + d * (tk // CHUNK) <= r, 0.0, NEG_INF)


def _gate_block(tq):
    return pl.BlockSpec((tq, 128), lambda h, i: (i, (Z_MIX + Z_C + Z_KR) // 128 + h))


def _attn_fwd(qh, kh, vh, z, name, tq=1024, tk=512):
    t = qh.shape[1]
    tq = min(tq, t)
    tk = min(tk, tq)
    ratio = tq // tk

    def body(q_ref, g_ref, k_hbm, v_hbm, o_ref, yc_ref, lse_ref, k_v, v_v, m_s, acc_s, s_a, s_b, mx_a, mx_b, bias_s,
             sem):
        h, i = pl.program_id(0), pl.program_id(1)

        @pl.when(i == 0)
        def _():
            ck = pltpu.make_async_copy(k_hbm.at[h], k_v, sem.at[0])
            cv = pltpu.make_async_copy(v_hbm.at[h], v_v.at[:, 0:V_DIM], sem.at[1])
            ck.start()
            cv.start()
            v_v[:, V_DIM:2 * V_DIM] = jnp.ones((t, V_DIM), MXU_DTYPE)
            _init_mask_bias(bias_s)
            ck.wait()
            cv.wait()

        q = q_ref[...]
        m_s[...] = jnp.full(m_s.shape, NEG_INF, F32)
        acc_s[...] = jnp.zeros(acc_s.shape, F32)

        last = ratio * (i + 1) - 1

        def keys(j):
            return pl.ds(pl.multiple_of(j * tk, tk), tk)

        def scores(s_ref, mx_ref, j, biased):
            s = _dot_nt(q, k_v[keys(j), :]) * (SCALE * LOG2E)
            if biased:
                s = s + bias_s[jnp.maximum(j - ratio * i + 1, 0)]
            s_ref[...] = s
            mx_ref[...] = jnp.broadcast_to(jnp.max(s, axis=-1, keepdims=True), mx_ref.shape)

        def softmax_pv(s_ref, mx_ref, j):
            m_old = m_s[...]
            m_new = jnp.maximum(m_old, mx_ref[...])
            p = jnp.exp2(s_ref[...] - jnp.tile(m_new, (1, tk // 128)))
            alpha = jnp.exp2(m_old - m_new)
            m_s[...] = m_new
            acc_s[...] = jnp.tile(alpha, (1, 2)) * acc_s[...] + _dot(_mx(p), v_v[keys(j), :])

        scores(s_a, mx_a, 0, True)

        def pair(pp, carry, biased):
            scores(s_b, mx_b, 2 * pp + 1, biased)
            softmax_pv(s_a, mx_a, 2 * pp)
            scores(s_a, mx_a, jnp.minimum(2 * pp + 2, last), biased)
            softmax_pv(s_b, mx_b, 2 * pp + 1)
            return carry

        n_pairs = (last + 1) // 2
        n_plain = jnp.maximum(ratio * i // 2 - 1, 0)
        _loop_in_long_trips(n_plain, lambda pp, carry: pair(pp, carry, False))
        _loop_in_long_trips(n_pairs - n_plain, lambda pp, carry: pair(n_plain + pp, carry, True))
        if ratio % 2 == 1:
            @pl.when(last % 2 == 0)
            def _():
                softmax_pv(s_a, mx_a, last)

        l = acc_s[:, V_DIM:2 * V_DIM]
        o = acc_s[:, 0:V_DIM] / l
        o_ref[...] = o
        yc_ref[...] = _mx(o * _silu_and_grad(g_ref[...])[0])
        lse_ref[...] = m_s[...] + jnp.log2(l)

    return pl.pallas_call(
        body, name=name, grid=(N_HEADS, t // tq),
        in_specs=[pl.BlockSpec((None, tq, QK_PAD), lambda h, i: (h, i, 0)), _gate_block(tq), ANY, ANY],
        out_specs=[pl.BlockSpec((tq, 128), lambda h, i: (i, h)), pl.BlockSpec((tq, 128), lambda h, i: (i, h)),
                   pl.BlockSpec((None, tq, 128), lambda h, i: (h, i, 0))],
        out_shape=[jax.ShapeDtypeStruct((t, N_HEADS * V_DIM), F32), jax.ShapeDtypeStruct((t, N_HEADS * V_DIM), MXU_DTYPE),
                   jax.ShapeDtypeStruct((N_HEADS, t, 128), F32)],
        scratch_shapes=[pltpu.VMEM((t, QK_PAD), MXU_DTYPE), pltpu.VMEM((t, 2 * V_DIM), MXU_DTYPE),
                        pltpu.VMEM((tq, 128), F32), pltpu.VMEM((tq, 2 * V_DIM), F32),
                        pltpu.VMEM((tq, tk), F32), pltpu.VMEM((tq, tk), F32), pltpu.VMEM((tq, 128), F32),
                        pltpu.VMEM((tq, 128), F32), pltpu.VMEM((ratio + 1, tq, tk), F32),
                        pltpu.SemaphoreType.DMA((2,))],
        compiler_params=_cp(2))(qh, z, kh, vh)


def _attn_bwd(qh, kh, vh, o, lse, dycat, z, name, tq=512):
    t = qh.shape[1]
    tq = min(tq, t)
    nq = t // tq

    def body(q_ref, o_ref, lse_ref, dy_ref, g_ref, k_hbm, v_hbm, dq_ref, dgate_ref, dk_hbm, dv_hbm,
             k_v, v_v, dk_acc, dv_acc, dq_acc, delta_s, s_a, dp_a, s_b, dp_b, bias_s, sem):
        h, i = pl.program_id(0), pl.program_id(1)

        @pl.when(i == 0)
        def _():
            ck = pltpu.make_async_copy(k_hbm.at[h], k_v, sem.at[0])
            cv = pltpu.make_async_copy(v_hbm.at[h], v_v, sem.at[1])
            ck.start()
            cv.start()
            _init_mask_bias(bias_s)
            dk_acc[...] = jnp.zeros(dk_acc.shape, F32)
            dv_acc[...] = jnp.zeros(dv_acc.shape, F32)
            ck.wait()
            cv.wait()

        gate, dy, of = g_ref[...], dy_ref[...], o_ref[...]
        silu, dsilu = _silu_and_grad(gate)
        do = dy * silu
        delta = jnp.sum(do * of, axis=-1, keepdims=True)
        dgate_ref[...] = _mx(dy * of * dsilu)
        dob = _mx(do)
        q = q_ref[...]
        delta_s[...] = jnp.broadcast_to(delta, delta_s.shape)
        dq_acc[...] = jnp.zeros(dq_acc.shape, F32)

        def keys(j):
            return pl.ds(pl.multiple_of(j * tq, tq), tq)

        def scores(s_ref, dp_ref, j):
            s = _dot_nt(q, k_v[keys(j), :]) * (SCALE * LOG2E) + bias_s[(j == i).astype(jnp.int32)]
            s_ref[...] = s - jnp.tile(lse_ref[...], (1, tq // 128))
            dp_ref[...] = _dot_nt(dob, v_v[keys(j), :]) - jnp.tile(delta_s[...], (1, tq // 128))

        def grads(s_ref, dp_ref, j):
            ks = keys(j)
            p = jnp.exp2(s_ref[...])
            ds = p * dp_ref[...] * SCALE
            pb, dsb = _mx(p), _mx(ds)
            dq_acc[...] += _dot(dsb, k_v[ks, :])
            dk_acc[ks, :] += _dot_tn(dsb, q)
            dv_acc[ks, :] += _dot_tn(pb, dob)

        scores(s_a, dp_a, 0)

        def pair(pp, carry):
            scores(s_b, dp_b, 2 * pp + 1)
            grads(s_a, dp_a, 2 * pp)
            scores(s_a, dp_a, jnp.minimum(2 * pp + 2, i))
            grads(s_b, dp_b, 2 * pp + 1)
            return carry

        _loop_in_long_trips((i + 1) // 2, pair, longest=8)

        @pl.when(i % 2 == 0)
        def _():
            grads(s_a, dp_a, i)

        dq_ref[...] = dq_acc[...]

        @pl.when(i == nq - 1)
        def _():
            ck = pltpu.make_async_copy(dk_acc, dk_hbm.at[h], sem.at[0])
            cv = pltpu.make_async_copy(dv_acc, dv_hbm.at[h], sem.at[1])
            ck.start()
            cv.start()
            ck.wait()
            cv.wait()

    return pl.pallas_call(
        body, name=name, grid=(N_HEADS, nq),
        in_specs=[pl.BlockSpec((None, tq, QK_PAD), lambda h, i: (h, i, 0)),
                  pl.BlockSpec((tq, 128), lambda h, i: (i, h)),
                  pl.BlockSpec((None, tq, 128), lambda h, i: (h, i, 0)),
                  pl.BlockSpec((tq, 128), lambda h, i: (i, N_HEADS + h)), _gate_block(tq), ANY, ANY],
        out_specs=[pl.BlockSpec((None, tq, QK_PAD), lambda h, i: (h, i, 0)),
                   pl.BlockSpec((tq, 128), lambda h, i: (i, h)), ANY, ANY],
        out_shape=[jax.ShapeDtypeStruct((N_HEADS, t, QK_PAD), F32), jax.ShapeDtypeStruct((t, Z_GATE), MXU_DTYPE),
                   jax.ShapeDtypeStruct((N_HEADS, t, QK_PAD), F32), jax.ShapeDtypeStruct((N_HEADS, t, V_DIM), F32)],
        scratch_shapes=[pltpu.VMEM((t, QK_PAD), MXU_DTYPE), pltpu.VMEM((t, V_DIM), MXU_DTYPE),
                        pltpu.VMEM((t, QK_PAD), F32), pltpu.VMEM((t, V_DIM), F32), pltpu.VMEM((tq, QK_PAD), F32),
                        pltpu.VMEM((tq, 128), F32)] + [pltpu.VMEM((tq, tq), F32)] * 4
        + [pltpu.VMEM((2, tq, tq), F32), pltpu.SemaphoreType.DMA((2,))],
        compiler_params=_cp(2))(qh, o, lse, dycat, z, kh, vh)


def _out_proj_fwd(yab, yc, w, x, g, target, name, tm=512):
    t, d = x.shape
    tm = min(tm, t)
    is_last = target is not None

    def body(*refs):
        if is_last:
            yab_ref, yc_ref, w_ref, x_ref, g_ref, t_ref, y_ref, dout_ref, loss_ref = refs
            _zero_when(pl.program_id(0) == 0, loss_ref)
        else:
            yab_ref, yc_ref, w_ref, x_ref, g_ref, y_ref, out_ref = refs
        y = _dot(jnp.concatenate([yab_ref[...], yc_ref[...]], axis=1), w_ref[...])
        y_ref[...] = y
        out = x_ref[...] + _rms(y, g_ref[...])[0]
        if is_last:
            diff = out - t_ref[...]
            dout_ref[...] = diff * (1.0 / d)
            part = jnp.sum(jnp.sum(diff * diff, axis=-1, keepdims=True), axis=0, keepdims=True) * (0.5 / d)
            _acc(loss_ref, jnp.broadcast_to(part, (1, 128)))
        else:
            out_ref[...] = out

    row = lambda wd: pl.BlockSpec((tm, wd), lambda i: (i, 0))
    fixed = lambda a, b: pl.BlockSpec((a, b), lambda i: (0, 0))
    in_specs = [row(512), row(512), fixed(d, d), row(d), fixed(1, d)]
    args = [yab, yc, w, x, g]
    out_specs = [row(d), row(d)]
    out_shape = [jax.ShapeDtypeStruct((t, d), F32), jax.ShapeDtypeStruct((t, d), F32)]
    if is_last:
        in_specs.append(row(d))
        args.append(target)
        out_specs.append(fixed(1, 128))
        out_shape.append(jax.ShapeDtypeStruct((1, 128), F32))
    return pl.pallas_call(body, name=name, grid=(t // tm,), in_specs=in_specs, out_specs=out_specs,
                          out_shape=out_shape, compiler_params=_cp())(*args)


def _halves(tm):
    half = tm // 2 if tm >= 512 else tm
    return [pl.ds(s, half) for s in range(0, tm, half)]


def _out_proj_bwd(dout, y, yab, yc, w, g, name, tm=1024):
    t, d = y.shape
    tm = min(tm, t)

    def body(dout_ref, y_ref, yab_ref, yc_ref, w_ref, g_ref, dycat_ref, dw_ref, dg_ref):
        _zero_when(pl.program_id(0) == 0, dw_ref, dg_ref)
        dybs = []
        for rows in _halves(tm):
            y = y_ref[rows, :]
            r = lax.rsqrt(jnp.mean(y * y, axis=-1, keepdims=True) + EPS)
            dy, dgt = _rms_bwd(y, r, g_ref[...], dout_ref[rows, :])
            _acc(dg_ref, _colsum(dgt))
            dybs.append(_mx(dy))
        for rows, dyb in zip(_halves(tm), dybs):
            _acc(dw_ref, _dot_tn(jnp.concatenate([yab_ref[rows, :], yc_ref[rows, :]], axis=1), dyb))
            dycat_ref[rows, :] = _dot_nt(dyb, w_ref[...])

    row = lambda wd: pl.BlockSpec((tm, wd), lambda i: (i, 0))
    fixed = lambda a, b: pl.BlockSpec((a, b), lambda i: (0, 0), pipeline_mode=pl.Buffered(1))
    return pl.pallas_call(
        body, name=name, grid=(t // tm,),
        in_specs=[row(d), row(d), row(512), row(512), fixed(d, d), fixed(1, d)],
        out_specs=[row(d), fixed(d, d), fixed(1, d)],
        out_shape=[jax.ShapeDtypeStruct((t, d), F32), jax.ShapeDtypeStruct((d, d), F32),
                   jax.ShapeDtypeStruct((1, d), F32)],
        compiler_params=_cp())(dout, y, yab, yc, w, g)


def _mesh_pos():
    return lax.axis_index("x"), lax.axis_index("y"), lax.axis_index("c")


def _remote(src, dst, send_sem, recv_sem, to):
    return pltpu.make_async_remote_copy(src_ref=src, dst_ref=dst, send_sem=send_sem, recv_sem=recv_sem,
                                        device_id=to, device_id_type=MESH)


CHUNK_ROWS = 256


def _pieces(rows):
    return [(s, min(CHUNK_ROWS, rows - s)) for s in range(0, rows, CHUNK_ROWS)]


def _piece_table(shapes):
    return [(a, s, sz) for a, shp in enumerate(shapes) for s, sz in _pieces(shp[-2])]


def _gather_weights(shards):
    n = len(shards)
    table = _piece_table([s.shape for s in shards])
    npc = len(table)

    def body(*refs):
        ins, outs = refs[:n], refs[n:2 * n]
        send_sems, recv_sems, fwd_send, fwd_recv = refs[2 * n:]
        x, y, c = _mesh_pos()
        me, sibling = (x, y, c), (x, y, 1 - c)
        chips = [(1 - x, y), (x, 1 - y), (1 - x, 1 - y)]
        slot = lambda cx, cy, layer: 2 * (2 * cx + cy) + layer
        first = []
        for a in range(n):
            for j, (cx, cy) in enumerate(chips):
                first.append(_remote(ins[a].at[c], outs[a].at[slot(x, y, c)], send_sems.at[a, j], recv_sems.at[a, j],
                                     (cx, cy, c)))
                first[-1].start()
        passed = []
        for j, (cx, cy) in enumerate(chips):
            for a in range(n):
                blk = outs[a].at[slot(cx, cy, c)]
                _remote(blk, blk, send_sems.at[a, j], recv_sems.at[a, j], me).wait_recv()
            for q, (a, s, sz) in enumerate(table):
                rows = outs[a].at[slot(cx, cy, c), pl.ds(s, sz)]
                passed.append(_remote(rows, rows, fwd_send.at[j, q], fwd_recv.at[j, q], sibling))
                passed[-1].start()
        for j, (cx, cy) in enumerate(chips):
            for q, (a, s, sz) in enumerate(table):
                rows = outs[a].at[slot(cx, cy, 1 - c), pl.ds(s, sz)]
                _remote(rows, rows, fwd_send.at[j, q], fwd_recv.at[j, q], me).wait_recv()
        for cp in first + passed:
            cp.wait_send()

    return pl.pallas_call(
        body, name="gather_weights", in_specs=[ANY] * n, out_specs=[ANY] * n,
        out_shape=[jax.ShapeDtypeStruct((8,) + s.shape[1:], s.dtype) for s in shards],
        scratch_shapes=[pltpu.SemaphoreType.DMA((n, 3)), pltpu.SemaphoreType.DMA((n, 3)),
                        pltpu.SemaphoreType.DMA((3, npc)), pltpu.SemaphoreType.DMA((3, npc))])(*shards)


def _pair_exchange(parts, common):
    n = len(parts)
    table = _piece_table([p.shape for p in parts] + [common.shape])
    npc = len(table)

    def body(*refs):
        ins, outs = refs[:n + 1], refs[n + 1:2 * n + 2]
        send_sems, recv_sems = refs[2 * n + 2:]
        x, y, c = _mesh_pos()
        sent = []
        for k in range(4):
            for q, (a, s, sz) in enumerate(table):
                if a == n and k > 0:
                    continue
                src = ins[a].at[2 * k + 1 - c, pl.ds(s, sz)] if a < n else ins[a].at[pl.ds(s, sz)]
                dst = outs[a].at[k, pl.ds(s, sz)] if a < n else outs[a].at[pl.ds(s, sz)]
                sent.append(_remote(src, dst, send_sems.at[k, q], recv_sems.at[k, q], (x, y, 1 - c)))
                sent[-1].start()
        for k in range(4):
            for q, (a, s, sz) in enumerate(table):
                if a == n and k > 0:
                    continue
                dst = outs[a].at[k, pl.ds(s, sz)] if a < n else outs[a].at[pl.ds(s, sz)]
                _remote(dst, dst, send_sems.at[k, q], recv_sems.at[k, q], (x, y, c)).wait_recv()
        for cp in sent:
            cp.wait_send()

    return pl.pallas_call(
        body, name="grad_pair_exchange", in_specs=[ANY] * (n + 1), out_specs=[ANY] * (n + 1),
        out_shape=[jax.ShapeDtypeStruct((4,) + p.shape[1:], p.dtype) for p in parts]
        + [jax.ShapeDtypeStruct(common.shape, common.dtype)],
        scratch_shapes=[pltpu.SemaphoreType.DMA((4, npc)), pltpu.SemaphoreType.DMA((4, npc))])(*parts, common)


def _chip_exchange(parts, common):
    n = len(parts)
    table = _piece_table([p.shape for p in parts] + [common.shape])
    npc = len(table)

    def body(*refs):
        ins, outs = refs[:n + 1], refs[n + 1:2 * n + 2]
        send_sems, recv_sems = refs[2 * n + 2:]
        x, y, c = _mesh_pos()
        mine = 2 * x + y
        chips = [(1 - x, y), (x, 1 - y), (1 - x, 1 - y)]
        src = lambda a, k: ins[a].at[k] if a < n else ins[a]
        sent = []
        for j, (cx, cy) in enumerate(chips):
            for q, (a, s, sz) in enumerate(table):
                sent.append(_remote(src(a, 2 * cx + cy).at[pl.ds(s, sz)], outs[a].at[mine, pl.ds(s, sz)],
                                    send_sems.at[j, q], recv_sems.at[j, q], (cx, cy, c)))
                sent[-1].start()
        for j, (cx, cy) in enumerate(chips):
            for q, (a, s, sz) in enumerate(table):
                dst = outs[a].at[2 * cx + cy, pl.ds(s, sz)]
                _remote(dst, dst, send_sems.at[j, q], recv_sems.at[j, q], (x, y, c)).wait_recv()
        for cp in sent:
            cp.wait_send()

    return pl.pallas_call(
        body, name="grad_chip_exchange", in_specs=[ANY] * (n + 1), out_specs=[ANY] * (n + 1),
        out_shape=[jax.ShapeDtypeStruct(p.shape, p.dtype) for p in parts]
        + [jax.ShapeDtypeStruct((4,) + common.shape, common.dtype)],
        scratch_shapes=[pltpu.SemaphoreType.DMA((3, npc)), pltpu.SemaphoreType.DMA((3, npc))])(*parts, common)


def _sibling_exchange(sums):
    n = len(sums)
    table = _piece_table([s.shape for s in sums])
    npc = len(table)

    def body(*refs):
        ins, outs = refs[:n], refs[n:2 * n]
        send_sems, recv_sems = refs[2 * n:]
        x, y, c = _mesh_pos()
        sent = []
        for q, (a, s, sz) in enumerate(table):
            sent.append(_remote(ins[a].at[pl.ds(s, sz)], outs[a].at[pl.ds(s, sz)], send_sems.at[q], recv_sems.at[q],
                                (x, y, 1 - c)))
            sent[-1].start()
        for q, (a, s, sz) in enumerate(table):
            dst = outs[a].at[pl.ds(s, sz)]
            _remote(dst, dst, send_sems.at[q], recv_sems.at[q], (x, y, c)).wait_recv()
        for cp in sent:
            cp.wait_send()

    return pl.pallas_call(
        body, name="sibling_exchange", in_specs=[ANY] * n, out_specs=[ANY] * n,
        out_shape=[jax.ShapeDtypeStruct(s.shape, s.dtype) for s in sums],
        scratch_shapes=[pltpu.SemaphoreType.DMA((npc,)), pltpu.SemaphoreType.DMA((npc,))])(*sums)


def _pair_sum(parts, got, name, tr):
    _, r, c = got.shape
    tr = min(tr, r)

    def body(p_ref, g_ref, o_ref, w_ref):
        total = p_ref[...] + g_ref[...]
        o_ref[...] = total
        w_ref[...] = total.astype(WIRE_DTYPE)

    blk = pl.BlockSpec((None, tr, c), lambda k, i: (k, i, 0))
    mine = pl.BlockSpec((None, tr, c), lambda k, i: (2 * k + lax.axis_index("c"), i, 0))
    return pl.pallas_call(
        body, name=name, grid=(4, r // tr), in_specs=[mine, blk], out_specs=[blk, blk],
        out_shape=[jax.ShapeDtypeStruct(got.shape, F32), jax.ShapeDtypeStruct(got.shape, WIRE_DTYPE)],
        compiler_params=_cp(2))(parts, got)


def _sum_chips(pair_sums, recv, name, tr):
    _, r, c = pair_sums.shape
    tr = min(tr, r)

    def body(own_ref, r_ref, o_ref):
        chip = 2 * lax.axis_index("x") + lax.axis_index("y")
        own_blk = own_ref[...]
        acc = jnp.where(chip == 0, own_blk, r_ref[0].astype(F32))
        for k in range(1, 4):
            acc = acc + jnp.where(chip == k, own_blk, r_ref[k].astype(F32))
        o_ref[...] = acc

    return pl.pallas_call(
        body, name=name, grid=(r // tr,),
        in_specs=[pl.BlockSpec((None, tr, c), lambda i: (2 * lax.axis_index("x") + lax.axis_index("y"), i, 0)),
                  pl.BlockSpec((4, tr, c), lambda i: (0, i, 0))],
        out_specs=pl.BlockSpec((tr, c), lambda i: (i, 0)),
        out_shape=jax.ShapeDtypeStruct((r, c), F32), compiler_params=_cp())(pair_sums, recv)


def _sum_leading(parts, name, tr):
    nlead, r, c = parts.shape
    tr = min(tr, r)

    def body(p_ref, o_ref):
        acc = p_ref[0]
        for j in range(1, nlead):
            acc = acc + p_ref[j]
        o_ref[...] = acc

    return pl.pallas_call(
        body, name=name, grid=(r // tr,),
        in_specs=[pl.BlockSpec((nlead, tr, c), lambda i: (0, i, 0))], out_specs=pl.BlockSpec((tr, c), lambda i: (i, 0)),
        out_shape=jax.ShapeDtypeStruct((r, c), parts.dtype), compiler_params=_cp())(parts)


def _adamw_update(w_ref, g_ref, m_ref, v_ref, d_ref, nm_ref, nv_ref):
    gg = g_ref[...]
    nm = ADAM_B1 * m_ref[...] + (1.0 - ADAM_B1) * gg
    nv = ADAM_B2 * v_ref[...] + (1.0 - ADAM_B2) * jnp.square(gg)
    m_hat = nm / (1.0 - ADAM_B1 ** ADAM_STEP)
    v_hat = nv / (1.0 - ADAM_B2 ** ADAM_STEP)
    d_ref[...] = -ADAM_LR * (m_hat / (jnp.sqrt(v_hat) + ADAM_EPS) + ADAM_WD * w_ref[...])
    nm_ref[...] = nm
    nv_ref[...] = nv


def _adamw(w, g, m, v, name, tr):
    r, c = w.shape
    tr = min(tr, r)

    def body(w_ref, g_ref, m_ref, v_ref, d_ref, nm_ref, nv_ref):
        _adamw_update(w_ref, g_ref, m_ref, v_ref, d_ref, nm_ref, nv_ref)

    blk = pl.BlockSpec((tr, c), lambda i: (i, 0))
    return pl.pallas_call(
        body, name=name, grid=(r // tr,), in_specs=[blk] * 4, out_specs=[blk] * 3,
        out_shape=[jax.ShapeDtypeStruct((r, c), F32)] * 3, compiler_params=_cp())(w, g, m, v)


def _adamw_many(ws, gs, ms, vs, name):
    n = len(ws)

    def body(*refs):
        ins, outs = refs[:4 * n], refs[4 * n:]
        for k in range(n):
            _adamw_update(ins[k], ins[n + k], ins[2 * n + k], ins[3 * n + k], outs[k], outs[n + k], outs[2 * n + k])

    vmem = pl.BlockSpec(memory_space=pltpu.VMEM)
    res = pl.pallas_call(
        body, name=name, in_specs=[vmem] * (4 * n), out_specs=[vmem] * (3 * n),
        out_shape=[jax.ShapeDtypeStruct(a.shape, F32) for a in ws] * 3)(*ws, *gs, *ms, *vs)
    return res[:n], res[n:2 * n], res[2 * n:]


def _rope_tables(positions):
    inv_freq = ROPE_BASE ** (-jnp.arange(0, 64, 2, dtype=F32) / 64)
    ang = positions.astype(F32)[:, None] * inv_freq
    cos, sin = jnp.cos(ang), jnp.sin(ang)
    t = positions.shape[0]
    rc = jnp.concatenate([jnp.ones((t, 128), F32), cos, cos, jnp.ones((t, 64), F32)], axis=1)
    rs = jnp.concatenate([jnp.zeros((t, 128), F32), sin, sin, jnp.zeros((t, 64), F32)], axis=1)
    return rc, rs


def _layer_params(l, w_in, w_uq, w_ukv, w_out, small):
    p = {}
    p["w_in"] = jnp.concatenate([w_in[l][:, :1984], jnp.zeros((1024, 64), w_in.dtype), w_in[l][:, 1984:]], axis=1)
    p["w_uq"] = jnp.pad(w_uq[l].reshape(384, 4, 192), ((0, 0), (0, 0), (0, 64))).reshape(384, 1024)
    p["w_ukv"] = w_ukv[l].reshape(256, 4, 2, 128).transpose(0, 2, 1, 3).reshape(256, 1024)
    p["w_out"] = w_out[l]
    p["pre_g"] = small["pre_norm_g"][l][None]
    p["post_g"] = small["post_norm_g"][l][None]
    p["sgu_w"] = small["sgu_w"][l].reshape(512, 128)
    p["sgu_wt"] = small["sgu_w"][l].transpose(0, 2, 1).reshape(512, 128)
    p["sgu_bias"] = jnp.repeat(small["sgu_b"][l].T, 64, axis=1)
    p["ln_g"] = small["sgu_ln_g"][l][None]
    p["ln_b"] = small["sgu_ln_b"][l][None]
    p["pool_wbd"] = _mx(jax.scipy.linalg.block_diag(*[small["pool_w"][l][gi] for gi in range(4)]))
    p["pool_scale"] = small["pool_scale"][l][None]
    p["gq"] = small["q_norm_g"][l][None]
    p["gkv"] = small["kv_norm_g"][l][None]
    return p


def _layer_fwd(l, x, p, rc, rs, target):
    z, h = _in_proj_fwd(x, p["pre_g"], p["w_in"], f"in_proj_fwd_{l}")
    yab = _mix_fwd(z, p["sgu_w"], p["sgu_bias"], p["ln_g"], p["ln_b"], p["pool_wbd"], p["pool_scale"], f"mix_fwd_{l}")
    qh, kh, vh = _qkv_fwd(z, rc, rs, p["w_uq"], p["w_ukv"], p["gq"], p["gkv"], f"qkv_fwd_{l}")
    o, yc, lse = _attn_fwd(qh, kh, vh, z, f"attn_fwd_{l}")
    outs = _out_proj_fwd(yab, yc, p["w_out"], x, p["post_g"], target, f"out_proj_fwd_{l}")
    saved = dict(x=x, z=z, h=h, yab=yab, qh=qh, kh=kh, vh=vh, o=o, yc=yc, lse=lse, y=outs[0])
    return saved, outs[1:]


def _layer_bwd(l, dout, sv, p, rc, rs):
    dycat, dw_out, dpost = _out_proj_bwd(dout, sv["y"], sv["yab"], sv["yc"], p["w_out"], p["post_g"], f"out_proj_bwd_{l}")
    dq, dgate, dk, dv = _attn_bwd(sv["qh"], sv["kh"], sv["vh"], sv["o"], sv["lse"], dycat, sv["z"], f"attn_bwd_{l}")
    dzc, dzk, dwq, dwkv, dgq, dgkv = _qkv_bwd(dq, dk, dv, sv["z"], rc, rs, p["w_uq"], p["w_ukv"], p["gq"], p["gkv"],
                                              f"qkv_bwd_{l}")
    dzm, dsw, dsb, dlng, dlnb, dpw, dps = _mix_bwd(sv["z"], dycat, p["sgu_w"], p["sgu_wt"], p["sgu_bias"], p["ln_g"],
                                                   p["ln_b"], p["pool_wbd"], p["pool_scale"], f"mix_bwd_{l}")
    dx, dw_in, dpre = _in_proj_bwd(dzm, dzc, dzk, dgate, sv["h"], sv["x"], dout, p["w_in"], p["pre_g"], f"in_proj_bwd_{l}")
    grads = {
        "pre_norm_g": dpre[0], "post_norm_g": dpost[0],
        "w_in": jnp.concatenate([dw_in[:, :1984], dw_in[:, 2048:]], axis=1),
        "sgu_w": dsw.reshape(4, 128, 128), "sgu_b": dsb[:, :4].T, "sgu_ln_g": dlng[0], "sgu_ln_b": dlnb[0],
        "pool_w": jnp.stack([dpw[64 * gi:64 * gi + 64, 64 * gi:64 * gi + 64] for gi in range(4)]),
        "pool_scale": dps[0], "q_norm_g": dgq[0],
        "w_uq": dwq.reshape(384, 4, 256)[:, :, :192].reshape(384, 768), "kv_norm_g": dgkv[0],
        "w_ukv": dwkv.reshape(256, 2, 4, 128).transpose(0, 2, 1, 3).reshape(256, 1024), "w_out": dw_out,
    }
    return dx, grads


SMALL_NAMES = ["pre_norm_g", "post_norm_g", "sgu_w", "sgu_b", "sgu_ln_g", "sgu_ln_b", "pool_w", "pool_scale",
               "q_norm_g", "kv_norm_g"]
BIG_NAMES = ["w_in", "w_uq", "w_ukv", "w_out"]
WEIGHT_NAMES = ["pre_norm_g", "post_norm_g", "w_in", "sgu_w", "sgu_b", "sgu_ln_g", "sgu_ln_b", "pool_w", "pool_scale",
                "q_norm_g", "w_uq", "kv_norm_g", "w_ukv", "w_out"]


def _local_step(x, positions, target, w_in, w_uq, w_ukv, w_out, small):
    rc, rs = _rope_tables(positions)
    params = [_layer_params(l, w_in, w_uq, w_ukv, w_out, small) for l in range(DEPTH)]
    saved = []
    for l in range(DEPTH):
        sv, outs = _layer_fwd(l, x, params[l], rc, rs, target if l == DEPTH - 1 else None)
        saved.append(sv)
        if l < DEPTH - 1:
            x = outs[0]
    dout, loss = outs
    grads = [None] * DEPTH
    for l in reversed(range(DEPTH)):
        dout, grads[l] = _layer_bwd(l, dout, saved[l], params[l], rc, rs)
    return loss[0, 0], dout, {k: jnp.stack([grads[l][k] for l in range(DEPTH)]) for k in WEIGHT_NAMES}


def _pack_small(tree, extra=None):
    pieces = [tree[k].reshape(-1) for k in SMALL_NAMES]
    pieces.append(jnp.zeros((1,), F32) if extra is None else extra.reshape(1))
    flat = jnp.concatenate(pieces)
    rows = -(-flat.shape[0] // 1024) * 8
    return jnp.pad(flat, (0, rows * 128 - flat.shape[0])).reshape(rows, 128)


def _unpack_small(packed, like):
    flat = packed.reshape(-1)
    out, off = {}, 0
    for k in SMALL_NAMES:
        size = like[k].size
        out[k] = flat[off:off + size].reshape(like[k].shape)
        off += size
    return out, flat[off]


def kernel(x, positions, pre_norm_g, post_norm_g, w_in, sgu_w, sgu_b, sgu_ln_g, sgu_ln_b, pool_w, pool_scale, q_norm_g, w_uq, kv_norm_g, w_ukv, w_out, loss_target, m_pre_norm_g, m_post_norm_g, m_w_in, m_sgu_w, m_sgu_b, m_sgu_ln_g, m_sgu_ln_b, m_pool_w, m_pool_scale, m_q_norm_g, m_w_uq, m_kv_norm_g, m_w_ukv, m_w_out, v_pre_norm_g, v_post_norm_g, v_w_in, v_sgu_w, v_sgu_b, v_sgu_ln_g, v_sgu_ln_b, v_pool_w, v_pool_scale, v_q_norm_g, v_w_uq, v_kv_norm_g, v_w_ukv, v_w_out):
    w = dict(pre_norm_g=pre_norm_g, post_norm_g=post_norm_g, w_in=w_in, sgu_w=sgu_w, sgu_b=sgu_b, sgu_ln_g=sgu_ln_g,
             sgu_ln_b=sgu_ln_b, pool_w=pool_w, pool_scale=pool_scale, q_norm_g=q_norm_g, w_uq=w_uq, kv_norm_g=kv_norm_g,
             w_ukv=w_ukv, w_out=w_out)
    m = dict(pre_norm_g=m_pre_norm_g, post_norm_g=m_post_norm_g, w_in=m_w_in, sgu_w=m_sgu_w, sgu_b=m_sgu_b,
             sgu_ln_g=m_sgu_ln_g, sgu_ln_b=m_sgu_ln_b, pool_w=m_pool_w, pool_scale=m_pool_scale, q_norm_g=m_q_norm_g,
             w_uq=m_w_uq, kv_norm_g=m_kv_norm_g, w_ukv=m_w_ukv, w_out=m_w_out)
    v = dict(pre_norm_g=v_pre_norm_g, post_norm_g=v_post_norm_g, w_in=v_w_in, sgu_w=v_sgu_w, sgu_b=v_sgu_b,
             sgu_ln_g=v_sgu_ln_g, sgu_ln_b=v_sgu_ln_b, pool_w=v_pool_w, pool_scale=v_pool_scale, q_norm_g=v_q_norm_g,
             w_uq=v_w_uq, kv_norm_g=v_kv_norm_g, w_ukv=v_w_ukv, w_out=v_w_out)

    core = lax.axis_index("c")
    chip = 2 * lax.axis_index("x") + lax.axis_index("y")
    shards = [_mx(w[k]) for k in BIG_NAMES]
    gathered = _gather_weights(shards)
    g_in, g_uq, g_ukv, g_out = [lax.dynamic_update_slice(g, s, (2 * chip, 0, 0)) for g, s in zip(gathered, shards)]
    cols = lambda g: g.reshape((4, 2) + g.shape[1:]).transpose(1, 2, 0, 3).reshape(2, g.shape[1], 4 * g.shape[2])
    full_out = g_out.reshape(4, 2, 256, 1024).transpose(1, 0, 2, 3).reshape(2, 1024, 1024)
    loss, dx, grads = _local_step(x[0], positions[0], loss_target[0], cols(g_in), cols(g_uq), cols(g_ukv), full_out, w)

    split_cols = lambda g: g.reshape(2, g.shape[1], 4, g.shape[2] // 4).transpose(2, 0, 1, 3).reshape(8, g.shape[1], g.shape[2] // 4)
    parts = [split_cols(grads["w_in"]), split_cols(grads["w_uq"]), split_cols(grads["w_ukv"]),
             grads["w_out"].reshape(2, 4, 256, 1024).transpose(1, 0, 2, 3).reshape(8, 256, 1024)]
    common = _pack_small(grads, loss)
    got = _pair_exchange(parts, common)
    pair_sums = [_pair_sum(parts[a], got[a], f"pair_sum_{BIG_NAMES[a]}", 128) for a in range(4)]
    chip_common = _sum_leading(jnp.stack([common, got[4]]), "pair_sum_small", common.shape[0])
    received = _chip_exchange([ps[1] for ps in pair_sums], chip_common)
    sums = [_sum_chips(pair_sums[a][0], received[a], f"sum_{BIG_NAMES[a]}", 128) for a in range(4)]
    all_common = lax.dynamic_update_slice(received[4], chip_common[None], (chip, 0, 0))
    small_sum, loss = _unpack_small(_sum_leading(all_common, "sum_small", all_common.shape[1]), w)
    others = _sibling_exchange(sums)
    total = dict(small_sum)
    for a, k in enumerate(BIG_NAMES):
        total[k] = jnp.where(core == 0, jnp.stack([sums[a], others[a]]), jnp.stack([others[a], sums[a]]))

    rows2d = lambda a: a.reshape(-1, a.shape[-1])
    small_out = _adamw_many(*[[rows2d(tree[k]) for k in SMALL_NAMES] for tree in (w, total, m, v)], "adamw_small")
    delta, new_m, new_v = ({k: r.reshape(w[k].shape) for k, r in zip(SMALL_NAMES, res)} for res in small_out)
    for k in BIG_NAMES:
        shape = w[k].shape
        flat = lambda a: a.reshape(shape[0] * shape[1], shape[2])
        res = _adamw(flat(w[k]), flat(total[k]), flat(m[k]), flat(v[k]), f"adamw_{k}", 256)
        delta[k], new_m[k], new_v[k] = (r.reshape(shape) for r in res)

    return (loss, dx[None], *[total[k] for k in WEIGHT_NAMES], *[delta[k] for k in WEIGHT_NAMES],
            *[new_m[k] for k in WEIGHT_NAMES], *[new_v[k] for k in WEIGHT_NAMES])
```

```python
import jax
import jax.numpy as jnp
from jax import lax
from jax.experimental import pallas as pl
from jax.experimental.pallas import tpu as pltpu

F32 = jnp.float32
MXU_DTYPE = jnp.bfloat16
WIRE_DTYPE = jnp.bfloat16
EPS = 1e-6
NEG_INF = -1e30
CHUNK = 64
DEPTH = 2
N_HEADS = 4
QK_PAD = 256
V_DIM = 128
SCALE = 192 ** -0.5
LOG2E = 1.4426950408889634
ROPE_BASE = 10000.0
ADAM_LR, ADAM_B1, ADAM_B2, ADAM_EPS, ADAM_WD, ADAM_STEP = 0.001, 0.9, 0.999, 1e-08, 0.01, 10
VMEM_LIMIT_BYTES = 56 * 1024 * 1024
MESH = pl.DeviceIdType.MESH
ANY = pl.BlockSpec(memory_space=pl.ANY)

Z_MIX, Z_C, Z_KR, Z_GATE = 1280, 640, 128, 512
Z_W = Z_MIX + Z_C + Z_KR + Z_GATE


def _cp(n_axes=1):
    return pltpu.CompilerParams(dimension_semantics=("arbitrary",) * n_axes, vmem_limit_bytes=VMEM_LIMIT_BYTES)


def _dot(a, b):
    return lax.dot_general(a, b, (((1,), (0,)), ((), ())), preferred_element_type=F32)


def _dot_nt(a, b):
    return lax.dot_general(a, b, (((1,), (1,)), ((), ())), preferred_element_type=F32)


def _dot_tn(a, b):
    return lax.dot_general(a, b, (((0,), (0,)), ((), ())), preferred_element_type=F32)


def _mx(a):
    return a.astype(MXU_DTYPE)


def _silu_and_grad(g):
    sg = jax.nn.sigmoid(g)
    return g * sg, sg * (1.0 + g * (1.0 - sg))


def _rms(x, g):
    r = lax.rsqrt(jnp.mean(x * x, axis=-1, keepdims=True) + EPS)
    return x * r * g, r


def _rms_bwd(x, r, g, dy):
    xhat = x * r
    dyg = dy * g
    dx = r * (dyg - xhat * jnp.mean(dyg * xhat, axis=-1, keepdims=True))
    return dx, dy * xhat


def _zero_when(first, *refs):
    @pl.when(first)
    def _():
        for ref in refs:
            ref[...] = jnp.zeros(ref.shape, ref.dtype)


def _acc(ref, val):
    ref[...] += val


def _colsum(a):
    return jnp.sum(a, axis=0, keepdims=True)


def _in_proj_fwd(x, g, w, name, tm=1024):
    t, d = x.shape
    n = w.shape[1]
    tm = min(tm, t)

    def body(x_ref, g_ref, w_ref, z_ref, h_ref):
        hs = []
        for rows in _halves(tm):
            h, _ = _rms(x_ref[rows, :], g_ref[...])
            hs.append(_mx(h))
            h_ref[rows, :] = hs[-1]
        for rows, h in zip(_halves(tm), hs):
            z_ref[rows, :] = _dot(h, w_ref[...])

    return pl.pallas_call(
        body, name=name, grid=(t // tm,),
        in_specs=[pl.BlockSpec((tm, d), lambda i: (i, 0)), pl.BlockSpec((1, d), lambda i: (0, 0)),
                  pl.BlockSpec((d, n), lambda i: (0, 0), pipeline_mode=pl.Buffered(1))],
        out_specs=[pl.BlockSpec((tm, n), lambda i: (i, 0)), pl.BlockSpec((tm, d), lambda i: (i, 0))],
        out_shape=[jax.ShapeDtypeStruct((t, n), F32), jax.ShapeDtypeStruct((t, d), MXU_DTYPE)],
        compiler_params=_cp())(x, g, w)


def _in_proj_bwd(dz_mix, dz_c, dz_kr, dz_gate, h, x, d_res, w, g, name, tm=512):
    t, d = x.shape
    n = w.shape[1]
    tm = min(tm, t)

    def body(dm_ref, dc_ref, dk_ref, dg_ref, h_ref, x_ref, dres_ref, w_ref, g_ref, dx_ref, dw_ref, dgn_ref):
        first = pl.program_id(0) == 0
        dz = jnp.concatenate([dm_ref[...], dc_ref[...], dk_ref[...], dg_ref[...]], axis=1)

        _zero_when(first, dw_ref, dgn_ref)
        dh = _dot_nt(dz, w_ref[...])
        hb = h_ref[...]
        for c0 in range(0, n, 512):
            dw_ref[:, c0:c0 + 512] += _dot_tn(hb, dz[:, c0:c0 + 512])
        xf = x_ref[...]
        r = lax.rsqrt(jnp.mean(xf * xf, axis=-1, keepdims=True) + EPS)
        dx, dgt = _rms_bwd(xf, r, g_ref[...], dh)
        dx_ref[...] = dx + dres_ref[...]
        _acc(dgn_ref, _colsum(dgt))

    row = lambda wd: pl.BlockSpec((tm, wd), lambda i: (i, 0))
    fixed = lambda a, b: pl.BlockSpec((a, b), lambda i: (0, 0), pipeline_mode=pl.Buffered(1))
    return pl.pallas_call(
        body, name=name, grid=(t // tm,),
        in_specs=[row(Z_MIX), row(Z_C), row(Z_KR), row(Z_GATE), row(d), row(d), row(d), fixed(d, n), fixed(1, d)],
        out_specs=[row(d), fixed(d, n), fixed(1, d)],
        out_shape=[jax.ShapeDtypeStruct((t, d), F32), jax.ShapeDtypeStruct((d, n), F32),
                   jax.ShapeDtypeStruct((1, d), F32)],
        compiler_params=_cp())(dz_mix, dz_c, dz_kr, dz_gate, h, x, d_res, w, g)


def _lane_group(shape):
    return lax.broadcasted_iota(jnp.int32, shape, 1) // 64


def _select_group(vals):
    grp = _lane_group(vals[0].shape)
    out = vals[3]
    for gi in (2, 1, 0):
        out = jnp.where(grp == gi, vals[gi], out)
    return out


def _sgu_mask(transposed):
    r = (lax.broadcasted_iota(jnp.int32, (512, 128), 0) % 128) // CHUNK
    c = lax.broadcasted_iota(jnp.int32, (512, 128), 1) // CHUNK
    return (r <= c) if transposed else (c <= r)


def _sgu_apply(wstack, vb, nblk):
    outs = []
    for n in range(nblk):
        r = _dot(wstack, vb[n * 128:(n + 1) * 128, :])
        outs.append(_select_group([r[hh * 128:(hh + 1) * 128, :] for hh in range(4)]))
    return jnp.concatenate(outs, axis=0)


def _layer_norm(v, g, b):
    mu = jnp.mean(v, axis=-1, keepdims=True)
    vc = v - mu
    rstd = lax.rsqrt(jnp.mean(vc * vc, axis=-1, keepdims=True) + EPS)
    vhat = vc * rstd
    return vhat * g + b, vhat, rstd


def _div_by_counts(x, t0):
    t = t0 + lax.broadcasted_iota(jnp.int32, (16, 256), 0)
    w = _select_group([jnp.full((16, 256), wv, jnp.int32) for wv in (2, 4, 8, 16)])
    head = x[:16, :] / jnp.minimum(t + 1, w).astype(F32)
    inv_w = _select_group([jnp.full((1, 256), 1.0 / wv, F32) for wv in (2, 4, 8, 16)])
    return jnp.concatenate([head, x[16:, :] * inv_w], axis=0)


def _pooled(p, halo, t0):
    tm = p.shape[0]
    ext = jnp.concatenate([halo, p], axis=0)
    s2 = ext + pltpu.roll(ext, 1, 0)
    s4 = s2 + pltpu.roll(s2, 2, 0)
    s8 = s4 + pltpu.roll(s4, 4, 0)
    s16 = s8 + pltpu.roll(s8, 8, 0)
    sel = _select_group([s2, s4, s8, s16])[16:, :]
    return _div_by_counts(sel, t0) - p


def _pooled_bwd(dpool, dpool_halo, t0):
    tm = dpool.shape[0]
    n = tm + 16
    ext = _div_by_counts(jnp.concatenate([dpool, dpool_halo], axis=0), t0)
    f2 = ext + pltpu.roll(ext, n - 1, 0)
    f4 = f2 + pltpu.roll(f2, n - 2, 0)
    f8 = f4 + pltpu.roll(f4, n - 4, 0)
    f16 = f8 + pltpu.roll(f8, n - 8, 0)
    return _select_group([f2, f4, f8, f16])[:tm, :] - dpool


def _mix_specs(t, tm):
    nt16 = t // 16
    zrow = pl.BlockSpec((tm, Z_MIX), lambda i: (i, 0))
    prev_halo = pl.BlockSpec((16, 256), lambda i: (jnp.maximum(i * (tm // 16) - 1, 0), 3))
    fixed = lambda a, b: pl.BlockSpec((a, b), lambda i: (0, 0))
    params = [fixed(512, 128), fixed(128, 256), fixed(1, 256), fixed(1, 256), fixed(256, 256), fixed(1, 256)]
    return nt16, zrow, prev_halo, fixed, params


def _mix_fwd(z, sgu_w, sgu_bias, ln_g, ln_b, pool_wbd, pool_scale, name, tm=512):
    t = z.shape[0]
    tm = min(tm, t)
    _, zrow, prev_halo, _, params = _mix_specs(t, tm)

    def body(z_ref, halo_ref, w_ref, bias_ref, lng_ref, lnb_ref, pw_ref, ps_ref, y_ref):
        i = pl.program_id(0)
        u, v, gate = z_ref[:, 0:256], z_ref[:, 256:512], z_ref[:, 512:768]
        p, pgate = z_ref[:, 768:1024], z_ref[:, 1024:1280]
        vn, _, _ = _layer_norm(v, lng_ref[...], lnb_ref[...])
        wm = _mx(jnp.where(_sgu_mask(False), w_ref[...], 0.0))
        halo = jnp.where(i > 0, halo_ref[...], 0.0)
        pooled = _pooled(p, halo, i * tm)
        mixed = _sgu_apply(wm, _mx(vn), tm // 128) + jnp.tile(bias_ref[...], (tm // 128, 1))
        mixedp = _dot(_mx(pooled), pw_ref[...])
        ya = u * mixed * _silu_and_grad(gate)[0]
        yb = mixedp * ps_ref[...] * _silu_and_grad(pgate)[0]
        y_ref[...] = _mx(jnp.concatenate([ya, yb], axis=1))

    return pl.pallas_call(
        body, name=name, grid=(t // tm,),
        in_specs=[zrow, prev_halo] + params,
        out_specs=pl.BlockSpec((tm, 512), lambda i: (i, 0)),
        out_shape=jax.ShapeDtypeStruct((t, 512), MXU_DTYPE),
        compiler_params=_cp())(z, z, sgu_w, sgu_bias, ln_g, ln_b, pool_wbd, pool_scale)


def _mix_bwd(z, dycat, sgu_w, sgu_wt, sgu_bias, ln_g, ln_b, pool_wbd, pool_scale, name, tm=512):
    t = z.shape[0]
    tm = min(tm, t)
    nt16, zrow, prev_halo, fixed, params = _mix_specs(t, tm)
    nblk = tm // 128
    last = t // tm - 1

    def body(z_ref, halo_ref, zn_ref, dy_ref, dyn_ref, w_ref, wt_ref, bias_ref, lng_ref, lnb_ref, pw_ref, ps_ref,
             dz_ref, dw_ref, db_ref, dlng_ref, dlnb_ref, dpw_ref, dps_ref):
        i = pl.program_id(0)
        _zero_when(i == 0, dw_ref, db_ref, dlng_ref, dlnb_ref, dpw_ref, dps_ref)
        u, v, gate = z_ref[:, 0:256], z_ref[:, 256:512], z_ref[:, 512:768]
        p, pgate = z_ref[:, 768:1024], z_ref[:, 1024:1280]
        dya, dyb = dy_ref[:, 0:256], dy_ref[:, 256:512]
        vn, vhat, rstd = _layer_norm(v, lng_ref[...], lnb_ref[...])
        vnb = _mx(vn)
        wm = _mx(jnp.where(_sgu_mask(False), w_ref[...], 0.0))
        wmt = _mx(jnp.where(_sgu_mask(True), wt_ref[...], 0.0))
        halo = jnp.where(i > 0, halo_ref[...], 0.0)
        pooled_b = _mx(_pooled(p, halo, i * tm))
        silu, dsilu = _silu_and_grad(gate)
        psilu, pdsilu = _silu_and_grad(pgate)
        mixed = _sgu_apply(wm, vnb, nblk) + jnp.tile(bias_ref[...], (nblk, 1))
        mixedp = _dot(pooled_b, pw_ref[...])
        t1 = u * mixed
        d_gate = dya * t1 * dsilu
        d_t1 = dya * silu
        d_u = d_t1 * mixed
        d_mixed = d_t1 * u
        dmb = _mx(d_mixed)
        d_pgate = dyb * (mixedp * ps_ref[...]) * pdsilu
        d_ms = dyb * psilu
        _acc(dps_ref, _colsum(d_ms * mixedp))
        dmpb = _mx(d_ms * ps_ref[...])
        dmp_halo = _mx(dyn_ref[...] * _silu_and_grad(zn_ref[...])[0] * ps_ref[...])
        d_vn = _sgu_apply(wmt, dmb, nblk)
        grp = _lane_group((128, 256))
        lane = lax.broadcasted_iota(jnp.int32, (128, 128), 1)
        dws = [jnp.zeros((128, 128), F32) for _ in range(4)]
        dbias = jnp.zeros((128, 128), F32)
        for n in range(nblk):
            dm_n, dmb_n, vnb_n = d_mixed[n * 128:(n + 1) * 128], dmb[n * 128:(n + 1) * 128], vnb[n * 128:(n + 1) * 128]
            for hh in range(4):
                dws[hh] = dws[hh] + _dot_nt(jnp.where(grp == hh, dmb_n, jnp.zeros_like(dmb_n)), vnb_n)
                rs = jnp.sum(jnp.where(grp == hh, dm_n, 0.0), axis=-1, keepdims=True)
                dbias = dbias + jnp.where(lane == hh, rs, 0.0)
        _acc(dw_ref, jnp.concatenate(dws, axis=0))
        _acc(db_ref, dbias)
        _acc(dpw_ref, _dot_tn(pooled_b, dmpb))
        d_pooled = _dot_nt(dmpb, pw_ref[...])
        d_pooled_halo = jnp.where(i < last, _dot_nt(dmp_halo, pw_ref[...]), 0.0)
        _acc(dlng_ref, _colsum(d_vn * vhat))
        _acc(dlnb_ref, _colsum(d_vn))
        dvh = d_vn * lng_ref[...]
        d_v = rstd * (dvh - jnp.mean(dvh, axis=-1, keepdims=True) - vhat * jnp.mean(dvh * vhat, axis=-1, keepdims=True))
        d_p = _pooled_bwd(d_pooled, d_pooled_halo, i * tm)
        dz_ref[...] = _mx(jnp.concatenate([d_u, d_v, d_gate, d_p, d_pgate], axis=1))

        @pl.when(i == last)
        def _():
            dw_ref[...] = jnp.where(_sgu_mask(False), dw_ref[...], 0.0)

    nxt = lambda i: jnp.minimum((i + 1) * (tm // 16), nt16 - 1)
    return pl.pallas_call(
        body, name=name, grid=(t // tm,),
        in_specs=[zrow, prev_halo, pl.BlockSpec((16, 256), lambda i: (nxt(i), 4)),
                  pl.BlockSpec((tm, 512), lambda i: (i, 0)), pl.BlockSpec((16, 256), lambda i: (nxt(i), 1)),
                  params[0], fixed(512, 128)] + params[1:],
        out_specs=[pl.BlockSpec((tm, Z_MIX), lambda i: (i, 0)), fixed(512, 128), fixed(128, 128), fixed(1, 256),
                   fixed(1, 256), fixed(256, 256), fixed(1, 256)],
        out_shape=[jax.ShapeDtypeStruct((t, Z_MIX), MXU_DTYPE), jax.ShapeDtypeStruct((512, 128), F32),
                   jax.ShapeDtypeStruct((128, 128), F32), jax.ShapeDtypeStruct((1, 256), F32),
                   jax.ShapeDtypeStruct((1, 256), F32), jax.ShapeDtypeStruct((256, 256), F32),
                   jax.ShapeDtypeStruct((1, 256), F32)],
        compiler_params=_cp())(z, z, z, dycat, dycat, sgu_w, sgu_wt, sgu_bias, ln_g, ln_b, pool_wbd, pool_scale)


def _rot_half(x, transpose):
    w = x.shape[1]
    lane = lax.broadcasted_iota(jnp.int32, x.shape, 1) % min(w, 256)
    base = 128 if w >= 256 else 0
    lo = jnp.logical_and(lane >= base, lane < base + 32)
    hi = jnp.logical_and(lane >= base + 32, lane < base + 64)
    up = pltpu.roll(x, w - 32, 1)
    down = pltpu.roll(x, 32, 1)
    if transpose:
        return jnp.where(lo, up, jnp.where(hi, -down, 0.0))
    return jnp.where(lo, -up, jnp.where(hi, down, 0.0))


def _rope(x, c, s):
    return x * c + _rot_half(x, False) * s


def _rope_bwd(dy, c, s):
    return dy * c + _rot_half(dy * s, True)


def _qkv_fwd(z, rc, rs, w_uq, w_ukv, gq, gkv, name, tm=512):
    t = z.shape[0]
    tm = min(tm, t)

    def body(zc_ref, zk_ref, rc_ref, rs_ref, wq_ref, wkv_ref, gq_ref, gkv_ref, q_ref, k_ref, v_ref):
        cq, ckv = zc_ref[:, 0:384], zc_ref[:, 384:640]
        c, s = rc_ref[...], rs_ref[...]
        qn, _ = _rms(cq, gq_ref[...])
        kvn, _ = _rms(ckv, gkv_ref[...])
        q_pre = _dot(_mx(qn), wq_ref[...])
        kv = _dot(_mx(kvn), wkv_ref[...])
        kpe = _rope(zk_ref[...], c[:, 128:256], s[:, 128:256])
        q = _rope(q_pre, jnp.tile(c, (1, N_HEADS)), jnp.tile(s, (1, N_HEADS)))
        for hh in range(N_HEADS):
            q_ref[hh] = _mx(q[:, hh * QK_PAD:(hh + 1) * QK_PAD])
            k_ref[hh] = _mx(jnp.concatenate([kv[:, hh * 128:(hh + 1) * 128], kpe], axis=1))
            v_ref[hh] = _mx(kv[:, 512 + hh * 128:512 + (hh + 1) * 128])

    fixed = lambda a, b: pl.BlockSpec((a, b), lambda i: (0, 0))
    heads = lambda wd: pl.BlockSpec((N_HEADS, tm, wd), lambda i: (0, i, 0))
    return pl.pallas_call(
        body, name=name, grid=(t // tm,),
        in_specs=[pl.BlockSpec((tm, Z_C), lambda i: (i, Z_MIX // Z_C)),
                  pl.BlockSpec((tm, Z_KR), lambda i: (i, (Z_MIX + Z_C) // Z_KR)),
                  pl.BlockSpec((tm, 256), lambda i: (i, 0)), pl.BlockSpec((tm, 256), lambda i: (i, 0)),
                  fixed(384, 1024), fixed(256, 1024), fixed(1, 384), fixed(1, 256)],
        out_specs=[heads(QK_PAD), heads(QK_PAD), heads(V_DIM)],
        out_shape=[jax.ShapeDtypeStruct((N_HEADS, t, QK_PAD), MXU_DTYPE),
                   jax.ShapeDtypeStruct((N_HEADS, t, QK_PAD), MXU_DTYPE),
                   jax.ShapeDtypeStruct((N_HEADS, t, V_DIM), MXU_DTYPE)],
        compiler_params=_cp())(z, z, rc, rs, w_uq, w_ukv, gq, gkv)


def _qkv_bwd(dq, dk, dv, z, rc, rs, w_uq, w_ukv, gq, gkv, name, tm=512):
    t = z.shape[0]
    tm = min(tm, t)

    def body(dq_ref, dk_ref, dv_ref, zc_ref, rc_ref, rs_ref, wq_ref, wkv_ref, gq_ref, gkv_ref,
             dzc_ref, dzk_ref, dwq_ref, dwkv_ref, dgq_ref, dgkv_ref):
        _zero_when(pl.program_id(0) == 0, dwq_ref, dwkv_ref, dgq_ref, dgkv_ref)
        cq, ckv = zc_ref[:, 0:384], zc_ref[:, 384:640]
        c, s = rc_ref[...], rs_ref[...]
        dkv = _mx(jnp.concatenate([dk_ref[hh][:, 0:128] for hh in range(N_HEADS)]
                                  + [dv_ref[hh] for hh in range(N_HEADS)], axis=1))
        kvn, rkv = _rms(ckv, gkv_ref[...])
        _acc(dwkv_ref, _dot_tn(_mx(kvn), dkv))
        d_kvn = _dot_nt(dkv, wkv_ref[...])
        dq_all = jnp.concatenate([dq_ref[hh] for hh in range(N_HEADS)], axis=1)
        dqp = _mx(_rope_bwd(dq_all, jnp.tile(c, (1, N_HEADS)), jnp.tile(s, (1, N_HEADS))))
        qn, rq = _rms(cq, gq_ref[...])
        dkpe = dk_ref[0][:, 128:256]
        for hh in range(1, N_HEADS):
            dkpe = dkpe + dk_ref[hh][:, 128:256]
        dzk_ref[...] = _mx(_rope_bwd(dkpe, c[:, 128:256], s[:, 128:256]))
        _acc(dwq_ref, _dot_tn(_mx(qn), dqp))
        d_qn = _dot_nt(dqp, wq_ref[...])
        d_ckv, dgkv_t = _rms_bwd(ckv, rkv, gkv_ref[...], d_kvn)
        _acc(dgkv_ref, _colsum(dgkv_t))
        d_cq, dgq_t = _rms_bwd(cq, rq, gq_ref[...], d_qn)
        _acc(dgq_ref, _colsum(dgq_t))
        dzc_ref[...] = _mx(jnp.concatenate([d_cq, d_ckv], axis=1))

    fixed = lambda a, b: pl.BlockSpec((a, b), lambda i: (0, 0))
    heads = lambda wd: pl.BlockSpec((N_HEADS, tm, wd), lambda i: (0, i, 0))
    return pl.pallas_call(
        body, name=name, grid=(t // tm,),
        in_specs=[heads(QK_PAD), heads(QK_PAD), heads(V_DIM), pl.BlockSpec((tm, Z_C), lambda i: (i, Z_MIX // Z_C)),
                  pl.BlockSpec((tm, 256), lambda i: (i, 0)), pl.BlockSpec((tm, 256), lambda i: (i, 0)),
                  fixed(384, 1024), fixed(256, 1024), fixed(1, 384), fixed(1, 256)],
        out_specs=[pl.BlockSpec((tm, Z_C), lambda i: (i, 0)), pl.BlockSpec((tm, Z_KR), lambda i: (i, 0)),
                   fixed(384, 1024), fixed(256, 1024), fixed(1, 384), fixed(1, 256)],
        out_shape=[jax.ShapeDtypeStruct((t, Z_C), MXU_DTYPE), jax.ShapeDtypeStruct((t, Z_KR), MXU_DTYPE),
                   jax.ShapeDtypeStruct((384, 1024), F32), jax.ShapeDtypeStruct((256, 1024), F32),
                   jax.ShapeDtypeStruct((1, 384), F32), jax.ShapeDtypeStruct((1, 256), F32)],
        compiler_params=_cp())(dq, dk, dv, z, rc, rs, w_uq, w_ukv, gq, gkv)


def _loop_in_long_trips(n, body, longest=4):
    def doubled(inner):
        return lambda t, carry: inner(2 * t + 1, inner(2 * t, carry))

    trips = [body]
    while 2 ** (len(trips) - 1) < longest:
        trips.append(doubled(trips[-1]))
    done = 0
    for level in reversed(range(len(trips))):
        size = 2 ** level
        end = n // size
        lax.fori_loop(done, end, trips[level], 0)
        done = 2 * end if level else end


def _init_mask_bias(bias_ref):
    _, tq, tk = bias_ref.shape
    r = lax.broadcasted_iota(jnp.int32, (tq, tk), 0) // CHUNK
    c = lax.broadcasted_iota(jnp.int32, (tq, tk), 1) // CHUNK
    bias_ref[0] = jnp.zeros((tq, tk), F32)
    for d in range(tq // tk):
        bias_ref[1 + d] = jnp.where(c + d * (tk // CHUNK) <= r, 0.0, NEG_INF)


def _gate_block(tq):
    return pl.BlockSpec((tq, 128), lambda h, i: (i, (Z_MIX + Z_C + Z_KR) // 128 + h))


def _attn_fwd(qh, kh, vh, z, name, tq=1024, tk=512):
    t = qh.shape[1]
    tq = min(tq, t)
    tk = min(tk, tq)
    ratio = tq // tk

    def body(q_ref, g_ref, k_hbm, v_hbm, o_ref, yc_ref, lse_ref, k_v, v_v, m_s, acc_s, s_a, s_b, mx_a, mx_b, bias_s,
             sem):
        h, i = pl.program_id(0), pl.program_id(1)

        @pl.when(i == 0)
        def _():
            ck = pltpu.make_async_copy(k_hbm.at[h], k_v, sem.at[0])
            cv = pltpu.make_async_copy(v_hbm.at[h], v_v.at[:, 0:V_DIM], sem.at[1])
            ck.start()
            cv.start()
            v_v[:, V_DIM:2 * V_DIM] = jnp.ones((t, V_DIM), MXU_DTYPE)
            _init_mask_bias(bias_s)
            ck.wait()
            cv.wait()

        q = q_ref[...]
        m_s[...] = jnp.full(m_s.shape, NEG_INF, F32)
        acc_s[...] = jnp.zeros(acc_s.shape, F32)

        last = ratio * (i + 1) - 1

        def keys(j):
            return pl.ds(pl.multiple_of(j * tk, tk), tk)

        def scores(s_ref, mx_ref, j, biased):
            s = _dot_nt(q, k_v[keys(j), :]) * (SCALE * LOG2E)
            if biased:
                s = s + bias_s[jnp.maximum(j - ratio * i + 1, 0)]
            s_ref[...] = s
            mx_ref[...] = jnp.broadcast_to(jnp.max(s, axis=-1, keepdims=True), mx_ref.shape)

        def softmax_pv(s_ref, mx_ref, j):
            m_old = m_s[...]
            m_new = jnp.maximum(m_old, mx_ref[...])
            p = jnp.exp2(s_ref[...] - jnp.tile(m_new, (1, tk // 128)))
            alpha = jnp.exp2(m_old - m_new)
            m_s[...] = m_new
            acc_s[...] = jnp.tile(alpha, (1, 2)) * acc_s[...] + _dot(_mx(p), v_v[keys(j), :])

        scores(s_a, mx_a, 0, True)

        def pair(pp, carry, biased):
            scores(s_b, mx_b, 2 * pp + 1, biased)
            softmax_pv(s_a, mx_a, 2 * pp)
            scores(s_a, mx_a, jnp.minimum(2 * pp + 2, last), biased)
            softmax_pv(s_b, mx_b, 2 * pp + 1)
            return carry

        n_pairs = (last + 1) // 2
        n_plain = jnp.maximum(ratio * i // 2 - 1, 0)
        _loop_in_long_trips(n_plain, lambda pp, carry: pair(pp, carry, False))
        _loop_in_long_trips(n_pairs - n_plain, lambda pp, carry: pair(n_plain + pp, carry, True))
        if ratio % 2 == 1:
            @pl.when(last % 2 == 0)
            def _():
                softmax_pv(s_a, mx_a, last)

        l = acc_s[:, V_DIM:2 * V_DIM]
        o = acc_s[:, 0:V_DIM] / l
        o_ref[...] = o
        yc_ref[...] = _mx(o * _silu_and_grad(g_ref[...])[0])
        lse_ref[...] = m_s[...] + jnp.log2(l)

    return pl.pallas_call(
        body, name=name, grid=(N_HEADS, t // tq),
        in_specs=[pl.BlockSpec((None, tq, QK_PAD), lambda h, i: (h, i, 0)), _gate_block(tq), ANY, ANY],
        out_specs=[pl.BlockSpec((tq, 128), lambda h, i: (i, h)), pl.BlockSpec((tq, 128), lambda h, i: (i, h)),
                   pl.BlockSpec((None, tq, 128), lambda h, i: (h, i, 0))],
        out_shape=[jax.ShapeDtypeStruct((t, N_HEADS * V_DIM), F32), jax.ShapeDtypeStruct((t, N_HEADS * V_DIM), MXU_DTYPE),
                   jax.ShapeDtypeStruct((N_HEADS, t, 128), F32)],
        scratch_shapes=[pltpu.VMEM((t, QK_PAD), MXU_DTYPE), pltpu.VMEM((t, 2 * V_DIM), MXU_DTYPE),
                        pltpu.VMEM((tq, 128), F32), pltpu.VMEM((tq, 2 * V_DIM), F32),
                        pltpu.VMEM((tq, tk), F32), pltpu.VMEM((tq, tk), F32), pltpu.VMEM((tq, 128), F32),
                        pltpu.VMEM((tq, 128), F32), pltpu.VMEM((ratio + 1, tq, tk), F32),
                        pltpu.SemaphoreType.DMA((2,))],
        compiler_params=_cp(2))(qh, z, kh, vh)


def _attn_bwd(qh, kh, vh, o, lse, dycat, z, name, tq=512):
    t = qh.shape[1]
    tq = min(tq, t)
    nq = t // tq

    def body(q_ref, o_ref, lse_ref, dy_ref, g_ref, k_hbm, v_hbm, dq_ref, dgate_ref, dk_hbm, dv_hbm,
             k_v, v_v, dk_acc, dv_acc, dq_acc, delta_s, s_a, dp_a, s_b, dp_b, bias_s, sem):
        h, i = pl.program_id(0), pl.program_id(1)

        @pl.when(i == 0)
        def _():
            ck = pltpu.make_async_copy(k_hbm.at[h], k_v, sem.at[0])
            cv = pltpu.make_async_copy(v_hbm.at[h], v_v, sem.at[1])
            ck.start()
            cv.start()
            _init_mask_bias(bias_s)
            dk_acc[...] = jnp.zeros(dk_acc.shape, F32)
            dv_acc[...] = jnp.zeros(dv_acc.shape, F32)
            ck.wait()
            cv.wait()

        gate, dy, of = g_ref[...], dy_ref[...], o_ref[...]
        silu, dsilu = _silu_and_grad(gate)
        do = dy * silu
        delta = jnp.sum(do * of, axis=-1, keepdims=True)
        dgate_ref[...] = _mx(dy * of * dsilu)
        dob = _mx(do)
        q = q_ref[...]
        delta_s[...] = jnp.broadcast_to(delta, delta_s.shape)
        dq_acc[...] = jnp.zeros(dq_acc.shape, F32)

        def keys(j):
            return pl.ds(pl.multiple_of(j * tq, tq), tq)

        def scores(s_ref, dp_ref, j):
            s = _dot_nt(q, k_v[keys(j), :]) * (SCALE * LOG2E) + bias_s[(j == i).astype(jnp.int32)]
            s_ref[...] = s - jnp.tile(lse_ref[...], (1, tq // 128))
            dp_ref[...] = _dot_nt(dob, v_v[keys(j), :]) - jnp.tile(delta_s[...], (1, tq // 128))

        def grads(s_ref, dp_ref, j):
            ks = keys(j)
            p = jnp.exp2(s_ref[...])
            ds = p * dp_ref[...] * SCALE
            pb, dsb = _mx(p), _mx(ds)
            dq_acc[...] += _dot(dsb, k_v[ks, :])
            dk_acc[ks, :] += _dot_tn(dsb, q)
            dv_acc[ks, :] += _dot_tn(pb, dob)

        scores(s_a, dp_a, 0)

        def pair(pp, carry):
            scores(s_b, dp_b, 2 * pp + 1)
            grads(s_a, dp_a, 2 * pp)
            scores(s_a, dp_a, jnp.minimum(2 * pp + 2, i))
            grads(s_b, dp_b, 2 * pp + 1)
            return carry

        _loop_in_long_trips((i + 1) // 2, pair, longest=8)

        @pl.when(i % 2 == 0)
        def _():
            grads(s_a, dp_a, i)

        dq_ref[...] = dq_acc[...]

        @pl.when(i == nq - 1)
        def _():
            ck = pltpu.make_async_copy(dk_acc, dk_hbm.at[h], sem.at[0])
            cv = pltpu.make_async_copy(dv_acc, dv_hbm.at[h], sem.at[1])
            ck.start()
            cv.start()
            ck.wait()
            cv.wait()

    return pl.pallas_call(
        body, name=name, grid=(N_HEADS, nq),
        in_specs=[pl.BlockSpec((None, tq, QK_PAD), lambda h, i: (h, i, 0)),
                  pl.BlockSpec((tq, 128), lambda h, i: (i, h)),
                  pl.BlockSpec((None, tq, 128), lambda h, i: (h, i, 0)),
                  pl.BlockSpec((tq, 128), lambda h, i: (i, N_HEADS + h)), _gate_block(tq), ANY, ANY],
        out_specs=[pl.BlockSpec((None, tq, QK_PAD), lambda h, i: (h, i, 0)),
                   pl.BlockSpec((tq, 128), lambda h, i: (i, h)), ANY, ANY],
        out_shape=[jax.ShapeDtypeStruct((N_HEADS, t, QK_PAD), F32), jax.ShapeDtypeStruct((t, Z_GATE), MXU_DTYPE),
                   jax.ShapeDtypeStruct((N_HEADS, t, QK_PAD), F32), jax.ShapeDtypeStruct((N_HEADS, t, V_DIM), F32)],
        scratch_shapes=[pltpu.VMEM((t, QK_PAD), MXU_DTYPE), pltpu.VMEM((t, V_DIM), MXU_DTYPE),
                        pltpu.VMEM((t, QK_PAD), F32), pltpu.VMEM((t, V_DIM), F32), pltpu.VMEM((tq, QK_PAD), F32),
                        pltpu.VMEM((tq, 128), F32)] + [pltpu.VMEM((tq, tq), F32)] * 4
        + [pltpu.VMEM((2, tq, tq), F32), pltpu.SemaphoreType.DMA((2,))],
        compiler_params=_cp(2))(qh, o, lse, dycat, z, kh, vh)


def _out_proj_fwd(yab, yc, w, x, g, target, name, tm=512):
    t, d = x.shape
    tm = min(tm, t)
    is_last = target is not None

    def body(*refs):
        if is_last:
            yab_ref, yc_ref, w_ref, x_ref, g_ref, t_ref, y_ref, dout_ref, loss_ref = refs
            _zero_when(pl.program_id(0) == 0, loss_ref)
        else:
            yab_ref, yc_ref, w_ref, x_ref, g_ref, y_ref, out_ref = refs
        y = _dot(jnp.concatenate([yab_ref[...], yc_ref[...]], axis=1), w_ref[...])
        y_ref[...] = y
        out = x_ref[...] + _rms(y, g_ref[...])[0]
        if is_last:
            diff = out - t_ref[...]
            dout_ref[...] = diff * (1.0 / d)
            part = jnp.sum(jnp.sum(diff * diff, axis=-1, keepdims=True), axis=0, keepdims=True) * (0.5 / d)
            _acc(loss_ref, jnp.broadcast_to(part, (1, 128)))
        else:
            out_ref[...] = out

    row = lambda wd: pl.BlockSpec((tm, wd), lambda i: (i, 0))
    fixed = lambda a, b: pl.BlockSpec((a, b), lambda i: (0, 0))
    in_specs = [row(512), row(512), fixed(d, d), row(d), fixed(1, d)]
    args = [yab, yc, w, x, g]
    out_specs = [row(d), row(d)]
    out_shape = [jax.ShapeDtypeStruct((t, d), F32), jax.ShapeDtypeStruct((t, d), F32)]
    if is_last:
        in_specs.append(row(d))
        args.append(target)
        out_specs.append(fixed(1, 128))
        out_shape.append(jax.ShapeDtypeStruct((1, 128), F32))
    return pl.pallas_call(body, name=name, grid=(t // tm,), in_specs=in_specs, out_specs=out_specs,
                          out_shape=out_shape, compiler_params=_cp())(*args)


def _halves(tm):
    half = tm // 2 if tm >= 512 else tm
    return [pl.ds(s, half) for s in range(0, tm, half)]


def _out_proj_bwd(dout, y, yab, yc, w, g, name, tm=1024):
    t, d = y.shape
    tm = min(tm, t)

    def body(dout_ref, y_ref, yab_ref, yc_ref, w_ref, g_ref, dycat_ref, dw_ref, dg_ref):
        _zero_when(pl.program_id(0) == 0, dw_ref, dg_ref)
        dybs = []
        for rows in _halves(tm):
            y = y_ref[rows, :]
            r = lax.rsqrt(jnp.mean(y * y, axis=-1, keepdims=True) + EPS)
            dy, dgt = _rms_bwd(y, r, g_ref[...], dout_ref[rows, :])
            _acc(dg_ref, _colsum(dgt))
            dybs.append(_mx(dy))
        for rows, dyb in zip(_halves(tm), dybs):
            _acc(dw_ref, _dot_tn(jnp.concatenate([yab_ref[rows, :], yc_ref[rows, :]], axis=1), dyb))
            dycat_ref[rows, :] = _dot_nt(dyb, w_ref[...])

    row = lambda wd: pl.BlockSpec((tm, wd), lambda i: (i, 0))
    fixed = lambda a, b: pl.BlockSpec((a, b), lambda i: (0, 0), pipeline_mode=pl.Buffered(1))
    return pl.pallas_call(
        body, name=name, grid=(t // tm,),
        in_specs=[row(d), row(d), row(512), row(512), fixed(d, d), fixed(1, d)],
        out_specs=[row(d), fixed(d, d), fixed(1, d)],
        out_shape=[jax.ShapeDtypeStruct((t, d), F32), jax.ShapeDtypeStruct((d, d), F32),
                   jax.ShapeDtypeStruct((1, d), F32)],
        compiler_params=_cp())(dout, y, yab, yc, w, g)


def _mesh_pos():
    return lax.axis_index("x"), lax.axis_index("y"), lax.axis_index("c")


def _remote(src, dst, send_sem, recv_sem, to):
    return pltpu.make_async_remote_copy(src_ref=src, dst_ref=dst, send_sem=send_sem, recv_sem=recv_sem,
                                        device_id=to, device_id_type=MESH)


CHUNK_ROWS = 256


def _pieces(rows):
    return [(s, min(CHUNK_ROWS, rows - s)) for s in range(0, rows, CHUNK_ROWS)]


def _piece_table(shapes):
    return [(a, s, sz) for a, shp in enumerate(shapes) for s, sz in _pieces(shp[-2])]


def _gather_weights(shards):
    n = len(shards)
    table = _piece_table([s.shape for s in shards])
    npc = len(table)

    def body(*refs):
        ins, outs = refs[:n], refs[n:2 * n]
        send_sems, recv_sems, fwd_send, fwd_recv = refs[2 * n:]
        x, y, c = _mesh_pos()
        me, sibling = (x, y, c), (x, y, 1 - c)
        chips = [(1 - x, y), (x, 1 - y), (1 - x, 1 - y)]
        slot = lambda cx, cy, layer: 2 * (2 * cx + cy) + layer
        first = []
        for a in range(n):
            for j, (cx, cy) in enumerate(chips):
                first.append(_remote(ins[a].at[c], outs[a].at[slot(x, y, c)], send_sems.at[a, j], recv_sems.at[a, j],
                                     (cx, cy, c)))
                first[-1].start()
        passed = []
        for j, (cx, cy) in enumerate(chips):
            for a in range(n):
                blk = outs[a].at[slot(cx, cy, c)]
                _remote(blk, blk, send_sems.at[a, j], recv_sems.at[a, j], me).wait_recv()
            for q, (a, s, sz) in enumerate(table):
                rows = outs[a].at[slot(cx, cy, c), pl.ds(s, sz)]
                passed.append(_remote(rows, rows, fwd_send.at[j, q], fwd_recv.at[j, q], sibling))
                passed[-1].start()
        for j, (cx, cy) in enumerate(chips):
            for q, (a, s, sz) in enumerate(table):
                rows = outs[a].at[slot(cx, cy, 1 - c), pl.ds(s, sz)]
                _remote(rows, rows, fwd_send.at[j, q], fwd_recv.at[j, q], me).wait_recv()
        for cp in first + passed:
            cp.wait_send()

    return pl.pallas_call(
        body, name="gather_weights", in_specs=[ANY] * n, out_specs=[ANY] * n,
        out_shape=[jax.ShapeDtypeStruct((8,) + s.shape[1:], s.dtype) for s in shards],
        scratch_shapes=[pltpu.SemaphoreType.DMA((n, 3)), pltpu.SemaphoreType.DMA((n, 3)),
                        pltpu.SemaphoreType.DMA((3, npc)), pltpu.SemaphoreType.DMA((3, npc))])(*shards)


def _pair_exchange(parts, common):
    n = len(parts)
    table = _piece_table([p.shape for p in parts] + [common.shape])
    npc = len(table)

    def body(*refs):
        ins, outs = refs[:n + 1], refs[n + 1:2 * n + 2]
        send_sems, recv_sems = refs[2 * n + 2:]
        x, y, c = _mesh_pos()
        sent = []
        for k in range(4):
            for q, (a, s, sz) in enumerate(table):
                if a == n and k > 0:
                    continue
                src = ins[a].at[2 * k + 1 - c, pl.ds(s, sz)] if a < n else ins[a].at[pl.ds(s, sz)]
                dst = outs[a].at[k, pl.ds(s, sz)] if a < n else outs[a].at[pl.ds(s, sz)]
                sent.append(_remote(src, dst, send_sems.at[k, q], recv_sems.at[k, q], (x, y, 1 - c)))
                sent[-1].start()
        for k in range(4):
            for q, (a, s, sz) in enumerate(table):
                if a == n and k > 0:
                    continue
                dst = outs[a].at[k, pl.ds(s, sz)] if a < n else outs[a].at[pl.ds(s, sz)]
                _remote(dst, dst, send_sems.at[k, q], recv_sems.at[k, q], (x, y, c)).wait_recv()
        for cp in sent:
            cp.wait_send()

    return pl.pallas_call(
        body, name="grad_pair_exchange", in_specs=[ANY] * (n + 1), out_specs=[ANY] * (n + 1),
        out_shape=[jax.ShapeDtypeStruct((4,) + p.shape[1:], p.dtype) for p in parts]
        + [jax.ShapeDtypeStruct(common.shape, common.dtype)],
        scratch_shapes=[pltpu.SemaphoreType.DMA((4, npc)), pltpu.SemaphoreType.DMA((4, npc))])(*parts, common)


def _chip_exchange(parts, common):
    n = len(parts)
    table = _piece_table([p.shape for p in parts] + [common.shape])
    npc = len(table)

    def body(*refs):
        ins, outs = refs[:n + 1], refs[n + 1:2 * n + 2]
        send_sems, recv_sems = refs[2 * n + 2:]
        x, y, c = _mesh_pos()
        mine = 2 * x + y
        chips = [(1 - x, y), (x, 1 - y), (1 - x, 1 - y)]
        src = lambda a, k: ins[a].at[k] if a < n else ins[a]
        sent = []
        for j, (cx, cy) in enumerate(chips):
            for q, (a, s, sz) in enumerate(table):
                sent.append(_remote(src(a, 2 * cx + cy).at[pl.ds(s, sz)], outs[a].at[mine, pl.ds(s, sz)],
                                    send_sems.at[j, q], recv_sems.at[j, q], (cx, cy, c)))
                sent[-1].start()
        for j, (cx, cy) in enumerate(chips):
            for q, (a, s, sz) in enumerate(table):
                dst = outs[a].at[2 * cx + cy, pl.ds(s, sz)]
                _remote(dst, dst, send_sems.at[j, q], recv_sems.at[j, q], (x, y, c)).wait_recv()
        for cp in sent:
            cp.wait_send()

    return pl.pallas_call(
        body, name="grad_chip_exchange", in_specs=[ANY] * (n + 1), out_specs=[ANY] * (n + 1),
        out_shape=[jax.ShapeDtypeStruct(p.shape, p.dtype) for p in parts]
        + [jax.ShapeDtypeStruct((4,) + common.shape, common.dtype)],
        scratch_shapes=[pltpu.SemaphoreType.DMA((3, npc)), pltpu.SemaphoreType.DMA((3, npc))])(*parts, common)


def _sibling_exchange(sums):
    n = len(sums)
    table = _piece_table([s.shape for s in sums])
    npc = len(table)

    def body(*refs):
        ins, outs = refs[:n], refs[n:2 * n]
        send_sems, recv_sems = refs[2 * n:]
        x, y, c = _mesh_pos()
        sent = []
        for q, (a, s, sz) in enumerate(table):
            sent.append(_remote(ins[a].at[pl.ds(s, sz)], outs[a].at[pl.ds(s, sz)], send_sems.at[q], recv_sems.at[q],
                                (x, y, 1 - c)))
            sent[-1].start()
        for q, (a, s, sz) in enumerate(table):
            dst = outs[a].at[pl.ds(s, sz)]
            _remote(dst, dst, send_sems.at[q], recv_sems.at[q], (x, y, c)).wait_recv()
        for cp in sent:
            cp.wait_send()

    return pl.pallas_call(
        body, name="sibling_exchange", in_specs=[ANY] * n, out_specs=[ANY] * n,
        out_shape=[jax.ShapeDtypeStruct(s.shape, s.dtype) for s in sums],
        scratch_shapes=[pltpu.SemaphoreType.DMA((npc,)), pltpu.SemaphoreType.DMA((npc,))])(*sums)


def _pair_sum(parts, got, name, tr):
    _, r, c = got.shape
    tr = min(tr, r)

    def body(p_ref, g_ref, o_ref, w_ref):
        total = p_ref[...] + g_ref[...]
        o_ref[...] = total
        w_ref[...] = total.astype(WIRE_DTYPE)

    blk = pl.BlockSpec((None, tr, c), lambda k, i: (k, i, 0))
    mine = pl.BlockSpec((None, tr, c), lambda k, i: (2 * k + lax.axis_index("c"), i, 0))
    return pl.pallas_call(
        body, name=name, grid=(4, r // tr), in_specs=[mine, blk], out_specs=[blk, blk],
        out_shape=[jax.ShapeDtypeStruct(got.shape, F32), jax.ShapeDtypeStruct(got.shape, WIRE_DTYPE)],
        compiler_params=_cp(2))(parts, got)


def _sum_chips(pair_sums, recv, name, tr):
    _, r, c = pair_sums.shape
    tr = min(tr, r)

    def body(own_ref, r_ref, o_ref):
        chip = 2 * lax.axis_index("x") + lax.axis_index("y")
        own_blk = own_ref[...]
        acc = jnp.where(chip == 0, own_blk, r_ref[0].astype(F32))
        for k in range(1, 4):
            acc = acc + jnp.where(chip == k, own_blk, r_ref[k].astype(F32))
        o_ref[...] = acc

    return pl.pallas_call(
        body, name=name, grid=(r // tr,),
        in_specs=[pl.BlockSpec((None, tr, c), lambda i: (2 * lax.axis_index("x") + lax.axis_index("y"), i, 0)),
                  pl.BlockSpec((4, tr, c), lambda i: (0, i, 0))],
        out_specs=pl.BlockSpec((tr, c), lambda i: (i, 0)),
        out_shape=jax.ShapeDtypeStruct((r, c), F32), compiler_params=_cp())(pair_sums, recv)


def _sum_leading(parts, name, tr):
    nlead, r, c = parts.shape
    tr = min(tr, r)

    def body(p_ref, o_ref):
        acc = p_ref[0]
        for j in range(1, nlead):
            acc = acc + p_ref[j]
        o_ref[...] = acc

    return pl.pallas_call(
        body, name=name, grid=(r // tr,),
        in_specs=[pl.BlockSpec((nlead, tr, c), lambda i: (0, i, 0))], out_specs=pl.BlockSpec((tr, c), lambda i: (i, 0)),
        out_shape=jax.ShapeDtypeStruct((r, c), parts.dtype), compiler_params=_cp())(parts)


def _adamw_update(w_ref, g_ref, m_ref, v_ref, d_ref, nm_ref, nv_ref):
    gg = g_ref[...]
    nm = ADAM_B1 * m_ref[...] + (1.0 - ADAM_B1) * gg
    nv = ADAM_B2 * v_ref[...] + (1.0 - ADAM_B2) * jnp.square(gg)
    m_hat = nm / (1.0 - ADAM_B1 ** ADAM_STEP)
    v_hat = nv / (1.0 - ADAM_B2 ** ADAM_STEP)
    d_ref[...] = -ADAM_LR * (m_hat / (jnp.sqrt(v_hat) + ADAM_EPS) + ADAM_WD * w_ref[...])
    nm_ref[...] = nm
    nv_ref[...] = nv


def _adamw(w, g, m, v, name, tr):
    r, c = w.shape
    tr = min(tr, r)

    def body(w_ref, g_ref, m_ref, v_ref, d_ref, nm_ref, nv_ref):
        _adamw_update(w_ref, g_ref, m_ref, v_ref, d_ref, nm_ref, nv_ref)

    blk = pl.BlockSpec((tr, c), lambda i: (i, 0))
    return pl.pallas_call(
        body, name=name, grid=(r // tr,), in_specs=[blk] * 4, out_specs=[blk] * 3,
        out_shape=[jax.ShapeDtypeStruct((r, c), F32)] * 3, compiler_params=_cp())(w, g, m, v)


def _adamw_many(ws, gs, ms, vs, name):
    n = len(ws)

    def body(*refs):
        ins, outs = refs[:4 * n], refs[4 * n:]
        for k in range(n):
            _adamw_update(ins[k], ins[n + k], ins[2 * n + k], ins[3 * n + k], outs[k], outs[n + k], outs[2 * n + k])

    vmem = pl.BlockSpec(memory_space=pltpu.VMEM)
    res = pl.pallas_call(
        body, name=name, in_specs=[vmem] * (4 * n), out_specs=[vmem] * (3 * n),
        out_shape=[jax.ShapeDtypeStruct(a.shape, F32) for a in ws] * 3)(*ws, *gs, *ms, *vs)
    return res[:n], res[n:2 * n], res[2 * n:]


def _rope_tables(positions):
    inv_freq = ROPE_BASE ** (-jnp.arange(0, 64, 2, dtype=F32) / 64)
    ang = positions.astype(F32)[:, None] * inv_freq
    cos, sin = jnp.cos(ang), jnp.sin(ang)
    t = positions.shape[0]
    rc = jnp.concatenate([jnp.ones((t, 128), F32), cos, cos, jnp.ones((t, 64), F32)], axis=1)
    rs = jnp.concatenate([jnp.zeros((t, 128), F32), sin, sin, jnp.zeros((t, 64), F32)], axis=1)
    return rc, rs


def _layer_params(l, w_in, w_uq, w_ukv, w_out, small):
    p = {}
    p["w_in"] = jnp.concatenate([w_in[l][:, :1984], jnp.zeros((1024, 64), w_in.dtype), w_in[l][:, 1984:]], axis=1)
    p["w_uq"] = jnp.pad(w_uq[l].reshape(384, 4, 192), ((0, 0), (0, 0), (0, 64))).reshape(384, 1024)
    p["w_ukv"] = w_ukv[l].reshape(256, 4, 2, 128).transpose(0, 2, 1, 3).reshape(256, 1024)
    p["w_out"] = w_out[l]
    p["pre_g"] = small["pre_norm_g"][l][None]
    p["post_g"] = small["post_norm_g"][l][None]
    p["sgu_w"] = small["sgu_w"][l].reshape(512, 128)
    p["sgu_wt"] = small["sgu_w"][l].transpose(0, 2, 1).reshape(512, 128)
    p["sgu_bias"] = jnp.repeat(small["sgu_b"][l].T, 64, axis=1)
    p["ln_g"] = small["sgu_ln_g"][l][None]
    p["ln_b"] = small["sgu_ln_b"][l][None]
    p["pool_wbd"] = _mx(jax.scipy.linalg.block_diag(*[small["pool_w"][l][gi] for gi in range(4)]))
    p["pool_scale"] = small["pool_scale"][l][None]
    p["gq"] = small["q_norm_g"][l][None]
    p["gkv"] = small["kv_norm_g"][l][None]
    return p


def _layer_fwd(l, x, p, rc, rs, target):
    z, h = _in_proj_fwd(x, p["pre_g"], p["w_in"], f"in_proj_fwd_{l}")
    yab = _mix_fwd(z, p["sgu_w"], p["sgu_bias"], p["ln_g"], p["ln_b"], p["pool_wbd"], p["pool_scale"], f"mix_fwd_{l}")
    qh, kh, vh = _qkv_fwd(z, rc, rs, p["w_uq"], p["w_ukv"], p["gq"], p["gkv"], f"qkv_fwd_{l}")
    o, yc, lse = _attn_fwd(qh, kh, vh, z, f"attn_fwd_{l}")
    outs = _out_proj_fwd(yab, yc, p["w_out"], x, p["post_g"], target, f"out_proj_fwd_{l}")
    saved = dict(x=x, z=z, h=h, yab=yab, qh=qh, kh=kh, vh=vh, o=o, yc=yc, lse=lse, y=outs[0])
    return saved, outs[1:]


def _layer_bwd(l, dout, sv, p, rc, rs):
    dycat, dw_out, dpost = _out_proj_bwd(dout, sv["y"], sv["yab"], sv["yc"], p["w_out"], p["post_g"], f"out_proj_bwd_{l}")
    dq, dgate, dk, dv = _attn_bwd(sv["qh"], sv["kh"], sv["vh"], sv["o"], sv["lse"], dycat, sv["z"], f"attn_bwd_{l}")
    dzc, dzk, dwq, dwkv, dgq, dgkv = _qkv_bwd(dq, dk, dv, sv["z"], rc, rs, p["w_uq"], p["w_ukv"], p["gq"], p["gkv"],
                                              f"qkv_bwd_{l}")
    dzm, dsw, dsb, dlng, dlnb, dpw, dps = _mix_bwd(sv["z"], dycat, p["sgu_w"], p["sgu_wt"], p["sgu_bias"], p["ln_g"],
                                                   p["ln_b"], p["pool_wbd"], p["pool_scale"], f"mix_bwd_{l}")
    dx, dw_in, dpre = _in_proj_bwd(dzm, dzc, dzk, dgate, sv["h"], sv["x"], dout, p["w_in"], p["pre_g"], f"in_proj_bwd_{l}")
    grads = {
        "pre_norm_g": dpre[0], "post_norm_g": dpost[0],
        "w_in": jnp.concatenate([dw_in[:, :1984], dw_in[:, 2048:]], axis=1),
        "sgu_w": dsw.reshape(4, 128, 128), "sgu_b": dsb[:, :4].T, "sgu_ln_g": dlng[0], "sgu_ln_b": dlnb[0],
        "pool_w": jnp.stack([dpw[64 * gi:64 * gi + 64, 64 * gi:64 * gi + 64] for gi in range(4)]),
        "pool_scale": dps[0], "q_norm_g": dgq[0],
        "w_uq": dwq.reshape(384, 4, 256)[:, :, :192].reshape(384, 768), "kv_norm_g": dgkv[0],
        "w_ukv": dwkv.reshape(256, 2, 4, 128).transpose(0, 2, 1, 3).reshape(256, 1024), "w_out": dw_out,
    }
    return dx, grads


SMALL_NAMES = ["pre_norm_g", "post_norm_g", "sgu_w", "sgu_b", "sgu_ln_g", "sgu_ln_b", "pool_w", "pool_scale",
               "q_norm_g", "kv_norm_g"]
BIG_NAMES = ["w_in", "w_uq", "w_ukv", "w_out"]
WEIGHT_NAMES = ["pre_norm_g", "post_norm_g", "w_in", "sgu_w", "sgu_b", "sgu_ln_g", "sgu_ln_b", "pool_w", "pool_scale",
                "q_norm_g", "w_uq", "kv_norm_g", "w_ukv", "w_out"]


def _local_step(x, positions, target, w_in, w_uq, w_ukv, w_out, small):
    rc, rs = _rope_tables(positions)
    params = [_layer_params(l, w_in, w_uq, w_ukv, w_out, small) for l in range(DEPTH)]
    saved = []
    for l in range(DEPTH):
        sv, outs = _layer_fwd(l, x, params[l], rc, rs, target if l == DEPTH - 1 else None)
        saved.append(sv)
        if l < DEPTH - 1:
            x = outs[0]
    dout, loss = outs
    grads = [None] * DEPTH
    for l in reversed(range(DEPTH)):
        dout, grads[l] = _layer_bwd(l, dout, saved[l], params[l], rc, rs)
    return loss[0, 0], dout, {k: jnp.stack([grads[l][k] for l in range(DEPTH)]) for k in WEIGHT_NAMES}


def _pack_small(tree, extra=None):
    pieces = [tree[k].reshape(-1) for k in SMALL_NAMES]
    pieces.append(jnp.zeros((1,), F32) if extra is None else extra.reshape(1))
    flat = jnp.concatenate(pieces)
    rows = -(-flat.shape[0] // 1024) * 8
    return jnp.pad(flat, (0, rows * 128 - flat.shape[0])).reshape(rows, 128)


def _unpack_small(packed, like):
    flat = packed.reshape(-1)
    out, off = {}, 0
    for k in SMALL_NAMES:
        size = like[k].size
        out[k] = flat[off:off + size].reshape(like[k].shape)
        off += size
    return out, flat[off]


def kernel(x, positions, pre_norm_g, post_norm_g, w_in, sgu_w, sgu_b, sgu_ln_g, sgu_ln_b, pool_w, pool_scale, q_norm_g, w_uq, kv_norm_g, w_ukv, w_out, loss_target, m_pre_norm_g, m_post_norm_g, m_w_in, m_sgu_w, m_sgu_b, m_sgu_ln_g, m_sgu_ln_b, m_pool_w, m_pool_scale, m_q_norm_g, m_w_uq, m_kv_norm_g, m_w_ukv, m_w_out, v_pre_norm_g, v_post_norm_g, v_w_in, v_sgu_w, v_sgu_b, v_sgu_ln_g, v_sgu_ln_b, v_pool_w, v_pool_scale, v_q_norm_g, v_w_uq, v_kv_norm_g, v_w_ukv, v_w_out):
    w = dict(pre_norm_g=pre_norm_g, post_norm_g=post_norm_g, w_in=w_in, sgu_w=sgu_w, sgu_b=sgu_b, sgu_ln_g=sgu_ln_g,
             sgu_ln_b=sgu_ln_b, pool_w=pool_w, pool_scale=pool_scale, q_norm_g=q_norm_g, w_uq=w_uq, kv_norm_g=kv_norm_g,
             w_ukv=w_ukv, w_out=w_out)
    m = dict(pre_norm_g=m_pre_norm_g, post_norm_g=m_post_norm_g, w_in=m_w_in, sgu_w=m_sgu_w, sgu_b=m_sgu_b,
             sgu_ln_g=m_sgu_ln_g, sgu_ln_b=m_sgu_ln_b, pool_w=m_pool_w, pool_scale=m_pool_scale, q_norm_g=m_q_norm_g,
             w_uq=m_w_uq, kv_norm_g=m_kv_norm_g, w_ukv=m_w_ukv, w_out=m_w_out)
    v = dict(pre_norm_g=v_pre_norm_g, post_norm_g=v_post_norm_g, w_in=v_w_in, sgu_w=v_sgu_w, sgu_b=v_sgu_b,
             sgu_ln_g=v_sgu_ln_g, sgu_ln_b=v_sgu_ln_b, pool_w=v_pool_w, pool_scale=v_pool_scale, q_norm_g=v_q_norm_g,
             w_uq=v_w_uq, kv_norm_g=v_kv_norm_g, w_ukv=v_w_ukv, w_out=v_w_out)

    core = lax.axis_index("c")
    chip = 2 * lax.axis_index("x") + lax.axis_index("y")
    shards = [_mx(w[k]) for k in BIG_NAMES]
    gathered = _gather_weights(shards)
    g_in, g_uq, g_ukv, g_out = [lax.dynamic_update_slice(g, s, (2 * chip, 0, 0)) for g, s in zip(gathered, shards)]
    cols = lambda g: g.reshape((4, 2) + g.shape[1:]).transpose(1, 2, 0, 3).reshape(2, g.shape[1], 4 * g.shape[2])
    full_out = g_out.reshape(4, 2, 256, 1024).transpose(1, 0, 2, 3).reshape(2, 1024, 1024)
    loss, dx, grads = _local_step(x[0], positions[0], loss_target[0], cols(g_in), cols(g_uq), cols(g_ukv), full_out, w)

    split_cols = lambda g: g.reshape(2, g.shape[1], 4, g.shape[2] // 4).transpose(2, 0, 1, 3).reshape(8, g.shape[1], g.shape[2] // 4)
    parts = [split_cols(grads["w_in"]), split_cols(grads["w_uq"]), split_cols(grads["w_ukv"]),
             grads["w_out"].reshape(2, 4, 256, 1024).transpose(1, 0, 2, 3).reshape(8, 256, 1024)]
    common = _pack_small(grads, loss)
    got = _pair_exchange(parts, common)
    pair_sums = [_pair_sum(parts[a], got[a], f"pair_sum_{BIG_NAMES[a]}", 128) for a in range(4)]
    chip_common = _sum_leading(jnp.stack([common, got[4]]), "pair_sum_small", common.shape[0])
    received = _chip_exchange([ps[1] for ps in pair_sums], chip_common)
    sums = [_sum_chips(pair_sums[a][0], received[a], f"sum_{BIG_NAMES[a]}", 128) for a in range(4)]
    all_common = lax.dynamic_update_slice(received[4], chip_common[None], (chip, 0, 0))
    small_sum, loss = _unpack_small(_sum_leading(all_common, "sum_small", all_common.shape[1]), w)
    others = _sibling_exchange(sums)
    total = dict(small_sum)
    for a, k in enumerate(BIG_NAMES):
        total[k] = jnp.where(core == 0, jnp.stack([sums[a], others[a]]), jnp.stack([others[a], sums[a]]))

    rows2d = lambda a: a.reshape(-1, a.shape[-1])
    small_out = _adamw_many(*[[rows2d(tree[k]) for k in SMALL_NAMES] for tree in (w, total, m, v)], "adamw_small")
    delta, new_m, new_v = ({k: r.reshape(w[k].shape) for k, r in zip(SMALL_NAMES, res)} for res in small_out)
    for k in BIG_NAMES:
        shape = w[k].shape
        flat = lambda a: a.reshape(shape[0] * shape[1], shape[2])
        res = _adamw(flat(w[k]), flat(total[k]), flat(m[k]), flat(v[k]), f"adamw_{k}", 256)
        delta[k], new_m[k], new_v[k] = (r.reshape(shape) for r in res)

    return (loss, dx[None], *[total[k] for k in WEIGHT_NAMES], *[delta[k] for k in WEIGHT_NAMES],
            *[new_m[k] for k in WEIGHT_NAMES], *[new_v[k] for k in WEIGHT_NAMES])
```

```python
import jax
import jax.numpy as jnp
from jax import lax
from jax.experimental import pallas as pl
from jax.experimental.pallas import tpu as pltpu

F32 = jnp.float32
MXU_DTYPE = jnp.bfloat16
WIRE_DTYPE = jnp.bfloat16
EPS = 1e-6
NEG_INF = -1e30
CHUNK = 64
DEPTH = 2
N_HEADS = 4
QK_PAD = 256
V_DIM = 128
SCALE = 192 ** -0.5
LOG2E = 1.4426950408889634
ROPE_BASE = 10000.0
ADAM_LR, ADAM_B1, ADAM_B2, ADAM_EPS, ADAM_WD, ADAM_STEP = 0.001, 0.9, 0.999, 1e-08, 0.01, 10
VMEM_LIMIT_BYTES = 56 * 1024 * 1024
MESH = pl.DeviceIdType.MESH
ANY = pl.BlockSpec(memory_space=pl.ANY)

Z_MIX, Z_C, Z_KR, Z_GATE = 1280, 640, 128, 512
Z_W = Z_MIX + Z_C + Z_KR + Z_GATE


def _cp(n_axes=1):
    return pltpu.CompilerParams(dimension_semantics=("arbitrary",) * n_axes, vmem_limit_bytes=VMEM_LIMIT_BYTES)


def _dot(a, b):
    return lax.dot_general(a, b, (((1,), (0,)), ((), ())), preferred_element_type=F32)


def _dot_nt(a, b):
    return lax.dot_general(a, b, (((1,), (1,)), ((), ())), preferred_element_type=F32)


def _dot_tn(a, b):
    return lax.dot_general(a, b, (((0,), (0,)), ((), ())), preferred_element_type=F32)


def _mx(a):
    return a.astype(MXU_DTYPE)


def _silu_and_grad(g):
    sg = jax.nn.sigmoid(g)
    return g * sg, sg * (1.0 + g * (1.0 - sg))


def _rms(x, g):
    r = lax.rsqrt(jnp.mean(x * x, axis=-1, keepdims=True) + EPS)
    return x * r * g, r


def _rms_bwd(x, r, g, dy):
    xhat = x * r
    dyg = dy * g
    dx = r * (dyg - xhat * jnp.mean(dyg * xhat, axis=-1, keepdims=True))
    return dx, dy * xhat


def _zero_when(first, *refs):
    @pl.when(first)
    def _():
        for ref in refs:
            ref[...] = jnp.zeros(ref.shape, ref.dtype)


def _acc(ref, val):
    ref[...] += val


def _colsum(a):
    return jnp.sum(a, axis=0, keepdims=True)


def _in_proj_fwd(x, g, w, name, tm=1024):
    t, d = x.shape
    n = w.shape[1]
    tm = min(tm, t)

    def body(x_ref, g_ref, w_ref, z_ref, h_ref):
        hs = []
        for rows in _halves(tm):
            h, _ = _rms(x_ref[rows, :], g_ref[...])
            hs.append(_mx(h))
            h_ref[rows, :] = hs[-1]
        for rows, h in zip(_halves(tm), hs):
            z_ref[rows, :] = _dot(h, w_ref[...])

    return pl.pallas_call(
        body, name=name, grid=(t // tm,),
        in_specs=[pl.BlockSpec((tm, d), lambda i: (i, 0)), pl.BlockSpec((1, d), lambda i: (0, 0)),
                  pl.BlockSpec((d, n), lambda i: (0, 0), pipeline_mode=pl.Buffered(1))],
        out_specs=[pl.BlockSpec((tm, n), lambda i: (i, 0)), pl.BlockSpec((tm, d), lambda i: (i, 0))],
        out_shape=[jax.ShapeDtypeStruct((t, n), F32), jax.ShapeDtypeStruct((t, d), MXU_DTYPE)],
        compiler_params=_cp())(x, g, w)


def _in_proj_bwd(dz_mix, dz_c, dz_kr, dz_gate, h, x, d_res, w, g, name, tm=512):
    t, d = x.shape
    n = w.shape[1]
    tm = min(tm, t)

    def body(dm_ref, dc_ref, dk_ref, dg_ref, h_ref, x_ref, dres_ref, w_ref, g_ref, dx_ref, dw_ref, dgn_ref):
        first = pl.program_id(0) == 0
        dz = jnp.concatenate([dm_ref[...], dc_ref[...], dk_ref[...], dg_ref[...]], axis=1)

        _zero_when(first, dw_ref, dgn_ref)
        dh = _dot_nt(dz, w_ref[...])
        hb = h_ref[...]
        for c0 in range(0, n, 512):
            dw_ref[:, c0:c0 + 512] += _dot_tn(hb, dz[:, c0:c0 + 512])
        xf = x_ref[...]
        r = lax.rsqrt(jnp.mean(xf * xf, axis=-1, keepdims=True) + EPS)
        dx, dgt = _rms_bwd(xf, r, g_ref[...], dh)
        dx_ref[...] = dx + dres_ref[...]
        _acc(dgn_ref, _colsum(dgt))

    row = lambda wd: pl.BlockSpec((tm, wd), lambda i: (i, 0))
    fixed = lambda a, b: pl.BlockSpec((a, b), lambda i: (0, 0), pipeline_mode=pl.Buffered(1))
    return pl.pallas_call(
        body, name=name, grid=(t // tm,),
        in_specs=[row(Z_MIX), row(Z_C), row(Z_KR), row(Z_GATE), row(d), row(d), row(d), fixed(d, n), fixed(1, d)],
        out_specs=[row(d), fixed(d, n), fixed(1, d)],
        out_shape=[jax.ShapeDtypeStruct((t, d), F32), jax.ShapeDtypeStruct((d, n), F32),
                   jax.ShapeDtypeStruct((1, d), F32)],
        compiler_params=_cp())(dz_mix, dz_c, dz_kr, dz_gate, h, x, d_res, w, g)


def _lane_group(shape):
    return lax.broadcasted_iota(jnp.int32, shape, 1) // 64


def _select_group(vals):
    grp = _lane_group(vals[0].shape)
    out = vals[3]
    for gi in (2, 1, 0):
        out = jnp.where(grp == gi, vals[gi], out)
    return out


def _sgu_mask(transposed):
    r = (lax.broadcasted_iota(jnp.int32, (512, 128), 0) % 128) // CHUNK
    c = lax.broadcasted_iota(jnp.int32, (512, 128), 1) // CHUNK
    return (r <= c) if transposed else (c <= r)


def _sgu_apply(wstack, vb, nblk):
    outs = []
    for n in range(nblk):
        r = _dot(wstack, vb[n * 128:(n + 1) * 128, :])
        outs.append(_select_group([r[hh * 128:(hh + 1) * 128, :] for hh in range(4)]))
    return jnp.concatenate(outs, axis=0)


def _layer_norm(v, g, b):
    mu = jnp.mean(v, axis=-1, keepdims=True)
    vc = v - mu
    rstd = lax.rsqrt(jnp.mean(vc * vc, axis=-1, keepdims=True) + EPS)
    vhat = vc * rstd
    return vhat * g + b, vhat, rstd


def _div_by_counts(x, t0):
    t = t0 + lax.broadcasted_iota(jnp.int32, (16, 256), 0)
    w = _select_group([jnp.full((16, 256), wv, jnp.int32) for wv in (2, 4, 8, 16)])
    head = x[:16, :] / jnp.minimum(t + 1, w).astype(F32)
    inv_w = _select_group([jnp.full((1, 256), 1.0 / wv, F32) for wv in (2, 4, 8, 16)])
    return jnp.concatenate([head, x[16:, :] * inv_w], axis=0)


def _pooled(p, halo, t0):
    tm = p.shape[0]
    ext = jnp.concatenate([halo, p], axis=0)
    s2 = ext + pltpu.roll(ext, 1, 0)
    s4 = s2 + pltpu.roll(s2, 2, 0)
    s8 = s4 + pltpu.roll(s4, 4, 0)
    s16 = s8 + pltpu.roll(s8, 8, 0)
    sel = _select_group([s2, s4, s8, s16])[16:, :]
    return _div_by_counts(sel, t0) - p


def _pooled_bwd(dpool, dpool_halo, t0):
    tm = dpool.shape[0]
    n = tm + 16
    ext = _div_by_counts(jnp.concatenate([dpool, dpool_halo], axis=0), t0)
    f2 = ext + pltpu.roll(ext, n - 1, 0)
    f4 = f2 + pltpu.roll(f2, n - 2, 0)
    f8 = f4 + pltpu.roll(f4, n - 4, 0)
    f16 = f8 + pltpu.roll(f8, n - 8, 0)
    return _select_group([f2, f4, f8, f16])[:tm, :] - dpool


def _mix_specs(t, tm):
    nt16 = t // 16
    zrow = pl.BlockSpec((tm, Z_MIX), lambda i: (i, 0))
    prev_halo = pl.BlockSpec((16, 256), lambda i: (jnp.maximum(i * (tm // 16) - 1, 0), 3))
    fixed = lambda a, b: pl.BlockSpec((a, b), lambda i: (0, 0))
    params = [fixed(512, 128), fixed(128, 256), fixed(1, 256), fixed(1, 256), fixed(256, 256), fixed(1, 256)]
    return nt16, zrow, prev_halo, fixed, params


def _mix_fwd(z, sgu_w, sgu_bias, ln_g, ln_b, pool_wbd, pool_scale, name, tm=512):
    t = z.shape[0]
    tm = min(tm, t)
    _, zrow, prev_halo, _, params = _mix_specs(t, tm)

    def body(z_ref, halo_ref, w_ref, bias_ref, lng_ref, lnb_ref, pw_ref, ps_ref, y_ref):
        i = pl.program_id(0)
        u, v, gate = z_ref[:, 0:256], z_ref[:, 256:512], z_ref[:, 512:768]
        p, pgate = z_ref[:, 768:1024], z_ref[:, 1024:1280]
        vn, _, _ = _layer_norm(v, lng_ref[...], lnb_ref[...])
        wm = _mx(jnp.where(_sgu_mask(False), w_ref[...], 0.0))
        halo = jnp.where(i > 0, halo_ref[...], 0.0)
        pooled = _pooled(p, halo, i * tm)
        mixed = _sgu_apply(wm, _mx(vn), tm // 128) + jnp.tile(bias_ref[...], (tm // 128, 1))
        mixedp = _dot(_mx(pooled), pw_ref[...])
        ya = u * mixed * _silu_and_grad(gate)[0]
        yb = mixedp * ps_ref[...] * _silu_and_grad(pgate)[0]
        y_ref[...] = _mx(jnp.concatenate([ya, yb], axis=1))

    return pl.pallas_call(
        body, name=name, grid=(t // tm,),
        in_specs=[zrow, prev_halo] + params,
        out_specs=pl.BlockSpec((tm, 512), lambda i: (i, 0)),
        out_shape=jax.ShapeDtypeStruct((t, 512), MXU_DTYPE),
        compiler_params=_cp())(z, z, sgu_w, sgu_bias, ln_g, ln_b, pool_wbd, pool_scale)


def _mix_bwd(z, dycat, sgu_w, sgu_wt, sgu_bias, ln_g, ln_b, pool_wbd, pool_scale, name, tm=512):
    t = z.shape[0]
    tm = min(tm, t)
    nt16, zrow, prev_halo, fixed, params = _mix_specs(t, tm)
    nblk = tm // 128
    last = t // tm - 1

    def body(z_ref, halo_ref, zn_ref, dy_ref, dyn_ref, w_ref, wt_ref, bias_ref, lng_ref, lnb_ref, pw_ref, ps_ref,
             dz_ref, dw_ref, db_ref, dlng_ref, dlnb_ref, dpw_ref, dps_ref):
        i = pl.program_id(0)
        _zero_when(i == 0, dw_ref, db_ref, dlng_ref, dlnb_ref, dpw_ref, dps_ref)
        u, v, gate = z_ref[:, 0:256], z_ref[:, 256:512], z_ref[:, 512:768]
        p, pgate = z_ref[:, 768:1024], z_ref[:, 1024:1280]
        dya, dyb = dy_ref[:, 0:256], dy_ref[:, 256:512]
        vn, vhat, rstd = _layer_norm(v, lng_ref[...], lnb_ref[...])
        vnb = _mx(vn)
        wm = _mx(jnp.where(_sgu_mask(False), w_ref[...], 0.0))
        wmt = _mx(jnp.where(_sgu_mask(True), wt_ref[...], 0.0))
        halo = jnp.where(i > 0, halo_ref[...], 0.0)
        pooled_b = _mx(_pooled(p, halo, i * tm))
        silu, dsilu = _silu_and_grad(gate)
        psilu, pdsilu = _silu_and_grad(pgate)
        mixed = _sgu_apply(wm, vnb, nblk) + jnp.tile(bias_ref[...], (nblk, 1))
        mixedp = _dot(pooled_b, pw_ref[...])
        t1 = u * mixed
        d_gate = dya * t1 * dsilu
        d_t1 = dya * silu
        d_u = d_t1 * mixed
        d_mixed = d_t1 * u
        dmb = _mx(d_mixed)
        d_pgate = dyb * (mixedp * ps_ref[...]) * pdsilu
        d_ms = dyb * psilu
        _acc(dps_ref, _colsum(d_ms * mixedp))
        dmpb = _mx(d_ms * ps_ref[...])
        dmp_halo = _mx(dyn_ref[...] * _silu_and_grad(zn_ref[...])[0] * ps_ref[...])
        d_vn = _sgu_apply(wmt, dmb, nblk)
        grp = _lane_group((128, 256))
        lane = lax.broadcasted_iota(jnp.int32, (128, 128), 1)
        dws = [jnp.zeros((128, 128), F32) for _ in range(4)]
        dbias = jnp.zeros((128, 128), F32)
        for n in range(nblk):
            dm_n, dmb_n, vnb_n = d_mixed[n * 128:(n + 1) * 128], dmb[n * 128:(n + 1) * 128], vnb[n * 128:(n + 1) * 128]
            for hh in range(4):
                dws[hh] = dws[hh] + _dot_nt(jnp.where(grp == hh, dmb_n, jnp.zeros_like(dmb_n)), vnb_n)
                rs = jnp.sum(jnp.where(grp == hh, dm_n, 0.0), axis=-1, keepdims=True)
                dbias = dbias + jnp.where(lane == hh, rs, 0.0)
        _acc(dw_ref, jnp.concatenate(dws, axis=0))
        _acc(db_ref, dbias)
        _acc(dpw_ref, _dot_tn(pooled_b, dmpb))
        d_pooled = _dot_nt(dmpb, pw_ref[...])
        d_pooled_halo = jnp.where(i < last, _dot_nt(dmp_halo, pw_ref[...]), 0.0)
        _acc(dlng_ref, _colsum(d_vn * vhat))
        _acc(dlnb_ref, _colsum(d_vn))
        dvh = d_vn * lng_ref[...]
        d_v = rstd * (dvh - jnp.mean(dvh, axis=-1, keepdims=True) - vhat * jnp.mean(dvh * vhat, axis=-1, keepdims=True))
        d_p = _pooled_bwd(d_pooled, d_pooled_halo, i * tm)
        dz_ref[...] = _mx(jnp.concatenate([d_u, d_v, d_gate, d_p, d_pgate], axis=1))

        @pl.when(i == last)
        def _():
            dw_ref[...] = jnp.where(_sgu_mask(False), dw_ref[...], 0.0)

    nxt = lambda i: jnp.minimum((i + 1) * (tm // 16), nt16 - 1)
    return pl.pallas_call(
        body, name=name, grid=(t // tm,),
        in_specs=[zrow, prev_halo, pl.BlockSpec((16, 256), lambda i: (nxt(i), 4)),
                  pl.BlockSpec((tm, 512), lambda i: (i, 0)), pl.BlockSpec((16, 256), lambda i: (nxt(i), 1)),
                  params[0], fixed(512, 128)] + params[1:],
        out_specs=[pl.BlockSpec((tm, Z_MIX), lambda i: (i, 0)), fixed(512, 128), fixed(128, 128), fixed(1, 256),
                   fixed(1, 256), fixed(256, 256), fixed(1, 256)],
        out_shape=[jax.ShapeDtypeStruct((t, Z_MIX), MXU_DTYPE), jax.ShapeDtypeStruct((512, 128), F32),
                   jax.ShapeDtypeStruct((128, 128), F32), jax.ShapeDtypeStruct((1, 256), F32),
                   jax.ShapeDtypeStruct((1, 256), F32), jax.ShapeDtypeStruct((256, 256), F32),
                   jax.ShapeDtypeStruct((1, 256), F32)],
        compiler_params=_cp())(z, z, z, dycat, dycat, sgu_w, sgu_wt, sgu_bias, ln_g, ln_b, pool_wbd, pool_scale)


def _rot_half(x, transpose):
    w = x.shape[1]
    lane = lax.broadcasted_iota(jnp.int32, x.shape, 1) % min(w, 256)
    base = 128 if w >= 256 else 0
    lo = jnp.logical_and(lane >= base, lane < base + 32)
    hi = jnp.logical_and(lane >= base + 32, lane < base + 64)
    up = pltpu.roll(x, w - 32, 1)
    down = pltpu.roll(x, 32, 1)
    if transpose:
        return jnp.where(lo, up, jnp.where(hi, -down, 0.0))
    return jnp.where(lo, -up, jnp.where(hi, down, 0.0))


def _rope(x, c, s):
    return x * c + _rot_half(x, False) * s


def _rope_bwd(dy, c, s):
    return dy * c + _rot_half(dy * s, True)


def _qkv_fwd(z, rc, rs, w_uq, w_ukv, gq, gkv, name, tm=512):
    t = z.shape[0]
    tm = min(tm, t)

    def body(zc_ref, zk_ref, rc_ref, rs_ref, wq_ref, wkv_ref, gq_ref, gkv_ref, q_ref, k_ref, v_ref):
        cq, ckv = zc_ref[:, 0:384], zc_ref[:, 384:640]
        c, s = rc_ref[...], rs_ref[...]
        qn, _ = _rms(cq, gq_ref[...])
        kvn, _ = _rms(ckv, gkv_ref[...])
        q_pre = _dot(_mx(qn), wq_ref[...])
        kv = _dot(_mx(kvn), wkv_ref[...])
        kpe = _rope(zk_ref[...], c[:, 128:256], s[:, 128:256])
        q = _rope(q_pre, jnp.tile(c, (1, N_HEADS)), jnp.tile(s, (1, N_HEADS)))
        for hh in range(N_HEADS):
            q_ref[hh] = _mx(q[:, hh * QK_PAD:(hh + 1) * QK_PAD])
            k_ref[hh] = _mx(jnp.concatenate([kv[:, hh * 128:(hh + 1) * 128], kpe], axis=1))
            v_ref[hh] = _mx(kv[:, 512 + hh * 128:512 + (hh + 1) * 128])

    fixed = lambda a, b: pl.BlockSpec((a, b), lambda i: (0, 0))
    heads = lambda wd: pl.BlockSpec((N_HEADS, tm, wd), lambda i: (0, i, 0))
    return pl.pallas_call(
        body, name=name, grid=(t // tm,),
        in_specs=[pl.BlockSpec((tm, Z_C), lambda i: (i, Z_MIX // Z_C)),
                  pl.BlockSpec((tm, Z_KR), lambda i: (i, (Z_MIX + Z_C) // Z_KR)),
                  pl.BlockSpec((tm, 256), lambda i: (i, 0)), pl.BlockSpec((tm, 256), lambda i: (i, 0)),
                  fixed(384, 1024), fixed(256, 1024), fixed(1, 384), fixed(1, 256)],
        out_specs=[heads(QK_PAD), heads(QK_PAD), heads(V_DIM)],
        out_shape=[jax.ShapeDtypeStruct((N_HEADS, t, QK_PAD), MXU_DTYPE),
                   jax.ShapeDtypeStruct((N_HEADS, t, QK_PAD), MXU_DTYPE),
                   jax.ShapeDtypeStruct((N_HEADS, t, V_DIM), MXU_DTYPE)],
        compiler_params=_cp())(z, z, rc, rs, w_uq, w_ukv, gq, gkv)


def _qkv_bwd(dq, dk, dv, z, rc, rs, w_uq, w_ukv, gq, gkv, name, tm=512):
    t = z.shape[0]
    tm = min(tm, t)

    def body(dq_ref, dk_ref, dv_ref, zc_ref, rc_ref, rs_ref, wq_ref, wkv_ref, gq_ref, gkv_ref,
             dzc_ref, dzk_ref, dwq_ref, dwkv_ref, dgq_ref, dgkv_ref):
        _zero_when(pl.program_id(0) == 0, dwq_ref, dwkv_ref, dgq_ref, dgkv_ref)
        cq, ckv = zc_ref[:, 0:384], zc_ref[:, 384:640]
        c, s = rc_ref[...], rs_ref[...]
        dkv = _mx(jnp.concatenate([dk_ref[hh][:, 0:128] for hh in range(N_HEADS)]
                                  + [dv_ref[hh] for hh in range(N_HEADS)], axis=1))
        kvn, rkv = _rms(ckv, gkv_ref[...])
        _acc(dwkv_ref, _dot_tn(_mx(kvn), dkv))
        d_kvn = _dot_nt(dkv, wkv_ref[...])
        dq_all = jnp.concatenate([dq_ref[hh] for hh in range(N_HEADS)], axis=1)
        dqp = _mx(_rope_bwd(dq_all, jnp.tile(c, (1, N_HEADS)), jnp.tile(s, (1, N_HEADS))))
        qn, rq = _rms(cq, gq_ref[...])
        dkpe = dk_ref[0][:, 128:256]
        for hh in range(1, N_HEADS):
            dkpe = dkpe + dk_ref[hh][:, 128:256]
        dzk_ref[...] = _mx(_rope_bwd(dkpe, c[:, 128:256], s[:, 128:256]))
        _acc(dwq_ref, _dot_tn(_mx(qn), dqp))
        d_qn = _dot_nt(dqp, wq_ref[...])
        d_ckv, dgkv_t = _rms_bwd(ckv, rkv, gkv_ref[...], d_kvn)
        _acc(dgkv_ref, _colsum(dgkv_t))
        d_cq, dgq_t = _rms_bwd(cq, rq, gq_ref[...], d_qn)
        _acc(dgq_ref, _colsum(dgq_t))
        dzc_ref[...] = _mx(jnp.concatenate([d_cq, d_ckv], axis=1))

    fixed = lambda a, b: pl.BlockSpec((a, b), lambda i: (0, 0))
    heads = lambda wd: pl.BlockSpec((N_HEADS, tm, wd), lambda i: (0, i, 0))
    return pl.pallas_call(
        body, name=name, grid=(t // tm,),
        in_specs=[heads(QK_PAD), heads(QK_PAD), heads(V_DIM), pl.BlockSpec((tm, Z_C), lambda i: (i, Z_MIX // Z_C)),
                  pl.BlockSpec((tm, 256), lambda i: (i, 0)), pl.BlockSpec((tm, 256), lambda i: (i, 0)),
                  fixed(384, 1024), fixed(256, 1024), fixed(1, 384), fixed(1, 256)],
        out_specs=[pl.BlockSpec((tm, Z_C), lambda i: (i, 0)), pl.BlockSpec((tm, Z_KR), lambda i: (i, 0)),
                   fixed(384, 1024), fixed(256, 1024), fixed(1, 384), fixed(1, 256)],
        out_shape=[jax.ShapeDtypeStruct((t, Z_C), MXU_DTYPE), jax.ShapeDtypeStruct((t, Z_KR), MXU_DTYPE),
                   jax.ShapeDtypeStruct((384, 1024), F32), jax.ShapeDtypeStruct((256, 1024), F32),
                   jax.ShapeDtypeStruct((1, 384), F32), jax.ShapeDtypeStruct((1, 256), F32)],
        compiler_params=_cp())(dq, dk, dv, z, rc, rs, w_uq, w_ukv, gq, gkv)


def _loop_in_long_trips(n, body, longest=4):
    def doubled(inner):
        return lambda t, carry: inner(2 * t + 1, inner(2 * t, carry))

    trips = [body]
    while 2 ** (len(trips) - 1) < longest:
        trips.append(doubled(trips[-1]))
    done = 0
    for level in reversed(range(len(trips))):
        size = 2 ** level
        end = n // size
        lax.fori_loop(done, end, trips[level], 0)
        done = 2 * end if level else end


def _head_loads(k_src, v_src, k_dst, v_dst, sems, first_rows):
    t = k_src.shape[0]
    parts = [(0, first_rows)] + ([(first_rows, t - first_rows)] if t > first_rows else [])
    return [pltpu.make_async_copy(src.at[pl.ds(r0, n)], dst.at[pl.ds(r0, n)], sems.at[2 * q + w])
            for q, (r0, n) in enumerate(parts) for w, (src, dst) in enumerate([(k_src, k_dst), (v_src, v_dst)])]


def _init_mask_bias(bias_ref):
    _, tq, tk = bias_ref.shape
    r = lax.broadcasted_iota(jnp.int32, (tq, tk), 0) // CHUNK
    c = lax.broadcasted_iota(jnp.int32, (tq, tk), 1) // CHUNK
    bias_ref[0] = jnp.zeros((tq, tk), F32)
    for d in range(tq // tk):
        bias_ref[1 + d] = jnp.where(c + d * (tk // CHUNK) <= r, 0.0, NEG_INF)


def _gate_block(tq):
    return pl.BlockSpec((tq, 128), lambda h, i: (i, (Z_MIX + Z_C + Z_KR) // 128 + h))


def _attn_fwd(qh, kh, vh, z, name, tq=1024, tk=512):
    t = qh.shape[1]
    tq = min(tq, t)
    tk = min(tk, tq)
    ratio = tq // tk

    def body(q_ref, g_ref, k_hbm, v_hbm, o_ref, yc_ref, lse_ref, k_v, v_v, m_s, acc_s, s_a, s_b, mx_a, mx_b, bias_s,
             sem):
        h, i = pl.program_id(0), pl.program_id(1)

        loads = _head_loads(k_hbm.at[h], v_hbm.at[h], k_v, v_v.at[:, 0:V_DIM], sem, tq)

        @pl.when(i == 0)
        def _():
            for cp in loads:
                cp.start()
            v_v[:, V_DIM:2 * V_DIM] = jnp.ones((t, V_DIM), MXU_DTYPE)
            _init_mask_bias(bias_s)
            for cp in loads[:2]:
                cp.wait()

        @pl.when(i == 1)
        def _():
            for cp in loads[2:]:
                cp.wait()

        q = q_ref[...]
        m_s[...] = jnp.full(m_s.shape, NEG_INF, F32)
        acc_s[...] = jnp.zeros(acc_s.shape, F32)

        last = ratio * (i + 1) - 1

        def keys(j):
            return pl.ds(pl.multiple_of(j * tk, tk), tk)

        def scores(s_ref, mx_ref, j, biased):
            s = _dot_nt(q, k_v[keys(j), :]) * (SCALE * LOG2E)
            if biased:
                s = s + bias_s[jnp.maximum(j - ratio * i + 1, 0)]
            s_ref[...] = s
            mx_ref[...] = jnp.broadcast_to(jnp.max(s, axis=-1, keepdims=True), mx_ref.shape)

        def softmax_pv(s_ref, mx_ref, j):
            m_old = m_s[...]
            m_new = jnp.maximum(m_old, mx_ref[...])
            p = jnp.exp2(s_ref[...] - jnp.tile(m_new, (1, tk // 128)))
            alpha = jnp.exp2(m_old - m_new)
            m_s[...] = m_new
            acc_s[...] = jnp.tile(alpha, (1, 2)) * acc_s[...] + _dot(_mx(p), v_v[keys(j), :])

        scores(s_a, mx_a, 0, True)

        def pair(pp, carry, biased):
            scores(s_b, mx_b, 2 * pp + 1, biased)
            softmax_pv(s_a, mx_a, 2 * pp)
            scores(s_a, mx_a, jnp.minimum(2 * pp + 2, last), biased)
            softmax_pv(s_b, mx_b, 2 * pp + 1)
            return carry

        n_pairs = (last + 1) // 2
        n_plain = jnp.maximum(ratio * i // 2 - 1, 0)
        _loop_in_long_trips(n_plain, lambda pp, carry: pair(pp, carry, False))
        _loop_in_long_trips(n_pairs - n_plain, lambda pp, carry: pair(n_plain + pp, carry, True))
        if ratio % 2 == 1:
            @pl.when(last % 2 == 0)
            def _():
                softmax_pv(s_a, mx_a, last)

        l = acc_s[:, V_DIM:2 * V_DIM]
        o = acc_s[:, 0:V_DIM] / l
        o_ref[...] = o
        yc_ref[...] = _mx(o * _silu_and_grad(g_ref[...])[0])
        lse_ref[...] = m_s[...] + jnp.log2(l)

    return pl.pallas_call(
        body, name=name, grid=(N_HEADS, t // tq),
        in_specs=[pl.BlockSpec((None, tq, QK_PAD), lambda h, i: (h, i, 0)), _gate_block(tq), ANY, ANY],
        out_specs=[pl.BlockSpec((tq, 128), lambda h, i: (i, h)), pl.BlockSpec((tq, 128), lambda h, i: (i, h)),
                   pl.BlockSpec((None, tq, 128), lambda h, i: (h, i, 0))],
        out_shape=[jax.ShapeDtypeStruct((t, N_HEADS * V_DIM), F32), jax.ShapeDtypeStruct((t, N_HEADS * V_DIM), MXU_DTYPE),
                   jax.ShapeDtypeStruct((N_HEADS, t, 128), F32)],
        scratch_shapes=[pltpu.VMEM((t, QK_PAD), MXU_DTYPE), pltpu.VMEM((t, 2 * V_DIM), MXU_DTYPE),
                        pltpu.VMEM((tq, 128), F32), pltpu.VMEM((tq, 2 * V_DIM), F32),
                        pltpu.VMEM((tq, tk), F32), pltpu.VMEM((tq, tk), F32), pltpu.VMEM((tq, 128), F32),
                        pltpu.VMEM((tq, 128), F32), pltpu.VMEM((ratio + 1, tq, tk), F32),
                        pltpu.SemaphoreType.DMA((4,))],
        compiler_params=_cp(2))(qh, z, kh, vh)


def _attn_bwd(qh, kh, vh, o, lse, dycat, z, name, tq=512):
    t = qh.shape[1]
    tq = min(tq, t)
    nq = t // tq

    def body(q_ref, o_ref, lse_ref, dy_ref, g_ref, k_hbm, v_hbm, dq_ref, dgate_ref, dk_hbm, dv_hbm,
             k_v, v_v, dk_acc, dv_acc, dq_acc, delta_s, s_a, dp_a, s_b, dp_b, bias_s, sem):
        h, i = pl.program_id(0), pl.program_id(1)

        loads = _head_loads(k_hbm.at[h], v_hbm.at[h], k_v, v_v, sem, tq)

        @pl.when(i == 0)
        def _():
            for cp in loads:
                cp.start()
            _init_mask_bias(bias_s)
            dk_acc[...] = jnp.zeros(dk_acc.shape, F32)
            dv_acc[...] = jnp.zeros(dv_acc.shape, F32)
            for cp in loads[:2]:
                cp.wait()

        @pl.when(i == 1)
        def _():
            for cp in loads[2:]:
                cp.wait()

        gate, dy, of = g_ref[...], dy_ref[...], o_ref[...]
        silu, dsilu = _silu_and_grad(gate)
        do = dy * silu
        delta = jnp.sum(do * of, axis=-1, keepdims=True)
        dgate_ref[...] = _mx(dy * of * dsilu)
        dob = _mx(do)
        q = q_ref[...]
        delta_s[...] = jnp.broadcast_to(delta, delta_s.shape)
        dq_acc[...] = jnp.zeros(dq_acc.shape, F32)

        def keys(j):
            return pl.ds(pl.multiple_of(j * tq, tq), tq)

        def scores(s_ref, dp_ref, j):
            s = _dot_nt(q, k_v[keys(j), :]) * (SCALE * LOG2E) + bias_s[(j == i).astype(jnp.int32)]
            s_ref[...] = s - jnp.tile(lse_ref[...], (1, tq // 128))
            dp_ref[...] = _dot_nt(dob, v_v[keys(j), :]) - jnp.tile(delta_s[...], (1, tq // 128))

        def grads(s_ref, dp_ref, j):
            ks = keys(j)
            p = jnp.exp2(s_ref[...])
            ds = p * dp_ref[...] * SCALE
            pb, dsb = _mx(p), _mx(ds)
            dq_acc[...] += _dot(dsb, k_v[ks, :])
            dk_acc[ks, :] += _dot_tn(dsb, q)
            dv_acc[ks, :] += _dot_tn(pb, dob)

        scores(s_a, dp_a, 0)

        def pair(pp, carry):
            scores(s_b, dp_b, 2 * pp + 1)
            grads(s_a, dp_a, 2 * pp)
            scores(s_a, dp_a, jnp.minimum(2 * pp + 2, i))
            grads(s_b, dp_b, 2 * pp + 1)
            return carry

        _loop_in_long_trips((i + 1) // 2, pair, longest=8)

        @pl.when(i % 2 == 0)
        def _():
            grads(s_a, dp_a, i)

        dq_ref[...] = dq_acc[...]

        @pl.when(i == nq - 1)
        def _():
            ck = pltpu.make_async_copy(dk_acc, dk_hbm.at[h], sem.at[0])
            cv = pltpu.make_async_copy(dv_acc, dv_hbm.at[h], sem.at[1])
            ck.start()
            cv.start()
            ck.wait()
            cv.wait()

    return pl.pallas_call(
        body, name=name, grid=(N_HEADS, nq),
        in_specs=[pl.BlockSpec((None, tq, QK_PAD), lambda h, i: (h, i, 0)),
                  pl.BlockSpec((tq, 128), lambda h, i: (i, h)),
                  pl.BlockSpec((None, tq, 128), lambda h, i: (h, i, 0)),
                  pl.BlockSpec((tq, 128), lambda h, i: (i, N_HEADS + h)), _gate_block(tq), ANY, ANY],
        out_specs=[pl.BlockSpec((None, tq, QK_PAD), lambda h, i: (h, i, 0)),
                   pl.BlockSpec((tq, 128), lambda h, i: (i, h)), ANY, ANY],
        out_shape=[jax.ShapeDtypeStruct((N_HEADS, t, QK_PAD), F32), jax.ShapeDtypeStruct((t, Z_GATE), MXU_DTYPE),
                   jax.ShapeDtypeStruct((N_HEADS, t, QK_PAD), F32), jax.ShapeDtypeStruct((N_HEADS, t, V_DIM), F32)],
        scratch_shapes=[pltpu.VMEM((t, QK_PAD), MXU_DTYPE), pltpu.VMEM((t, V_DIM), MXU_DTYPE),
                        pltpu.VMEM((t, QK_PAD), F32), pltpu.VMEM((t, V_DIM), F32), pltpu.VMEM((tq, QK_PAD), F32),
                        pltpu.VMEM((tq, 128), F32)] + [pltpu.VMEM((tq, tq), F32)] * 4
        + [pltpu.VMEM((2, tq, tq), F32), pltpu.SemaphoreType.DMA((4,))],
        compiler_params=_cp(2))(qh, o, lse, dycat, z, kh, vh)


def _out_proj_fwd(yab, yc, w, x, g, target, name, tm=512):
    t, d = x.shape
    tm = min(tm, t)
    is_last = target is not None

    def body(*refs):
        if is_last:
            yab_ref, yc_ref, w_ref, x_ref, g_ref, t_ref, y_ref, dout_ref, loss_ref = refs
            _zero_when(pl.program_id(0) == 0, loss_ref)
        else:
            yab_ref, yc_ref, w_ref, x_ref, g_ref, y_ref, out_ref = refs
        y = _dot(jnp.concatenate([yab_ref[...], yc_ref[...]], axis=1), w_ref[...])
        y_ref[...] = y
        out = x_ref[...] + _rms(y, g_ref[...])[0]
        if is_last:
            diff = out - t_ref[...]
            dout_ref[...] = diff * (1.0 / d)
            part = jnp.sum(jnp.sum(diff * diff, axis=-1, keepdims=True), axis=0, keepdims=True) * (0.5 / d)
            _acc(loss_ref, jnp.broadcast_to(part, (1, 128)))
        else:
            out_ref[...] = out

    row = lambda wd: pl.BlockSpec((tm, wd), lambda i: (i, 0))
    fixed = lambda a, b: pl.BlockSpec((a, b), lambda i: (0, 0))
    in_specs = [row(512), row(512), fixed(d, d), row(d), fixed(1, d)]
    args = [yab, yc, w, x, g]
    out_specs = [row(d), row(d)]
    out_shape = [jax.ShapeDtypeStruct((t, d), F32), jax.ShapeDtypeStruct((t, d), F32)]
    if is_last:
        in_specs.append(row(d))
        args.append(target)
        out_specs.append(fixed(1, 128))
        out_shape.append(jax.ShapeDtypeStruct((1, 128), F32))
    return pl.pallas_call(body, name=name, grid=(t // tm,), in_specs=in_specs, out_specs=out_specs,
                          out_shape=out_shape, compiler_params=_cp())(*args)


def _halves(tm):
    half = tm // 2 if tm >= 512 else tm
    return [pl.ds(s, half) for s in range(0, tm, half)]


def _out_proj_bwd(dout, y, yab, yc, w, g, name, tm=1024):
    t, d = y.shape
    tm = min(tm, t)

    def body(dout_ref, y_ref, yab_ref, yc_ref, w_ref, g_ref, dycat_ref, dw_ref, dg_ref):
        _zero_when(pl.program_id(0) == 0, dw_ref, dg_ref)
        dybs = []
        for rows in _halves(tm):
            y = y_ref[rows, :]
            r = lax.rsqrt(jnp.mean(y * y, axis=-1, keepdims=True) + EPS)
            dy, dgt = _rms_bwd(y, r, g_ref[...], dout_ref[rows, :])
            _acc(dg_ref, _colsum(dgt))
            dybs.append(_mx(dy))
        for rows, dyb in zip(_halves(tm), dybs):
            _acc(dw_ref, _dot_tn(jnp.concatenate([yab_ref[rows, :], yc_ref[rows, :]], axis=1), dyb))
            dycat_ref[rows, :] = _dot_nt(dyb, w_ref[...])

    row = lambda wd: pl.BlockSpec((tm, wd), lambda i: (i, 0))
    fixed = lambda a, b: pl.BlockSpec((a, b), lambda i: (0, 0), pipeline_mode=pl.Buffered(1))
    return pl.pallas_call(
        body, name=name, grid=(t // tm,),
        in_specs=[row(d), row(d), row(512), row(512), fixed(d, d), fixed(1, d)],
        out_specs=[row(d), fixed(d, d), fixed(1, d)],
        out_shape=[jax.ShapeDtypeStruct((t, d), F32), jax.ShapeDtypeStruct((d, d), F32),
                   jax.ShapeDtypeStruct((1, d), F32)],
        compiler_params=_cp())(dout, y, yab, yc, w, g)


def _mesh_pos():
    return lax.axis_index("x"), lax.axis_index("y"), lax.axis_index("c")


def _remote(src, dst, send_sem, recv_sem, to):
    return pltpu.make_async_remote_copy(src_ref=src, dst_ref=dst, send_sem=send_sem, recv_sem=recv_sem,
                                        device_id=to, device_id_type=MESH)


CHUNK_ROWS = 256


def _pieces(rows):
    return [(s, min(CHUNK_ROWS, rows - s)) for s in range(0, rows, CHUNK_ROWS)]


def _piece_table(shapes):
    return [(a, s, sz) for a, shp in enumerate(shapes) for s, sz in _pieces(shp[-2])]


def _gather_weights(shards):
    n = len(shards)
    table = _piece_table([s.shape for s in shards])
    npc = len(table)

    def body(*refs):
        ins, outs = refs[:n], refs[n:2 * n]
        send_sems, recv_sems, fwd_send, fwd_recv = refs[2 * n:]
        x, y, c = _mesh_pos()
        me, sibling = (x, y, c), (x, y, 1 - c)
        chips = [(1 - x, y), (x, 1 - y), (1 - x, 1 - y)]
        slot = lambda cx, cy, layer: 2 * (2 * cx + cy) + layer
        first = []
        for a in range(n):
            for j, (cx, cy) in enumerate(chips):
                first.append(_remote(ins[a].at[c], outs[a].at[slot(x, y, c)], send_sems.at[a, j], recv_sems.at[a, j],
                                     (cx, cy, c)))
                first[-1].start()
        passed = []
        for j, (cx, cy) in enumerate(chips):
            for a in range(n):
                blk = outs[a].at[slot(cx, cy, c)]
                _remote(blk, blk, send_sems.at[a, j], recv_sems.at[a, j], me).wait_recv()
            for q, (a, s, sz) in enumerate(table):
                rows = outs[a].at[slot(cx, cy, c), pl.ds(s, sz)]
                passed.append(_remote(rows, rows, fwd_send.at[j, q], fwd_recv.at[j, q], sibling))
                passed[-1].start()
        for j, (cx, cy) in enumerate(chips):
            for q, (a, s, sz) in enumerate(table):
                rows = outs[a].at[slot(cx, cy, 1 - c), pl.ds(s, sz)]
                _remote(rows, rows, fwd_send.at[j, q], fwd_recv.at[j, q], me).wait_recv()
        for cp in first + passed:
            cp.wait_send()

    return pl.pallas_call(
        body, name="gather_weights", in_specs=[ANY] * n, out_specs=[ANY] * n,
        out_shape=[jax.ShapeDtypeStruct((8,) + s.shape[1:], s.dtype) for s in shards],
        scratch_shapes=[pltpu.SemaphoreType.DMA((n, 3)), pltpu.SemaphoreType.DMA((n, 3)),
                        pltpu.SemaphoreType.DMA((3, npc)), pltpu.SemaphoreType.DMA((3, npc))])(*shards)


def _pair_exchange(parts, common):
    n = len(parts)
    table = _piece_table([p.shape for p in parts] + [common.shape])
    npc = len(table)

    def body(*refs):
        ins, outs = refs[:n + 1], refs[n + 1:2 * n + 2]
        send_sems, recv_sems = refs[2 * n + 2:]
        x, y, c = _mesh_pos()
        sent = []
        for k in range(4):
            for q, (a, s, sz) in enumerate(table):
                if a == n and k > 0:
                    continue
                src = ins[a].at[2 * k + 1 - c, pl.ds(s, sz)] if a < n else ins[a].at[pl.ds(s, sz)]
                dst = outs[a].at[k, pl.ds(s, sz)] if a < n else outs[a].at[pl.ds(s, sz)]
                sent.append(_remote(src, dst, send_sems.at[k, q], recv_sems.at[k, q], (x, y, 1 - c)))
                sent[-1].start()
        for k in range(4):
            for q, (a, s, sz) in enumerate(table):
                if a == n and k > 0:
                    continue
                dst = outs[a].at[k, pl.ds(s, sz)] if a < n else outs[a].at[pl.ds(s, sz)]
                _remote(dst, dst, send_sems.at[k, q], recv_sems.at[k, q], (x, y, c)).wait_recv()
        for cp in sent:
            cp.wait_send()

    return pl.pallas_call(
        body, name="grad_pair_exchange", in_specs=[ANY] * (n + 1), out_specs=[ANY] * (n + 1),
        out_shape=[jax.ShapeDtypeStruct((4,) + p.shape[1:], p.dtype) for p in parts]
        + [jax.ShapeDtypeStruct(common.shape, common.dtype)],
        scratch_shapes=[pltpu.SemaphoreType.DMA((4, npc)), pltpu.SemaphoreType.DMA((4, npc))])(*parts, common)


def _chip_exchange(parts, common):
    n = len(parts)
    table = _piece_table([p.shape for p in parts] + [common.shape])
    npc = len(table)

    def body(*refs):
        ins, outs = refs[:n + 1], refs[n + 1:2 * n + 2]
        send_sems, recv_sems = refs[2 * n + 2:]
        x, y, c = _mesh_pos()
        mine = 2 * x + y
        chips = [(1 - x, y), (x, 1 - y), (1 - x, 1 - y)]
        src = lambda a, k: ins[a].at[k] if a < n else ins[a]
        sent = []
        for j, (cx, cy) in enumerate(chips):
            for q, (a, s, sz) in enumerate(table):
                sent.append(_remote(src(a, 2 * cx + cy).at[pl.ds(s, sz)], outs[a].at[mine, pl.ds(s, sz)],
                                    send_sems.at[j, q], recv_sems.at[j, q], (cx, cy, c)))
                sent[-1].start()
        for j, (cx, cy) in enumerate(chips):
            for q, (a, s, sz) in enumerate(table):
                dst = outs[a].at[2 * cx + cy, pl.ds(s, sz)]
                _remote(dst, dst, send_sems.at[j, q], recv_sems.at[j, q], (x, y, c)).wait_recv()
        for cp in sent:
            cp.wait_send()

    return pl.pallas_call(
        body, name="grad_chip_exchange", in_specs=[ANY] * (n + 1), out_specs=[ANY] * (n + 1),
        out_shape=[jax.ShapeDtypeStruct(p.shape, p.dtype) for p in parts]
        + [jax.ShapeDtypeStruct((4,) + common.shape, common.dtype)],
        scratch_shapes=[pltpu.SemaphoreType.DMA((3, npc)), pltpu.SemaphoreType.DMA((3, npc))])(*parts, common)


def _sibling_exchange(sums):
    n = len(sums)
    table = _piece_table([s.shape for s in sums])
    npc = len(table)

    def body(*refs):
        ins, outs = refs[:n], refs[n:2 * n]
        send_sems, recv_sems = refs[2 * n:]
        x, y, c = _mesh_pos()
        sent = []
        for q, (a, s, sz) in enumerate(table):
            sent.append(_remote(ins[a].at[pl.ds(s, sz)], outs[a].at[pl.ds(s, sz)], send_sems.at[q], recv_sems.at[q],
                                (x, y, 1 - c)))
            sent[-1].start()
        for q, (a, s, sz) in enumerate(table):
            dst = outs[a].at[pl.ds(s, sz)]
            _remote(dst, dst, send_sems.at[q], recv_sems.at[q], (x, y, c)).wait_recv()
        for cp in sent:
            cp.wait_send()

    return pl.pallas_call(
        body, name="sibling_exchange", in_specs=[ANY] * n, out_specs=[ANY] * n,
        out_shape=[jax.ShapeDtypeStruct(s.shape, s.dtype) for s in sums],
        scratch_shapes=[pltpu.SemaphoreType.DMA((npc,)), pltpu.SemaphoreType.DMA((npc,))])(*sums)


def _pair_sum(parts, got, name, tr):
    _, r, c = got.shape
    tr = min(tr, r)

    def body(p_ref, g_ref, o_ref, w_ref):
        total = p_ref[...] + g_ref[...]
        o_ref[...] = total
        w_ref[...] = total.astype(WIRE_DTYPE)

    blk = pl.BlockSpec((None, tr, c), lambda k, i: (k, i, 0))
    mine = pl.BlockSpec((None, tr, c), lambda k, i: (2 * k + lax.axis_index("c"), i, 0))
    return pl.pallas_call(
        body, name=name, grid=(4, r // tr), in_specs=[mine, blk], out_specs=[blk, blk],
        out_shape=[jax.ShapeDtypeStruct(got.shape, F32), jax.ShapeDtypeStruct(got.shape, WIRE_DTYPE)],
        compiler_params=_cp(2))(parts, got)


def _sum_chips(pair_sums, recv, name, tr):
    _, r, c = pair_sums.shape
    tr = min(tr, r)

    def body(own_ref, r_ref, o_ref):
        chip = 2 * lax.axis_index("x") + lax.axis_index("y")
        own_blk = own_ref[...]
        acc = jnp.where(chip == 0, own_blk, r_ref[0].astype(F32))
        for k in range(1, 4):
            acc = acc + jnp.where(chip == k, own_blk, r_ref[k].astype(F32))
        o_ref[...] = acc

    return pl.pallas_call(
        body, name=name, grid=(r // tr,),
        in_specs=[pl.BlockSpec((None, tr, c), lambda i: (2 * lax.axis_index("x") + lax.axis_index("y"), i, 0)),
                  pl.BlockSpec((4, tr, c), lambda i: (0, i, 0))],
        out_specs=pl.BlockSpec((tr, c), lambda i: (i, 0)),
        out_shape=jax.ShapeDtypeStruct((r, c), F32), compiler_params=_cp())(pair_sums, recv)


def _sum_leading(parts, name, tr):
    nlead, r, c = parts.shape
    tr = min(tr, r)

    def body(p_ref, o_ref):
        acc = p_ref[0]
        for j in range(1, nlead):
            acc = acc + p_ref[j]
        o_ref[...] = acc

    return pl.pallas_call(
        body, name=name, grid=(r // tr,),
        in_specs=[pl.BlockSpec((nlead, tr, c), lambda i: (0, i, 0))], out_specs=pl.BlockSpec((tr, c), lambda i: (i, 0)),
        out_shape=jax.ShapeDtypeStruct((r, c), parts.dtype), compiler_params=_cp())(parts)


def _adamw_update(w_ref, g_ref, m_ref, v_ref, d_ref, nm_ref, nv_ref):
    gg = g_ref[...]
    nm = ADAM_B1 * m_ref[...] + (1.0 - ADAM_B1) * gg
    nv = ADAM_B2 * v_ref[...] + (1.0 - ADAM_B2) * jnp.square(gg)
    m_hat = nm / (1.0 - ADAM_B1 ** ADAM_STEP)
    v_hat = nv / (1.0 - ADAM_B2 ** ADAM_STEP)
    d_ref[...] = -ADAM_LR * (m_hat / (jnp.sqrt(v_hat) + ADAM_EPS) + ADAM_WD * w_ref[...])
    nm_ref[...] = nm
    nv_ref[...] = nv


def _adamw(w, g, m, v, name, tr):
    r, c = w.shape
    tr = min(tr, r)

    def body(w_ref, g_ref, m_ref, v_ref, d_ref, nm_ref, nv_ref):
        _adamw_update(w_ref, g_ref, m_ref, v_ref, d_ref, nm_ref, nv_ref)

    blk = pl.BlockSpec((tr, c), lambda i: (i, 0))
    return pl.pallas_call(
        body, name=name, grid=(r // tr,), in_specs=[blk] * 4, out_specs=[blk] * 3,
        out_shape=[jax.ShapeDtypeStruct((r, c), F32)] * 3, compiler_params=_cp())(w, g, m, v)


def _adamw_many(ws, gs, ms, vs, name):
    n = len(ws)

    def body(*refs):
        ins, outs = refs[:4 * n], refs[4 * n:]
        for k in range(n):
            _adamw_update(ins[k], ins[n + k], ins[2 * n + k], ins[3 * n + k], outs[k], outs[n + k], outs[2 * n + k])

    vmem = pl.BlockSpec(memory_space=pltpu.VMEM)
    res = pl.pallas_call(
        body, name=name, in_specs=[vmem] * (4 * n), out_specs=[vmem] * (3 * n),
        out_shape=[jax.ShapeDtypeStruct(a.shape, F32) for a in ws] * 3)(*ws, *gs, *ms, *vs)
    return res[:n], res[n:2 * n], res[2 * n:]


def _rope_tables(positions):
    inv_freq = ROPE_BASE ** (-jnp.arange(0, 64, 2, dtype=F32) / 64)
    ang = positions.astype(F32)[:, None] * inv_freq
    cos, sin = jnp.cos(ang), jnp.sin(ang)
    t = positions.shape[0]
    rc = jnp.concatenate([jnp.ones((t, 128), F32), cos, cos, jnp.ones((t, 64), F32)], axis=1)
    rs = jnp.concatenate([jnp.zeros((t, 128), F32), sin, sin, jnp.zeros((t, 64), F32)], axis=1)
    return rc, rs


def _layer_params(l, w_in, w_uq, w_ukv, w_out, small):
    p = {}
    p["w_in"] = jnp.concatenate([w_in[l][:, :1984], jnp.zeros((1024, 64), w_in.dtype), w_in[l][:, 1984:]], axis=1)
    p["w_uq"] = jnp.pad(w_uq[l].reshape(384, 4, 192), ((0, 0), (0, 0), (0, 64))).reshape(384, 1024)
    p["w_ukv"] = w_ukv[l].reshape(256, 4, 2, 128).transpose(0, 2, 1, 3).reshape(256, 1024)
    p["w_out"] = w_out[l]
    p["pre_g"] = small["pre_norm_g"][l][None]
    p["post_g"] = small["post_norm_g"][l][None]
    p["sgu_w"] = small["sgu_w"][l].reshape(512, 128)
    p["sgu_wt"] = small["sgu_w"][l].transpose(0, 2, 1).reshape(512, 128)
    p["sgu_bias"] = jnp.repeat(small["sgu_b"][l].T, 64, axis=1)
    p["ln_g"] = small["sgu_ln_g"][l][None]
    p["ln_b"] = small["sgu_ln_b"][l][None]
    p["pool_wbd"] = _mx(jax.scipy.linalg.block_diag(*[small["pool_w"][l][gi] for gi in range(4)]))
    p["pool_scale"] = small["pool_scale"][l][None]
    p["gq"] = small["q_norm_g"][l][None]
    p["gkv"] = small["kv_norm_g"][l][None]
    return p


def _layer_fwd(l, x, p, rc, rs, target):
    z, h = _in_proj_fwd(x, p["pre_g"], p["w_in"], f"in_proj_fwd_{l}")
    yab = _mix_fwd(z, p["sgu_w"], p["sgu_bias"], p["ln_g"], p["ln_b"], p["pool_wbd"], p["pool_scale"], f"mix_fwd_{l}")
    qh, kh, vh = _qkv_fwd(z, rc, rs, p["w_uq"], p["w_ukv"], p["gq"], p["gkv"], f"qkv_fwd_{l}")
    o, yc, lse = _attn_fwd(qh, kh, vh, z, f"attn_fwd_{l}")
    outs = _out_proj_fwd(yab, yc, p["w_out"], x, p["post_g"], target, f"out_proj_fwd_{l}")
    saved = dict(x=x, z=z, h=h, yab=yab, qh=qh, kh=kh, vh=vh, o=o, yc=yc, lse=lse, y=outs[0])
    return saved, outs[1:]


def _layer_bwd(l, dout, sv, p, rc, rs):
    dycat, dw_out, dpost = _out_proj_bwd(dout, sv["y"], sv["yab"], sv["yc"], p["w_out"], p["post_g"], f"out_proj_bwd_{l}")
    dq, dgate, dk, dv = _attn_bwd(sv["qh"], sv["kh"], sv["vh"], sv["o"], sv["lse"], dycat, sv["z"], f"attn_bwd_{l}")
    dzc, dzk, dwq, dwkv, dgq, dgkv = _qkv_bwd(dq, dk, dv, sv["z"], rc, rs, p["w_uq"], p["w_ukv"], p["gq"], p["gkv"],
                                              f"qkv_bwd_{l}")
    dzm, dsw, dsb, dlng, dlnb, dpw, dps = _mix_bwd(sv["z"], dycat, p["sgu_w"], p["sgu_wt"], p["sgu_bias"], p["ln_g"],
                                                   p["ln_b"], p["pool_wbd"], p["pool_scale"], f"mix_bwd_{l}")
    dx, dw_in, dpre = _in_proj_bwd(dzm, dzc, dzk, dgate, sv["h"], sv["x"], dout, p["w_in"], p["pre_g"], f"in_proj_bwd_{l}")
    grads = {
        "pre_norm_g": dpre[0], "post_norm_g": dpost[0],
        "w_in": jnp.concatenate([dw_in[:, :1984], dw_in[:, 2048:]], axis=1),
        "sgu_w": dsw.reshape(4, 128, 128), "sgu_b": dsb[:, :4].T, "sgu_ln_g": dlng[0], "sgu_ln_b": dlnb[0],
        "pool_w": jnp.stack([dpw[64 * gi:64 * gi + 64, 64 * gi:64 * gi + 64] for gi in range(4)]),
        "pool_scale": dps[0], "q_norm_g": dgq[0],
        "w_uq": dwq.reshape(384, 4, 256)[:, :, :192].reshape(384, 768), "kv_norm_g": dgkv[0],
        "w_ukv": dwkv.reshape(256, 2, 4, 128).transpose(0, 2, 1, 3).reshape(256, 1024), "w_out": dw_out,
    }
    return dx, grads


SMALL_NAMES = ["pre_norm_g", "post_norm_g", "sgu_w", "sgu_b", "sgu_ln_g", "sgu_ln_b", "pool_w", "pool_scale",
               "q_norm_g", "kv_norm_g"]
BIG_NAMES = ["w_in", "w_uq", "w_ukv", "w_out"]
WEIGHT_NAMES = ["pre_norm_g", "post_norm_g", "w_in", "sgu_w", "sgu_b", "sgu_ln_g", "sgu_ln_b", "pool_w", "pool_scale",
                "q_norm_g", "w_uq", "kv_norm_g", "w_ukv", "w_out"]


def _local_step(x, positions, target, w_in, w_uq, w_ukv, w_out, small):
    rc, rs = _rope_tables(positions)
    params = [_layer_params(l, w_in, w_uq, w_ukv, w_out, small) for l in range(DEPTH)]
    saved = []
    for l in range(DEPTH):
        sv, outs = _layer_fwd(l, x, params[l], rc, rs, target if l == DEPTH - 1 else None)
        saved.append(sv)
        if l < DEPTH - 1:
            x = outs[0]
    dout, loss = outs
    grads = [None] * DEPTH
    for l in reversed(range(DEPTH)):
        dout, grads[l] = _layer_bwd(l, dout, saved[l], params[l], rc, rs)
    return loss[0, 0], dout, {k: jnp.stack([grads[l][k] for l in range(DEPTH)]) for k in WEIGHT_NAMES}


def _pack_small(tree, extra=None):
    pieces = [tree[k].reshape(-1) for k in SMALL_NAMES]
    pieces.append(jnp.zeros((1,), F32) if extra is None else extra.reshape(1))
    flat = jnp.concatenate(pieces)
    rows = -(-flat.shape[0] // 1024) * 8
    return jnp.pad(flat, (0, rows * 128 - flat.shape[0])).reshape(rows, 128)


def _unpack_small(packed, like):
    flat = packed.reshape(-1)
    out, off = {}, 0
    for k in SMALL_NAMES:
        size = like[k].size
        out[k] = flat[off:off + size].reshape(like[k].shape)
        off += size
    return out, flat[off]


def kernel(x, positions, pre_norm_g, post_norm_g, w_in, sgu_w, sgu_b, sgu_ln_g, sgu_ln_b, pool_w, pool_scale, q_norm_g, w_uq, kv_norm_g, w_ukv, w_out, loss_target, m_pre_norm_g, m_post_norm_g, m_w_in, m_sgu_w, m_sgu_b, m_sgu_ln_g, m_sgu_ln_b, m_pool_w, m_pool_scale, m_q_norm_g, m_w_uq, m_kv_norm_g, m_w_ukv, m_w_out, v_pre_norm_g, v_post_norm_g, v_w_in, v_sgu_w, v_sgu_b, v_sgu_ln_g, v_sgu_ln_b, v_pool_w, v_pool_scale, v_q_norm_g, v_w_uq, v_kv_norm_g, v_w_ukv, v_w_out):
    w = dict(pre_norm_g=pre_norm_g, post_norm_g=post_norm_g, w_in=w_in, sgu_w=sgu_w, sgu_b=sgu_b, sgu_ln_g=sgu_ln_g,
             sgu_ln_b=sgu_ln_b, pool_w=pool_w, pool_scale=pool_scale, q_norm_g=q_norm_g, w_uq=w_uq, kv_norm_g=kv_norm_g,
             w_ukv=w_ukv, w_out=w_out)
    m = dict(pre_norm_g=m_pre_norm_g, post_norm_g=m_post_norm_g, w_in=m_w_in, sgu_w=m_sgu_w, sgu_b=m_sgu_b,
             sgu_ln_g=m_sgu_ln_g, sgu_ln_b=m_sgu_ln_b, pool_w=m_pool_w, pool_scale=m_pool_scale, q_norm_g=m_q_norm_g,
             w_uq=m_w_uq, kv_norm_g=m_kv_norm_g, w_ukv=m_w_ukv, w_out=m_w_out)
    v = dict(pre_norm_g=v_pre_norm_g, post_norm_g=v_post_norm_g, w_in=v_w_in, sgu_w=v_sgu_w, sgu_b=v_sgu_b,
             sgu_ln_g=v_sgu_ln_g, sgu_ln_b=v_sgu_ln_b, pool_w=v_pool_w, pool_scale=v_pool_scale, q_norm_g=v_q_norm_g,
             w_uq=v_w_uq, kv_norm_g=v_kv_norm_g, w_ukv=v_w_ukv, w_out=v_w_out)

    core = lax.axis_index("c")
    chip = 2 * lax.axis_index("x") + lax.axis_index("y")
    shards = [_mx(w[k]) for k in BIG_NAMES]
    gathered = _gather_weights(shards)
    g_in, g_uq, g_ukv, g_out = [lax.dynamic_update_slice(g, s, (2 * chip, 0, 0)) for g, s in zip(gathered, shards)]
    cols = lambda g: g.reshape((4, 2) + g.shape[1:]).transpose(1, 2, 0, 3).reshape(2, g.shape[1], 4 * g.shape[2])
    full_out = g_out.reshape(4, 2, 256, 1024).transpose(1, 0, 2, 3).reshape(2, 1024, 1024)
    loss, dx, grads = _local_step(x[0], positions[0], loss_target[0], cols(g_in), cols(g_uq), cols(g_ukv), full_out, w)

    split_cols = lambda g: g.reshape(2, g.shape[1], 4, g.shape[2] // 4).transpose(2, 0, 1, 3).reshape(8, g.shape[1], g.shape[2] // 4)
    parts = [split_cols(grads["w_in"]), split_cols(grads["w_uq"]), split_cols(grads["w_ukv"]),
             grads["w_out"].reshape(2, 4, 256, 1024).transpose(1, 0, 2, 3).reshape(8, 256, 1024)]
    common = _pack_small(grads, loss)
    got = _pair_exchange(parts, common)
    pair_sums = [_pair_sum(parts[a], got[a], f"pair_sum_{BIG_NAMES[a]}", 128) for a in range(4)]
    chip_common = _sum_leading(jnp.stack([common, got[4]]), "pair_sum_small", common.shape[0])
    received = _chip_exchange([ps[1] for ps in pair_sums], chip_common)
    sums = [_sum_chips(pair_sums[a][0], received[a], f"sum_{BIG_NAMES[a]}", 128) for a in range(4)]
    all_common = lax.dynamic_update_slice(received[4], chip_common[None], (chip, 0, 0))
    small_sum, loss = _unpack_small(_sum_leading(all_common, "sum_small", all_common.shape[1]), w)
    others = _sibling_exchange(sums)
    total = dict(small_sum)
    for a, k in enumerate(BIG_NAMES):
        total[k] = jnp.where(core == 0, jnp.stack([sums[a], others[a]]), jnp.stack([others[a], sums[a]]))

    rows2d = lambda a: a.reshape(-1, a.shape[-1])
    small_out = _adamw_many(*[[rows2d(tree[k]) for k in SMALL_NAMES] for tree in (w, total, m, v)], "adamw_small")
    delta, new_m, new_v = ({k: r.reshape(w[k].shape) for k, r in zip(SMALL_NAMES, res)} for res in small_out)
    for k in BIG_NAMES:
        shape = w[k].shape
        flat = lambda a: a.reshape(shape[0] * shape[1], shape[2])
        res = _adamw(flat(w[k]), flat(total[k]), flat(m[k]), flat(v[k]), f"adamw_{k}", 256)
        delta[k], new_m[k], new_v[k] = (r.reshape(shape) for r in res)

    return (loss, dx[None], *[total[k] for k in WEIGHT_NAMES], *[delta[k] for k in WEIGHT_NAMES],
            *[new_m[k] for k in WEIGHT_NAMES], *[new_v[k] for k in WEIGHT_NAMES])
```

```python
import jax
import jax.numpy as jnp
from jax import lax
from jax.experimental import pallas as pl
from jax.experimental.pallas import tpu as pltpu

F32 = jnp.float32
MXU_DTYPE = jnp.bfloat16
WIRE_DTYPE = jnp.bfloat16
EPS = 1e-6
NEG_INF = -1e30
CHUNK = 64
DEPTH = 2
N_HEADS = 4
QK_PAD = 256
V_DIM = 128
SCALE = 192 ** -0.5
LOG2E = 1.4426950408889634
ROPE_BASE = 10000.0
ADAM_LR, ADAM_B1, ADAM_B2, ADAM_EPS, ADAM_WD, ADAM_STEP = 0.001, 0.9, 0.999, 1e-08, 0.01, 10
VMEM_LIMIT_BYTES = 56 * 1024 * 1024
MESH = pl.DeviceIdType.MESH
ANY = pl.BlockSpec(memory_space=pl.ANY)

Z_MIX, Z_C, Z_KR, Z_GATE = 1280, 640, 128, 512
Z_W = Z_MIX + Z_C + Z_KR + Z_GATE


def _cp(n_axes=1):
    return pltpu.CompilerParams(dimension_semantics=("arbitrary",) * n_axes, vmem_limit_bytes=VMEM_LIMIT_BYTES)


def _dot(a, b):
    return lax.dot_general(a, b, (((1,), (0,)), ((), ())), preferred_element_type=F32)


def _dot_nt(a, b):
    return lax.dot_general(a, b, (((1,), (1,)), ((), ())), preferred_element_type=F32)


def _dot_tn(a, b):
    return lax.dot_general(a, b, (((0,), (0,)), ((), ())), preferred_element_type=F32)


def _mx(a):
    return a.astype(MXU_DTYPE)


def _silu_and_grad(g):
    sg = jax.nn.sigmoid(g)
    return g * sg, sg * (1.0 + g * (1.0 - sg))


def _rms(x, g):
    r = lax.rsqrt(jnp.mean(x * x, axis=-1, keepdims=True) + EPS)
    return x * r * g, r


def _rms_bwd(x, r, g, dy):
    xhat = x * r
    dyg = dy * g
    dx = r * (dyg - xhat * jnp.mean(dyg * xhat, axis=-1, keepdims=True))
    return dx, dy * xhat


def _zero_when(first, *refs):
    @pl.when(first)
    def _():
        for ref in refs:
            ref[...] = jnp.zeros(ref.shape, ref.dtype)


def _acc(ref, val):
    ref[...] += val


def _colsum(a):
    return jnp.sum(a, axis=0, keepdims=True)


def _in_proj_fwd(x, g, w, name, tm=1024):
    t, d = x.shape
    n = w.shape[1]
    tm = min(tm, t)

    def body(x_ref, g_ref, w_ref, z_ref, h_ref):
        hs = []
        for rows in _halves(tm):
            h, _ = _rms(x_ref[rows, :], g_ref[...])
            hs.append(_mx(h))
            h_ref[rows, :] = hs[-1]
        for rows, h in zip(_halves(tm), hs):
            z_ref[rows, :] = _dot(h, w_ref[...])

    return pl.pallas_call(
        body, name=name, grid=(t // tm,),
        in_specs=[pl.BlockSpec((tm, d), lambda i: (i, 0)), pl.BlockSpec((1, d), lambda i: (0, 0)),
                  pl.BlockSpec((d, n), lambda i: (0, 0), pipeline_mode=pl.Buffered(1))],
        out_specs=[pl.BlockSpec((tm, n), lambda i: (i, 0)), pl.BlockSpec((tm, d), lambda i: (i, 0))],
        out_shape=[jax.ShapeDtypeStruct((t, n), F32), jax.ShapeDtypeStruct((t, d), MXU_DTYPE)],
        compiler_params=_cp())(x, g, w)


def _in_proj_bwd(dz_mix, dz_c, dz_kr, dz_gate, h, x, d_res, w, g, name, tm=512):
    t, d = x.shape
    n = w.shape[1]
    tm = min(tm, t)

    def body(dm_ref, dc_ref, dk_ref, dg_ref, h_ref, x_ref, dres_ref, w_ref, g_ref, dx_ref, dw_ref, dgn_ref):
        first = pl.program_id(0) == 0
        dz = jnp.concatenate([dm_ref[...], dc_ref[...], dk_ref[...], dg_ref[...]], axis=1)

        _zero_when(first, dw_ref, dgn_ref)
        dh = _dot_nt(dz, w_ref[...])
        hb = h_ref[...]
        for c0 in range(0, n, 512):
            dw_ref[:, c0:c0 + 512] += _dot_tn(hb, dz[:, c0:c0 + 512])
        xf = x_ref[...]
        r = lax.rsqrt(jnp.mean(xf * xf, axis=-1, keepdims=True) + EPS)
        dx, dgt = _rms_bwd(xf, r, g_ref[...], dh)
        dx_ref[...] = dx + dres_ref[...]
        _acc(dgn_ref, _colsum(dgt))

    row = lambda wd: pl.BlockSpec((tm, wd), lambda i: (i, 0))
    fixed = lambda a, b: pl.BlockSpec((a, b), lambda i: (0, 0), pipeline_mode=pl.Buffered(1))
    return pl.pallas_call(
        body, name=name, grid=(t // tm,),
        in_specs=[row(Z_MIX), row(Z_C), row(Z_KR), row(Z_GATE), row(d), row(d), row(d), fixed(d, n), fixed(1, d)],
        out_specs=[row(d), fixed(d, n), fixed(1, d)],
        out_shape=[jax.ShapeDtypeStruct((t, d), F32), jax.ShapeDtypeStruct((d, n), F32),
                   jax.ShapeDtypeStruct((1, d), F32)],
        compiler_params=_cp())(dz_mix, dz_c, dz_kr, dz_gate, h, x, d_res, w, g)


def _lane_group(shape):
    return lax.broadcasted_iota(jnp.int32, shape, 1) // 64


def _select_group(vals):
    grp = _lane_group(vals[0].shape)
    out = vals[3]
    for gi in (2, 1, 0):
        out = jnp.where(grp == gi, vals[gi], out)
    return out


def _sgu_mask(transposed):
    r = (lax.broadcasted_iota(jnp.int32, (512, 128), 0) % 128) // CHUNK
    c = lax.broadcasted_iota(jnp.int32, (512, 128), 1) // CHUNK
    return (r <= c) if transposed else (c <= r)


def _sgu_apply(wstack, vb, nblk):
    outs = []
    for n in range(nblk):
        r = _dot(wstack, vb[n * 128:(n + 1) * 128, :])
        outs.append(_select_group([r[hh * 128:(hh + 1) * 128, :] for hh in range(4)]))
    return jnp.concatenate(outs, axis=0)


def _layer_norm(v, g, b):
    mu = jnp.mean(v, axis=-1, keepdims=True)
    vc = v - mu
    rstd = lax.rsqrt(jnp.mean(vc * vc, axis=-1, keepdims=True) + EPS)
    vhat = vc * rstd
    return vhat * g + b, vhat, rstd


def _div_by_counts(x, t0):
    t = t0 + lax.broadcasted_iota(jnp.int32, (16, 256), 0)
    w = _select_group([jnp.full((16, 256), wv, jnp.int32) for wv in (2, 4, 8, 16)])
    head = x[:16, :] / jnp.minimum(t + 1, w).astype(F32)
    inv_w = _select_group([jnp.full((1, 256), 1.0 / wv, F32) for wv in (2, 4, 8, 16)])
    return jnp.concatenate([head, x[16:, :] * inv_w], axis=0)


def _pooled(p, halo, t0):
    tm = p.shape[0]
    ext = jnp.concatenate([halo, p], axis=0)
    s2 = ext + pltpu.roll(ext, 1, 0)
    s4 = s2 + pltpu.roll(s2, 2, 0)
    s8 = s4 + pltpu.roll(s4, 4, 0)
    s16 = s8 + pltpu.roll(s8, 8, 0)
    sel = _select_group([s2, s4, s8, s16])[16:, :]
    return _div_by_counts(sel, t0) - p


def _pooled_bwd(dpool, dpool_halo, t0):
    tm = dpool.shape[0]
    n = tm + 16
    ext = _div_by_counts(jnp.concatenate([dpool, dpool_halo], axis=0), t0)
    f2 = ext + pltpu.roll(ext, n - 1, 0)
    f4 = f2 + pltpu.roll(f2, n - 2, 0)
    f8 = f4 + pltpu.roll(f4, n - 4, 0)
    f16 = f8 + pltpu.roll(f8, n - 8, 0)
    return _select_group([f2, f4, f8, f16])[:tm, :] - dpool


def _mix_specs(t, tm):
    nt16 = t // 16
    zrow = pl.BlockSpec((tm, Z_MIX), lambda i: (i, 0))
    prev_halo = pl.BlockSpec((16, 256), lambda i: (jnp.maximum(i * (tm // 16) - 1, 0), 3))
    fixed = lambda a, b: pl.BlockSpec((a, b), lambda i: (0, 0))
    params = [fixed(512, 128), fixed(128, 256), fixed(1, 256), fixed(1, 256), fixed(256, 256), fixed(1, 256)]
    return nt16, zrow, prev_halo, fixed, params


def _mix_fwd(z, sgu_w, sgu_bias, ln_g, ln_b, pool_wbd, pool_scale, name, tm=512):
    t = z.shape[0]
    tm = min(tm, t)
    _, zrow, prev_halo, _, params = _mix_specs(t, tm)

    def body(z_ref, halo_ref, w_ref, bias_ref, lng_ref, lnb_ref, pw_ref, ps_ref, y_ref):
        i = pl.program_id(0)
        u, v, gate = z_ref[:, 0:256], z_ref[:, 256:512], z_ref[:, 512:768]
        p, pgate = z_ref[:, 768:1024], z_ref[:, 1024:1280]
        vn, _, _ = _layer_norm(v, lng_ref[...], lnb_ref[...])
        wm = _mx(jnp.where(_sgu_mask(False), w_ref[...], 0.0))
        halo = jnp.where(i > 0, halo_ref[...], 0.0)
        pooled = _pooled(p, halo, i * tm)
        mixed = _sgu_apply(wm, _mx(vn), tm // 128) + jnp.tile(bias_ref[...], (tm // 128, 1))
        mixedp = _dot(_mx(pooled), pw_ref[...])
        ya = u * mixed * _silu_and_grad(gate)[0]
        yb = mixedp * ps_ref[...] * _silu_and_grad(pgate)[0]
        y_ref[...] = _mx(jnp.concatenate([ya, yb], axis=1))

    return pl.pallas_call(
        body, name=name, grid=(t // tm,),
        in_specs=[zrow, prev_halo] + params,
        out_specs=pl.BlockSpec((tm, 512), lambda i: (i, 0)),
        out_shape=jax.ShapeDtypeStruct((t, 512), MXU_DTYPE),
        compiler_params=_cp())(z, z, sgu_w, sgu_bias, ln_g, ln_b, pool_wbd, pool_scale)


def _mix_bwd(z, dycat, sgu_w, sgu_wt, sgu_bias, ln_g, ln_b, pool_wbd, pool_scale, name, tm=512):
    t = z.shape[0]
    tm = min(tm, t)
    nt16, zrow, prev_halo, fixed, params = _mix_specs(t, tm)
    nblk = tm // 128
    last = t // tm - 1

    def body(z_ref, halo_ref, zn_ref, dy_ref, dyn_ref, w_ref, wt_ref, bias_ref, lng_ref, lnb_ref, pw_ref, ps_ref,
             dz_ref, dw_ref, db_ref, dlng_ref, dlnb_ref, dpw_ref, dps_ref):
        i = pl.program_id(0)
        _zero_when(i == 0, dw_ref, db_ref, dlng_ref, dlnb_ref, dpw_ref, dps_ref)
        u, v, gate = z_ref[:, 0:256], z_ref[:, 256:512], z_ref[:, 512:768]
        p, pgate = z_ref[:, 768:1024], z_ref[:, 1024:1280]
        dya, dyb = dy_ref[:, 0:256], dy_ref[:, 256:512]
        vn, vhat, rstd = _layer_norm(v, lng_ref[...], lnb_ref[...])
        vnb = _mx(vn)
        wm = _mx(jnp.where(_sgu_mask(False), w_ref[...], 0.0))
        wmt = _mx(jnp.where(_sgu_mask(True), wt_ref[...], 0.0))
        halo = jnp.where(i > 0, halo_ref[...], 0.0)
        pooled_b = _mx(_pooled(p, halo, i * tm))
        silu, dsilu = _silu_and_grad(gate)
        psilu, pdsilu = _silu_and_grad(pgate)
        mixed = _sgu_apply(wm, vnb, nblk) + jnp.tile(bias_ref[...], (nblk, 1))
        mixedp = _dot(pooled_b, pw_ref[...])
        t1 = u * mixed
        d_gate = dya * t1 * dsilu
        d_t1 = dya * silu
        d_u = d_t1 * mixed
        d_mixed = d_t1 * u
        dmb = _mx(d_mixed)
        d_pgate = dyb * (mixedp * ps_ref[...]) * pdsilu
        d_ms = dyb * psilu
        _acc(dps_ref, _colsum(d_ms * mixedp))
        dmpb = _mx(d_ms * ps_ref[...])
        dmp_halo = _mx(dyn_ref[...] * _silu_and_grad(zn_ref[...])[0] * ps_ref[...])
        d_vn = _sgu_apply(wmt, dmb, nblk)
        grp = _lane_group((128, 256))
        lane = lax.broadcasted_iota(jnp.int32, (128, 128), 1)
        dws = [jnp.zeros((128, 128), F32) for _ in range(4)]
        dbias = jnp.zeros((128, 128), F32)
        for n in range(nblk):
            dm_n, dmb_n, vnb_n = d_mixed[n * 128:(n + 1) * 128], dmb[n * 128:(n + 1) * 128], vnb[n * 128:(n + 1) * 128]
            for hh in range(4):
                dws[hh] = dws[hh] + _dot_nt(jnp.where(grp == hh, dmb_n, jnp.zeros_like(dmb_n)), vnb_n)
                rs = jnp.sum(jnp.where(grp == hh, dm_n, 0.0), axis=-1, keepdims=True)
                dbias = dbias + jnp.where(lane == hh, rs, 0.0)
        _acc(dw_ref, jnp.concatenate(dws, axis=0))
        _acc(db_ref, dbias)
        _acc(dpw_ref, _dot_tn(pooled_b, dmpb))
        d_pooled = _dot_nt(dmpb, pw_ref[...])
        d_pooled_halo = jnp.where(i < last, _dot_nt(dmp_halo, pw_ref[...]), 0.0)
        _acc(dlng_ref, _colsum(d_vn * vhat))
        _acc(dlnb_ref, _colsum(d_vn))
        dvh = d_vn * lng_ref[...]
        d_v = rstd * (dvh - jnp.mean(dvh, axis=-1, keepdims=True) - vhat * jnp.mean(dvh * vhat, axis=-1, keepdims=True))
        d_p = _pooled_bwd(d_pooled, d_pooled_halo, i * tm)
        dz_ref[...] = _mx(jnp.concatenate([d_u, d_v, d_gate, d_p, d_pgate], axis=1))

        @pl.when(i == last)
        def _():
            dw_ref[...] = jnp.where(_sgu_mask(False), dw_ref[...], 0.0)

    nxt = lambda i: jnp.minimum((i + 1) * (tm // 16), nt16 - 1)
    return pl.pallas_call(
        body, name=name, grid=(t // tm,),
        in_specs=[zrow, prev_halo, pl.BlockSpec((16, 256), lambda i: (nxt(i), 4)),
                  pl.BlockSpec((tm, 512), lambda i: (i, 0)), pl.BlockSpec((16, 256), lambda i: (nxt(i), 1)),
                  params[0], fixed(512, 128)] + params[1:],
        out_specs=[pl.BlockSpec((tm, Z_MIX), lambda i: (i, 0)), fixed(512, 128), fixed(128, 128), fixed(1, 256),
                   fixed(1, 256), fixed(256, 256), fixed(1, 256)],
        out_shape=[jax.ShapeDtypeStruct((t, Z_MIX), MXU_DTYPE), jax.ShapeDtypeStruct((512, 128), F32),
                   jax.ShapeDtypeStruct((128, 128), F32), jax.ShapeDtypeStruct((1, 256), F32),
                   jax.ShapeDtypeStruct((1, 256), F32), jax.ShapeDtypeStruct((256, 256), F32),
                   jax.ShapeDtypeStruct((1, 256), F32)],
        compiler_params=_cp())(z, z, z, dycat, dycat, sgu_w, sgu_wt, sgu_bias, ln_g, ln_b, pool_wbd, pool_scale)


def _rot_half(x, transpose):
    w = x.shape[1]
    lane = lax.broadcasted_iota(jnp.int32, x.shape, 1) % min(w, 256)
    base = 128 if w >= 256 else 0
    lo = jnp.logical_and(lane >= base, lane < base + 32)
    hi = jnp.logical_and(lane >= base + 32, lane < base + 64)
    up = pltpu.roll(x, w - 32, 1)
    down = pltpu.roll(x, 32, 1)
    if transpose:
        return jnp.where(lo, up, jnp.where(hi, -down, 0.0))
    return jnp.where(lo, -up, jnp.where(hi, down, 0.0))


def _rope(x, c, s):
    return x * c + _rot_half(x, False) * s


def _rope_bwd(dy, c, s):
    return dy * c + _rot_half(dy * s, True)


def _qkv_fwd(z, rc, rs, w_uq, w_ukv, gq, gkv, name, tm=512):
    t = z.shape[0]
    tm = min(tm, t)

    def body(zc_ref, zk_ref, rc_ref, rs_ref, wq_ref, wkv_ref, gq_ref, gkv_ref, q_ref, k_ref, v_ref):
        cq, ckv = zc_ref[:, 0:384], zc_ref[:, 384:640]
        c, s = rc_ref[...], rs_ref[...]
        qn, _ = _rms(cq, gq_ref[...])
        kvn, _ = _rms(ckv, gkv_ref[...])
        q_pre = _dot(_mx(qn), wq_ref[...])
        kv = _dot(_mx(kvn), wkv_ref[...])
        kpe = _rope(zk_ref[...], c[:, 128:256], s[:, 128:256])
        q = _rope(q_pre, jnp.tile(c, (1, N_HEADS)), jnp.tile(s, (1, N_HEADS)))
        for hh in range(N_HEADS):
            q_ref[hh] = _mx(q[:, hh * QK_PAD:(hh + 1) * QK_PAD])
            k_ref[hh] = _mx(jnp.concatenate([kv[:, hh * 128:(hh + 1) * 128], kpe], axis=1))
            v_ref[hh] = _mx(kv[:, 512 + hh * 128:512 + (hh + 1) * 128])

    fixed = lambda a, b: pl.BlockSpec((a, b), lambda i: (0, 0))
    heads = lambda wd: pl.BlockSpec((N_HEADS, tm, wd), lambda i: (0, i, 0))
    return pl.pallas_call(
        body, name=name, grid=(t // tm,),
        in_specs=[pl.BlockSpec((tm, Z_C), lambda i: (i, Z_MIX // Z_C)),
                  pl.BlockSpec((tm, Z_KR), lambda i: (i, (Z_MIX + Z_C) // Z_KR)),
                  pl.BlockSpec((tm, 256), lambda i: (i, 0)), pl.BlockSpec((tm, 256), lambda i: (i, 0)),
                  fixed(384, 1024), fixed(256, 1024), fixed(1, 384), fixed(1, 256)],
        out_specs=[heads(QK_PAD), heads(QK_PAD), heads(V_DIM)],
        out_shape=[jax.ShapeDtypeStruct((N_HEADS, t, QK_PAD), MXU_DTYPE),
                   jax.ShapeDtypeStruct((N_HEADS, t, QK_PAD), MXU_DTYPE),
                   jax.ShapeDtypeStruct((N_HEADS, t, V_DIM), MXU_DTYPE)],
        compiler_params=_cp())(z, z, rc, rs, w_uq, w_ukv, gq, gkv)


def _qkv_bwd(dq, dk, dv, z, rc, rs, w_uq, w_ukv, gq, gkv, name, tm=512):
    t = z.shape[0]
    tm = min(tm, t)

    def body(dq_ref, dk_ref, dv_ref, zc_ref, rc_ref, rs_ref, wq_ref, wkv_ref, gq_ref, gkv_ref,
             dzc_ref, dzk_ref, dwq_ref, dwkv_ref, dgq_ref, dgkv_ref):
        _zero_when(pl.program_id(0) == 0, dwq_ref, dwkv_ref, dgq_ref, dgkv_ref)
        cq, ckv = zc_ref[:, 0:384], zc_ref[:, 384:640]
        c, s = rc_ref[...], rs_ref[...]
        dkv = _mx(jnp.concatenate([dk_ref[hh][:, 0:128] for hh in range(N_HEADS)]
                                  + [dv_ref[hh] for hh in range(N_HEADS)], axis=1))
        kvn, rkv = _rms(ckv, gkv_ref[...])
        _acc(dwkv_ref, _dot_tn(_mx(kvn), dkv))
        d_kvn = _dot_nt(dkv, wkv_ref[...])
        dq_all = jnp.concatenate([dq_ref[hh] for hh in range(N_HEADS)], axis=1)
        dqp = _mx(_rope_bwd(dq_all, jnp.tile(c, (1, N_HEADS)), jnp.tile(s, (1, N_HEADS))))
        qn, rq = _rms(cq, gq_ref[...])
        dkpe = dk_ref[0][:, 128:256]
        for hh in range(1, N_HEADS):
            dkpe = dkpe + dk_ref[hh][:, 128:256]
        dzk_ref[...] = _mx(_rope_bwd(dkpe, c[:, 128:256], s[:, 128:256]))
        _acc(dwq_ref, _dot_tn(_mx(qn), dqp))
        d_qn = _dot_nt(dqp, wq_ref[...])
        d_ckv, dgkv_t = _rms_bwd(ckv, rkv, gkv_ref[...], d_kvn)
        _acc(dgkv_ref, _colsum(dgkv_t))
        d_cq, dgq_t = _rms_bwd(cq, rq, gq_ref[...], d_qn)
        _acc(dgq_ref, _colsum(dgq_t))
        dzc_ref[...] = _mx(jnp.concatenate([d_cq, d_ckv], axis=1))

    fixed = lambda a, b: pl.BlockSpec((a, b), lambda i: (0, 0))
    heads = lambda wd: pl.BlockSpec((N_HEADS, tm, wd), lambda i: (0, i, 0))
    return pl.pallas_call(
        body, name=name, grid=(t // tm,),
        in_specs=[heads(QK_PAD), heads(QK_PAD), heads(V_DIM), pl.BlockSpec((tm, Z_C), lambda i: (i, Z_MIX // Z_C)),
                  pl.BlockSpec((tm, 256), lambda i: (i, 0)), pl.BlockSpec((tm, 256), lambda i: (i, 0)),
                  fixed(384, 1024), fixed(256, 1024), fixed(1, 384), fixed(1, 256)],
        out_specs=[pl.BlockSpec((tm, Z_C), lambda i: (i, 0)), pl.BlockSpec((tm, Z_KR), lambda i: (i, 0)),
                   fixed(384, 1024), fixed(256, 1024), fixed(1, 384), fixed(1, 256)],
        out_shape=[jax.ShapeDtypeStruct((t, Z_C), MXU_DTYPE), jax.ShapeDtypeStruct((t, Z_KR), MXU_DTYPE),
                   jax.ShapeDtypeStruct((384, 1024), F32), jax.ShapeDtypeStruct((256, 1024), F32),
                   jax.ShapeDtypeStruct((1, 384), F32), jax.ShapeDtypeStruct((1, 256), F32)],
        compiler_params=_cp())(dq, dk, dv, z, rc, rs, w_uq, w_ukv, gq, gkv)


def _loop_in_long_trips(n, body, longest=4):
    def doubled(inner):
        return lambda t, carry: inner(2 * t + 1, inner(2 * t, carry))

    trips = [body]
    while 2 ** (len(trips) - 1) < longest:
        trips.append(doubled(trips[-1]))
    done = 0
    for level in reversed(range(len(trips))):
        size = 2 ** level
        end = n // size
        lax.fori_loop(done, end, trips[level], 0)
        done = 2 * end if level else end


def _head_loads(k_src, v_src, k_dst, v_dst, sems, first_rows):
    t = k_src.shape[0]
    parts = [(0, first_rows)] + ([(first_rows, t - first_rows)] if t > first_rows else [])
    return [pltpu.make_async_copy(src.at[pl.ds(r0, n)], dst.at[pl.ds(r0, n)], sems.at[2 * q + w])
            for q, (r0, n) in enumerate(parts) for w, (src, dst) in enumerate([(k_src, k_dst), (v_src, v_dst)])]


def _init_mask_bias(bias_ref):
    _, tq, tk = bias_ref.shape
    r = lax.broadcasted_iota(jnp.int32, (tq, tk), 0) // CHUNK
    c = lax.broadcasted_iota(jnp.int32, (tq, tk), 1) // CHUNK
    bias_ref[0] = jnp.zeros((tq, tk), F32)
    for d in range(tq // tk):
        bias_ref[1 + d] = jnp.where(c + d * (tk // CHUNK) <= r, 0.0, NEG_INF)


def _gate_block(tq):
    return pl.BlockSpec((tq, 128), lambda h, i: (i, (Z_MIX + Z_C + Z_KR) // 128 + h))


def _attn_fwd(qh, kh, vh, z, name, tq=1024, tk=512):
    t = qh.shape[1]
    tq = min(tq, t)
    tk = min(tk, tq)
    ratio = tq // tk

    def body(q_ref, g_ref, k_hbm, v_hbm, o_ref, yc_ref, lse_ref, k_v, v_v, m_s, acc_s, s_a, s_b, mx_a, mx_b, bias_s,
             sem):
        h, i = pl.program_id(0), pl.program_id(1)

        loads = _head_loads(k_hbm.at[h], v_hbm.at[h], k_v, v_v.at[:, 0:V_DIM], sem, tq)

        @pl.when(i == 0)
        def _():
            for cp in loads:
                cp.start()
            v_v[:, V_DIM:2 * V_DIM] = jnp.ones((t, V_DIM), MXU_DTYPE)
            _init_mask_bias(bias_s)
            for cp in loads[:2]:
                cp.wait()

        @pl.when(i == 1)
        def _():
            for cp in loads[2:]:
                cp.wait()

        q = q_ref[...]
        m_s[...] = jnp.full(m_s.shape, NEG_INF, F32)
        acc_s[...] = jnp.zeros(acc_s.shape, F32)

        last = ratio * (i + 1) - 1

        def keys(j):
            return pl.ds(pl.multiple_of(j * tk, tk), tk)

        def scores(s_ref, mx_ref, j, biased):
            s = _dot_nt(q, k_v[keys(j), :]) * (SCALE * LOG2E)
            if biased:
                s = s + bias_s[jnp.maximum(j - ratio * i + 1, 0)]
            s_ref[...] = s
            mx_ref[...] = jnp.broadcast_to(jnp.max(s, axis=-1, keepdims=True), mx_ref.shape)

        def softmax_pv(s_ref, mx_ref, j):
            m_old = m_s[...]
            m_new = jnp.maximum(m_old, mx_ref[...])
            p = jnp.exp2(s_ref[...] - jnp.tile(m_new, (1, tk // 128)))
            alpha = jnp.exp2(m_old - m_new)
            m_s[...] = m_new
            acc_s[...] = jnp.tile(alpha, (1, 2)) * acc_s[...] + _dot(_mx(p), v_v[keys(j), :])

        scores(s_a, mx_a, 0, True)

        def pair(pp, carry, biased):
            scores(s_b, mx_b, 2 * pp + 1, biased)
            softmax_pv(s_a, mx_a, 2 * pp)
            scores(s_a, mx_a, jnp.minimum(2 * pp + 2, last), biased)
            softmax_pv(s_b, mx_b, 2 * pp + 1)
            return carry

        n_pairs = (last + 1) // 2
        n_plain = jnp.maximum(ratio * i // 2 - 1, 0)
        _loop_in_long_trips(n_plain, lambda pp, carry: pair(pp, carry, False))
        if ratio % 2 == 0:
            _loop_in_long_trips(n_pairs - n_plain - 1, lambda pp, carry: pair(n_plain + pp, carry, True))
            scores(s_b, mx_b, last, True)
            softmax_pv(s_a, mx_a, last - 1)
            softmax_pv(s_b, mx_b, last)
        else:
            _loop_in_long_trips(n_pairs - n_plain, lambda pp, carry: pair(n_plain + pp, carry, True))

            @pl.when(last % 2 == 0)
            def _():
                softmax_pv(s_a, mx_a, last)

        l = acc_s[:, V_DIM:2 * V_DIM]
        o = acc_s[:, 0:V_DIM] / l
        o_ref[...] = o
        yc_ref[...] = _mx(o * _silu_and_grad(g_ref[...])[0])
        lse_ref[...] = m_s[...] + jnp.log2(l)

    return pl.pallas_call(
        body, name=name, grid=(N_HEADS, t // tq),
        in_specs=[pl.BlockSpec((None, tq, QK_PAD), lambda h, i: (h, i, 0)), _gate_block(tq), ANY, ANY],
        out_specs=[pl.BlockSpec((tq, 128), lambda h, i: (i, h)), pl.BlockSpec((tq, 128), lambda h, i: (i, h)),
                   pl.BlockSpec((None, tq, 128), lambda h, i: (h, i, 0))],
        out_shape=[jax.ShapeDtypeStruct((t, N_HEADS * V_DIM), F32), jax.ShapeDtypeStruct((t, N_HEADS * V_DIM), MXU_DTYPE),
                   jax.ShapeDtypeStruct((N_HEADS, t, 128), F32)],
        scratch_shapes=[pltpu.VMEM((t, QK_PAD), MXU_DTYPE), pltpu.VMEM((t, 2 * V_DIM), MXU_DTYPE),
                        pltpu.VMEM((tq, 128), F32), pltpu.VMEM((tq, 2 * V_DIM), F32),
                        pltpu.VMEM((tq, tk), F32), pltpu.VMEM((tq, tk), F32), pltpu.VMEM((tq, 128), F32),
                        pltpu.VMEM((tq, 128), F32), pltpu.VMEM((ratio + 1, tq, tk), F32),
                        pltpu.SemaphoreType.DMA((4,))],
        compiler_params=_cp(2))(qh, z, kh, vh)


def _attn_bwd(qh, kh, vh, o, lse, dycat, z, name, tq=512):
    t = qh.shape[1]
    tq = min(tq, t)
    nq = t // tq

    def body(q_ref, o_ref, lse_ref, dy_ref, g_ref, k_hbm, v_hbm, dq_ref, dgate_ref, dk_hbm, dv_hbm,
             k_v, v_v, dk_acc, dv_acc, dq_acc, delta_s, s_a, dp_a, s_b, dp_b, bias_s, sem):
        h, i = pl.program_id(0), pl.program_id(1)

        loads = _head_loads(k_hbm.at[h], v_hbm.at[h], k_v, v_v, sem, tq)

        @pl.when(i == 0)
        def _():
            for cp in loads:
                cp.start()
            _init_mask_bias(bias_s)
            dk_acc[...] = jnp.zeros(dk_acc.shape, F32)
            dv_acc[...] = jnp.zeros(dv_acc.shape, F32)
            for cp in loads[:2]:
                cp.wait()

        @pl.when(i == 1)
        def _():
            for cp in loads[2:]:
                cp.wait()

        gate, dy, of = g_ref[...], dy_ref[...], o_ref[...]
        silu, dsilu = _silu_and_grad(gate)
        do = dy * silu
        delta = jnp.sum(do * of, axis=-1, keepdims=True)
        dgate_ref[...] = _mx(dy * of * dsilu)
        dob = _mx(do)
        q = q_ref[...]
        delta_s[...] = jnp.broadcast_to(delta, delta_s.shape)
        dq_acc[...] = jnp.zeros(dq_acc.shape, F32)

        def keys(j):
            return pl.ds(pl.multiple_of(j * tq, tq), tq)

        def scores(s_ref, dp_ref, j):
            s = _dot_nt(q, k_v[keys(j), :]) * (SCALE * LOG2E) + bias_s[(j == i).astype(jnp.int32)]
            s_ref[...] = s - jnp.tile(lse_ref[...], (1, tq // 128))
            dp_ref[...] = _dot_nt(dob, v_v[keys(j), :]) - jnp.tile(delta_s[...], (1, tq // 128))

        def grads(s_ref, dp_ref, j):
            ks = keys(j)
            p = jnp.exp2(s_ref[...])
            ds = p * dp_ref[...] * SCALE
            pb, dsb = _mx(p), _mx(ds)
            dq_acc[...] += _dot(dsb, k_v[ks, :])
            dk_acc[ks, :] += _dot_tn(dsb, q)
            dv_acc[ks, :] += _dot_tn(pb, dob)

        scores(s_a, dp_a, 0)

        def pair(pp, carry):
            scores(s_b, dp_b, 2 * pp + 1)
            grads(s_a, dp_a, 2 * pp)
            scores(s_a, dp_a, 2 * pp + 2)
            grads(s_b, dp_b, 2 * pp + 1)
            return carry

        _loop_in_long_trips(i // 2, pair, longest=8)

        @pl.when(i % 2 == 1)
        def _():
            scores(s_b, dp_b, i)
            grads(s_a, dp_a, i - 1)
            grads(s_b, dp_b, i)

        @pl.when(i % 2 == 0)
        def _():
            grads(s_a, dp_a, i)

        dq_ref[...] = dq_acc[...]

        @pl.when(i == nq - 1)
        def _():
            ck = pltpu.make_async_copy(dk_acc, dk_hbm.at[h], sem.at[0])
            cv = pltpu.make_async_copy(dv_acc, dv_hbm.at[h], sem.at[1])
            ck.start()
            cv.start()
            ck.wait()
            cv.wait()

    return pl.pallas_call(
        body, name=name, grid=(N_HEADS, nq),
        in_specs=[pl.BlockSpec((None, tq, QK_PAD), lambda h, i: (h, i, 0)),
                  pl.BlockSpec((tq, 128), lambda h, i: (i, h)),
                  pl.BlockSpec((None, tq, 128), lambda h, i: (h, i, 0)),
                  pl.BlockSpec((tq, 128), lambda h, i: (i, N_HEADS + h)), _gate_block(tq), ANY, ANY],
        out_specs=[pl.BlockSpec((None, tq, QK_PAD), lambda h, i: (h, i, 0)),
                   pl.BlockSpec((tq, 128), lambda h, i: (i, h)), ANY, ANY],
        out_shape=[jax.ShapeDtypeStruct((N_HEADS, t, QK_PAD), F32), jax.ShapeDtypeStruct((t, Z_GATE), MXU_DTYPE),
                   jax.ShapeDtypeStruct((N_HEADS, t, QK_PAD), F32), jax.ShapeDtypeStruct((N_HEADS, t, V_DIM), F32)],
        scratch_shapes=[pltpu.VMEM((t, QK_PAD), MXU_DTYPE), pltpu.VMEM((t, V_DIM), MXU_DTYPE),
                        pltpu.VMEM((t, QK_PAD), F32), pltpu.VMEM((t, V_DIM), F32), pltpu.VMEM((tq, QK_PAD), F32),
                        pltpu.VMEM((tq, 128), F32)] + [pltpu.VMEM((tq, tq), F32)] * 4
        + [pltpu.VMEM((2, tq, tq), F32), pltpu.SemaphoreType.DMA((4,))],
        compiler_params=_cp(2))(qh, o, lse, dycat, z, kh, vh)


def _out_proj_fwd(yab, yc, w, x, g, target, name, tm=512):
    t, d = x.shape
    tm = min(tm, t)
    is_last = target is not None

    def body(*refs):
        if is_last:
            yab_ref, yc_ref, w_ref, x_ref, g_ref, t_ref, y_ref, dout_ref, loss_ref = refs
            _zero_when(pl.program_id(0) == 0, loss_ref)
        else:
            yab_ref, yc_ref, w_ref, x_ref, g_ref, y_ref, out_ref = refs
        y = _dot(jnp.concatenate([yab_ref[...], yc_ref[...]], axis=1), w_ref[...])
        y_ref[...] = y
        out = x_ref[...] + _rms(y, g_ref[...])[0]
        if is_last:
            diff = out - t_ref[...]
            dout_ref[...] = diff * (1.0 / d)
            part = jnp.sum(jnp.sum(diff * diff, axis=-1, keepdims=True), axis=0, keepdims=True) * (0.5 / d)
            _acc(loss_ref, jnp.broadcast_to(part, (1, 128)))
        else:
            out_ref[...] = out

    row = lambda wd: pl.BlockSpec((tm, wd), lambda i: (i, 0))
    fixed = lambda a, b: pl.BlockSpec((a, b), lambda i: (0, 0))
    in_specs = [row(512), row(512), fixed(d, d), row(d), fixed(1, d)]
    args = [yab, yc, w, x, g]
    out_specs = [row(d), row(d)]
    out_shape = [jax.ShapeDtypeStruct((t, d), F32), jax.ShapeDtypeStruct((t, d), F32)]
    if is_last:
        in_specs.append(row(d))
        args.append(target)
        out_specs.append(fixed(1, 128))
        out_shape.append(jax.ShapeDtypeStruct((1, 128), F32))
    return pl.pallas_call(body, name=name, grid=(t // tm,), in_specs=in_specs, out_specs=out_specs,
                          out_shape=out_shape, compiler_params=_cp())(*args)


def _halves(tm):
    half = tm // 2 if tm >= 512 else tm
    return [pl.ds(s, half) for s in range(0, tm, half)]


def _out_proj_bwd(dout, y, yab, yc, w, g, name, tm=1024):
    t, d = y.shape
    tm = min(tm, t)

    def body(dout_ref, y_ref, yab_ref, yc_ref, w_ref, g_ref, dycat_ref, dw_ref, dg_ref):
        _zero_when(pl.program_id(0) == 0, dw_ref, dg_ref)
        dybs = []
        for rows in _halves(tm):
            y = y_ref[rows, :]
            r = lax.rsqrt(jnp.mean(y * y, axis=-1, keepdims=True) + EPS)
            dy, dgt = _rms_bwd(y, r, g_ref[...], dout_ref[rows, :])
            _acc(dg_ref, _colsum(dgt))
            dybs.append(_mx(dy))
        for rows, dyb in zip(_halves(tm), dybs):
            _acc(dw_ref, _dot_tn(jnp.concatenate([yab_ref[rows, :], yc_ref[rows, :]], axis=1), dyb))
            dycat_ref[rows, :] = _dot_nt(dyb, w_ref[...])

    row = lambda wd: pl.BlockSpec((tm, wd), lambda i: (i, 0))
    fixed = lambda a, b: pl.BlockSpec((a, b), lambda i: (0, 0), pipeline_mode=pl.Buffered(1))
    return pl.pallas_call(
        body, name=name, grid=(t // tm,),
        in_specs=[row(d), row(d), row(512), row(512), fixed(d, d), fixed(1, d)],
        out_specs=[row(d), fixed(d, d), fixed(1, d)],
        out_shape=[jax.ShapeDtypeStruct((t, d), F32), jax.ShapeDtypeStruct((d, d), F32),
                   jax.ShapeDtypeStruct((1, d), F32)],
        compiler_params=_cp())(dout, y, yab, yc, w, g)


def _mesh_pos():
    return lax.axis_index("x"), lax.axis_index("y"), lax.axis_index("c")


def _remote(src, dst, send_sem, recv_sem, to):
    return pltpu.make_async_remote_copy(src_ref=src, dst_ref=dst, send_sem=send_sem, recv_sem=recv_sem,
                                        device_id=to, device_id_type=MESH)


CHUNK_ROWS = 256


def _pieces(rows):
    return [(s, min(CHUNK_ROWS, rows - s)) for s in range(0, rows, CHUNK_ROWS)]


def _piece_table(shapes):
    return [(a, s, sz) for a, shp in enumerate(shapes) for s, sz in _pieces(shp[-2])]


def _gather_weights(shards):
    n = len(shards)
    table = _piece_table([s.shape for s in shards])
    npc = len(table)

    def body(*refs):
        ins, outs = refs[:n], refs[n:2 * n]
        send_sems, recv_sems, fwd_send, fwd_recv = refs[2 * n:]
        x, y, c = _mesh_pos()
        me, sibling = (x, y, c), (x, y, 1 - c)
        chips = [(1 - x, y), (x, 1 - y), (1 - x, 1 - y)]
        slot = lambda cx, cy, layer: 2 * (2 * cx + cy) + layer
        first = []
        for a in range(n):
            for j, (cx, cy) in enumerate(chips):
                first.append(_remote(ins[a].at[c], outs[a].at[slot(x, y, c)], send_sems.at[a, j], recv_sems.at[a, j],
                                     (cx, cy, c)))
                first[-1].start()
        passed = []
        for j, (cx, cy) in enumerate(chips):
            for a in range(n):
                blk = outs[a].at[slot(cx, cy, c)]
                _remote(blk, blk, send_sems.at[a, j], recv_sems.at[a, j], me).wait_recv()
            for q, (a, s, sz) in enumerate(table):
                rows = outs[a].at[slot(cx, cy, c), pl.ds(s, sz)]
                passed.append(_remote(rows, rows, fwd_send.at[j, q], fwd_recv.at[j, q], sibling))
                passed[-1].start()
        for j, (cx, cy) in enumerate(chips):
            for q, (a, s, sz) in enumerate(table):
                rows = outs[a].at[slot(cx, cy, 1 - c), pl.ds(s, sz)]
                _remote(rows, rows, fwd_send.at[j, q], fwd_recv.at[j, q], me).wait_recv()
        for cp in first + passed:
            cp.wait_send()

    return pl.pallas_call(
        body, name="gather_weights", in_specs=[ANY] * n, out_specs=[ANY] * n,
        out_shape=[jax.ShapeDtypeStruct((8,) + s.shape[1:], s.dtype) for s in shards],
        scratch_shapes=[pltpu.SemaphoreType.DMA((n, 3)), pltpu.SemaphoreType.DMA((n, 3)),
                        pltpu.SemaphoreType.DMA((3, npc)), pltpu.SemaphoreType.DMA((3, npc))])(*shards)


def _pair_exchange(parts, common):
    n = len(parts)
    table = _piece_table([p.shape for p in parts] + [common.shape])
    npc = len(table)

    def body(*refs):
        ins, outs = refs[:n + 1], refs[n + 1:2 * n + 2]
        send_sems, recv_sems = refs[2 * n + 2:]
        x, y, c = _mesh_pos()
        sent = []
        for k in range(4):
            for q, (a, s, sz) in enumerate(table):
                if a == n and k > 0:
                    continue
                src = ins[a].at[2 * k + 1 - c, pl.ds(s, sz)] if a < n else ins[a].at[pl.ds(s, sz)]
                dst = outs[a].at[k, pl.ds(s, sz)] if a < n else outs[a].at[pl.ds(s, sz)]
                sent.append(_remote(src, dst, send_sems.at[k, q], recv_sems.at[k, q], (x, y, 1 - c)))
                sent[-1].start()
        for k in range(4):
            for q, (a, s, sz) in enumerate(table):
                if a == n and k > 0:
                    continue
                dst = outs[a].at[k, pl.ds(s, sz)] if a < n else outs[a].at[pl.ds(s, sz)]
                _remote(dst, dst, send_sems.at[k, q], recv_sems.at[k, q], (x, y, c)).wait_recv()
        for cp in sent:
            cp.wait_send()

    return pl.pallas_call(
        body, name="grad_pair_exchange", in_specs=[ANY] * (n + 1), out_specs=[ANY] * (n + 1),
        out_shape=[jax.ShapeDtypeStruct((4,) + p.shape[1:], p.dtype) for p in parts]
        + [jax.ShapeDtypeStruct(common.shape, common.dtype)],
        scratch_shapes=[pltpu.SemaphoreType.DMA((4, npc)), pltpu.SemaphoreType.DMA((4, npc))])(*parts, common)


def _chip_exchange(parts, common):
    n = len(parts)
    table = _piece_table([p.shape for p in parts] + [common.shape])
    npc = len(table)

    def body(*refs):
        ins, outs = refs[:n + 1], refs[n + 1:2 * n + 2]
        send_sems, recv_sems = refs[2 * n + 2:]
        x, y, c = _mesh_pos()
        mine = 2 * x + y
        chips = [(1 - x, y), (x, 1 - y), (1 - x, 1 - y)]
        src = lambda a, k: ins[a].at[k] if a < n else ins[a]
        sent = []
        for j, (cx, cy) in enumerate(chips):
            for q, (a, s, sz) in enumerate(table):
                sent.append(_remote(src(a, 2 * cx + cy).at[pl.ds(s, sz)], outs[a].at[mine, pl.ds(s, sz)],
                                    send_sems.at[j, q], recv_sems.at[j, q], (cx, cy, c)))
                sent[-1].start()
        for j, (cx, cy) in enumerate(chips):
            for q, (a, s, sz) in enumerate(table):
                dst = outs[a].at[2 * cx + cy, pl.ds(s, sz)]
                _remote(dst, dst, send_sems.at[j, q], recv_sems.at[j, q], (x, y, c)).wait_recv()
        for cp in sent:
            cp.wait_send()

    return pl.pallas_call(
        body, name="grad_chip_exchange", in_specs=[ANY] * (n + 1), out_specs=[ANY] * (n + 1),
        out_shape=[jax.ShapeDtypeStruct(p.shape, p.dtype) for p in parts]
        + [jax.ShapeDtypeStruct((4,) + common.shape, common.dtype)],
        scratch_shapes=[pltpu.SemaphoreType.DMA((3, npc)), pltpu.SemaphoreType.DMA((3, npc))])(*parts, common)


def _sibling_exchange(sums):
    n = len(sums)
    table = _piece_table([s.shape for s in sums])
    npc = len(table)

    def body(*refs):
        ins, outs = refs[:n], refs[n:2 * n]
        send_sems, recv_sems = refs[2 * n:]
        x, y, c = _mesh_pos()
        sent = []
        for q, (a, s, sz) in enumerate(table):
            sent.append(_remote(ins[a].at[pl.ds(s, sz)], outs[a].at[pl.ds(s, sz)], send_sems.at[q], recv_sems.at[q],
                                (x, y, 1 - c)))
            sent[-1].start()
        for q, (a, s, sz) in enumerate(table):
            dst = outs[a].at[pl.ds(s, sz)]
            _remote(dst, dst, send_sems.at[q], recv_sems.at[q], (x, y, c)).wait_recv()
        for cp in sent:
            cp.wait_send()

    return pl.pallas_call(
        body, name="sibling_exchange", in_specs=[ANY] * n, out_specs=[ANY] * n,
        out_shape=[jax.ShapeDtypeStruct(s.shape, s.dtype) for s in sums],
        scratch_shapes=[pltpu.SemaphoreType.DMA((npc,)), pltpu.SemaphoreType.DMA((npc,))])(*sums)


def _pair_sum(parts, got, name, tr):
    _, r, c = got.shape
    tr = min(tr, r)

    def body(p_ref, g_ref, o_ref, w_ref):
        total = p_ref[...] + g_ref[...]
        o_ref[...] = total
        w_ref[...] = total.astype(WIRE_DTYPE)

    blk = pl.BlockSpec((None, tr, c), lambda k, i: (k, i, 0))
    mine = pl.BlockSpec((None, tr, c), lambda k, i: (2 * k + lax.axis_index("c"), i, 0))
    return pl.pallas_call(
        body, name=name, grid=(4, r // tr), in_specs=[mine, blk], out_specs=[blk, blk],
        out_shape=[jax.ShapeDtypeStruct(got.shape, F32), jax.ShapeDtypeStruct(got.shape, WIRE_DTYPE)],
        compiler_params=_cp(2))(parts, got)


def _sum_chips(pair_sums, recv, name, tr):
    _, r, c = pair_sums.shape
    tr = min(tr, r)

    def body(own_ref, r_ref, o_ref):
        chip = 2 * lax.axis_index("x") + lax.axis_index("y")
        own_blk = own_ref[...]
        acc = jnp.where(chip == 0, own_blk, r_ref[0].astype(F32))
        for k in range(1, 4):
            acc = acc + jnp.where(chip == k, own_blk, r_ref[k].astype(F32))
        o_ref[...] = acc

    return pl.pallas_call(
        body, name=name, grid=(r // tr,),
        in_specs=[pl.BlockSpec((None, tr, c), lambda i: (2 * lax.axis_index("x") + lax.axis_index("y"), i, 0)),
                  pl.BlockSpec((4, tr, c), lambda i: (0, i, 0))],
        out_specs=pl.BlockSpec((tr, c), lambda i: (i, 0)),
        out_shape=jax.ShapeDtypeStruct((r, c), F32), compiler_params=_cp())(pair_sums, recv)


def _sum_leading(parts, name, tr):
    nlead, r, c = parts.shape
    tr = min(tr, r)

    def body(p_ref, o_ref):
        acc = p_ref[0]
        for j in range(1, nlead):
            acc = acc + p_ref[j]
        o_ref[...] = acc

    return pl.pallas_call(
        body, name=name, grid=(r // tr,),
        in_specs=[pl.BlockSpec((nlead, tr, c), lambda i: (0, i, 0))], out_specs=pl.BlockSpec((tr, c), lambda i: (i, 0)),
        out_shape=jax.ShapeDtypeStruct((r, c), parts.dtype), compiler_params=_cp())(parts)


def _adamw_update(w_ref, g_ref, m_ref, v_ref, d_ref, nm_ref, nv_ref):
    gg = g_ref[...]
    nm = ADAM_B1 * m_ref[...] + (1.0 - ADAM_B1) * gg
    nv = ADAM_B2 * v_ref[...] + (1.0 - ADAM_B2) * jnp.square(gg)
    m_hat = nm / (1.0 - ADAM_B1 ** ADAM_STEP)
    v_hat = nv / (1.0 - ADAM_B2 ** ADAM_STEP)
    d_ref[...] = -ADAM_LR * (m_hat / (jnp.sqrt(v_hat) + ADAM_EPS) + ADAM_WD * w_ref[...])
    nm_ref[...] = nm
    nv_ref[...] = nv


def _adamw(w, g, m, v, name, tr):
    r, c = w.shape
    tr = min(tr, r)

    def body(w_ref, g_ref, m_ref, v_ref, d_ref, nm_ref, nv_ref):
        _adamw_update(w_ref, g_ref, m_ref, v_ref, d_ref, nm_ref, nv_ref)

    blk = pl.BlockSpec((tr, c), lambda i: (i, 0))
    return pl.pallas_call(
        body, name=name, grid=(r // tr,), in_specs=[blk] * 4, out_specs=[blk] * 3,
        out_shape=[jax.ShapeDtypeStruct((r, c), F32)] * 3, compiler_params=_cp())(w, g, m, v)


def _adamw_many(ws, gs, ms, vs, name):
    n = len(ws)

    def body(*refs):
        ins, outs = refs[:4 * n], refs[4 * n:]
        for k in range(n):
            _adamw_update(ins[k], ins[n + k], ins[2 * n + k], ins[3 * n + k], outs[k], outs[n + k], outs[2 * n + k])

    vmem = pl.BlockSpec(memory_space=pltpu.VMEM)
    res = pl.pallas_call(
        body, name=name, in_specs=[vmem] * (4 * n), out_specs=[vmem] * (3 * n),
        out_shape=[jax.ShapeDtypeStruct(a.shape, F32) for a in ws] * 3)(*ws, *gs, *ms, *vs)
    return res[:n], res[n:2 * n], res[2 * n:]


def _rope_tables(positions):
    inv_freq = ROPE_BASE ** (-jnp.arange(0, 64, 2, dtype=F32) / 64)
    ang = positions.astype(F32)[:, None] * inv_freq
    cos, sin = jnp.cos(ang), jnp.sin(ang)
    t = positions.shape[0]
    rc = jnp.concatenate([jnp.ones((t, 128), F32), cos, cos, jnp.ones((t, 64), F32)], axis=1)
    rs = jnp.concatenate([jnp.zeros((t, 128), F32), sin, sin, jnp.zeros((t, 64), F32)], axis=1)
    return rc, rs


def _layer_params(l, w_in, w_uq, w_ukv, w_out, small):
    p = {}
    p["w_in"] = jnp.concatenate([w_in[l][:, :1984], jnp.zeros((1024, 64), w_in.dtype), w_in[l][:, 1984:]], axis=1)
    p["w_uq"] = jnp.pad(w_uq[l].reshape(384, 4, 192), ((0, 0), (0, 0), (0, 64))).reshape(384, 1024)
    p["w_ukv"] = w_ukv[l].reshape(256, 4, 2, 128).transpose(0, 2, 1, 3).reshape(256, 1024)
    p["w_out"] = w_out[l]
    p["pre_g"] = small["pre_norm_g"][l][None]
    p["post_g"] = small["post_norm_g"][l][None]
    p["sgu_w"] = small["sgu_w"][l].reshape(512, 128)
    p["sgu_wt"] = small["sgu_w"][l].transpose(0, 2, 1).reshape(512, 128)
    p["sgu_bias"] = jnp.repeat(small["sgu_b"][l].T, 64, axis=1)
    p["ln_g"] = small["sgu_ln_g"][l][None]
    p["ln_b"] = small["sgu_ln_b"][l][None]
    p["pool_wbd"] = _mx(jax.scipy.linalg.block_diag(*[small["pool_w"][l][gi] for gi in range(4)]))
    p["pool_scale"] = small["pool_scale"][l][None]
    p["gq"] = small["q_norm_g"][l][None]
    p["gkv"] = small["kv_norm_g"][l][None]
    return p


def _layer_fwd(l, x, p, rc, rs, target):
    z, h = _in_proj_fwd(x, p["pre_g"], p["w_in"], f"in_proj_fwd_{l}")
    yab = _mix_fwd(z, p["sgu_w"], p["sgu_bias"], p["ln_g"], p["ln_b"], p["pool_wbd"], p["pool_scale"], f"mix_fwd_{l}")
    qh, kh, vh = _qkv_fwd(z, rc, rs, p["w_uq"], p["w_ukv"], p["gq"], p["gkv"], f"qkv_fwd_{l}")
    o, yc, lse = _attn_fwd(qh, kh, vh, z, f"attn_fwd_{l}")
    outs = _out_proj_fwd(yab, yc, p["w_out"], x, p["post_g"], target, f"out_proj_fwd_{l}")
    saved = dict(x=x, z=z, h=h, yab=yab, qh=qh, kh=kh, vh=vh, o=o, yc=yc, lse=lse, y=outs[0])
    return saved, outs[1:]


def _layer_bwd(l, dout, sv, p, rc, rs):
    dycat, dw_out, dpost = _out_proj_bwd(dout, sv["y"], sv["yab"], sv["yc"], p["w_out"], p["post_g"], f"out_proj_bwd_{l}")
    dq, dgate, dk, dv = _attn_bwd(sv["qh"], sv["kh"], sv["vh"], sv["o"], sv["lse"], dycat, sv["z"], f"attn_bwd_{l}")
    dzc, dzk, dwq, dwkv, dgq, dgkv = _qkv_bwd(dq, dk, dv, sv["z"], rc, rs, p["w_uq"], p["w_ukv"], p["gq"], p["gkv"],
                                              f"qkv_bwd_{l}")
    dzm, dsw, dsb, dlng, dlnb, dpw, dps = _mix_bwd(sv["z"], dycat, p["sgu_w"], p["sgu_wt"], p["sgu_bias"], p["ln_g"],
                                                   p["ln_b"], p["pool_wbd"], p["pool_scale"], f"mix_bwd_{l}")
    dx, dw_in, dpre = _in_proj_bwd(dzm, dzc, dzk, dgate, sv["h"], sv["x"], dout, p["w_in"], p["pre_g"], f"in_proj_bwd_{l}")
    grads = {
        "pre_norm_g": dpre[0], "post_norm_g": dpost[0],
        "w_in": jnp.concatenate([dw_in[:, :1984], dw_in[:, 2048:]], axis=1),
        "sgu_w": dsw.reshape(4, 128, 128), "sgu_b": dsb[:, :4].T, "sgu_ln_g": dlng[0], "sgu_ln_b": dlnb[0],
        "pool_w": jnp.stack([dpw[64 * gi:64 * gi + 64, 64 * gi:64 * gi + 64] for gi in range(4)]),
        "pool_scale": dps[0], "q_norm_g": dgq[0],
        "w_uq": dwq.reshape(384, 4, 256)[:, :, :192].reshape(384, 768), "kv_norm_g": dgkv[0],
        "w_ukv": dwkv.reshape(256, 2, 4, 128).transpose(0, 2, 1, 3).reshape(256, 1024), "w_out": dw_out,
    }
    return dx, grads


SMALL_NAMES = ["pre_norm_g", "post_norm_g", "sgu_w", "sgu_b", "sgu_ln_g", "sgu_ln_b", "pool_w", "pool_scale",
               "q_norm_g", "kv_norm_g"]
BIG_NAMES = ["w_in", "w_uq", "w_ukv", "w_out"]
WEIGHT_NAMES = ["pre_norm_g", "post_norm_g", "w_in", "sgu_w", "sgu_b", "sgu_ln_g", "sgu_ln_b", "pool_w", "pool_scale",
                "q_norm_g", "w_uq", "kv_norm_g", "w_ukv", "w_out"]


def _local_step(x, positions, target, w_in, w_uq, w_ukv, w_out, small):
    rc, rs = _rope_tables(positions)
    params = [_layer_params(l, w_in, w_uq, w_ukv, w_out, small) for l in range(DEPTH)]
    saved = []
    for l in range(DEPTH):
        sv, outs = _layer_fwd(l, x, params[l], rc, rs, target if l == DEPTH - 1 else None)
        saved.append(sv)
        if l < DEPTH - 1:
            x = outs[0]
    dout, loss = outs
    grads = [None] * DEPTH
    for l in reversed(range(DEPTH)):
        dout, grads[l] = _layer_bwd(l, dout, saved[l], params[l], rc, rs)
    return loss[0, 0], dout, {k: jnp.stack([grads[l][k] for l in range(DEPTH)]) for k in WEIGHT_NAMES}


def _pack_small(tree, extra=None):
    pieces = [tree[k].reshape(-1) for k in SMALL_NAMES]
    pieces.append(jnp.zeros((1,), F32) if extra is None else extra.reshape(1))
    flat = jnp.concatenate(pieces)
    rows = -(-flat.shape[0] // 1024) * 8
    return jnp.pad(flat, (0, rows * 128 - flat.shape[0])).reshape(rows, 128)


def _unpack_small(packed, like):
    flat = packed.reshape(-1)
    out, off = {}, 0
    for k in SMALL_NAMES:
        size = like[k].size
        out[k] = flat[off:off + size].reshape(like[k].shape)
        off += size
    return out, flat[off]


def kernel(x, positions, pre_norm_g, post_norm_g, w_in, sgu_w, sgu_b, sgu_ln_g, sgu_ln_b, pool_w, pool_scale, q_norm_g, w_uq, kv_norm_g, w_ukv, w_out, loss_target, m_pre_norm_g, m_post_norm_g, m_w_in, m_sgu_w, m_sgu_b, m_sgu_ln_g, m_sgu_ln_b, m_pool_w, m_pool_scale, m_q_norm_g, m_w_uq, m_kv_norm_g, m_w_ukv, m_w_out, v_pre_norm_g, v_post_norm_g, v_w_in, v_sgu_w, v_sgu_b, v_sgu_ln_g, v_sgu_ln_b, v_pool_w, v_pool_scale, v_q_norm_g, v_w_uq, v_kv_norm_g, v_w_ukv, v_w_out):
    w = dict(pre_norm_g=pre_norm_g, post_norm_g=post_norm_g, w_in=w_in, sgu_w=sgu_w, sgu_b=sgu_b, sgu_ln_g=sgu_ln_g,
             sgu_ln_b=sgu_ln_b, pool_w=pool_w, pool_scale=pool_scale, q_norm_g=q_norm_g, w_uq=w_uq, kv_norm_g=kv_norm_g,
             w_ukv=w_ukv, w_out=w_out)
    m = dict(pre_norm_g=m_pre_norm_g, post_norm_g=m_post_norm_g, w_in=m_w_in, sgu_w=m_sgu_w, sgu_b=m_sgu_b,
             sgu_ln_g=m_sgu_ln_g, sgu_ln_b=m_sgu_ln_b, pool_w=m_pool_w, pool_scale=m_pool_scale, q_norm_g=m_q_norm_g,
             w_uq=m_w_uq, kv_norm_g=m_kv_norm_g, w_ukv=m_w_ukv, w_out=m_w_out)
    v = dict(pre_norm_g=v_pre_norm_g, post_norm_g=v_post_norm_g, w_in=v_w_in, sgu_w=v_sgu_w, sgu_b=v_sgu_b,
             sgu_ln_g=v_sgu_ln_g, sgu_ln_b=v_sgu_ln_b, pool_w=v_pool_w, pool_scale=v_pool_scale, q_norm_g=v_q_norm_g,
             w_uq=v_w_uq, kv_norm_g=v_kv_norm_g, w_ukv=v_w_ukv, w_out=v_w_out)

    core = lax.axis_index("c")
    chip = 2 * lax.axis_index("x") + lax.axis_index("y")
    shards = [_mx(w[k]) for k in BIG_NAMES]
    gathered = _gather_weights(shards)
    g_in, g_uq, g_ukv, g_out = [lax.dynamic_update_slice(g, s, (2 * chip, 0, 0)) for g, s in zip(gathered, shards)]
    cols = lambda g: g.reshape((4, 2) + g.shape[1:]).transpose(1, 2, 0, 3).reshape(2, g.shape[1], 4 * g.shape[2])
    full_out = g_out.reshape(4, 2, 256, 1024).transpose(1, 0, 2, 3).reshape(2, 1024, 1024)
    loss, dx, grads = _local_step(x[0], positions[0], loss_target[0], cols(g_in), cols(g_uq), cols(g_ukv), full_out, w)

    split_cols = lambda g: g.reshape(2, g.shape[1], 4, g.shape[2] // 4).transpose(2, 0, 1, 3).reshape(8, g.shape[1], g.shape[2] // 4)
    parts = [split_cols(grads["w_in"]), split_cols(grads["w_uq"]), split_cols(grads["w_ukv"]),
             grads["w_out"].reshape(2, 4, 256, 1024).transpose(1, 0, 2, 3).reshape(8, 256, 1024)]
    common = _pack_small(grads, loss)
    got = _pair_exchange(parts, common)
    pair_sums = [_pair_sum(parts[a], got[a], f"pair_sum_{BIG_NAMES[a]}", 128) for a in range(4)]
    chip_common = _sum_leading(jnp.stack([common, got[4]]), "pair_sum_small", common.shape[0])
    received = _chip_exchange([ps[1] for ps in pair_sums], chip_common)
    sums = [_sum_chips(pair_sums[a][0], received[a], f"sum_{BIG_NAMES[a]}", 128) for a in range(4)]
    all_common = lax.dynamic_update_slice(received[4], chip_common[None], (chip, 0, 0))
    small_sum, loss = _unpack_small(_sum_leading(all_common, "sum_small", all_common.shape[1]), w)
    others = _sibling_exchange(sums)
    total = dict(small_sum)
    for a, k in enumerate(BIG_NAMES):
        total[k] = jnp.where(core == 0, jnp.stack([sums[a], others[a]]), jnp.stack([others[a], sums[a]]))

    rows2d = lambda a: a.reshape(-1, a.shape[-1])
    small_out = _adamw_many(*[[rows2d(tree[k]) for k in SMALL_NAMES] for tree in (w, total, m, v)], "adamw_small")
    delta, new_m, new_v = ({k: r.reshape(w[k].shape) for k, r in zip(SMALL_NAMES, res)} for res in small_out)
    for k in BIG_NAMES:
        shape = w[k].shape
        flat = lambda a: a.reshape(shape[0] * shape[1], shape[2])
        res = _adamw(flat(w[k]), flat(total[k]), flat(m[k]), flat(v[k]), f"adamw_{k}", 256)
        delta[k], new_m[k], new_v[k] = (r.reshape(shape) for r in res)

    return (loss, dx[None], *[total[k] for k in WEIGHT_NAMES], *[delta[k] for k in WEIGHT_NAMES],
            *[new_m[k] for k in WEIGHT_NAMES], *[new_v[k] for k in WEIGHT_NAMES])
```

```python
import jax
import jax.numpy as jnp
from jax import lax
from jax.experimental import pallas as pl
from jax.experimental.pallas import tpu as pltpu

F32 = jnp.float32
MXU_DTYPE = jnp.bfloat16
WIRE_DTYPE = jnp.bfloat16
EPS = 1e-6
NEG_INF = -1e30
CHUNK = 64
DEPTH = 2
N_HEADS = 4
QK_PAD = 256
V_DIM = 128
SCALE = 192 ** -0.5
LOG2E = 1.4426950408889634
ROPE_BASE = 10000.0
ADAM_LR, ADAM_B1, ADAM_B2, ADAM_EPS, ADAM_WD, ADAM_STEP = 0.001, 0.9, 0.999, 1e-08, 0.01, 10
VMEM_LIMIT_BYTES = 56 * 1024 * 1024
MESH = pl.DeviceIdType.MESH
ANY = pl.BlockSpec(memory_space=pl.ANY)

Z_MIX, Z_C, Z_KR, Z_GATE = 1280, 640, 128, 512
Z_W = Z_MIX + Z_C + Z_KR + Z_GATE


def _cp(n_axes=1):
    return pltpu.CompilerParams(dimension_semantics=("arbitrary",) * n_axes, vmem_limit_bytes=VMEM_LIMIT_BYTES)


def _dot(a, b):
    return lax.dot_general(a, b, (((1,), (0,)), ((), ())), preferred_element_type=F32)


def _dot_nt(a, b):
    return lax.dot_general(a, b, (((1,), (1,)), ((), ())), preferred_element_type=F32)


def _dot_tn(a, b):
    return lax.dot_general(a, b, (((0,), (0,)), ((), ())), preferred_element_type=F32)


def _mx(a):
    return a.astype(MXU_DTYPE)


def _silu_and_grad(g):
    sg = jax.nn.sigmoid(g)
    return g * sg, sg * (1.0 + g * (1.0 - sg))


def _rms(x, g):
    r = lax.rsqrt(jnp.mean(x * x, axis=-1, keepdims=True) + EPS)
    return x * r * g, r


def _rms_bwd(x, r, g, dy):
    xhat = x * r
    dyg = dy * g
    dx = r * (dyg - xhat * jnp.mean(dyg * xhat, axis=-1, keepdims=True))
    return dx, dy * xhat


def _zero_when(first, *refs):
    @pl.when(first)
    def _():
        for ref in refs:
            ref[...] = jnp.zeros(ref.shape, ref.dtype)


def _acc(ref, val):
    ref[...] += val


def _colsum(a):
    return jnp.sum(a, axis=0, keepdims=True)


def _in_proj_fwd(x, g, w, name, tm=1024):
    t, d = x.shape
    n = w.shape[1]
    tm = min(tm, t)

    def body(x_ref, g_ref, w_ref, z_ref, h_ref):
        hs = []
        for rows in _halves(tm):
            h, _ = _rms(x_ref[rows, :], g_ref[...])
            hs.append(_mx(h))
            h_ref[rows, :] = hs[-1]
        for rows, h in zip(_halves(tm), hs):
            z_ref[rows, :] = _dot(h, w_ref[...])

    return pl.pallas_call(
        body, name=name, grid=(t // tm,),
        in_specs=[pl.BlockSpec((tm, d), lambda i: (i, 0)), pl.BlockSpec((1, d), lambda i: (0, 0)),
                  pl.BlockSpec((d, n), lambda i: (0, 0), pipeline_mode=pl.Buffered(1))],
        out_specs=[pl.BlockSpec((tm, n), lambda i: (i, 0)), pl.BlockSpec((tm, d), lambda i: (i, 0))],
        out_shape=[jax.ShapeDtypeStruct((t, n), F32), jax.ShapeDtypeStruct((t, d), MXU_DTYPE)],
        compiler_params=_cp())(x, g, w)


def _in_proj_bwd(dz_mix, dz_c, dz_kr, dz_gate, h, x, d_res, w, g, name, tm=512):
    t, d = x.shape
    n = w.shape[1]
    tm = min(tm, t)

    def body(dm_ref, dc_ref, dk_ref, dg_ref, h_ref, x_ref, dres_ref, w_ref, g_ref, dx_ref, dw_ref, dgn_ref):
        first = pl.program_id(0) == 0
        dz = jnp.concatenate([dm_ref[...], dc_ref[...], dk_ref[...], dg_ref[...]], axis=1)

        _zero_when(first, dw_ref, dgn_ref)
        dh = _dot_nt(dz, w_ref[...])
        hb = h_ref[...]
        for c0 in range(0, n, 512):
            dw_ref[:, c0:c0 + 512] += _dot_tn(hb, dz[:, c0:c0 + 512])
        xf = x_ref[...]
        r = lax.rsqrt(jnp.mean(xf * xf, axis=-1, keepdims=True) + EPS)
        dx, dgt = _rms_bwd(xf, r, g_ref[...], dh)
        dx_ref[...] = dx + dres_ref[...]
        _acc(dgn_ref, _colsum(dgt))

    row = lambda wd: pl.BlockSpec((tm, wd), lambda i: (i, 0))
    fixed = lambda a, b: pl.BlockSpec((a, b), lambda i: (0, 0), pipeline_mode=pl.Buffered(1))
    return pl.pallas_call(
        body, name=name, grid=(t // tm,),
        in_specs=[row(Z_MIX), row(Z_C), row(Z_KR), row(Z_GATE), row(d), row(d), row(d), fixed(d, n), fixed(1, d)],
        out_specs=[row(d), fixed(d, n), fixed(1, d)],
        out_shape=[jax.ShapeDtypeStruct((t, d), F32), jax.ShapeDtypeStruct((d, n), F32),
                   jax.ShapeDtypeStruct((1, d), F32)],
        compiler_params=_cp())(dz_mix, dz_c, dz_kr, dz_gate, h, x, d_res, w, g)


def _lane_group(shape):
    return lax.broadcasted_iota(jnp.int32, shape, 1) // 64


def _select_group(vals):
    grp = _lane_group(vals[0].shape)
    out = vals[3]
    for gi in (2, 1, 0):
        out = jnp.where(grp == gi, vals[gi], out)
    return out


def _sgu_mask(transposed):
    r = (lax.broadcasted_iota(jnp.int32, (512, 128), 0) % 128) // CHUNK
    c = lax.broadcasted_iota(jnp.int32, (512, 128), 1) // CHUNK
    return (r <= c) if transposed else (c <= r)


def _sgu_apply(wstack, vb, nblk):
    outs = []
    for n in range(nblk):
        r = _dot(wstack, vb[n * 128:(n + 1) * 128, :])
        outs.append(_select_group([r[hh * 128:(hh + 1) * 128, :] for hh in range(4)]))
    return jnp.concatenate(outs, axis=0)


def _layer_norm(v, g, b):
    mu = jnp.mean(v, axis=-1, keepdims=True)
    vc = v - mu
    rstd = lax.rsqrt(jnp.mean(vc * vc, axis=-1, keepdims=True) + EPS)
    vhat = vc * rstd
    return vhat * g + b, vhat, rstd


def _div_by_counts(x, t0):
    t = t0 + lax.broadcasted_iota(jnp.int32, (16, 256), 0)
    w = _select_group([jnp.full((16, 256), wv, jnp.int32) for wv in (2, 4, 8, 16)])
    head = x[:16, :] / jnp.minimum(t + 1, w).astype(F32)
    inv_w = _select_group([jnp.full((1, 256), 1.0 / wv, F32) for wv in (2, 4, 8, 16)])
    return jnp.concatenate([head, x[16:, :] * inv_w], axis=0)


def _pooled(p, halo, t0):
    tm = p.shape[0]
    ext = jnp.concatenate([halo, p], axis=0)
    s2 = ext + pltpu.roll(ext, 1, 0)
    s4 = s2 + pltpu.roll(s2, 2, 0)
    s8 = s4 + pltpu.roll(s4, 4, 0)
    s16 = s8 + pltpu.roll(s8, 8, 0)
    sel = _select_group([s2, s4, s8, s16])[16:, :]
    return _div_by_counts(sel, t0) - p


def _pooled_bwd(dpool, dpool_halo, t0):
    tm = dpool.shape[0]
    n = tm + 16
    ext = _div_by_counts(jnp.concatenate([dpool, dpool_halo], axis=0), t0)
    f2 = ext + pltpu.roll(ext, n - 1, 0)
    f4 = f2 + pltpu.roll(f2, n - 2, 0)
    f8 = f4 + pltpu.roll(f4, n - 4, 0)
    f16 = f8 + pltpu.roll(f8, n - 8, 0)
    return _select_group([f2, f4, f8, f16])[:tm, :] - dpool


def _mix_specs(t, tm):
    nt16 = t // 16
    zrow = pl.BlockSpec((tm, Z_MIX), lambda i: (i, 0))
    prev_halo = pl.BlockSpec((16, 256), lambda i: (jnp.maximum(i * (tm // 16) - 1, 0), 3))
    fixed = lambda a, b: pl.BlockSpec((a, b), lambda i: (0, 0))
    params = [fixed(512, 128), fixed(128, 256), fixed(1, 256), fixed(1, 256), fixed(256, 256), fixed(1, 256)]
    return nt16, zrow, prev_halo, fixed, params


def _mix_fwd(z, sgu_w, sgu_bias, ln_g, ln_b, pool_wbd, pool_scale, name, tm=512):
    t = z.shape[0]
    tm = min(tm, t)
    _, zrow, prev_halo, _, params = _mix_specs(t, tm)

    def body(z_ref, halo_ref, w_ref, bias_ref, lng_ref, lnb_ref, pw_ref, ps_ref, y_ref):
        i = pl.program_id(0)
        u, v, gate = z_ref[:, 0:256], z_ref[:, 256:512], z_ref[:, 512:768]
        p, pgate = z_ref[:, 768:1024], z_ref[:, 1024:1280]
        vn, _, _ = _layer_norm(v, lng_ref[...], lnb_ref[...])
        wm = _mx(jnp.where(_sgu_mask(False), w_ref[...], 0.0))
        halo = jnp.where(i > 0, halo_ref[...], 0.0)
        pooled = _pooled(p, halo, i * tm)
        mixed = _sgu_apply(wm, _mx(vn), tm // 128) + jnp.tile(bias_ref[...], (tm // 128, 1))
        mixedp = _dot(_mx(pooled), pw_ref[...])
        ya = u * mixed * _silu_and_grad(gate)[0]
        yb = mixedp * ps_ref[...] * _silu_and_grad(pgate)[0]
        y_ref[...] = _mx(jnp.concatenate([ya, yb], axis=1))

    return pl.pallas_call(
        body, name=name, grid=(t // tm,),
        in_specs=[zrow, prev_halo] + params,
        out_specs=pl.BlockSpec((tm, 512), lambda i: (i, 0)),
        out_shape=jax.ShapeDtypeStruct((t, 512), MXU_DTYPE),
        compiler_params=_cp())(z, z, sgu_w, sgu_bias, ln_g, ln_b, pool_wbd, pool_scale)


def _mix_bwd(z, dycat, sgu_w, sgu_wt, sgu_bias, ln_g, ln_b, pool_wbd, pool_scale, name, tm=512):
    t = z.shape[0]
    tm = min(tm, t)
    nt16, zrow, prev_halo, fixed, params = _mix_specs(t, tm)
    nblk = tm // 128
    last = t // tm - 1

    def body(z_ref, halo_ref, zn_ref, dy_ref, dyn_ref, w_ref, wt_ref, bias_ref, lng_ref, lnb_ref, pw_ref, ps_ref,
             dz_ref, dw_ref, db_ref, dlng_ref, dlnb_ref, dpw_ref, dps_ref):
        i = pl.program_id(0)
        _zero_when(i == 0, dw_ref, db_ref, dlng_ref, dlnb_ref, dpw_ref, dps_ref)
        u, v, gate = z_ref[:, 0:256], z_ref[:, 256:512], z_ref[:, 512:768]
        p, pgate = z_ref[:, 768:1024], z_ref[:, 1024:1280]
        dya, dyb = dy_ref[:, 0:256], dy_ref[:, 256:512]
        vn, vhat, rstd = _layer_norm(v, lng_ref[...], lnb_ref[...])
        vnb = _mx(vn)
        wm = _mx(jnp.where(_sgu_mask(False), w_ref[...], 0.0))
        wmt = _mx(jnp.where(_sgu_mask(True), wt_ref[...], 0.0))
        halo = jnp.where(i > 0, halo_ref[...], 0.0)
        pooled_b = _mx(_pooled(p, halo, i * tm))
        silu, dsilu = _silu_and_grad(gate)
        psilu, pdsilu = _silu_and_grad(pgate)
        mixed = _sgu_apply(wm, vnb, nblk) + jnp.tile(bias_ref[...], (nblk, 1))
        mixedp = _dot(pooled_b, pw_ref[...])
        t1 = u * mixed
        d_gate = dya * t1 * dsilu
        d_t1 = dya * silu
        d_u = d_t1 * mixed
        d_mixed = d_t1 * u
        dmb = _mx(d_mixed)
        d_pgate = dyb * (mixedp * ps_ref[...]) * pdsilu
        d_ms = dyb * psilu
        _acc(dps_ref, _colsum(d_ms * mixedp))
        dmpb = _mx(d_ms * ps_ref[...])
        dmp_halo = _mx(dyn_ref[...] * _silu_and_grad(zn_ref[...])[0] * ps_ref[...])
        d_vn = _sgu_apply(wmt, dmb, nblk)
        grp = _lane_group((128, 256))
        lane = lax.broadcasted_iota(jnp.int32, (128, 128), 1)
        dws = [jnp.zeros((128, 128), F32) for _ in range(4)]
        dbias = jnp.zeros((128, 128), F32)
        for n in range(nblk):
            dm_n, dmb_n, vnb_n = d_mixed[n * 128:(n + 1) * 128], dmb[n * 128:(n + 1) * 128], vnb[n * 128:(n + 1) * 128]
            for hh in range(4):
                dws[hh] = dws[hh] + _dot_nt(jnp.where(grp == hh, dmb_n, jnp.zeros_like(dmb_n)), vnb_n)
                rs = jnp.sum(jnp.where(grp == hh, dm_n, 0.0), axis=-1, keepdims=True)
                dbias = dbias + jnp.where(lane == hh, rs, 0.0)
        _acc(dw_ref, jnp.concatenate(dws, axis=0))
        _acc(db_ref, dbias)
        _acc(dpw_ref, _dot_tn(pooled_b, dmpb))
        d_pooled = _dot_nt(dmpb, pw_ref[...])
        d_pooled_halo = jnp.where(i < last, _dot_nt(dmp_halo, pw_ref[...]), 0.0)
        _acc(dlng_ref, _colsum(d_vn * vhat))
        _acc(dlnb_ref, _colsum(d_vn))
        dvh = d_vn * lng_ref[...]
        d_v = rstd * (dvh - jnp.mean(dvh, axis=-1, keepdims=True) - vhat * jnp.mean(dvh * vhat, axis=-1, keepdims=True))
        d_p = _pooled_bwd(d_pooled, d_pooled_halo, i * tm)
        dz_ref[...] = _mx(jnp.concatenate([d_u, d_v, d_gate, d_p, d_pgate], axis=1))

        @pl.when(i == last)
        def _():
            dw_ref[...] = jnp.where(_sgu_mask(False), dw_ref[...], 0.0)

    nxt = lambda i: jnp.minimum((i + 1) * (tm // 16), nt16 - 1)
    return pl.pallas_call(
        body, name=name, grid=(t // tm,),
        in_specs=[zrow, prev_halo, pl.BlockSpec((16, 256), lambda i: (nxt(i), 4)),
                  pl.BlockSpec((tm, 512), lambda i: (i, 0)), pl.BlockSpec((16, 256), lambda i: (nxt(i), 1)),
                  params[0], fixed(512, 128)] + params[1:],
        out_specs=[pl.BlockSpec((tm, Z_MIX), lambda i: (i, 0)), fixed(512, 128), fixed(128, 128), fixed(1, 256),
                   fixed(1, 256), fixed(256, 256), fixed(1, 256)],
        out_shape=[jax.ShapeDtypeStruct((t, Z_MIX), MXU_DTYPE), jax.ShapeDtypeStruct((512, 128), F32),
                   jax.ShapeDtypeStruct((128, 128), F32), jax.ShapeDtypeStruct((1, 256), F32),
                   jax.ShapeDtypeStruct((1, 256), F32), jax.ShapeDtypeStruct((256, 256), F32),
                   jax.ShapeDtypeStruct((1, 256), F32)],
        compiler_params=_cp())(z, z, z, dycat, dycat, sgu_w, sgu_wt, sgu_bias, ln_g, ln_b, pool_wbd, pool_scale)


def _rot_half(x, transpose):
    w = x.shape[1]
    lane = lax.broadcasted_iota(jnp.int32, x.shape, 1) % min(w, 256)
    base = 128 if w >= 256 else 0
    lo = jnp.logical_and(lane >= base, lane < base + 32)
    hi = jnp.logical_and(lane >= base + 32, lane < base + 64)
    up = pltpu.roll(x, w - 32, 1)
    down = pltpu.roll(x, 32, 1)
    if transpose:
        return jnp.where(lo, up, jnp.where(hi, -down, 0.0))
    return jnp.where(lo, -up, jnp.where(hi, down, 0.0))


def _rope(x, c, s):
    return x * c + _rot_half(x, False) * s


def _rope_bwd(dy, c, s):
    return dy * c + _rot_half(dy * s, True)


def _qkv_fwd(z, rc, rs, w_uq, w_ukv, gq, gkv, name, tm=512):
    t = z.shape[0]
    tm = min(tm, t)

    def body(zc_ref, zk_ref, rc_ref, rs_ref, wq_ref, wkv_ref, gq_ref, gkv_ref, q_ref, k_ref, v_ref):
        cq, ckv = zc_ref[:, 0:384], zc_ref[:, 384:640]
        c, s = rc_ref[...], rs_ref[...]
        qn, _ = _rms(cq, gq_ref[...])
        kvn, _ = _rms(ckv, gkv_ref[...])
        q_pre = _dot(_mx(qn), wq_ref[...])
        kv = _dot(_mx(kvn), wkv_ref[...])
        kpe = _rope(zk_ref[...], c[:, 128:256], s[:, 128:256])
        q = _rope(q_pre, jnp.tile(c, (1, N_HEADS)), jnp.tile(s, (1, N_HEADS)))
        for hh in range(N_HEADS):
            q_ref[hh] = _mx(q[:, hh * QK_PAD:(hh + 1) * QK_PAD])
            k_ref[hh] = _mx(jnp.concatenate([kv[:, hh * 128:(hh + 1) * 128], kpe], axis=1))
            v_ref[hh] = _mx(kv[:, 512 + hh * 128:512 + (hh + 1) * 128])

    fixed = lambda a, b: pl.BlockSpec((a, b), lambda i: (0, 0))
    heads = lambda wd: pl.BlockSpec((N_HEADS, tm, wd), lambda i: (0, i, 0))
    return pl.pallas_call(
        body, name=name, grid=(t // tm,),
        in_specs=[pl.BlockSpec((tm, Z_C), lambda i: (i, Z_MIX // Z_C)),
                  pl.BlockSpec((tm, Z_KR), lambda i: (i, (Z_MIX + Z_C) // Z_KR)),
                  pl.BlockSpec((tm, 256), lambda i: (i, 0)), pl.BlockSpec((tm, 256), lambda i: (i, 0)),
                  fixed(384, 1024), fixed(256, 1024), fixed(1, 384), fixed(1, 256)],
        out_specs=[heads(QK_PAD), heads(QK_PAD), heads(V_DIM)],
        out_shape=[jax.ShapeDtypeStruct((N_HEADS, t, QK_PAD), MXU_DTYPE),
                   jax.ShapeDtypeStruct((N_HEADS, t, QK_PAD), MXU_DTYPE),
                   jax.ShapeDtypeStruct((N_HEADS, t, V_DIM), MXU_DTYPE)],
        compiler_params=_cp())(z, z, rc, rs, w_uq, w_ukv, gq, gkv)


def _qkv_bwd(dq, dk, dv, z, rc, rs, w_uq, w_ukv, gq, gkv, name, tm=512):
    t = z.shape[0]
    tm = min(tm, t)

    def body(dq_ref, dk_ref, dv_ref, zc_ref, rc_ref, rs_ref, wq_ref, wkv_ref, gq_ref, gkv_ref,
             dzc_ref, dzk_ref, dwq_ref, dwkv_ref, dgq_ref, dgkv_ref):
        _zero_when(pl.program_id(0) == 0, dwq_ref, dwkv_ref, dgq_ref, dgkv_ref)
        cq, ckv = zc_ref[:, 0:384], zc_ref[:, 384:640]
        c, s = rc_ref[...], rs_ref[...]
        dkv = _mx(jnp.concatenate([dk_ref[hh][:, 0:128] for hh in range(N_HEADS)]
                                  + [dv_ref[hh] for hh in range(N_HEADS)], axis=1))
        kvn, rkv = _rms(ckv, gkv_ref[...])
        _acc(dwkv_ref, _dot_tn(_mx(kvn), dkv))
        d_kvn = _dot_nt(dkv, wkv_ref[...])
        dq_all = jnp.concatenate([dq_ref[hh] for hh in range(N_HEADS)], axis=1)
        dqp = _mx(_rope_bwd(dq_all, jnp.tile(c, (1, N_HEADS)), jnp.tile(s, (1, N_HEADS))))
        qn, rq = _rms(cq, gq_ref[...])
        dkpe = dk_ref[0][:, 128:256]
        for hh in range(1, N_HEADS):
            dkpe = dkpe + dk_ref[hh][:, 128:256]
        dzk_ref[...] = _mx(_rope_bwd(dkpe, c[:, 128:256], s[:, 128:256]))
        _acc(dwq_ref, _dot_tn(_mx(qn), dqp))
        d_qn = _dot_nt(dqp, wq_ref[...])
        d_ckv, dgkv_t = _rms_bwd(ckv, rkv, gkv_ref[...], d_kvn)
        _acc(dgkv_ref, _colsum(dgkv_t))
        d_cq, dgq_t = _rms_bwd(cq, rq, gq_ref[...], d_qn)
        _acc(dgq_ref, _colsum(dgq_t))
        dzc_ref[...] = _mx(jnp.concatenate([d_cq, d_ckv], axis=1))

    fixed = lambda a, b: pl.BlockSpec((a, b), lambda i: (0, 0))
    heads = lambda wd: pl.BlockSpec((N_HEADS, tm, wd), lambda i: (0, i, 0))
    return pl.pallas_call(
        body, name=name, grid=(t // tm,),
        in_specs=[heads(QK_PAD), heads(QK_PAD), heads(V_DIM), pl.BlockSpec((tm, Z_C), lambda i: (i, Z_MIX // Z_C)),
                  pl.BlockSpec((tm, 256), lambda i: (i, 0)), pl.BlockSpec((tm, 256), lambda i: (i, 0)),
                  fixed(384, 1024), fixed(256, 1024), fixed(1, 384), fixed(1, 256)],
        out_specs=[pl.BlockSpec((tm, Z_C), lambda i: (i, 0)), pl.BlockSpec((tm, Z_KR), lambda i: (i, 0)),
                   fixed(384, 1024), fixed(256, 1024), fixed(1, 384), fixed(1, 256)],
        out_shape=[jax.ShapeDtypeStruct((t, Z_C), MXU_DTYPE), jax.ShapeDtypeStruct((t, Z_KR), MXU_DTYPE),
                   jax.ShapeDtypeStruct((384, 1024), F32), jax.ShapeDtypeStruct((256, 1024), F32),
                   jax.ShapeDtypeStruct((1, 384), F32), jax.ShapeDtypeStruct((1, 256), F32)],
        compiler_params=_cp())(dq, dk, dv, z, rc, rs, w_uq, w_ukv, gq, gkv)


def _loop_in_long_trips(n, body, longest=4):
    def doubled(inner):
        return lambda t, carry: inner(2 * t + 1, inner(2 * t, carry))

    trips = [body]
    while 2 ** (len(trips) - 1) < longest:
        trips.append(doubled(trips[-1]))
    done = 0
    for level in reversed(range(len(trips))):
        size = 2 ** level
        end = n // size
        lax.fori_loop(done, end, trips[level], 0)
        done = 2 * end if level else end


def _head_loads(k_src, v_src, k_dst, v_dst, sems, first_rows):
    t = k_src.shape[0]
    parts = [(0, first_rows)] + ([(first_rows, t - first_rows)] if t > first_rows else [])
    return [pltpu.make_async_copy(src.at[pl.ds(r0, n)], dst.at[pl.ds(r0, n)], sems.at[2 * q + w])
            for q, (r0, n) in enumerate(parts) for w, (src, dst) in enumerate([(k_src, k_dst), (v_src, v_dst)])]


def _init_mask_bias(bias_ref):
    _, tq, tk = bias_ref.shape
    r = lax.broadcasted_iota(jnp.int32, (tq, tk), 0) // CHUNK
    c = lax.broadcasted_iota(jnp.int32, (tq, tk), 1) // CHUNK
    bias_ref[0] = jnp.zeros((tq, tk), F32)
    for d in range(tq // tk):
        bias_ref[1 + d] = jnp.where(c + d * (tk // CHUNK) <= r, 0.0, NEG_INF)


def _gate_block(tq):
    return pl.BlockSpec((tq, 128), lambda h, i: (i, (Z_MIX + Z_C + Z_KR) // 128 + h))


def _attn_fwd(qh, kh, vh, z, name, tq=1024, tk=512):
    t = qh.shape[1]
    tq = min(tq, t)
    tk = min(tk, tq)
    ratio = tq // tk

    def body(q_ref, g_ref, k_hbm, v_hbm, o_ref, yc_ref, lse_ref, k_v, v_v, m_s, acc_s, s_a, s_b, mx_a, mx_b, bias_s,
             sem):
        h, i = pl.program_id(0), pl.program_id(1)

        loads = _head_loads(k_hbm.at[h], v_hbm.at[h], k_v, v_v.at[:, 0:V_DIM], sem, tq)

        @pl.when(i == 0)
        def _():
            for cp in loads:
                cp.start()
            v_v[:, V_DIM:2 * V_DIM] = jnp.ones((t, V_DIM), MXU_DTYPE)
            _init_mask_bias(bias_s)
            for cp in loads[:2]:
                cp.wait()

        @pl.when(i == 1)
        def _():
            for cp in loads[2:]:
                cp.wait()

        q = q_ref[...]
        m_s[...] = jnp.full(m_s.shape, NEG_INF, F32)
        acc_s[...] = jnp.zeros(acc_s.shape, F32)

        last = ratio * (i + 1) - 1

        def keys(j):
            return pl.ds(pl.multiple_of(j * tk, tk), tk)

        def scores(s_ref, mx_ref, j, biased):
            s = _dot_nt(q, k_v[keys(j), :]) * (SCALE * LOG2E)
            if biased:
                s = s + bias_s[jnp.maximum(j - ratio * i + 1, 0)]
            s_ref[...] = s
            mx_ref[...] = jnp.broadcast_to(jnp.max(s, axis=-1, keepdims=True), mx_ref.shape)

        def softmax_pv(s_ref, mx_ref, j):
            m_old = m_s[...]
            m_new = jnp.maximum(m_old, mx_ref[...])
            p = jnp.exp2(s_ref[...] - jnp.tile(m_new, (1, tk // 128)))
            alpha = jnp.exp2(m_old - m_new)
            m_s[...] = m_new
            acc_s[...] = jnp.tile(alpha, (1, 2)) * acc_s[...] + _dot(_mx(p), v_v[keys(j), :])

        scores(s_a, mx_a, 0, True)

        def pair(pp, carry, biased):
            scores(s_b, mx_b, 2 * pp + 1, biased)
            softmax_pv(s_a, mx_a, 2 * pp)
            scores(s_a, mx_a, jnp.minimum(2 * pp + 2, last), biased)
            softmax_pv(s_b, mx_b, 2 * pp + 1)
            return carry

        n_pairs = (last + 1) // 2
        n_plain = jnp.maximum(ratio * i // 2 - 1, 0)
        _loop_in_long_trips(n_plain, lambda pp, carry: pair(pp, carry, False))
        _loop_in_long_trips(n_pairs - n_plain, lambda pp, carry: pair(n_plain + pp, carry, True))
        if ratio % 2 == 1:
            @pl.when(last % 2 == 0)
            def _():
                softmax_pv(s_a, mx_a, last)

        l = acc_s[:, V_DIM:2 * V_DIM]
        o = acc_s[:, 0:V_DIM] / l
        o_ref[...] = o
        yc_ref[...] = _mx(o * _silu_and_grad(g_ref[...])[0])
        lse_ref[...] = m_s[...] + jnp.log2(l)

    return pl.pallas_call(
        body, name=name, grid=(N_HEADS, t // tq),
        in_specs=[pl.BlockSpec((None, tq, QK_PAD), lambda h, i: (h, i, 0)), _gate_block(tq), ANY, ANY],
        out_specs=[pl.BlockSpec((tq, 128), lambda h, i: (i, h)), pl.BlockSpec((tq, 128), lambda h, i: (i, h)),
                   pl.BlockSpec((None, tq, 128), lambda h, i: (h, i, 0))],
        out_shape=[jax.ShapeDtypeStruct((t, N_HEADS * V_DIM), F32), jax.ShapeDtypeStruct((t, N_HEADS * V_DIM), MXU_DTYPE),
                   jax.ShapeDtypeStruct((N_HEADS, t, 128), F32)],
        scratch_shapes=[pltpu.VMEM((t, QK_PAD), MXU_DTYPE), pltpu.VMEM((t, 2 * V_DIM), MXU_DTYPE),
                        pltpu.VMEM((tq, 128), F32), pltpu.VMEM((tq, 2 * V_DIM), F32),
                        pltpu.VMEM((tq, tk), F32), pltpu.VMEM((tq, tk), F32), pltpu.VMEM((tq, 128), F32),
                        pltpu.VMEM((tq, 128), F32), pltpu.VMEM((ratio + 1, tq, tk), F32),
                        pltpu.SemaphoreType.DMA((4,))],
        compiler_params=_cp(2))(qh, z, kh, vh)


def _attn_bwd(qh, kh, vh, o, lse, dycat, z, name, tq=512):
    t = qh.shape[1]
    tq = min(tq, t)
    nq = t // tq

    def body(q_ref, o_ref, lse_ref, dy_ref, g_ref, k_hbm, v_hbm, dq_ref, dgate_ref, dk_hbm, dv_hbm,
             k_v, v_v, dk_acc, dv_acc, dq_acc, delta_s, s_a, dp_a, s_b, dp_b, bias_s, sem):
        h, i = pl.program_id(0), pl.program_id(1)

        loads = _head_loads(k_hbm.at[h], v_hbm.at[h], k_v, v_v, sem, tq)

        @pl.when(i == 0)
        def _():
            for cp in loads:
                cp.start()
            _init_mask_bias(bias_s)
            dk_acc[...] = jnp.zeros(dk_acc.shape, F32)
            dv_acc[...] = jnp.zeros(dv_acc.shape, F32)
            for cp in loads[:2]:
                cp.wait()

        @pl.when(i == 1)
        def _():
            for cp in loads[2:]:
                cp.wait()

        gate, dy, of = g_ref[...], dy_ref[...], o_ref[...]
        silu, dsilu = _silu_and_grad(gate)
        do = dy * silu
        delta = jnp.sum(do * of, axis=-1, keepdims=True)
        dgate_ref[...] = _mx(dy * of * dsilu)
        dob = _mx(do)
        q = q_ref[...]
        delta_s[...] = jnp.broadcast_to(delta, delta_s.shape)
        dq_acc[...] = jnp.zeros(dq_acc.shape, F32)

        def keys(j):
            return pl.ds(pl.multiple_of(j * tq, tq), tq)

        def scores(s_ref, dp_ref, j):
            s = _dot_nt(q, k_v[keys(j), :]) * (SCALE * LOG2E) + bias_s[(j == i).astype(jnp.int32)]
            s_ref[...] = s - jnp.tile(lse_ref[...], (1, tq // 128))
            dp_ref[...] = _dot_nt(dob, v_v[keys(j), :]) - jnp.tile(delta_s[...], (1, tq // 128))

        def grads(s_ref, dp_ref, j):
            ks = keys(j)
            p = jnp.exp2(s_ref[...])
            ds = p * dp_ref[...] * SCALE
            pb, dsb = _mx(p), _mx(ds)
            dq_acc[...] += _dot(dsb, k_v[ks, :])
            dk_acc[ks, :] += _dot_tn(dsb, q)
            dv_acc[ks, :] += _dot_tn(pb, dob)

        scores(s_a, dp_a, 0)

        def pair(pp, carry):
            scores(s_b, dp_b, 2 * pp + 1)
            grads(s_a, dp_a, 2 * pp)
            scores(s_a, dp_a, jnp.minimum(2 * pp + 2, i))
            grads(s_b, dp_b, 2 * pp + 1)
            return carry

        _loop_in_long_trips((i + 1) // 2, pair, longest=8)

        @pl.when(i % 2 == 0)
        def _():
            grads(s_a, dp_a, i)

        dq_ref[...] = dq_acc[...]

        @pl.when(i == nq - 1)
        def _():
            ck = pltpu.make_async_copy(dk_acc, dk_hbm.at[h], sem.at[0])
            cv = pltpu.make_async_copy(dv_acc, dv_hbm.at[h], sem.at[1])
            ck.start()
            cv.start()
            ck.wait()
            cv.wait()

    return pl.pallas_call(
        body, name=name, grid=(N_HEADS, nq),
        in_specs=[pl.BlockSpec((None, tq, QK_PAD), lambda h, i: (h, i, 0)),
                  pl.BlockSpec((tq, 128), lambda h, i: (i, h)),
                  pl.BlockSpec((None, tq, 128), lambda h, i: (h, i, 0)),
                  pl.BlockSpec((tq, 128), lambda h, i: (i, N_HEADS + h)), _gate_block(tq), ANY, ANY],
        out_specs=[pl.BlockSpec((None, tq, QK_PAD), lambda h, i: (h, i, 0)),
                   pl.BlockSpec((tq, 128), lambda h, i: (i, h)), ANY, ANY],
        out_shape=[jax.ShapeDtypeStruct((N_HEADS, t, QK_PAD), F32), jax.ShapeDtypeStruct((t, Z_GATE), MXU_DTYPE),
                   jax.ShapeDtypeStruct((N_HEADS, t, QK_PAD), F32), jax.ShapeDtypeStruct((N_HEADS, t, V_DIM), F32)],
        scratch_shapes=[pltpu.VMEM((t, QK_PAD), MXU_DTYPE), pltpu.VMEM((t, V_DIM), MXU_DTYPE),
                        pltpu.VMEM((t, QK_PAD), F32), pltpu.VMEM((t, V_DIM), F32), pltpu.VMEM((tq, QK_PAD), F32),
                        pltpu.VMEM((tq, 128), F32)] + [pltpu.VMEM((tq, tq), F32)] * 4
        + [pltpu.VMEM((2, tq, tq), F32), pltpu.SemaphoreType.DMA((4,))],
        compiler_params=_cp(2))(qh, o, lse, dycat, z, kh, vh)


def _out_proj_fwd(yab, yc, w, x, g, target, name, tm=512):
    t, d = x.shape
    tm = min(tm, t)
    is_last = target is not None

    def body(*refs):
        if is_last:
            yab_ref, yc_ref, w_ref, x_ref, g_ref, t_ref, y_ref, dout_ref, loss_ref = refs
            _zero_when(pl.program_id(0) == 0, loss_ref)
        else:
            yab_ref, yc_ref, w_ref, x_ref, g_ref, y_ref, out_ref = refs
        y = _dot(jnp.concatenate([yab_ref[...], yc_ref[...]], axis=1), w_ref[...])
        y_ref[...] = y
        out = x_ref[...] + _rms(y, g_ref[...])[0]
        if is_last:
            diff = out - t_ref[...]
            dout_ref[...] = diff * (1.0 / d)
            part = jnp.sum(jnp.sum(diff * diff, axis=-1, keepdims=True), axis=0, keepdims=True) * (0.5 / d)
            _acc(loss_ref, jnp.broadcast_to(part, (1, 128)))
        else:
            out_ref[...] = out

    row = lambda wd: pl.BlockSpec((tm, wd), lambda i: (i, 0))
    fixed = lambda a, b: pl.BlockSpec((a, b), lambda i: (0, 0))
    in_specs = [row(512), row(512), fixed(d, d), row(d), fixed(1, d)]
    args = [yab, yc, w, x, g]
    out_specs = [row(d), row(d)]
    out_shape = [jax.ShapeDtypeStruct((t, d), F32), jax.ShapeDtypeStruct((t, d), F32)]
    if is_last:
        in_specs.append(row(d))
        args.append(target)
        out_specs.append(fixed(1, 128))
        out_shape.append(jax.ShapeDtypeStruct((1, 128), F32))
    return pl.pallas_call(body, name=name, grid=(t // tm,), in_specs=in_specs, out_specs=out_specs,
                          out_shape=out_shape, compiler_params=_cp())(*args)


def _halves(tm):
    half = tm // 2 if tm >= 512 else tm
    return [pl.ds(s, half) for s in range(0, tm, half)]


def _out_proj_bwd(dout, y, yab, yc, w, g, name, tm=1024):
    t, d = y.shape
    tm = min(tm, t)

    def body(dout_ref, y_ref, yab_ref, yc_ref, w_ref, g_ref, dycat_ref, dw_ref, dg_ref):
        _zero_when(pl.program_id(0) == 0, dw_ref, dg_ref)
        dybs = []
        for rows in _halves(tm):
            y = y_ref[rows, :]
            r = lax.rsqrt(jnp.mean(y * y, axis=-1, keepdims=True) + EPS)
            dy, dgt = _rms_bwd(y, r, g_ref[...], dout_ref[rows, :])
            _acc(dg_ref, _colsum(dgt))
            dybs.append(_mx(dy))
        for rows, dyb in zip(_halves(tm), dybs):
            _acc(dw_ref, _dot_tn(jnp.concatenate([yab_ref[rows, :], yc_ref[rows, :]], axis=1), dyb))
            dycat_ref[rows, :] = _dot_nt(dyb, w_ref[...])

    row = lambda wd: pl.BlockSpec((tm, wd), lambda i: (i, 0))
    fixed = lambda a, b: pl.BlockSpec((a, b), lambda i: (0, 0), pipeline_mode=pl.Buffered(1))
    return pl.pallas_call(
        body, name=name, grid=(t // tm,),
        in_specs=[row(d), row(d), row(512), row(512), fixed(d, d), fixed(1, d)],
        out_specs=[row(d), fixed(d, d), fixed(1, d)],
        out_shape=[jax.ShapeDtypeStruct((t, d), F32), jax.ShapeDtypeStruct((d, d), F32),
                   jax.ShapeDtypeStruct((1, d), F32)],
        compiler_params=_cp())(dout, y, yab, yc, w, g)


def _mesh_pos():
    return lax.axis_index("x"), lax.axis_index("y"), lax.axis_index("c")


def _remote(src, dst, send_sem, recv_sem, to):
    return pltpu.make_async_remote_copy(src_ref=src, dst_ref=dst, send_sem=send_sem, recv_sem=recv_sem,
                                        device_id=to, device_id_type=MESH)


CHUNK_ROWS = 256


def _pieces(rows):
    return [(s, min(CHUNK_ROWS, rows - s)) for s in range(0, rows, CHUNK_ROWS)]


def _piece_table(shapes):
    return [(a, s, sz) for a, shp in enumerate(shapes) for s, sz in _pieces(shp[-2])]


def _gather_weights(shards):
    n = len(shards)
    table = _piece_table([s.shape for s in shards])
    npc = len(table)

    def body(*refs):
        ins, outs = refs[:n], refs[n:2 * n]
        send_sems, recv_sems, fwd_send, fwd_recv = refs[2 * n:]
        x, y, c = _mesh_pos()
        me, sibling = (x, y, c), (x, y, 1 - c)
        chips = [(1 - x, y), (x, 1 - y), (1 - x, 1 - y)]
        slot = lambda cx, cy, layer: 2 * (2 * cx + cy) + layer
        first = []
        for a in range(n):
            for j, (cx, cy) in enumerate(chips):
                first.append(_remote(ins[a].at[c], outs[a].at[slot(x, y, c)], send_sems.at[a, j], recv_sems.at[a, j],
                                     (cx, cy, c)))
                first[-1].start()
        passed = []
        for j, (cx, cy) in enumerate(chips):
            for a in range(n):
                blk = outs[a].at[slot(cx, cy, c)]
                _remote(blk, blk, send_sems.at[a, j], recv_sems.at[a, j], me).wait_recv()
            for q, (a, s, sz) in enumerate(table):
                rows = outs[a].at[slot(cx, cy, c), pl.ds(s, sz)]
                passed.append(_remote(rows, rows, fwd_send.at[j, q], fwd_recv.at[j, q], sibling))
                passed[-1].start()
        for j, (cx, cy) in enumerate(chips):
            for q, (a, s, sz) in enumerate(table):
                rows = outs[a].at[slot(cx, cy, 1 - c), pl.ds(s, sz)]
                _remote(rows, rows, fwd_send.at[j, q], fwd_recv.at[j, q], me).wait_recv()
        for cp in first + passed:
            cp.wait_send()

    return pl.pallas_call(
        body, name="gather_weights", in_specs=[ANY] * n, out_specs=[ANY] * n,
        out_shape=[jax.ShapeDtypeStruct((8,) + s.shape[1:], s.dtype) for s in shards],
        scratch_shapes=[pltpu.SemaphoreType.DMA((n, 3)), pltpu.SemaphoreType.DMA((n, 3)),
                        pltpu.SemaphoreType.DMA((3, npc)), pltpu.SemaphoreType.DMA((3, npc))])(*shards)


def _pair_exchange(parts, common):
    n = len(parts)
    table = _piece_table([p.shape for p in parts] + [common.shape])
    npc = len(table)

    def body(*refs):
        ins, outs = refs[:n + 1], refs[n + 1:2 * n + 2]
        send_sems, recv_sems = refs[2 * n + 2:]
        x, y, c = _mesh_pos()
        sent = []
        for k in range(4):
            for q, (a, s, sz) in enumerate(table):
                if a == n and k > 0:
                    continue
                src = ins[a].at[2 * k + 1 - c, pl.ds(s, sz)] if a < n else ins[a].at[pl.ds(s, sz)]
                dst = outs[a].at[k, pl.ds(s, sz)] if a < n else outs[a].at[pl.ds(s, sz)]
                sent.append(_remote(src, dst, send_sems.at[k, q], recv_sems.at[k, q], (x, y, 1 - c)))
                sent[-1].start()
        for k in range(4):
            for q, (a, s, sz) in enumerate(table):
                if a == n and k > 0:
                    continue
                dst = outs[a].at[k, pl.ds(s, sz)] if a < n else outs[a].at[pl.ds(s, sz)]
                _remote(dst, dst, send_sems.at[k, q], recv_sems.at[k, q], (x, y, c)).wait_recv()
        for cp in sent:
            cp.wait_send()

    return pl.pallas_call(
        body, name="grad_pair_exchange", in_specs=[ANY] * (n + 1), out_specs=[ANY] * (n + 1),
        out_shape=[jax.ShapeDtypeStruct((4,) + p.shape[1:], p.dtype) for p in parts]
        + [jax.ShapeDtypeStruct(common.shape, common.dtype)],
        scratch_shapes=[pltpu.SemaphoreType.DMA((4, npc)), pltpu.SemaphoreType.DMA((4, npc))])(*parts, common)


def _chip_exchange(parts, common):
    n = len(parts)
    table = _piece_table([p.shape for p in parts] + [common.shape])
    npc = len(table)

    def body(*refs):
        ins, outs = refs[:n + 1], refs[n + 1:2 * n + 2]
        send_sems, recv_sems = refs[2 * n + 2:]
        x, y, c = _mesh_pos()
        mine = 2 * x + y
        chips = [(1 - x, y), (x, 1 - y), (1 - x, 1 - y)]
        src = lambda a, k: ins[a].at[k] if a < n else ins[a]
        sent = []
        for j, (cx, cy) in enumerate(chips):
            for q, (a, s, sz) in enumerate(table):
                sent.append(_remote(src(a, 2 * cx + cy).at[pl.ds(s, sz)], outs[a].at[mine, pl.ds(s, sz)],
                                    send_sems.at[j, q], recv_sems.at[j, q], (cx, cy, c)))
                sent[-1].start()
        for j, (cx, cy) in enumerate(chips):
            for q, (a, s, sz) in enumerate(table):
                dst = outs[a].at[2 * cx + cy, pl.ds(s, sz)]
                _remote(dst, dst, send_sems.at[j, q], recv_sems.at[j, q], (x, y, c)).wait_recv()
        for cp in sent:
            cp.wait_send()

    return pl.pallas_call(
        body, name="grad_chip_exchange", in_specs=[ANY] * (n + 1), out_specs=[ANY] * (n + 1),
        out_shape=[jax.ShapeDtypeStruct(p.shape, p.dtype) for p in parts]
        + [jax.ShapeDtypeStruct((4,) + common.shape, common.dtype)],
        scratch_shapes=[pltpu.SemaphoreType.DMA((3, npc)), pltpu.SemaphoreType.DMA((3, npc))])(*parts, common)


def _sibling_exchange(sums):
    n = len(sums)
    table = _piece_table([s.shape for s in sums])
    npc = len(table)

    def body(*refs):
        ins, outs = refs[:n], refs[n:2 * n]
        send_sems, recv_sems = refs[2 * n:]
        x, y, c = _mesh_pos()
        sent = []
        for q, (a, s, sz) in enumerate(table):
            sent.append(_remote(ins[a].at[pl.ds(s, sz)], outs[a].at[pl.ds(s, sz)], send_sems.at[q], recv_sems.at[q],
                                (x, y, 1 - c)))
            sent[-1].start()
        for q, (a, s, sz) in enumerate(table):
            dst = outs[a].at[pl.ds(s, sz)]
            _remote(dst, dst, send_sems.at[q], recv_sems.at[q], (x, y, c)).wait_recv()
        for cp in sent:
            cp.wait_send()

    return pl.pallas_call(
        body, name="sibling_exchange", in_specs=[ANY] * n, out_specs=[ANY] * n,
        out_shape=[jax.ShapeDtypeStruct(s.shape, s.dtype) for s in sums],
        scratch_shapes=[pltpu.SemaphoreType.DMA((npc,)), pltpu.SemaphoreType.DMA((npc,))])(*sums)


def _pair_sum(parts, got, name, tr):
    _, r, c = got.shape
    tr = min(tr, r)

    def body(p_ref, g_ref, o_ref, w_ref):
        total = p_ref[...] + g_ref[...]
        o_ref[...] = total
        w_ref[...] = total.astype(WIRE_DTYPE)

    blk = pl.BlockSpec((None, tr, c), lambda k, i: (k, i, 0))
    mine = pl.BlockSpec((None, tr, c), lambda k, i: (2 * k + lax.axis_index("c"), i, 0))
    return pl.pallas_call(
        body, name=name, grid=(4, r // tr), in_specs=[mine, blk], out_specs=[blk, blk],
        out_shape=[jax.ShapeDtypeStruct(got.shape, F32), jax.ShapeDtypeStruct(got.shape, WIRE_DTYPE)],
        compiler_params=_cp(2))(parts, got)


def _sum_chips(pair_sums, recv, name, tr):
    _, r, c = pair_sums.shape
    tr = min(tr, r)

    def body(own_ref, r_ref, o_ref):
        chip = 2 * lax.axis_index("x") + lax.axis_index("y")
        own_blk = own_ref[...]
        acc = jnp.where(chip == 0, own_blk, r_ref[0].astype(F32))
        for k in range(1, 4):
            acc = acc + jnp.where(chip == k, own_blk, r_ref[k].astype(F32))
        o_ref[...] = acc

    return pl.pallas_call(
        body, name=name, grid=(r // tr,),
        in_specs=[pl.BlockSpec((None, tr, c), lambda i: (2 * lax.axis_index("x") + lax.axis_index("y"), i, 0)),
                  pl.BlockSpec((4, tr, c), lambda i: (0, i, 0))],
        out_specs=pl.BlockSpec((tr, c), lambda i: (i, 0)),
        out_shape=jax.ShapeDtypeStruct((r, c), F32), compiler_params=_cp())(pair_sums, recv)


def _sum_leading(parts, name, tr):
    nlead, r, c = parts.shape
    tr = min(tr, r)

    def body(p_ref, o_ref):
        acc = p_ref[0]
        for j in range(1, nlead):
            acc = acc + p_ref[j]
        o_ref[...] = acc

    return pl.pallas_call(
        body, name=name, grid=(r // tr,),
        in_specs=[pl.BlockSpec((nlead, tr, c), lambda i: (0, i, 0))], out_specs=pl.BlockSpec((tr, c), lambda i: (i, 0)),
        out_shape=jax.ShapeDtypeStruct((r, c), parts.dtype), compiler_params=_cp())(parts)


def _adamw_update(w_ref, g_ref, m_ref, v_ref, d_ref, nm_ref, nv_ref):
    gg = g_ref[...]
    nm = ADAM_B1 * m_ref[...] + (1.0 - ADAM_B1) * gg
    nv = ADAM_B2 * v_ref[...] + (1.0 - ADAM_B2) * jnp.square(gg)
    m_hat = nm / (1.0 - ADAM_B1 ** ADAM_STEP)
    v_hat = nv / (1.0 - ADAM_B2 ** ADAM_STEP)
    d_ref[...] = -ADAM_LR * (m_hat / (jnp.sqrt(v_hat) + ADAM_EPS) + ADAM_WD * w_ref[...])
    nm_ref[...] = nm
    nv_ref[...] = nv


def _adamw(w, g, m, v, name, tr):
    r, c = w.shape
    tr = min(tr, r)

    def body(w_ref, g_ref, m_ref, v_ref, d_ref, nm_ref, nv_ref):
        _adamw_update(w_ref, g_ref, m_ref, v_ref, d_ref, nm_ref, nv_ref)

    blk = pl.BlockSpec((tr, c), lambda i: (i, 0))
    return pl.pallas_call(
        body, name=name, grid=(r // tr,), in_specs=[blk] * 4, out_specs=[blk] * 3,
        out_shape=[jax.ShapeDtypeStruct((r, c), F32)] * 3, compiler_params=_cp())(w, g, m, v)


def _adamw_many(ws, gs, ms, vs, name):
    n = len(ws)

    def body(*refs):
        ins, outs = refs[:4 * n], refs[4 * n:]
        for k in range(n):
            _adamw_update(ins[k], ins[n + k], ins[2 * n + k], ins[3 * n + k], outs[k], outs[n + k], outs[2 * n + k])

    vmem = pl.BlockSpec(memory_space=pltpu.VMEM)
    res = pl.pallas_call(
        body, name=name, in_specs=[vmem] * (4 * n), out_specs=[vmem] * (3 * n),
        out_shape=[jax.ShapeDtypeStruct(a.shape, F32) for a in ws] * 3)(*ws, *gs, *ms, *vs)
    return res[:n], res[n:2 * n], res[2 * n:]


def _rope_tables(positions):
    inv_freq = ROPE_BASE ** (-jnp.arange(0, 64, 2, dtype=F32) / 64)
    ang = positions.astype(F32)[:, None] * inv_freq
    cos, sin = jnp.cos(ang), jnp.sin(ang)
    t = positions.shape[0]
    rc = jnp.concatenate([jnp.ones((t, 128), F32), cos, cos, jnp.ones((t, 64), F32)], axis=1)
    rs = jnp.concatenate([jnp.zeros((t, 128), F32), sin, sin, jnp.zeros((t, 64), F32)], axis=1)
    return rc, rs


def _layer_params(l, w_in, w_uq, w_ukv, w_out, small):
    p = {}
    p["w_in"] = jnp.concatenate([w_in[l][:, :1984], jnp.zeros((1024, 64), w_in.dtype), w_in[l][:, 1984:]], axis=1)
    p["w_uq"] = jnp.pad(w_uq[l].reshape(384, 4, 192), ((0, 0), (0, 0), (0, 64))).reshape(384, 1024)
    p["w_ukv"] = w_ukv[l].reshape(256, 4, 2, 128).transpose(0, 2, 1, 3).reshape(256, 1024)
    p["w_out"] = w_out[l]
    p["pre_g"] = small["pre_norm_g"][l][None]
    p["post_g"] = small["post_norm_g"][l][None]
    p["sgu_w"] = small["sgu_w"][l].reshape(512, 128)
    p["sgu_wt"] = small["sgu_w"][l].transpose(0, 2, 1).reshape(512, 128)
    p["sgu_bias"] = jnp.repeat(small["sgu_b"][l].T, 64, axis=1)
    p["ln_g"] = small["sgu_ln_g"][l][None]
    p["ln_b"] = small["sgu_ln_b"][l][None]
    p["pool_wbd"] = _mx(jax.scipy.linalg.block_diag(*[small["pool_w"][l][gi] for gi in range(4)]))
    p["pool_scale"] = small["pool_scale"][l][None]
    p["gq"] = small["q_norm_g"][l][None]
    p["gkv"] = small["kv_norm_g"][l][None]
    return p


def _layer_fwd(l, x, p, rc, rs, target):
    z, h = _in_proj_fwd(x, p["pre_g"], p["w_in"], f"in_proj_fwd_{l}")
    yab = _mix_fwd(z, p["sgu_w"], p["sgu_bias"], p["ln_g"], p["ln_b"], p["pool_wbd"], p["pool_scale"], f"mix_fwd_{l}")
    qh, kh, vh = _qkv_fwd(z, rc, rs, p["w_uq"], p["w_ukv"], p["gq"], p["gkv"], f"qkv_fwd_{l}")
    o, yc, lse = _attn_fwd(qh, kh, vh, z, f"attn_fwd_{l}")
    outs = _out_proj_fwd(yab, yc, p["w_out"], x, p["post_g"], target, f"out_proj_fwd_{l}")
    saved = dict(x=x, z=z, h=h, yab=yab, qh=qh, kh=kh, vh=vh, o=o, yc=yc, lse=lse, y=outs[0])
    return saved, outs[1:]


def _layer_bwd(l, dout, sv, p, rc, rs):
    dycat, dw_out, dpost = _out_proj_bwd(dout, sv["y"], sv["yab"], sv["yc"], p["w_out"], p["post_g"], f"out_proj_bwd_{l}")
    dq, dgate, dk, dv = _attn_bwd(sv["qh"], sv["kh"], sv["vh"], sv["o"], sv["lse"], dycat, sv["z"], f"attn_bwd_{l}")
    dzc, dzk, dwq, dwkv, dgq, dgkv = _qkv_bwd(dq, dk, dv, sv["z"], rc, rs, p["w_uq"], p["w_ukv"], p["gq"], p["gkv"],
                                              f"qkv_bwd_{l}")
    dzm, dsw, dsb, dlng, dlnb, dpw, dps = _mix_bwd(sv["z"], dycat, p["sgu_w"], p["sgu_wt"], p["sgu_bias"], p["ln_g"],
                                                   p["ln_b"], p["pool_wbd"], p["pool_scale"], f"mix_bwd_{l}")
    dx, dw_in, dpre = _in_proj_bwd(dzm, dzc, dzk, dgate, sv["h"], sv["x"], dout, p["w_in"], p["pre_g"], f"in_proj_bwd_{l}")
    grads = {
        "pre_norm_g": dpre[0], "post_norm_g": dpost[0],
        "w_in": jnp.concatenate([dw_in[:, :1984], dw_in[:, 2048:]], axis=1),
        "sgu_w": dsw.reshape(4, 128, 128), "sgu_b": dsb[:, :4].T, "sgu_ln_g": dlng[0], "sgu_ln_b": dlnb[0],
        "pool_w": jnp.stack([dpw[64 * gi:64 * gi + 64, 64 * gi:64 * gi + 64] for gi in range(4)]),
        "pool_scale": dps[0], "q_norm_g": dgq[0],
        "w_uq": dwq.reshape(384, 4, 256)[:, :, :192].reshape(384, 768), "kv_norm_g": dgkv[0],
        "w_ukv": dwkv.reshape(256, 2, 4, 128).transpose(0, 2, 1, 3).reshape(256, 1024), "w_out": dw_out,
    }
    return dx, grads


SMALL_NAMES = ["pre_norm_g", "post_norm_g", "sgu_w", "sgu_b", "sgu_ln_g", "sgu_ln_b", "pool_w", "pool_scale",
               "q_norm_g", "kv_norm_g"]
BIG_NAMES = ["w_in", "w_uq", "w_ukv", "w_out"]
WEIGHT_NAMES = ["pre_norm_g", "post_norm_g", "w_in", "sgu_w", "sgu_b", "sgu_ln_g", "sgu_ln_b", "pool_w", "pool_scale",
                "q_norm_g", "w_uq", "kv_norm_g", "w_ukv", "w_out"]


def _local_step(x, positions, target, w_in, w_uq, w_ukv, w_out, small):
    rc, rs = _rope_tables(positions)
    params = [_layer_params(l, w_in, w_uq, w_ukv, w_out, small) for l in range(DEPTH)]
    saved = []
    for l in range(DEPTH):
        sv, outs = _layer_fwd(l, x, params[l], rc, rs, target if l == DEPTH - 1 else None)
        saved.append(sv)
        if l < DEPTH - 1:
            x = outs[0]
    dout, loss = outs
    grads = [None] * DEPTH
    for l in reversed(range(DEPTH)):
        dout, grads[l] = _layer_bwd(l, dout, saved[l], params[l], rc, rs)
    return loss[0, 0], dout, {k: jnp.stack([grads[l][k] for l in range(DEPTH)]) for k in WEIGHT_NAMES}


def _pack_small(tree, extra=None):
    pieces = [tree[k].reshape(-1) for k in SMALL_NAMES]
    pieces.append(jnp.zeros((1,), F32) if extra is None else extra.reshape(1))
    flat = jnp.concatenate(pieces)
    rows = -(-flat.shape[0] // 1024) * 8
    return jnp.pad(flat, (0, rows * 128 - flat.shape[0])).reshape(rows, 128)


def _unpack_small(packed, like):
    flat = packed.reshape(-1)
    out, off = {}, 0
    for k in SMALL_NAMES:
        size = like[k].size
        out[k] = flat[off:off + size].reshape(like[k].shape)
        off += size
    return out, flat[off]


def kernel(x, positions, pre_norm_g, post_norm_g, w_in, sgu_w, sgu_b, sgu_ln_g, sgu_ln_b, pool_w, pool_scale, q_norm_g, w_uq, kv_norm_g, w_ukv, w_out, loss_target, m_pre_norm_g, m_post_norm_g, m_w_in, m_sgu_w, m_sgu_b, m_sgu_ln_g, m_sgu_ln_b, m_pool_w, m_pool_scale, m_q_norm_g, m_w_uq, m_kv_norm_g, m_w_ukv, m_w_out, v_pre_norm_g, v_post_norm_g, v_w_in, v_sgu_w, v_sgu_b, v_sgu_ln_g, v_sgu_ln_b, v_pool_w, v_pool_scale, v_q_norm_g, v_w_uq, v_kv_norm_g, v_w_ukv, v_w_out):
    w = dict(pre_norm_g=pre_norm_g, post_norm_g=post_norm_g, w_in=w_in, sgu_w=sgu_w, sgu_b=sgu_b, sgu_ln_g=sgu_ln_g,
             sgu_ln_b=sgu_ln_b, pool_w=pool_w, pool_scale=pool_scale, q_norm_g=q_norm_g, w_uq=w_uq, kv_norm_g=kv_norm_g,
             w_ukv=w_ukv, w_out=w_out)
    m = dict(pre_norm_g=m_pre_norm_g, post_norm_g=m_post_norm_g, w_in=m_w_in, sgu_w=m_sgu_w, sgu_b=m_sgu_b,
             sgu_ln_g=m_sgu_ln_g, sgu_ln_b=m_sgu_ln_b, pool_w=m_pool_w, pool_scale=m_pool_scale, q_norm_g=m_q_norm_g,
             w_uq=m_w_uq, kv_norm_g=m_kv_norm_g, w_ukv=m_w_ukv, w_out=m_w_out)
    v = dict(pre_norm_g=v_pre_norm_g, post_norm_g=v_post_norm_g, w_in=v_w_in, sgu_w=v_sgu_w, sgu_b=v_sgu_b,
             sgu_ln_g=v_sgu_ln_g, sgu_ln_b=v_sgu_ln_b, pool_w=v_pool_w, pool_scale=v_pool_scale, q_norm_g=v_q_norm_g,
             w_uq=v_w_uq, kv_norm_g=v_kv_norm_g, w_ukv=v_w_ukv, w_out=v_w_out)

    core = lax.axis_index("c")
    chip = 2 * lax.axis_index("x") + lax.axis_index("y")
    shards = [_mx(w[k]) for k in BIG_NAMES]
    gathered = _gather_weights(shards)
    g_in, g_uq, g_ukv, g_out = [lax.dynamic_update_slice(g, s, (2 * chip, 0, 0)) for g, s in zip(gathered, shards)]
    cols = lambda g: g.reshape((4, 2) + g.shape[1:]).transpose(1, 2, 0, 3).reshape(2, g.shape[1], 4 * g.shape[2])
    full_out = g_out.reshape(4, 2, 256, 1024).transpose(1, 0, 2, 3).reshape(2, 1024, 1024)
    loss, dx, grads = _local_step(x[0], positions[0], loss_target[0], cols(g_in), cols(g_uq), cols(g_ukv), full_out, w)

    split_cols = lambda g: g.reshape(2, g.shape[1], 4, g.shape[2] // 4).transpose(2, 0, 1, 3).reshape(8, g.shape[1], g.shape[2] // 4)
    parts = [split_cols(grads["w_in"]), split_cols(grads["w_uq"]), split_cols(grads["w_ukv"]),
             grads["w_out"].reshape(2, 4, 256, 1024).transpose(1, 0, 2, 3).reshape(8, 256, 1024)]
    common = _pack_small(grads, loss)
    got = _pair_exchange(parts, common)
    pair_sums = [_pair_sum(parts[a], got[a], f"pair_sum_{BIG_NAMES[a]}", 512) for a in range(4)]
    chip_common = _sum_leading(jnp.stack([common, got[4]]), "pair_sum_small", common.shape[0])
    received = _chip_exchange([ps[1] for ps in pair_sums], chip_common)
    sums = [_sum_chips(pair_sums[a][0], received[a], f"sum_{BIG_NAMES[a]}", 512) for a in range(4)]
    all_common = lax.dynamic_update_slice(received[4], chip_common[None], (chip, 0, 0))
    small_sum, loss = _unpack_small(_sum_leading(all_common, "sum_small", all_common.shape[1]), w)
    others = _sibling_exchange(sums)
    total = dict(small_sum)
    for a, k in enumerate(BIG_NAMES):
        total[k] = jnp.where(core == 0, jnp.stack([sums[a], others[a]]), jnp.stack([others[a], sums[a]]))

    rows2d = lambda a: a.reshape(-1, a.shape[-1])
    small_out = _adamw_many(*[[rows2d(tree[k]) for k in SMALL_NAMES] for tree in (w, total, m, v)], "adamw_small")
    delta, new_m, new_v = ({k: r.reshape(w[k].shape) for k, r in zip(SMALL_NAMES, res)} for res in small_out)
    for k in BIG_NAMES:
        shape = w[k].shape
        flat = lambda a: a.reshape(shape[0] * shape[1], shape[2])
        res = _adamw(flat(w[k]), flat(total[k]), flat(m[k]), flat(v[k]), f"adamw_{k}", 256)
        delta[k], new_m[k], new_v[k] = (r.reshape(shape) for r in res)

    return (loss, dx[None], *[total[k] for k in WEIGHT_NAMES], *[delta[k] for k in WEIGHT_NAMES],
            *[new_m[k] for k in WEIGHT_NAMES], *[new_v[k] for k in WEIGHT_NAMES])
```

```python
import jax
import jax.numpy as jnp
from jax import lax
from jax.experimental import pallas as pl
from jax.experimental.pallas import tpu as pltpu

F32 = jnp.float32
MXU_DTYPE = jnp.bfloat16
WIRE_DTYPE = jnp.bfloat16
EPS = 1e-6
NEG_INF = -1e30
CHUNK = 64
DEPTH = 2
N_HEADS = 4
QK_PAD = 256
V_DIM = 128
SCALE = 192 ** -0.5
LOG2E = 1.4426950408889634
ROPE_BASE = 10000.0
ADAM_LR, ADAM_B1, ADAM_B2, ADAM_EPS, ADAM_WD, ADAM_STEP = 0.001, 0.9, 0.999, 1e-08, 0.01, 10
VMEM_LIMIT_BYTES = 56 * 1024 * 1024
MESH = pl.DeviceIdType.MESH
ANY = pl.BlockSpec(memory_space=pl.ANY)

Z_MIX, Z_C, Z_KR, Z_GATE = 1280, 640, 128, 512
Z_W = Z_MIX + Z_C + Z_KR + Z_GATE


def _cp(n_axes=1):
    return pltpu.CompilerParams(dimension_semantics=("arbitrary",) * n_axes, vmem_limit_bytes=VMEM_LIMIT_BYTES)


def _dot(a, b):
    return lax.dot_general(a, b, (((1,), (0,)), ((), ())), preferred_element_type=F32)


def _dot_nt(a, b):
    return lax.dot_general(a, b, (((1,), (1,)), ((), ())), preferred_element_type=F32)


def _dot_tn(a, b):
    return lax.dot_general(a, b, (((0,), (0,)), ((), ())), preferred_element_type=F32)


def _mx(a):
    return a.astype(MXU_DTYPE)


def _silu_and_grad(g):
    sg = jax.nn.sigmoid(g)
    return g * sg, sg * (1.0 + g * (1.0 - sg))


def _rms(x, g):
    r = lax.rsqrt(jnp.mean(x * x, axis=-1, keepdims=True) + EPS)
    return x * r * g, r


def _rms_bwd(x, r, g, dy):
    xhat = x * r
    dyg = dy * g
    dx = r * (dyg - xhat * jnp.mean(dyg * xhat, axis=-1, keepdims=True))
    return dx, dy * xhat


def _zero_when(first, *refs):
    @pl.when(first)
    def _():
        for ref in refs:
            ref[...] = jnp.zeros(ref.shape, ref.dtype)


def _acc(ref, val):
    ref[...] += val


def _colsum(a):
    return jnp.sum(a, axis=0, keepdims=True)


def _in_proj_fwd(x, g, w, name, tm=1024):
    t, d = x.shape
    n = w.shape[1]
    tm = min(tm, t)

    def body(x_ref, g_ref, w_ref, z_ref, h_ref):
        hs = []
        for rows in _halves(tm):
            h, _ = _rms(x_ref[rows, :], g_ref[...])
            hs.append(_mx(h))
            h_ref[rows, :] = hs[-1]
        for rows, h in zip(_halves(tm), hs):
            z_ref[rows, :] = _dot(h, w_ref[...])

    return pl.pallas_call(
        body, name=name, grid=(t // tm,),
        in_specs=[pl.BlockSpec((tm, d), lambda i: (i, 0)), pl.BlockSpec((1, d), lambda i: (0, 0)),
                  pl.BlockSpec((d, n), lambda i: (0, 0), pipeline_mode=pl.Buffered(1))],
        out_specs=[pl.BlockSpec((tm, n), lambda i: (i, 0)), pl.BlockSpec((tm, d), lambda i: (i, 0))],
        out_shape=[jax.ShapeDtypeStruct((t, n), F32), jax.ShapeDtypeStruct((t, d), MXU_DTYPE)],
        compiler_params=_cp())(x, g, w)


def _in_proj_bwd(dz_mix, dz_c, dz_kr, dz_gate, h, x, d_res, w, g, name, tm=512):
    t, d = x.shape
    n = w.shape[1]
    tm = min(tm, t)

    def body(dm_ref, dc_ref, dk_ref, dg_ref, h_ref, x_ref, dres_ref, w_ref, g_ref, dx_ref, dw_ref, dgn_ref):
        first = pl.program_id(0) == 0
        dz = jnp.concatenate([dm_ref[...], dc_ref[...], dk_ref[...], dg_ref[...]], axis=1)

        _zero_when(first, dw_ref, dgn_ref)
        dh = _dot_nt(dz, w_ref[...])
        hb = h_ref[...]
        for c0 in range(0, n, 512):
            dw_ref[:, c0:c0 + 512] += _dot_tn(hb, dz[:, c0:c0 + 512])
        xf = x_ref[...]
        r = lax.rsqrt(jnp.mean(xf * xf, axis=-1, keepdims=True) + EPS)
        dx, dgt = _rms_bwd(xf, r, g_ref[...], dh)
        dx_ref[...] = dx + dres_ref[...]
        _acc(dgn_ref, _colsum(dgt))

    row = lambda wd: pl.BlockSpec((tm, wd), lambda i: (i, 0))
    fixed = lambda a, b: pl.BlockSpec((a, b), lambda i: (0, 0), pipeline_mode=pl.Buffered(1))
    return pl.pallas_call(
        body, name=name, grid=(t // tm,),
        in_specs=[row(Z_MIX), row(Z_C), row(Z_KR), row(Z_GATE), row(d), row(d), row(d), fixed(d, n), fixed(1, d)],
        out_specs=[row(d), fixed(d, n), fixed(1, d)],
        out_shape=[jax.ShapeDtypeStruct((t, d), F32), jax.ShapeDtypeStruct((d, n), F32),
                   jax.ShapeDtypeStruct((1, d), F32)],
        compiler_params=_cp())(dz_mix, dz_c, dz_kr, dz_gate, h, x, d_res, w, g)


def _lane_group(shape):
    return lax.broadcasted_iota(jnp.int32, shape, 1) // 64


def _select_group(vals):
    grp = _lane_group(vals[0].shape)
    out = vals[3]
    for gi in (2, 1, 0):
        out = jnp.where(grp == gi, vals[gi], out)
    return out


def _sgu_mask(transposed):
    r = (lax.broadcasted_iota(jnp.int32, (512, 128), 0) % 128) // CHUNK
    c = lax.broadcasted_iota(jnp.int32, (512, 128), 1) // CHUNK
    return (r <= c) if transposed else (c <= r)


def _sgu_apply(wstack, vb, nblk):
    outs = []
    for n in range(nblk):
        r = _dot(wstack, vb[n * 128:(n + 1) * 128, :])
        outs.append(_select_group([r[hh * 128:(hh + 1) * 128, :] for hh in range(4)]))
    return jnp.concatenate(outs, axis=0)


def _layer_norm(v, g, b):
    mu = jnp.mean(v, axis=-1, keepdims=True)
    vc = v - mu
    rstd = lax.rsqrt(jnp.mean(vc * vc, axis=-1, keepdims=True) + EPS)
    vhat = vc * rstd
    return vhat * g + b, vhat, rstd


def _div_by_counts(x, t0):
    t = t0 + lax.broadcasted_iota(jnp.int32, (16, 256), 0)
    w = _select_group([jnp.full((16, 256), wv, jnp.int32) for wv in (2, 4, 8, 16)])
    head = x[:16, :] / jnp.minimum(t + 1, w).astype(F32)
    inv_w = _select_group([jnp.full((1, 256), 1.0 / wv, F32) for wv in (2, 4, 8, 16)])
    return jnp.concatenate([head, x[16:, :] * inv_w], axis=0)


def _pooled(p, halo, t0):
    tm = p.shape[0]
    ext = jnp.concatenate([halo, p], axis=0)
    s2 = ext + pltpu.roll(ext, 1, 0)
    s4 = s2 + pltpu.roll(s2, 2, 0)
    s8 = s4 + pltpu.roll(s4, 4, 0)
    s16 = s8 + pltpu.roll(s8, 8, 0)
    sel = _select_group([s2, s4, s8, s16])[16:, :]
    return _div_by_counts(sel, t0) - p


def _pooled_bwd(dpool, dpool_halo, t0):
    tm = dpool.shape[0]
    n = tm + 16
    ext = _div_by_counts(jnp.concatenate([dpool, dpool_halo], axis=0), t0)
    f2 = ext + pltpu.roll(ext, n - 1, 0)
    f4 = f2 + pltpu.roll(f2, n - 2, 0)
    f8 = f4 + pltpu.roll(f4, n - 4, 0)
    f16 = f8 + pltpu.roll(f8, n - 8, 0)
    return _select_group([f2, f4, f8, f16])[:tm, :] - dpool


def _mix_specs(t, tm):
    nt16 = t // 16
    zrow = pl.BlockSpec((tm, Z_MIX), lambda i: (i, 0))
    prev_halo = pl.BlockSpec((16, 256), lambda i: (jnp.maximum(i * (tm // 16) - 1, 0), 3))
    fixed = lambda a, b: pl.BlockSpec((a, b), lambda i: (0, 0))
    params = [fixed(512, 128), fixed(128, 256), fixed(1, 256), fixed(1, 256), fixed(256, 256), fixed(1, 256)]
    return nt16, zrow, prev_halo, fixed, params


def _mix_fwd(z, sgu_w, sgu_bias, ln_g, ln_b, pool_wbd, pool_scale, name, tm=512):
    t = z.shape[0]
    tm = min(tm, t)
    _, zrow, prev_halo, _, params = _mix_specs(t, tm)

    def body(z_ref, halo_ref, w_ref, bias_ref, lng_ref, lnb_ref, pw_ref, ps_ref, y_ref):
        i = pl.program_id(0)
        u, v, gate = z_ref[:, 0:256], z_ref[:, 256:512], z_ref[:, 512:768]
        p, pgate = z_ref[:, 768:1024], z_ref[:, 1024:1280]
        vn, _, _ = _layer_norm(v, lng_ref[...], lnb_ref[...])
        wm = _mx(jnp.where(_sgu_mask(False), w_ref[...], 0.0))
        halo = jnp.where(i > 0, halo_ref[...], 0.0)
        pooled = _pooled(p, halo, i * tm)
        mixed = _sgu_apply(wm, _mx(vn), tm // 128) + jnp.tile(bias_ref[...], (tm // 128, 1))
        mixedp = _dot(_mx(pooled), pw_ref[...])
        ya = u * mixed * _silu_and_grad(gate)[0]
        yb = mixedp * ps_ref[...] * _silu_and_grad(pgate)[0]
        y_ref[...] = _mx(jnp.concatenate([ya, yb], axis=1))

    return pl.pallas_call(
        body, name=name, grid=(t // tm,),
        in_specs=[zrow, prev_halo] + params,
        out_specs=pl.BlockSpec((tm, 512), lambda i: (i, 0)),
        out_shape=jax.ShapeDtypeStruct((t, 512), MXU_DTYPE),
        compiler_params=_cp())(z, z, sgu_w, sgu_bias, ln_g, ln_b, pool_wbd, pool_scale)


def _mix_bwd(z, dycat, sgu_w, sgu_wt, sgu_bias, ln_g, ln_b, pool_wbd, pool_scale, name, tm=512):
    t = z.shape[0]
    tm = min(tm, t)
    nt16, zrow, prev_halo, fixed, params = _mix_specs(t, tm)
    nblk = tm // 128
    last = t // tm - 1

    def body(z_ref, halo_ref, zn_ref, dy_ref, dyn_ref, w_ref, wt_ref, bias_ref, lng_ref, lnb_ref, pw_ref, ps_ref,
             dz_ref, dw_ref, db_ref, dlng_ref, dlnb_ref, dpw_ref, dps_ref):
        i = pl.program_id(0)
        _zero_when(i == 0, dw_ref, db_ref, dlng_ref, dlnb_ref, dpw_ref, dps_ref)
        u, v, gate = z_ref[:, 0:256], z_ref[:, 256:512], z_ref[:, 512:768]
        p, pgate = z_ref[:, 768:1024], z_ref[:, 1024:1280]
        dya, dyb = dy_ref[:, 0:256], dy_ref[:, 256:512]
        vn, vhat, rstd = _layer_norm(v, lng_ref[...], lnb_ref[...])
        vnb = _mx(vn)
        wm = _mx(jnp.where(_sgu_mask(False), w_ref[...], 0.0))
        wmt = _mx(jnp.where(_sgu_mask(True), wt_ref[...], 0.0))
        halo = jnp.where(i > 0, halo_ref[...], 0.0)
        pooled_b = _mx(_pooled(p, halo, i * tm))
        silu, dsilu = _silu_and_grad(gate)
        psilu, pdsilu = _silu_and_grad(pgate)
        mixed = _sgu_apply(wm, vnb, nblk) + jnp.tile(bias_ref[...], (nblk, 1))
        mixedp = _dot(pooled_b, pw_ref[...])
        t1 = u * mixed
        d_gate = dya * t1 * dsilu
        d_t1 = dya * silu
        d_u = d_t1 * mixed
        d_mixed = d_t1 * u
        dmb = _mx(d_mixed)
        d_pgate = dyb * (mixedp * ps_ref[...]) * pdsilu
        d_ms = dyb * psilu
        _acc(dps_ref, _colsum(d_ms * mixedp))
        dmpb = _mx(d_ms * ps_ref[...])
        dmp_halo = _mx(dyn_ref[...] * _silu_and_grad(zn_ref[...])[0] * ps_ref[...])
        d_vn = _sgu_apply(wmt, dmb, nblk)
        grp = _lane_group((128, 256))
        lane = lax.broadcasted_iota(jnp.int32, (128, 128), 1)
        dws = [jnp.zeros((128, 128), F32) for _ in range(4)]
        dbias = jnp.zeros((128, 128), F32)
        for n in range(nblk):
            dm_n, dmb_n, vnb_n = d_mixed[n * 128:(n + 1) * 128], dmb[n * 128:(n + 1) * 128], vnb[n * 128:(n + 1) * 128]
            for hh in range(4):
                dws[hh] = dws[hh] + _dot_nt(jnp.where(grp == hh, dmb_n, jnp.zeros_like(dmb_n)), vnb_n)
                rs = jnp.sum(jnp.where(grp == hh, dm_n, 0.0), axis=-1, keepdims=True)
                dbias = dbias + jnp.where(lane == hh, rs, 0.0)
        _acc(dw_ref, jnp.concatenate(dws, axis=0))
        _acc(db_ref, dbias)
        _acc(dpw_ref, _dot_tn(pooled_b, dmpb))
        d_pooled = _dot_nt(dmpb, pw_ref[...])
        d_pooled_halo = jnp.where(i < last, _dot_nt(dmp_halo, pw_ref[...]), 0.0)
        _acc(dlng_ref, _colsum(d_vn * vhat))
        _acc(dlnb_ref, _colsum(d_vn))
        dvh = d_vn * lng_ref[...]
        d_v = rstd * (dvh - jnp.mean(dvh, axis=-1, keepdims=True) - vhat * jnp.mean(dvh * vhat, axis=-1, keepdims=True))
        d_p = _pooled_bwd(d_pooled, d_pooled_halo, i * tm)
        dz_ref[...] = _mx(jnp.concatenate([d_u, d_v, d_gate, d_p, d_pgate], axis=1))

        @pl.when(i == last)
        def _():
            dw_ref[...] = jnp.where(_sgu_mask(False), dw_ref[...], 0.0)

    nxt = lambda i: jnp.minimum((i + 1) * (tm // 16), nt16 - 1)
    return pl.pallas_call(
        body, name=name, grid=(t // tm,),
        in_specs=[zrow, prev_halo, pl.BlockSpec((16, 256), lambda i: (nxt(i), 4)),
                  pl.BlockSpec((tm, 512), lambda i: (i, 0)), pl.BlockSpec((16, 256), lambda i: (nxt(i), 1)),
                  params[0], fixed(512, 128)] + params[1:],
        out_specs=[pl.BlockSpec((tm, Z_MIX), lambda i: (i, 0)), fixed(512, 128), fixed(128, 128), fixed(1, 256),
                   fixed(1, 256), fixed(256, 256), fixed(1, 256)],
        out_shape=[jax.ShapeDtypeStruct((t, Z_MIX), MXU_DTYPE), jax.ShapeDtypeStruct((512, 128), F32),
                   jax.ShapeDtypeStruct((128, 128), F32), jax.ShapeDtypeStruct((1, 256), F32),
                   jax.ShapeDtypeStruct((1, 256), F32), jax.ShapeDtypeStruct((256, 256), F32),
                   jax.ShapeDtypeStruct((1, 256), F32)],
        compiler_params=_cp())(z, z, z, dycat, dycat, sgu_w, sgu_wt, sgu_bias, ln_g, ln_b, pool_wbd, pool_scale)


def _rot_half(x, transpose):
    w = x.shape[1]
    lane = lax.broadcasted_iota(jnp.int32, x.shape, 1) % min(w, 256)
    base = 128 if w >= 256 else 0
    lo = jnp.logical_and(lane >= base, lane < base + 32)
    hi = jnp.logical_and(lane >= base + 32, lane < base + 64)
    up = pltpu.roll(x, w - 32, 1)
    down = pltpu.roll(x, 32, 1)
    if transpose:
        return jnp.where(lo, up, jnp.where(hi, -down, 0.0))
    return jnp.where(lo, -up, jnp.where(hi, down, 0.0))


def _rope(x, c, s):
    return x * c + _rot_half(x, False) * s


def _rope_bwd(dy, c, s):
    return dy * c + _rot_half(dy * s, True)


def _qkv_fwd(z, rc, rs, w_uq, w_ukv, gq, gkv, name, tm=512):
    t = z.shape[0]
    tm = min(tm, t)

    def body(zc_ref, zk_ref, rc_ref, rs_ref, wq_ref, wkv_ref, gq_ref, gkv_ref, q_ref, k_ref, v_ref):
        cq, ckv = zc_ref[:, 0:384], zc_ref[:, 384:640]
        c, s = rc_ref[...], rs_ref[...]
        qn, _ = _rms(cq, gq_ref[...])
        kvn, _ = _rms(ckv, gkv_ref[...])
        q_pre = _dot(_mx(qn), wq_ref[...])
        kv = _dot(_mx(kvn), wkv_ref[...])
        kpe = _rope(zk_ref[...], c[:, 128:256], s[:, 128:256])
        q = _rope(q_pre, jnp.tile(c, (1, N_HEADS)), jnp.tile(s, (1, N_HEADS)))
        for hh in range(N_HEADS):
            q_ref[hh] = _mx(q[:, hh * QK_PAD:(hh + 1) * QK_PAD])
            k_ref[hh] = _mx(jnp.concatenate([kv[:, hh * 128:(hh + 1) * 128], kpe], axis=1))
            v_ref[hh] = _mx(kv[:, 512 + hh * 128:512 + (hh + 1) * 128])

    fixed = lambda a, b: pl.BlockSpec((a, b), lambda i: (0, 0))
    heads = lambda wd: pl.BlockSpec((N_HEADS, tm, wd), lambda i: (0, i, 0))
    return pl.pallas_call(
        body, name=name, grid=(t // tm,),
        in_specs=[pl.BlockSpec((tm, Z_C), lambda i: (i, Z_MIX // Z_C)),
                  pl.BlockSpec((tm, Z_KR), lambda i: (i, (Z_MIX + Z_C) // Z_KR)),
                  pl.BlockSpec((tm, 256), lambda i: (i, 0)), pl.BlockSpec((tm, 256), lambda i: (i, 0)),
                  fixed(384, 1024), fixed(256, 1024), fixed(1, 384), fixed(1, 256)],
        out_specs=[heads(QK_PAD), heads(QK_PAD), heads(V_DIM)],
        out_shape=[jax.ShapeDtypeStruct((N_HEADS, t, QK_PAD), MXU_DTYPE),
                   jax.ShapeDtypeStruct((N_HEADS, t, QK_PAD), MXU_DTYPE),
                   jax.ShapeDtypeStruct((N_HEADS, t, V_DIM), MXU_DTYPE)],
        compiler_params=_cp())(z, z, rc, rs, w_uq, w_ukv, gq, gkv)


def _qkv_bwd(dq, dk, dv, z, rc, rs, w_uq, w_ukv, gq, gkv, name, tm=512):
    t = z.shape[0]
    tm = min(tm, t)

    def body(dq_ref, dk_ref, dv_ref, zc_ref, rc_ref, rs_ref, wq_ref, wkv_ref, gq_ref, gkv_ref,
             dzc_ref, dzk_ref, dwq_ref, dwkv_ref, dgq_ref, dgkv_ref):
        _zero_when(pl.program_id(0) == 0, dwq_ref, dwkv_ref, dgq_ref, dgkv_ref)
        cq, ckv = zc_ref[:, 0:384], zc_ref[:, 384:640]
        c, s = rc_ref[...], rs_ref[...]
        dkv = _mx(jnp.concatenate([dk_ref[hh][:, 0:128] for hh in range(N_HEADS)]
                                  + [dv_ref[hh] for hh in range(N_HEADS)], axis=1))
        kvn, rkv = _rms(ckv, gkv_ref[...])
        _acc(dwkv_ref, _dot_tn(_mx(kvn), dkv))
        d_kvn = _dot_nt(dkv, wkv_ref[...])
        dq_all = jnp.concatenate([dq_ref[hh] for hh in range(N_HEADS)], axis=1)
        dqp = _mx(_rope_bwd(dq_all, jnp.tile(c, (1, N_HEADS)), jnp.tile(s, (1, N_HEADS))))
        qn, rq = _rms(cq, gq_ref[...])
        dkpe = dk_ref[0][:, 128:256]
        for hh in range(1, N_HEADS):
            dkpe = dkpe + dk_ref[hh][:, 128:256]
        dzk_ref[...] = _mx(_rope_bwd(dkpe, c[:, 128:256], s[:, 128:256]))
        _acc(dwq_ref, _dot_tn(_mx(qn), dqp))
        d_qn = _dot_nt(dqp, wq_ref[...])
        d_ckv, dgkv_t = _rms_bwd(ckv, rkv, gkv_ref[...], d_kvn)
        _acc(dgkv_ref, _colsum(dgkv_t))
        d_cq, dgq_t = _rms_bwd(cq, rq, gq_ref[...], d_qn)
        _acc(dgq_ref, _colsum(dgq_t))
        dzc_ref[...] = _mx(jnp.concatenate([d_cq, d_ckv], axis=1))

    fixed = lambda a, b: pl.BlockSpec((a, b), lambda i: (0, 0))
    heads = lambda wd: pl.BlockSpec((N_HEADS, tm, wd), lambda i: (0, i, 0))
    return pl.pallas_call(
        body, name=name, grid=(t // tm,),
        in_specs=[heads(QK_PAD), heads(QK_PAD), heads(V_DIM), pl.BlockSpec((tm, Z_C), lambda i: (i, Z_MIX // Z_C)),
                  pl.BlockSpec((tm, 256), lambda i: (i, 0)), pl.BlockSpec((tm, 256), lambda i: (i, 0)),
                  fixed(384, 1024), fixed(256, 1024), fixed(1, 384), fixed(1, 256)],
        out_specs=[pl.BlockSpec((tm, Z_C), lambda i: (i, 0)), pl.BlockSpec((tm, Z_KR), lambda i: (i, 0)),
                   fixed(384, 1024), fixed(256, 1024), fixed(1, 384), fixed(1, 256)],
        out_shape=[jax.ShapeDtypeStruct((t, Z_C), MXU_DTYPE), jax.ShapeDtypeStruct((t, Z_KR), MXU_DTYPE),
                   jax.ShapeDtypeStruct((384, 1024), F32), jax.ShapeDtypeStruct((256, 1024), F32),
                   jax.ShapeDtypeStruct((1, 384), F32), jax.ShapeDtypeStruct((1, 256), F32)],
        compiler_params=_cp())(dq, dk, dv, z, rc, rs, w_uq, w_ukv, gq, gkv)


def _loop_in_long_trips(n, body, longest=4):
    def doubled(inner):
        return lambda t, carry: inner(2 * t + 1, inner(2 * t, carry))

    trips = [body]
    while 2 ** (len(trips) - 1) < longest:
        trips.append(doubled(trips[-1]))
    done = 0
    for level in reversed(range(len(trips))):
        size = 2 ** level
        end = n // size
        lax.fori_loop(done, end, trips[level], 0)
        done = 2 * end if level else end


def _head_loads(k_src, v_src, k_dst, v_dst, sems, first_rows):
    t = k_src.shape[0]
    parts = [(0, first_rows)] + ([(first_rows, t - first_rows)] if t > first_rows else [])
    return [pltpu.make_async_copy(src.at[pl.ds(r0, n)], dst.at[pl.ds(r0, n)], sems.at[2 * q + w])
            for q, (r0, n) in enumerate(parts) for w, (src, dst) in enumerate([(k_src, k_dst), (v_src, v_dst)])]


def _init_mask_bias(bias_ref):
    _, tq, tk = bias_ref.shape
    r = lax.broadcasted_iota(jnp.int32, (tq, tk), 0) // CHUNK
    c = lax.broadcasted_iota(jnp.int32, (tq, tk), 1) // CHUNK
    bias_ref[0] = jnp.zeros((tq, tk), F32)
    for d in range(tq // tk):
        bias_ref[1 + d] = jnp.where(c + d * (tk // CHUNK) <= r, 0.0, NEG_INF)


def _gate_block(tq):
    return pl.BlockSpec((tq, 128), lambda h, i: (i, (Z_MIX + Z_C + Z_KR) // 128 + h))


def _attn_fwd(qh, kh, vh, z, name, tq=1024, tk=512):
    t = qh.shape[1]
    tq = min(tq, t)
    tk = min(tk, tq)
    ratio = tq // tk

    def body(q_ref, g_ref, k_hbm, v_hbm, o_ref, yc_ref, lse_ref, k_v, v_v, m_s, acc_s, s_a, s_b, mx_a, mx_b, bias_s,
             sem):
        h, i = pl.program_id(0), pl.program_id(1)

        loads = _head_loads(k_hbm.at[h], v_hbm.at[h], k_v, v_v.at[:, 0:V_DIM], sem, tq)

        @pl.when(i == 0)
        def _():
            for cp in loads:
                cp.start()
            v_v[:, V_DIM:2 * V_DIM] = jnp.ones((t, V_DIM), MXU_DTYPE)
            _init_mask_bias(bias_s)
            for cp in loads[:2]:
                cp.wait()

        @pl.when(i == 1)
        def _():
            for cp in loads[2:]:
                cp.wait()

        q = q_ref[...]
        m_s[...] = jnp.full(m_s.shape, NEG_INF, F32)
        acc_s[...] = jnp.zeros(acc_s.shape, F32)

        last = ratio * (i + 1) - 1

        def keys(j):
            return pl.ds(pl.multiple_of(j * tk, tk), tk)

        def scores(s_ref, mx_ref, j, biased):
            s = _dot_nt(q, k_v[keys(j), :]) * (SCALE * LOG2E)
            if biased:
                s = s + bias_s[jnp.maximum(j - ratio * i + 1, 0)]
            s_ref[...] = s
            mx_ref[...] = jnp.broadcast_to(jnp.max(s, axis=-1, keepdims=True), mx_ref.shape)

        def softmax_pv(s_ref, mx_ref, j):
            m_old = m_s[...]
            m_new = jnp.maximum(m_old, mx_ref[...])
            p = jnp.exp2(s_ref[...] - jnp.tile(m_new, (1, tk // 128)))
            alpha = jnp.exp2(m_old - m_new)
            m_s[...] = m_new
            acc_s[...] = jnp.tile(alpha, (1, 2)) * acc_s[...] + _dot(_mx(p), v_v[keys(j), :])

        scores(s_a, mx_a, 0, True)

        def pair(pp, carry, biased):
            scores(s_b, mx_b, 2 * pp + 1, biased)
            softmax_pv(s_a, mx_a, 2 * pp)
            scores(s_a, mx_a, jnp.minimum(2 * pp + 2, last), biased)
            softmax_pv(s_b, mx_b, 2 * pp + 1)
            return carry

        n_pairs = (last + 1) // 2
        n_plain = jnp.maximum(ratio * i // 2 - 1, 0)
        _loop_in_long_trips(n_plain, lambda pp, carry: pair(pp, carry, False))
        _loop_in_long_trips(n_pairs - n_plain, lambda pp, carry: pair(n_plain + pp, carry, True))
        if ratio % 2 == 1:
            @pl.when(last % 2 == 0)
            def _():
                softmax_pv(s_a, mx_a, last)

        l = acc_s[:, V_DIM:2 * V_DIM]
        o = acc_s[:, 0:V_DIM] / l
        o_ref[...] = o
        yc_ref[...] = _mx(o * _silu_and_grad(g_ref[...])[0])
        lse_ref[...] = m_s[...] + jnp.log2(l)

    return pl.pallas_call(
        body, name=name, grid=(N_HEADS, t // tq),
        in_specs=[pl.BlockSpec((None, tq, QK_PAD), lambda h, i: (h, i, 0)), _gate_block(tq), ANY, ANY],
        out_specs=[pl.BlockSpec((tq, 128), lambda h, i: (i, h)), pl.BlockSpec((tq, 128), lambda h, i: (i, h)),
                   pl.BlockSpec((None, tq, 128), lambda h, i: (h, i, 0))],
        out_shape=[jax.ShapeDtypeStruct((t, N_HEADS * V_DIM), F32), jax.ShapeDtypeStruct((t, N_HEADS * V_DIM), MXU_DTYPE),
                   jax.ShapeDtypeStruct((N_HEADS, t, 128), F32)],
        scratch_shapes=[pltpu.VMEM((t, QK_PAD), MXU_DTYPE), pltpu.VMEM((t, 2 * V_DIM), MXU_DTYPE),
                        pltpu.VMEM((tq, 128), F32), pltpu.VMEM((tq, 2 * V_DIM), F32),
                        pltpu.VMEM((tq, tk), F32), pltpu.VMEM((tq, tk), F32), pltpu.VMEM((tq, 128), F32),
                        pltpu.VMEM((tq, 128), F32), pltpu.VMEM((ratio + 1, tq, tk), F32),
                        pltpu.SemaphoreType.DMA((4,))],
        compiler_params=_cp(2))(qh, z, kh, vh)


def _attn_bwd(qh, kh, vh, o, lse, dycat, z, name, tq=512):
    t = qh.shape[1]
    tq = min(tq, t)
    nq = t // tq

    def body(q_ref, o_ref, lse_ref, dy_ref, g_ref, k_hbm, v_hbm, dq_ref, dgate_ref, dk_hbm, dv_hbm,
             k_v, v_v, dk_acc, dv_acc, dq_acc, delta_s, s_a, dp_a, s_b, dp_b, bias_s, sem):
        h, i = pl.program_id(0), pl.program_id(1)

        loads = _head_loads(k_hbm.at[h], v_hbm.at[h], k_v, v_v, sem, tq)

        @pl.when(i == 0)
        def _():
            for cp in loads:
                cp.start()
            _init_mask_bias(bias_s)
            dk_acc[...] = jnp.zeros(dk_acc.shape, F32)
            dv_acc[...] = jnp.zeros(dv_acc.shape, F32)
            for cp in loads[:2]:
                cp.wait()

        @pl.when(i == 1)
        def _():
            for cp in loads[2:]:
                cp.wait()

        gate, dy, of = g_ref[...], dy_ref[...], o_ref[...]
        silu, dsilu = _silu_and_grad(gate)
        do = dy * silu
        delta = jnp.sum(do * of, axis=-1, keepdims=True)
        dgate_ref[...] = _mx(dy * of * dsilu)
        dob = _mx(do)
        q = q_ref[...]
        delta_s[...] = jnp.broadcast_to(delta, delta_s.shape)
        dq_acc[...] = jnp.zeros(dq_acc.shape, F32)

        def keys(j):
            return pl.ds(pl.multiple_of(j * tq, tq), tq)

        def scores(s_ref, dp_ref, j):
            s = _dot_nt(q, k_v[keys(j), :]) * (SCALE * LOG2E) + bias_s[(j == i).astype(jnp.int32)]
            s_ref[...] = s - jnp.tile(lse_ref[...], (1, tq // 128))
            dp_ref[...] = _dot_nt(dob, v_v[keys(j), :]) - jnp.tile(delta_s[...], (1, tq // 128))

        def grads(s_ref, dp_ref, j):
            ks = keys(j)
            p = jnp.exp2(s_ref[...])
            ds = p * dp_ref[...] * SCALE
            pb, dsb = _mx(p), _mx(ds)
            dq_acc[...] += _dot(dsb, k_v[ks, :])
            dk_acc[ks, :] += _dot_tn(dsb, q)
            dv_acc[ks, :] += _dot_tn(pb, dob)

        scores(s_a, dp_a, 0)

        def pair(pp, carry):
            scores(s_b, dp_b, 2 * pp + 1)
            grads(s_a, dp_a, 2 * pp)
            scores(s_a, dp_a, jnp.minimum(2 * pp + 2, i))
            grads(s_b, dp_b, 2 * pp + 1)
            return carry

        _loop_in_long_trips((i + 1) // 2, pair, longest=8)

        @pl.when(i % 2 == 0)
        def _():
            grads(s_a, dp_a, i)

        dq_ref[...] = dq_acc[...]

        @pl.when(i == nq - 1)
        def _():
            ck = pltpu.make_async_copy(dk_acc, dk_hbm.at[h], sem.at[0])
            cv = pltpu.make_async_copy(dv_acc, dv_hbm.at[h], sem.at[1])
            ck.start()
            cv.start()
            ck.wait()
            cv.wait()

    return pl.pallas_call(
        body, name=name, grid=(N_HEADS, nq),
        in_specs=[pl.BlockSpec((None, tq, QK_PAD), lambda h, i: (h, i, 0)),
                  pl.BlockSpec((tq, 128), lambda h, i: (i, h)),
                  pl.BlockSpec((None, tq, 128), lambda h, i: (h, i, 0)),
                  pl.BlockSpec((tq, 128), lambda h, i: (i, N_HEADS + h)), _gate_block(tq), ANY, ANY],
        out_specs=[pl.BlockSpec((None, tq, QK_PAD), lambda h, i: (h, i, 0)),
                   pl.BlockSpec((tq, 128), lambda h, i: (i, h)), ANY, ANY],
        out_shape=[jax.ShapeDtypeStruct((N_HEADS, t, QK_PAD), F32), jax.ShapeDtypeStruct((t, Z_GATE), MXU_DTYPE),
                   jax.ShapeDtypeStruct((N_HEADS, t, QK_PAD), F32), jax.ShapeDtypeStruct((N_HEADS, t, V_DIM), F32)],
        scratch_shapes=[pltpu.VMEM((t, QK_PAD), MXU_DTYPE), pltpu.VMEM((t, V_DIM), MXU_DTYPE),
                        pltpu.VMEM((t, QK_PAD), F32), pltpu.VMEM((t, V_DIM), F32), pltpu.VMEM((tq, QK_PAD), F32),
                        pltpu.VMEM((tq, 128), F32)] + [pltpu.VMEM((tq, tq), F32)] * 4
        + [pltpu.VMEM((2, tq, tq), F32), pltpu.SemaphoreType.DMA((4,))],
        compiler_params=_cp(2))(qh, o, lse, dycat, z, kh, vh)


def _out_proj_fwd(yab, yc, w, x, g, target, name, tm=512):
    t, d = x.shape
    tm = min(tm, t)
    is_last = target is not None

    def body(*refs):
        if is_last:
            yab_ref, yc_ref, w_ref, x_ref, g_ref, t_ref, y_ref, dout_ref, loss_ref = refs
            _zero_when(pl.program_id(0) == 0, loss_ref)
        else:
            yab_ref, yc_ref, w_ref, x_ref, g_ref, y_ref, out_ref = refs
        y = _dot(jnp.concatenate([yab_ref[...], yc_ref[...]], axis=1), w_ref[...])
        y_ref[...] = y
        out = x_ref[...] + _rms(y, g_ref[...])[0]
        if is_last:
            diff = out - t_ref[...]
            dout_ref[...] = diff * (1.0 / d)
            part = jnp.sum(jnp.sum(diff * diff, axis=-1, keepdims=True), axis=0, keepdims=True) * (0.5 / d)
            _acc(loss_ref, jnp.broadcast_to(part, (1, 128)))
        else:
            out_ref[...] = out

    row = lambda wd: pl.BlockSpec((tm, wd), lambda i: (i, 0))
    fixed = lambda a, b: pl.BlockSpec((a, b), lambda i: (0, 0))
    in_specs = [row(512), row(512), fixed(d, d), row(d), fixed(1, d)]
    args = [yab, yc, w, x, g]
    out_specs = [row(d), row(d)]
    out_shape = [jax.ShapeDtypeStruct((t, d), F32), jax.ShapeDtypeStruct((t, d), F32)]
    if is_last:
        in_specs.append(row(d))
        args.append(target)
        out_specs.append(fixed(1, 128))
        out_shape.append(jax.ShapeDtypeStruct((1, 128), F32))
    return pl.pallas_call(body, name=name, grid=(t // tm,), in_specs=in_specs, out_specs=out_specs,
                          out_shape=out_shape, compiler_params=_cp())(*args)


def _halves(tm):
    half = tm // 2 if tm >= 512 else tm
    return [pl.ds(s, half) for s in range(0, tm, half)]


def _out_proj_bwd(dout, y, yab, yc, w, g, name, tm=1024):
    t, d = y.shape
    tm = min(tm, t)

    def body(dout_ref, y_ref, yab_ref, yc_ref, w_ref, g_ref, dycat_ref, dw_ref, dg_ref):
        _zero_when(pl.program_id(0) == 0, dw_ref, dg_ref)
        dybs = []
        for rows in _halves(tm):
            y = y_ref[rows, :]
            r = lax.rsqrt(jnp.mean(y * y, axis=-1, keepdims=True) + EPS)
            dy, dgt = _rms_bwd(y, r, g_ref[...], dout_ref[rows, :])
            _acc(dg_ref, _colsum(dgt))
            dybs.append(_mx(dy))
        for rows, dyb in zip(_halves(tm), dybs):
            _acc(dw_ref, _dot_tn(jnp.concatenate([yab_ref[rows, :], yc_ref[rows, :]], axis=1), dyb))
            dycat_ref[rows, :] = _dot_nt(dyb, w_ref[...])

    row = lambda wd: pl.BlockSpec((tm, wd), lambda i: (i, 0))
    fixed = lambda a, b: pl.BlockSpec((a, b), lambda i: (0, 0), pipeline_mode=pl.Buffered(1))
    return pl.pallas_call(
        body, name=name, grid=(t // tm,),
        in_specs=[row(d), row(d), row(512), row(512), fixed(d, d), fixed(1, d)],
        out_specs=[row(d), fixed(d, d), fixed(1, d)],
        out_shape=[jax.ShapeDtypeStruct((t, d), F32), jax.ShapeDtypeStruct((d, d), F32),
                   jax.ShapeDtypeStruct((1, d), F32)],
        compiler_params=_cp())(dout, y, yab, yc, w, g)


def _mesh_pos():
    return lax.axis_index("x"), lax.axis_index("y"), lax.axis_index("c")


def _remote(src, dst, send_sem, recv_sem, to):
    return pltpu.make_async_remote_copy(src_ref=src, dst_ref=dst, send_sem=send_sem, recv_sem=recv_sem,
                                        device_id=to, device_id_type=MESH)


CHUNK_ROWS = 256


def _pieces(rows):
    return [(s, min(CHUNK_ROWS, rows - s)) for s in range(0, rows, CHUNK_ROWS)]


def _piece_table(shapes):
    return [(a, s, sz) for a, shp in enumerate(shapes) for s, sz in _pieces(shp[-2])]


def _gather_weights(shards):
    n = len(shards)
    table = _piece_table([s.shape for s in shards])
    npc = len(table)

    def body(*refs):
        ins, outs = refs[:n], refs[n:2 * n]
        send_sems, recv_sems, fwd_send, fwd_recv = refs[2 * n:]
        x, y, c = _mesh_pos()
        me, sibling = (x, y, c), (x, y, 1 - c)
        chips = [(1 - x, y), (x, 1 - y), (1 - x, 1 - y)]
        slot = lambda cx, cy, layer: 2 * (2 * cx + cy) + layer
        first = []
        for a in range(n):
            for j, (cx, cy) in enumerate(chips):
                first.append(_remote(ins[a].at[c], outs[a].at[slot(x, y, c)], send_sems.at[a, j], recv_sems.at[a, j],
                                     (cx, cy, c)))
                first[-1].start()
        passed = []
        for j, (cx, cy) in enumerate(chips):
            for a in range(n):
                blk = outs[a].at[slot(cx, cy, c)]
                _remote(blk, blk, send_sems.at[a, j], recv_sems.at[a, j], me).wait_recv()
            for q, (a, s, sz) in enumerate(table):
                rows = outs[a].at[slot(cx, cy, c), pl.ds(s, sz)]
                passed.append(_remote(rows, rows, fwd_send.at[j, q], fwd_recv.at[j, q], sibling))
                passed[-1].start()
        for j, (cx, cy) in enumerate(chips):
            for q, (a, s, sz) in enumerate(table):
                rows = outs[a].at[slot(cx, cy, 1 - c), pl.ds(s, sz)]
                _remote(rows, rows, fwd_send.at[j, q], fwd_recv.at[j, q], me).wait_recv()
        for cp in first + passed:
            cp.wait_send()

    return pl.pallas_call(
        body, name="gather_weights", in_specs=[ANY] * n, out_specs=[ANY] * n,
        out_shape=[jax.ShapeDtypeStruct((8,) + s.shape[1:], s.dtype) for s in shards],
        scratch_shapes=[pltpu.SemaphoreType.DMA((n, 3)), pltpu.SemaphoreType.DMA((n, 3)),
                        pltpu.SemaphoreType.DMA((3, npc)), pltpu.SemaphoreType.DMA((3, npc))])(*shards)


def _pair_exchange(parts, common):
    n = len(parts)
    table = _piece_table([p.shape for p in parts] + [common.shape])
    npc = len(table)

    def body(*refs):
        ins, outs = refs[:n + 1], refs[n + 1:2 * n + 2]
        send_sems, recv_sems = refs[2 * n + 2:]
        x, y, c = _mesh_pos()
        sent = []
        for k in range(4):
            for q, (a, s, sz) in enumerate(table):
                if a == n and k > 0:
                    continue
                src = ins[a].at[2 * k + 1 - c, pl.ds(s, sz)] if a < n else ins[a].at[pl.ds(s, sz)]
                dst = outs[a].at[k, pl.ds(s, sz)] if a < n else outs[a].at[pl.ds(s, sz)]
                sent.append(_remote(src, dst, send_sems.at[k, q], recv_sems.at[k, q], (x, y, 1 - c)))
                sent[-1].start()
        for k in range(4):
            for q, (a, s, sz) in enumerate(table):
                if a == n and k > 0:
                    continue
                dst = outs[a].at[k, pl.ds(s, sz)] if a < n else outs[a].at[pl.ds(s, sz)]
                _remote(dst, dst, send_sems.at[k, q], recv_sems.at[k, q], (x, y, c)).wait_recv()
        for cp in sent:
            cp.wait_send()

    return pl.pallas_call(
        body, name="grad_pair_exchange", in_specs=[ANY] * (n + 1), out_specs=[ANY] * (n + 1),
        out_shape=[jax.ShapeDtypeStruct((4,) + p.shape[1:], p.dtype) for p in parts]
        + [jax.ShapeDtypeStruct(common.shape, common.dtype)],
        scratch_shapes=[pltpu.SemaphoreType.DMA((4, npc)), pltpu.SemaphoreType.DMA((4, npc))])(*parts, common)


def _chip_exchange(parts, common):
    n = len(parts)
    table = _piece_table([p.shape for p in parts] + [common.shape])
    npc = len(table)

    def body(*refs):
        ins, outs = refs[:n + 1], refs[n + 1:2 * n + 2]
        send_sems, recv_sems = refs[2 * n + 2:]
        x, y, c = _mesh_pos()
        mine = 2 * x + y
        chips = [(1 - x, y), (x, 1 - y), (1 - x, 1 - y)]
        src = lambda a, k: ins[a].at[k] if a < n else ins[a]
        sent = []
        for j, (cx, cy) in enumerate(chips):
            for q, (a, s, sz) in enumerate(table):
                sent.append(_remote(src(a, 2 * cx + cy).at[pl.ds(s, sz)], outs[a].at[mine, pl.ds(s, sz)],
                                    send_sems.at[j, q], recv_sems.at[j, q], (cx, cy, c)))
                sent[-1].start()
        for j, (cx, cy) in enumerate(chips):
            for q, (a, s, sz) in enumerate(table):
                dst = outs[a].at[2 * cx + cy, pl.ds(s, sz)]
                _remote(dst, dst, send_sems.at[j, q], recv_sems.at[j, q], (x, y, c)).wait_recv()
        for cp in sent:
            cp.wait_send()

    return pl.pallas_call(
        body, name="grad_chip_exchange", in_specs=[ANY] * (n + 1), out_specs=[ANY] * (n + 1),
        out_shape=[jax.ShapeDtypeStruct(p.shape, p.dtype) for p in parts]
        + [jax.ShapeDtypeStruct((4,) + common.shape, common.dtype)],
        scratch_shapes=[pltpu.SemaphoreType.DMA((3, npc)), pltpu.SemaphoreType.DMA((3, npc))])(*parts, common)


def _sibling_exchange(sums):
    n = len(sums)
    table = _piece_table([s.shape for s in sums])
    npc = len(table)

    def body(*refs):
        ins, outs = refs[:n], refs[n:2 * n]
        send_sems, recv_sems = refs[2 * n:]
        x, y, c = _mesh_pos()
        sent = []
        for q, (a, s, sz) in enumerate(table):
            sent.append(_remote(ins[a].at[pl.ds(s, sz)], outs[a].at[pl.ds(s, sz)], send_sems.at[q], recv_sems.at[q],
                                (x, y, 1 - c)))
            sent[-1].start()
        for q, (a, s, sz) in enumerate(table):
            dst = outs[a].at[pl.ds(s, sz)]
            _remote(dst, dst, send_sems.at[q], recv_sems.at[q], (x, y, c)).wait_recv()
        for cp in sent:
            cp.wait_send()

    return pl.pallas_call(
        body, name="sibling_exchange", in_specs=[ANY] * n, out_specs=[ANY] * n,
        out_shape=[jax.ShapeDtypeStruct(s.shape, s.dtype) for s in sums],
        scratch_shapes=[pltpu.SemaphoreType.DMA((npc,)), pltpu.SemaphoreType.DMA((npc,))])(*sums)


def _pair_sum(parts, got, name, tr):
    _, r, c = got.shape
    tr = min(tr, r)

    def body(p_ref, g_ref, o_ref, w_ref):
        total = p_ref[...] + g_ref[...]
        o_ref[...] = total
        w_ref[...] = total.astype(WIRE_DTYPE)

    blk = pl.BlockSpec((None, tr, c), lambda k, i: (k, i, 0))
    mine = pl.BlockSpec((None, tr, c), lambda k, i: (2 * k + lax.axis_index("c"), i, 0))
    return pl.pallas_call(
        body, name=name, grid=(4, r // tr), in_specs=[mine, blk], out_specs=[blk, blk],
        out_shape=[jax.ShapeDtypeStruct(got.shape, F32), jax.ShapeDtypeStruct(got.shape, WIRE_DTYPE)],
        compiler_params=_cp(2))(parts, got)


def _sum_chips(pair_sums, recv, name, tr):
    _, r, c = pair_sums.shape
    tr = min(tr, r)

    def body(own_ref, r_ref, o_ref):
        chip = 2 * lax.axis_index("x") + lax.axis_index("y")
        own_blk = own_ref[...]
        acc = jnp.where(chip == 0, own_blk, r_ref[0].astype(F32))
        for k in range(1, 4):
            acc = acc + jnp.where(chip == k, own_blk, r_ref[k].astype(F32))
        o_ref[...] = acc

    return pl.pallas_call(
        body, name=name, grid=(r // tr,),
        in_specs=[pl.BlockSpec((None, tr, c), lambda i: (2 * lax.axis_index("x") + lax.axis_index("y"), i, 0)),
                  pl.BlockSpec((4, tr, c), lambda i: (0, i, 0))],
        out_specs=pl.BlockSpec((tr, c), lambda i: (i, 0)),
        out_shape=jax.ShapeDtypeStruct((r, c), F32), compiler_params=_cp())(pair_sums, recv)


def _sum_leading(parts, name, tr):
    nlead, r, c = parts.shape
    tr = min(tr, r)

    def body(p_ref, o_ref):
        acc = p_ref[0]
        for j in range(1, nlead):
            acc = acc + p_ref[j]
        o_ref[...] = acc

    return pl.pallas_call(
        body, name=name, grid=(r // tr,),
        in_specs=[pl.BlockSpec((nlead, tr, c), lambda i: (0, i, 0))], out_specs=pl.BlockSpec((tr, c), lambda i: (i, 0)),
        out_shape=jax.ShapeDtypeStruct((r, c), parts.dtype), compiler_params=_cp())(parts)


def _adamw_update(w_ref, g_ref, m_ref, v_ref, d_ref, nm_ref, nv_ref):
    gg = g_ref[...]
    nm = ADAM_B1 * m_ref[...] + (1.0 - ADAM_B1) * gg
    nv = ADAM_B2 * v_ref[...] + (1.0 - ADAM_B2) * jnp.square(gg)
    m_hat = nm / (1.0 - ADAM_B1 ** ADAM_STEP)
    v_hat = nv / (1.0 - ADAM_B2 ** ADAM_STEP)
    d_ref[...] = -ADAM_LR * (m_hat / (jnp.sqrt(v_hat) + ADAM_EPS) + ADAM_WD * w_ref[...])
    nm_ref[...] = nm
    nv_ref[...] = nv


def _adamw(w, g, m, v, name, tr):
    r, c = w.shape
    tr = tr if r % tr == 0 else r

    def body(w_ref, g_ref, m_ref, v_ref, d_ref, nm_ref, nv_ref):
        _adamw_update(w_ref, g_ref, m_ref, v_ref, d_ref, nm_ref, nv_ref)

    blk = pl.BlockSpec((tr, c), lambda i: (i, 0))
    return pl.pallas_call(
        body, name=name, grid=(r // tr,), in_specs=[blk] * 4, out_specs=[blk] * 3,
        out_shape=[jax.ShapeDtypeStruct((r, c), F32)] * 3, compiler_params=_cp())(w, g, m, v)


def _adamw_many(ws, gs, ms, vs, name):
    n = len(ws)

    def body(*refs):
        ins, outs = refs[:4 * n], refs[4 * n:]
        for k in range(n):
            _adamw_update(ins[k], ins[n + k], ins[2 * n + k], ins[3 * n + k], outs[k], outs[n + k], outs[2 * n + k])

    vmem = pl.BlockSpec(memory_space=pltpu.VMEM)
    res = pl.pallas_call(
        body, name=name, in_specs=[vmem] * (4 * n), out_specs=[vmem] * (3 * n),
        out_shape=[jax.ShapeDtypeStruct(a.shape, F32) for a in ws] * 3)(*ws, *gs, *ms, *vs)
    return res[:n], res[n:2 * n], res[2 * n:]


def _rope_tables(positions):
    inv_freq = ROPE_BASE ** (-jnp.arange(0, 64, 2, dtype=F32) / 64)
    ang = positions.astype(F32)[:, None] * inv_freq
    cos, sin = jnp.cos(ang), jnp.sin(ang)
    t = positions.shape[0]
    rc = jnp.concatenate([jnp.ones((t, 128), F32), cos, cos, jnp.ones((t, 64), F32)], axis=1)
    rs = jnp.concatenate([jnp.zeros((t, 128), F32), sin, sin, jnp.zeros((t, 64), F32)], axis=1)
    return rc, rs


def _layer_params(l, w_in, w_uq, w_ukv, w_out, small):
    p = {}
    p["w_in"] = jnp.concatenate([w_in[l][:, :1984], jnp.zeros((1024, 64), w_in.dtype), w_in[l][:, 1984:]], axis=1)
    p["w_uq"] = jnp.pad(w_uq[l].reshape(384, 4, 192), ((0, 0), (0, 0), (0, 64))).reshape(384, 1024)
    p["w_ukv"] = w_ukv[l].reshape(256, 4, 2, 128).transpose(0, 2, 1, 3).reshape(256, 1024)
    p["w_out"] = w_out[l]
    p["pre_g"] = small["pre_norm_g"][l][None]
    p["post_g"] = small["post_norm_g"][l][None]
    p["sgu_w"] = small["sgu_w"][l].reshape(512, 128)
    p["sgu_wt"] = small["sgu_w"][l].transpose(0, 2, 1).reshape(512, 128)
    p["sgu_bias"] = jnp.repeat(small["sgu_b"][l].T, 64, axis=1)
    p["ln_g"] = small["sgu_ln_g"][l][None]
    p["ln_b"] = small["sgu_ln_b"][l][None]
    p["pool_wbd"] = _mx(jax.scipy.linalg.block_diag(*[small["pool_w"][l][gi] for gi in range(4)]))
    p["pool_scale"] = small["pool_scale"][l][None]
    p["gq"] = small["q_norm_g"][l][None]
    p["gkv"] = small["kv_norm_g"][l][None]
    return p


def _layer_fwd(l, x, p, rc, rs, target):
    z, h = _in_proj_fwd(x, p["pre_g"], p["w_in"], f"in_proj_fwd_{l}")
    yab = _mix_fwd(z, p["sgu_w"], p["sgu_bias"], p["ln_g"], p["ln_b"], p["pool_wbd"], p["pool_scale"], f"mix_fwd_{l}")
    qh, kh, vh = _qkv_fwd(z, rc, rs, p["w_uq"], p["w_ukv"], p["gq"], p["gkv"], f"qkv_fwd_{l}")
    o, yc, lse = _attn_fwd(qh, kh, vh, z, f"attn_fwd_{l}")
    outs = _out_proj_fwd(yab, yc, p["w_out"], x, p["post_g"], target, f"out_proj_fwd_{l}")
    saved = dict(x=x, z=z, h=h, yab=yab, qh=qh, kh=kh, vh=vh, o=o, yc=yc, lse=lse, y=outs[0])
    return saved, outs[1:]


def _layer_bwd(l, dout, sv, p, rc, rs):
    dycat, dw_out, dpost = _out_proj_bwd(dout, sv["y"], sv["yab"], sv["yc"], p["w_out"], p["post_g"], f"out_proj_bwd_{l}")
    dq, dgate, dk, dv = _attn_bwd(sv["qh"], sv["kh"], sv["vh"], sv["o"], sv["lse"], dycat, sv["z"], f"attn_bwd_{l}")
    dzc, dzk, dwq, dwkv, dgq, dgkv = _qkv_bwd(dq, dk, dv, sv["z"], rc, rs, p["w_uq"], p["w_ukv"], p["gq"], p["gkv"],
                                              f"qkv_bwd_{l}")
    dzm, dsw, dsb, dlng, dlnb, dpw, dps = _mix_bwd(sv["z"], dycat, p["sgu_w"], p["sgu_wt"], p["sgu_bias"], p["ln_g"],
                                                   p["ln_b"], p["pool_wbd"], p["pool_scale"], f"mix_bwd_{l}")
    dx, dw_in, dpre = _in_proj_bwd(dzm, dzc, dzk, dgate, sv["h"], sv["x"], dout, p["w_in"], p["pre_g"], f"in_proj_bwd_{l}")
    grads = {
        "pre_norm_g": dpre[0], "post_norm_g": dpost[0],
        "w_in": jnp.concatenate([dw_in[:, :1984], dw_in[:, 2048:]], axis=1),
        "sgu_w": dsw.reshape(4, 128, 128), "sgu_b": dsb[:, :4].T, "sgu_ln_g": dlng[0], "sgu_ln_b": dlnb[0],
        "pool_w": jnp.stack([dpw[64 * gi:64 * gi + 64, 64 * gi:64 * gi + 64] for gi in range(4)]),
        "pool_scale": dps[0], "q_norm_g": dgq[0],
        "w_uq": dwq.reshape(384, 4, 256)[:, :, :192].reshape(384, 768), "kv_norm_g": dgkv[0],
        "w_ukv": dwkv.reshape(256, 2, 4, 128).transpose(0, 2, 1, 3).reshape(256, 1024), "w_out": dw_out,
    }
    return dx, grads


SMALL_NAMES = ["pre_norm_g", "post_norm_g", "sgu_w", "sgu_b", "sgu_ln_g", "sgu_ln_b", "pool_w", "pool_scale",
               "q_norm_g", "kv_norm_g"]
BIG_NAMES = ["w_in", "w_uq", "w_ukv", "w_out"]
WEIGHT_NAMES = ["pre_norm_g", "post_norm_g", "w_in", "sgu_w", "sgu_b", "sgu_ln_g", "sgu_ln_b", "pool_w", "pool_scale",
                "q_norm_g", "w_uq", "kv_norm_g", "w_ukv", "w_out"]


def _local_step(x, positions, target, w_in, w_uq, w_ukv, w_out, small):
    rc, rs = _rope_tables(positions)
    params = [_layer_params(l, w_in, w_uq, w_ukv, w_out, small) for l in range(DEPTH)]
    saved = []
    for l in range(DEPTH):
        sv, outs = _layer_fwd(l, x, params[l], rc, rs, target if l == DEPTH - 1 else None)
        saved.append(sv)
        if l < DEPTH - 1:
            x = outs[0]
    dout, loss = outs
    grads = [None] * DEPTH
    for l in reversed(range(DEPTH)):
        dout, grads[l] = _layer_bwd(l, dout, saved[l], params[l], rc, rs)
    return loss[0, 0], dout, {k: jnp.stack([grads[l][k] for l in range(DEPTH)]) for k in WEIGHT_NAMES}


def _pack_small(tree, extra=None):
    pieces = [tree[k].reshape(-1) for k in SMALL_NAMES]
    pieces.append(jnp.zeros((1,), F32) if extra is None else extra.reshape(1))
    flat = jnp.concatenate(pieces)
    rows = -(-flat.shape[0] // 1024) * 8
    return jnp.pad(flat, (0, rows * 128 - flat.shape[0])).reshape(rows, 128)


def _unpack_small(packed, like):
    flat = packed.reshape(-1)
    out, off = {}, 0
    for k in SMALL_NAMES:
        size = like[k].size
        out[k] = flat[off:off + size].reshape(like[k].shape)
        off += size
    return out, flat[off]


def kernel(x, positions, pre_norm_g, post_norm_g, w_in, sgu_w, sgu_b, sgu_ln_g, sgu_ln_b, pool_w, pool_scale, q_norm_g, w_uq, kv_norm_g, w_ukv, w_out, loss_target, m_pre_norm_g, m_post_norm_g, m_w_in, m_sgu_w, m_sgu_b, m_sgu_ln_g, m_sgu_ln_b, m_pool_w, m_pool_scale, m_q_norm_g, m_w_uq, m_kv_norm_g, m_w_ukv, m_w_out, v_pre_norm_g, v_post_norm_g, v_w_in, v_sgu_w, v_sgu_b, v_sgu_ln_g, v_sgu_ln_b, v_pool_w, v_pool_scale, v_q_norm_g, v_w_uq, v_kv_norm_g, v_w_ukv, v_w_out):
    w = dict(pre_norm_g=pre_norm_g, post_norm_g=post_norm_g, w_in=w_in, sgu_w=sgu_w, sgu_b=sgu_b, sgu_ln_g=sgu_ln_g,
             sgu_ln_b=sgu_ln_b, pool_w=pool_w, pool_scale=pool_scale, q_norm_g=q_norm_g, w_uq=w_uq, kv_norm_g=kv_norm_g,
             w_ukv=w_ukv, w_out=w_out)
    m = dict(pre_norm_g=m_pre_norm_g, post_norm_g=m_post_norm_g, w_in=m_w_in, sgu_w=m_sgu_w, sgu_b=m_sgu_b,
             sgu_ln_g=m_sgu_ln_g, sgu_ln_b=m_sgu_ln_b, pool_w=m_pool_w, pool_scale=m_pool_scale, q_norm_g=m_q_norm_g,
             w_uq=m_w_uq, kv_norm_g=m_kv_norm_g, w_ukv=m_w_ukv, w_out=m_w_out)
    v = dict(pre_norm_g=v_pre_norm_g, post_norm_g=v_post_norm_g, w_in=v_w_in, sgu_w=v_sgu_w, sgu_b=v_sgu_b,
             sgu_ln_g=v_sgu_ln_g, sgu_ln_b=v_sgu_ln_b, pool_w=v_pool_w, pool_scale=v_pool_scale, q_norm_g=v_q_norm_g,
             w_uq=v_w_uq, kv_norm_g=v_kv_norm_g, w_ukv=v_w_ukv, w_out=v_w_out)

    core = lax.axis_index("c")
    chip = 2 * lax.axis_index("x") + lax.axis_index("y")
    shards = [_mx(w[k]) for k in BIG_NAMES]
    gathered = _gather_weights(shards)
    g_in, g_uq, g_ukv, g_out = [lax.dynamic_update_slice(g, s, (2 * chip, 0, 0)) for g, s in zip(gathered, shards)]
    cols = lambda g: g.reshape((4, 2) + g.shape[1:]).transpose(1, 2, 0, 3).reshape(2, g.shape[1], 4 * g.shape[2])
    full_out = g_out.reshape(4, 2, 256, 1024).transpose(1, 0, 2, 3).reshape(2, 1024, 1024)
    loss, dx, grads = _local_step(x[0], positions[0], loss_target[0], cols(g_in), cols(g_uq), cols(g_ukv), full_out, w)

    split_cols = lambda g: g.reshape(2, g.shape[1], 4, g.shape[2] // 4).transpose(2, 0, 1, 3).reshape(8, g.shape[1], g.shape[2] // 4)
    parts = [split_cols(grads["w_in"]), split_cols(grads["w_uq"]), split_cols(grads["w_ukv"]),
             grads["w_out"].reshape(2, 4, 256, 1024).transpose(1, 0, 2, 3).reshape(8, 256, 1024)]
    common = _pack_small(grads, loss)
    got = _pair_exchange(parts, common)
    pair_sums = [_pair_sum(parts[a], got[a], f"pair_sum_{BIG_NAMES[a]}", 512) for a in range(4)]
    chip_common = _sum_leading(jnp.stack([common, got[4]]), "pair_sum_small", common.shape[0])
    received = _chip_exchange([ps[1] for ps in pair_sums], chip_common)
    sums = [_sum_chips(pair_sums[a][0], received[a], f"sum_{BIG_NAMES[a]}", 512) for a in range(4)]
    all_common = lax.dynamic_update_slice(received[4], chip_common[None], (chip, 0, 0))
    small_sum, loss = _unpack_small(_sum_leading(all_common, "sum_small", all_common.shape[1]), w)
    others = _sibling_exchange(sums)
    total = dict(small_sum)
    for a, k in enumerate(BIG_NAMES):
        total[k] = jnp.where(core == 0, jnp.stack([sums[a], others[a]]), jnp.stack([others[a], sums[a]]))

    rows2d = lambda a: a.reshape(-1, a.shape[-1])
    small_out = _adamw_many(*[[rows2d(tree[k]) for k in SMALL_NAMES] for tree in (w, total, m, v)], "adamw_small")
    delta, new_m, new_v = ({k: r.reshape(w[k].shape) for k, r in zip(SMALL_NAMES, res)} for res in small_out)
    for k in BIG_NAMES:
        shape = w[k].shape
        flat = lambda a: a.reshape(shape[0] * shape[1], shape[2])
        res = _adamw(flat(w[k]), flat(total[k]), flat(m[k]), flat(v[k]), f"adamw_{k}", 512)
        delta[k], new_m[k], new_v[k] = (r.reshape(shape) for r in res)

    return (loss, dx[None], *[total[k] for k in WEIGHT_NAMES], *[delta[k] for k in WEIGHT_NAMES],
            *[new_m[k] for k in WEIGHT_NAMES], *[new_v[k] for k in WEIGHT_NAMES])
```

```python
import jax
import jax.numpy as jnp
from jax import lax
from jax.experimental import pallas as pl
from jax.experimental.pallas import tpu as pltpu

F32 = jnp.float32
MXU_DTYPE = jnp.bfloat16
WIRE_DTYPE = jnp.bfloat16
EPS = 1e-6
NEG_INF = -1e30
CHUNK = 64
DEPTH = 2
N_HEADS = 4
QK_PAD = 256
V_DIM = 128
SCALE = 192 ** -0.5
LOG2E = 1.4426950408889634
ROPE_BASE = 10000.0
ADAM_LR, ADAM_B1, ADAM_B2, ADAM_EPS, ADAM_WD, ADAM_STEP = 0.001, 0.9, 0.999, 1e-08, 0.01, 10
VMEM_LIMIT_BYTES = 56 * 1024 * 1024
MESH = pl.DeviceIdType.MESH
ANY = pl.BlockSpec(memory_space=pl.ANY)

Z_MIX, Z_C, Z_KR, Z_GATE = 1280, 640, 128, 512
Z_W = Z_MIX + Z_C + Z_KR + Z_GATE


def _cp(n_axes=1):
    return pltpu.CompilerParams(dimension_semantics=("arbitrary",) * n_axes, vmem_limit_bytes=VMEM_LIMIT_BYTES)


def _dot(a, b):
    return lax.dot_general(a, b, (((1,), (0,)), ((), ())), preferred_element_type=F32)


def _dot_nt(a, b):
    return lax.dot_general(a, b, (((1,), (1,)), ((), ())), preferred_element_type=F32)


def _dot_tn(a, b):
    return lax.dot_general(a, b, (((0,), (0,)), ((), ())), preferred_element_type=F32)


def _mx(a):
    return a.astype(MXU_DTYPE)


def _silu_and_grad(g):
    sg = jax.nn.sigmoid(g)
    return g * sg, sg * (1.0 + g * (1.0 - sg))


def _rms(x, g):
    r = lax.rsqrt(jnp.mean(x * x, axis=-1, keepdims=True) + EPS)
    return x * r * g, r


def _rms_bwd(x, r, g, dy):
    xhat = x * r
    dyg = dy * g
    dx = r * (dyg - xhat * jnp.mean(dyg * xhat, axis=-1, keepdims=True))
    return dx, dy * xhat


def _zero_when(first, *refs):
    @pl.when(first)
    def _():
        for ref in refs:
            ref[...] = jnp.zeros(ref.shape, ref.dtype)


def _acc(ref, val):
    ref[...] += val


def _colsum(a):
    return jnp.sum(a, axis=0, keepdims=True)


def _in_proj_fwd(x, g, w, name, tm=1024):
    t, d = x.shape
    n = w.shape[1]
    tm = min(tm, t)

    def body(x_ref, g_ref, w_ref, z_ref, h_ref):
        hs = []
        for rows in _halves(tm):
            h, _ = _rms(x_ref[rows, :], g_ref[...])
            hs.append(_mx(h))
            h_ref[rows, :] = hs[-1]
        for rows, h in zip(_halves(tm), hs):
            z_ref[rows, :] = _dot(h, w_ref[...])

    return pl.pallas_call(
        body, name=name, grid=(t // tm,),
        in_specs=[pl.BlockSpec((tm, d), lambda i: (i, 0)), pl.BlockSpec((1, d), lambda i: (0, 0)),
                  pl.BlockSpec((d, n), lambda i: (0, 0), pipeline_mode=pl.Buffered(1))],
        out_specs=[pl.BlockSpec((tm, n), lambda i: (i, 0)), pl.BlockSpec((tm, d), lambda i: (i, 0))],
        out_shape=[jax.ShapeDtypeStruct((t, n), F32), jax.ShapeDtypeStruct((t, d), MXU_DTYPE)],
        compiler_params=_cp())(x, g, w)


def _in_proj_bwd(dz_mix, dz_c, dz_kr, dz_gate, h, x, d_res, w, g, name, tm=512):
    t, d = x.shape
    n = w.shape[1]
    tm = min(tm, t)

    def body(dm_ref, dc_ref, dk_ref, dg_ref, h_ref, x_ref, dres_ref, w_ref, g_ref, dx_ref, dw_ref, dgn_ref):
        first = pl.program_id(0) == 0
        dz = jnp.concatenate([dm_ref[...], dc_ref[...], dk_ref[...], dg_ref[...]], axis=1)

        _zero_when(first, dw_ref, dgn_ref)
        dh = _dot_nt(dz, w_ref[...])
        hb = h_ref[...]
        for c0 in range(0, n, 512):
            dw_ref[:, c0:c0 + 512] += _dot_tn(hb, dz[:, c0:c0 + 512])
        xf = x_ref[...]
        r = lax.rsqrt(jnp.mean(xf * xf, axis=-1, keepdims=True) + EPS)
        dx, dgt = _rms_bwd(xf, r, g_ref[...], dh)
        dx_ref[...] = dx + dres_ref[...]
        _acc(dgn_ref, _colsum(dgt))

    row = lambda wd: pl.BlockSpec((tm, wd), lambda i: (i, 0))
    fixed = lambda a, b: pl.BlockSpec((a, b), lambda i: (0, 0), pipeline_mode=pl.Buffered(1))
    return pl.pallas_call(
        body, name=name, grid=(t // tm,),
        in_specs=[row(Z_MIX), row(Z_C), row(Z_KR), row(Z_GATE), row(d), row(d), row(d), fixed(d, n), fixed(1, d)],
        out_specs=[row(d), fixed(d, n), fixed(1, d)],
        out_shape=[jax.ShapeDtypeStruct((t, d), F32), jax.ShapeDtypeStruct((d, n), F32),
                   jax.ShapeDtypeStruct((1, d), F32)],
        compiler_params=_cp())(dz_mix, dz_c, dz_kr, dz_gate, h, x, d_res, w, g)


def _lane_group(shape):
    return lax.broadcasted_iota(jnp.int32, shape, 1) // 64


def _select_group(vals):
    grp = _lane_group(vals[0].shape)
    out = vals[3]
    for gi in (2, 1, 0):
        out = jnp.where(grp == gi, vals[gi], out)
    return out


def _sgu_mask(transposed):
    r = (lax.broadcasted_iota(jnp.int32, (512, 128), 0) % 128) // CHUNK
    c = lax.broadcasted_iota(jnp.int32, (512, 128), 1) // CHUNK
    return (r <= c) if transposed else (c <= r)


def _sgu_apply(wstack, vb, nblk):
    outs = []
    for n in range(nblk):
        r = _dot(wstack, vb[n * 128:(n + 1) * 128, :])
        outs.append(_select_group([r[hh * 128:(hh + 1) * 128, :] for hh in range(4)]))
    return jnp.concatenate(outs, axis=0)


def _layer_norm(v, g, b):
    mu = jnp.mean(v, axis=-1, keepdims=True)
    vc = v - mu
    rstd = lax.rsqrt(jnp.mean(vc * vc, axis=-1, keepdims=True) + EPS)
    vhat = vc * rstd
    return vhat * g + b, vhat, rstd


def _div_by_counts(x, t0):
    t = t0 + lax.broadcasted_iota(jnp.int32, (16, 256), 0)
    w = _select_group([jnp.full((16, 256), wv, jnp.int32) for wv in (2, 4, 8, 16)])
    head = x[:16, :] / jnp.minimum(t + 1, w).astype(F32)
    inv_w = _select_group([jnp.full((1, 256), 1.0 / wv, F32) for wv in (2, 4, 8, 16)])
    return jnp.concatenate([head, x[16:, :] * inv_w], axis=0)


def _pooled(p, halo, t0):
    tm = p.shape[0]
    ext = jnp.concatenate([halo, p], axis=0)
    s2 = ext + pltpu.roll(ext, 1, 0)
    s4 = s2 + pltpu.roll(s2, 2, 0)
    s8 = s4 + pltpu.roll(s4, 4, 0)
    s16 = s8 + pltpu.roll(s8, 8, 0)
    sel = _select_group([s2, s4, s8, s16])[16:, :]
    return _div_by_counts(sel, t0) - p


def _pooled_bwd(dpool, dpool_halo, t0):
    tm = dpool.shape[0]
    n = tm + 16
    ext = _div_by_counts(jnp.concatenate([dpool, dpool_halo], axis=0), t0)
    f2 = ext + pltpu.roll(ext, n - 1, 0)
    f4 = f2 + pltpu.roll(f2, n - 2, 0)
    f8 = f4 + pltpu.roll(f4, n - 4, 0)
    f16 = f8 + pltpu.roll(f8, n - 8, 0)
    return _select_group([f2, f4, f8, f16])[:tm, :] - dpool


def _mix_specs(t, tm):
    nt16 = t // 16
    zrow = pl.BlockSpec((tm, Z_MIX), lambda i: (i, 0))
    prev_halo = pl.BlockSpec((16, 256), lambda i: (jnp.maximum(i * (tm // 16) - 1, 0), 3))
    fixed = lambda a, b: pl.BlockSpec((a, b), lambda i: (0, 0))
    params = [fixed(512, 128), fixed(128, 256), fixed(1, 256), fixed(1, 256), fixed(256, 256), fixed(1, 256)]
    return nt16, zrow, prev_halo, fixed, params


def _mix_fwd(z, sgu_w, sgu_bias, ln_g, ln_b, pool_wbd, pool_scale, name, tm=512):
    t = z.shape[0]
    tm = min(tm, t)
    _, zrow, prev_halo, _, params = _mix_specs(t, tm)

    def body(z_ref, halo_ref, w_ref, bias_ref, lng_ref, lnb_ref, pw_ref, ps_ref, y_ref):
        i = pl.program_id(0)
        u, v, gate = z_ref[:, 0:256], z_ref[:, 256:512], z_ref[:, 512:768]
        p, pgate = z_ref[:, 768:1024], z_ref[:, 1024:1280]
        vn, _, _ = _layer_norm(v, lng_ref[...], lnb_ref[...])
        wm = _mx(jnp.where(_sgu_mask(False), w_ref[...], 0.0))
        halo = jnp.where(i > 0, halo_ref[...], 0.0)
        pooled = _pooled(p, halo, i * tm)
        mixed = _sgu_apply(wm, _mx(vn), tm // 128) + jnp.tile(bias_ref[...], (tm // 128, 1))
        mixedp = _dot(_mx(pooled), pw_ref[...])
        ya = u * mixed * _silu_and_grad(gate)[0]
        yb = mixedp * ps_ref[...] * _silu_and_grad(pgate)[0]
        y_ref[...] = _mx(jnp.concatenate([ya, yb], axis=1))

    return pl.pallas_call(
        body, name=name, grid=(t // tm,),
        in_specs=[zrow, prev_halo] + params,
        out_specs=pl.BlockSpec((tm, 512), lambda i: (i, 0)),
        out_shape=jax.ShapeDtypeStruct((t, 512), MXU_DTYPE),
        compiler_params=_cp())(z, z, sgu_w, sgu_bias, ln_g, ln_b, pool_wbd, pool_scale)


def _mix_bwd(z, dycat, sgu_w, sgu_wt, sgu_bias, ln_g, ln_b, pool_wbd, pool_scale, name, tm=512):
    t = z.shape[0]
    tm = min(tm, t)
    nt16, zrow, prev_halo, fixed, params = _mix_specs(t, tm)
    nblk = tm // 128
    last = t // tm - 1

    def body(z_ref, halo_ref, zn_ref, dy_ref, dyn_ref, w_ref, wt_ref, bias_ref, lng_ref, lnb_ref, pw_ref, ps_ref,
             dz_ref, dw_ref, db_ref, dlng_ref, dlnb_ref, dpw_ref, dps_ref):
        i = pl.program_id(0)
        _zero_when(i == 0, dw_ref, db_ref, dlng_ref, dlnb_ref, dpw_ref, dps_ref)
        u, v, gate = z_ref[:, 0:256], z_ref[:, 256:512], z_ref[:, 512:768]
        p, pgate = z_ref[:, 768:1024], z_ref[:, 1024:1280]
        dya, dyb = dy_ref[:, 0:256], dy_ref[:, 256:512]
        vn, vhat, rstd = _layer_norm(v, lng_ref[...], lnb_ref[...])
        vnb = _mx(vn)
        wm = _mx(jnp.where(_sgu_mask(False), w_ref[...], 0.0))
        wmt = _mx(jnp.where(_sgu_mask(True), wt_ref[...], 0.0))
        halo = jnp.where(i > 0, halo_ref[...], 0.0)
        pooled_b = _mx(_pooled(p, halo, i * tm))
        silu, dsilu = _silu_and_grad(gate)
        psilu, pdsilu = _silu_and_grad(pgate)
        mixed = _sgu_apply(wm, vnb, nblk) + jnp.tile(bias_ref[...], (nblk, 1))
        mixedp = _dot(pooled_b, pw_ref[...])
        t1 = u * mixed
        d_gate = dya * t1 * dsilu
        d_t1 = dya * silu
        d_u = d_t1 * mixed
        d_mixed = d_t1 * u
        dmb = _mx(d_mixed)
        d_pgate = dyb * (mixedp * ps_ref[...]) * pdsilu
        d_ms = dyb * psilu
        _acc(dps_ref, _colsum(d_ms * mixedp))
        dmpb = _mx(d_ms * ps_ref[...])
        dmp_halo = _mx(dyn_ref[...] * _silu_and_grad(zn_ref[...])[0] * ps_ref[...])
        d_vn = _sgu_apply(wmt, dmb, nblk)
        grp = _lane_group((128, 256))
        lane = lax.broadcasted_iota(jnp.int32, (128, 128), 1)
        dws = [jnp.zeros((128, 128), F32) for _ in range(4)]
        dbias = jnp.zeros((128, 128), F32)
        for n in range(nblk):
            dm_n, dmb_n, vnb_n = d_mixed[n * 128:(n + 1) * 128], dmb[n * 128:(n + 1) * 128], vnb[n * 128:(n + 1) * 128]
            for hh in range(4):
                dws[hh] = dws[hh] + _dot_nt(jnp.where(grp == hh, dmb_n, jnp.zeros_like(dmb_n)), vnb_n)
                rs = jnp.sum(jnp.where(grp == hh, dm_n, 0.0), axis=-1, keepdims=True)
                dbias = dbias + jnp.where(lane == hh, rs, 0.0)
        _acc(dw_ref, jnp.concatenate(dws, axis=0))
        _acc(db_ref, dbias)
        _acc(dpw_ref, _dot_tn(pooled_b, dmpb))
        d_pooled = _dot_nt(dmpb, pw_ref[...])
        d_pooled_halo = jnp.where(i < last, _dot_nt(dmp_halo, pw_ref[...]), 0.0)
        _acc(dlng_ref, _colsum(d_vn * vhat))
        _acc(dlnb_ref, _colsum(d_vn))
        dvh = d_vn * lng_ref[...]
        d_v = rstd * (dvh - jnp.mean(dvh, axis=-1, keepdims=True) - vhat * jnp.mean(dvh * vhat, axis=-1, keepdims=True))
        d_p = _pooled_bwd(d_pooled, d_pooled_halo, i * tm)
        dz_ref[...] = _mx(jnp.concatenate([d_u, d_v, d_gate, d_p, d_pgate], axis=1))

        @pl.when(i == last)
        def _():
            dw_ref[...] = jnp.where(_sgu_mask(False), dw_ref[...], 0.0)

    nxt = lambda i: jnp.minimum((i + 1) * (tm // 16), nt16 - 1)
    return pl.pallas_call(
        body, name=name, grid=(t // tm,),
        in_specs=[zrow, prev_halo, pl.BlockSpec((16, 256), lambda i: (nxt(i), 4)),
                  pl.BlockSpec((tm, 512), lambda i: (i, 0)), pl.BlockSpec((16, 256), lambda i: (nxt(i), 1)),
                  params[0], fixed(512, 128)] + params[1:],
        out_specs=[pl.BlockSpec((tm, Z_MIX), lambda i: (i, 0)), fixed(512, 128), fixed(128, 128), fixed(1, 256),
                   fixed(1, 256), fixed(256, 256), fixed(1, 256)],
        out_shape=[jax.ShapeDtypeStruct((t, Z_MIX), MXU_DTYPE), jax.ShapeDtypeStruct((512, 128), F32),
                   jax.ShapeDtypeStruct((128, 128), F32), jax.ShapeDtypeStruct((1, 256), F32),
                   jax.ShapeDtypeStruct((1, 256), F32), jax.ShapeDtypeStruct((256, 256), F32),
                   jax.ShapeDtypeStruct((1, 256), F32)],
        compiler_params=_cp())(z, z, z, dycat, dycat, sgu_w, sgu_wt, sgu_bias, ln_g, ln_b, pool_wbd, pool_scale)


def _rot_half(x, transpose):
    w = x.shape[1]
    lane = lax.broadcasted_iota(jnp.int32, x.shape, 1) % min(w, 256)
    base = 128 if w >= 256 else 0
    lo = jnp.logical_and(lane >= base, lane < base + 32)
    hi = jnp.logical_and(lane >= base + 32, lane < base + 64)
    up = pltpu.roll(x, w - 32, 1)
    down = pltpu.roll(x, 32, 1)
    if transpose:
        return jnp.where(lo, up, jnp.where(hi, -down, 0.0))
    return jnp.where(lo, -up, jnp.where(hi, down, 0.0))


def _rope(x, c, s):
    return x * c + _rot_half(x, False) * s


def _rope_bwd(dy, c, s):
    return dy * c + _rot_half(dy * s, True)


def _qkv_fwd(z, rc, rs, w_uq, w_ukv, gq, gkv, name, tm=512):
    t = z.shape[0]
    tm = min(tm, t)

    def body(zc_ref, zk_ref, rc_ref, rs_ref, wq_ref, wkv_ref, gq_ref, gkv_ref, q_ref, k_ref, v_ref):
        cq, ckv = zc_ref[:, 0:384], zc_ref[:, 384:640]
        c, s = rc_ref[...], rs_ref[...]
        qn, _ = _rms(cq, gq_ref[...])
        kvn, _ = _rms(ckv, gkv_ref[...])
        q_pre = _dot(_mx(qn), wq_ref[...])
        kv = _dot(_mx(kvn), wkv_ref[...])
        kpe = _rope(zk_ref[...], c[:, 128:256], s[:, 128:256])
        q = _rope(q_pre, jnp.tile(c, (1, N_HEADS)), jnp.tile(s, (1, N_HEADS)))
        for hh in range(N_HEADS):
            q_ref[hh] = _mx(q[:, hh * QK_PAD:(hh + 1) * QK_PAD])
            k_ref[hh] = _mx(jnp.concatenate([kv[:, hh * 128:(hh + 1) * 128], kpe], axis=1))
            v_ref[hh] = _mx(kv[:, 512 + hh * 128:512 + (hh + 1) * 128])

    fixed = lambda a, b: pl.BlockSpec((a, b), lambda i: (0, 0))
    heads = lambda wd: pl.BlockSpec((N_HEADS, tm, wd), lambda i: (0, i, 0))
    return pl.pallas_call(
        body, name=name, grid=(t // tm,),
        in_specs=[pl.BlockSpec((tm, Z_C), lambda i: (i, Z_MIX // Z_C)),
                  pl.BlockSpec((tm, Z_KR), lambda i: (i, (Z_MIX + Z_C) // Z_KR)),
                  pl.BlockSpec((tm, 256), lambda i: (i, 0)), pl.BlockSpec((tm, 256), lambda i: (i, 0)),
                  fixed(384, 1024), fixed(256, 1024), fixed(1, 384), fixed(1, 256)],
        out_specs=[heads(QK_PAD), heads(QK_PAD), heads(V_DIM)],
        out_shape=[jax.ShapeDtypeStruct((N_HEADS, t, QK_PAD), MXU_DTYPE),
                   jax.ShapeDtypeStruct((N_HEADS, t, QK_PAD), MXU_DTYPE),
                   jax.ShapeDtypeStruct((N_HEADS, t, V_DIM), MXU_DTYPE)],
        compiler_params=_cp())(z, z, rc, rs, w_uq, w_ukv, gq, gkv)


def _qkv_bwd(dq, dk, dv, z, rc, rs, w_uq, w_ukv, gq, gkv, name, tm=512):
    t = z.shape[0]
    tm = min(tm, t)

    def body(dq_ref, dk_ref, dv_ref, zc_ref, rc_ref, rs_ref, wq_ref, wkv_ref, gq_ref, gkv_ref,
             dzc_ref, dzk_ref, dwq_ref, dwkv_ref, dgq_ref, dgkv_ref):
        _zero_when(pl.program_id(0) == 0, dwq_ref, dwkv_ref, dgq_ref, dgkv_ref)
        cq, ckv = zc_ref[:, 0:384], zc_ref[:, 384:640]
        c, s = rc_ref[...], rs_ref[...]
        dkv = _mx(jnp.concatenate([dk_ref[hh][:, 0:128] for hh in range(N_HEADS)]
                                  + [dv_ref[hh] for hh in range(N_HEADS)], axis=1))
        kvn, rkv = _rms(ckv, gkv_ref[...])
        _acc(dwkv_ref, _dot_tn(_mx(kvn), dkv))
        d_kvn = _dot_nt(dkv, wkv_ref[...])
        dq_all = jnp.concatenate([dq_ref[hh] for hh in range(N_HEADS)], axis=1)
        dqp = _mx(_rope_bwd(dq_all, jnp.tile(c, (1, N_HEADS)), jnp.tile(s, (1, N_HEADS))))
        qn, rq = _rms(cq, gq_ref[...])
        dkpe = dk_ref[0][:, 128:256]
        for hh in range(1, N_HEADS):
            dkpe = dkpe + dk_ref[hh][:, 128:256]
        dzk_ref[...] = _mx(_rope_bwd(dkpe, c[:, 128:256], s[:, 128:256]))
        _acc(dwq_ref, _dot_tn(_mx(qn), dqp))
        d_qn = _dot_nt(dqp, wq_ref[...])
        d_ckv, dgkv_t = _rms_bwd(ckv, rkv, gkv_ref[...], d_kvn)
        _acc(dgkv_ref, _colsum(dgkv_t))
        d_cq, dgq_t = _rms_bwd(cq, rq, gq_ref[...], d_qn)
        _acc(dgq_ref, _colsum(dgq_t))
        dzc_ref[...] = _mx(jnp.concatenate([d_cq, d_ckv], axis=1))

    fixed = lambda a, b: pl.BlockSpec((a, b), lambda i: (0, 0))
    heads = lambda wd: pl.BlockSpec((N_HEADS, tm, wd), lambda i: (0, i, 0))
    return pl.pallas_call(
        body, name=name, grid=(t // tm,),
        in_specs=[heads(QK_PAD), heads(QK_PAD), heads(V_DIM), pl.BlockSpec((tm, Z_C), lambda i: (i, Z_MIX // Z_C)),
                  pl.BlockSpec((tm, 256), lambda i: (i, 0)), pl.BlockSpec((tm, 256), lambda i: (i, 0)),
                  fixed(384, 1024), fixed(256, 1024), fixed(1, 384), fixed(1, 256)],
        out_specs=[pl.BlockSpec((tm, Z_C), lambda i: (i, 0)), pl.BlockSpec((tm, Z_KR), lambda i: (i, 0)),
                   fixed(384, 1024), fixed(256, 1024), fixed(1, 384), fixed(1, 256)],
        out_shape=[jax.ShapeDtypeStruct((t, Z_C), MXU_DTYPE), jax.ShapeDtypeStruct((t, Z_KR), MXU_DTYPE),
                   jax.ShapeDtypeStruct((384, 1024), F32), jax.ShapeDtypeStruct((256, 1024), F32),
                   jax.ShapeDtypeStruct((1, 384), F32), jax.ShapeDtypeStruct((1, 256), F32)],
        compiler_params=_cp())(dq, dk, dv, z, rc, rs, w_uq, w_ukv, gq, gkv)


def _loop_in_long_trips(n, body, longest=4):
    def doubled(inner):
        return lambda t, carry: inner(2 * t + 1, inner(2 * t, carry))

    trips = [body]
    while 2 ** (len(trips) - 1) < longest:
        trips.append(doubled(trips[-1]))
    done = 0
    for level in reversed(range(len(trips))):
        size = 2 ** level
        end = n // size
        lax.fori_loop(done, end, trips[level], 0)
        done = 2 * end if level else end


def _head_loads(k_src, v_src, k_dst, v_dst, sems, first_rows):
    t = k_src.shape[0]
    parts = [(0, first_rows)] + ([(first_rows, t - first_rows)] if t > first_rows else [])
    return [pltpu.make_async_copy(src.at[pl.ds(r0, n)], dst.at[pl.ds(r0, n)], sems.at[2 * q + w])
            for q, (r0, n) in enumerate(parts) for w, (src, dst) in enumerate([(k_src, k_dst), (v_src, v_dst)])]


def _init_mask_bias(bias_ref):
    _, tq, tk = bias_ref.shape
    r = lax.broadcasted_iota(jnp.int32, (tq, tk), 0) // CHUNK
    c = lax.broadcasted_iota(jnp.int32, (tq, tk), 1) // CHUNK
    bias_ref[0] = jnp.zeros((tq, tk), F32)
    for d in range(tq // tk):
        bias_ref[1 + d] = jnp.where(c + d * (tk // CHUNK) <= r, 0.0, NEG_INF)


def _gate_block(tq):
    return pl.BlockSpec((tq, 128), lambda h, i: (i, (Z_MIX + Z_C + Z_KR) // 128 + h))


def _attn_fwd(qh, kh, vh, z, name, tq=1024, tk=512):
    t = qh.shape[1]
    tq = min(tq, t)
    tk = min(tk, tq)
    ratio = tq // tk

    def body(q_ref, g_ref, k_hbm, v_hbm, o_ref, yc_ref, lse_ref, k_v, v_v, m_s, acc_s, s_a, s_b, mx_a, mx_b, bias_s,
             sem):
        h, i = pl.program_id(0), pl.program_id(1)

        loads = _head_loads(k_hbm.at[h], v_hbm.at[h], k_v, v_v.at[:, 0:V_DIM], sem, tq)

        @pl.when(i == 0)
        def _():
            for cp in loads:
                cp.start()
            v_v[:, V_DIM:2 * V_DIM] = jnp.ones((t, V_DIM), MXU_DTYPE)
            _init_mask_bias(bias_s)
            for cp in loads[:2]:
                cp.wait()

        @pl.when(i == 1)
        def _():
            for cp in loads[2:]:
                cp.wait()

        q = q_ref[...]
        m_s[...] = jnp.full(m_s.shape, NEG_INF, F32)
        acc_s[...] = jnp.zeros(acc_s.shape, F32)

        last = ratio * (i + 1) - 1

        def keys(j):
            return pl.ds(pl.multiple_of(j * tk, tk), tk)

        def scores(s_ref, mx_ref, j, biased):
            s = _dot_nt(q, k_v[keys(j), :]) * (SCALE * LOG2E)
            if biased:
                s = s + bias_s[jnp.maximum(j - ratio * i + 1, 0)]
            s_ref[...] = s
            mx_ref[...] = jnp.broadcast_to(jnp.max(s, axis=-1, keepdims=True), mx_ref.shape)

        def softmax_pv(s_ref, mx_ref, j):
            m_old = m_s[...]
            m_new = jnp.maximum(m_old, mx_ref[...])
            p = jnp.exp2(s_ref[...] - jnp.tile(m_new, (1, tk // 128)))
            alpha = jnp.exp2(m_old - m_new)
            m_s[...] = m_new
            acc_s[...] = jnp.tile(alpha, (1, 2)) * acc_s[...] + _dot(_mx(p), v_v[keys(j), :])

        scores(s_a, mx_a, 0, True)

        def pair(pp, carry, biased):
            scores(s_b, mx_b, 2 * pp + 1, biased)
            softmax_pv(s_a, mx_a, 2 * pp)
            scores(s_a, mx_a, jnp.minimum(2 * pp + 2, last), biased)
            softmax_pv(s_b, mx_b, 2 * pp + 1)
            return carry

        n_pairs = (last + 1) // 2
        n_plain = jnp.maximum(ratio * i // 2 - 1, 0)
        _loop_in_long_trips(n_plain, lambda pp, carry: pair(pp, carry, False), longest=8)
        _loop_in_long_trips(n_pairs - n_plain, lambda pp, carry: pair(n_plain + pp, carry, True))
        if ratio % 2 == 1:
            @pl.when(last % 2 == 0)
            def _():
                softmax_pv(s_a, mx_a, last)

        l = acc_s[:, V_DIM:2 * V_DIM]
        o = acc_s[:, 0:V_DIM] / l
        o_ref[...] = o
        yc_ref[...] = _mx(o * _silu_and_grad(g_ref[...])[0])
        lse_ref[...] = m_s[...] + jnp.log2(l)

    return pl.pallas_call(
        body, name=name, grid=(N_HEADS, t // tq),
        in_specs=[pl.BlockSpec((None, tq, QK_PAD), lambda h, i: (h, i, 0)), _gate_block(tq), ANY, ANY],
        out_specs=[pl.BlockSpec((tq, 128), lambda h, i: (i, h)), pl.BlockSpec((tq, 128), lambda h, i: (i, h)),
                   pl.BlockSpec((None, tq, 128), lambda h, i: (h, i, 0))],
        out_shape=[jax.ShapeDtypeStruct((t, N_HEADS * V_DIM), F32), jax.ShapeDtypeStruct((t, N_HEADS * V_DIM), MXU_DTYPE),
                   jax.ShapeDtypeStruct((N_HEADS, t, 128), F32)],
        scratch_shapes=[pltpu.VMEM((t, QK_PAD), MXU_DTYPE), pltpu.VMEM((t, 2 * V_DIM), MXU_DTYPE),
                        pltpu.VMEM((tq, 128), F32), pltpu.VMEM((tq, 2 * V_DIM), F32),
                        pltpu.VMEM((tq, tk), F32), pltpu.VMEM((tq, tk), F32), pltpu.VMEM((tq, 128), F32),
                        pltpu.VMEM((tq, 128), F32), pltpu.VMEM((ratio + 1, tq, tk), F32),
                        pltpu.SemaphoreType.DMA((4,))],
        compiler_params=_cp(2))(qh, z, kh, vh)


def _attn_bwd(qh, kh, vh, o, lse, dycat, z, name, tq=512):
    t = qh.shape[1]
    tq = min(tq, t)
    nq = t // tq

    def body(q_ref, o_ref, lse_ref, dy_ref, g_ref, k_hbm, v_hbm, dq_ref, dgate_ref, dk_hbm, dv_hbm,
             k_v, v_v, dk_acc, dv_acc, dq_acc, delta_s, s_a, dp_a, s_b, dp_b, bias_s, sem):
        h, i = pl.program_id(0), pl.program_id(1)

        loads = _head_loads(k_hbm.at[h], v_hbm.at[h], k_v, v_v, sem, tq)

        @pl.when(i == 0)
        def _():
            for cp in loads:
                cp.start()
            _init_mask_bias(bias_s)
            dk_acc[...] = jnp.zeros(dk_acc.shape, F32)
            dv_acc[...] = jnp.zeros(dv_acc.shape, F32)
            for cp in loads[:2]:
                cp.wait()

        @pl.when(i == 1)
        def _():
            for cp in loads[2:]:
                cp.wait()

        gate, dy, of = g_ref[...], dy_ref[...], o_ref[...]
        silu, dsilu = _silu_and_grad(gate)
        do = dy * silu
        delta = jnp.sum(do * of, axis=-1, keepdims=True)
        dgate_ref[...] = _mx(dy * of * dsilu)
        dob = _mx(do)
        q = q_ref[...]
        delta_s[...] = jnp.broadcast_to(delta, delta_s.shape)
        dq_acc[...] = jnp.zeros(dq_acc.shape, F32)

        def keys(j):
            return pl.ds(pl.multiple_of(j * tq, tq), tq)

        def scores(s_ref, dp_ref, j):
            s = _dot_nt(q, k_v[keys(j), :]) * (SCALE * LOG2E) + bias_s[(j == i).astype(jnp.int32)]
            s_ref[...] = s - jnp.tile(lse_ref[...], (1, tq // 128))
            dp_ref[...] = _dot_nt(dob, v_v[keys(j), :]) - jnp.tile(delta_s[...], (1, tq // 128))

        def grads(s_ref, dp_ref, j):
            ks = keys(j)
            p = jnp.exp2(s_ref[...])
            ds = p * dp_ref[...] * SCALE
            pb, dsb = _mx(p), _mx(ds)
            dq_acc[...] += _dot(dsb, k_v[ks, :])
            dk_acc[ks, :] += _dot_tn(dsb, q)
            dv_acc[ks, :] += _dot_tn(pb, dob)

        scores(s_a, dp_a, 0)

        def pair(pp, carry):
            scores(s_b, dp_b, 2 * pp + 1)
            grads(s_a, dp_a, 2 * pp)
            scores(s_a, dp_a, jnp.minimum(2 * pp + 2, i))
            grads(s_b, dp_b, 2 * pp + 1)
            return carry

        _loop_in_long_trips((i + 1) // 2, pair, longest=8)

        @pl.when(i % 2 == 0)
        def _():
            grads(s_a, dp_a, i)

        dq_ref[...] = dq_acc[...]

        @pl.when(i == nq - 1)
        def _():
            ck = pltpu.make_async_copy(dk_acc, dk_hbm.at[h], sem.at[0])
            cv = pltpu.make_async_copy(dv_acc, dv_hbm.at[h], sem.at[1])
            ck.start()
            cv.start()
            ck.wait()
            cv.wait()

    return pl.pallas_call(
        body, name=name, grid=(N_HEADS, nq),
        in_specs=[pl.BlockSpec((None, tq, QK_PAD), lambda h, i: (h, i, 0)),
                  pl.BlockSpec((tq, 128), lambda h, i: (i, h)),
                  pl.BlockSpec((None, tq, 128), lambda h, i: (h, i, 0)),
                  pl.BlockSpec((tq, 128), lambda h, i: (i, N_HEADS + h)), _gate_block(tq), ANY, ANY],
        out_specs=[pl.BlockSpec((None, tq, QK_PAD), lambda h, i: (h, i, 0)),
                   pl.BlockSpec((tq, 128), lambda h, i: (i, h)), ANY, ANY],
        out_shape=[jax.ShapeDtypeStruct((N_HEADS, t, QK_PAD), F32), jax.ShapeDtypeStruct((t, Z_GATE), MXU_DTYPE),
                   jax.ShapeDtypeStruct((N_HEADS, t, QK_PAD), F32), jax.ShapeDtypeStruct((N_HEADS, t, V_DIM), F32)],
        scratch_shapes=[pltpu.VMEM((t, QK_PAD), MXU_DTYPE), pltpu.VMEM((t, V_DIM), MXU_DTYPE),
                        pltpu.VMEM((t, QK_PAD), F32), pltpu.VMEM((t, V_DIM), F32), pltpu.VMEM((tq, QK_PAD), F32),
                        pltpu.VMEM((tq, 128), F32)] + [pltpu.VMEM((tq, tq), F32)] * 4
        + [pltpu.VMEM((2, tq, tq), F32), pltpu.SemaphoreType.DMA((4,))],
        compiler_params=_cp(2))(qh, o, lse, dycat, z, kh, vh)


def _out_proj_fwd(yab, yc, w, x, g, target, name, tm=512):
    t, d = x.shape
    tm = min(tm, t)
    is_last = target is not None

    def body(*refs):
        if is_last:
            yab_ref, yc_ref, w_ref, x_ref, g_ref, t_ref, y_ref, dout_ref, loss_ref = refs
            _zero_when(pl.program_id(0) == 0, loss_ref)
        else:
            yab_ref, yc_ref, w_ref, x_ref, g_ref, y_ref, out_ref = refs
        y = _dot(jnp.concatenate([yab_ref[...], yc_ref[...]], axis=1), w_ref[...])
        y_ref[...] = y
        out = x_ref[...] + _rms(y, g_ref[...])[0]
        if is_last:
            diff = out - t_ref[...]
            dout_ref[...] = diff * (1.0 / d)
            part = jnp.sum(jnp.sum(diff * diff, axis=-1, keepdims=True), axis=0, keepdims=True) * (0.5 / d)
            _acc(loss_ref, jnp.broadcast_to(part, (1, 128)))
        else:
            out_ref[...] = out

    row = lambda wd: pl.BlockSpec((tm, wd), lambda i: (i, 0))
    fixed = lambda a, b: pl.BlockSpec((a, b), lambda i: (0, 0))
    in_specs = [row(512), row(512), fixed(d, d), row(d), fixed(1, d)]
    args = [yab, yc, w, x, g]
    out_specs = [row(d), row(d)]
    out_shape = [jax.ShapeDtypeStruct((t, d), F32), jax.ShapeDtypeStruct((t, d), F32)]
    if is_last:
        in_specs.append(row(d))
        args.append(target)
        out_specs.append(fixed(1, 128))
        out_shape.append(jax.ShapeDtypeStruct((1, 128), F32))
    return pl.pallas_call(body, name=name, grid=(t // tm,), in_specs=in_specs, out_specs=out_specs,
                          out_shape=out_shape, compiler_params=_cp())(*args)


def _halves(tm):
    half = tm // 2 if tm >= 512 else tm
    return [pl.ds(s, half) for s in range(0, tm, half)]


def _out_proj_bwd(dout, y, yab, yc, w, g, name, tm=1024):
    t, d = y.shape
    tm = min(tm, t)

    def body(dout_ref, y_ref, yab_ref, yc_ref, w_ref, g_ref, dycat_ref, dw_ref, dg_ref):
        _zero_when(pl.program_id(0) == 0, dw_ref, dg_ref)
        dybs = []
        for rows in _halves(tm):
            y = y_ref[rows, :]
            r = lax.rsqrt(jnp.mean(y * y, axis=-1, keepdims=True) + EPS)
            dy, dgt = _rms_bwd(y, r, g_ref[...], dout_ref[rows, :])
            _acc(dg_ref, _colsum(dgt))
            dybs.append(_mx(dy))
        for rows, dyb in zip(_halves(tm), dybs):
            _acc(dw_ref, _dot_tn(jnp.concatenate([yab_ref[rows, :], yc_ref[rows, :]], axis=1), dyb))
            dycat_ref[rows, :] = _dot_nt(dyb, w_ref[...])

    row = lambda wd: pl.BlockSpec((tm, wd), lambda i: (i, 0))
    fixed = lambda a, b: pl.BlockSpec((a, b), lambda i: (0, 0), pipeline_mode=pl.Buffered(1))
    return pl.pallas_call(
        body, name=name, grid=(t // tm,),
        in_specs=[row(d), row(d), row(512), row(512), fixed(d, d), fixed(1, d)],
        out_specs=[row(d), fixed(d, d), fixed(1, d)],
        out_shape=[jax.ShapeDtypeStruct((t, d), F32), jax.ShapeDtypeStruct((d, d), F32),
                   jax.ShapeDtypeStruct((1, d), F32)],
        compiler_params=_cp())(dout, y, yab, yc, w, g)


def _mesh_pos():
    return lax.axis_index("x"), lax.axis_index("y"), lax.axis_index("c")


def _remote(src, dst, send_sem, recv_sem, to):
    return pltpu.make_async_remote_copy(src_ref=src, dst_ref=dst, send_sem=send_sem, recv_sem=recv_sem,
                                        device_id=to, device_id_type=MESH)


CHUNK_ROWS = 256


def _pieces(rows):
    return [(s, min(CHUNK_ROWS, rows - s)) for s in range(0, rows, CHUNK_ROWS)]


def _piece_table(shapes):
    return [(a, s, sz) for a, shp in enumerate(shapes) for s, sz in _pieces(shp[-2])]


def _gather_weights(shards):
    n = len(shards)
    table = _piece_table([s.shape for s in shards])
    npc = len(table)

    def body(*refs):
        ins, outs = refs[:n], refs[n:2 * n]
        send_sems, recv_sems, fwd_send, fwd_recv = refs[2 * n:]
        x, y, c = _mesh_pos()
        me, sibling = (x, y, c), (x, y, 1 - c)
        chips = [(1 - x, y), (x, 1 - y), (1 - x, 1 - y)]
        slot = lambda cx, cy, layer: 2 * (2 * cx + cy) + layer
        first = []
        for a in range(n):
            for j, (cx, cy) in enumerate(chips):
                first.append(_remote(ins[a].at[c], outs[a].at[slot(x, y, c)], send_sems.at[a, j], recv_sems.at[a, j],
                                     (cx, cy, c)))
                first[-1].start()
        passed = []
        for j, (cx, cy) in enumerate(chips):
            for a in range(n):
                blk = outs[a].at[slot(cx, cy, c)]
                _remote(blk, blk, send_sems.at[a, j], recv_sems.at[a, j], me).wait_recv()
            for q, (a, s, sz) in enumerate(table):
                rows = outs[a].at[slot(cx, cy, c), pl.ds(s, sz)]
                passed.append(_remote(rows, rows, fwd_send.at[j, q], fwd_recv.at[j, q], sibling))
                passed[-1].start()
        for j, (cx, cy) in enumerate(chips):
            for q, (a, s, sz) in enumerate(table):
                rows = outs[a].at[slot(cx, cy, 1 - c), pl.ds(s, sz)]
                _remote(rows, rows, fwd_send.at[j, q], fwd_recv.at[j, q], me).wait_recv()
        for cp in first + passed:
            cp.wait_send()

    return pl.pallas_call(
        body, name="gather_weights", in_specs=[ANY] * n, out_specs=[ANY] * n,
        out_shape=[jax.ShapeDtypeStruct((8,) + s.shape[1:], s.dtype) for s in shards],
        scratch_shapes=[pltpu.SemaphoreType.DMA((n, 3)), pltpu.SemaphoreType.DMA((n, 3)),
                        pltpu.SemaphoreType.DMA((3, npc)), pltpu.SemaphoreType.DMA((3, npc))])(*shards)


def _pair_exchange(parts, common):
    n = len(parts)
    table = _piece_table([p.shape for p in parts] + [common.shape])
    npc = len(table)

    def body(*refs):
        ins, outs = refs[:n + 1], refs[n + 1:2 * n + 2]
        send_sems, recv_sems = refs[2 * n + 2:]
        x, y, c = _mesh_pos()
        sent = []
        for k in range(4):
            for q, (a, s, sz) in enumerate(table):
                if a == n and k > 0:
                    continue
                src = ins[a].at[2 * k + 1 - c, pl.ds(s, sz)] if a < n else ins[a].at[pl.ds(s, sz)]
                dst = outs[a].at[k, pl.ds(s, sz)] if a < n else outs[a].at[pl.ds(s, sz)]
                sent.append(_remote(src, dst, send_sems.at[k, q], recv_sems.at[k, q], (x, y, 1 - c)))
                sent[-1].start()
        for k in range(4):
            for q, (a, s, sz) in enumerate(table):
                if a == n and k > 0:
                    continue
                dst = outs[a].at[k, pl.ds(s, sz)] if a < n else outs[a].at[pl.ds(s, sz)]
                _remote(dst, dst, send_sems.at[k, q], recv_sems.at[k, q], (x, y, c)).wait_recv()
        for cp in sent:
            cp.wait_send()

    return pl.pallas_call(
        body, name="grad_pair_exchange", in_specs=[ANY] * (n + 1), out_specs=[ANY] * (n + 1),
        out_shape=[jax.ShapeDtypeStruct((4,) + p.shape[1:], p.dtype) for p in parts]
        + [jax.ShapeDtypeStruct(common.shape, common.dtype)],
        scratch_shapes=[pltpu.SemaphoreType.DMA((4, npc)), pltpu.SemaphoreType.DMA((4, npc))])(*parts, common)


def _chip_exchange(parts, common):
    n = len(parts)
    table = _piece_table([p.shape for p in parts] + [common.shape])
    npc = len(table)

    def body(*refs):
        ins, outs = refs[:n + 1], refs[n + 1:2 * n + 2]
        send_sems, recv_sems = refs[2 * n + 2:]
        x, y, c = _mesh_pos()
        mine = 2 * x + y
        chips = [(1 - x, y), (x, 1 - y), (1 - x, 1 - y)]
        src = lambda a, k: ins[a].at[k] if a < n else ins[a]
        sent = []
        for j, (cx, cy) in enumerate(chips):
            for q, (a, s, sz) in enumerate(table):
                sent.append(_remote(src(a, 2 * cx + cy).at[pl.ds(s, sz)], outs[a].at[mine, pl.ds(s, sz)],
                                    send_sems.at[j, q], recv_sems.at[j, q], (cx, cy, c)))
                sent[-1].start()
        for j, (cx, cy) in enumerate(chips):
            for q, (a, s, sz) in enumerate(table):
                dst = outs[a].at[2 * cx + cy, pl.ds(s, sz)]
                _remote(dst, dst, send_sems.at[j, q], recv_sems.at[j, q], (x, y, c)).wait_recv()
        for cp in sent:
            cp.wait_send()

    return pl.pallas_call(
        body, name="grad_chip_exchange", in_specs=[ANY] * (n + 1), out_specs=[ANY] * (n + 1),
        out_shape=[jax.ShapeDtypeStruct(p.shape, p.dtype) for p in parts]
        + [jax.ShapeDtypeStruct((4,) + common.shape, common.dtype)],
        scratch_shapes=[pltpu.SemaphoreType.DMA((3, npc)), pltpu.SemaphoreType.DMA((3, npc))])(*parts, common)


def _sibling_exchange(sums):
    n = len(sums)
    table = _piece_table([s.shape for s in sums])
    npc = len(table)

    def body(*refs):
        ins, outs = refs[:n], refs[n:2 * n]
        send_sems, recv_sems = refs[2 * n:]
        x, y, c = _mesh_pos()
        sent = []
        for q, (a, s, sz) in enumerate(table):
            sent.append(_remote(ins[a].at[pl.ds(s, sz)], outs[a].at[pl.ds(s, sz)], send_sems.at[q], recv_sems.at[q],
                                (x, y, 1 - c)))
            sent[-1].start()
        for q, (a, s, sz) in enumerate(table):
            dst = outs[a].at[pl.ds(s, sz)]
            _remote(dst, dst, send_sems.at[q], recv_sems.at[q], (x, y, c)).wait_recv()
        for cp in sent:
            cp.wait_send()

    return pl.pallas_call(
        body, name="sibling_exchange", in_specs=[ANY] * n, out_specs=[ANY] * n,
        out_shape=[jax.ShapeDtypeStruct(s.shape, s.dtype) for s in sums],
        scratch_shapes=[pltpu.SemaphoreType.DMA((npc,)), pltpu.SemaphoreType.DMA((npc,))])(*sums)


def _pair_sum(parts, got, name, tr):
    _, r, c = got.shape
    tr = min(tr, r)

    def body(p_ref, g_ref, o_ref, w_ref):
        total = p_ref[...] + g_ref[...]
        o_ref[...] = total
        w_ref[...] = total.astype(WIRE_DTYPE)

    blk = pl.BlockSpec((None, tr, c), lambda k, i: (k, i, 0))
    mine = pl.BlockSpec((None, tr, c), lambda k, i: (2 * k + lax.axis_index("c"), i, 0))
    return pl.pallas_call(
        body, name=name, grid=(4, r // tr), in_specs=[mine, blk], out_specs=[blk, blk],
        out_shape=[jax.ShapeDtypeStruct(got.shape, F32), jax.ShapeDtypeStruct(got.shape, WIRE_DTYPE)],
        compiler_params=_cp(2))(parts, got)


def _sum_chips(pair_sums, recv, name, tr):
    _, r, c = pair_sums.shape
    tr = min(tr, r)

    def body(own_ref, r_ref, o_ref):
        chip = 2 * lax.axis_index("x") + lax.axis_index("y")
        own_blk = own_ref[...]
        acc = jnp.where(chip == 0, own_blk, r_ref[0].astype(F32))
        for k in range(1, 4):
            acc = acc + jnp.where(chip == k, own_blk, r_ref[k].astype(F32))
        o_ref[...] = acc

    return pl.pallas_call(
        body, name=name, grid=(r // tr,),
        in_specs=[pl.BlockSpec((None, tr, c), lambda i: (2 * lax.axis_index("x") + lax.axis_index("y"), i, 0)),
                  pl.BlockSpec((4, tr, c), lambda i: (0, i, 0))],
        out_specs=pl.BlockSpec((tr, c), lambda i: (i, 0)),
        out_shape=jax.ShapeDtypeStruct((r, c), F32), compiler_params=_cp())(pair_sums, recv)


def _sum_leading(parts, name, tr):
    nlead, r, c = parts.shape
    tr = min(tr, r)

    def body(p_ref, o_ref):
        acc = p_ref[0]
        for j in range(1, nlead):
            acc = acc + p_ref[j]
        o_ref[...] = acc

    return pl.pallas_call(
        body, name=name, grid=(r // tr,),
        in_specs=[pl.BlockSpec((nlead, tr, c), lambda i: (0, i, 0))], out_specs=pl.BlockSpec((tr, c), lambda i: (i, 0)),
        out_shape=jax.ShapeDtypeStruct((r, c), parts.dtype), compiler_params=_cp())(parts)


def _adamw_update(w_ref, g_ref, m_ref, v_ref, d_ref, nm_ref, nv_ref):
    gg = g_ref[...]
    nm = ADAM_B1 * m_ref[...] + (1.0 - ADAM_B1) * gg
    nv = ADAM_B2 * v_ref[...] + (1.0 - ADAM_B2) * jnp.square(gg)
    m_hat = nm / (1.0 - ADAM_B1 ** ADAM_STEP)
    v_hat = nv / (1.0 - ADAM_B2 ** ADAM_STEP)
    d_ref[...] = -ADAM_LR * (m_hat / (jnp.sqrt(v_hat) + ADAM_EPS) + ADAM_WD * w_ref[...])
    nm_ref[...] = nm
    nv_ref[...] = nv


def _adamw(w, g, m, v, name, tr):
    r, c = w.shape
    tr = min(tr, r)

    def body(w_ref, g_ref, m_ref, v_ref, d_ref, nm_ref, nv_ref):
        _adamw_update(w_ref, g_ref, m_ref, v_ref, d_ref, nm_ref, nv_ref)

    blk = pl.BlockSpec((tr, c), lambda i: (i, 0))
    return pl.pallas_call(
        body, name=name, grid=(r // tr,), in_specs=[blk] * 4, out_specs=[blk] * 3,
        out_shape=[jax.ShapeDtypeStruct((r, c), F32)] * 3, compiler_params=_cp())(w, g, m, v)


def _adamw_many(ws, gs, ms, vs, name):
    n = len(ws)

    def body(*refs):
        ins, outs = refs[:4 * n], refs[4 * n:]
        for k in range(n):
            _adamw_update(ins[k], ins[n + k], ins[2 * n + k], ins[3 * n + k], outs[k], outs[n + k], outs[2 * n + k])

    vmem = pl.BlockSpec(memory_space=pltpu.VMEM)
    res = pl.pallas_call(
        body, name=name, in_specs=[vmem] * (4 * n), out_specs=[vmem] * (3 * n),
        out_shape=[jax.ShapeDtypeStruct(a.shape, F32) for a in ws] * 3)(*ws, *gs, *ms, *vs)
    return res[:n], res[n:2 * n], res[2 * n:]


def _rope_tables(positions):
    inv_freq = ROPE_BASE ** (-jnp.arange(0, 64, 2, dtype=F32) / 64)
    ang = positions.astype(F32)[:, None] * inv_freq
    cos, sin = jnp.cos(ang), jnp.sin(ang)
    t = positions.shape[0]
    rc = jnp.concatenate([jnp.ones((t, 128), F32), cos, cos, jnp.ones((t, 64), F32)], axis=1)
    rs = jnp.concatenate([jnp.zeros((t, 128), F32), sin, sin, jnp.zeros((t, 64), F32)], axis=1)
    return rc, rs


def _layer_params(l, w_in, w_uq, w_ukv, w_out, small):
    p = {}
    p["w_in"] = jnp.concatenate([w_in[l][:, :1984], jnp.zeros((1024, 64), w_in.dtype), w_in[l][:, 1984:]], axis=1)
    p["w_uq"] = jnp.pad(w_uq[l].reshape(384, 4, 192), ((0, 0), (0, 0), (0, 64))).reshape(384, 1024)
    p["w_ukv"] = w_ukv[l].reshape(256, 4, 2, 128).transpose(0, 2, 1, 3).reshape(256, 1024)
    p["w_out"] = w_out[l]
    p["pre_g"] = small["pre_norm_g"][l][None]
    p["post_g"] = small["post_norm_g"][l][None]
    p["sgu_w"] = small["sgu_w"][l].reshape(512, 128)
    p["sgu_wt"] = small["sgu_w"][l].transpose(0, 2, 1).reshape(512, 128)
    p["sgu_bias"] = jnp.repeat(small["sgu_b"][l].T, 64, axis=1)
    p["ln_g"] = small["sgu_ln_g"][l][None]
    p["ln_b"] = small["sgu_ln_b"][l][None]
    p["pool_wbd"] = _mx(jax.scipy.linalg.block_diag(*[small["pool_w"][l][gi] for gi in range(4)]))
    p["pool_scale"] = small["pool_scale"][l][None]
    p["gq"] = small["q_norm_g"][l][None]
    p["gkv"] = small["kv_norm_g"][l][None]
    return p


def _layer_fwd(l, x, p, rc, rs, target):
    z, h = _in_proj_fwd(x, p["pre_g"], p["w_in"], f"in_proj_fwd_{l}")
    yab = _mix_fwd(z, p["sgu_w"], p["sgu_bias"], p["ln_g"], p["ln_b"], p["pool_wbd"], p["pool_scale"], f"mix_fwd_{l}")
    qh, kh, vh = _qkv_fwd(z, rc, rs, p["w_uq"], p["w_ukv"], p["gq"], p["gkv"], f"qkv_fwd_{l}")
    o, yc, lse = _attn_fwd(qh, kh, vh, z, f"attn_fwd_{l}")
    outs = _out_proj_fwd(yab, yc, p["w_out"], x, p["post_g"], target, f"out_proj_fwd_{l}")
    saved = dict(x=x, z=z, h=h, yab=yab, qh=qh, kh=kh, vh=vh, o=o, yc=yc, lse=lse, y=outs[0])
    return saved, outs[1:]


def _layer_bwd(l, dout, sv, p, rc, rs):
    dycat, dw_out, dpost = _out_proj_bwd(dout, sv["y"], sv["yab"], sv["yc"], p["w_out"], p["post_g"], f"out_proj_bwd_{l}")
    dq, dgate, dk, dv = _attn_bwd(sv["qh"], sv["kh"], sv["vh"], sv["o"], sv["lse"], dycat, sv["z"], f"attn_bwd_{l}")
    dzc, dzk, dwq, dwkv, dgq, dgkv = _qkv_bwd(dq, dk, dv, sv["z"], rc, rs, p["w_uq"], p["w_ukv"], p["gq"], p["gkv"],
                                              f"qkv_bwd_{l}")
    dzm, dsw, dsb, dlng, dlnb, dpw, dps = _mix_bwd(sv["z"], dycat, p["sgu_w"], p["sgu_wt"], p["sgu_bias"], p["ln_g"],
                                                   p["ln_b"], p["pool_wbd"], p["pool_scale"], f"mix_bwd_{l}")
    dx, dw_in, dpre = _in_proj_bwd(dzm, dzc, dzk, dgate, sv["h"], sv["x"], dout, p["w_in"], p["pre_g"], f"in_proj_bwd_{l}")
    grads = {
        "pre_norm_g": dpre[0], "post_norm_g": dpost[0],
        "w_in": jnp.concatenate([dw_in[:, :1984], dw_in[:, 2048:]], axis=1),
        "sgu_w": dsw.reshape(4, 128, 128), "sgu_b": dsb[:, :4].T, "sgu_ln_g": dlng[0], "sgu_ln_b": dlnb[0],
        "pool_w": jnp.stack([dpw[64 * gi:64 * gi + 64, 64 * gi:64 * gi + 64] for gi in range(4)]),
        "pool_scale": dps[0], "q_norm_g": dgq[0],
        "w_uq": dwq.reshape(384, 4, 256)[:, :, :192].reshape(384, 768), "kv_norm_g": dgkv[0],
        "w_ukv": dwkv.reshape(256, 2, 4, 128).transpose(0, 2, 1, 3).reshape(256, 1024), "w_out": dw_out,
    }
    return dx, grads


SMALL_NAMES = ["pre_norm_g", "post_norm_g", "sgu_w", "sgu_b", "sgu_ln_g", "sgu_ln_b", "pool_w", "pool_scale",
               "q_norm_g", "kv_norm_g"]
BIG_NAMES = ["w_in", "w_uq", "w_ukv", "w_out"]
WEIGHT_NAMES = ["pre_norm_g", "post_norm_g", "w_in", "sgu_w", "sgu_b", "sgu_ln_g", "sgu_ln_b", "pool_w", "pool_scale",
                "q_norm_g", "w_uq", "kv_norm_g", "w_ukv", "w_out"]


def _local_step(x, positions, target, w_in, w_uq, w_ukv, w_out, small):
    rc, rs = _rope_tables(positions)
    params = [_layer_params(l, w_in, w_uq, w_ukv, w_out, small) for l in range(DEPTH)]
    saved = []
    for l in range(DEPTH):
        sv, outs = _layer_fwd(l, x, params[l], rc, rs, target if l == DEPTH - 1 else None)
        saved.append(sv)
        if l < DEPTH - 1:
            x = outs[0]
    dout, loss = outs
    grads = [None] * DEPTH
    for l in reversed(range(DEPTH)):
        dout, grads[l] = _layer_bwd(l, dout, saved[l], params[l], rc, rs)
    return loss[0, 0], dout, {k: jnp.stack([grads[l][k] for l in range(DEPTH)]) for k in WEIGHT_NAMES}


def _pack_small(tree, extra=None):
    pieces = [tree[k].reshape(-1) for k in SMALL_NAMES]
    pieces.append(jnp.zeros((1,), F32) if extra is None else extra.reshape(1))
    flat = jnp.concatenate(pieces)
    rows = -(-flat.shape[0] // 1024) * 8
    return jnp.pad(flat, (0, rows * 128 - flat.shape[0])).reshape(rows, 128)


def _unpack_small(packed, like):
    flat = packed.reshape(-1)
    out, off = {}, 0
    for k in SMALL_NAMES:
        size = like[k].size
        out[k] = flat[off:off + size].reshape(like[k].shape)
        off += size
    return out, flat[off]


def kernel(x, positions, pre_norm_g, post_norm_g, w_in, sgu_w, sgu_b, sgu_ln_g, sgu_ln_b, pool_w, pool_scale, q_norm_g, w_uq, kv_norm_g, w_ukv, w_out, loss_target, m_pre_norm_g, m_post_norm_g, m_w_in, m_sgu_w, m_sgu_b, m_sgu_ln_g, m_sgu_ln_b, m_pool_w, m_pool_scale, m_q_norm_g, m_w_uq, m_kv_norm_g, m_w_ukv, m_w_out, v_pre_norm_g, v_post_norm_g, v_w_in, v_sgu_w, v_sgu_b, v_sgu_ln_g, v_sgu_ln_b, v_pool_w, v_pool_scale, v_q_norm_g, v_w_uq, v_kv_norm_g, v_w_ukv, v_w_out):
    w = dict(pre_norm_g=pre_norm_g, post_norm_g=post_norm_g, w_in=w_in, sgu_w=sgu_w, sgu_b=sgu_b, sgu_ln_g=sgu_ln_g,
             sgu_ln_b=sgu_ln_b, pool_w=pool_w, pool_scale=pool_scale, q_norm_g=q_norm_g, w_uq=w_uq, kv_norm_g=kv_norm_g,
             w_ukv=w_ukv, w_out=w_out)
    m = dict(pre_norm_g=m_pre_norm_g, post_norm_g=m_post_norm_g, w_in=m_w_in, sgu_w=m_sgu_w, sgu_b=m_sgu_b,
             sgu_ln_g=m_sgu_ln_g, sgu_ln_b=m_sgu_ln_b, pool_w=m_pool_w, pool_scale=m_pool_scale, q_norm_g=m_q_norm_g,
             w_uq=m_w_uq, kv_norm_g=m_kv_norm_g, w_ukv=m_w_ukv, w_out=m_w_out)
    v = dict(pre_norm_g=v_pre_norm_g, post_norm_g=v_post_norm_g, w_in=v_w_in, sgu_w=v_sgu_w, sgu_b=v_sgu_b,
             sgu_ln_g=v_sgu_ln_g, sgu_ln_b=v_sgu_ln_b, pool_w=v_pool_w, pool_scale=v_pool_scale, q_norm_g=v_q_norm_g,
             w_uq=v_w_uq, kv_norm_g=v_kv_norm_g, w_ukv=v_w_ukv, w_out=v_w_out)

    core = lax.axis_index("c")
    chip = 2 * lax.axis_index("x") + lax.axis_index("y")
    shards = [_mx(w[k]) for k in BIG_NAMES]
    gathered = _gather_weights(shards)
    g_in, g_uq, g_ukv, g_out = [lax.dynamic_update_slice(g, s, (2 * chip, 0, 0)) for g, s in zip(gathered, shards)]
    cols = lambda g: g.reshape((4, 2) + g.shape[1:]).transpose(1, 2, 0, 3).reshape(2, g.shape[1], 4 * g.shape[2])
    full_out = g_out.reshape(4, 2, 256, 1024).transpose(1, 0, 2, 3).reshape(2, 1024, 1024)
    loss, dx, grads = _local_step(x[0], positions[0], loss_target[0], cols(g_in), cols(g_uq), cols(g_ukv), full_out, w)

    split_cols = lambda g: g.reshape(2, g.shape[1], 4, g.shape[2] // 4).transpose(2, 0, 1, 3).reshape(8, g.shape[1], g.shape[2] // 4)
    parts = [split_cols(grads["w_in"]), split_cols(grads["w_uq"]), split_cols(grads["w_ukv"]),
             grads["w_out"].reshape(2, 4, 256, 1024).transpose(1, 0, 2, 3).reshape(8, 256, 1024)]
    common = _pack_small(grads, loss)
    got = _pair_exchange(parts, common)
    pair_sums = [_pair_sum(parts[a], got[a], f"pair_sum_{BIG_NAMES[a]}", 512) for a in range(4)]
    chip_common = _sum_leading(jnp.stack([common, got[4]]), "pair_sum_small", common.shape[0])
    received = _chip_exchange([ps[1] for ps in pair_sums], chip_common)
    sums = [_sum_chips(pair_sums[a][0], received[a], f"sum_{BIG_NAMES[a]}", 512) for a in range(4)]
    all_common = lax.dynamic_update_slice(received[4], chip_common[None], (chip, 0, 0))
    small_sum, loss = _unpack_small(_sum_leading(all_common, "sum_small", all_common.shape[1]), w)
    others = _sibling_exchange(sums)
    total = dict(small_sum)
    for a, k in enumerate(BIG_NAMES):
        total[k] = jnp.where(core == 0, jnp.stack([sums[a], others[a]]), jnp.stack([others[a], sums[a]]))

    rows2d = lambda a: a.reshape(-1, a.shape[-1])
    small_out = _adamw_many(*[[rows2d(tree[k]) for k in SMALL_NAMES] for tree in (w, total, m, v)], "adamw_small")
    delta, new_m, new_v = ({k: r.reshape(w[k].shape) for k, r in zip(SMALL_NAMES, res)} for res in small_out)
    for k in BIG_NAMES:
        shape = w[k].shape
        flat = lambda a: a.reshape(shape[0] * shape[1], shape[2])
        res = _adamw(flat(w[k]), flat(total[k]), flat(m[k]), flat(v[k]), f"adamw_{k}", 256)
        delta[k], new_m[k], new_v[k] = (r.reshape(shape) for r in res)

    return (loss, dx[None], *[total[k] for k in WEIGHT_NAMES], *[delta[k] for k in WEIGHT_NAMES],
            *[new_m[k] for k in WEIGHT_NAMES], *[new_v[k] for k in WEIGHT_NAMES])
```
